```python
import jax, jax.numpy as jnp
from jax import lax
import numpy as np

D_MODEL = 1024
BATCH = 8
SEQ = 4096
DEPTH = 1

SSM_EXPAND = 2
SSM_D_INNER = SSM_EXPAND * D_MODEL
SSM_HEAD_DIM = 64
SSM_N_HEADS = SSM_D_INNER // SSM_HEAD_DIM
SSM_N_GROUPS = 8
SSM_D_STATE = 128
SSM_CHUNK = 128
SSM_CONV_DIM = SSM_D_INNER + 2 * SSM_N_GROUPS * SSM_D_STATE
GDN_HEAD_K = 128
GDN_HEAD_V = 128
GDN_N_QK_HEADS = D_MODEL // GDN_HEAD_K
GDN_N_V_HEADS = 2 * GDN_N_QK_HEADS
GDN_KEY_DIM = GDN_N_QK_HEADS * GDN_HEAD_K
GDN_VAL_DIM = GDN_N_V_HEADS * GDN_HEAD_V
GDN_CHUNK = 64
GDN_CONV_DIM = 2 * GDN_KEY_DIM + GDN_VAL_DIM
CONV_K = 4
MLP_HIDDEN = 4 * D_MODEL
EPS = 1e-6
IN_SPLIT_SIZES = (SSM_D_INNER, SSM_CONV_DIM, SSM_N_HEADS, GDN_CONV_DIM, GDN_VAL_DIM,
                  GDN_N_V_HEADS, GDN_N_V_HEADS, D_MODEL, D_MODEL)
IN_PROJ_DIM = sum(IN_SPLIT_SIZES)

kernel_name = "hybrid_ssd_gdn_sandwich_adaln_block"


def rmsnorm(x, w):
    xf = x.astype(jnp.float32)
    y = xf * lax.rsqrt(jnp.mean(xf * xf, axis=-1, keepdims=True) + EPS)
    return (y * w.astype(jnp.float32)).astype(x.dtype)


def l2norm(x):
    return x * lax.rsqrt(jnp.sum(x * x, axis=-1, keepdims=True) + EPS)


def causal_depthwise_conv(x, w):
    return lax.conv_general_dilated(
        x, w[:, None, :].astype(x.dtype), window_strides=(1,), padding=[(CONV_K - 1, 0)],
        dimension_numbers=('NWC', 'WIO', 'NWC'), feature_group_count=x.shape[-1])


def ssd_chunked_scan(xh, dt, A, Bm, Cm):
    Bsz, S, H, P = xh.shape
    G, N = Bm.shape[-2:]
    hg = H // G
    L = SSM_CHUNK
    nc = S // L
    xdt = jnp.moveaxis((xh * dt[..., None]).reshape(Bsz, nc, L, G, hg, P), 1, 0)
    a = jnp.moveaxis((dt * A).reshape(Bsz, nc, L, G, hg), 1, 0)
    Bc = jnp.moveaxis(Bm.reshape(Bsz, nc, L, G, N), 1, 0)
    Cc = jnp.moveaxis(Cm.reshape(Bsz, nc, L, G, N), 1, 0)
    causal = jnp.tril(jnp.ones((L, L), dtype=bool))[None, :, :, None, None]

    def step(state, inp):
        xc, ac, bc, cc = inp
        acum = jnp.cumsum(ac, axis=1)
        seg = acum[:, :, None] - acum[:, None, :]
        decay = jnp.exp(jnp.where(causal, seg, -jnp.inf))
        cb = jnp.einsum('blgn,bsgn->blsg', cc, bc)
        y_diag = jnp.einsum('blsg,blsgh,bsghp->blghp', cb, decay, xc)
        y_off = jnp.einsum('blgn,bghpn->blghp', cc, state) * jnp.exp(acum)[..., None]
        a_last = acum[:, -1]
        w_s = jnp.exp(a_last[:, None] - acum)
        new_state = state * jnp.exp(a_last)[..., None, None] + jnp.einsum(
            'bsgn,bsgh,bsghp->bghpn', bc, w_s, xc)
        return new_state, y_diag + y_off

    state0 = jnp.zeros((Bsz, G, hg, P, N), dtype=jnp.float32)
    _, y = lax.scan(step, state0, (xdt, a, Bc, Cc))
    return jnp.moveaxis(y, 0, 1).reshape(Bsz, S, H, P)


def gated_delta_rule_chunked(q, k, v, g, beta):
    Bsz, S, H, dk = q.shape
    dv = v.shape[-1]
    L = GDN_CHUNK
    nc = S // L
    q = q * (dk ** -0.5)

    def chunks(t):
        t = t.reshape((Bsz, nc, L, H) + t.shape[3:])
        return jnp.moveaxis(t, (1, 3), (0, 2))

    causal = jnp.tril(jnp.ones((L, L), dtype=bool))
    strict = jnp.tril(jnp.ones((L, L), dtype=bool), -1)
    eye = jnp.eye(L, dtype=jnp.float32)

    def step(state, inp):
        qc, kc, vc, gc, bc = inp
        gcum = jnp.cumsum(gc, axis=-1)
        dmat = jnp.exp(jnp.where(causal, gcum[..., :, None] - gcum[..., None, :], -jnp.inf))
        kb = kc * bc[..., None]
        a_low = jnp.where(strict, jnp.einsum('bhid,bhjd->bhij', kb, kc) * dmat, 0.0)
        rhs = jnp.concatenate([vc * bc[..., None], kb * jnp.exp(gcum)[..., None]], axis=-1)
        sol = lax.linalg.triangular_solve(eye + a_low, rhs, left_side=True, lower=True,
                                          unit_diagonal=True)
        u, w = sol[..., :dv], sol[..., dv:]
        attn = jnp.einsum('bhid,bhjd->bhij', qc, kc) * dmat
        v_new = u - jnp.einsum('bhlk,bhkv->bhlv', w, state)
        o = jnp.einsum('bhlk,bhkv->bhlv', qc * jnp.exp(gcum)[..., None], state) + jnp.einsum(
            'bhij,bhjv->bhiv', attn, v_new)
        g_last = gcum[..., -1]
        k_dec = kc * jnp.exp(g_last[..., None] - gcum)[..., None]
        new_state = state * jnp.exp(g_last)[..., None, None] + jnp.einsum(
            'bhlk,bhlv->bhkv', k_dec, v_new)
        return new_state, o

    state0 = jnp.zeros((Bsz, H, dk, dv), dtype=jnp.float32)
    _, o = lax.scan(step, state0, (chunks(q), chunks(k), chunks(v), chunks(g), chunks(beta)))
    return jnp.moveaxis(o, (0, 2), (1, 3)).reshape(Bsz, S, H, dv)


def mamba2_branch(z, xbc, dt_raw, conv_w, conv_b, dt_bias, A_log, d_skip, norm_w):
    f32 = jnp.float32
    Bsz, S, _ = xbc.shape
    xbc = jax.nn.silu(causal_depthwise_conv(xbc, conv_w) + conv_b)
    xs, Bm, Cm = jnp.split(xbc, [SSM_D_INNER, SSM_D_INNER + SSM_N_GROUPS * SSM_D_STATE], axis=-1)
    xh = xs.reshape(Bsz, S, SSM_N_HEADS, SSM_HEAD_DIM).astype(f32)
    dt = jax.nn.softplus(dt_raw.astype(f32) + dt_bias.astype(f32))
    A = -jnp.exp(A_log.astype(f32))
    y = ssd_chunked_scan(xh, dt, A,
                         Bm.reshape(Bsz, S, SSM_N_GROUPS, SSM_D_STATE).astype(f32),
                         Cm.reshape(Bsz, S, SSM_N_GROUPS, SSM_D_STATE).astype(f32))
    y = y + d_skip.astype(f32)[:, None] * xh
    y = y.reshape(Bsz, S, SSM_D_INNER) * jax.nn.silu(z.astype(f32))
    y = rmsnorm(y.reshape(Bsz, S, SSM_N_GROUPS, -1), norm_w.reshape(SSM_N_GROUPS, -1))
    return y.reshape(Bsz, S, SSM_D_INNER).astype(z.dtype)


def gated_deltanet_branch(qkv, z, b, a, conv_w, dt_bias, A_log, norm_w):
    f32 = jnp.float32
    Bsz, S, _ = qkv.shape
    qkv = jax.nn.silu(causal_depthwise_conv(qkv, conv_w))
    q, k, v = jnp.split(qkv, [GDN_KEY_DIM, 2 * GDN_KEY_DIM], axis=-1)
    rep = GDN_N_V_HEADS // GDN_N_QK_HEADS
    q = jnp.repeat(l2norm(q.reshape(Bsz, S, GDN_N_QK_HEADS, GDN_HEAD_K).astype(f32)), rep, axis=2)
    k = jnp.repeat(l2norm(k.reshape(Bsz, S, GDN_N_QK_HEADS, GDN_HEAD_K).astype(f32)), rep, axis=2)
    v = v.reshape(Bsz, S, GDN_N_V_HEADS, GDN_HEAD_V).astype(f32)
    beta = jax.nn.sigmoid(b.astype(f32))
    g = -jnp.exp(A_log.astype(f32)) * jax.nn.softplus(a.astype(f32) + dt_bias.astype(f32))
    o = gated_delta_rule_chunked(q, k, v, g, beta)
    o = rmsnorm(o, norm_w) * jax.nn.silu(z.reshape(Bsz, S, GDN_N_V_HEADS, GDN_HEAD_V).astype(f32))
    return o.reshape(Bsz, S, GDN_VAL_DIM).astype(z.dtype)


def _fwd_setup_inputs(seed: int = 0) -> dict:
    key = jax.random.key(seed)
    ks = jax.random.split(key, 26)
    f32 = jnp.float32
    nrm = lambda k, shape, scale: jax.random.normal(k, shape, f32) * scale
    gain = lambda k, shape: 1.0 + 0.02 * jax.random.normal(k, shape, f32)

    def inv_softplus_dt(k, shape):
        dt = jnp.exp(jax.random.uniform(k, shape, f32, np.log(1e-3), np.log(1e-1)))
        return dt + jnp.log(-jnp.expm1(-dt))

    Dm = D_MODEL
    return {
        'x': jax.random.normal(ks[0], (BATCH, SEQ, Dm), f32),
        'c': jax.random.normal(ks[1], (BATCH, Dm), f32),
        'w_ada': nrm(ks[2], (DEPTH, Dm, 6 * Dm), 0.5 * Dm ** -0.5),
        'b_ada': nrm(ks[3], (DEPTH, 6 * Dm), 0.02),
        'norm_mix_pre': gain(ks[4], (DEPTH, Dm)),
        'norm_mix_post': gain(ks[5], (DEPTH, Dm)),
        'w_in': nrm(ks[6], (DEPTH, Dm, IN_PROJ_DIM), Dm ** -0.5),
        'ssm_conv_w': nrm(ks[7], (DEPTH, CONV_K, SSM_CONV_DIM), CONV_K ** -0.5),
        'ssm_conv_b': nrm(ks[8], (DEPTH, SSM_CONV_DIM), 0.02),
        'ssm_dt_bias': inv_softplus_dt(ks[9], (DEPTH, SSM_N_HEADS)),
        'ssm_A_log': jnp.log(jax.random.uniform(ks[10], (DEPTH, SSM_N_HEADS), f32, 1.0, 16.0)),
        'ssm_D': gain(ks[11], (DEPTH, SSM_N_HEADS)),
        'ssm_norm_w': gain(ks[12], (DEPTH, SSM_D_INNER)),
        'gdn_conv_w': nrm(ks[13], (DEPTH, CONV_K, GDN_CONV_DIM), CONV_K ** -0.5),
        'gdn_dt_bias': inv_softplus_dt(ks[14], (DEPTH, GDN_N_V_HEADS)),
        'gdn_A_log': jnp.log(jax.random.uniform(ks[15], (DEPTH, GDN_N_V_HEADS), f32, 1.0, 16.0)),
        'gdn_norm_w': gain(ks[16], (DEPTH, GDN_HEAD_V)),
        'w_ssm_up': nrm(ks[17], (DEPTH, SSM_D_INNER, Dm), SSM_D_INNER ** -0.5),
        'w_gdn_up': nrm(ks[18], (DEPTH, GDN_VAL_DIM, Dm), GDN_VAL_DIM ** -0.5),
        'w_out': nrm(ks[19], (DEPTH, Dm, Dm), Dm ** -0.5),
        'norm_mlp_pre': gain(ks[20], (DEPTH, Dm)),
        'norm_mlp_post': gain(ks[21], (DEPTH, Dm)),
        'w_mlp_up': nrm(ks[22], (DEPTH, Dm, MLP_HIDDEN), Dm ** -0.5),
        'w_mlp_down': nrm(ks[23], (DEPTH, MLP_HIDDEN, Dm), MLP_HIDDEN ** -0.5),
    }


def _fwd_reference(x, c, w_ada, b_ada, norm_mix_pre, norm_mix_post, w_in, ssm_conv_w, ssm_conv_b,
              ssm_dt_bias, ssm_A_log, ssm_D, ssm_norm_w, gdn_conv_w, gdn_dt_bias, gdn_A_log,
              gdn_norm_w, w_ssm_up, w_gdn_up, w_out, norm_mlp_pre, norm_mlp_post, w_mlp_up,
              w_mlp_down):
    offsets = [int(o) for o in np.cumsum(IN_SPLIT_SIZES)[:-1]]
    c_act = jax.nn.silu(c)
    for l in range(DEPTH):
        mod = c_act @ w_ada[l] + b_ada[l]
        sh1, sc1, g1, sh2, sc2, g2 = [m[:, None, :] for m in jnp.split(mod, 6, axis=-1)]

        h = rmsnorm(x, norm_mix_pre[l]) * (1.0 + sc1) + sh1
        proj = h @ w_in[l]
        (z_ssm, xbc, dt_raw, qkv, z_gdn, b_gdn, a_gdn,
         gate_ssm, gate_gdn) = jnp.split(proj, offsets, axis=-1)
        y_ssm = mamba2_branch(z_ssm, xbc, dt_raw, ssm_conv_w[l], ssm_conv_b[l], ssm_dt_bias[l],
                              ssm_A_log[l], ssm_D[l], ssm_norm_w[l]) @ w_ssm_up[l]
        y_gdn = gated_deltanet_branch(qkv, z_gdn, b_gdn, a_gdn, gdn_conv_w[l], gdn_dt_bias[l],
                                      gdn_A_log[l], gdn_norm_w[l]) @ w_gdn_up[l]
        merged = jax.nn.sigmoid(gate_ssm) * y_ssm + jax.nn.sigmoid(gate_gdn) * y_gdn
        x = x + g1 * rmsnorm(merged @ w_out[l], norm_mix_post[l])

        h = rmsnorm(x, norm_mlp_pre[l]) * (1.0 + sc2) + sh2
        y = jnp.square(jax.nn.relu(h @ w_mlp_up[l])) @ w_mlp_down[l]
        x = x + g2 * rmsnorm(y, norm_mlp_post[l])
    return x


import jax as _jax
import jax.numpy as _jnp

TWIN_FORMAT = 'train_step'
FWD_PARAMS = ['x', 'c', 'w_ada', 'b_ada', 'norm_mix_pre', 'norm_mix_post', 'w_in', 'ssm_conv_w', 'ssm_conv_b', 'ssm_dt_bias', 'ssm_A_log', 'ssm_D', 'ssm_norm_w', 'gdn_conv_w', 'gdn_dt_bias', 'gdn_A_log', 'gdn_norm_w', 'w_ssm_up', 'w_gdn_up', 'w_out', 'norm_mlp_pre', 'norm_mlp_post', 'w_mlp_up', 'w_mlp_down']
TWIN_WEIGHTS = ['w_ada', 'b_ada', 'norm_mix_pre', 'norm_mix_post', 'w_in', 'ssm_conv_w', 'ssm_conv_b', 'ssm_dt_bias', 'ssm_A_log', 'ssm_D', 'ssm_norm_w', 'gdn_conv_w', 'gdn_dt_bias', 'gdn_A_log', 'gdn_norm_w', 'w_ssm_up', 'w_gdn_up', 'w_out', 'norm_mlp_pre', 'norm_mlp_post', 'w_mlp_up', 'w_mlp_down']
TWIN_DIFF_INPUT = 'x'
TWIN_INPUTS = ['x', 'c', 'w_ada', 'b_ada', 'norm_mix_pre', 'norm_mix_post', 'w_in', 'ssm_conv_w', 'ssm_conv_b', 'ssm_dt_bias', 'ssm_A_log', 'ssm_D', 'ssm_norm_w', 'gdn_conv_w', 'gdn_dt_bias', 'gdn_A_log', 'gdn_norm_w', 'w_ssm_up', 'w_gdn_up', 'w_out', 'norm_mlp_pre', 'norm_mlp_post', 'w_mlp_up', 'w_mlp_down', 'loss_target', 'm_w_ada', 'm_b_ada', 'm_norm_mix_pre', 'm_norm_mix_post', 'm_w_in', 'm_ssm_conv_w', 'm_ssm_conv_b', 'm_ssm_dt_bias', 'm_ssm_A_log', 'm_ssm_D', 'm_ssm_norm_w', 'm_gdn_conv_w', 'm_gdn_dt_bias', 'm_gdn_A_log', 'm_gdn_norm_w', 'm_w_ssm_up', 'm_w_gdn_up', 'm_w_out', 'm_norm_mlp_pre', 'm_norm_mlp_post', 'm_w_mlp_up', 'm_w_mlp_down', 'v_w_ada', 'v_b_ada', 'v_norm_mix_pre', 'v_norm_mix_post', 'v_w_in', 'v_ssm_conv_w', 'v_ssm_conv_b', 'v_ssm_dt_bias', 'v_ssm_A_log', 'v_ssm_D', 'v_ssm_norm_w', 'v_gdn_conv_w', 'v_gdn_dt_bias', 'v_gdn_A_log', 'v_gdn_norm_w', 'v_w_ssm_up', 'v_w_gdn_up', 'v_w_out', 'v_norm_mlp_pre', 'v_norm_mlp_post', 'v_w_mlp_up', 'v_w_mlp_down']
TWIN_OUTPUTS = ['loss', 'grad_x', 'grad_w_ada', 'grad_b_ada', 'grad_norm_mix_pre', 'grad_norm_mix_post', 'grad_w_in', 'grad_ssm_conv_w', 'grad_ssm_conv_b', 'grad_ssm_dt_bias', 'grad_ssm_A_log', 'grad_ssm_D', 'grad_ssm_norm_w', 'grad_gdn_conv_w', 'grad_gdn_dt_bias', 'grad_gdn_A_log', 'grad_gdn_norm_w', 'grad_w_ssm_up', 'grad_w_gdn_up', 'grad_w_out', 'grad_norm_mlp_pre', 'grad_norm_mlp_post', 'grad_w_mlp_up', 'grad_w_mlp_down', 'delta_w_ada', 'delta_b_ada', 'delta_norm_mix_pre', 'delta_norm_mix_post', 'delta_w_in', 'delta_ssm_conv_w', 'delta_ssm_conv_b', 'delta_ssm_dt_bias', 'delta_ssm_A_log', 'delta_ssm_D', 'delta_ssm_norm_w', 'delta_gdn_conv_w', 'delta_gdn_dt_bias', 'delta_gdn_A_log', 'delta_gdn_norm_w', 'delta_w_ssm_up', 'delta_w_gdn_up', 'delta_w_out', 'delta_norm_mlp_pre', 'delta_norm_mlp_post', 'delta_w_mlp_up', 'delta_w_mlp_down', 'new_m_w_ada', 'new_m_b_ada', 'new_m_norm_mix_pre', 'new_m_norm_mix_post', 'new_m_w_in', 'new_m_ssm_conv_w', 'new_m_ssm_conv_b', 'new_m_ssm_dt_bias', 'new_m_ssm_A_log', 'new_m_ssm_D', 'new_m_ssm_norm_w', 'new_m_gdn_conv_w', 'new_m_gdn_dt_bias', 'new_m_gdn_A_log', 'new_m_gdn_norm_w', 'new_m_w_ssm_up', 'new_m_w_gdn_up', 'new_m_w_out', 'new_m_norm_mlp_pre', 'new_m_norm_mlp_post', 'new_m_w_mlp_up', 'new_m_w_mlp_down', 'new_v_w_ada', 'new_v_b_ada', 'new_v_norm_mix_pre', 'new_v_norm_mix_post', 'new_v_w_in', 'new_v_ssm_conv_w', 'new_v_ssm_conv_b', 'new_v_ssm_dt_bias', 'new_v_ssm_A_log', 'new_v_ssm_D', 'new_v_ssm_norm_w', 'new_v_gdn_conv_w', 'new_v_gdn_dt_bias', 'new_v_gdn_A_log', 'new_v_gdn_norm_w', 'new_v_w_ssm_up', 'new_v_w_gdn_up', 'new_v_w_out', 'new_v_norm_mlp_pre', 'new_v_norm_mlp_post', 'new_v_w_mlp_up', 'new_v_w_mlp_down']
TWIN_LEAF_KINDS = {'loss': 'loss', 'grad_x': 'grad_x', 'grad_w_ada': 'grad_w', 'grad_b_ada': 'grad_w', 'grad_norm_mix_pre': 'grad_w', 'grad_norm_mix_post': 'grad_w', 'grad_w_in': 'grad_w', 'grad_ssm_conv_w': 'grad_w', 'grad_ssm_conv_b': 'grad_w', 'grad_ssm_dt_bias': 'grad_w', 'grad_ssm_A_log': 'grad_w', 'grad_ssm_D': 'grad_w', 'grad_ssm_norm_w': 'grad_w', 'grad_gdn_conv_w': 'grad_w', 'grad_gdn_dt_bias': 'grad_w', 'grad_gdn_A_log': 'grad_w', 'grad_gdn_norm_w': 'grad_w', 'grad_w_ssm_up': 'grad_w', 'grad_w_gdn_up': 'grad_w', 'grad_w_out': 'grad_w', 'grad_norm_mlp_pre': 'grad_w', 'grad_norm_mlp_post': 'grad_w', 'grad_w_mlp_up': 'grad_w', 'grad_w_mlp_down': 'grad_w', 'delta_w_ada': 'delta_w', 'delta_b_ada': 'delta_w', 'delta_norm_mix_pre': 'delta_w', 'delta_norm_mix_post': 'delta_w', 'delta_w_in': 'delta_w', 'delta_ssm_conv_w': 'delta_w', 'delta_ssm_conv_b': 'delta_w', 'delta_ssm_dt_bias': 'delta_w', 'delta_ssm_A_log': 'delta_w', 'delta_ssm_D': 'delta_w', 'delta_ssm_norm_w': 'delta_w', 'delta_gdn_conv_w': 'delta_w', 'delta_gdn_dt_bias': 'delta_w', 'delta_gdn_A_log': 'delta_w', 'delta_gdn_norm_w': 'delta_w', 'delta_w_ssm_up': 'delta_w', 'delta_w_gdn_up': 'delta_w', 'delta_w_out': 'delta_w', 'delta_norm_mlp_pre': 'delta_w', 'delta_norm_mlp_post': 'delta_w', 'delta_w_mlp_up': 'delta_w', 'delta_w_mlp_down': 'delta_w', 'new_m_w_ada': 'new_m', 'new_m_b_ada': 'new_m', 'new_m_norm_mix_pre': 'new_m', 'new_m_norm_mix_post': 'new_m', 'new_m_w_in': 'new_m', 'new_m_ssm_conv_w': 'new_m', 'new_m_ssm_conv_b': 'new_m', 'new_m_ssm_dt_bias': 'new_m', 'new_m_ssm_A_log': 'new_m', 'new_m_ssm_D': 'new_m', 'new_m_ssm_norm_w': 'new_m', 'new_m_gdn_conv_w': 'new_m', 'new_m_gdn_dt_bias': 'new_m', 'new_m_gdn_A_log': 'new_m', 'new_m_gdn_norm_w': 'new_m', 'new_m_w_ssm_up': 'new_m', 'new_m_w_gdn_up': 'new_m', 'new_m_w_out': 'new_m', 'new_m_norm_mlp_pre': 'new_m', 'new_m_norm_mlp_post': 'new_m', 'new_m_w_mlp_up': 'new_m', 'new_m_w_mlp_down': 'new_m', 'new_v_w_ada': 'new_v', 'new_v_b_ada': 'new_v', 'new_v_norm_mix_pre': 'new_v', 'new_v_norm_mix_post': 'new_v', 'new_v_w_in': 'new_v', 'new_v_ssm_conv_w': 'new_v', 'new_v_ssm_conv_b': 'new_v', 'new_v_ssm_dt_bias': 'new_v', 'new_v_ssm_A_log': 'new_v', 'new_v_ssm_D': 'new_v', 'new_v_ssm_norm_w': 'new_v', 'new_v_gdn_conv_w': 'new_v', 'new_v_gdn_dt_bias': 'new_v', 'new_v_gdn_A_log': 'new_v', 'new_v_gdn_norm_w': 'new_v', 'new_v_w_ssm_up': 'new_v', 'new_v_w_gdn_up': 'new_v', 'new_v_w_out': 'new_v', 'new_v_norm_mlp_pre': 'new_v', 'new_v_norm_mlp_post': 'new_v', 'new_v_w_mlp_up': 'new_v', 'new_v_w_mlp_down': 'new_v'}


def _forward(args):
    return _fwd_reference(*[args[k] for k in FWD_PARAMS])


def _output_shape():
    out = _jax.eval_shape(lambda: _forward(_fwd_setup_inputs(0)))
    return out.shape, out.dtype

N_MICROBATCH = 1
ADAM_LR = 0.001
ADAM_B1 = 0.9
ADAM_B2 = 0.999
ADAM_EPS = 1e-08
ADAM_WD = 0.01
ADAM_STEP = 10
PER_EXAMPLE_BATCH_AXIS = {'x': 0, 'c': 0, 'loss_target': 0}
SHARED_INPUTS = []
_WEIGHT_DTYPES = {'w_ada': _jnp.float32, 'b_ada': _jnp.float32, 'norm_mix_pre': _jnp.float32, 'norm_mix_post': _jnp.float32, 'w_in': _jnp.float32, 'ssm_conv_w': _jnp.float32, 'ssm_conv_b': _jnp.float32, 'ssm_dt_bias': _jnp.float32, 'ssm_A_log': _jnp.float32, 'ssm_D': _jnp.float32, 'ssm_norm_w': _jnp.float32, 'gdn_conv_w': _jnp.float32, 'gdn_dt_bias': _jnp.float32, 'gdn_A_log': _jnp.float32, 'gdn_norm_w': _jnp.float32, 'w_ssm_up': _jnp.float32, 'w_gdn_up': _jnp.float32, 'w_out': _jnp.float32, 'norm_mlp_pre': _jnp.float32, 'norm_mlp_post': _jnp.float32, 'w_mlp_up': _jnp.float32, 'w_mlp_down': _jnp.float32}
MOMENT_SCALE = {'w_ada': 1.817565e+00, 'b_ada': 3.462583e+00, 'norm_mix_pre': 1.377878e-01, 'norm_mix_post': 3.803124e+00, 'w_in': 5.066637e-02, 'ssm_conv_w': 6.874790e-02, 'ssm_conv_b': 1.579658e-01, 'ssm_dt_bias': 1.239996e-01, 'ssm_A_log': 7.210734e-01, 'ssm_D': 2.837529e-01, 'ssm_norm_w': 1.216222e-01, 'gdn_conv_w': 4.670566e-02, 'gdn_dt_bias': 9.436967e-02, 'gdn_A_log': 9.660792e-02, 'gdn_norm_w': 3.470824e-01, 'w_ssm_up': 1.762979e-01, 'w_gdn_up': 1.127339e-01, 'w_out': 2.146557e-01, 'norm_mlp_pre': 1.158774e-01, 'norm_mlp_post': 3.936973e+00, 'w_mlp_up': 9.650828e-02, 'w_mlp_down': 4.491692e-01}


def _to_microbatches(a, axis):
    t = _jnp.moveaxis(a, axis, 0)
    t = t.reshape((N_MICROBATCH, t.shape[0] // N_MICROBATCH) + t.shape[1:])
    return _jnp.moveaxis(t, 1, axis + 1)


def setup_inputs(seed: int = 0) -> dict:
    inp = _fwd_setup_inputs(seed)
    key = _jax.random.fold_in(_jax.random.key(seed), 7919)
    shape, _ = _output_shape()
    out = dict(inp)
    out["loss_target"] = _jax.random.normal(_jax.random.fold_in(key, 0), shape, _jnp.float32)
    for i, name in enumerate(TWIN_WEIGHTS):
        w = inp[name].astype(_jnp.float32)
        if MOMENT_SCALE is None:
            s = _jnp.sqrt(_jnp.mean(_jnp.square(w)) + 1e-30)
        else:
            s = MOMENT_SCALE[name]
        km, kv = _jax.random.split(_jax.random.fold_in(key, i + 1))
        out[name] = w
        out["m_" + name] = s * _jax.random.normal(km, w.shape, _jnp.float32)
        out["v_" + name] = (s * s) * _jax.random.uniform(kv, w.shape, _jnp.float32, 0.5, 1.5)
    if N_MICROBATCH > 1:
        for name, axis in PER_EXAMPLE_BATCH_AXIS.items():
            out[name] = _to_microbatches(out[name], axis)
    return {'x': out['x'], 'c': out['c'], 'w_ada': out['w_ada'], 'b_ada': out['b_ada'], 'norm_mix_pre': out['norm_mix_pre'], 'norm_mix_post': out['norm_mix_post'], 'w_in': out['w_in'], 'ssm_conv_w': out['ssm_conv_w'], 'ssm_conv_b': out['ssm_conv_b'], 'ssm_dt_bias': out['ssm_dt_bias'], 'ssm_A_log': out['ssm_A_log'], 'ssm_D': out['ssm_D'], 'ssm_norm_w': out['ssm_norm_w'], 'gdn_conv_w': out['gdn_conv_w'], 'gdn_dt_bias': out['gdn_dt_bias'], 'gdn_A_log': out['gdn_A_log'], 'gdn_norm_w': out['gdn_norm_w'], 'w_ssm_up': out['w_ssm_up'], 'w_gdn_up': out['w_gdn_up'], 'w_out': out['w_out'], 'norm_mlp_pre': out['norm_mlp_pre'], 'norm_mlp_post': out['norm_mlp_post'], 'w_mlp_up': out['w_mlp_up'], 'w_mlp_down': out['w_mlp_down'], 'loss_target': out['loss_target'], 'm_w_ada': out['m_w_ada'], 'm_b_ada': out['m_b_ada'], 'm_norm_mix_pre': out['m_norm_mix_pre'], 'm_norm_mix_post': out['m_norm_mix_post'], 'm_w_in': out['m_w_in'], 'm_ssm_conv_w': out['m_ssm_conv_w'], 'm_ssm_conv_b': out['m_ssm_conv_b'], 'm_ssm_dt_bias': out['m_ssm_dt_bias'], 'm_ssm_A_log': out['m_ssm_A_log'], 'm_ssm_D': out['m_ssm_D'], 'm_ssm_norm_w': out['m_ssm_norm_w'], 'm_gdn_conv_w': out['m_gdn_conv_w'], 'm_gdn_dt_bias': out['m_gdn_dt_bias'], 'm_gdn_A_log': out['m_gdn_A_log'], 'm_gdn_norm_w': out['m_gdn_norm_w'], 'm_w_ssm_up': out['m_w_ssm_up'], 'm_w_gdn_up': out['m_w_gdn_up'], 'm_w_out': out['m_w_out'], 'm_norm_mlp_pre': out['m_norm_mlp_pre'], 'm_norm_mlp_post': out['m_norm_mlp_post'], 'm_w_mlp_up': out['m_w_mlp_up'], 'm_w_mlp_down': out['m_w_mlp_down'], 'v_w_ada': out['v_w_ada'], 'v_b_ada': out['v_b_ada'], 'v_norm_mix_pre': out['v_norm_mix_pre'], 'v_norm_mix_post': out['v_norm_mix_post'], 'v_w_in': out['v_w_in'], 'v_ssm_conv_w': out['v_ssm_conv_w'], 'v_ssm_conv_b': out['v_ssm_conv_b'], 'v_ssm_dt_bias': out['v_ssm_dt_bias'], 'v_ssm_A_log': out['v_ssm_A_log'], 'v_ssm_D': out['v_ssm_D'], 'v_ssm_norm_w': out['v_ssm_norm_w'], 'v_gdn_conv_w': out['v_gdn_conv_w'], 'v_gdn_dt_bias': out['v_gdn_dt_bias'], 'v_gdn_A_log': out['v_gdn_A_log'], 'v_gdn_norm_w': out['v_gdn_norm_w'], 'v_w_ssm_up': out['v_w_ssm_up'], 'v_w_gdn_up': out['v_w_gdn_up'], 'v_w_out': out['v_w_out'], 'v_norm_mlp_pre': out['v_norm_mlp_pre'], 'v_norm_mlp_post': out['v_norm_mlp_post'], 'v_w_mlp_up': out['v_w_mlp_up'], 'v_w_mlp_down': out['v_w_mlp_down']}


def _loss(weights, diff, rest, loss_target):
    with _jax.named_scope("forward"):
        args = {**rest, TWIN_DIFF_INPUT: diff, **{k: w.astype(_WEIGHT_DTYPES[k]) for k, w in weights.items()}}
        y = _forward(args)
    with _jax.named_scope("loss_head"):
        err = _jnp.square(y.astype(_jnp.float32) - loss_target)
        return 0.5 * _jnp.sum(_jnp.mean(err, axis=-1)) if err.ndim else 0.5 * err


def _adamw(w, g, m, v):
    m = ADAM_B1 * m + (1.0 - ADAM_B1) * g
    v = ADAM_B2 * v + (1.0 - ADAM_B2) * _jnp.square(g)
    m_hat = m / (1.0 - ADAM_B1 ** ADAM_STEP)
    v_hat = v / (1.0 - ADAM_B2 ** ADAM_STEP)
    delta = -ADAM_LR * (m_hat / (_jnp.sqrt(v_hat) + ADAM_EPS) + ADAM_WD * w)
    return delta, m, v


def reference(x, c, w_ada, b_ada, norm_mix_pre, norm_mix_post, w_in, ssm_conv_w, ssm_conv_b, ssm_dt_bias, ssm_A_log, ssm_D, ssm_norm_w, gdn_conv_w, gdn_dt_bias, gdn_A_log, gdn_norm_w, w_ssm_up, w_gdn_up, w_out, norm_mlp_pre, norm_mlp_post, w_mlp_up, w_mlp_down, loss_target, m_w_ada, m_b_ada, m_norm_mix_pre, m_norm_mix_post, m_w_in, m_ssm_conv_w, m_ssm_conv_b, m_ssm_dt_bias, m_ssm_A_log, m_ssm_D, m_ssm_norm_w, m_gdn_conv_w, m_gdn_dt_bias, m_gdn_A_log, m_gdn_norm_w, m_w_ssm_up, m_w_gdn_up, m_w_out, m_norm_mlp_pre, m_norm_mlp_post, m_w_mlp_up, m_w_mlp_down, v_w_ada, v_b_ada, v_norm_mix_pre, v_norm_mix_post, v_w_in, v_ssm_conv_w, v_ssm_conv_b, v_ssm_dt_bias, v_ssm_A_log, v_ssm_D, v_ssm_norm_w, v_gdn_conv_w, v_gdn_dt_bias, v_gdn_A_log, v_gdn_norm_w, v_w_ssm_up, v_w_gdn_up, v_w_out, v_norm_mlp_pre, v_norm_mlp_post, v_w_mlp_up, v_w_mlp_down):
    given = dict(x=x, c=c, w_ada=w_ada, b_ada=b_ada, norm_mix_pre=norm_mix_pre, norm_mix_post=norm_mix_post, w_in=w_in, ssm_conv_w=ssm_conv_w, ssm_conv_b=ssm_conv_b, ssm_dt_bias=ssm_dt_bias, ssm_A_log=ssm_A_log, ssm_D=ssm_D, ssm_norm_w=ssm_norm_w, gdn_conv_w=gdn_conv_w, gdn_dt_bias=gdn_dt_bias, gdn_A_log=gdn_A_log, gdn_norm_w=gdn_norm_w, w_ssm_up=w_ssm_up, w_gdn_up=w_gdn_up, w_out=w_out, norm_mlp_pre=norm_mlp_pre, norm_mlp_post=norm_mlp_post, w_mlp_up=w_mlp_up, w_mlp_down=w_mlp_down, loss_target=loss_target, m_w_ada=m_w_ada, m_b_ada=m_b_ada, m_norm_mix_pre=m_norm_mix_pre, m_norm_mix_post=m_norm_mix_post, m_w_in=m_w_in, m_ssm_conv_w=m_ssm_conv_w, m_ssm_conv_b=m_ssm_conv_b, m_ssm_dt_bias=m_ssm_dt_bias, m_ssm_A_log=m_ssm_A_log, m_ssm_D=m_ssm_D, m_ssm_norm_w=m_ssm_norm_w, m_gdn_conv_w=m_gdn_conv_w, m_gdn_dt_bias=m_gdn_dt_bias, m_gdn_A_log=m_gdn_A_log, m_gdn_norm_w=m_gdn_norm_w, m_w_ssm_up=m_w_ssm_up, m_w_gdn_up=m_w_gdn_up, m_w_out=m_w_out, m_norm_mlp_pre=m_norm_mlp_pre, m_norm_mlp_post=m_norm_mlp_post, m_w_mlp_up=m_w_mlp_up, m_w_mlp_down=m_w_mlp_down, v_w_ada=v_w_ada, v_b_ada=v_b_ada, v_norm_mix_pre=v_norm_mix_pre, v_norm_mix_post=v_norm_mix_post, v_w_in=v_w_in, v_ssm_conv_w=v_ssm_conv_w, v_ssm_conv_b=v_ssm_conv_b, v_ssm_dt_bias=v_ssm_dt_bias, v_ssm_A_log=v_ssm_A_log, v_ssm_D=v_ssm_D, v_ssm_norm_w=v_ssm_norm_w, v_gdn_conv_w=v_gdn_conv_w, v_gdn_dt_bias=v_gdn_dt_bias, v_gdn_A_log=v_gdn_A_log, v_gdn_norm_w=v_gdn_norm_w, v_w_ssm_up=v_w_ssm_up, v_w_gdn_up=v_w_gdn_up, v_w_out=v_w_out, v_norm_mlp_pre=v_norm_mlp_pre, v_norm_mlp_post=v_norm_mlp_post, v_w_mlp_up=v_w_mlp_up, v_w_mlp_down=v_w_mlp_down)
    weights = {n: given[n] for n in TWIN_WEIGHTS}
    shared = {n: given[n] for n in SHARED_INPUTS}
    per_example = {n: given[n] for n in ['x', 'c']}
    grad_fn = _jax.value_and_grad(_loss, argnums=(0, 1))

    def one_microbatch(ex, loss_target):
        ex = dict(ex)
        diff = ex.pop(TWIN_DIFF_INPUT)
        return grad_fn(weights, diff, {**shared, **ex}, loss_target)

    if N_MICROBATCH == 1:
        loss, (grad_w, grad_x) = one_microbatch(per_example, given["loss_target"])
    else:
        def body(carry, xs):
            loss_sum, grad_sum = carry
            l_k, (gw_k, gx_k) = one_microbatch(xs[0], xs[1])
            with _jax.named_scope("update"):
                return (loss_sum + l_k, _jax.tree.map(_jnp.add, grad_sum, gw_k)), gx_k

        init = (_jnp.zeros((), _jnp.float32), _jax.tree.map(_jnp.zeros_like, weights))
        (loss, grad_w), grad_x = _jax.lax.scan(body, init, (per_example, given["loss_target"]))
    with _jax.named_scope("update"):
        delta_w, new_m, new_v = {}, {}, {}
        for n in TWIN_WEIGHTS:
            delta_w[n], new_m[n], new_v[n] = _adamw(weights[n], grad_w[n], given["m_" + n], given["v_" + n])
    return (loss, grad_x, *[grad_w[n] for n in TWIN_WEIGHTS], *[delta_w[n] for n in TWIN_WEIGHTS],
            *[new_m[n] for n in TWIN_WEIGHTS], *[new_v[n] for n in TWIN_WEIGHTS])
```

```python
import functools

import jax
import jax.numpy as jnp
from jax import lax
from jax.experimental import pallas as pl
from jax.experimental.pallas import tpu as pltpu

F32 = jnp.float32
BF16 = jnp.bfloat16
MESH = pl.DeviceIdType.MESH

EPS = 1e-6
SSM_HEAD_DIM = 64
SSM_HEADS_PER_GROUP = 4
SSM_D_STATE = 128
SSM_CHUNK = 128
GDN_HEAD = 128
GDN_CHUNK = 64
CONV_K = 4
LANE_DT, LANE_B, LANE_A = 0, 32, 48
ADAM_LR, ADAM_B1, ADAM_B2, ADAM_EPS, ADAM_WD, ADAM_STEP = 0.001, 0.9, 0.999, 1e-08, 0.01, 10

VMEM_LIMIT_BYTES = 56 * 1024 * 1024
LANES = 128
N_DEV = 8

NN = (((1,), (0,)), ((), ()))
NT = (((1,), (1,)), ((), ()))
TN = (((0,), (0,)), ((), ()))


def _bdot(a, b, dims=NN):
    return lax.dot_general(a.astype(BF16), b.astype(BF16), dims, preferred_element_type=F32)


def _hdot(a, b, dims=NN):
    return lax.dot_general(a, b, dims, precision=lax.Precision.HIGHEST, preferred_element_type=F32)


def _sigmoid(x):
    return 1.0 / (1.0 + jnp.exp(-x))


def _silu(x):
    return x * _sigmoid(x)


def _softplus(x):
    return jnp.maximum(x, 0.0) + jnp.log(1.0 + jnp.exp(-jnp.abs(x)))


def _rms(x, w):
    return x * lax.rsqrt(jnp.mean(x * x, axis=-1, keepdims=True) + EPS) * w


def _lane_col(m, idx):
    lane = lax.broadcasted_iota(jnp.int32, m.shape, 1)
    return jnp.sum(jnp.where(lane == idx, m, 0.0), axis=1, keepdims=True)


def _row_form(col):
    n = col.shape[0]
    wide = jnp.broadcast_to(col, (n, LANES))
    onehot = (lax.broadcasted_iota(jnp.int32, (n, LANES), 1) == 0).astype(F32)
    return _hdot(onehot, wide, NT)


def _tril(n, strict=False):
    r = lax.broadcasted_iota(jnp.int32, (n, n), 0)
    c = lax.broadcasted_iota(jnp.int32, (n, n), 1)
    return (r > c) if strict else (r >= c)


def _params(sem):
    return pltpu.CompilerParams(dimension_semantics=sem, vmem_limit_bytes=VMEM_LIMIT_BYTES)


def blockmap(name, fn, grid, ins, outs, accs=(), scalars=None):
    n_in, n_out, n_acc = len(ins), len(outs), len(accs)
    n_grid = len(grid)
    n_pre = 0 if scalars is None else 1

    def body(*refs):
        refs = refs[n_pre:]
        vals = fn(*[r[...] for r in refs[:n_in]])
        if not isinstance(vals, (tuple, list)):
            vals = (vals,)
        for r, v in zip(refs[n_in:n_in + n_out], vals[:n_out]):
            r[...] = v.astype(r.dtype)
        if n_acc:
            first = functools.reduce(jnp.logical_and, [pl.program_id(a) == 0 for a in range(n_grid)])
            acc_refs = refs[n_in + n_out:]

            @pl.when(first)
            def _():
                for r in acc_refs:
                    r[...] = jnp.zeros(r.shape, r.dtype)

            for r, v in zip(acc_refs, vals[n_out:]):
                r[...] += v.astype(r.dtype)

    zeros = lambda nd: (lambda *_: (0,) * nd)
    in_specs = [pl.BlockSpec(b, im) for _, b, im in ins]
    out_specs = [pl.BlockSpec(b, im) for _, _, b, im in outs] + [pl.BlockSpec(s, zeros(len(s))) for s, _ in accs]
    out_shape = [jax.ShapeDtypeStruct(s, d) for s, d, _, _ in outs] + [jax.ShapeDtypeStruct(s, d) for s, d in accs]
    cparams = _params(("arbitrary",) * n_grid if n_acc else ("parallel",) * n_grid)
    arrays = [a for a, _, _ in ins]
    if scalars is None:
        return pl.pallas_call(body, name=name, grid=grid, in_specs=in_specs, out_specs=out_specs, out_shape=out_shape,
                              compiler_params=cparams)(*arrays)
    spec = pltpu.PrefetchScalarGridSpec(num_scalar_prefetch=1, grid=grid, in_specs=in_specs, out_specs=out_specs)
    return pl.pallas_call(body, name=name, grid_spec=spec, out_shape=out_shape, compiler_params=cparams)(scalars, *arrays)


def rowmap(name, fn, rows, consts, outs, accs=(), rb=256):
    norm = [(r, r.shape[1], 0) if not isinstance(r, tuple) else (r[0], r[1], r[2] // r[1]) for r in rows]
    assert all(not isinstance(r, tuple) or r[2] % r[1] == 0 for r in rows)
    t = norm[0][0].shape[0]
    rb = min(rb, t)
    ins = [(a, (rb, n), (lambda i, cb=cb: (i, cb))) for a, n, cb in norm]
    ins += [(cst, cst.shape, (lambda i, nd=cst.ndim: (0,) * nd)) for cst in consts]
    o = [((t, n), d, (rb, n), lambda i: (i, 0)) for n, d in outs]
    return blockmap(name, fn, (t // rb,), ins, o, accs)


def matmul(name, a, b, ta=False, tb=False, out_dtypes=(F32,), epi=None, extras=(), a_cols=None, tm=512, tn=512, tk=512):
    if a_cols is not None:
        assert not ta
        k_dim, a_off = a_cols
        m_dim = a.shape[0]
    else:
        (k_dim, m_dim) = a.shape if ta else a.shape[::-1]
        a_off = 0
    n_dim = b.shape[0] if tb else b.shape[1]
    assert (b.shape[1] if tb else b.shape[0]) == k_dim, (name, a.shape, b.shape)
    tm, tn, tk = min(tm, m_dim), min(tn, n_dim), min(tk, k_dim)
    assert m_dim % tm == 0 and n_dim % tn == 0 and k_dim % tk == 0, (name, m_dim, n_dim, k_dim)
    k_steps = k_dim // tk
    n_extra, n_out = len(extras), len(out_dtypes)
    a_dims = 0 if ta else 1
    b_dims = 1 if tb else 0
    dims = (((a_dims,), (b_dims,)), ((), ()))

    def body(*refs):
        a_ref, b_ref = refs[0], refs[1]
        extra_refs = refs[2:2 + n_extra]
        out_refs = refs[2 + n_extra:2 + n_extra + n_out]
        acc_ref = refs[-1]
        k = pl.program_id(2)

        @pl.when(k == 0)
        def _():
            acc_ref[...] = jnp.zeros(acc_ref.shape, F32)

        acc_ref[...] += lax.dot_general(a_ref[...].astype(BF16), b_ref[...].astype(BF16), dims, preferred_element_type=F32)

        @pl.when(k == k_steps - 1)
        def _():
            acc = acc_ref[...]
            vals = (acc,) if epi is None else epi(acc, *[r[...] for r in extra_refs])
            if not isinstance(vals, (tuple, list)):
                vals = (vals,)
            for r, v in zip(out_refs, vals):
                r[...] = v.astype(r.dtype)

    assert a_off % tk == 0
    kb0 = a_off // tk
    a_spec = pl.BlockSpec((tk, tm), lambda i, j, k: (k, i)) if ta else pl.BlockSpec((tm, tk), lambda i, j, k: (i, kb0 + k))
    b_spec = pl.BlockSpec((tn, tk), lambda i, j, k: (j, k)) if tb else pl.BlockSpec((tk, tn), lambda i, j, k: (k, j))
    mn_spec = pl.BlockSpec((tm, tn), lambda i, j, k: (i, j))
    res = pl.pallas_call(
        body, name=name, grid=(m_dim // tm, n_dim // tn, k_steps),
        in_specs=[a_spec, b_spec] + [mn_spec] * n_extra,
        out_specs=[mn_spec] * n_out,
        out_shape=[jax.ShapeDtypeStruct((m_dim, n_dim), d) for d in out_dtypes],
        scratch_shapes=[pltpu.VMEM((tm, tn), F32)],
        compiler_params=_params(("parallel", "parallel", "arbitrary")),
    )(a, b, *extras)
    return res if n_out > 1 else res[0]


def _pair_cols(c0, c1, n):
    lane = lax.broadcasted_iota(jnp.int32, (n, LANES), 1)
    return jnp.where(lane < SSM_HEAD_DIM, c0, c1)


def ssd_step(g, state, xs, bm, cm, small, z, p_dtb, p_alog, p_dsk, nw):
    n = xs.shape[0]
    causal = _tril(n)
    dt_all = _softplus(small + p_dtb)
    a_all = dt_all * (-jnp.exp(p_alog))
    acum_all = _hdot(causal.astype(F32), a_all)
    cb = _bdot(cm, bm, NT)
    ys, new_states = [], []
    for pair in range(2):
        cols = []
        for h in range(2):
            hh = LANE_DT + SSM_HEADS_PER_GROUP * g + 2 * pair + h
            acum = _lane_col(acum_all, hh)
            dt = _lane_col(dt_all, hh)
            dsk = _lane_col(p_dsk, hh)
            decay = jnp.exp(jnp.where(causal, acum - _row_form(acum), -jnp.inf))
            a_last = acum[n - 1:n, :]
            cols.append((dt, jnp.exp(acum), jnp.exp(a_last - acum), jnp.exp(a_last), dsk, cb * decay))
        xp = xs[:, pair * LANES:(pair + 1) * LANES]
        dt_p = _pair_cols(cols[0][0], cols[1][0], n)
        eac_p = _pair_cols(cols[0][1], cols[1][1], n)
        ws_p = _pair_cols(cols[0][2], cols[1][2], n)
        dsk_p = _pair_cols(cols[0][4], cols[1][4], 1)
        xdt = xp * dt_p
        lane = lax.broadcasted_iota(jnp.int32, (n, LANES), 1)
        y_diag = jnp.where(lane < SSM_HEAD_DIM, _bdot(cols[0][5], xdt), _bdot(cols[1][5], xdt))
        st = state[pair]
        y_off = _bdot(cm, st, NT) * eac_p
        prow = lax.broadcasted_iota(jnp.int32, (LANES, LANES), 0)
        el_p = jnp.where(prow < SSM_HEAD_DIM, cols[0][3], cols[1][3])
        new_states.append(st * el_p + _bdot(xdt * ws_p, bm, TN))
        ys.append(y_diag + y_off + dsk_p * xp)
    y = jnp.concatenate(ys, axis=1) * _silu(z)
    return jnp.stack(new_states), _rms(y, nw)


def _unit_lower_inverse(a):
    n = a.shape[0]
    eye = (lax.broadcasted_iota(jnp.int32, (n, n), 0) == lax.broadcasted_iota(jnp.int32, (n, n), 1)).astype(F32)
    inv = eye - a
    power = a
    span = 2
    while span < n:
        power = _hdot(power, power)
        inv = inv + _hdot(inv, power)
        span *= 2
    return inv


def gdn_step(hq, state, q, k, v, z, small, p_dtb, p_alog, nw):
    n = q.shape[0]
    causal, strict = _tril(n), _tril(n, True)
    beta_all = _sigmoid(small)
    g_all = -jnp.exp(p_alog) * _softplus(small + p_dtb)
    gcum_all = _hdot(causal.astype(F32), g_all)
    qs = q * (GDN_HEAD ** -0.5)
    kk = _bdot(k, k, NT)
    qk = _bdot(qs, k, NT)
    outs, new_states = [], []
    for r in range(2):
        hv = 2 * hq + r
        gcum = _lane_col(gcum_all, LANE_A + hv)
        beta = _lane_col(beta_all, LANE_B + hv)
        dmat = jnp.exp(jnp.where(causal, gcum - _row_form(gcum), -jnp.inf))
        a_low = jnp.where(strict, beta * kk * dmat, 0.0)
        inv = _unit_lower_inverse(a_low)
        vr = v[:, r * LANES:(r + 1) * LANES]
        egc = jnp.exp(gcum)
        u = _hdot(inv, vr * beta)
        w = _hdot(inv, k * (beta * egc))
        st = state[r]
        v_new = u - _bdot(w, st)
        o = _bdot(qs * egc, st) + _bdot(qk * dmat, v_new)
        g_last = gcum[n - 1:n, :]
        k_dec = k * jnp.exp(g_last - gcum)
        new_states.append(st * jnp.exp(g_last) + _bdot(k_dec, v_new, TN))
        outs.append(_rms(o, nw) * _silu(z[:, r * LANES:(r + 1) * LANES]))
    return jnp.stack(new_states), jnp.concatenate(outs, axis=1)


STATE_SHAPE = (2, LANES, LANES)


def _scan_specs(rows, consts, chunk, chunk_of):
    specs = []
    for _, n, off, per_group in rows:
        assert off % n == 0
        cb = off // n
        if per_group:
            specs.append(pl.BlockSpec((chunk, n), lambda c, g, cb=cb: (chunk_of(c), cb + g)))
        else:
            specs.append(pl.BlockSpec((chunk, n), lambda c, g, cb=cb: (chunk_of(c), cb)))
    for arr, per_group in consts:
        if per_group:
            specs.append(pl.BlockSpec((None, 1, arr.shape[2]), lambda c, g: (g, 0, 0)))
        else:
            specs.append(pl.BlockSpec(arr.shape, lambda c, g, nd=arr.ndim: (0,) * nd))
    return specs


def scan_fwd(name, step, chunk, n_grp, rows, consts, out_cols):
    t = rows[0][0].shape[0]
    nc = t // chunk
    n_rows, n_consts = len(rows), len(consts)

    def body(*refs):
        row_refs, const_refs = refs[:n_rows], refs[n_rows:n_rows + n_consts]
        y_ref, st_ref, state = refs[n_rows + n_consts:]
        c, g = pl.program_id(0), pl.program_id(1)

        @pl.when(c == 0)
        def _():
            state[g] = jnp.zeros(STATE_SHAPE, F32)

        st = state[g]
        st_ref[...] = st
        new, y = step(g, st, *[r[...] for r in row_refs], *[r[...] for r in const_refs])
        state[g] = new
        y_ref[...] = y.astype(y_ref.dtype)

    return pl.pallas_call(
        body, name=name, grid=(nc, n_grp),
        in_specs=_scan_specs(rows, consts, chunk, lambda c: c),
        out_specs=[pl.BlockSpec((chunk, out_cols), lambda c, g: (c, g)),
                   pl.BlockSpec((None, None) + STATE_SHAPE, lambda c, g: (c, g, 0, 0, 0))],
        out_shape=[jax.ShapeDtypeStruct((t, n_grp * out_cols), BF16),
                   jax.ShapeDtypeStruct((nc, n_grp) + STATE_SHAPE, F32)],
        scratch_shapes=[pltpu.VMEM((n_grp,) + STATE_SHAPE, F32)],
        compiler_params=_params(("arbitrary", "arbitrary")),
    )(*[r[0] for r in rows], *[c[0] for c in consts])


def scan_bwd(name, step, chunk, n_grp, rows, consts, states, dy, row_dtypes):
    t = rows[0][0].shape[0]
    nc = t // chunk
    n_rows, n_consts = len(rows), len(consts)
    out_cols = dy.shape[1] // n_grp

    def body(*refs):
        row_refs, const_refs = refs[:n_rows], refs[n_rows:n_rows + n_consts]
        st_ref, dy_ref = refs[n_rows + n_consts:n_rows + n_consts + 2]
        outs = refs[n_rows + n_consts + 2:-1]
        drow_refs, dconst_refs = outs[:n_rows], outs[n_rows:]
        dstate = refs[-1]
        c, g = pl.program_id(0), pl.program_id(1)

        @pl.when(c == 0)
        def _():
            dstate[g] = jnp.zeros(STATE_SHAPE, F32)

        @pl.when(jnp.logical_and(c == 0, g == 0))
        def _():
            for r in dconst_refs:
                r[...] = jnp.zeros(r.shape, r.dtype)

        _, vjp = jax.vjp(functools.partial(step, g), st_ref[...], *[r[...] for r in row_refs], *[r[...] for r in const_refs])
        grads = vjp((dstate[g], dy_ref[...].astype(F32)))
        dstate[g] = grads[0]
        for (_, _, _, per_group), r, d in zip(rows, drow_refs, grads[1:1 + n_rows]):
            if per_group:
                r[...] = d.astype(r.dtype)
            else:
                @pl.when(g == 0)
                def _(r=r):
                    r[...] = jnp.zeros(r.shape, r.dtype)

                r[...] += d.astype(r.dtype)
        for (_, per_group), r, d in zip(consts, dconst_refs, grads[1 + n_rows:]):
            if per_group:
                r[g] += d
            else:
                r[...] += d

    rev = lambda c: nc - 1 - c
    out_specs, out_shape = [], []
    for (_, n, _, per_group), dt in zip(rows, row_dtypes):
        if per_group:
            out_specs.append(pl.BlockSpec((chunk, n), lambda c, g: (rev(c), g)))
            out_shape.append(jax.ShapeDtypeStruct((t, n_grp * n), dt))
        else:
            out_specs.append(pl.BlockSpec((chunk, n), lambda c, g: (rev(c), 0)))
            out_shape.append(jax.ShapeDtypeStruct((t, n), dt))
    for arr, _ in consts:
        out_specs.append(pl.BlockSpec(arr.shape, lambda c, g, nd=arr.ndim: (0,) * nd))
        out_shape.append(jax.ShapeDtypeStruct(arr.shape, F32))
    return pl.pallas_call(
        body, name=name, grid=(nc, n_grp),
        in_specs=_scan_specs(rows, consts, chunk, rev)
        + [pl.BlockSpec((None, None) + STATE_SHAPE, lambda c, g: (rev(c), g, 0, 0, 0)),
           pl.BlockSpec((chunk, out_cols), lambda c, g: (rev(c), g))],
        out_specs=out_specs, out_shape=out_shape,
        scratch_shapes=[pltpu.VMEM((n_grp,) + STATE_SHAPE, F32)],
        compiler_params=_params(("arbitrary", "arbitrary")),
    )(*[r[0] for r in rows], *[c[0] for c in consts], states, dy)


@functools.partial(jax.custom_vjp, nondiff_argnums=(1,))
def _shift_rows(x, k):
    t = x.shape[0]
    row = lax.broadcasted_iota(jnp.int32, x.shape, 0)
    rolled = pltpu.roll(x, k % t, 0)
    return jnp.where(jnp.logical_and(row >= k, row < t + k), rolled, 0.0)


def _shift_rows_fwd(x, k):
    return _shift_rows(x, k), None


def _shift_rows_bwd(k, _, dy):
    return (_shift_rows(dy, -k),)


_shift_rows.defvjp(_shift_rows_fwd, _shift_rows_bwd)


def _conv_silu(x, cw, cb):
    pre = cb + sum(cw[j:j + 1, :] * _shift_rows(x, CONV_K - 1 - j) for j in range(CONV_K))
    return _silu(pre)


def _conv_silu_l2(x, cw, cb):
    y = _conv_silu(x, cw, cb)
    return y * lax.rsqrt(jnp.sum(y * y, axis=-1, keepdims=True) + EPS)


def conv_fwd(name, fn, src, n, off, cw, cb, cw_off):
    t = src.shape[0]
    sb, wb = off // LANES, cw_off // LANES
    ins = [(src, (t, LANES), lambda i: (0, sb + i)), (cw, (CONV_K, LANES), lambda i: (0, wb + i)),
           (cb, (1, LANES), lambda i: (0, wb + i))]
    return blockmap(name, fn, (n // LANES,), ins, [((t, n), F32, (t, LANES), lambda i: (0, i))])[0]


def conv_bwd(name, fn, src, n, off, cw, cb, cw_off, dy):
    t = src.shape[0]
    sb, wb = off // LANES, cw_off // LANES

    def bwd(x, w, b, d):
        _, vjp = jax.vjp(fn, x, w, b)
        return vjp(d.astype(F32))

    ins = [(src, (t, LANES), lambda i: (0, sb + i)), (cw, (CONV_K, LANES), lambda i: (0, wb + i)),
           (cb, (1, LANES), lambda i: (0, wb + i)), (dy, (t, LANES), lambda i: (0, i))]
    outs = [((t, n), BF16, (t, LANES), lambda i: (0, i)), ((CONV_K, n), F32, (CONV_K, LANES), lambda i: (0, i)),
            ((1, n), F32, (1, LANES), lambda i: (0, i))]
    return blockmap(name, bwd, (n // LANES,), ins, outs)


def _place():
    return lax.axis_index("x"), lax.axis_index("y"), lax.axis_index("c")


def _other_chips(x, y):
    return [(1 - x, y), (x, 1 - y), (1 - x, 1 - y)]


ANY = pl.BlockSpec(memory_space=pl.ANY)


def all_gather8(name, v):
    m_per, n = v.shape

    def body(x_ref, out_ref, send_sems, recv_sems, local_sem):
        x, y, c = _place()
        me, sibling = (x, y, c), (x, y, 1 - c)
        chips = _other_chips(x, y)

        def rows(px, py, pc):
            return out_ref.at[pl.ds((4 * px + 2 * py + pc) * m_per, m_per), :]

        def copy(k, block, to, src=None):
            return pltpu.make_async_remote_copy(
                src_ref=rows(*block) if src is None else src, dst_ref=rows(*block),
                send_sem=send_sems.at[k], recv_sem=recv_sems.at[k], device_id=to, device_id_type=MESH)

        mine = pltpu.make_async_copy(x_ref, rows(*me), local_sem)
        mine.start()
        first = [copy(0, me, sibling, src=x_ref)]
        first += [copy(1 + q, me, (*chip, c), src=x_ref) for q, chip in enumerate(chips)]
        for cp in first:
            cp.start()
        passed = [copy(4 + q, (*chip, c), sibling) for q, chip in enumerate(chips)]
        for q, chip in enumerate(chips):
            copy(1 + q, (*chip, c), me).wait_recv()
            passed[q].start()
        copy(0, sibling, me).wait_recv()
        for q, chip in enumerate(chips):
            copy(4 + q, (*chip, 1 - c), me).wait_recv()
        for cp in first + passed:
            cp.wait_send()
        mine.wait()

    return pl.pallas_call(
        body, name=name, out_shape=jax.ShapeDtypeStruct((N_DEV * m_per, n), v.dtype),
        in_specs=[pl.BlockSpec(memory_space=pltpu.VMEM)], out_specs=pl.BlockSpec(memory_space=pltpu.VMEM),
        scratch_shapes=[pltpu.SemaphoreType.DMA((7,)), pltpu.SemaphoreType.DMA((7,)), pltpu.SemaphoreType.DMA],
    )(v)


def gather_flat(name, vec):
    n = vec.shape[0]
    n_pad = -(-n // (8 * LANES)) * (8 * LANES)
    v = jnp.pad(vec, (0, n_pad - n)).reshape(8, n_pad // 8)
    return all_gather8(name, v).reshape(N_DEV, n_pad)[:, :n]


def all_gather_shards(name, shards):
    n_t = len(shards)

    def body(*refs):
        ins, outs = refs[:n_t], refs[n_t:2 * n_t]
        send_sems, recv_sems, local_sems = refs[2 * n_t:]
        x, y, c = _place()
        sibling = (x, y, 1 - c)
        chips = _other_chips(x, y)
        mine = 2 * x + y

        def half(t, pc):
            h = ins[t].shape[0] // 2
            return pl.ds(pc * h, h)

        def copy(t, k, quarter, pc, to, src=None):
            dst = outs[t].at[quarter, half(t, pc)]
            return pltpu.make_async_remote_copy(
                src_ref=dst if src is None else src, dst_ref=dst,
                send_sem=send_sems.at[t, k], recv_sem=recv_sems.at[t, k], device_id=to, device_id_type=MESH)

        local = [pltpu.make_async_copy(ins[t], outs[t].at[mine], local_sems.at[t]) for t in range(n_t)]
        for cp in local:
            cp.start()
        first = [copy(t, q, mine, c, (*chip, c), src=ins[t].at[half(t, c)]) for t in range(n_t) for q, chip in enumerate(chips)]
        for cp in first:
            cp.start()
        passed = []
        for t in range(n_t):
            for q, (px, py) in enumerate(chips):
                copy(t, q, 2 * px + py, c, (x, y, c)).wait_recv()
                cp = copy(t, 3 + q, 2 * px + py, c, sibling)
                cp.start()
                passed.append(cp)
        for t in range(n_t):
            for q, (px, py) in enumerate(chips):
                copy(t, 3 + q, 2 * px + py, 1 - c, (x, y, c)).wait_recv()
        for cp in first + passed:
            cp.wait_send()
        for cp in local:
            cp.wait()

    return pl.pallas_call(
        body, name=name, out_shape=[jax.ShapeDtypeStruct((4,) + s.shape, s.dtype) for s in shards],
        in_specs=[ANY] * n_t, out_specs=[ANY] * n_t,
        scratch_shapes=[pltpu.SemaphoreType.DMA((n_t, 6)), pltpu.SemaphoreType.DMA((n_t, 6)), pltpu.SemaphoreType.DMA((n_t,))],
    )(*shards)


def exchange_halves_d2d(name, grads):
    n_t = len(grads)

    def body(*refs):
        ins, outs = refs[:n_t], refs[n_t:2 * n_t]
        send_sems, recv_sems = refs[2 * n_t:]
        x, y, c = _place()
        copies = []
        for t in range(n_t):
            h = ins[t].shape[1] // 2
            copies.append(pltpu.make_async_remote_copy(
                src_ref=ins[t].at[:, pl.ds((1 - c) * h, h), :], dst_ref=outs[t],
                send_sem=send_sems.at[t], recv_sem=recv_sems.at[t], device_id=(x, y, 1 - c), device_id_type=MESH))
        for cp in copies:
            cp.start()
        for cp in copies:
            cp.wait()

    return pl.pallas_call(
        body, name=name,
        out_shape=[jax.ShapeDtypeStruct((4, g.shape[1] // 2, g.shape[2]), g.dtype) for g in grads],
        in_specs=[ANY] * n_t, out_specs=[ANY] * n_t,
        scratch_shapes=[pltpu.SemaphoreType.DMA((n_t,)), pltpu.SemaphoreType.DMA((n_t,))],
    )(*grads)


def exchange_quarters_ici(name, parts):
    n_t = len(parts)

    def body(*refs):
        ins, outs = refs[:n_t], refs[n_t:2 * n_t]
        send_sems, recv_sems = refs[2 * n_t:]
        x, y, c = _place()
        copies = []
        for t in range(n_t):
            for q, (px, py) in enumerate(_other_chips(x, y)):
                copies.append(pltpu.make_async_remote_copy(
                    src_ref=ins[t].at[2 * px + py], dst_ref=outs[t].at[q],
                    send_sem=send_sems.at[t, q], recv_sem=recv_sems.at[t, q], device_id=(px, py, c), device_id_type=MESH))
        for cp in copies:
            cp.start()
        for cp in copies:
            cp.wait()

    return pl.pallas_call(
        body, name=name,
        out_shape=[jax.ShapeDtypeStruct((3,) + p.shape[1:], p.dtype) for p in parts],
        in_specs=[ANY] * n_t, out_specs=[ANY] * n_t,
        scratch_shapes=[pltpu.SemaphoreType.DMA((n_t, 3)), pltpu.SemaphoreType.DMA((n_t, 3))],
    )(*parts)


def join_halves_d2d(name, halves):
    n_t = len(halves)

    def body(*refs):
        ins, outs = refs[:n_t], refs[n_t:2 * n_t]
        send_sems, recv_sems, local_sems = refs[2 * n_t:]
        x, y, c = _place()
        local, remote = [], []
        for t in range(n_t):
            h = ins[t].shape[0]
            dst = outs[t].at[pl.ds(c * h, h)]
            local.append(pltpu.make_async_copy(ins[t], dst, local_sems.at[t]))
            remote.append(pltpu.make_async_remote_copy(
                src_ref=ins[t], dst_ref=dst, send_sem=send_sems.at[t], recv_sem=recv_sems.at[t],
                device_id=(x, y, 1 - c), device_id_type=MESH))
        for cp in local + remote:
            cp.start()
        for t in range(n_t):
            h = ins[t].shape[0]
            other = outs[t].at[pl.ds((1 - c) * h, h)]
            pltpu.make_async_remote_copy(
                src_ref=ins[t], dst_ref=other, send_sem=send_sems.at[t], recv_sem=recv_sems.at[t],
                device_id=(x, y, 1 - c), device_id_type=MESH).wait_recv()
        for cp in remote:
            cp.wait_send()
        for cp in local:
            cp.wait()

    return pl.pallas_call(
        body, name=name,
        out_shape=[jax.ShapeDtypeStruct((2 * h.shape[0],) + h.shape[1:], h.dtype) for h in halves],
        in_specs=[ANY] * n_t, out_specs=[ANY] * n_t,
        scratch_shapes=[pltpu.SemaphoreType.DMA((n_t,)), pltpu.SemaphoreType.DMA((n_t,)), pltpu.SemaphoreType.DMA((n_t,))],
    )(*halves)


BLOCK_BYTES = 1 << 20


def _row_block(r, c):
    rb = r
    while rb * c * 4 > BLOCK_BYTES and rb % 16 == 0:
        rb //= 2
    return rb


def reduce_scatter_grads(grads):
    x, y, c = _place()
    place = jnp.stack([c, 2 * x + y]).astype(jnp.int32)
    from_sibling = exchange_halves_d2d("rs_d2d", grads)
    parts = []
    for t, (g, s) in enumerate(zip(grads, from_sibling)):
        _, h, cols = s.shape
        rb = _row_block(h, cols)
        nb = h // rb
        parts.append(blockmap(
            f"rs_add{t}", lambda a, b: a + b, (4, nb),
            [(g, (None, rb, cols), lambda k, i, s_ref, nb=nb: (k, s_ref[0] * nb + i, 0)), (s, (None, rb, cols), lambda k, i, s_ref: (k, i, 0))],
            [(s.shape, F32, (None, rb, cols), lambda k, i, s_ref: (k, i, 0))], scalars=place)[0])
    from_chips = exchange_quarters_ici("rs_ici", parts)
    halves = []
    for t, (p, q) in enumerate(zip(parts, from_chips)):
        _, h, cols = p.shape
        rb = _row_block(h, cols)
        halves.append(blockmap(
            f"rs_sum{t}", lambda a, b: a + b[0] + b[1] + b[2], (h // rb,),
            [(p, (None, rb, cols), lambda i, s_ref: (s_ref[1], i, 0)), (q, (3, rb, cols), lambda i, s_ref: (0, i, 0))],
            [((h, cols), F32, (rb, cols), lambda i, s_ref: (i, 0))], scalars=place)[0])
    return join_halves_d2d("rs_join", halves)


def _adamw(w, g, m, v):
    m = ADAM_B1 * m + (1.0 - ADAM_B1) * g
    v = ADAM_B2 * v + (1.0 - ADAM_B2) * jnp.square(g)
    m_hat = m / (1.0 - ADAM_B1 ** ADAM_STEP)
    v_hat = v / (1.0 - ADAM_B2 ** ADAM_STEP)
    delta = -ADAM_LR * (m_hat / (jnp.sqrt(v_hat) + ADAM_EPS) + ADAM_WD * w)
    return delta, m, v


def adamw(name, w, g, m, v):
    r, c = w.shape
    rb = _row_block(r, c)
    blk = lambda a: (a, (rb, c), lambda i: (i, 0))
    return blockmap(name, _adamw, (r // rb,), [blk(w), blk(g), blk(m), blk(v)], [((r, c), F32, (rb, c), lambda i: (i, 0))] * 3)


def _whole(name, fn, ins, outs):
    return blockmap(name, fn, (1,), [(a, a.shape, lambda i, nd=a.ndim: (0,) * nd) for a in ins],
                    [(s, d, s, lambda i, nd=len(s): (0,) * nd) for s, d in outs])


def _premix(x, w, sc, sh):
    return _rms(x, w) * (1.0 + sc) + sh


def _postmix(x, u, w_post, g1, w_pre2, sc2, sh2):
    x1 = x + g1 * _rms(u, w_post)
    return x1, _premix(x1, w_pre2, sc2, sh2)


def _merge(gs, gg, ys, yg):
    return _sigmoid(gs) * ys + _sigmoid(gg) * yg


def _final(x1, y2, w_post2, g2):
    return x1 + g2 * _rms(y2, w_post2)


def kernel(x, c, w_ada, b_ada, norm_mix_pre, norm_mix_post, w_in, ssm_conv_w, ssm_conv_b, ssm_dt_bias, ssm_A_log, ssm_D, ssm_norm_w, gdn_conv_w, gdn_dt_bias, gdn_A_log, gdn_norm_w, w_ssm_up, w_gdn_up, w_out, norm_mlp_pre, norm_mlp_post, w_mlp_up, w_mlp_down, loss_target, m_w_ada, m_b_ada, m_norm_mix_pre, m_norm_mix_post, m_w_in, m_ssm_conv_w, m_ssm_conv_b, m_ssm_dt_bias, m_ssm_A_log, m_ssm_D, m_ssm_norm_w, m_gdn_conv_w, m_gdn_dt_bias, m_gdn_A_log, m_gdn_norm_w, m_w_ssm_up, m_w_gdn_up, m_w_out, m_norm_mlp_pre, m_norm_mlp_post, m_w_mlp_up, m_w_mlp_down, v_w_ada, v_b_ada, v_norm_mix_pre, v_norm_mix_post, v_w_in, v_ssm_conv_w, v_ssm_conv_b, v_ssm_dt_bias, v_ssm_A_log, v_ssm_D, v_ssm_norm_w, v_gdn_conv_w, v_gdn_dt_bias, v_gdn_A_log, v_gdn_norm_w, v_w_ssm_up, v_w_gdn_up, v_w_out, v_norm_mlp_pre, v_norm_mlp_post, v_w_mlp_up, v_w_mlp_down):
    args = dict(locals())
    xi, yi, ci = _place()
    quarter = 2 * xi + yi
    batch = 4 * xi + 2 * yi + ci

    xt, target = x[0], loss_target[0]
    t, d = xt.shape
    hs, hv = ssm_dt_bias.shape[-1], gdn_dt_bias.shape[-1]
    d_inner = hs * SSM_HEAD_DIM
    n_grp = hs // SSM_HEADS_PER_GROUP
    gn = n_grp * SSM_D_STATE
    conv_ssm = d_inner + 2 * gn
    hq = hv // 2
    key, val = hq * GDN_HEAD, hv * GDN_HEAD
    conv_gdn = 2 * key + val
    hidden = 4 * w_mlp_up.shape[-1]
    o_dt = d_inner + conv_ssm
    o_qkv = o_dt + hs
    o_b = o_qkv + conv_gdn + val
    o_a = o_b + hv
    o_gs = o_a + hv
    n_proj = o_gs + 2 * d
    a_z, a_xs, a_bm, a_cm = 0, d_inner, 2 * d_inner, 2 * d_inner + gn
    a_q = o_dt
    a_k, a_v, a_zg = a_q + key, a_q + 2 * key, a_q + conv_gdn
    a_gs = a_zg + val
    a_gg = a_gs + d
    a_small = a_gg + d
    n_al = -(-(a_small + LANES) // 512) * 512

    def to_aligned(w):
        z = lambda n: jnp.zeros((w.shape[0], n), w.dtype)
        return jnp.concatenate([
            w[:, :o_dt], w[:, o_qkv:o_b], w[:, o_gs:],
            w[:, o_dt:o_qkv], z(LANE_B - hs), w[:, o_b:o_a], z(LANE_A - LANE_B - hv), w[:, o_a:o_gs], z(LANES - LANE_A - hv),
            z(n_al - a_small - LANES)], axis=1)

    def from_aligned(w):
        s = a_small
        return jnp.concatenate([
            w[:, :o_dt], w[:, s + LANE_DT:s + LANE_DT + hs], w[:, a_q:a_gs], w[:, s + LANE_B:s + LANE_B + hv],
            w[:, s + LANE_A:s + LANE_A + hv], w[:, a_gs:a_small]], axis=1)

    def lanes(vec, at):
        return jnp.zeros((1, LANES), F32).at[:, at:at + vec.shape[-1]].set(vec.reshape(1, -1))

    n_cw = CONV_K * ssm_conv_w.shape[-1]
    small_in = gather_flat("ag_small", jnp.concatenate([c.reshape(-1), ssm_conv_w.reshape(-1), gdn_conv_w.reshape(-1)]))
    c_all = small_in[:, :d]
    by_chip = small_in[0::2]

    def whole_conv_w(lo):
        return jnp.transpose(by_chip[:, lo:lo + n_cw].reshape(4, CONV_K, -1), (1, 0, 2)).reshape(CONV_K, -1)

    cw_ssm, cw_gdn = whole_conv_w(d), whole_conv_w(d + n_cw)
    cb_ssm = ssm_conv_b
    cb_gdn = jnp.zeros((1, conv_gdn), F32)

    n_ada = w_ada.shape[-1]
    b_q = lax.dynamic_slice_in_dim(b_ada, quarter * n_ada, n_ada, axis=1)
    mod_q = _whole("ada_fwd", lambda ca, w, b: _bdot(_silu(ca), w) + b, [c_all, w_ada[0], b_q], [((N_DEV, n_ada), F32)])[0]
    mod_all = gather_flat("ag_mod", mod_q.reshape(-1)).reshape(N_DEV, N_DEV, n_ada)[0::2]
    mod = lax.dynamic_index_in_dim(mod_all, batch, axis=1, keepdims=False).reshape(1, 4 * n_ada)
    sh1, sc1, g1, sh2, sc2, g2 = [mod[:, i * d:(i + 1) * d] for i in range(6)]

    gathered = all_gather_shards("ag_w", [w.astype(BF16) for w in (w_in[0], w_ssm_up[0], w_gdn_up[0], w_out[0], w_mlp_up[0], w_mlp_down[0])])
    cols_major = lambda g: jnp.transpose(g, (1, 0, 2)).reshape(g.shape[1], -1)
    rows_major = lambda g: g.reshape(-1, g.shape[2])
    wb_in = to_aligned(cols_major(gathered[0]))
    wb_ssm_up, wb_gdn_up, wb_out = rows_major(gathered[1]), rows_major(gathered[2]), rows_major(gathered[3])
    wb_up, wb_down = cols_major(gathered[4]), rows_major(gathered[5])

    h1 = rowmap("premix", _premix, [xt], [norm_mix_pre, sc1, sh1], [(d, BF16)])[0]
    proj = matmul("in_proj", h1, wb_in)
    xs = conv_fwd("conv_xs", _conv_silu, proj, d_inner, a_xs, cw_ssm, cb_ssm, 0)
    bm = conv_fwd("conv_bm", _conv_silu, proj, gn, a_bm, cw_ssm, cb_ssm, d_inner)
    cm = conv_fwd("conv_cm", _conv_silu, proj, gn, a_cm, cw_ssm, cb_ssm, d_inner + gn)
    q = conv_fwd("conv_q", _conv_silu_l2, proj, key, a_q, cw_gdn, cb_gdn, 0)
    k = conv_fwd("conv_k", _conv_silu_l2, proj, key, a_k, cw_gdn, cb_gdn, key)
    v = conv_fwd("conv_v", _conv_silu, proj, val, a_v, cw_gdn, cb_gdn, 2 * key)

    wide = 2 * LANES
    ssd_rows = [(xs, wide, 0, True), (bm, LANES, 0, True), (cm, LANES, 0, True), (proj, LANES, a_small, False), (proj, wide, a_z, True)]
    ssd_consts = [(lanes(ssm_dt_bias, LANE_DT), False), (lanes(ssm_A_log, LANE_DT), False), (lanes(ssm_D, LANE_DT), False),
                  (ssm_norm_w.reshape(n_grp, 1, wide), True)]
    y_ssm_n, st_ssm = scan_fwd("ssd_fwd", ssd_step, SSM_CHUNK, n_grp, ssd_rows, ssd_consts, wide)
    gdn_rows = [(q, LANES, 0, True), (k, LANES, 0, True), (v, wide, 0, True), (proj, wide, a_zg, True), (proj, LANES, a_small, False)]
    gdn_consts = [(lanes(gdn_dt_bias, LANE_A), False), (lanes(gdn_A_log, LANE_A), False), (gdn_norm_w, False)]
    y_gdn_n, st_gdn = scan_fwd("gdn_fwd", gdn_step, GDN_CHUNK, hq, gdn_rows, gdn_consts, wide)

    y_ssm = matmul("ssm_up", y_ssm_n, wb_ssm_up)
    y_gdn = matmul("gdn_up", y_gdn_n, wb_gdn_up)
    gates = [(proj, d, a_gs), (proj, d, a_gg)]
    merged = rowmap("merge", _merge, gates + [y_ssm, y_gdn], [], [(d, BF16)])[0]
    u = matmul("w_out", merged, wb_out)
    post_consts = [norm_mix_post, g1, norm_mlp_pre, sc2, sh2]
    x1, h2 = rowmap("postmix", _postmix, [xt, u], post_consts, [(d, F32), (d, BF16)])
    relu2 = lambda acc: (acc, jnp.square(jnp.maximum(acc, 0.0)))
    a_up, act = matmul("mlp_up", h2, wb_up, out_dtypes=(BF16, BF16), epi=relu2)
    y2 = matmul("mlp_down", act, wb_down)

    def final_bwd(x1_, y2_, tgt, w_, g_):
        x2, vjp = jax.vjp(_final, x1_, y2_, w_, g_)
        err = x2 - tgt
        loss = 0.5 * jnp.sum(jnp.mean(err * err, axis=-1, keepdims=True), axis=0, keepdims=True)
        dx1, dy2, dw, dg = vjp(err / d)
        return dx1, dy2, loss, dw, dg

    dx1, dy2, loss_part, d_norm_mlp_post, dg2 = rowmap(
        "final", final_bwd, [x1, y2, target], [norm_mlp_post, g2], [(d, F32), (d, BF16)], [((1, 1), F32), ((1, d), F32), ((1, d), F32)])
    loss = lax.psum(loss_part[0, 0], ("x", "y", "c"))

    d_a = matmul("mlp_down_dx", dy2, wb_down, tb=True, out_dtypes=(BF16,), extras=[a_up],
                 epi=lambda acc, a: acc * 2.0 * jnp.maximum(a.astype(F32), 0.0))
    gw_down = matmul("mlp_down_dw", act, dy2, ta=True)
    dh2 = matmul("mlp_up_dx", d_a, wb_up, tb=True)
    gw_up = matmul("mlp_up_dw", h2, d_a, ta=True)

    def postmix_bwd(x_, u_, dx1_, dh2_, *cs):
        _, vjp = jax.vjp(_postmix, x_, u_, *cs)
        return vjp((dx1_, dh2_))

    dxa, du, d_norm_mix_post, dg1, d_norm_mlp_pre, dsc2, dsh2 = rowmap(
        "postmix_bwd", postmix_bwd, [xt, u, dx1, dh2], post_consts, [(d, F32), (d, BF16)], [((1, d), F32)] * 5)
    d_merged = matmul("w_out_dx", du, wb_out, tb=True)
    gw_out = matmul("w_out_dw", merged, du, ta=True)

    def merge_bwd(gs, gg, ys, yg, dm):
        _, vjp = jax.vjp(_merge, gs, gg, ys, yg)
        dgs, dgg, dys, dyg = vjp(dm)
        return dys, dyg, dgs, dgg

    dy_ssm, dy_gdn, dgs, dgg = rowmap("merge_bwd", merge_bwd, gates + [y_ssm, y_gdn, d_merged], [], [(d, BF16)] * 4)
    dy_ssm_n = matmul("ssm_up_dx", dy_ssm, wb_ssm_up, tb=True, out_dtypes=(BF16,))
    gw_ssm_up = matmul("ssm_up_dw", y_ssm_n, dy_ssm, ta=True)
    dy_gdn_n = matmul("gdn_up_dx", dy_gdn, wb_gdn_up, tb=True, out_dtypes=(BF16,))
    gw_gdn_up = matmul("gdn_up_dw", y_gdn_n, dy_gdn, ta=True)

    dxs, dbm, dcm, dsmall_ssm, dz_ssm, d_sdtb, d_salog, d_sdsk, d_snw = scan_bwd(
        "ssd_bwd", ssd_step, SSM_CHUNK, n_grp, ssd_rows, ssd_consts, st_ssm, dy_ssm_n, [BF16, BF16, BF16, F32, BF16])
    dq, dk, dv, dz_gdn, dsmall_gdn, d_gdtb, d_galog, d_gnw = scan_bwd(
        "gdn_bwd", gdn_step, GDN_CHUNK, hq, gdn_rows, gdn_consts, st_gdn, dy_gdn_n, [BF16, BF16, BF16, BF16, F32])

    dxs_p, dcw_xs, dcb_xs = conv_bwd("conv_xs_bwd", _conv_silu, proj, d_inner, a_xs, cw_ssm, cb_ssm, 0, dxs)
    dbm_p, dcw_bm, dcb_bm = conv_bwd("conv_bm_bwd", _conv_silu, proj, gn, a_bm, cw_ssm, cb_ssm, d_inner, dbm)
    dcm_p, dcw_cm, dcb_cm = conv_bwd("conv_cm_bwd", _conv_silu, proj, gn, a_cm, cw_ssm, cb_ssm, d_inner + gn, dcm)
    dq_p, dcw_q, _ = conv_bwd("conv_q_bwd", _conv_silu_l2, proj, key, a_q, cw_gdn, cb_gdn, 0, dq)
    dk_p, dcw_k, _ = conv_bwd("conv_k_bwd", _conv_silu_l2, proj, key, a_k, cw_gdn, cb_gdn, key, dk)
    dv_p, dcw_v, _ = conv_bwd("conv_v_bwd", _conv_silu, proj, val, a_v, cw_gdn, cb_gdn, 2 * key, dv)
    dsmall = rowmap("small_sum", lambda a, b: a + b, [dsmall_ssm, dsmall_gdn], [], [(LANES, BF16)])[0]
    dproj = jnp.concatenate([dz_ssm, dxs_p, dbm_p, dcm_p, dq_p, dk_p, dv_p, dz_gdn, dgs, dgg, dsmall,
                             jnp.zeros((t, n_al - a_small - LANES), BF16)], axis=1)
    dh1 = matmul("in_proj_dx", dproj, wb_in, tb=True)
    gw_in_al = matmul("in_proj_dw", h1, dproj, ta=True)

    def premix_bwd(x_, dxa_, dh1_, w_, sc_, sh_):
        _, vjp = jax.vjp(_premix, x_, w_, sc_, sh_)
        dx, dw, dsc, dsh = vjp(dh1_)
        return dx + dxa_, dw, dsc, dsh

    grad_x, d_norm_mix_pre, dsc1, dsh1 = rowmap(
        "premix_bwd", premix_bwd, [xt, dxa, dh1], [norm_mix_pre, sc1, sh1], [(d, F32)], [((1, d), F32)] * 3)

    dmod_all = gather_flat("ag_dmod", jnp.concatenate([dsh1, dsc1, dg1, dsh2, dsc2, dg2], axis=1).reshape(-1))
    dmod_q = lax.dynamic_slice_in_dim(dmod_all, quarter * n_ada, n_ada, axis=1)
    gw_ada, gb_ada = _whole(
        "ada_bwd", lambda ca, dq_, da_: (_bdot(_silu(ca), dq_, TN), jnp.sum(da_, axis=0, keepdims=True)),
        [c_all, dmod_q, dmod_all], [((d, n_ada), F32), ((1, 4 * n_ada), F32)])

    dcw_ssm = jnp.concatenate([dcw_xs, dcw_bm, dcw_cm], axis=1)
    dcb_ssm = jnp.concatenate([dcb_xs, dcb_bm, dcb_cm], axis=1)
    dcw_gdn = jnp.concatenate([dcw_q, dcw_k, dcw_v], axis=1)
    partial = [d_norm_mix_pre, d_norm_mix_post, dcw_ssm, dcb_ssm, d_sdtb[:, LANE_DT:LANE_DT + hs], d_salog[:, LANE_DT:LANE_DT + hs],
               d_sdsk[:, LANE_DT:LANE_DT + hs], d_snw, dcw_gdn, d_gdtb[:, LANE_A:LANE_A + hv], d_galog[:, LANE_A:LANE_A + hv], d_gnw,
               d_norm_mlp_pre, d_norm_mlp_post]
    sizes = [p.size for p in partial]
    stacked = gather_flat("ag_grads", jnp.concatenate([p.reshape(-1) for p in partial]))
    summed = _whole("small_sum8", lambda s: jnp.sum(s, axis=0, keepdims=True), [stacked], [((1, stacked.shape[1]), F32)])[0][0]
    offs = [0]
    for s in sizes:
        offs.append(offs[-1] + s)
    red = [summed[offs[i]:offs[i + 1]] for i in range(len(sizes))]
    my_cols = lambda full: lax.dynamic_slice_in_dim(full.reshape(CONV_K, -1), quarter * (n_cw // CONV_K), n_cw // CONV_K, axis=1)
    small_grads = {
        "b_ada": gb_ada, "norm_mix_pre": red[0], "norm_mix_post": red[1], "ssm_conv_w": my_cols(red[2]), "ssm_conv_b": red[3],
        "ssm_dt_bias": red[4], "ssm_A_log": red[5], "ssm_D": red[6], "ssm_norm_w": red[7], "gdn_conv_w": my_cols(red[8]),
        "gdn_dt_bias": red[9], "gdn_A_log": red[10], "gdn_norm_w": red[11], "norm_mlp_pre": red[12], "norm_mlp_post": red[13]}

    quarters_cols = lambda g: jnp.transpose(g.reshape(g.shape[0], 4, -1), (1, 0, 2))
    quarters_rows = lambda g: g.reshape(4, g.shape[0] // 4, g.shape[1])
    big_names = ["w_in", "w_ssm_up", "w_gdn_up", "w_out", "w_mlp_up", "w_mlp_down"]
    big_partial = [quarters_cols(from_aligned(gw_in_al)), quarters_rows(gw_ssm_up), quarters_rows(gw_gdn_up), quarters_rows(gw_out),
                   quarters_cols(gw_up), quarters_rows(gw_down)]
    big_grads = dict(zip(big_names, reduce_scatter_grads(big_partial)))
    big_grads["w_ada"] = gw_ada

    names = ['w_ada', 'b_ada', 'norm_mix_pre', 'norm_mix_post', 'w_in', 'ssm_conv_w', 'ssm_conv_b', 'ssm_dt_bias', 'ssm_A_log', 'ssm_D',
             'ssm_norm_w', 'gdn_conv_w', 'gdn_dt_bias', 'gdn_A_log', 'gdn_norm_w', 'w_ssm_up', 'w_gdn_up', 'w_out', 'norm_mlp_pre',
             'norm_mlp_post', 'w_mlp_up', 'w_mlp_down']
    grad, delta, new_m, new_v = {}, {}, {}, {}
    for n in big_grads:
        shape = args[n].shape
        g2d = big_grads[n]
        dl, nm, nv = adamw("adamw_" + n, args[n][0], g2d, args["m_" + n][0], args["v_" + n][0])
        grad[n], delta[n], new_m[n], new_v[n] = [a.reshape(shape) for a in (g2d, dl, nm, nv)]
    small_names = [n for n in names if n not in big_grads]
    flat = lambda pre: jnp.concatenate([args[pre + n].reshape(-1) for n in small_names]).reshape(1, -1)
    g_flat = jnp.concatenate([small_grads[n].reshape(-1) for n in small_names]).reshape(1, -1)
    dl, nm, nv = adamw("adamw_small", flat(""), g_flat, flat("m_"), flat("v_"))
    off = 0
    for n in small_names:
        shape = args[n].shape
        size = args[n].size
        grad[n], delta[n], new_m[n], new_v[n] = [a[0, off:off + size].reshape(shape) for a in (g_flat, dl, nm, nv)]
        off += size

    return (loss, grad_x.reshape(x.shape), *[grad[n] for n in names], *[delta[n] for n in names],
            *[new_m[n] for n in names], *[new_v[n] for n in names])
```

```python
import functools

import jax
import jax.numpy as jnp
from jax import lax
from jax.experimental import pallas as pl
from jax.experimental.pallas import tpu as pltpu

F32 = jnp.float32
BF16 = jnp.bfloat16
MESH = pl.DeviceIdType.MESH

EPS = 1e-6
SSM_HEAD_DIM = 64
SSM_HEADS_PER_GROUP = 4
SSM_D_STATE = 128
SSM_CHUNK = 128
GDN_HEAD = 128
GDN_CHUNK = 64
CONV_K = 4
LANE_DT, LANE_B, LANE_A = 0, 32, 48
ADAM_LR, ADAM_B1, ADAM_B2, ADAM_EPS, ADAM_WD, ADAM_STEP = 0.001, 0.9, 0.999, 1e-08, 0.01, 10

VMEM_LIMIT_BYTES = 56 * 1024 * 1024
LANES = 128
N_DEV = 8

NN = (((1,), (0,)), ((), ()))
NT = (((1,), (1,)), ((), ()))
TN = (((0,), (0,)), ((), ()))


BNN = (((2,), (1,)), ((0,), (0,)))
BNT = (((2,), (2,)), ((0,), (0,)))
BTN = (((1,), (1,)), ((0,), (0,)))
_KIND = {NN: ("NN", 0), NT: ("NT", 0), TN: ("TN", 0), BNN: ("NN", 1), BNT: ("NT", 1), BTN: ("TN", 1)}
_DIMS = {"NN": (NN, BNN), "NT": (NT, BNT), "TN": (TN, BTN)}


def _dg(a, b, dims):
    return lax.dot_general(a, b, dims, preferred_element_type=F32)


def _raw_bf16(a, b, dims):
    return _dg(a.astype(BF16), b.astype(BF16), dims)


def _raw_bf16x3(a, b, dims):
    ah, bh = a.astype(BF16), b.astype(BF16)
    al, bl = (a - ah.astype(F32)).astype(BF16), (b - bh.astype(F32)).astype(BF16)
    return _dg(ah, bh, dims) + (_dg(ah, bl, dims) + _dg(al, bh, dims))


def _make_dot(raw):
    @functools.partial(jax.custom_vjp, nondiff_argnums=(2,))
    def dot(a, b, dims):
        return raw(a, b, dims)

    def fwd(a, b, dims):
        return raw(a, b, dims), (a, b)

    def bwd(dims, res, ct):
        a, b = res
        kind, batched = _KIND[dims]
        d = lambda k: _DIMS[k][batched]
        if kind == "NN":
            da, db = raw(ct, b, d("NT")), raw(a, ct, d("TN"))
        elif kind == "NT":
            da, db = raw(ct, b, d("NN")), raw(ct, a, d("TN"))
        else:
            da, db = raw(b, ct, d("NT")), raw(a, ct, d("NN"))
        return da.astype(a.dtype), db.astype(b.dtype)

    dot.defvjp(fwd, bwd)
    return lambda a, b, dims=NN: dot(a, b, dims)


_bdot = _make_dot(_raw_bf16)
_hdot = _make_dot(_raw_bf16x3)


def _mask_dot(mask, x, dims, mask_first=True):
    m = mask.astype(BF16)
    hi = x.astype(BF16)
    r = x - hi.astype(F32)
    mid = r.astype(BF16)
    lo = (r - mid.astype(F32)).astype(BF16)
    return sum(_dg(m, p, dims) if mask_first else _dg(p, m, dims) for p in (hi, mid, lo))


def _sigmoid(x):
    return 1.0 / (1.0 + jnp.exp(-x))


def _silu(x):
    return x * _sigmoid(x)


def _softplus(x):
    return jnp.maximum(x, 0.0) + jnp.log(1.0 + jnp.exp(-jnp.abs(x)))


def _rms(x, w):
    return x * lax.rsqrt(jnp.mean(x * x, axis=-1, keepdims=True) + EPS) * w


def _lane_col(m, idx):
    lane = lax.broadcasted_iota(jnp.int32, m.shape, 1)
    return jnp.sum(jnp.where(lane == idx, m, 0.0), axis=1, keepdims=True)


def _tril(n, strict=False):
    r = lax.broadcasted_iota(jnp.int32, (n, n), 0)
    c = lax.broadcasted_iota(jnp.int32, (n, n), 1)
    return (r > c) if strict else (r >= c)


def _first_lane(shape):
    return lax.broadcasted_iota(jnp.int32, shape, len(shape) - 1) == 0


@jax.custom_vjp
def _row_form(col):
    shape = col.shape[:-1] + (LANES,)
    return _mask_dot(_first_lane(shape), jnp.broadcast_to(col, shape), NT if col.ndim == 2 else BNT)


def _row_form_fwd(col):
    return _row_form(col), None


def _row_form_bwd(_, ct):
    shape = ct.shape[:-1] + (LANES,)
    sums = _mask_dot(_first_lane(shape), ct, TN if ct.ndim == 2 else BTN, mask_first=False)
    return (jnp.sum(sums, axis=-1, keepdims=True),)


_row_form.defvjp(_row_form_fwd, _row_form_bwd)


@jax.custom_vjp
def _cumsum_rows(x):
    return _mask_dot(_tril(x.shape[0]), x, NN)


def _cumsum_rows_fwd(x):
    return _cumsum_rows(x), None


def _cumsum_rows_bwd(_, ct):
    return (_mask_dot(_tril(ct.shape[0]), ct, TN),)


_cumsum_rows.defvjp(_cumsum_rows_fwd, _cumsum_rows_bwd)


def _params(sem):
    return pltpu.CompilerParams(dimension_semantics=sem, vmem_limit_bytes=VMEM_LIMIT_BYTES)


def blockmap(name, fn, grid, ins, outs, accs=(), scalars=None):
    n_in, n_out, n_acc = len(ins), len(outs), len(accs)
    n_grid = len(grid)
    n_pre = 0 if scalars is None else 1

    def body(*refs):
        refs = refs[n_pre:]
        vals = fn(*[r[...] for r in refs[:n_in]])
        if not isinstance(vals, (tuple, list)):
            vals = (vals,)
        for r, v in zip(refs[n_in:n_in + n_out], vals[:n_out]):
            r[...] = v.astype(r.dtype)
        if n_acc:
            first = functools.reduce(jnp.logical_and, [pl.program_id(a) == 0 for a in range(n_grid)])
            acc_refs = refs[n_in + n_out:]

            @pl.when(first)
            def _():
                for r in acc_refs:
                    r[...] = jnp.zeros(r.shape, r.dtype)

            for r, v in zip(acc_refs, vals[n_out:]):
                r[...] += v.astype(r.dtype)

    zeros = lambda nd: (lambda *_: (0,) * nd)
    in_specs = [pl.BlockSpec(b, im) for _, b, im in ins]
    out_specs = [pl.BlockSpec(b, im) for _, _, b, im in outs] + [pl.BlockSpec(s, zeros(len(s))) for s, _ in accs]
    out_shape = [jax.ShapeDtypeStruct(s, d) for s, d, _, _ in outs] + [jax.ShapeDtypeStruct(s, d) for s, d in accs]
    cparams = _params(("arbitrary",) * n_grid if n_acc else ("parallel",) * n_grid)
    arrays = [a for a, _, _ in ins]
    if scalars is None:
        return pl.pallas_call(body, name=name, grid=grid, in_specs=in_specs, out_specs=out_specs, out_shape=out_shape,
                              compiler_params=cparams)(*arrays)
    spec = pltpu.PrefetchScalarGridSpec(num_scalar_prefetch=1, grid=grid, in_specs=in_specs, out_specs=out_specs)
    return pl.pallas_call(body, name=name, grid_spec=spec, out_shape=out_shape, compiler_params=cparams)(scalars, *arrays)


def rowmap(name, fn, rows, consts, outs, accs=(), rb=256):
    norm = [(r, r.shape[1], 0) if not isinstance(r, tuple) else (r[0], r[1], r[2] // r[1]) for r in rows]
    assert all(not isinstance(r, tuple) or r[2] % r[1] == 0 for r in rows)
    t = norm[0][0].shape[0]
    rb = min(rb, t)
    ins = [(a, (rb, n), (lambda i, cb=cb: (i, cb))) for a, n, cb in norm]
    ins += [(cst, cst.shape, (lambda i, nd=cst.ndim: (0,) * nd)) for cst in consts]
    o = [((t, n), d, (rb, n), lambda i: (i, 0)) for n, d in outs]
    return blockmap(name, fn, (t // rb,), ins, o, accs)


def matmul(name, a, b, ta=False, tb=False, out_dtypes=(F32,), epi=None, extras=(), a_cols=None, tm=512, tn=512, tk=512):
    if a_cols is not None:
        assert not ta
        k_dim, a_off = a_cols
        m_dim = a.shape[0]
    else:
        (k_dim, m_dim) = a.shape if ta else a.shape[::-1]
        a_off = 0
    n_dim = b.shape[0] if tb else b.shape[1]
    assert (b.shape[1] if tb else b.shape[0]) == k_dim, (name, a.shape, b.shape)
    tm, tn, tk = min(tm, m_dim), min(tn, n_dim), min(tk, k_dim)
    assert m_dim % tm == 0 and n_dim % tn == 0 and k_dim % tk == 0, (name, m_dim, n_dim, k_dim)
    k_steps = k_dim // tk
    n_extra, n_out = len(extras), len(out_dtypes)
    a_dims = 0 if ta else 1
    b_dims = 1 if tb else 0
    dims = (((a_dims,), (b_dims,)), ((), ()))

    def body(*refs):
        a_ref, b_ref = refs[0], refs[1]
        extra_refs = refs[2:2 + n_extra]
        out_refs = refs[2 + n_extra:2 + n_extra + n_out]
        acc_ref = refs[-1]
        k = pl.program_id(2)

        @pl.when(k == 0)
        def _():
            acc_ref[...] = jnp.zeros(acc_ref.shape, F32)

        acc_ref[...] += lax.dot_general(a_ref[...].astype(BF16), b_ref[...].astype(BF16), dims, preferred_element_type=F32)

        @pl.when(k == k_steps - 1)
        def _():
            acc = acc_ref[...]
            vals = (acc,) if epi is None else epi(acc, *[r[...] for r in extra_refs])
            if not isinstance(vals, (tuple, list)):
                vals = (vals,)
            for r, v in zip(out_refs, vals):
                r[...] = v.astype(r.dtype)

    assert a_off % tk == 0
    kb0 = a_off // tk
    a_spec = pl.BlockSpec((tk, tm), lambda i, j, k: (k, i)) if ta else pl.BlockSpec((tm, tk), lambda i, j, k: (i, kb0 + k))
    b_spec = pl.BlockSpec((tn, tk), lambda i, j, k: (j, k)) if tb else pl.BlockSpec((tk, tn), lambda i, j, k: (k, j))
    mn_spec = pl.BlockSpec((tm, tn), lambda i, j, k: (i, j))
    res = pl.pallas_call(
        body, name=name, grid=(m_dim // tm, n_dim // tn, k_steps),
        in_specs=[a_spec, b_spec] + [mn_spec] * n_extra,
        out_specs=[mn_spec] * n_out,
        out_shape=[jax.ShapeDtypeStruct((m_dim, n_dim), d) for d in out_dtypes],
        scratch_shapes=[pltpu.VMEM((tm, tn), F32)],
        compiler_params=_params(("parallel", "parallel", "arbitrary")),
    )(a, b, *extras)
    return res if n_out > 1 else res[0]


def _pair_cols(c0, c1, n):
    lane = lax.broadcasted_iota(jnp.int32, (n, LANES), 1)
    return jnp.where(lane < SSM_HEAD_DIM, c0, c1)


def ssd_step(g0, state, xs, bm, cm, small, z, p_dtb, p_alog, p_dsk, nw):
    wide = 2 * LANES
    res = [_ssd_group(g0 + i, state[i], xs[:, i * wide:(i + 1) * wide], bm[:, i * LANES:(i + 1) * LANES],
                      cm[:, i * LANES:(i + 1) * LANES], small, z[:, i * wide:(i + 1) * wide], p_dtb, p_alog, p_dsk, nw[i])
           for i in range(state.shape[0])]
    return jnp.stack([r[0] for r in res]), jnp.concatenate([r[1] for r in res], axis=1)


def _ssd_group(g, state, xs, bm, cm, small, z, p_dtb, p_alog, p_dsk, nw):
    n = xs.shape[0]
    causal = _tril(n)
    dt_all = _softplus(small + p_dtb)
    a_all = dt_all * (-jnp.exp(p_alog))
    acum_all = _cumsum_rows(a_all)
    cb = _bdot(cm, bm, NT)
    ys, new_states = [], []
    for pair in range(2):
        cols = []
        for h in range(2):
            hh = LANE_DT + SSM_HEADS_PER_GROUP * g + 2 * pair + h
            acum = _lane_col(acum_all, hh)
            dt = _lane_col(dt_all, hh)
            dsk = _lane_col(p_dsk, hh)
            decay = jnp.exp(jnp.where(causal, acum - _row_form(acum), -jnp.inf))
            a_last = acum[n - 1:n, :]
            cols.append((dt, jnp.exp(acum), jnp.exp(a_last - acum), jnp.exp(a_last), dsk, cb * decay))
        xp = xs[:, pair * LANES:(pair + 1) * LANES]
        dt_p = _pair_cols(cols[0][0], cols[1][0], n)
        eac_p = _pair_cols(cols[0][1], cols[1][1], n)
        ws_p = _pair_cols(cols[0][2], cols[1][2], n)
        dsk_p = _pair_cols(cols[0][4], cols[1][4], 1)
        xdt = xp * dt_p
        lane = lax.broadcasted_iota(jnp.int32, (n, LANES), 1)
        y_diag = jnp.where(lane < SSM_HEAD_DIM, _bdot(cols[0][5], xdt), _bdot(cols[1][5], xdt))
        st = state[pair]
        y_off = _bdot(cm, st, NT) * eac_p
        prow = lax.broadcasted_iota(jnp.int32, (LANES, LANES), 0)
        el_p = jnp.where(prow < SSM_HEAD_DIM, cols[0][3], cols[1][3])
        new_states.append(st * el_p + _bdot(xdt * ws_p, bm, TN))
        ys.append(y_diag + y_off + dsk_p * xp)
    y = jnp.concatenate(ys, axis=1) * _silu(z)
    return jnp.stack(new_states), _rms(y, nw)


def _unit_lower_inverse(a):
    n = a.shape[-1]
    eye = (lax.broadcasted_iota(jnp.int32, (n, n), 0) == lax.broadcasted_iota(jnp.int32, (n, n), 1)).astype(F32)
    inv = eye - a
    power = a
    span = 2
    while span < n:
        power = _hdot(power, power, BNN)
        inv = inv + _hdot(inv, power, BNN)
        span *= 2
    return inv


def gdn_step(hq0, state, q, k, v, z, small, p_dtb, p_alog, nw):
    n = q.shape[0]
    nb = 2 * state.shape[0]
    st = state.reshape(nb, LANES, LANES)
    causal, strict = _tril(n), _tril(n, True)
    beta_all = _sigmoid(small)
    g_all = -jnp.exp(p_alog) * _softplus(small + p_dtb)
    gcum_all = _cumsum_rows(g_all)
    split = lambda a: [a[:, i * LANES:(i + 1) * LANES] for i in range(a.shape[1] // LANES)]
    qs, ks = split(q), split(k)
    q2 = jnp.stack([qs[i // 2] for i in range(nb)]) * (GDN_HEAD ** -0.5)
    k2 = jnp.stack([ks[i // 2] for i in range(nb)])
    v2, z2 = jnp.stack(split(v)), jnp.stack(split(z))
    gcum = jnp.stack([_lane_col(gcum_all, LANE_A + 2 * hq0 + i) for i in range(nb)])
    beta = jnp.stack([_lane_col(beta_all, LANE_B + 2 * hq0 + i) for i in range(nb)])
    dmat = jnp.exp(jnp.where(causal, gcum - _row_form(gcum), -jnp.inf))
    a_low = jnp.where(strict, beta * _bdot(k2, k2, BNT) * dmat, 0.0)
    inv = _unit_lower_inverse(a_low)
    egc = jnp.exp(gcum)
    u = _hdot(inv, v2 * beta, BNN)
    w = _hdot(inv, k2 * (beta * egc), BNN)
    v_new = u - _bdot(w, st, BNN)
    o = _bdot(q2 * egc, st, BNN) + _bdot(_bdot(q2, k2, BNT) * dmat, v_new, BNN)
    g_last = gcum[:, n - 1:n, :]
    k_dec = k2 * jnp.exp(g_last - gcum)
    new = st * jnp.exp(g_last) + _bdot(k_dec, v_new, BTN)
    out = _rms(o, nw) * _silu(z2)
    return new.reshape(state.shape), jnp.concatenate([out[i] for i in range(nb)], axis=1)


STATE_SHAPE = (2, LANES, LANES)
SSD_GROUPS_PER_STEP = 1
GDN_HEADS_PER_STEP = 4


def _scan_specs(rows, consts, chunk, chunk_of, hb):
    specs = []
    for _, n, off, per_group in rows:
        if per_group:
            assert off % (n * hb) == 0
            specs.append(pl.BlockSpec((chunk, n * hb), lambda c, g, cb=off // (n * hb): (chunk_of(c), cb + g)))
        else:
            assert off % n == 0
            specs.append(pl.BlockSpec((chunk, n), lambda c, g, cb=off // n: (chunk_of(c), cb)))
    for arr, per_group in consts:
        if per_group:
            specs.append(pl.BlockSpec((hb, 1, arr.shape[2]), lambda c, g: (g, 0, 0)))
        else:
            specs.append(pl.BlockSpec(arr.shape, lambda c, g, nd=arr.ndim: (0,) * nd))
    return specs


def scan_fwd(name, step, chunk, n_grp, rows, consts, out_cols, hb):
    t = rows[0][0].shape[0]
    nc = t // chunk
    n_rows, n_consts = len(rows), len(consts)

    def body(*refs):
        row_refs, const_refs = refs[:n_rows], refs[n_rows:n_rows + n_consts]
        y_ref, st_ref, state = refs[n_rows + n_consts:]
        c, g = pl.program_id(0), pl.program_id(1)

        @pl.when(c == 0)
        def _():
            state[g] = jnp.zeros((hb,) + STATE_SHAPE, F32)

        st = state[g]
        st_ref[...] = st
        new, y = step(g * hb, st, *[r[...] for r in row_refs], *[r[...] for r in const_refs])
        state[g] = new
        y_ref[...] = y.astype(y_ref.dtype)

    return pl.pallas_call(
        body, name=name, grid=(nc, n_grp // hb),
        in_specs=_scan_specs(rows, consts, chunk, lambda c: c, hb),
        out_specs=[pl.BlockSpec((chunk, out_cols * hb), lambda c, g: (c, g)),
                   pl.BlockSpec((None, None, hb) + STATE_SHAPE, lambda c, g: (c, g, 0, 0, 0, 0))],
        out_shape=[jax.ShapeDtypeStruct((t, n_grp * out_cols), BF16),
                   jax.ShapeDtypeStruct((nc, n_grp // hb, hb) + STATE_SHAPE, F32)],
        scratch_shapes=[pltpu.VMEM((n_grp // hb, hb) + STATE_SHAPE, F32)],
        compiler_params=_params(("arbitrary", "arbitrary")),
    )(*[r[0] for r in rows], *[c[0] for c in consts])


def scan_bwd(name, step, chunk, n_grp, rows, consts, states, dy, row_dtypes, hb):
    t = rows[0][0].shape[0]
    nc = t // chunk
    n_rows, n_consts = len(rows), len(consts)
    out_cols = dy.shape[1] // n_grp

    def body(*refs):
        row_refs, const_refs = refs[:n_rows], refs[n_rows:n_rows + n_consts]
        st_ref, dy_ref = refs[n_rows + n_consts:n_rows + n_consts + 2]
        outs = refs[n_rows + n_consts + 2:-1]
        dstate = refs[-1]
        c, g = pl.program_id(0), pl.program_id(1)

        @pl.when(c == 0)
        def _():
            dstate[g] = jnp.zeros((hb,) + STATE_SHAPE, F32)

        @pl.when(jnp.logical_and(c == 0, g == 0))
        def _():
            for r in outs[n_rows:]:
                r[...] = jnp.zeros(r.shape, r.dtype)

        _, vjp = jax.vjp(functools.partial(step, g * hb), st_ref[...], *[r[...] for r in row_refs], *[r[...] for r in const_refs])
        grads = vjp((dstate[g], dy_ref[...].astype(F32)))
        dstate[g] = grads[0]
        for (_, _, _, per_group), r, d in zip(rows, outs[:n_rows], grads[1:1 + n_rows]):
            if per_group:
                r[...] = d.astype(r.dtype)
            else:
                @pl.when(g == 0)
                def _(r=r):
                    r[...] = jnp.zeros(r.shape, r.dtype)

                r[...] += d.astype(r.dtype)
        for (_, per_group), r, d in zip(consts, outs[n_rows:], grads[1 + n_rows:]):
            if per_group:
                r[pl.ds(g * hb, hb)] += d
            else:
                r[...] += d

    rev = lambda c: nc - 1 - c
    out_specs, out_shape = [], []
    for (_, n, _, per_group), dt in zip(rows, row_dtypes):
        if per_group:
            out_specs.append(pl.BlockSpec((chunk, n * hb), lambda c, g: (rev(c), g)))
            out_shape.append(jax.ShapeDtypeStruct((t, n_grp * n), dt))
        else:
            out_specs.append(pl.BlockSpec((chunk, n), lambda c, g: (rev(c), 0)))
            out_shape.append(jax.ShapeDtypeStruct((t, n), dt))
    for arr, _ in consts:
        out_specs.append(pl.BlockSpec(arr.shape, lambda c, g, nd=arr.ndim: (0,) * nd))
        out_shape.append(jax.ShapeDtypeStruct(arr.shape, F32))
    return pl.pallas_call(
        body, name=name, grid=(nc, n_grp // hb),
        in_specs=_scan_specs(rows, consts, chunk, rev, hb)
        + [pl.BlockSpec((None, None, hb) + STATE_SHAPE, lambda c, g: (rev(c), g, 0, 0, 0, 0)),
           pl.BlockSpec((chunk, out_cols * hb), lambda c, g: (rev(c), g))],
        out_specs=out_specs, out_shape=out_shape,
        scratch_shapes=[pltpu.VMEM((n_grp // hb, hb) + STATE_SHAPE, F32)],
        compiler_params=_params(("arbitrary", "arbitrary")),
    )(*[r[0] for r in rows], *[c[0] for c in consts], states, dy)


@functools.partial(jax.custom_vjp, nondiff_argnums=(1,))
def _shift_rows(x, k):
    t = x.shape[0]
    row = lax.broadcasted_iota(jnp.int32, x.shape, 0)
    rolled = pltpu.roll(x, k % t, 0)
    return jnp.where(jnp.logical_and(row >= k, row < t + k), rolled, 0.0)


def _shift_rows_fwd(x, k):
    return _shift_rows(x, k), None


def _shift_rows_bwd(k, _, dy):
    return (_shift_rows(dy, -k),)


_shift_rows.defvjp(_shift_rows_fwd, _shift_rows_bwd)


def _conv_silu(x, cw, cb):
    pre = cb + sum(cw[j:j + 1, :] * _shift_rows(x, CONV_K - 1 - j) for j in range(CONV_K))
    return _silu(pre)


def _conv_silu_l2(x, cw, cb):
    y = _conv_silu(x, cw, cb)
    return y * lax.rsqrt(jnp.sum(y * y, axis=-1, keepdims=True) + EPS)


def conv_fwd(name, fn, src, n, off, cw, cb, cw_off):
    t = src.shape[0]
    sb, wb = off // LANES, cw_off // LANES
    ins = [(src, (t, LANES), lambda i: (0, sb + i)), (cw, (CONV_K, LANES), lambda i: (0, wb + i)),
           (cb, (1, LANES), lambda i: (0, wb + i))]
    return blockmap(name, fn, (n // LANES,), ins, [((t, n), F32, (t, LANES), lambda i: (0, i))])[0]


def conv_bwd(name, fn, src, n, off, cw, cb, cw_off, dy):
    t = src.shape[0]
    sb, wb = off // LANES, cw_off // LANES

    def bwd(x, w, b, d):
        _, vjp = jax.vjp(fn, x, w, b)
        return vjp(d.astype(F32))

    ins = [(src, (t, LANES), lambda i: (0, sb + i)), (cw, (CONV_K, LANES), lambda i: (0, wb + i)),
           (cb, (1, LANES), lambda i: (0, wb + i)), (dy, (t, LANES), lambda i: (0, i))]
    outs = [((t, n), BF16, (t, LANES), lambda i: (0, i)), ((CONV_K, n), F32, (CONV_K, LANES), lambda i: (0, i)),
            ((1, n), F32, (1, LANES), lambda i: (0, i))]
    return blockmap(name, bwd, (n // LANES,), ins, outs)


def _place():
    return lax.axis_index("x"), lax.axis_index("y"), lax.axis_index("c")


def _other_chips(x, y):
    return [(1 - x, y), (x, 1 - y), (1 - x, 1 - y)]


ANY = pl.BlockSpec(memory_space=pl.ANY)


def all_gather8(name, v):
    m_per, n = v.shape

    def body(x_ref, out_ref, send_sems, recv_sems, local_sem):
        x, y, c = _place()
        me, sibling = (x, y, c), (x, y, 1 - c)
        chips = _other_chips(x, y)

        def rows(px, py, pc):
            return out_ref.at[pl.ds((4 * px + 2 * py + pc) * m_per, m_per), :]

        def copy(k, block, to, src=None):
            return pltpu.make_async_remote_copy(
                src_ref=rows(*block) if src is None else src, dst_ref=rows(*block),
                send_sem=send_sems.at[k], recv_sem=recv_sems.at[k], device_id=to, device_id_type=MESH)

        mine = pltpu.make_async_copy(x_ref, rows(*me), local_sem)
        mine.start()
        first = [copy(0, me, sibling, src=x_ref)]
        first += [copy(1 + q, me, (*chip, c), src=x_ref) for q, chip in enumerate(chips)]
        for cp in first:
            cp.start()
        passed = [copy(4 + q, (*chip, c), sibling) for q, chip in enumerate(chips)]
        for q, chip in enumerate(chips):
            copy(1 + q, (*chip, c), me).wait_recv()
            passed[q].start()
        copy(0, sibling, me).wait_recv()
        for q, chip in enumerate(chips):
            copy(4 + q, (*chip, 1 - c), me).wait_recv()
        for cp in first + passed:
            cp.wait_send()
        mine.wait()

    return pl.pallas_call(
        body, name=name, out_shape=jax.ShapeDtypeStruct((N_DEV * m_per, n), v.dtype),
        in_specs=[pl.BlockSpec(memory_space=pltpu.VMEM)], out_specs=pl.BlockSpec(memory_space=pltpu.VMEM),
        scratch_shapes=[pltpu.SemaphoreType.DMA((7,)), pltpu.SemaphoreType.DMA((7,)), pltpu.SemaphoreType.DMA],
    )(v)


def gather_flat(name, vec):
    n = vec.shape[0]
    n_pad = -(-n // (8 * LANES)) * (8 * LANES)
    v = jnp.pad(vec, (0, n_pad - n)).reshape(8, n_pad // 8)
    return all_gather8(name, v).reshape(N_DEV, n_pad)[:, :n]


def all_gather_shards(name, shards):
    n_t = len(shards)

    def body(*refs):
        ins, outs = refs[:n_t], refs[n_t:2 * n_t]
        send_sems, recv_sems, local_sems = refs[2 * n_t:]
        x, y, c = _place()
        sibling = (x, y, 1 - c)
        chips = _other_chips(x, y)
        mine = 2 * x + y

        def half(t, pc):
            h = ins[t].shape[0] // 2
            return pl.ds(pc * h, h)

        def copy(t, k, quarter, pc, to, src=None):
            dst = outs[t].at[quarter, half(t, pc)]
            return pltpu.make_async_remote_copy(
                src_ref=dst if src is None else src, dst_ref=dst,
                send_sem=send_sems.at[t, k], recv_sem=recv_sems.at[t, k], device_id=to, device_id_type=MESH)

        local = [pltpu.make_async_copy(ins[t], outs[t].at[mine], local_sems.at[t]) for t in range(n_t)]
        for cp in local:
            cp.start()
        first = [copy(t, q, mine, c, (*chip, c), src=ins[t].at[half(t, c)]) for t in range(n_t) for q, chip in enumerate(chips)]
        for cp in first:
            cp.start()
        passed = []
        for t in range(n_t):
            for q, (px, py) in enumerate(chips):
                copy(t, q, 2 * px + py, c, (x, y, c)).wait_recv()
                cp = copy(t, 3 + q, 2 * px + py, c, sibling)
                cp.start()
                passed.append(cp)
        for t in range(n_t):
            for q, (px, py) in enumerate(chips):
                copy(t, 3 + q, 2 * px + py, 1 - c, (x, y, c)).wait_recv()
        for cp in first + passed:
            cp.wait_send()
        for cp in local:
            cp.wait()

    return pl.pallas_call(
        body, name=name, out_shape=[jax.ShapeDtypeStruct((4,) + s.shape, s.dtype) for s in shards],
        in_specs=[ANY] * n_t, out_specs=[ANY] * n_t,
        scratch_shapes=[pltpu.SemaphoreType.DMA((n_t, 6)), pltpu.SemaphoreType.DMA((n_t, 6)), pltpu.SemaphoreType.DMA((n_t,))],
    )(*shards)


def exchange_halves_d2d(name, grads):
    n_t = len(grads)

    def body(*refs):
        ins, outs = refs[:n_t], refs[n_t:2 * n_t]
        send_sems, recv_sems = refs[2 * n_t:]
        x, y, c = _place()
        copies = []
        for t in range(n_t):
            h = ins[t].shape[1] // 2
            copies.append(pltpu.make_async_remote_copy(
                src_ref=ins[t].at[:, pl.ds((1 - c) * h, h), :], dst_ref=outs[t],
                send_sem=send_sems.at[t], recv_sem=recv_sems.at[t], device_id=(x, y, 1 - c), device_id_type=MESH))
        for cp in copies:
            cp.start()
        for cp in copies:
            cp.wait()

    return pl.pallas_call(
        body, name=name,
        out_shape=[jax.ShapeDtypeStruct((4, g.shape[1] // 2, g.shape[2]), g.dtype) for g in grads],
        in_specs=[ANY] * n_t, out_specs=[ANY] * n_t,
        scratch_shapes=[pltpu.SemaphoreType.DMA((n_t,)), pltpu.SemaphoreType.DMA((n_t,))],
    )(*grads)


def exchange_quarters_ici(name, parts):
    n_t = len(parts)

    def body(*refs):
        ins, outs = refs[:n_t], refs[n_t:2 * n_t]
        send_sems, recv_sems = refs[2 * n_t:]
        x, y, c = _place()
        copies = []
        for t in range(n_t):
            for q, (px, py) in enumerate(_other_chips(x, y)):
                copies.append(pltpu.make_async_remote_copy(
                    src_ref=ins[t].at[2 * px + py], dst_ref=outs[t].at[q],
                    send_sem=send_sems.at[t, q], recv_sem=recv_sems.at[t, q], device_id=(px, py, c), device_id_type=MESH))
        for cp in copies:
            cp.start()
        for cp in copies:
            cp.wait()

    return pl.pallas_call(
        body, name=name,
        out_shape=[jax.ShapeDtypeStruct((3,) + p.shape[1:], p.dtype) for p in parts],
        in_specs=[ANY] * n_t, out_specs=[ANY] * n_t,
        scratch_shapes=[pltpu.SemaphoreType.DMA((n_t, 3)), pltpu.SemaphoreType.DMA((n_t, 3))],
    )(*parts)


def join_halves_d2d(name, halves):
    n_t = len(halves)

    def body(*refs):
        ins, outs = refs[:n_t], refs[n_t:2 * n_t]
        send_sems, recv_sems, local_sems = refs[2 * n_t:]
        x, y, c = _place()
        local, remote = [], []
        for t in range(n_t):
            h = ins[t].shape[0]
            dst = outs[t].at[pl.ds(c * h, h)]
            local.append(pltpu.make_async_copy(ins[t], dst, local_sems.at[t]))
            remote.append(pltpu.make_async_remote_copy(
                src_ref=ins[t], dst_ref=dst, send_sem=send_sems.at[t], recv_sem=recv_sems.at[t],
                device_id=(x, y, 1 - c), device_id_type=MESH))
        for cp in local + remote:
            cp.start()
        for t in range(n_t):
            h = ins[t].shape[0]
            other = outs[t].at[pl.ds((1 - c) * h, h)]
            pltpu.make_async_remote_copy(
                src_ref=ins[t], dst_ref=other, send_sem=send_sems.at[t], recv_sem=recv_sems.at[t],
                device_id=(x, y, 1 - c), device_id_type=MESH).wait_recv()
        for cp in remote:
            cp.wait_send()
        for cp in local:
            cp.wait()

    return pl.pallas_call(
        body, name=name,
        out_shape=[jax.ShapeDtypeStruct((2 * h.shape[0],) + h.shape[1:], h.dtype) for h in halves],
        in_specs=[ANY] * n_t, out_specs=[ANY] * n_t,
        scratch_shapes=[pltpu.SemaphoreType.DMA((n_t,)), pltpu.SemaphoreType.DMA((n_t,)), pltpu.SemaphoreType.DMA((n_t,))],
    )(*halves)


BLOCK_BYTES = 1 << 20


def _row_block(r, c):
    rb = r
    while rb * c * 4 > BLOCK_BYTES and rb % 16 == 0:
        rb //= 2
    return rb


def reduce_scatter_grads(grads):
    x, y, c = _place()
    place = jnp.stack([c, 2 * x + y]).astype(jnp.int32)
    from_sibling = exchange_halves_d2d("rs_d2d", grads)
    parts = []
    for t, (g, s) in enumerate(zip(grads, from_sibling)):
        _, h, cols = s.shape
        rb = _row_block(h, cols)
        nb = h // rb
        parts.append(blockmap(
            f"rs_add{t}", lambda a, b: a + b, (4, nb),
            [(g, (None, rb, cols), lambda k, i, s_ref, nb=nb: (k, s_ref[0] * nb + i, 0)), (s, (None, rb, cols), lambda k, i, s_ref: (k, i, 0))],
            [(s.shape, F32, (None, rb, cols), lambda k, i, s_ref: (k, i, 0))], scalars=place)[0])
    from_chips = exchange_quarters_ici("rs_ici", parts)
    halves = []
    for t, (p, q) in enumerate(zip(parts, from_chips)):
        _, h, cols = p.shape
        rb = _row_block(h, cols)
        halves.append(blockmap(
            f"rs_sum{t}", lambda a, b: a + b[0] + b[1] + b[2], (h // rb,),
            [(p, (None, rb, cols), lambda i, s_ref: (s_ref[1], i, 0)), (q, (3, rb, cols), lambda i, s_ref: (0, i, 0))],
            [((h, cols), F32, (rb, cols), lambda i, s_ref: (i, 0))], scalars=place)[0])
    return join_halves_d2d("rs_join", halves)


def _adamw(w, g, m, v):
    m = ADAM_B1 * m + (1.0 - ADAM_B1) * g
    v = ADAM_B2 * v + (1.0 - ADAM_B2) * jnp.square(g)
    m_hat = m / (1.0 - ADAM_B1 ** ADAM_STEP)
    v_hat = v / (1.0 - ADAM_B2 ** ADAM_STEP)
    delta = -ADAM_LR * (m_hat / (jnp.sqrt(v_hat) + ADAM_EPS) + ADAM_WD * w)
    return delta, m, v


def adamw(name, w, g, m, v):
    r, c = w.shape
    rb = _row_block(r, c)
    blk = lambda a: (a, (rb, c), lambda i: (i, 0))
    return blockmap(name, _adamw, (r // rb,), [blk(w), blk(g), blk(m), blk(v)], [((r, c), F32, (rb, c), lambda i: (i, 0))] * 3)


def _whole(name, fn, ins, outs):
    return blockmap(name, fn, (1,), [(a, a.shape, lambda i, nd=a.ndim: (0,) * nd) for a in ins],
                    [(s, d, s, lambda i, nd=len(s): (0,) * nd) for s, d in outs])


def _premix(x, w, sc, sh):
    return _rms(x, w) * (1.0 + sc) + sh


def _postmix(x, u, w_post, g1, w_pre2, sc2, sh2):
    x1 = x + g1 * _rms(u, w_post)
    return x1, _premix(x1, w_pre2, sc2, sh2)


def _merge(gs, gg, ys, yg):
    return _sigmoid(gs) * ys + _sigmoid(gg) * yg


def _final(x1, y2, w_post2, g2):
    return x1 + g2 * _rms(y2, w_post2)


def kernel(x, c, w_ada, b_ada, norm_mix_pre, norm_mix_post, w_in, ssm_conv_w, ssm_conv_b, ssm_dt_bias, ssm_A_log, ssm_D, ssm_norm_w, gdn_conv_w, gdn_dt_bias, gdn_A_log, gdn_norm_w, w_ssm_up, w_gdn_up, w_out, norm_mlp_pre, norm_mlp_post, w_mlp_up, w_mlp_down, loss_target, m_w_ada, m_b_ada, m_norm_mix_pre, m_norm_mix_post, m_w_in, m_ssm_conv_w, m_ssm_conv_b, m_ssm_dt_bias, m_ssm_A_log, m_ssm_D, m_ssm_norm_w, m_gdn_conv_w, m_gdn_dt_bias, m_gdn_A_log, m_gdn_norm_w, m_w_ssm_up, m_w_gdn_up, m_w_out, m_norm_mlp_pre, m_norm_mlp_post, m_w_mlp_up, m_w_mlp_down, v_w_ada, v_b_ada, v_norm_mix_pre, v_norm_mix_post, v_w_in, v_ssm_conv_w, v_ssm_conv_b, v_ssm_dt_bias, v_ssm_A_log, v_ssm_D, v_ssm_norm_w, v_gdn_conv_w, v_gdn_dt_bias, v_gdn_A_log, v_gdn_norm_w, v_w_ssm_up, v_w_gdn_up, v_w_out, v_norm_mlp_pre, v_norm_mlp_post, v_w_mlp_up, v_w_mlp_down):
    args = dict(locals())
    xi, yi, ci = _place()
    quarter = 2 * xi + yi
    batch = 4 * xi + 2 * yi + ci

    xt, target = x[0], loss_target[0]
    t, d = xt.shape
    hs, hv = ssm_dt_bias.shape[-1], gdn_dt_bias.shape[-1]
    d_inner = hs * SSM_HEAD_DIM
    n_grp = hs // SSM_HEADS_PER_GROUP
    gn = n_grp * SSM_D_STATE
    conv_ssm = d_inner + 2 * gn
    hq = hv // 2
    key, val = hq * GDN_HEAD, hv * GDN_HEAD
    conv_gdn = 2 * key + val
    hidden = 4 * w_mlp_up.shape[-1]
    o_dt = d_inner + conv_ssm
    o_qkv = o_dt + hs
    o_b = o_qkv + conv_gdn + val
    o_a = o_b + hv
    o_gs = o_a + hv
    n_proj = o_gs + 2 * d
    a_z, a_xs, a_bm, a_cm = 0, d_inner, 2 * d_inner, 2 * d_inner + gn
    a_q = o_dt
    a_k, a_v, a_zg = a_q + key, a_q + 2 * key, a_q + conv_gdn
    a_gs = a_zg + val
    a_gg = a_gs + d
    a_small = a_gg + d
    n_al = -(-(a_small + LANES) // 512) * 512

    def to_aligned(w):
        z = lambda n: jnp.zeros((w.shape[0], n), w.dtype)
        return jnp.concatenate([
            w[:, :o_dt], w[:, o_qkv:o_b], w[:, o_gs:],
            w[:, o_dt:o_qkv], z(LANE_B - hs), w[:, o_b:o_a], z(LANE_A - LANE_B - hv), w[:, o_a:o_gs], z(LANES - LANE_A - hv),
            z(n_al - a_small - LANES)], axis=1)

    def from_aligned(w):
        s = a_small
        return jnp.concatenate([
            w[:, :o_dt], w[:, s + LANE_DT:s + LANE_DT + hs], w[:, a_q:a_gs], w[:, s + LANE_B:s + LANE_B + hv],
            w[:, s + LANE_A:s + LANE_A + hv], w[:, a_gs:a_small]], axis=1)

    def lanes(vec, at):
        return jnp.zeros((1, LANES), F32).at[:, at:at + vec.shape[-1]].set(vec.reshape(1, -1))

    n_cw = CONV_K * ssm_conv_w.shape[-1]
    small_in = gather_flat("ag_small", jnp.concatenate([c.reshape(-1), ssm_conv_w.reshape(-1), gdn_conv_w.reshape(-1)]))
    c_all = small_in[:, :d]
    by_chip = small_in[0::2]

    def whole_conv_w(lo):
        return jnp.transpose(by_chip[:, lo:lo + n_cw].reshape(4, CONV_K, -1), (1, 0, 2)).reshape(CONV_K, -1)

    cw_ssm, cw_gdn = whole_conv_w(d), whole_conv_w(d + n_cw)
    cb_ssm = ssm_conv_b
    cb_gdn = jnp.zeros((1, conv_gdn), F32)

    n_ada = w_ada.shape[-1]
    b_q = lax.dynamic_slice_in_dim(b_ada, quarter * n_ada, n_ada, axis=1)
    mod_q = _whole("ada_fwd", lambda ca, w, b: _bdot(_silu(ca), w) + b, [c_all, w_ada[0], b_q], [((N_DEV, n_ada), F32)])[0]
    mod_all = gather_flat("ag_mod", mod_q.reshape(-1)).reshape(N_DEV, N_DEV, n_ada)[0::2]
    mod = lax.dynamic_index_in_dim(mod_all, batch, axis=1, keepdims=False).reshape(1, 4 * n_ada)
    sh1, sc1, g1, sh2, sc2, g2 = [mod[:, i * d:(i + 1) * d] for i in range(6)]

    gathered = all_gather_shards("ag_w", [w.astype(BF16) for w in (w_in[0], w_ssm_up[0], w_gdn_up[0], w_out[0], w_mlp_up[0], w_mlp_down[0])])
    cols_major = lambda g: jnp.transpose(g, (1, 0, 2)).reshape(g.shape[1], -1)
    rows_major = lambda g: g.reshape(-1, g.shape[2])
    wb_in = to_aligned(cols_major(gathered[0]))
    wb_ssm_up, wb_gdn_up, wb_out = rows_major(gathered[1]), rows_major(gathered[2]), rows_major(gathered[3])
    wb_up, wb_down = cols_major(gathered[4]), rows_major(gathered[5])

    h1 = rowmap("premix", _premix, [xt], [norm_mix_pre, sc1, sh1], [(d, BF16)])[0]
    proj = matmul("in_proj", h1, wb_in)
    xs = conv_fwd("conv_xs", _conv_silu, proj, d_inner, a_xs, cw_ssm, cb_ssm, 0)
    bm = conv_fwd("conv_bm", _conv_silu, proj, gn, a_bm, cw_ssm, cb_ssm, d_inner)
    cm = conv_fwd("conv_cm", _conv_silu, proj, gn, a_cm, cw_ssm, cb_ssm, d_inner + gn)
    q = conv_fwd("conv_q", _conv_silu_l2, proj, key, a_q, cw_gdn, cb_gdn, 0)
    k = conv_fwd("conv_k", _conv_silu_l2, proj, key, a_k, cw_gdn, cb_gdn, key)
    v = conv_fwd("conv_v", _conv_silu, proj, val, a_v, cw_gdn, cb_gdn, 2 * key)

    wide = 2 * LANES
    ssd_rows = [(xs, wide, 0, True), (bm, LANES, 0, True), (cm, LANES, 0, True), (proj, LANES, a_small, False), (proj, wide, a_z, True)]
    ssd_consts = [(lanes(ssm_dt_bias, LANE_DT), False), (lanes(ssm_A_log, LANE_DT), False), (lanes(ssm_D, LANE_DT), False),
                  (ssm_norm_w.reshape(n_grp, 1, wide), True)]
    hb_ssd, hb_gdn = min(SSD_GROUPS_PER_STEP, n_grp), min(GDN_HEADS_PER_STEP, hq)
    y_ssm_n, st_ssm = scan_fwd("ssd_fwd", ssd_step, SSM_CHUNK, n_grp, ssd_rows, ssd_consts, wide, hb_ssd)
    gdn_rows = [(q, LANES, 0, True), (k, LANES, 0, True), (v, wide, 0, True), (proj, wide, a_zg, True), (proj, LANES, a_small, False)]
    gdn_consts = [(lanes(gdn_dt_bias, LANE_A), False), (lanes(gdn_A_log, LANE_A), False), (gdn_norm_w, False)]
    y_gdn_n, st_gdn = scan_fwd("gdn_fwd", gdn_step, GDN_CHUNK, hq, gdn_rows, gdn_consts, wide, hb_gdn)

    y_ssm = matmul("ssm_up", y_ssm_n, wb_ssm_up)
    y_gdn = matmul("gdn_up", y_gdn_n, wb_gdn_up)
    gates = [(proj, d, a_gs), (proj, d, a_gg)]
    merged = rowmap("merge", _merge, gates + [y_ssm, y_gdn], [], [(d, BF16)])[0]
    u = matmul("w_out", merged, wb_out)
    post_consts = [norm_mix_post, g1, norm_mlp_pre, sc2, sh2]
    x1, h2 = rowmap("postmix", _postmix, [xt, u], post_consts, [(d, F32), (d, BF16)])
    relu2 = lambda acc: (acc, jnp.square(jnp.maximum(acc, 0.0)))
    a_up, act = matmul("mlp_up", h2, wb_up, out_dtypes=(BF16, BF16), epi=relu2)
    y2 = matmul("mlp_down", act, wb_down)

    def final_bwd(x1_, y2_, tgt, w_, g_):
        x2, vjp = jax.vjp(_final, x1_, y2_, w_, g_)
        err = x2 - tgt
        loss = 0.5 * jnp.sum(jnp.mean(err * err, axis=-1, keepdims=True), axis=0, keepdims=True)
        dx1, dy2, dw, dg = vjp(err / d)
        return dx1, dy2, loss, dw, dg

    dx1, dy2, loss_part, d_norm_mlp_post, dg2 = rowmap(
        "final", final_bwd, [x1, y2, target], [norm_mlp_post, g2], [(d, F32), (d, BF16)], [((1, 1), F32), ((1, d), F32), ((1, d), F32)])
    loss = lax.psum(loss_part[0, 0], ("x", "y", "c"))

    d_a = matmul("mlp_down_dx", dy2, wb_down, tb=True, out_dtypes=(BF16,), extras=[a_up],
                 epi=lambda acc, a: acc * 2.0 * jnp.maximum(a.astype(F32), 0.0))
    gw_down = matmul("mlp_down_dw", act, dy2, ta=True)
    dh2 = matmul("mlp_up_dx", d_a, wb_up, tb=True)
    gw_up = matmul("mlp_up_dw", h2, d_a, ta=True)

    def postmix_bwd(x_, u_, dx1_, dh2_, *cs):
        _, vjp = jax.vjp(_postmix, x_, u_, *cs)
        return vjp((dx1_, dh2_))

    dxa, du, d_norm_mix_post, dg1, d_norm_mlp_pre, dsc2, dsh2 = rowmap(
        "postmix_bwd", postmix_bwd, [xt, u, dx1, dh2], post_consts, [(d, F32), (d, BF16)], [((1, d), F32)] * 5)
    d_merged = matmul("w_out_dx", du, wb_out, tb=True)
    gw_out = matmul("w_out_dw", merged, du, ta=True)

    def merge_bwd(gs, gg, ys, yg, dm):
        _, vjp = jax.vjp(_merge, gs, gg, ys, yg)
        dgs, dgg, dys, dyg = vjp(dm)
        return dys, dyg, dgs, dgg

    dy_ssm, dy_gdn, dgs, dgg = rowmap("merge_bwd", merge_bwd, gates + [y_ssm, y_gdn, d_merged], [], [(d, BF16)] * 4)
    dy_ssm_n = matmul("ssm_up_dx", dy_ssm, wb_ssm_up, tb=True, out_dtypes=(BF16,))
    gw_ssm_up = matmul("ssm_up_dw", y_ssm_n, dy_ssm, ta=True)
    dy_gdn_n = matmul("gdn_up_dx", dy_gdn, wb_gdn_up, tb=True, out_dtypes=(BF16,))
    gw_gdn_up = matmul("gdn_up_dw", y_gdn_n, dy_gdn, ta=True)

    dxs, dbm, dcm, dsmall_ssm, dz_ssm, d_sdtb, d_salog, d_sdsk, d_snw = scan_bwd(
        "ssd_bwd", ssd_step, SSM_CHUNK, n_grp, ssd_rows, ssd_consts, st_ssm, dy_ssm_n, [BF16, BF16, BF16, F32, BF16], hb_ssd)
    dq, dk, dv, dz_gdn, dsmall_gdn, d_gdtb, d_galog, d_gnw = scan_bwd(
        "gdn_bwd", gdn_step, GDN_CHUNK, hq, gdn_rows, gdn_consts, st_gdn, dy_gdn_n, [BF16, BF16, BF16, BF16, F32], hb_gdn)

    dxs_p, dcw_xs, dcb_xs = conv_bwd("conv_xs_bwd", _conv_silu, proj, d_inner, a_xs, cw_ssm, cb_ssm, 0, dxs)
    dbm_p, dcw_bm, dcb_bm = conv_bwd("conv_bm_bwd", _conv_silu, proj, gn, a_bm, cw_ssm, cb_ssm, d_inner, dbm)
    dcm_p, dcw_cm, dcb_cm = conv_bwd("conv_cm_bwd", _conv_silu, proj, gn, a_cm, cw_ssm, cb_ssm, d_inner + gn, dcm)
    dq_p, dcw_q, _ = conv_bwd("conv_q_bwd", _conv_silu_l2, proj, key, a_q, cw_gdn, cb_gdn, 0, dq)
    dk_p, dcw_k, _ = conv_bwd("conv_k_bwd", _conv_silu_l2, proj, key, a_k, cw_gdn, cb_gdn, key, dk)
    dv_p, dcw_v, _ = conv_bwd("conv_v_bwd", _conv_silu, proj, val, a_v, cw_gdn, cb_gdn, 2 * key, dv)
    dsmall = rowmap("small_sum", lambda a, b: a + b, [dsmall_ssm, dsmall_gdn], [], [(LANES, BF16)])[0]
    dproj = jnp.concatenate([dz_ssm, dxs_p, dbm_p, dcm_p, dq_p, dk_p, dv_p, dz_gdn, dgs, dgg, dsmall,
                             jnp.zeros((t, n_al - a_small - LANES), BF16)], axis=1)
    dh1 = matmul("in_proj_dx", dproj, wb_in, tb=True)
    gw_in_al = matmul("in_proj_dw", h1, dproj, ta=True)

    def premix_bwd(x_, dxa_, dh1_, w_, sc_, sh_):
        _, vjp = jax.vjp(_premix, x_, w_, sc_, sh_)
        dx, dw, dsc, dsh = vjp(dh1_)
        return dx + dxa_, dw, dsc, dsh

    grad_x, d_norm_mix_pre, dsc1, dsh1 = rowmap(
        "premix_bwd", premix_bwd, [xt, dxa, dh1], [norm_mix_pre, sc1, sh1], [(d, F32)], [((1, d), F32)] * 3)

    dmod_all = gather_flat("ag_dmod", jnp.concatenate([dsh1, dsc1, dg1, dsh2, dsc2, dg2], axis=1).reshape(-1))
    dmod_q = lax.dynamic_slice_in_dim(dmod_all, quarter * n_ada, n_ada, axis=1)
    gw_ada, gb_ada = _whole(
        "ada_bwd", lambda ca, dq_, da_: (_bdot(_silu(ca), dq_, TN), jnp.sum(da_, axis=0, keepdims=True)),
        [c_all, dmod_q, dmod_all], [((d, n_ada), F32), ((1, 4 * n_ada), F32)])

    dcw_ssm = jnp.concatenate([dcw_xs, dcw_bm, dcw_cm], axis=1)
    dcb_ssm = jnp.concatenate([dcb_xs, dcb_bm, dcb_cm], axis=1)
    dcw_gdn = jnp.concatenate([dcw_q, dcw_k, dcw_v], axis=1)
    partial = [d_norm_mix_pre, d_norm_mix_post, dcw_ssm, dcb_ssm, d_sdtb[:, LANE_DT:LANE_DT + hs], d_salog[:, LANE_DT:LANE_DT + hs],
               d_sdsk[:, LANE_DT:LANE_DT + hs], d_snw, dcw_gdn, d_gdtb[:, LANE_A:LANE_A + hv], d_galog[:, LANE_A:LANE_A + hv], d_gnw,
               d_norm_mlp_pre, d_norm_mlp_post]
    sizes = [p.size for p in partial]
    stacked = gather_flat("ag_grads", jnp.concatenate([p.reshape(-1) for p in partial]))
    summed = _whole("small_sum8", lambda s: jnp.sum(s, axis=0, keepdims=True), [stacked], [((1, stacked.shape[1]), F32)])[0][0]
    offs = [0]
    for s in sizes:
        offs.append(offs[-1] + s)
    red = [summed[offs[i]:offs[i + 1]] for i in range(len(sizes))]
    my_cols = lambda full: lax.dynamic_slice_in_dim(full.reshape(CONV_K, -1), quarter * (n_cw // CONV_K), n_cw // CONV_K, axis=1)
    small_grads = {
        "b_ada": gb_ada, "norm_mix_pre": red[0], "norm_mix_post": red[1], "ssm_conv_w": my_cols(red[2]), "ssm_conv_b": red[3],
        "ssm_dt_bias": red[4], "ssm_A_log": red[5], "ssm_D": red[6], "ssm_norm_w": red[7], "gdn_conv_w": my_cols(red[8]),
        "gdn_dt_bias": red[9], "gdn_A_log": red[10], "gdn_norm_w": red[11], "norm_mlp_pre": red[12], "norm_mlp_post": red[13]}

    quarters_cols = lambda g: jnp.transpose(g.reshape(g.shape[0], 4, -1), (1, 0, 2))
    quarters_rows = lambda g: g.reshape(4, g.shape[0] // 4, g.shape[1])
    big_names = ["w_in", "w_ssm_up", "w_gdn_up", "w_out", "w_mlp_up", "w_mlp_down"]
    big_partial = [quarters_cols(from_aligned(gw_in_al)), quarters_rows(gw_ssm_up), quarters_rows(gw_gdn_up), quarters_rows(gw_out),
                   quarters_cols(gw_up), quarters_rows(gw_down)]
    big_grads = dict(zip(big_names, reduce_scatter_grads(big_partial)))
    big_grads["w_ada"] = gw_ada

    names = ['w_ada', 'b_ada', 'norm_mix_pre', 'norm_mix_post', 'w_in', 'ssm_conv_w', 'ssm_conv_b', 'ssm_dt_bias', 'ssm_A_log', 'ssm_D',
             'ssm_norm_w', 'gdn_conv_w', 'gdn_dt_bias', 'gdn_A_log', 'gdn_norm_w', 'w_ssm_up', 'w_gdn_up', 'w_out', 'norm_mlp_pre',
             'norm_mlp_post', 'w_mlp_up', 'w_mlp_down']
    grad, delta, new_m, new_v = {}, {}, {}, {}
    for n in big_grads:
        shape = args[n].shape
        g2d = big_grads[n]
        dl, nm, nv = adamw("adamw_" + n, args[n][0], g2d, args["m_" + n][0], args["v_" + n][0])
        grad[n], delta[n], new_m[n], new_v[n] = [a.reshape(shape) for a in (g2d, dl, nm, nv)]
    small_names = [n for n in names if n not in big_grads]
    flat = lambda pre: jnp.concatenate([args[pre + n].reshape(-1) for n in small_names]).reshape(1, -1)
    g_flat = jnp.concatenate([small_grads[n].reshape(-1) for n in small_names]).reshape(1, -1)
    dl, nm, nv = adamw("adamw_small", flat(""), g_flat, flat("m_"), flat("v_"))
    off = 0
    for n in small_names:
        shape = args[n].shape
        size = args[n].size
        grad[n], delta[n], new_m[n], new_v[n] = [a[0, off:off + size].reshape(shape) for a in (g_flat, dl, nm, nv)]
        off += size

    return (loss, grad_x.reshape(x.shape), *[grad[n] for n in names], *[delta[n] for n in names],
            *[new_m[n] for n in names], *[new_v[n] for n in names])
```

```python
import functools

import jax
import jax.numpy as jnp
from jax import lax
from jax.experimental import pallas as pl
from jax.experimental.pallas import tpu as pltpu

F32 = jnp.float32
BF16 = jnp.bfloat16
MESH = pl.DeviceIdType.MESH

EPS = 1e-6
SSM_HEAD_DIM = 64
SSM_HEADS_PER_GROUP = 4
SSM_D_STATE = 128
SSM_CHUNK = 128
GDN_HEAD = 128
GDN_CHUNK = 64
CONV_K = 4
LANE_DT, LANE_B, LANE_A = 0, 32, 48
ADAM_LR, ADAM_B1, ADAM_B2, ADAM_EPS, ADAM_WD, ADAM_STEP = 0.001, 0.9, 0.999, 1e-08, 0.01, 10

VMEM_LIMIT_BYTES = 56 * 1024 * 1024
LANES = 128
N_DEV = 8

NN = (((1,), (0,)), ((), ()))
NT = (((1,), (1,)), ((), ()))
TN = (((0,), (0,)), ((), ()))


BNN = (((2,), (1,)), ((0,), (0,)))
BNT = (((2,), (2,)), ((0,), (0,)))
BTN = (((1,), (1,)), ((0,), (0,)))
_KIND = {NN: ("NN", 0), NT: ("NT", 0), TN: ("TN", 0), BNN: ("NN", 1), BNT: ("NT", 1), BTN: ("TN", 1)}
_DIMS = {"NN": (NN, BNN), "NT": (NT, BNT), "TN": (TN, BTN)}


def _dg(a, b, dims):
    return lax.dot_general(a, b, dims, preferred_element_type=F32)


def _raw_bf16(a, b, dims):
    return _dg(a.astype(BF16), b.astype(BF16), dims)


def _raw_bf16x3(a, b, dims):
    ah, bh = a.astype(BF16), b.astype(BF16)
    al, bl = (a - ah.astype(F32)).astype(BF16), (b - bh.astype(F32)).astype(BF16)
    return _dg(ah, bh, dims) + (_dg(ah, bl, dims) + _dg(al, bh, dims))


def _make_dot(raw):
    @functools.partial(jax.custom_vjp, nondiff_argnums=(2,))
    def dot(a, b, dims):
        return raw(a, b, dims)

    def fwd(a, b, dims):
        return raw(a, b, dims), (a, b)

    def bwd(dims, res, ct):
        a, b = res
        kind, batched = _KIND[dims]
        d = lambda k: _DIMS[k][batched]
        if kind == "NN":
            da, db = raw(ct, b, d("NT")), raw(a, ct, d("TN"))
        elif kind == "NT":
            da, db = raw(ct, b, d("NN")), raw(ct, a, d("TN"))
        else:
            da, db = raw(b, ct, d("NT")), raw(a, ct, d("NN"))
        return da.astype(a.dtype), db.astype(b.dtype)

    dot.defvjp(fwd, bwd)
    return lambda a, b, dims=NN: dot(a, b, dims)


_bdot = _make_dot(_raw_bf16)
_hdot = _make_dot(_raw_bf16x3)


def _mask_dot(mask, x, dims, mask_first=True):
    m = mask.astype(BF16)
    hi = x.astype(BF16)
    r = x - hi.astype(F32)
    mid = r.astype(BF16)
    lo = (r - mid.astype(F32)).astype(BF16)
    return sum(_dg(m, p, dims) if mask_first else _dg(p, m, dims) for p in (hi, mid, lo))


def _sigmoid(x):
    return 1.0 / (1.0 + jnp.exp(-x))


def _silu(x):
    return x * _sigmoid(x)


def _softplus(x):
    return jnp.maximum(x, 0.0) + jnp.log(1.0 + jnp.exp(-jnp.abs(x)))


def _rms(x, w):
    return x * lax.rsqrt(jnp.mean(x * x, axis=-1, keepdims=True) + EPS) * w


def _lane_col(m, idx):
    lane = lax.broadcasted_iota(jnp.int32, m.shape, 1)
    return jnp.sum(jnp.where(lane == idx, m, 0.0), axis=1, keepdims=True)


def _tril(n, strict=False):
    r = lax.broadcasted_iota(jnp.int32, (n, n), 0)
    c = lax.broadcasted_iota(jnp.int32, (n, n), 1)
    return (r > c) if strict else (r >= c)


def _first_lane(shape):
    return lax.broadcasted_iota(jnp.int32, shape, len(shape) - 1) == 0


@jax.custom_vjp
def _row_form(col):
    shape = col.shape[:-1] + (LANES,)
    return _mask_dot(_first_lane(shape), jnp.broadcast_to(col, shape), NT if col.ndim == 2 else BNT)


def _row_form_fwd(col):
    return _row_form(col), None


def _row_form_bwd(_, ct):
    shape = ct.shape[:-1] + (LANES,)
    sums = _mask_dot(_first_lane(shape), ct, TN if ct.ndim == 2 else BTN, mask_first=False)
    return (jnp.sum(sums, axis=-1, keepdims=True),)


_row_form.defvjp(_row_form_fwd, _row_form_bwd)


@jax.custom_vjp
def _cumsum_rows(x):
    return _mask_dot(_tril(x.shape[0]), x, NN)


def _cumsum_rows_fwd(x):
    return _cumsum_rows(x), None


def _cumsum_rows_bwd(_, ct):
    return (_mask_dot(_tril(ct.shape[0]), ct, TN),)


_cumsum_rows.defvjp(_cumsum_rows_fwd, _cumsum_rows_bwd)


def _params(sem):
    return pltpu.CompilerParams(dimension_semantics=sem, vmem_limit_bytes=VMEM_LIMIT_BYTES)


def blockmap(name, fn, grid, ins, outs, accs=(), scalars=None):
    n_in, n_out, n_acc = len(ins), len(outs), len(accs)
    n_grid = len(grid)
    n_pre = 0 if scalars is None else 1

    def body(*refs):
        refs = refs[n_pre:]
        vals = fn(*[r[...] for r in refs[:n_in]])
        if not isinstance(vals, (tuple, list)):
            vals = (vals,)
        for r, v in zip(refs[n_in:n_in + n_out], vals[:n_out]):
            r[...] = v.astype(r.dtype)
        if n_acc:
            first = functools.reduce(jnp.logical_and, [pl.program_id(a) == 0 for a in range(n_grid)])
            acc_refs = refs[n_in + n_out:]

            @pl.when(first)
            def _():
                for r in acc_refs:
                    r[...] = jnp.zeros(r.shape, r.dtype)

            for r, v in zip(acc_refs, vals[n_out:]):
                r[...] += v.astype(r.dtype)

    zeros = lambda nd: (lambda *_: (0,) * nd)
    in_specs = [pl.BlockSpec(b, im) for _, b, im in ins]
    out_specs = [pl.BlockSpec(b, im) for _, _, b, im in outs] + [pl.BlockSpec(s, zeros(len(s))) for s, _ in accs]
    out_shape = [jax.ShapeDtypeStruct(s, d) for s, d, _, _ in outs] + [jax.ShapeDtypeStruct(s, d) for s, d in accs]
    cparams = _params(("arbitrary",) * n_grid if n_acc else ("parallel",) * n_grid)
    arrays = [a for a, _, _ in ins]
    if scalars is None:
        return pl.pallas_call(body, name=name, grid=grid, in_specs=in_specs, out_specs=out_specs, out_shape=out_shape,
                              compiler_params=cparams)(*arrays)
    spec = pltpu.PrefetchScalarGridSpec(num_scalar_prefetch=1, grid=grid, in_specs=in_specs, out_specs=out_specs)
    return pl.pallas_call(body, name=name, grid_spec=spec, out_shape=out_shape, compiler_params=cparams)(scalars, *arrays)


def rowmap(name, fn, rows, consts, outs, accs=(), rb=256):
    norm = [(r, r.shape[1], 0) if not isinstance(r, tuple) else (r[0], r[1], r[2] // r[1]) for r in rows]
    assert all(not isinstance(r, tuple) or r[2] % r[1] == 0 for r in rows)
    t = norm[0][0].shape[0]
    rb = min(rb, t)
    ins = [(a, (rb, n), (lambda i, cb=cb: (i, cb))) for a, n, cb in norm]
    ins += [(cst, cst.shape, (lambda i, nd=cst.ndim: (0,) * nd)) for cst in consts]
    o = [((t, n), d, (rb, n), lambda i: (i, 0)) for n, d in outs]
    return blockmap(name, fn, (t // rb,), ins, o, accs)


MM_TILE_M, MM_TILE_N, MM_TILE_K = 1024, 1024, 2048


def _tile(dim, cap):
    if dim <= cap:
        return dim
    best = max(t for t in range(LANES, cap + 1, LANES) if dim % t == 0)
    return best


def matmul(name, a, b, ta=False, tb=False, out_dtypes=(F32,), epi=None, extras=()):
    (k_dim, m_dim) = a.shape if ta else a.shape[::-1]
    n_dim = b.shape[0] if tb else b.shape[1]
    assert (b.shape[1] if tb else b.shape[0]) == k_dim, (name, a.shape, b.shape)
    tm, tn, tk = _tile(m_dim, MM_TILE_M), _tile(n_dim, MM_TILE_N), _tile(k_dim, MM_TILE_K)
    k_steps = k_dim // tk
    n_extra, n_out = len(extras), len(out_dtypes)
    dims = (((0 if ta else 1,), (1 if tb else 0,)), ((), ()))

    def finish(acc, extra_refs, out_refs):
        vals = (acc,) if epi is None else epi(acc, *[r[...] for r in extra_refs])
        if not isinstance(vals, (tuple, list)):
            vals = (vals,)
        for r, v in zip(out_refs, vals):
            r[...] = v.astype(r.dtype)

    def body_one(*refs):
        acc = lax.dot_general(refs[0][...].astype(BF16), refs[1][...].astype(BF16), dims, preferred_element_type=F32)
        finish(acc, refs[2:2 + n_extra], refs[2 + n_extra:2 + n_extra + n_out])

    def body_loop(*refs):
        acc_ref = refs[-1]
        k = pl.program_id(2)

        @pl.when(k == 0)
        def _():
            acc_ref[...] = jnp.zeros(acc_ref.shape, F32)

        acc_ref[...] += lax.dot_general(refs[0][...].astype(BF16), refs[1][...].astype(BF16), dims, preferred_element_type=F32)

        @pl.when(k == k_steps - 1)
        def _():
            finish(acc_ref[...], refs[2:2 + n_extra], refs[2 + n_extra:2 + n_extra + n_out])

    a_spec = pl.BlockSpec((tk, tm), lambda i, j, k: (k, i)) if ta else pl.BlockSpec((tm, tk), lambda i, j, k: (i, k))
    b_spec = pl.BlockSpec((tn, tk), lambda i, j, k: (j, k)) if tb else pl.BlockSpec((tk, tn), lambda i, j, k: (k, j))
    mn_spec = pl.BlockSpec((tm, tn), lambda i, j, k: (i, j))
    res = pl.pallas_call(
        body_one if k_steps == 1 else body_loop, name=name, grid=(m_dim // tm, n_dim // tn, k_steps),
        in_specs=[a_spec, b_spec] + [mn_spec] * n_extra,
        out_specs=[mn_spec] * n_out,
        out_shape=[jax.ShapeDtypeStruct((m_dim, n_dim), d) for d in out_dtypes],
        scratch_shapes=[] if k_steps == 1 else [pltpu.VMEM((tm, tn), F32)],
        compiler_params=_params(("parallel", "parallel", "arbitrary")),
    )(a, b, *extras)
    return res if n_out > 1 else res[0]


def _pair_cols(c0, c1, n):
    lane = lax.broadcasted_iota(jnp.int32, (n, LANES), 1)
    return jnp.where(lane < SSM_HEAD_DIM, c0, c1)


def ssd_step(g0, state, xs, bm, cm, small, z, p_dtb, p_alog, p_dsk, nw):
    wide = 2 * LANES
    res = [_ssd_group(g0 + i, state[i], xs[:, i * wide:(i + 1) * wide], bm[:, i * LANES:(i + 1) * LANES],
                      cm[:, i * LANES:(i + 1) * LANES], small, z[:, i * wide:(i + 1) * wide], p_dtb, p_alog, p_dsk, nw[i])
           for i in range(state.shape[0])]
    return jnp.stack([r[0] for r in res]), jnp.concatenate([r[1] for r in res], axis=1)


def _ssd_group(g, state, xs, bm, cm, small, z, p_dtb, p_alog, p_dsk, nw):
    n = xs.shape[0]
    causal = _tril(n)
    dt_all = _softplus(small + p_dtb)
    a_all = dt_all * (-jnp.exp(p_alog))
    acum_all = _cumsum_rows(a_all)
    cb = _bdot(cm, bm, NT)
    ys, new_states = [], []
    for pair in range(2):
        cols = []
        for h in range(2):
            hh = LANE_DT + SSM_HEADS_PER_GROUP * g + 2 * pair + h
            acum = _lane_col(acum_all, hh)
            dt = _lane_col(dt_all, hh)
            dsk = _lane_col(p_dsk, hh)
            decay = jnp.exp(jnp.where(causal, acum - _row_form(acum), -jnp.inf))
            a_last = acum[n - 1:n, :]
            cols.append((dt, jnp.exp(acum), jnp.exp(a_last - acum), jnp.exp(a_last), dsk, cb * decay))
        xp = xs[:, pair * LANES:(pair + 1) * LANES]
        dt_p = _pair_cols(cols[0][0], cols[1][0], n)
        eac_p = _pair_cols(cols[0][1], cols[1][1], n)
        ws_p = _pair_cols(cols[0][2], cols[1][2], n)
        dsk_p = _pair_cols(cols[0][4], cols[1][4], 1)
        xdt = xp * dt_p
        lane = lax.broadcasted_iota(jnp.int32, (n, LANES), 1)
        y_diag = jnp.where(lane < SSM_HEAD_DIM, _bdot(cols[0][5], xdt), _bdot(cols[1][5], xdt))
        st = state[pair]
        y_off = _bdot(cm, st, NT) * eac_p
        prow = lax.broadcasted_iota(jnp.int32, (LANES, LANES), 0)
        el_p = jnp.where(prow < SSM_HEAD_DIM, cols[0][3], cols[1][3])
        new_states.append(st * el_p + _bdot(xdt * ws_p, bm, TN))
        ys.append(y_diag + y_off + dsk_p * xp)
    y = jnp.concatenate(ys, axis=1) * _silu(z)
    return jnp.stack(new_states), _rms(y, nw)


def _unit_lower_inverse(a):
    n = a.shape[-1]
    eye = (lax.broadcasted_iota(jnp.int32, (n, n), 0) == lax.broadcasted_iota(jnp.int32, (n, n), 1)).astype(F32)
    inv = eye - a
    power = a
    span = 2
    while span < n:
        power = _hdot(power, power, BNN)
        inv = inv + _hdot(inv, power, BNN)
        span *= 2
    return inv


def gdn_step(hq0, state, q, k, v, z, small, p_dtb, p_alog, nw):
    n = q.shape[0]
    nb = 2 * state.shape[0]
    st = state.reshape(nb, LANES, LANES)
    causal, strict = _tril(n), _tril(n, True)
    beta_all = _sigmoid(small)
    g_all = -jnp.exp(p_alog) * _softplus(small + p_dtb)
    gcum_all = _cumsum_rows(g_all)
    split = lambda a: [a[:, i * LANES:(i + 1) * LANES] for i in range(a.shape[1] // LANES)]
    qs, ks = split(q), split(k)
    q2 = jnp.stack([qs[i // 2] for i in range(nb)]) * (GDN_HEAD ** -0.5)
    k2 = jnp.stack([ks[i // 2] for i in range(nb)])
    v2, z2 = jnp.stack(split(v)), jnp.stack(split(z))
    gcum = jnp.stack([_lane_col(gcum_all, LANE_A + 2 * hq0 + i) for i in range(nb)])
    beta = jnp.stack([_lane_col(beta_all, LANE_B + 2 * hq0 + i) for i in range(nb)])
    dmat = jnp.exp(jnp.where(causal, gcum - _row_form(gcum), -jnp.inf))
    a_low = jnp.where(strict, beta * _bdot(k2, k2, BNT) * dmat, 0.0)
    inv = _unit_lower_inverse(a_low)
    egc = jnp.exp(gcum)
    u = _hdot(inv, v2 * beta, BNN)
    w = _hdot(inv, k2 * (beta * egc), BNN)
    v_new = u - _bdot(w, st, BNN)
    o = _bdot(q2 * egc, st, BNN) + _bdot(_bdot(q2, k2, BNT) * dmat, v_new, BNN)
    g_last = gcum[:, n - 1:n, :]
    k_dec = k2 * jnp.exp(g_last - gcum)
    new = st * jnp.exp(g_last) + _bdot(k_dec, v_new, BTN)
    out = _rms(o, nw) * _silu(z2)
    return new.reshape(state.shape), jnp.concatenate([out[i] for i in range(nb)], axis=1)


STATE_SHAPE = (2, LANES, LANES)
SSD_GROUPS_PER_STEP = 1
GDN_HEADS_PER_STEP = 4


def _scan_specs(rows, consts, chunk, chunk_of, hb):
    specs = []
    for _, n, off, per_group in rows:
        if per_group:
            assert off % (n * hb) == 0
            specs.append(pl.BlockSpec((chunk, n * hb), lambda c, g, cb=off // (n * hb): (chunk_of(c), cb + g)))
        else:
            assert off % n == 0
            specs.append(pl.BlockSpec((chunk, n), lambda c, g, cb=off // n: (chunk_of(c), cb)))
    for arr, per_group in consts:
        if per_group:
            specs.append(pl.BlockSpec((hb, 1, arr.shape[2]), lambda c, g: (g, 0, 0)))
        else:
            specs.append(pl.BlockSpec(arr.shape, lambda c, g, nd=arr.ndim: (0,) * nd))
    return specs


def scan_fwd(name, step, chunk, n_grp, rows, consts, out_cols, hb):
    t = rows[0][0].shape[0]
    nc = t // chunk
    n_rows, n_consts = len(rows), len(consts)

    def body(*refs):
        row_refs, const_refs = refs[:n_rows], refs[n_rows:n_rows + n_consts]
        y_ref, st_ref, state = refs[n_rows + n_consts:]
        c, g = pl.program_id(0), pl.program_id(1)

        @pl.when(c == 0)
        def _():
            state[g] = jnp.zeros((hb,) + STATE_SHAPE, F32)

        st = state[g]
        st_ref[...] = st
        new, y = step(g * hb, st, *[r[...] for r in row_refs], *[r[...] for r in const_refs])
        state[g] = new
        y_ref[...] = y.astype(y_ref.dtype)

    return pl.pallas_call(
        body, name=name, grid=(nc, n_grp // hb),
        in_specs=_scan_specs(rows, consts, chunk, lambda c: c, hb),
        out_specs=[pl.BlockSpec((chunk, out_cols * hb), lambda c, g: (c, g)),
                   pl.BlockSpec((None, None, hb) + STATE_SHAPE, lambda c, g: (c, g, 0, 0, 0, 0))],
        out_shape=[jax.ShapeDtypeStruct((t, n_grp * out_cols), BF16),
                   jax.ShapeDtypeStruct((nc, n_grp // hb, hb) + STATE_SHAPE, F32)],
        scratch_shapes=[pltpu.VMEM((n_grp // hb, hb) + STATE_SHAPE, F32)],
        compiler_params=_params(("arbitrary", "arbitrary")),
    )(*[r[0] for r in rows], *[c[0] for c in consts])


def scan_bwd(name, step, chunk, n_grp, rows, consts, states, dy, row_dtypes, hb):
    t = rows[0][0].shape[0]
    nc = t // chunk
    n_rows, n_consts = len(rows), len(consts)
    out_cols = dy.shape[1] // n_grp

    def body(*refs):
        row_refs, const_refs = refs[:n_rows], refs[n_rows:n_rows + n_consts]
        st_ref, dy_ref = refs[n_rows + n_consts:n_rows + n_consts + 2]
        outs = refs[n_rows + n_consts + 2:-1]
        dstate = refs[-1]
        c, g = pl.program_id(0), pl.program_id(1)

        @pl.when(c == 0)
        def _():
            dstate[g] = jnp.zeros((hb,) + STATE_SHAPE, F32)

        @pl.when(jnp.logical_and(c == 0, g == 0))
        def _():
            for r in outs[n_rows:]:
                r[...] = jnp.zeros(r.shape, r.dtype)

        _, vjp = jax.vjp(functools.partial(step, g * hb), st_ref[...], *[r[...] for r in row_refs], *[r[...] for r in const_refs])
        grads = vjp((dstate[g], dy_ref[...].astype(F32)))
        dstate[g] = grads[0]
        for (_, _, _, per_group), r, d in zip(rows, outs[:n_rows], grads[1:1 + n_rows]):
            if per_group:
                r[...] = d.astype(r.dtype)
            else:
                @pl.when(g == 0)
                def _(r=r):
                    r[...] = jnp.zeros(r.shape, r.dtype)

                r[...] += d.astype(r.dtype)
        for (_, per_group), r, d in zip(consts, outs[n_rows:], grads[1 + n_rows:]):
            if per_group:
                r[pl.ds(g * hb, hb)] += d
            else:
                r[...] += d

    rev = lambda c: nc - 1 - c
    out_specs, out_shape = [], []
    for (_, n, _, per_group), dt in zip(rows, row_dtypes):
        if per_group:
            out_specs.append(pl.BlockSpec((chunk, n * hb), lambda c, g: (rev(c), g)))
            out_shape.append(jax.ShapeDtypeStruct((t, n_grp * n), dt))
        else:
            out_specs.append(pl.BlockSpec((chunk, n), lambda c, g: (rev(c), 0)))
            out_shape.append(jax.ShapeDtypeStruct((t, n), dt))
    for arr, _ in consts:
        out_specs.append(pl.BlockSpec(arr.shape, lambda c, g, nd=arr.ndim: (0,) * nd))
        out_shape.append(jax.ShapeDtypeStruct(arr.shape, F32))
    return pl.pallas_call(
        body, name=name, grid=(nc, n_grp // hb),
        in_specs=_scan_specs(rows, consts, chunk, rev, hb)
        + [pl.BlockSpec((None, None, hb) + STATE_SHAPE, lambda c, g: (rev(c), g, 0, 0, 0, 0)),
           pl.BlockSpec((chunk, out_cols * hb), lambda c, g: (rev(c), g))],
        out_specs=out_specs, out_shape=out_shape,
        scratch_shapes=[pltpu.VMEM((n_grp // hb, hb) + STATE_SHAPE, F32)],
        compiler_params=_params(("arbitrary", "arbitrary")),
    )(*[r[0] for r in rows], *[c[0] for c in consts], states, dy)


@functools.partial(jax.custom_vjp, nondiff_argnums=(1,))
def _shift_rows(x, k):
    t = x.shape[0]
    row = lax.broadcasted_iota(jnp.int32, x.shape, 0)
    rolled = pltpu.roll(x, k % t, 0)
    return jnp.where(jnp.logical_and(row >= k, row < t + k), rolled, 0.0)


def _shift_rows_fwd(x, k):
    return _shift_rows(x, k), None


def _shift_rows_bwd(k, _, dy):
    return (_shift_rows(dy, -k),)


_shift_rows.defvjp(_shift_rows_fwd, _shift_rows_bwd)


def _conv_silu(x, cw, cb):
    pre = cb + sum(cw[j:j + 1, :] * _shift_rows(x, CONV_K - 1 - j) for j in range(CONV_K))
    return _silu(pre)


def _conv_silu_l2(x, cw, cb):
    y = _conv_silu(x, cw, cb)
    return y * lax.rsqrt(jnp.sum(y * y, axis=-1, keepdims=True) + EPS)


def conv_fwd(name, fn, src, n, off, cw, cb, cw_off):
    t = src.shape[0]
    sb, wb = off // LANES, cw_off // LANES
    ins = [(src, (t, LANES), lambda i: (0, sb + i)), (cw, (CONV_K, LANES), lambda i: (0, wb + i)),
           (cb, (1, LANES), lambda i: (0, wb + i))]
    return blockmap(name, fn, (n // LANES,), ins, [((t, n), F32, (t, LANES), lambda i: (0, i))])[0]


def conv_bwd(name, fn, src, n, off, cw, cb, cw_off, dy):
    t = src.shape[0]
    sb, wb = off // LANES, cw_off // LANES

    def bwd(x, w, b, d):
        _, vjp = jax.vjp(fn, x, w, b)
        return vjp(d.astype(F32))

    ins = [(src, (t, LANES), lambda i: (0, sb + i)), (cw, (CONV_K, LANES), lambda i: (0, wb + i)),
           (cb, (1, LANES), lambda i: (0, wb + i)), (dy, (t, LANES), lambda i: (0, i))]
    outs = [((t, n), BF16, (t, LANES), lambda i: (0, i)), ((CONV_K, n), F32, (CONV_K, LANES), lambda i: (0, i)),
            ((1, n), F32, (1, LANES), lambda i: (0, i))]
    return blockmap(name, bwd, (n // LANES,), ins, outs)


def _place():
    return lax.axis_index("x"), lax.axis_index("y"), lax.axis_index("c")


def _other_chips(x, y):
    return [(1 - x, y), (x, 1 - y), (1 - x, 1 - y)]


ANY = pl.BlockSpec(memory_space=pl.ANY)


def all_gather8(name, v):
    m_per, n = v.shape

    def body(x_ref, out_ref, send_sems, recv_sems, local_sem):
        x, y, c = _place()
        me, sibling = (x, y, c), (x, y, 1 - c)
        chips = _other_chips(x, y)

        def rows(px, py, pc):
            return out_ref.at[pl.ds((4 * px + 2 * py + pc) * m_per, m_per), :]

        def copy(k, block, to, src=None):
            return pltpu.make_async_remote_copy(
                src_ref=rows(*block) if src is None else src, dst_ref=rows(*block),
                send_sem=send_sems.at[k], recv_sem=recv_sems.at[k], device_id=to, device_id_type=MESH)

        mine = pltpu.make_async_copy(x_ref, rows(*me), local_sem)
        mine.start()
        first = [copy(0, me, sibling, src=x_ref)]
        first += [copy(1 + q, me, (*chip, c), src=x_ref) for q, chip in enumerate(chips)]
        for cp in first:
            cp.start()
        passed = [copy(4 + q, (*chip, c), sibling) for q, chip in enumerate(chips)]
        for q, chip in enumerate(chips):
            copy(1 + q, (*chip, c), me).wait_recv()
            passed[q].start()
        copy(0, sibling, me).wait_recv()
        for q, chip in enumerate(chips):
            copy(4 + q, (*chip, 1 - c), me).wait_recv()
        for cp in first + passed:
            cp.wait_send()
        mine.wait()

    return pl.pallas_call(
        body, name=name, out_shape=jax.ShapeDtypeStruct((N_DEV * m_per, n), v.dtype),
        in_specs=[pl.BlockSpec(memory_space=pltpu.VMEM)], out_specs=pl.BlockSpec(memory_space=pltpu.VMEM),
        scratch_shapes=[pltpu.SemaphoreType.DMA((7,)), pltpu.SemaphoreType.DMA((7,)), pltpu.SemaphoreType.DMA],
    )(v)


def gather_flat(name, vec):
    n = vec.shape[0]
    n_pad = -(-n // (8 * LANES)) * (8 * LANES)
    v = jnp.pad(vec, (0, n_pad - n)).reshape(8, n_pad // 8)
    return all_gather8(name, v).reshape(N_DEV, n_pad)[:, :n]


def all_gather_shards(name, shards):
    n_t = len(shards)

    def body(*refs):
        ins, outs = refs[:n_t], refs[n_t:2 * n_t]
        send_sems, recv_sems = refs[2 * n_t:]
        x, y, c = _place()
        sibling = (x, y, 1 - c)
        chips = _other_chips(x, y)
        mine = 2 * x + y

        def half(t, pc):
            h = ins[t].shape[0] // 2
            return pl.ds(pc * h, h)

        def copy(t, k, quarter, pc, to, src=None):
            dst = outs[t].at[quarter, half(t, pc)]
            return pltpu.make_async_remote_copy(
                src_ref=dst if src is None else src, dst_ref=dst,
                send_sem=send_sems.at[t, k], recv_sem=recv_sems.at[t, k], device_id=to, device_id_type=MESH)

        first = [copy(t, q, mine, c, (*chip, c), src=ins[t].at[half(t, c)]) for t in range(n_t) for q, chip in enumerate(chips)]
        for cp in first:
            cp.start()
        passed = []
        for t in range(n_t):
            for q, (px, py) in enumerate(chips):
                copy(t, q, 2 * px + py, c, (x, y, c)).wait_recv()
                cp = copy(t, 3 + q, 2 * px + py, c, sibling)
                cp.start()
                passed.append(cp)
        for t in range(n_t):
            for q, (px, py) in enumerate(chips):
                copy(t, 3 + q, 2 * px + py, 1 - c, (x, y, c)).wait_recv()
        for cp in first + passed:
            cp.wait_send()

    return pl.pallas_call(
        body, name=name, out_shape=[jax.ShapeDtypeStruct((4,) + s.shape, s.dtype) for s in shards],
        in_specs=[ANY] * n_t, out_specs=[ANY] * n_t,
        scratch_shapes=[pltpu.SemaphoreType.DMA((n_t, 6)), pltpu.SemaphoreType.DMA((n_t, 6))],
    )(*shards)


def exchange_halves_d2d(name, grads):
    n_t = len(grads)

    def body(*refs):
        ins, outs = refs[:n_t], refs[n_t:2 * n_t]
        send_sems, recv_sems = refs[2 * n_t:]
        x, y, c = _place()
        copies = []
        for t in range(n_t):
            h = ins[t].shape[1] // 2
            copies.append(pltpu.make_async_remote_copy(
                src_ref=ins[t].at[:, pl.ds((1 - c) * h, h), :], dst_ref=outs[t],
                send_sem=send_sems.at[t], recv_sem=recv_sems.at[t], device_id=(x, y, 1 - c), device_id_type=MESH))
        for cp in copies:
            cp.start()
        for cp in copies:
            cp.wait()

    return pl.pallas_call(
        body, name=name,
        out_shape=[jax.ShapeDtypeStruct((4, g.shape[1] // 2, g.shape[2]), g.dtype) for g in grads],
        in_specs=[ANY] * n_t, out_specs=[ANY] * n_t,
        scratch_shapes=[pltpu.SemaphoreType.DMA((n_t,)), pltpu.SemaphoreType.DMA((n_t,))],
    )(*grads)


def exchange_quarters_ici(name, parts):
    n_t = len(parts)

    def body(*refs):
        ins, outs = refs[:n_t], refs[n_t:2 * n_t]
        send_sems, recv_sems = refs[2 * n_t:]
        x, y, c = _place()
        copies = []
        for t in range(n_t):
            for q, (px, py) in enumerate(_other_chips(x, y)):
                copies.append(pltpu.make_async_remote_copy(
                    src_ref=ins[t].at[2 * px + py], dst_ref=outs[t].at[q],
                    send_sem=send_sems.at[t, q], recv_sem=recv_sems.at[t, q], device_id=(px, py, c), device_id_type=MESH))
        for cp in copies:
            cp.start()
        for cp in copies:
            cp.wait()

    return pl.pallas_call(
        body, name=name,
        out_shape=[jax.ShapeDtypeStruct((3,) + p.shape[1:], p.dtype) for p in parts],
        in_specs=[ANY] * n_t, out_specs=[ANY] * n_t,
        scratch_shapes=[pltpu.SemaphoreType.DMA((n_t, 3)), pltpu.SemaphoreType.DMA((n_t, 3))],
    )(*parts)


def swap_d2d(name, halves):
    n_t = len(halves)

    def body(*refs):
        ins, outs = refs[:n_t], refs[n_t:2 * n_t]
        send_sems, recv_sems = refs[2 * n_t:]
        x, y, c = _place()
        copies = [pltpu.make_async_remote_copy(
            src_ref=ins[t], dst_ref=outs[t], send_sem=send_sems.at[t], recv_sem=recv_sems.at[t],
            device_id=(x, y, 1 - c), device_id_type=MESH) for t in range(n_t)]
        for cp in copies:
            cp.start()
        for cp in copies:
            cp.wait()

    return pl.pallas_call(
        body, name=name, out_shape=[jax.ShapeDtypeStruct(h.shape, h.dtype) for h in halves],
        in_specs=[ANY] * n_t, out_specs=[ANY] * n_t,
        scratch_shapes=[pltpu.SemaphoreType.DMA((n_t,)), pltpu.SemaphoreType.DMA((n_t,))],
    )(*halves)


BLOCK_BYTES = 1 << 20


def _row_block(r, c):
    rb = r
    while rb * c * 4 > BLOCK_BYTES and rb % 16 == 0:
        rb //= 2
    return rb


def reduce_scatter_grads(grads):
    x, y, c = _place()
    place = jnp.stack([c, 2 * x + y]).astype(jnp.int32)
    from_sibling = exchange_halves_d2d("rs_d2d", grads)
    parts = []
    for t, (g, s) in enumerate(zip(grads, from_sibling)):
        _, h, cols = s.shape
        rb = _row_block(h, cols)
        nb = h // rb
        parts.append(blockmap(
            f"rs_add{t}", lambda a, b: a + b, (4, nb),
            [(g, (None, rb, cols), lambda k, i, s_ref, nb=nb: (k, s_ref[0] * nb + i, 0)), (s, (None, rb, cols), lambda k, i, s_ref: (k, i, 0))],
            [(s.shape, F32, (None, rb, cols), lambda k, i, s_ref: (k, i, 0))], scalars=place)[0])
    from_chips = exchange_quarters_ici("rs_ici", parts)
    halves = []
    for t, (p, q) in enumerate(zip(parts, from_chips)):
        _, h, cols = p.shape
        rb = _row_block(h, cols)
        halves.append(blockmap(
            f"rs_sum{t}", lambda a, b: a + b[0] + b[1] + b[2], (h // rb,),
            [(p, (None, rb, cols), lambda i, s_ref: (s_ref[1], i, 0)), (q, (3, rb, cols), lambda i, s_ref: (0, i, 0))],
            [((h, cols), F32, (rb, cols), lambda i, s_ref: (i, 0))], scalars=place)[0])
    others = swap_d2d("rs_swap", halves)
    south = c == 0
    return [jnp.concatenate([lax.select(south, mine, other), lax.select(south, other, mine)], axis=0)
            for mine, other in zip(halves, others)]


def _adamw(w, g, m, v):
    m = ADAM_B1 * m + (1.0 - ADAM_B1) * g
    v = ADAM_B2 * v + (1.0 - ADAM_B2) * jnp.square(g)
    m_hat = m / (1.0 - ADAM_B1 ** ADAM_STEP)
    v_hat = v / (1.0 - ADAM_B2 ** ADAM_STEP)
    delta = -ADAM_LR * (m_hat / (jnp.sqrt(v_hat) + ADAM_EPS) + ADAM_WD * w)
    return delta, m, v


def adamw(name, w, g, m, v):
    r, c = w.shape
    rb = _row_block(r, c)
    blk = lambda a: (a, (rb, c), lambda i: (i, 0))
    return blockmap(name, _adamw, (r // rb,), [blk(w), blk(g), blk(m), blk(v)], [((r, c), F32, (rb, c), lambda i: (i, 0))] * 3)


def _whole(name, fn, ins, outs):
    return blockmap(name, fn, (1,), [(a, a.shape, lambda i, nd=a.ndim: (0,) * nd) for a in ins],
                    [(s, d, s, lambda i, nd=len(s): (0,) * nd) for s, d in outs])


def _premix(x, w, sc, sh):
    return _rms(x, w) * (1.0 + sc) + sh


def _postmix(x, u, w_post, g1, w_pre2, sc2, sh2):
    x1 = x + g1 * _rms(u, w_post)
    return x1, _premix(x1, w_pre2, sc2, sh2)


def _merge(gs, gg, ys, yg):
    return _sigmoid(gs) * ys + _sigmoid(gg) * yg


def _final(x1, y2, w_post2, g2):
    return x1 + g2 * _rms(y2, w_post2)


def kernel(x, c, w_ada, b_ada, norm_mix_pre, norm_mix_post, w_in, ssm_conv_w, ssm_conv_b, ssm_dt_bias, ssm_A_log, ssm_D, ssm_norm_w, gdn_conv_w, gdn_dt_bias, gdn_A_log, gdn_norm_w, w_ssm_up, w_gdn_up, w_out, norm_mlp_pre, norm_mlp_post, w_mlp_up, w_mlp_down, loss_target, m_w_ada, m_b_ada, m_norm_mix_pre, m_norm_mix_post, m_w_in, m_ssm_conv_w, m_ssm_conv_b, m_ssm_dt_bias, m_ssm_A_log, m_ssm_D, m_ssm_norm_w, m_gdn_conv_w, m_gdn_dt_bias, m_gdn_A_log, m_gdn_norm_w, m_w_ssm_up, m_w_gdn_up, m_w_out, m_norm_mlp_pre, m_norm_mlp_post, m_w_mlp_up, m_w_mlp_down, v_w_ada, v_b_ada, v_norm_mix_pre, v_norm_mix_post, v_w_in, v_ssm_conv_w, v_ssm_conv_b, v_ssm_dt_bias, v_ssm_A_log, v_ssm_D, v_ssm_norm_w, v_gdn_conv_w, v_gdn_dt_bias, v_gdn_A_log, v_gdn_norm_w, v_w_ssm_up, v_w_gdn_up, v_w_out, v_norm_mlp_pre, v_norm_mlp_post, v_w_mlp_up, v_w_mlp_down):
    args = dict(locals())
    xi, yi, ci = _place()
    quarter = 2 * xi + yi
    batch = 4 * xi + 2 * yi + ci

    xt, target = x[0], loss_target[0]
    t, d = xt.shape
    hs, hv = ssm_dt_bias.shape[-1], gdn_dt_bias.shape[-1]
    d_inner = hs * SSM_HEAD_DIM
    n_grp = hs // SSM_HEADS_PER_GROUP
    gn = n_grp * SSM_D_STATE
    conv_ssm = d_inner + 2 * gn
    hq = hv // 2
    key, val = hq * GDN_HEAD, hv * GDN_HEAD
    conv_gdn = 2 * key + val
    hidden = 4 * w_mlp_up.shape[-1]
    o_dt = d_inner + conv_ssm
    o_qkv = o_dt + hs
    o_b = o_qkv + conv_gdn + val
    o_a = o_b + hv
    o_gs = o_a + hv
    n_proj = o_gs + 2 * d
    a_z, a_xs, a_bm, a_cm = 0, d_inner, 2 * d_inner, 2 * d_inner + gn
    a_q = o_dt
    a_k, a_v, a_zg = a_q + key, a_q + 2 * key, a_q + conv_gdn
    a_gs = a_zg + val
    a_gg = a_gs + d
    a_small = a_gg + d
    n_al = -(-(a_small + LANES) // MM_TILE_N) * MM_TILE_N

    def to_aligned(w):
        z = lambda n: jnp.zeros((w.shape[0], n), w.dtype)
        return jnp.concatenate([
            w[:, :o_dt], w[:, o_qkv:o_b], w[:, o_gs:],
            w[:, o_dt:o_qkv], z(LANE_B - hs), w[:, o_b:o_a], z(LANE_A - LANE_B - hv), w[:, o_a:o_gs], z(LANES - LANE_A - hv),
            z(n_al - a_small - LANES)], axis=1)

    def from_aligned(w):
        s = a_small
        return jnp.concatenate([
            w[:, :o_dt], w[:, s + LANE_DT:s + LANE_DT + hs], w[:, a_q:a_gs], w[:, s + LANE_B:s + LANE_B + hv],
            w[:, s + LANE_A:s + LANE_A + hv], w[:, a_gs:a_small]], axis=1)

    def lanes(vec, at):
        return jnp.zeros((1, LANES), F32).at[:, at:at + vec.shape[-1]].set(vec.reshape(1, -1))

    n_cw = CONV_K * ssm_conv_w.shape[-1]
    small_in = gather_flat("ag_small", jnp.concatenate([c.reshape(-1), ssm_conv_w.reshape(-1), gdn_conv_w.reshape(-1)]))
    c_all = small_in[:, :d]
    by_chip = small_in[0::2]

    def whole_conv_w(lo):
        return jnp.transpose(by_chip[:, lo:lo + n_cw].reshape(4, CONV_K, -1), (1, 0, 2)).reshape(CONV_K, -1)

    cw_ssm, cw_gdn = whole_conv_w(d), whole_conv_w(d + n_cw)
    cb_ssm = ssm_conv_b
    cb_gdn = jnp.zeros((1, conv_gdn), F32)

    n_ada = w_ada.shape[-1]
    b_q = lax.dynamic_slice_in_dim(b_ada, quarter * n_ada, n_ada, axis=1)
    mod_q = _whole("ada_fwd", lambda ca, w, b: _bdot(_silu(ca), w) + b, [c_all, w_ada[0], b_q], [((N_DEV, n_ada), F32)])[0]
    mod_all = gather_flat("ag_mod", mod_q.reshape(-1)).reshape(N_DEV, N_DEV, n_ada)[0::2]
    mod = lax.dynamic_index_in_dim(mod_all, batch, axis=1, keepdims=False).reshape(1, 4 * n_ada)
    sh1, sc1, g1, sh2, sc2, g2 = [mod[:, i * d:(i + 1) * d] for i in range(6)]

    own = [w.astype(BF16) for w in (w_in[0], w_ssm_up[0], w_gdn_up[0], w_out[0], w_mlp_up[0], w_mlp_down[0])]
    gathered = [lax.dynamic_update_index_in_dim(g, w, quarter, 0) for g, w in zip(all_gather_shards("ag_w", own), own)]
    cols_major = lambda g: jnp.transpose(g, (1, 0, 2)).reshape(g.shape[1], -1)
    rows_major = lambda g: g.reshape(-1, g.shape[2])
    wb_in = to_aligned(cols_major(gathered[0]))
    wb_ssm_up, wb_gdn_up, wb_out = rows_major(gathered[1]), rows_major(gathered[2]), rows_major(gathered[3])
    wb_up, wb_down = cols_major(gathered[4]), rows_major(gathered[5])

    h1 = rowmap("premix", _premix, [xt], [norm_mix_pre, sc1, sh1], [(d, BF16)])[0]
    proj = matmul("in_proj", h1, wb_in)
    xs = conv_fwd("conv_xs", _conv_silu, proj, d_inner, a_xs, cw_ssm, cb_ssm, 0)
    bm = conv_fwd("conv_bm", _conv_silu, proj, gn, a_bm, cw_ssm, cb_ssm, d_inner)
    cm = conv_fwd("conv_cm", _conv_silu, proj, gn, a_cm, cw_ssm, cb_ssm, d_inner + gn)
    q = conv_fwd("conv_q", _conv_silu_l2, proj, key, a_q, cw_gdn, cb_gdn, 0)
    k = conv_fwd("conv_k", _conv_silu_l2, proj, key, a_k, cw_gdn, cb_gdn, key)
    v = conv_fwd("conv_v", _conv_silu, proj, val, a_v, cw_gdn, cb_gdn, 2 * key)

    wide = 2 * LANES
    ssd_rows = [(xs, wide, 0, True), (bm, LANES, 0, True), (cm, LANES, 0, True), (proj, LANES, a_small, False), (proj, wide, a_z, True)]
    ssd_consts = [(lanes(ssm_dt_bias, LANE_DT), False), (lanes(ssm_A_log, LANE_DT), False), (lanes(ssm_D, LANE_DT), False),
                  (ssm_norm_w.reshape(n_grp, 1, wide), True)]
    hb_ssd, hb_gdn = min(SSD_GROUPS_PER_STEP, n_grp), min(GDN_HEADS_PER_STEP, hq)
    y_ssm_n, st_ssm = scan_fwd("ssd_fwd", ssd_step, SSM_CHUNK, n_grp, ssd_rows, ssd_consts, wide, hb_ssd)
    gdn_rows = [(q, LANES, 0, True), (k, LANES, 0, True), (v, wide, 0, True), (proj, wide, a_zg, True), (proj, LANES, a_small, False)]
    gdn_consts = [(lanes(gdn_dt_bias, LANE_A), False), (lanes(gdn_A_log, LANE_A), False), (gdn_norm_w, False)]
    y_gdn_n, st_gdn = scan_fwd("gdn_fwd", gdn_step, GDN_CHUNK, hq, gdn_rows, gdn_consts, wide, hb_gdn)

    y_ssm = matmul("ssm_up", y_ssm_n, wb_ssm_up)
    y_gdn = matmul("gdn_up", y_gdn_n, wb_gdn_up)
    gates = [(proj, d, a_gs), (proj, d, a_gg)]
    merged = rowmap("merge", _merge, gates + [y_ssm, y_gdn], [], [(d, BF16)])[0]
    u = matmul("w_out", merged, wb_out)
    post_consts = [norm_mix_post, g1, norm_mlp_pre, sc2, sh2]
    x1, h2 = rowmap("postmix", _postmix, [xt, u], post_consts, [(d, F32), (d, BF16)])
    relu2 = lambda acc: (acc, jnp.square(jnp.maximum(acc, 0.0)))
    a_up, act = matmul("mlp_up", h2, wb_up, out_dtypes=(BF16, BF16), epi=relu2)
    y2 = matmul("mlp_down", act, wb_down)

    def final_bwd(x1_, y2_, tgt, w_, g_):
        x2, vjp = jax.vjp(_final, x1_, y2_, w_, g_)
        err = x2 - tgt
        loss = 0.5 * jnp.sum(jnp.mean(err * err, axis=-1, keepdims=True), axis=0, keepdims=True)
        dx1, dy2, dw, dg = vjp(err / d)
        return dx1, dy2, loss, dw, dg

    dx1, dy2, loss_part, d_norm_mlp_post, dg2 = rowmap(
        "final", final_bwd, [x1, y2, target], [norm_mlp_post, g2], [(d, F32), (d, BF16)], [((1, 1), F32), ((1, d), F32), ((1, d), F32)])
    loss = lax.psum(loss_part[0, 0], ("x", "y", "c"))

    d_a = matmul("mlp_down_dx", dy2, wb_down, tb=True, out_dtypes=(BF16,), extras=[a_up],
                 epi=lambda acc, a: acc * 2.0 * jnp.maximum(a.astype(F32), 0.0))
    gw_down = matmul("mlp_down_dw", act, dy2, ta=True)
    dh2 = matmul("mlp_up_dx", d_a, wb_up, tb=True)
    gw_up = matmul("mlp_up_dw", h2, d_a, ta=True)

    def postmix_bwd(x_, u_, dx1_, dh2_, *cs):
        _, vjp = jax.vjp(_postmix, x_, u_, *cs)
        return vjp((dx1_, dh2_))

    dxa, du, d_norm_mix_post, dg1, d_norm_mlp_pre, dsc2, dsh2 = rowmap(
        "postmix_bwd", postmix_bwd, [xt, u, dx1, dh2], post_consts, [(d, F32), (d, BF16)], [((1, d), F32)] * 5)
    d_merged = matmul("w_out_dx", du, wb_out, tb=True)
    gw_out = matmul("w_out_dw", merged, du, ta=True)

    def merge_bwd(gs, gg, ys, yg, dm):
        _, vjp = jax.vjp(_merge, gs, gg, ys, yg)
        dgs, dgg, dys, dyg = vjp(dm)
        return dys, dyg, dgs, dgg

    dy_ssm, dy_gdn, dgs, dgg = rowmap("merge_bwd", merge_bwd, gates + [y_ssm, y_gdn, d_merged], [], [(d, BF16)] * 4)
    dy_ssm_n = matmul("ssm_up_dx", dy_ssm, wb_ssm_up, tb=True, out_dtypes=(BF16,))
    gw_ssm_up = matmul("ssm_up_dw", y_ssm_n, dy_ssm, ta=True)
    dy_gdn_n = matmul("gdn_up_dx", dy_gdn, wb_gdn_up, tb=True, out_dtypes=(BF16,))
    gw_gdn_up = matmul("gdn_up_dw", y_gdn_n, dy_gdn, ta=True)

    dxs, dbm, dcm, dsmall_ssm, dz_ssm, d_sdtb, d_salog, d_sdsk, d_snw = scan_bwd(
        "ssd_bwd", ssd_step, SSM_CHUNK, n_grp, ssd_rows, ssd_consts, st_ssm, dy_ssm_n, [BF16, BF16, BF16, F32, BF16], hb_ssd)
    dq, dk, dv, dz_gdn, dsmall_gdn, d_gdtb, d_galog, d_gnw = scan_bwd(
        "gdn_bwd", gdn_step, GDN_CHUNK, hq, gdn_rows, gdn_consts, st_gdn, dy_gdn_n, [BF16, BF16, BF16, BF16, F32], hb_gdn)

    dxs_p, dcw_xs, dcb_xs = conv_bwd("conv_xs_bwd", _conv_silu, proj, d_inner, a_xs, cw_ssm, cb_ssm, 0, dxs)
    dbm_p, dcw_bm, dcb_bm = conv_bwd("conv_bm_bwd", _conv_silu, proj, gn, a_bm, cw_ssm, cb_ssm, d_inner, dbm)
    dcm_p, dcw_cm, dcb_cm = conv_bwd("conv_cm_bwd", _conv_silu, proj, gn, a_cm, cw_ssm, cb_ssm, d_inner + gn, dcm)
    dq_p, dcw_q, _ = conv_bwd("conv_q_bwd", _conv_silu_l2, proj, key, a_q, cw_gdn, cb_gdn, 0, dq)
    dk_p, dcw_k, _ = conv_bwd("conv_k_bwd", _conv_silu_l2, proj, key, a_k, cw_gdn, cb_gdn, key, dk)
    dv_p, dcw_v, _ = conv_bwd("conv_v_bwd", _conv_silu, proj, val, a_v, cw_gdn, cb_gdn, 2 * key, dv)
    dsmall = rowmap("small_sum", lambda a, b: a + b, [dsmall_ssm, dsmall_gdn], [], [(LANES, BF16)])[0]
    dproj = jnp.concatenate([dz_ssm, dxs_p, dbm_p, dcm_p, dq_p, dk_p, dv_p, dz_gdn, dgs, dgg, dsmall,
                             jnp.zeros((t, n_al - a_small - LANES), BF16)], axis=1)
    dh1 = matmul("in_proj_dx", dproj, wb_in, tb=True)
    gw_in_al = matmul("in_proj_dw", h1, dproj, ta=True)

    def premix_bwd(x_, dxa_, dh1_, w_, sc_, sh_):
        _, vjp = jax.vjp(_premix, x_, w_, sc_, sh_)
        dx, dw, dsc, dsh = vjp(dh1_)
        return dx + dxa_, dw, dsc, dsh

    grad_x, d_norm_mix_pre, dsc1, dsh1 = rowmap(
        "premix_bwd", premix_bwd, [xt, dxa, dh1], [norm_mix_pre, sc1, sh1], [(d, F32)], [((1, d), F32)] * 3)

    dmod_all = gather_flat("ag_dmod", jnp.concatenate([dsh1, dsc1, dg1, dsh2, dsc2, dg2], axis=1).reshape(-1))
    dmod_q = lax.dynamic_slice_in_dim(dmod_all, quarter * n_ada, n_ada, axis=1)
    gw_ada, gb_ada = _whole(
        "ada_bwd", lambda ca, dq_, da_: (_bdot(_silu(ca), dq_, TN), jnp.sum(da_, axis=0, keepdims=True)),
        [c_all, dmod_q, dmod_all], [((d, n_ada), F32), ((1, 4 * n_ada), F32)])

    dcw_ssm = jnp.concatenate([dcw_xs, dcw_bm, dcw_cm], axis=1)
    dcb_ssm = jnp.concatenate([dcb_xs, dcb_bm, dcb_cm], axis=1)
    dcw_gdn = jnp.concatenate([dcw_q, dcw_k, dcw_v], axis=1)
    partial = [d_norm_mix_pre, d_norm_mix_post, dcw_ssm, dcb_ssm, d_sdtb[:, LANE_DT:LANE_DT + hs], d_salog[:, LANE_DT:LANE_DT + hs],
               d_sdsk[:, LANE_DT:LANE_DT + hs], d_snw, dcw_gdn, d_gdtb[:, LANE_A:LANE_A + hv], d_galog[:, LANE_A:LANE_A + hv], d_gnw,
               d_norm_mlp_pre, d_norm_mlp_post]
    sizes = [p.size for p in partial]
    stacked = gather_flat("ag_grads", jnp.concatenate([p.reshape(-1) for p in partial]))
    summed = _whole("small_sum8", lambda s: jnp.sum(s, axis=0, keepdims=True), [stacked], [((1, stacked.shape[1]), F32)])[0][0]
    offs = [0]
    for s in sizes:
        offs.append(offs[-1] + s)
    red = [summed[offs[i]:offs[i + 1]] for i in range(len(sizes))]
    my_cols = lambda full: lax.dynamic_slice_in_dim(full.reshape(CONV_K, -1), quarter * (n_cw // CONV_K), n_cw // CONV_K, axis=1)
    small_grads = {
        "b_ada": gb_ada, "norm_mix_pre": red[0], "norm_mix_post": red[1], "ssm_conv_w": my_cols(red[2]), "ssm_conv_b": red[3],
        "ssm_dt_bias": red[4], "ssm_A_log": red[5], "ssm_D": red[6], "ssm_norm_w": red[7], "gdn_conv_w": my_cols(red[8]),
        "gdn_dt_bias": red[9], "gdn_A_log": red[10], "gdn_norm_w": red[11], "norm_mlp_pre": red[12], "norm_mlp_post": red[13]}

    quarters_cols = lambda g: jnp.transpose(g.reshape(g.shape[0], 4, -1), (1, 0, 2))
    quarters_rows = lambda g: g.reshape(4, g.shape[0] // 4, g.shape[1])
    big_names = ["w_in", "w_ssm_up", "w_gdn_up", "w_out", "w_mlp_up", "w_mlp_down"]
    big_partial = [quarters_cols(from_aligned(gw_in_al)), quarters_rows(gw_ssm_up), quarters_rows(gw_gdn_up), quarters_rows(gw_out),
                   quarters_cols(gw_up), quarters_rows(gw_down)]
    big_grads = dict(zip(big_names, reduce_scatter_grads(big_partial)))
    big_grads["w_ada"] = gw_ada

    names = ['w_ada', 'b_ada', 'norm_mix_pre', 'norm_mix_post', 'w_in', 'ssm_conv_w', 'ssm_conv_b', 'ssm_dt_bias', 'ssm_A_log', 'ssm_D',
             'ssm_norm_w', 'gdn_conv_w', 'gdn_dt_bias', 'gdn_A_log', 'gdn_norm_w', 'w_ssm_up', 'w_gdn_up', 'w_out', 'norm_mlp_pre',
             'norm_mlp_post', 'w_mlp_up', 'w_mlp_down']
    grad, delta, new_m, new_v = {}, {}, {}, {}
    for n in big_grads:
        shape = args[n].shape
        g2d = big_grads[n]
        dl, nm, nv = adamw("adamw_" + n, args[n][0], g2d, args["m_" + n][0], args["v_" + n][0])
        grad[n], delta[n], new_m[n], new_v[n] = [a.reshape(shape) for a in (g2d, dl, nm, nv)]
    small_names = [n for n in names if n not in big_grads]
    flat = lambda pre: jnp.concatenate([args[pre + n].reshape(-1) for n in small_names]).reshape(1, -1)
    g_flat = jnp.concatenate([small_grads[n].reshape(-1) for n in small_names]).reshape(1, -1)
    dl, nm, nv = adamw("adamw_small", flat(""), g_flat, flat("m_"), flat("v_"))
    off = 0
    for n in small_names:
        shape = args[n].shape
        size = args[n].size
        grad[n], delta[n], new_m[n], new_v[n] = [a[0, off:off + size].reshape(shape) for a in (g_flat, dl, nm, nv)]
        off += size

    return (loss, grad_x.reshape(x.shape), *[grad[n] for n in names], *[delta[n] for n in names],
            *[new_m[n] for n in names], *[new_v[n] for n in names])
```

```python
import functools

import jax
import jax.numpy as jnp
from jax import lax
from jax.experimental import pallas as pl
from jax.experimental.pallas import tpu as pltpu

F32 = jnp.float32
BF16 = jnp.bfloat16
MESH = pl.DeviceIdType.MESH

EPS = 1e-6
SSM_HEAD_DIM = 64
SSM_HEADS_PER_GROUP = 4
SSM_D_STATE = 128
SSM_CHUNK = 128
GDN_HEAD = 128
GDN_CHUNK = 64
CONV_K = 4
LANE_DT, LANE_B, LANE_A = 0, 32, 48
ADAM_LR, ADAM_B1, ADAM_B2, ADAM_EPS, ADAM_WD, ADAM_STEP = 0.001, 0.9, 0.999, 1e-08, 0.01, 10

VMEM_LIMIT_BYTES = 56 * 1024 * 1024
LANES = 128
N_DEV = 8

NN = (((1,), (0,)), ((), ()))
NT = (((1,), (1,)), ((), ()))
TN = (((0,), (0,)), ((), ()))


BNN = (((2,), (1,)), ((0,), (0,)))
BNT = (((2,), (2,)), ((0,), (0,)))
BTN = (((1,), (1,)), ((0,), (0,)))
_KIND = {NN: ("NN", 0), NT: ("NT", 0), TN: ("TN", 0), BNN: ("NN", 1), BNT: ("NT", 1), BTN: ("TN", 1)}
_DIMS = {"NN": (NN, BNN), "NT": (NT, BNT), "TN": (TN, BTN)}


def _dg(a, b, dims):
    return lax.dot_general(a, b, dims, preferred_element_type=F32)


def _raw_bf16(a, b, dims):
    return _dg(a.astype(BF16), b.astype(BF16), dims)


def _raw_bf16x3(a, b, dims):
    ah, bh = a.astype(BF16), b.astype(BF16)
    al, bl = (a - ah.astype(F32)).astype(BF16), (b - bh.astype(F32)).astype(BF16)
    return _dg(ah, bh, dims) + (_dg(ah, bl, dims) + _dg(al, bh, dims))


def _make_dot(raw):
    @functools.partial(jax.custom_vjp, nondiff_argnums=(2,))
    def dot(a, b, dims):
        return raw(a, b, dims)

    def fwd(a, b, dims):
        return raw(a, b, dims), (a, b)

    def bwd(dims, res, ct):
        a, b = res
        kind, batched = _KIND[dims]
        d = lambda k: _DIMS[k][batched]
        if kind == "NN":
            da, db = raw(ct, b, d("NT")), raw(a, ct, d("TN"))
        elif kind == "NT":
            da, db = raw(ct, b, d("NN")), raw(ct, a, d("TN"))
        else:
            da, db = raw(b, ct, d("NT")), raw(a, ct, d("NN"))
        return da.astype(a.dtype), db.astype(b.dtype)

    dot.defvjp(fwd, bwd)
    return lambda a, b, dims=NN: dot(a, b, dims)


_bdot = _make_dot(_raw_bf16)
_hdot = _make_dot(_raw_bf16x3)


def _mask_dot(mask, x, dims, mask_first=True):
    m = mask.astype(BF16)
    hi = x.astype(BF16)
    r = x - hi.astype(F32)
    mid = r.astype(BF16)
    lo = (r - mid.astype(F32)).astype(BF16)
    return sum(_dg(m, p, dims) if mask_first else _dg(p, m, dims) for p in (hi, mid, lo))


def _sigmoid(x):
    return 1.0 / (1.0 + jnp.exp(-x))


def _silu(x):
    return x * _sigmoid(x)


def _softplus(x):
    return jnp.maximum(x, 0.0) + jnp.log(1.0 + jnp.exp(-jnp.abs(x)))


def _rms(x, w):
    return x * lax.rsqrt(jnp.mean(x * x, axis=-1, keepdims=True) + EPS) * w


def _lane_col(m, idx):
    lane = lax.broadcasted_iota(jnp.int32, m.shape, 1)
    return jnp.sum(jnp.where(lane == idx, m, 0.0), axis=1, keepdims=True)


def _tril(n, strict=False):
    r = lax.broadcasted_iota(jnp.int32, (n, n), 0)
    c = lax.broadcasted_iota(jnp.int32, (n, n), 1)
    return (r > c) if strict else (r >= c)


def _first_lane(shape):
    return lax.broadcasted_iota(jnp.int32, shape, len(shape) - 1) == 0


@jax.custom_vjp
def _row_form(col):
    shape = col.shape[:-1] + (LANES,)
    return _mask_dot(_first_lane(shape), jnp.broadcast_to(col, shape), NT if col.ndim == 2 else BNT)


def _row_form_fwd(col):
    return _row_form(col), None


def _row_form_bwd(_, ct):
    shape = ct.shape[:-1] + (LANES,)
    sums = _mask_dot(_first_lane(shape), ct, TN if ct.ndim == 2 else BTN, mask_first=False)
    return (jnp.sum(sums, axis=-1, keepdims=True),)


_row_form.defvjp(_row_form_fwd, _row_form_bwd)


@jax.custom_vjp
def _cumsum_rows(x):
    return _mask_dot(_tril(x.shape[0]), x, NN)


def _cumsum_rows_fwd(x):
    return _cumsum_rows(x), None


def _cumsum_rows_bwd(_, ct):
    return (_mask_dot(_tril(ct.shape[0]), ct, TN),)


_cumsum_rows.defvjp(_cumsum_rows_fwd, _cumsum_rows_bwd)


def _params(sem):
    return pltpu.CompilerParams(dimension_semantics=sem, vmem_limit_bytes=VMEM_LIMIT_BYTES)


def blockmap(name, fn, grid, ins, outs, accs=(), scalars=None):
    n_in, n_out, n_acc = len(ins), len(outs), len(accs)
    n_grid = len(grid)
    n_pre = 0 if scalars is None else 1

    def body(*refs):
        refs = refs[n_pre:]
        vals = fn(*[r[...] for r in refs[:n_in]])
        if not isinstance(vals, (tuple, list)):
            vals = (vals,)
        for r, v in zip(refs[n_in:n_in + n_out], vals[:n_out]):
            r[...] = v.astype(r.dtype)
        if n_acc:
            first = functools.reduce(jnp.logical_and, [pl.program_id(a) == 0 for a in range(n_grid)])
            acc_refs = refs[n_in + n_out:]

            @pl.when(first)
            def _():
                for r in acc_refs:
                    r[...] = jnp.zeros(r.shape, r.dtype)

            for r, v in zip(acc_refs, vals[n_out:]):
                r[...] += v.astype(r.dtype)

    zeros = lambda nd: (lambda *_: (0,) * nd)
    in_specs = [pl.BlockSpec(b, im) for _, b, im in ins]
    out_specs = [pl.BlockSpec(b, im) for _, _, b, im in outs] + [pl.BlockSpec(s, zeros(len(s))) for s, _ in accs]
    out_shape = [jax.ShapeDtypeStruct(s, d) for s, d, _, _ in outs] + [jax.ShapeDtypeStruct(s, d) for s, d in accs]
    cparams = _params(("arbitrary",) * n_grid if n_acc else ("parallel",) * n_grid)
    arrays = [a for a, _, _ in ins]
    if scalars is None:
        return pl.pallas_call(body, name=name, grid=grid, in_specs=in_specs, out_specs=out_specs, out_shape=out_shape,
                              compiler_params=cparams)(*arrays)
    spec = pltpu.PrefetchScalarGridSpec(num_scalar_prefetch=1, grid=grid, in_specs=in_specs, out_specs=out_specs)
    return pl.pallas_call(body, name=name, grid_spec=spec, out_shape=out_shape, compiler_params=cparams)(scalars, *arrays)


def rowmap(name, fn, rows, consts, outs, accs=(), rb=256):
    norm = [(r, r.shape[1], 0) if not isinstance(r, tuple) else (r[0], r[1], r[2] // r[1]) for r in rows]
    assert all(not isinstance(r, tuple) or r[2] % r[1] == 0 for r in rows)
    t = norm[0][0].shape[0]
    rb = min(rb, t)
    ins = [(a, (rb, n), (lambda i, cb=cb: (i, cb))) for a, n, cb in norm]
    ins += [(cst, cst.shape, (lambda i, nd=cst.ndim: (0,) * nd)) for cst in consts]
    o = [((t, n), d, (rb, n), lambda i: (i, 0)) for n, d in outs]
    return blockmap(name, fn, (t // rb,), ins, o, accs)


MM_TILE_M, MM_TILE_N, MM_TILE_K = 1024, 1024, 2048


def _tile(dim, cap):
    if dim <= cap:
        return dim
    best = max(t for t in range(LANES, cap + 1, LANES) if dim % t == 0)
    return best


def matmul(name, a, b, ta=False, tb=False, out_dtypes=(F32,), epi=None, extras=(), comm=None):
    (k_dim, m_dim) = a.shape if ta else a.shape[::-1]
    n_dim = b.shape[0] if tb else b.shape[1]
    assert (b.shape[1] if tb else b.shape[0]) == k_dim, (name, a.shape, b.shape)
    tm, tn, tk = _tile(m_dim, MM_TILE_M), _tile(n_dim, MM_TILE_N), _tile(k_dim, MM_TILE_K)
    grid = (m_dim // tm, n_dim // tn, k_dim // tk)
    k_steps = grid[2]
    n_extra, n_out = len(extras), len(out_dtypes)
    n_cin, n_cout = (len(comm.operands), len(comm.out_shapes)) if comm else (0, 0)
    dims = (((0 if ta else 1,), (1 if tb else 0,)), ((), ()))

    def body(*refs):
        ins, outs, scratch = refs[:2 + n_extra + n_cin], refs[2 + n_extra + n_cin:][:n_out + n_cout], refs[2 + n_extra + n_cin + n_out + n_cout:]
        extra_refs, out_refs = ins[2:2 + n_extra], outs[:n_out]
        ids = [pl.program_id(ax) for ax in range(3)]
        if comm:
            comm_refs = (ins[2 + n_extra:], outs[n_out:], scratch[-2], scratch[-1])

            @pl.when(functools.reduce(jnp.logical_and, [i == 0 for i in ids]))
            def _():
                comm.start(*comm_refs)

        def finish(acc):
            vals = (acc,) if epi is None else epi(acc, *[r[...] for r in extra_refs])
            if not isinstance(vals, (tuple, list)):
                vals = (vals,)
            for r, v in zip(out_refs, vals):
                r[...] = v.astype(r.dtype)

        prod = lax.dot_general(ins[0][...].astype(BF16), ins[1][...].astype(BF16), dims, preferred_element_type=F32)
        if k_steps == 1:
            finish(prod)
        else:
            acc_ref = scratch[0]

            @pl.when(ids[2] == 0)
            def _():
                acc_ref[...] = jnp.zeros(acc_ref.shape, F32)

            acc_ref[...] += prod

            @pl.when(ids[2] == k_steps - 1)
            def _():
                finish(acc_ref[...])

        if comm:
            @pl.when(functools.reduce(jnp.logical_and, [i == g - 1 for i, g in zip(ids, grid)]))
            def _():
                comm.finish(*comm_refs)

    a_spec = pl.BlockSpec((tk, tm), lambda i, j, k: (k, i)) if ta else pl.BlockSpec((tm, tk), lambda i, j, k: (i, k))
    b_spec = pl.BlockSpec((tn, tk), lambda i, j, k: (j, k)) if tb else pl.BlockSpec((tk, tn), lambda i, j, k: (k, j))
    mn_spec = pl.BlockSpec((tm, tn), lambda i, j, k: (i, j))
    scratch_shapes = [] if k_steps == 1 else [pltpu.VMEM((tm, tn), F32)]
    if comm:
        scratch_shapes += [pltpu.SemaphoreType.DMA(comm.sem_shape), pltpu.SemaphoreType.DMA(comm.sem_shape)]
    res = pl.pallas_call(
        body, name=name, grid=grid,
        in_specs=[a_spec, b_spec] + [mn_spec] * n_extra + [ANY] * n_cin,
        out_specs=[mn_spec] * n_out + [ANY] * n_cout,
        out_shape=[jax.ShapeDtypeStruct((m_dim, n_dim), d) for d in out_dtypes] + (comm.out_shapes if comm else []),
        scratch_shapes=scratch_shapes,
        compiler_params=_params(("arbitrary",) * 3 if comm else ("parallel", "parallel", "arbitrary")),
    )(a, b, *extras, *(comm.operands if comm else []))
    main = res[:n_out] if n_out > 1 else res[0]
    return (main, list(res[n_out:])) if comm else main


def ssd_step(g0, state, xs, bm, cm, small, z, p_dtb, p_alog, p_dsk, nw):
    hb, n = state.shape[0], xs.shape[0]
    n_pair, n_head = 2 * hb, 4 * hb
    causal = _tril(n)
    dt_all = _softplus(small + p_dtb)
    a_all = dt_all * (-jnp.exp(p_alog))
    acum_all = _cumsum_rows(a_all)
    acum_t = acum_all.T
    lane0 = LANE_DT + SSM_HEADS_PER_GROUP * g0
    sub = lax.broadcasted_iota(jnp.int32, acum_t.shape, 0)
    heads = range(n_head)
    acum = jnp.stack([_lane_col(acum_all, lane0 + i) for i in heads])
    acum_row = jnp.stack([jnp.sum(jnp.where(sub == lane0 + i, acum_t, 0.0), axis=0, keepdims=True) for i in heads])
    dt = jnp.stack([_lane_col(dt_all, lane0 + i) for i in heads])
    dsk = jnp.stack([_lane_col(p_dsk, lane0 + i) for i in heads])
    decay = jnp.exp(jnp.where(causal, acum - acum_row, -jnp.inf))
    a_last = acum[:, n - 1:n, :]

    def split(a):
        return [a[:, i * LANES:(i + 1) * LANES] for i in range(a.shape[1] // LANES)]

    def pairs(a, axis=2):
        even = jnp.stack([a[2 * p] for p in range(n_pair)])
        odd = jnp.stack([a[2 * p + 1] for p in range(n_pair)])
        shape = (n_pair, LANES, LANES) if axis == 1 else (n_pair, a.shape[1], LANES)
        return jnp.where(lax.broadcasted_iota(jnp.int32, shape, axis) < SSM_HEAD_DIM, even, odd)

    bms, cms = split(bm), split(cm)
    cb = _bdot(jnp.stack(cms), jnp.stack(bms), BNT)
    cbd = jnp.stack([cb[i // SSM_HEADS_PER_GROUP] for i in heads]) * decay
    xp = jnp.stack(split(xs))
    xdt = xp * pairs(dt)
    yd = _bdot(cbd, jnp.stack([xdt[i // 2] for i in heads]), BNN)
    lane = lax.broadcasted_iota(jnp.int32, (n_pair, n, LANES), 2)
    y_diag = jnp.where(lane < SSM_HEAD_DIM, jnp.stack([yd[2 * p] for p in range(n_pair)]), jnp.stack([yd[2 * p + 1] for p in range(n_pair)]))
    st = state.reshape(n_pair, LANES, LANES)
    cm2 = jnp.stack([cms[p // 2] for p in range(n_pair)])
    bm2 = jnp.stack([bms[p // 2] for p in range(n_pair)])
    y_off = _bdot(cm2, st, BNT) * pairs(jnp.exp(acum))
    new = st * pairs(jnp.exp(a_last), axis=1) + _bdot(xdt * pairs(jnp.exp(a_last - acum)), bm2, BTN)
    y = y_diag + y_off + pairs(dsk) * xp
    y = jnp.concatenate([y[p] for p in range(n_pair)], axis=1) * _silu(z)
    wide = 2 * LANES
    y = jnp.concatenate([_rms(y[:, i * wide:(i + 1) * wide], nw[i]) for i in range(hb)], axis=1)
    return new.reshape(state.shape), y


def _unit_lower_inverse(a):
    n = a.shape[-1]
    eye = (lax.broadcasted_iota(jnp.int32, (n, n), 0) == lax.broadcasted_iota(jnp.int32, (n, n), 1)).astype(F32)
    inv = eye - a
    power = a
    span = 2
    while span < n:
        power = _hdot(power, power, BNN)
        inv = inv + _hdot(inv, power, BNN)
        span *= 2
    return inv


def gdn_step(hq0, state, q, k, v, z, small, p_dtb, p_alog, nw):
    n = q.shape[0]
    nb = 2 * state.shape[0]
    st = state.reshape(nb, LANES, LANES)
    causal, strict = _tril(n), _tril(n, True)
    beta_all = _sigmoid(small)
    g_all = -jnp.exp(p_alog) * _softplus(small + p_dtb)
    gcum_all = _cumsum_rows(g_all)
    split = lambda a: [a[:, i * LANES:(i + 1) * LANES] for i in range(a.shape[1] // LANES)]
    qs, ks = split(q), split(k)
    q2 = jnp.stack([qs[i // 2] for i in range(nb)]) * (GDN_HEAD ** -0.5)
    k2 = jnp.stack([ks[i // 2] for i in range(nb)])
    v2, z2 = jnp.stack(split(v)), jnp.stack(split(z))
    gcum = jnp.stack([_lane_col(gcum_all, LANE_A + 2 * hq0 + i) for i in range(nb)])
    beta = jnp.stack([_lane_col(beta_all, LANE_B + 2 * hq0 + i) for i in range(nb)])
    dmat = jnp.exp(jnp.where(causal, gcum - _row_form(gcum), -jnp.inf))
    a_low = jnp.where(strict, beta * _bdot(k2, k2, BNT) * dmat, 0.0)
    inv = _unit_lower_inverse(a_low)
    egc = jnp.exp(gcum)
    u = _hdot(inv, v2 * beta, BNN)
    w = _hdot(inv, k2 * (beta * egc), BNN)
    v_new = u - _bdot(w, st, BNN)
    o = _bdot(q2 * egc, st, BNN) + _bdot(_bdot(q2, k2, BNT) * dmat, v_new, BNN)
    g_last = gcum[:, n - 1:n, :]
    k_dec = k2 * jnp.exp(g_last - gcum)
    new = st * jnp.exp(g_last) + _bdot(k_dec, v_new, BTN)
    out = _rms(o, nw) * _silu(z2)
    return new.reshape(state.shape), jnp.concatenate([out[i] for i in range(nb)], axis=1)


STATE_SHAPE = (2, LANES, LANES)
SSD_GROUPS_PER_STEP = 4
GDN_HEADS_PER_STEP = 4


def _scan_specs(rows, consts, chunk, chunk_of, hb):
    specs = []
    for _, n, off, per_group in rows:
        if per_group:
            assert off % (n * hb) == 0
            specs.append(pl.BlockSpec((chunk, n * hb), lambda c, g, cb=off // (n * hb): (chunk_of(c), cb + g)))
        else:
            assert off % n == 0
            specs.append(pl.BlockSpec((chunk, n), lambda c, g, cb=off // n: (chunk_of(c), cb)))
    for arr, per_group in consts:
        if per_group:
            specs.append(pl.BlockSpec((hb, 1, arr.shape[2]), lambda c, g: (g, 0, 0)))
        else:
            specs.append(pl.BlockSpec(arr.shape, lambda c, g, nd=arr.ndim: (0,) * nd))
    return specs


def scan_fwd(name, step, chunk, n_grp, rows, consts, out_cols, hb):
    t = rows[0][0].shape[0]
    nc = t // chunk
    n_rows, n_consts = len(rows), len(consts)

    def body(*refs):
        row_refs, const_refs = refs[:n_rows], refs[n_rows:n_rows + n_consts]
        y_ref, st_ref, state = refs[n_rows + n_consts:]
        c, g = pl.program_id(0), pl.program_id(1)

        @pl.when(c == 0)
        def _():
            state[g] = jnp.zeros((hb,) + STATE_SHAPE, F32)

        st = state[g]
        st_ref[...] = st
        new, y = step(g * hb, st, *[r[...] for r in row_refs], *[r[...] for r in const_refs])
        state[g] = new
        y_ref[...] = y.astype(y_ref.dtype)

    return pl.pallas_call(
        body, name=name, grid=(nc, n_grp // hb),
        in_specs=_scan_specs(rows, consts, chunk, lambda c: c, hb),
        out_specs=[pl.BlockSpec((chunk, out_cols * hb), lambda c, g: (c, g)),
                   pl.BlockSpec((None, None, hb) + STATE_SHAPE, lambda c, g: (c, g, 0, 0, 0, 0))],
        out_shape=[jax.ShapeDtypeStruct((t, n_grp * out_cols), BF16),
                   jax.ShapeDtypeStruct((nc, n_grp // hb, hb) + STATE_SHAPE, F32)],
        scratch_shapes=[pltpu.VMEM((n_grp // hb, hb) + STATE_SHAPE, F32)],
        compiler_params=_params(("arbitrary", "arbitrary")),
    )(*[r[0] for r in rows], *[c[0] for c in consts])


def scan_bwd(name, step, chunk, n_grp, rows, consts, states, dy, row_dtypes, hb):
    t = rows[0][0].shape[0]
    nc = t // chunk
    n_rows, n_consts = len(rows), len(consts)
    out_cols = dy.shape[1] // n_grp

    def body(*refs):
        row_refs, const_refs = refs[:n_rows], refs[n_rows:n_rows + n_consts]
        st_ref, dy_ref = refs[n_rows + n_consts:n_rows + n_consts + 2]
        outs = refs[n_rows + n_consts + 2:-1]
        dstate = refs[-1]
        c, g = pl.program_id(0), pl.program_id(1)

        @pl.when(c == 0)
        def _():
            dstate[g] = jnp.zeros((hb,) + STATE_SHAPE, F32)

        @pl.when(jnp.logical_and(c == 0, g == 0))
        def _():
            for r in outs[n_rows:]:
                r[...] = jnp.zeros(r.shape, r.dtype)

        _, vjp = jax.vjp(functools.partial(step, g * hb), st_ref[...], *[r[...] for r in row_refs], *[r[...] for r in const_refs])
        grads = vjp((dstate[g], dy_ref[...].astype(F32)))
        dstate[g] = grads[0]
        for (_, _, _, per_group), r, d in zip(rows, outs[:n_rows], grads[1:1 + n_rows]):
            if per_group:
                r[...] = d.astype(r.dtype)
            else:
                @pl.when(g == 0)
                def _(r=r):
                    r[...] = jnp.zeros(r.shape, r.dtype)

                r[...] += d.astype(r.dtype)
        for (_, per_group), r, d in zip(consts, outs[n_rows:], grads[1 + n_rows:]):
            if per_group:
                r[pl.ds(g * hb, hb)] += d
            else:
                r[...] += d

    rev = lambda c: nc - 1 - c
    out_specs, out_shape = [], []
    for (_, n, _, per_group), dt in zip(rows, row_dtypes):
        if per_group:
            out_specs.append(pl.BlockSpec((chunk, n * hb), lambda c, g: (rev(c), g)))
            out_shape.append(jax.ShapeDtypeStruct((t, n_grp * n), dt))
        else:
            out_specs.append(pl.BlockSpec((chunk, n), lambda c, g: (rev(c), 0)))
            out_shape.append(jax.ShapeDtypeStruct((t, n), dt))
    for arr, _ in consts:
        out_specs.append(pl.BlockSpec(arr.shape, lambda c, g, nd=arr.ndim: (0,) * nd))
        out_shape.append(jax.ShapeDtypeStruct(arr.shape, F32))
    return pl.pallas_call(
        body, name=name, grid=(nc, n_grp // hb),
        in_specs=_scan_specs(rows, consts, chunk, rev, hb)
        + [pl.BlockSpec((None, None, hb) + STATE_SHAPE, lambda c, g: (rev(c), g, 0, 0, 0, 0)),
           pl.BlockSpec((chunk, out_cols * hb), lambda c, g: (rev(c), g))],
        out_specs=out_specs, out_shape=out_shape,
        scratch_shapes=[pltpu.VMEM((n_grp // hb, hb) + STATE_SHAPE, F32)],
        compiler_params=_params(("arbitrary", "arbitrary")),
    )(*[r[0] for r in rows], *[c[0] for c in consts], states, dy)


@functools.partial(jax.custom_vjp, nondiff_argnums=(1,))
def _shift_rows(x, k):
    t = x.shape[0]
    row = lax.broadcasted_iota(jnp.int32, x.shape, 0)
    rolled = pltpu.roll(x, k % t, 0)
    return jnp.where(jnp.logical_and(row >= k, row < t + k), rolled, 0.0)


def _shift_rows_fwd(x, k):
    return _shift_rows(x, k), None


def _shift_rows_bwd(k, _, dy):
    return (_shift_rows(dy, -k),)


_shift_rows.defvjp(_shift_rows_fwd, _shift_rows_bwd)


def _conv_silu(x, cw, cb):
    pre = cb + sum(cw[j:j + 1, :] * _shift_rows(x, CONV_K - 1 - j) for j in range(CONV_K))
    return _silu(pre)


def _conv_silu_l2(x, cw, cb):
    y = _conv_silu(x, cw, cb)
    return y * lax.rsqrt(jnp.sum(y * y, axis=-1, keepdims=True) + EPS)


def conv_fwd(name, fn, src, n, off, cw, cb, cw_off):
    t = src.shape[0]
    sb, wb = off // LANES, cw_off // LANES
    ins = [(src, (t, LANES), lambda i: (0, sb + i)), (cw, (CONV_K, LANES), lambda i: (0, wb + i)),
           (cb, (1, LANES), lambda i: (0, wb + i))]
    return blockmap(name, fn, (n // LANES,), ins, [((t, n), F32, (t, LANES), lambda i: (0, i))])[0]


def conv_bwd(name, fn, src, n, off, cw, cb, cw_off, dy):
    t = src.shape[0]
    sb, wb = off // LANES, cw_off // LANES

    def bwd(x, w, b, d):
        _, vjp = jax.vjp(fn, x, w, b)
        return vjp(d.astype(F32))

    ins = [(src, (t, LANES), lambda i: (0, sb + i)), (cw, (CONV_K, LANES), lambda i: (0, wb + i)),
           (cb, (1, LANES), lambda i: (0, wb + i)), (dy, (t, LANES), lambda i: (0, i))]
    outs = [((t, n), BF16, (t, LANES), lambda i: (0, i)), ((CONV_K, n), F32, (CONV_K, LANES), lambda i: (0, i)),
            ((1, n), F32, (1, LANES), lambda i: (0, i))]
    return blockmap(name, bwd, (n // LANES,), ins, outs)


def _place():
    return lax.axis_index("x"), lax.axis_index("y"), lax.axis_index("c")


def _other_chips(x, y):
    return [(1 - x, y), (x, 1 - y), (1 - x, 1 - y)]


ANY = pl.BlockSpec(memory_space=pl.ANY)


def all_gather8(name, v):
    m_per, n = v.shape

    def body(x_ref, out_ref, send_sems, recv_sems, local_sem):
        x, y, c = _place()
        me, sibling = (x, y, c), (x, y, 1 - c)
        chips = _other_chips(x, y)

        def rows(px, py, pc):
            return out_ref.at[pl.ds((4 * px + 2 * py + pc) * m_per, m_per), :]

        def copy(k, block, to, src=None):
            return pltpu.make_async_remote_copy(
                src_ref=rows(*block) if src is None else src, dst_ref=rows(*block),
                send_sem=send_sems.at[k], recv_sem=recv_sems.at[k], device_id=to, device_id_type=MESH)

        mine = pltpu.make_async_copy(x_ref, rows(*me), local_sem)
        mine.start()
        first = [copy(0, me, sibling, src=x_ref)]
        first += [copy(1 + q, me, (*chip, c), src=x_ref) for q, chip in enumerate(chips)]
        for cp in first:
            cp.start()
        passed = [copy(4 + q, (*chip, c), sibling) for q, chip in enumerate(chips)]
        for q, chip in enumerate(chips):
            copy(1 + q, (*chip, c), me).wait_recv()
            passed[q].start()
        copy(0, sibling, me).wait_recv()
        for q, chip in enumerate(chips):
            copy(4 + q, (*chip, 1 - c), me).wait_recv()
        for cp in first + passed:
            cp.wait_send()
        mine.wait()

    return pl.pallas_call(
        body, name=name, out_shape=jax.ShapeDtypeStruct((N_DEV * m_per, n), v.dtype),
        in_specs=[pl.BlockSpec(memory_space=pltpu.VMEM)], out_specs=pl.BlockSpec(memory_space=pltpu.VMEM),
        scratch_shapes=[pltpu.SemaphoreType.DMA((7,)), pltpu.SemaphoreType.DMA((7,)), pltpu.SemaphoreType.DMA],
    )(v)


def gather_flat(name, vec):
    n = vec.shape[0]
    n_pad = -(-n // (8 * LANES)) * (8 * LANES)
    v = jnp.pad(vec, (0, n_pad - n)).reshape(8, n_pad // 8)
    return all_gather8(name, v).reshape(N_DEV, n_pad)[:, :n]


class Exchange:
    def __init__(self, operands, out_shapes, sem_shape, start, finish):
        self.operands, self.out_shapes, self.sem_shape, self.start, self.finish = list(operands), out_shapes, sem_shape, start, finish


def _start_all_wait_all(make_copies):
    def start(*refs):
        for cp in make_copies(*refs):
            cp.start()

    def finish(*refs):
        for cp in make_copies(*refs):
            cp.wait()

    return start, finish


def run_exchange(name, ex):
    n_in, n_out = len(ex.operands), len(ex.out_shapes)

    def body(*refs):
        ins, outs = refs[:n_in], refs[n_in:n_in + n_out]
        ex.start(ins, outs, *refs[n_in + n_out:])
        ex.finish(ins, outs, *refs[n_in + n_out:])

    return pl.pallas_call(
        body, name=name, out_shape=ex.out_shapes, in_specs=[ANY] * n_in, out_specs=[ANY] * n_out,
        scratch_shapes=[pltpu.SemaphoreType.DMA(ex.sem_shape), pltpu.SemaphoreType.DMA(ex.sem_shape)],
    )(*ex.operands)


def all_gather_shards(shards):
    n_t = len(shards)

    def copies(ins, outs, send_sems, recv_sems):
        x, y, c = _place()
        chips = _other_chips(x, y)

        def copy(t, k, quarter, pc, to, src=None):
            h = ins[t].shape[0] // 2
            dst = outs[t].at[quarter, pl.ds(pc * h, h)]
            return pltpu.make_async_remote_copy(
                src_ref=dst if src is None else ins[t].at[pl.ds(pc * h, h)], dst_ref=dst,
                send_sem=send_sems.at[t, k], recv_sem=recv_sems.at[t, k], device_id=to, device_id_type=MESH)

        first = [copy(t, q, 2 * x + y, c, (*chip, c), src=True) for t in range(n_t) for q, chip in enumerate(chips)]
        landed = [copy(t, q, 2 * px + py, c, (x, y, c)) for t in range(n_t) for q, (px, py) in enumerate(chips)]
        passed = [copy(t, 3 + q, 2 * px + py, c, (x, y, 1 - c)) for t in range(n_t) for q, (px, py) in enumerate(chips)]
        from_sibling = [copy(t, 3 + q, 2 * px + py, 1 - c, (x, y, c)) for t in range(n_t) for q, (px, py) in enumerate(chips)]
        return first, landed, passed, from_sibling

    def start(*refs):
        for cp in copies(*refs)[0]:
            cp.start()

    def finish(*refs):
        first, landed, passed, from_sibling = copies(*refs)
        for arrived, onward in zip(landed, passed):
            arrived.wait_recv()
            onward.start()
        for cp in from_sibling:
            cp.wait_recv()
        for cp in first + passed:
            cp.wait_send()

    return Exchange(shards, [jax.ShapeDtypeStruct((4,) + s.shape, s.dtype) for s in shards], (n_t, 6), start, finish)


def exchange_halves_d2d(grads):
    n_t = len(grads)

    def copies(ins, outs, send_sems, recv_sems):
        x, y, c = _place()
        return [pltpu.make_async_remote_copy(
            src_ref=ins[t].at[:, pl.ds((1 - c) * (ins[t].shape[1] // 2), ins[t].shape[1] // 2), :], dst_ref=outs[t],
            send_sem=send_sems.at[t], recv_sem=recv_sems.at[t], device_id=(x, y, 1 - c), device_id_type=MESH) for t in range(n_t)]

    shapes = [jax.ShapeDtypeStruct((4, g.shape[1] // 2, g.shape[2]), g.dtype) for g in grads]
    return Exchange(grads, shapes, (n_t,), *_start_all_wait_all(copies))


def exchange_quarters_ici(parts):
    n_t = len(parts)

    def copies(ins, outs, send_sems, recv_sems):
        x, y, c = _place()
        return [pltpu.make_async_remote_copy(
            src_ref=ins[t].at[2 * px + py], dst_ref=outs[t].at[q],
            send_sem=send_sems.at[t, q], recv_sem=recv_sems.at[t, q], device_id=(px, py, c), device_id_type=MESH)
            for t in range(n_t) for q, (px, py) in enumerate(_other_chips(x, y))]

    shapes = [jax.ShapeDtypeStruct((3,) + p.shape[1:], p.dtype) for p in parts]
    return Exchange(parts, shapes, (n_t, 3), *_start_all_wait_all(copies))


def swap_d2d(halves):
    n_t = len(halves)

    def copies(ins, outs, send_sems, recv_sems):
        x, y, c = _place()
        return [pltpu.make_async_remote_copy(
            src_ref=ins[t], dst_ref=outs[t], send_sem=send_sems.at[t], recv_sem=recv_sems.at[t],
            device_id=(x, y, 1 - c), device_id_type=MESH) for t in range(n_t)]

    return Exchange(halves, [jax.ShapeDtypeStruct(h.shape, h.dtype) for h in halves], (n_t,), *_start_all_wait_all(copies))


BLOCK_BYTES = 1 << 20


def _row_block(r, c):
    rb = r
    while rb * c * 4 > BLOCK_BYTES and rb % 16 == 0:
        rb //= 2
    return rb


def _place_scalars():
    x, y, c = _place()
    return jnp.stack([c, 2 * x + y]).astype(jnp.int32)


def reduce_on_chip(tag, grads):
    from_sibling = run_exchange(f"rs_d2d_{tag}", exchange_halves_d2d(grads))
    parts, parts_bf16 = [], []
    for t, (g, s) in enumerate(zip(grads, from_sibling)):
        _, h, cols = s.shape
        rb = _row_block(h, cols)
        nb = h // rb
        blk = lambda k, i, s_ref: (k, i, 0)
        p32, p16 = blockmap(
            f"rs_add_{tag}{t}", lambda a, b: (a + b, a + b), (4, nb),
            [(g, (None, rb, cols), lambda k, i, s_ref, nb=nb: (k, s_ref[0] * nb + i, 0)), (s, (None, rb, cols), blk)],
            [(s.shape, F32, (None, rb, cols), blk), (s.shape, BF16, (None, rb, cols), blk)], scalars=_place_scalars())
        parts.append(p32)
        parts_bf16.append(p16)
    return parts, parts_bf16


def reduce_across_chips(parts, from_chips):
    halves = []
    for t, (p, q) in enumerate(zip(parts, from_chips)):
        _, h, cols = p.shape
        rb = _row_block(h, cols)
        halves.append(blockmap(
            f"rs_sum{t}", lambda a, b: a + b[0].astype(F32) + b[1].astype(F32) + b[2].astype(F32), (h // rb,),
            [(p, (None, rb, cols), lambda i, s_ref: (s_ref[1], i, 0)), (q, (3, rb, cols), lambda i, s_ref: (0, i, 0))],
            [((h, cols), F32, (rb, cols), lambda i, s_ref: (i, 0))], scalars=_place_scalars())[0])
    others = run_exchange("rs_swap", swap_d2d(halves))
    south = lax.axis_index("c") == 0
    return [jnp.concatenate([lax.select(south, mine, other), lax.select(south, other, mine)], axis=0)
            for mine, other in zip(halves, others)]


def _adamw(w, g, m, v):
    m = ADAM_B1 * m + (1.0 - ADAM_B1) * g
    v = ADAM_B2 * v + (1.0 - ADAM_B2) * jnp.square(g)
    m_hat = m / (1.0 - ADAM_B1 ** ADAM_STEP)
    v_hat = v / (1.0 - ADAM_B2 ** ADAM_STEP)
    delta = -ADAM_LR * (m_hat / (jnp.sqrt(v_hat) + ADAM_EPS) + ADAM_WD * w)
    return delta, m, v


def adamw(name, w, g, m, v):
    r, c = w.shape
    rb = _row_block(r, c)
    blk = lambda a: (a, (rb, c), lambda i: (i, 0))
    return blockmap(name, _adamw, (r // rb,), [blk(w), blk(g), blk(m), blk(v)], [((r, c), F32, (rb, c), lambda i: (i, 0))] * 3)


def _whole(name, fn, ins, outs):
    return blockmap(name, fn, (1,), [(a, a.shape, lambda i, nd=a.ndim: (0,) * nd) for a in ins],
                    [(s, d, s, lambda i, nd=len(s): (0,) * nd) for s, d in outs])


def _premix(x, w, sc, sh):
    return _rms(x, w) * (1.0 + sc) + sh


def _postmix(x, u, w_post, g1, w_pre2, sc2, sh2):
    x1 = x + g1 * _rms(u, w_post)
    return x1, _premix(x1, w_pre2, sc2, sh2)


def _merge(gs, gg, ys, yg):
    return _sigmoid(gs) * ys + _sigmoid(gg) * yg


def _final(x1, y2, w_post2, g2):
    return x1 + g2 * _rms(y2, w_post2)


def kernel(x, c, w_ada, b_ada, norm_mix_pre, norm_mix_post, w_in, ssm_conv_w, ssm_conv_b, ssm_dt_bias, ssm_A_log, ssm_D, ssm_norm_w, gdn_conv_w, gdn_dt_bias, gdn_A_log, gdn_norm_w, w_ssm_up, w_gdn_up, w_out, norm_mlp_pre, norm_mlp_post, w_mlp_up, w_mlp_down, loss_target, m_w_ada, m_b_ada, m_norm_mix_pre, m_norm_mix_post, m_w_in, m_ssm_conv_w, m_ssm_conv_b, m_ssm_dt_bias, m_ssm_A_log, m_ssm_D, m_ssm_norm_w, m_gdn_conv_w, m_gdn_dt_bias, m_gdn_A_log, m_gdn_norm_w, m_w_ssm_up, m_w_gdn_up, m_w_out, m_norm_mlp_pre, m_norm_mlp_post, m_w_mlp_up, m_w_mlp_down, v_w_ada, v_b_ada, v_norm_mix_pre, v_norm_mix_post, v_w_in, v_ssm_conv_w, v_ssm_conv_b, v_ssm_dt_bias, v_ssm_A_log, v_ssm_D, v_ssm_norm_w, v_gdn_conv_w, v_gdn_dt_bias, v_gdn_A_log, v_gdn_norm_w, v_w_ssm_up, v_w_gdn_up, v_w_out, v_norm_mlp_pre, v_norm_mlp_post, v_w_mlp_up, v_w_mlp_down):
    args = dict(locals())
    xi, yi, ci = _place()
    quarter = 2 * xi + yi
    batch = 4 * xi + 2 * yi + ci

    xt, target = x[0], loss_target[0]
    t, d = xt.shape
    hs, hv = ssm_dt_bias.shape[-1], gdn_dt_bias.shape[-1]
    d_inner = hs * SSM_HEAD_DIM
    n_grp = hs // SSM_HEADS_PER_GROUP
    gn = n_grp * SSM_D_STATE
    conv_ssm = d_inner + 2 * gn
    hq = hv // 2
    key, val = hq * GDN_HEAD, hv * GDN_HEAD
    conv_gdn = 2 * key + val
    hidden = 4 * w_mlp_up.shape[-1]
    o_dt = d_inner + conv_ssm
    o_qkv = o_dt + hs
    o_b = o_qkv + conv_gdn + val
    o_a = o_b + hv
    o_gs = o_a + hv
    n_proj = o_gs + 2 * d
    a_z, a_xs, a_bm, a_cm = 0, d_inner, 2 * d_inner, 2 * d_inner + gn
    a_q = o_dt
    a_k, a_v, a_zg = a_q + key, a_q + 2 * key, a_q + conv_gdn
    a_gs = a_zg + val
    a_gg = a_gs + d
    a_small = a_gg + d
    n_al = -(-(a_small + LANES) // MM_TILE_N) * MM_TILE_N

    def to_aligned(w):
        z = lambda n: jnp.zeros((w.shape[0], n), w.dtype)
        return jnp.concatenate([
            w[:, :o_dt], w[:, o_qkv:o_b], w[:, o_gs:],
            w[:, o_dt:o_qkv], z(LANE_B - hs), w[:, o_b:o_a], z(LANE_A - LANE_B - hv), w[:, o_a:o_gs], z(LANES - LANE_A - hv),
            z(n_al - a_small - LANES)], axis=1)

    def from_aligned(w):
        s = a_small
        return jnp.concatenate([
            w[:, :o_dt], w[:, s + LANE_DT:s + LANE_DT + hs], w[:, a_q:a_gs], w[:, s + LANE_B:s + LANE_B + hv],
            w[:, s + LANE_A:s + LANE_A + hv], w[:, a_gs:a_small]], axis=1)

    def lanes(vec, at):
        return jnp.zeros((1, LANES), F32).at[:, at:at + vec.shape[-1]].set(vec.reshape(1, -1))

    n_cw = CONV_K * ssm_conv_w.shape[-1]
    small_in = gather_flat("ag_small", jnp.concatenate([c.reshape(-1), ssm_conv_w.reshape(-1), gdn_conv_w.reshape(-1)]))
    c_all = small_in[:, :d]
    by_chip = small_in[0::2]

    def whole_conv_w(lo):
        return jnp.transpose(by_chip[:, lo:lo + n_cw].reshape(4, CONV_K, -1), (1, 0, 2)).reshape(CONV_K, -1)

    cw_ssm, cw_gdn = whole_conv_w(d), whole_conv_w(d + n_cw)
    cb_ssm = ssm_conv_b
    cb_gdn = jnp.zeros((1, conv_gdn), F32)

    n_ada = w_ada.shape[-1]
    b_q = lax.dynamic_slice_in_dim(b_ada, quarter * n_ada, n_ada, axis=1)
    mod_q = _whole("ada_fwd", lambda ca, w, b: _bdot(_silu(ca), w) + b, [c_all, w_ada[0], b_q], [((N_DEV, n_ada), F32)])[0]
    mod_all = gather_flat("ag_mod", mod_q.reshape(-1)).reshape(N_DEV, N_DEV, n_ada)[0::2]
    mod = lax.dynamic_index_in_dim(mod_all, batch, axis=1, keepdims=False).reshape(1, 4 * n_ada)
    sh1, sc1, g1, sh2, sc2, g2 = [mod[:, i * d:(i + 1) * d] for i in range(6)]

    own = [w.astype(BF16) for w in (w_in[0], w_ssm_up[0], w_gdn_up[0], w_out[0], w_mlp_up[0], w_mlp_down[0])]
    with_own = lambda gs, ws: [lax.dynamic_update_index_in_dim(g, w, quarter, 0) for g, w in zip(gs, ws)]
    cols_major = lambda g: jnp.transpose(g, (1, 0, 2)).reshape(g.shape[1], -1)
    rows_major = lambda g: g.reshape(-1, g.shape[2])
    wb_in = to_aligned(cols_major(with_own(run_exchange("ag_w_in", all_gather_shards(own[:1])), own[:1])[0]))

    h1 = rowmap("premix", _premix, [xt], [norm_mix_pre, sc1, sh1], [(d, BF16)])[0]
    proj, gathered = matmul("in_proj", h1, wb_in, comm=all_gather_shards(own[1:]))
    gathered = with_own(gathered, own[1:])
    wb_ssm_up, wb_gdn_up, wb_out = rows_major(gathered[0]), rows_major(gathered[1]), rows_major(gathered[2])
    wb_up, wb_down = cols_major(gathered[3]), rows_major(gathered[4])
    xs = conv_fwd("conv_xs", _conv_silu, proj, d_inner, a_xs, cw_ssm, cb_ssm, 0)
    bm = conv_fwd("conv_bm", _conv_silu, proj, gn, a_bm, cw_ssm, cb_ssm, d_inner)
    cm = conv_fwd("conv_cm", _conv_silu, proj, gn, a_cm, cw_ssm, cb_ssm, d_inner + gn)
    q = conv_fwd("conv_q", _conv_silu_l2, proj, key, a_q, cw_gdn, cb_gdn, 0)
    k = conv_fwd("conv_k", _conv_silu_l2, proj, key, a_k, cw_gdn, cb_gdn, key)
    v = conv_fwd("conv_v", _conv_silu, proj, val, a_v, cw_gdn, cb_gdn, 2 * key)

    wide = 2 * LANES
    ssd_rows = [(xs, wide, 0, True), (bm, LANES, 0, True), (cm, LANES, 0, True), (proj, LANES, a_small, False), (proj, wide, a_z, True)]
    ssd_consts = [(lanes(ssm_dt_bias, LANE_DT), False), (lanes(ssm_A_log, LANE_DT), False), (lanes(ssm_D, LANE_DT), False),
                  (ssm_norm_w.reshape(n_grp, 1, wide), True)]
    hb_ssd, hb_gdn = min(SSD_GROUPS_PER_STEP, n_grp), min(GDN_HEADS_PER_STEP, hq)
    y_ssm_n, st_ssm = scan_fwd("ssd_fwd", ssd_step, SSM_CHUNK, n_grp, ssd_rows, ssd_consts, wide, hb_ssd)
    gdn_rows = [(q, LANES, 0, True), (k, LANES, 0, True), (v, wide, 0, True), (proj, wide, a_zg, True), (proj, LANES, a_small, False)]
    gdn_consts = [(lanes(gdn_dt_bias, LANE_A), False), (lanes(gdn_A_log, LANE_A), False), (gdn_norm_w, False)]
    y_gdn_n, st_gdn = scan_fwd("gdn_fwd", gdn_step, GDN_CHUNK, hq, gdn_rows, gdn_consts, wide, hb_gdn)

    y_ssm = matmul("ssm_up", y_ssm_n, wb_ssm_up)
    y_gdn = matmul("gdn_up", y_gdn_n, wb_gdn_up)
    gates = [(proj, d, a_gs), (proj, d, a_gg)]
    merged = rowmap("merge", _merge, gates + [y_ssm, y_gdn], [], [(d, BF16)])[0]
    u = matmul("w_out", merged, wb_out)
    post_consts = [norm_mix_post, g1, norm_mlp_pre, sc2, sh2]
    x1, h2 = rowmap("postmix", _postmix, [xt, u], post_consts, [(d, F32), (d, BF16)])
    relu2 = lambda acc: (acc, jnp.square(jnp.maximum(acc, 0.0)))
    a_up, act = matmul("mlp_up", h2, wb_up, out_dtypes=(BF16, BF16), epi=relu2)
    y2 = matmul("mlp_down", act, wb_down)

    def final_bwd(x1_, y2_, tgt, w_, g_):
        x2, vjp = jax.vjp(_final, x1_, y2_, w_, g_)
        err = x2 - tgt
        loss = 0.5 * jnp.sum(jnp.mean(err * err, axis=-1, keepdims=True), axis=0, keepdims=True)
        dx1, dy2, dw, dg = vjp(err / d)
        return dx1, dy2, loss, dw, dg

    dx1, dy2, loss_part, d_norm_mlp_post, dg2 = rowmap(
        "final", final_bwd, [x1, y2, target], [norm_mlp_post, g2], [(d, F32), (d, BF16)], [((1, 1), F32), ((1, d), F32), ((1, d), F32)])
    loss = lax.psum(loss_part[0, 0], ("x", "y", "c"))

    d_a = matmul("mlp_down_dx", dy2, wb_down, tb=True, out_dtypes=(BF16,), extras=[a_up],
                 epi=lambda acc, a: acc * 2.0 * jnp.maximum(a.astype(F32), 0.0))
    gw_down = matmul("mlp_down_dw", act, dy2, ta=True)
    dh2 = matmul("mlp_up_dx", d_a, wb_up, tb=True)
    gw_up = matmul("mlp_up_dw", h2, d_a, ta=True)

    def postmix_bwd(x_, u_, dx1_, dh2_, *cs):
        _, vjp = jax.vjp(_postmix, x_, u_, *cs)
        return vjp((dx1_, dh2_))

    dxa, du, d_norm_mix_post, dg1, d_norm_mlp_pre, dsc2, dsh2 = rowmap(
        "postmix_bwd", postmix_bwd, [xt, u, dx1, dh2], post_consts, [(d, F32), (d, BF16)], [((1, d), F32)] * 5)
    d_merged = matmul("w_out_dx", du, wb_out, tb=True)
    gw_out = matmul("w_out_dw", merged, du, ta=True)

    def merge_bwd(gs, gg, ys, yg, dm):
        _, vjp = jax.vjp(_merge, gs, gg, ys, yg)
        dgs, dgg, dys, dyg = vjp(dm)
        return dys, dyg, dgs, dgg

    dy_ssm, dy_gdn, dgs, dgg = rowmap("merge_bwd", merge_bwd, gates + [y_ssm, y_gdn, d_merged], [], [(d, BF16)] * 4)
    dy_ssm_n = matmul("ssm_up_dx", dy_ssm, wb_ssm_up, tb=True, out_dtypes=(BF16,))
    gw_ssm_up = matmul("ssm_up_dw", y_ssm_n, dy_ssm, ta=True)
    dy_gdn_n = matmul("gdn_up_dx", dy_gdn, wb_gdn_up, tb=True, out_dtypes=(BF16,))
    gw_gdn_up = matmul("gdn_up_dw", y_gdn_n, dy_gdn, ta=True)

    dxs, dbm, dcm, dsmall_ssm, dz_ssm, d_sdtb, d_salog, d_sdsk, d_snw = scan_bwd(
        "ssd_bwd", ssd_step, SSM_CHUNK, n_grp, ssd_rows, ssd_consts, st_ssm, dy_ssm_n, [BF16, BF16, BF16, F32, BF16], hb_ssd)
    dq, dk, dv, dz_gdn, dsmall_gdn, d_gdtb, d_galog, d_gnw = scan_bwd(
        "gdn_bwd", gdn_step, GDN_CHUNK, hq, gdn_rows, gdn_consts, st_gdn, dy_gdn_n, [BF16, BF16, BF16, BF16, F32], hb_gdn)

    dxs_p, dcw_xs, dcb_xs = conv_bwd("conv_xs_bwd", _conv_silu, proj, d_inner, a_xs, cw_ssm, cb_ssm, 0, dxs)
    dbm_p, dcw_bm, dcb_bm = conv_bwd("conv_bm_bwd", _conv_silu, proj, gn, a_bm, cw_ssm, cb_ssm, d_inner, dbm)
    dcm_p, dcw_cm, dcb_cm = conv_bwd("conv_cm_bwd", _conv_silu, proj, gn, a_cm, cw_ssm, cb_ssm, d_inner + gn, dcm)
    dq_p, dcw_q, _ = conv_bwd("conv_q_bwd", _conv_silu_l2, proj, key, a_q, cw_gdn, cb_gdn, 0, dq)
    dk_p, dcw_k, _ = conv_bwd("conv_k_bwd", _conv_silu_l2, proj, key, a_k, cw_gdn, cb_gdn, key, dk)
    dv_p, dcw_v, _ = conv_bwd("conv_v_bwd", _conv_silu, proj, val, a_v, cw_gdn, cb_gdn, 2 * key, dv)
    dsmall = rowmap("small_sum", lambda a, b: a + b, [dsmall_ssm, dsmall_gdn], [], [(LANES, BF16)])[0]
    dproj = jnp.concatenate([dz_ssm, dxs_p, dbm_p, dcm_p, dq_p, dk_p, dv_p, dz_gdn, dgs, dgg, dsmall,
                             jnp.zeros((t, n_al - a_small - LANES), BF16)], axis=1)
    quarters_cols = lambda g: jnp.transpose(g.reshape(g.shape[0], 4, -1), (1, 0, 2))
    quarters_rows = lambda g: g.reshape(4, g.shape[0] // 4, g.shape[1])
    rest32, rest16 = reduce_on_chip("rest", [quarters_rows(gw_ssm_up), quarters_rows(gw_gdn_up), quarters_rows(gw_out),
                                             quarters_cols(gw_up), quarters_rows(gw_down)])
    gw_in_al, rest_chips = matmul("in_proj_dw", h1, dproj, ta=True, comm=exchange_quarters_ici(rest16))
    in32, in16 = reduce_on_chip("in", [quarters_cols(from_aligned(gw_in_al))])
    dh1, in_chips = matmul("in_proj_dx", dproj, wb_in, tb=True, comm=exchange_quarters_ici(in16))

    def premix_bwd(x_, dxa_, dh1_, w_, sc_, sh_):
        _, vjp = jax.vjp(_premix, x_, w_, sc_, sh_)
        dx, dw, dsc, dsh = vjp(dh1_)
        return dx + dxa_, dw, dsc, dsh

    grad_x, d_norm_mix_pre, dsc1, dsh1 = rowmap(
        "premix_bwd", premix_bwd, [xt, dxa, dh1], [norm_mix_pre, sc1, sh1], [(d, F32)], [((1, d), F32)] * 3)

    dmod_all = gather_flat("ag_dmod", jnp.concatenate([dsh1, dsc1, dg1, dsh2, dsc2, dg2], axis=1).reshape(-1))
    dmod_q = lax.dynamic_slice_in_dim(dmod_all, quarter * n_ada, n_ada, axis=1)
    gw_ada, gb_ada = _whole(
        "ada_bwd", lambda ca, dq_, da_: (_bdot(_silu(ca), dq_, TN), jnp.sum(da_, axis=0, keepdims=True)),
        [c_all, dmod_q, dmod_all], [((d, n_ada), F32), ((1, 4 * n_ada), F32)])

    dcw_ssm = jnp.concatenate([dcw_xs, dcw_bm, dcw_cm], axis=1)
    dcb_ssm = jnp.concatenate([dcb_xs, dcb_bm, dcb_cm], axis=1)
    dcw_gdn = jnp.concatenate([dcw_q, dcw_k, dcw_v], axis=1)
    partial = [d_norm_mix_pre, d_norm_mix_post, dcw_ssm, dcb_ssm, d_sdtb[:, LANE_DT:LANE_DT + hs], d_salog[:, LANE_DT:LANE_DT + hs],
               d_sdsk[:, LANE_DT:LANE_DT + hs], d_snw, dcw_gdn, d_gdtb[:, LANE_A:LANE_A + hv], d_galog[:, LANE_A:LANE_A + hv], d_gnw,
               d_norm_mlp_pre, d_norm_mlp_post]
    sizes = [p.size for p in partial]
    stacked = gather_flat("ag_grads", jnp.concatenate([p.reshape(-1) for p in partial]))
    summed = _whole("small_sum8", lambda s: jnp.sum(s, axis=0, keepdims=True), [stacked], [((1, stacked.shape[1]), F32)])[0][0]
    offs = [0]
    for s in sizes:
        offs.append(offs[-1] + s)
    red = [summed[offs[i]:offs[i + 1]] for i in range(len(sizes))]
    my_cols = lambda full: lax.dynamic_slice_in_dim(full.reshape(CONV_K, -1), quarter * (n_cw // CONV_K), n_cw // CONV_K, axis=1)
    small_grads = {
        "b_ada": gb_ada, "norm_mix_pre": red[0], "norm_mix_post": red[1], "ssm_conv_w": my_cols(red[2]), "ssm_conv_b": red[3],
        "ssm_dt_bias": red[4], "ssm_A_log": red[5], "ssm_D": red[6], "ssm_norm_w": red[7], "gdn_conv_w": my_cols(red[8]),
        "gdn_dt_bias": red[9], "gdn_A_log": red[10], "gdn_norm_w": red[11], "norm_mlp_pre": red[12], "norm_mlp_post": red[13]}

    big_names = ["w_in", "w_ssm_up", "w_gdn_up", "w_out", "w_mlp_up", "w_mlp_down"]
    big_grads = dict(zip(big_names, reduce_across_chips(in32 + rest32, in_chips + rest_chips)))
    big_grads["w_ada"] = gw_ada

    names = ['w_ada', 'b_ada', 'norm_mix_pre', 'norm_mix_post', 'w_in', 'ssm_conv_w', 'ssm_conv_b', 'ssm_dt_bias', 'ssm_A_log', 'ssm_D',
             'ssm_norm_w', 'gdn_conv_w', 'gdn_dt_bias', 'gdn_A_log', 'gdn_norm_w', 'w_ssm_up', 'w_gdn_up', 'w_out', 'norm_mlp_pre',
             'norm_mlp_post', 'w_mlp_up', 'w_mlp_down']
    grad, delta, new_m, new_v = {}, {}, {}, {}
    for n in big_grads:
        shape = args[n].shape
        g2d = big_grads[n]
        dl, nm, nv = adamw("adamw_" + n, args[n][0], g2d, args["m_" + n][0], args["v_" + n][0])
        grad[n], delta[n], new_m[n], new_v[n] = [a.reshape(shape) for a in (g2d, dl, nm, nv)]
    small_names = [n for n in names if n not in big_grads]
    flat = lambda pre: jnp.concatenate([args[pre + n].reshape(-1) for n in small_names]).reshape(1, -1)
    g_flat = jnp.concatenate([small_grads[n].reshape(-1) for n in small_names]).reshape(1, -1)
    dl, nm, nv = adamw("adamw_small", flat(""), g_flat, flat("m_"), flat("v_"))
    off = 0
    for n in small_names:
        shape = args[n].shape
        size = args[n].size
        grad[n], delta[n], new_m[n], new_v[n] = [a[0, off:off + size].reshape(shape) for a in (g_flat, dl, nm, nv)]
        off += size

    return (loss, grad_x.reshape(x.shape), *[grad[n] for n in names], *[delta[n] for n in names],
            *[new_m[n] for n in names], *[new_v[n] for n in names])
```

```python
import functools

import jax
import jax.numpy as jnp
from jax import lax
from jax.experimental import pallas as pl
from jax.experimental.pallas import tpu as pltpu

F32 = jnp.float32
BF16 = jnp.bfloat16
MESH = pl.DeviceIdType.MESH

EPS = 1e-6
SSM_HEAD_DIM = 64
SSM_HEADS_PER_GROUP = 4
SSM_D_STATE = 128
SSM_CHUNK = 128
GDN_HEAD = 128
GDN_CHUNK = 64
CONV_K = 4
LANE_DT, LANE_B, LANE_A = 0, 32, 48
ADAM_LR, ADAM_B1, ADAM_B2, ADAM_EPS, ADAM_WD, ADAM_STEP = 0.001, 0.9, 0.999, 1e-08, 0.01, 10

VMEM_LIMIT_BYTES = 56 * 1024 * 1024
LANES = 128
N_DEV = 8

NN = (((1,), (0,)), ((), ()))
NT = (((1,), (1,)), ((), ()))
TN = (((0,), (0,)), ((), ()))


BNN = (((2,), (1,)), ((0,), (0,)))
BNT = (((2,), (2,)), ((0,), (0,)))
BTN = (((1,), (1,)), ((0,), (0,)))
_KIND = {NN: ("NN", 0), NT: ("NT", 0), TN: ("TN", 0), BNN: ("NN", 1), BNT: ("NT", 1), BTN: ("TN", 1)}
_DIMS = {"NN": (NN, BNN), "NT": (NT, BNT), "TN": (TN, BTN)}


def _dg(a, b, dims):
    return lax.dot_general(a, b, dims, preferred_element_type=F32)


def _raw_bf16(a, b, dims):
    return _dg(a.astype(BF16), b.astype(BF16), dims)


def _raw_bf16x3(a, b, dims):
    ah, bh = a.astype(BF16), b.astype(BF16)
    al, bl = (a - ah.astype(F32)).astype(BF16), (b - bh.astype(F32)).astype(BF16)
    return _dg(ah, bh, dims) + (_dg(ah, bl, dims) + _dg(al, bh, dims))


def _make_dot(raw):
    @functools.partial(jax.custom_vjp, nondiff_argnums=(2,))
    def dot(a, b, dims):
        return raw(a, b, dims)

    def fwd(a, b, dims):
        return raw(a, b, dims), (a, b)

    def bwd(dims, res, ct):
        a, b = res
        kind, batched = _KIND[dims]
        d = lambda k: _DIMS[k][batched]
        if kind == "NN":
            da, db = raw(ct, b, d("NT")), raw(a, ct, d("TN"))
        elif kind == "NT":
            da, db = raw(ct, b, d("NN")), raw(ct, a, d("TN"))
        else:
            da, db = raw(b, ct, d("NT")), raw(a, ct, d("NN"))
        return da.astype(a.dtype), db.astype(b.dtype)

    dot.defvjp(fwd, bwd)
    return lambda a, b, dims=NN: dot(a, b, dims)


_bdot = _make_dot(_raw_bf16)
_hdot = _make_dot(_raw_bf16x3)


def _mask_dot(mask, x, dims, mask_first=True):
    m = mask.astype(BF16)
    hi = x.astype(BF16)
    r = x - hi.astype(F32)
    mid = r.astype(BF16)
    lo = (r - mid.astype(F32)).astype(BF16)
    return sum(_dg(m, p, dims) if mask_first else _dg(p, m, dims) for p in (hi, mid, lo))


def _sigmoid(x):
    return 1.0 / (1.0 + jnp.exp(-x))


def _silu(x):
    return x * _sigmoid(x)


def _softplus(x):
    return jnp.maximum(x, 0.0) + jnp.log(1.0 + jnp.exp(-jnp.abs(x)))


def _rms(x, w):
    return x * lax.rsqrt(jnp.mean(x * x, axis=-1, keepdims=True) + EPS) * w


def _lane_col(m, idx):
    lane = lax.broadcasted_iota(jnp.int32, m.shape, 1)
    return jnp.sum(jnp.where(lane == idx, m, 0.0), axis=1, keepdims=True)


def _tril(n, strict=False, seg=None):
    r = lax.broadcasted_iota(jnp.int32, (n, n), 0)
    c = lax.broadcasted_iota(jnp.int32, (n, n), 1)
    low = (r > c) if strict else (r >= c)
    if seg is None or seg >= n:
        return low
    shift = seg.bit_length() - 1
    return jnp.logical_and(low, (r >> shift) == (c >> shift))


def _first_lane(shape):
    return lax.broadcasted_iota(jnp.int32, shape, len(shape) - 1) == 0


@jax.custom_vjp
def _row_form(col):
    shape = col.shape[:-1] + (LANES,)
    return _mask_dot(_first_lane(shape), jnp.broadcast_to(col, shape), NT if col.ndim == 2 else BNT)


def _row_form_fwd(col):
    return _row_form(col), None


def _row_form_bwd(_, ct):
    shape = ct.shape[:-1] + (LANES,)
    sums = _mask_dot(_first_lane(shape), ct, TN if ct.ndim == 2 else BTN, mask_first=False)
    return (jnp.sum(sums, axis=-1, keepdims=True),)


_row_form.defvjp(_row_form_fwd, _row_form_bwd)


@functools.partial(jax.custom_vjp, nondiff_argnums=(1,))
def _cumsum_rows(x, seg):
    return _mask_dot(_tril(x.shape[0], seg=seg), x, NN)


def _cumsum_rows_fwd(x, seg):
    return _cumsum_rows(x, seg), None


def _cumsum_rows_bwd(seg, _, ct):
    return (_mask_dot(_tril(ct.shape[0], seg=seg), ct, TN),)


_cumsum_rows.defvjp(_cumsum_rows_fwd, _cumsum_rows_bwd)


def _head_rows(m_t, idx):
    sub = lax.broadcasted_iota(jnp.int32, m_t.shape, 0)
    return jnp.sum(jnp.where(sub == idx, m_t, 0.0), axis=0, keepdims=True)


def _params(sem):
    return pltpu.CompilerParams(dimension_semantics=sem, vmem_limit_bytes=VMEM_LIMIT_BYTES)


def blockmap(name, fn, grid, ins, outs, accs=(), scalars=None):
    n_in, n_out, n_acc = len(ins), len(outs), len(accs)
    n_grid = len(grid)
    n_pre = 0 if scalars is None else 1

    def body(*refs):
        refs = refs[n_pre:]
        vals = fn(*[r[...] for r in refs[:n_in]])
        if not isinstance(vals, (tuple, list)):
            vals = (vals,)
        for r, v in zip(refs[n_in:n_in + n_out], vals[:n_out]):
            r[...] = v.astype(r.dtype)
        if n_acc:
            first = functools.reduce(jnp.logical_and, [pl.program_id(a) == 0 for a in range(n_grid)])
            acc_refs = refs[n_in + n_out:]

            @pl.when(first)
            def _():
                for r in acc_refs:
                    r[...] = jnp.zeros(r.shape, r.dtype)

            for r, v in zip(acc_refs, vals[n_out:]):
                r[...] += v.astype(r.dtype)

    zeros = lambda nd: (lambda *_: (0,) * nd)
    in_specs = [pl.BlockSpec(b, im) for _, b, im in ins]
    out_specs = [pl.BlockSpec(b, im) for _, _, b, im in outs] + [pl.BlockSpec(s, zeros(len(s))) for s, _ in accs]
    out_shape = [jax.ShapeDtypeStruct(s, d) for s, d, _, _ in outs] + [jax.ShapeDtypeStruct(s, d) for s, d in accs]
    cparams = _params(("arbitrary",) * n_grid if n_acc else ("parallel",) * n_grid)
    arrays = [a for a, _, _ in ins]
    if scalars is None:
        return pl.pallas_call(body, name=name, grid=grid, in_specs=in_specs, out_specs=out_specs, out_shape=out_shape,
                              compiler_params=cparams)(*arrays)
    spec = pltpu.PrefetchScalarGridSpec(num_scalar_prefetch=1, grid=grid, in_specs=in_specs, out_specs=out_specs)
    return pl.pallas_call(body, name=name, grid_spec=spec, out_shape=out_shape, compiler_params=cparams)(scalars, *arrays)


def rowmap(name, fn, rows, consts, outs, accs=(), rb=256):
    norm = [(r, r.shape[1], 0) if not isinstance(r, tuple) else (r[0], r[1], r[2] // r[1]) for r in rows]
    assert all(not isinstance(r, tuple) or r[2] % r[1] == 0 for r in rows)
    t = norm[0][0].shape[0]
    rb = min(rb, t)
    ins = [(a, (rb, n), (lambda i, cb=cb: (i, cb))) for a, n, cb in norm]
    ins += [(cst, cst.shape, (lambda i, nd=cst.ndim: (0,) * nd)) for cst in consts]
    o = [((t, n), d, (rb, n), lambda i: (i, 0)) for n, d in outs]
    return blockmap(name, fn, (t // rb,), ins, o, accs)


MM_TILE_M, MM_TILE_N, MM_TILE_K = 1024, 1024, 2048


def _tile(dim, cap):
    if dim <= cap:
        return dim
    best = max(t for t in range(LANES, cap + 1, LANES) if dim % t == 0)
    return best


def matmul(name, a, b, ta=False, tb=False, out_dtypes=(F32,), epi=None, extras=(), comm=None):
    (k_dim, m_dim) = a.shape if ta else a.shape[::-1]
    n_dim = b.shape[0] if tb else b.shape[1]
    assert (b.shape[1] if tb else b.shape[0]) == k_dim, (name, a.shape, b.shape)
    tm, tn, tk = _tile(m_dim, MM_TILE_M), _tile(n_dim, MM_TILE_N), _tile(k_dim, MM_TILE_K)
    grid = (m_dim // tm, n_dim // tn, k_dim // tk)
    k_steps = grid[2]
    n_extra, n_out = len(extras), len(out_dtypes)
    n_cin, n_cout = (len(comm.operands), len(comm.out_shapes)) if comm else (0, 0)
    dims = (((0 if ta else 1,), (1 if tb else 0,)), ((), ()))

    def body(*refs):
        ins, outs, scratch = refs[:2 + n_extra + n_cin], refs[2 + n_extra + n_cin:][:n_out + n_cout], refs[2 + n_extra + n_cin + n_out + n_cout:]
        extra_refs, out_refs = ins[2:2 + n_extra], outs[:n_out]
        ids = [pl.program_id(ax) for ax in range(3)]
        if comm:
            comm_refs = (ins[2 + n_extra:], outs[n_out:], scratch[-2], scratch[-1])

            @pl.when(functools.reduce(jnp.logical_and, [i == 0 for i in ids]))
            def _():
                comm.start(*comm_refs)

        def finish(acc):
            vals = (acc,) if epi is None else epi(acc, *[r[...] for r in extra_refs])
            if not isinstance(vals, (tuple, list)):
                vals = (vals,)
            for r, v in zip(out_refs, vals):
                r[...] = v.astype(r.dtype)

        prod = lax.dot_general(ins[0][...].astype(BF16), ins[1][...].astype(BF16), dims, preferred_element_type=F32)
        if k_steps == 1:
            finish(prod)
        else:
            acc_ref = scratch[0]

            @pl.when(ids[2] == 0)
            def _():
                acc_ref[...] = jnp.zeros(acc_ref.shape, F32)

            acc_ref[...] += prod

            @pl.when(ids[2] == k_steps - 1)
            def _():
                finish(acc_ref[...])

        if comm:
            @pl.when(functools.reduce(jnp.logical_and, [i == g - 1 for i, g in zip(ids, grid)]))
            def _():
                comm.finish(*comm_refs)

    a_spec = pl.BlockSpec((tk, tm), lambda i, j, k: (k, i)) if ta else pl.BlockSpec((tm, tk), lambda i, j, k: (i, k))
    b_spec = pl.BlockSpec((tn, tk), lambda i, j, k: (j, k)) if tb else pl.BlockSpec((tk, tn), lambda i, j, k: (k, j))
    mn_spec = pl.BlockSpec((tm, tn), lambda i, j, k: (i, j))
    scratch_shapes = [] if k_steps == 1 else [pltpu.VMEM((tm, tn), F32)]
    if comm:
        scratch_shapes += [pltpu.SemaphoreType.DMA(comm.sem_shape), pltpu.SemaphoreType.DMA(comm.sem_shape)]
    res = pl.pallas_call(
        body, name=name, grid=grid,
        in_specs=[a_spec, b_spec] + [mn_spec] * n_extra + [ANY] * n_cin,
        out_specs=[mn_spec] * n_out + [ANY] * n_cout,
        out_shape=[jax.ShapeDtypeStruct((m_dim, n_dim), d) for d in out_dtypes] + (comm.out_shapes if comm else []),
        scratch_shapes=scratch_shapes,
        compiler_params=_params(("arbitrary",) * 3 if comm else ("parallel", "parallel", "arbitrary")),
    )(a, b, *extras, *(comm.operands if comm else []))
    main = res[:n_out] if n_out > 1 else res[0]
    return (main, list(res[n_out:])) if comm else main


def ssd_step(g0, state, xs, bm, cm, small, z, p_dtb, p_alog, p_dsk, nw):
    hb, n = state.shape[0], xs.shape[0]
    n_pair, n_head = 2 * hb, 4 * hb
    causal = _tril(n)
    dt_all = _softplus(small + p_dtb)
    a_all = dt_all * (-jnp.exp(p_alog))
    acum_all = _cumsum_rows(a_all, n)
    acum_t = acum_all.T
    lane0 = LANE_DT + SSM_HEADS_PER_GROUP * g0
    sub = lax.broadcasted_iota(jnp.int32, acum_t.shape, 0)
    heads = range(n_head)
    acum = jnp.stack([_lane_col(acum_all, lane0 + i) for i in heads])
    acum_row = jnp.stack([jnp.sum(jnp.where(sub == lane0 + i, acum_t, 0.0), axis=0, keepdims=True) for i in heads])
    dt = jnp.stack([_lane_col(dt_all, lane0 + i) for i in heads])
    dsk = jnp.stack([_lane_col(p_dsk, lane0 + i) for i in heads])
    decay = jnp.exp(jnp.where(causal, acum - acum_row, -jnp.inf))
    a_last = acum[:, n - 1:n, :]

    def split(a):
        return [a[:, i * LANES:(i + 1) * LANES] for i in range(a.shape[1] // LANES)]

    def pairs(a, axis=2):
        even = jnp.stack([a[2 * p] for p in range(n_pair)])
        odd = jnp.stack([a[2 * p + 1] for p in range(n_pair)])
        shape = (n_pair, LANES, LANES) if axis == 1 else (n_pair, a.shape[1], LANES)
        return jnp.where(lax.broadcasted_iota(jnp.int32, shape, axis) < SSM_HEAD_DIM, even, odd)

    bms, cms = split(bm), split(cm)
    cb = _bdot(jnp.stack(cms), jnp.stack(bms), BNT)
    cbd = jnp.stack([cb[i // SSM_HEADS_PER_GROUP] for i in heads]) * decay
    xp = jnp.stack(split(xs))
    xdt = xp * pairs(dt)
    yd = _bdot(cbd, jnp.stack([xdt[i // 2] for i in heads]), BNN)
    lane = lax.broadcasted_iota(jnp.int32, (n_pair, n, LANES), 2)
    y_diag = jnp.where(lane < SSM_HEAD_DIM, jnp.stack([yd[2 * p] for p in range(n_pair)]), jnp.stack([yd[2 * p + 1] for p in range(n_pair)]))
    st = state.reshape(n_pair, LANES, LANES)
    cm2 = jnp.stack([cms[p // 2] for p in range(n_pair)])
    bm2 = jnp.stack([bms[p // 2] for p in range(n_pair)])
    y_off = _bdot(cm2, st, BNT) * pairs(jnp.exp(acum))
    new = st * pairs(jnp.exp(a_last), axis=1) + _bdot(xdt * pairs(jnp.exp(a_last - acum)), bm2, BTN)
    y = y_diag + y_off + pairs(dsk) * xp
    y = jnp.concatenate([y[p] for p in range(n_pair)], axis=1) * _silu(z)
    wide = 2 * LANES
    y = jnp.concatenate([_rms(y[:, i * wide:(i + 1) * wide], nw[i]) for i in range(hb)], axis=1)
    return new.reshape(state.shape), y


@functools.partial(jax.custom_vjp, nondiff_argnums=(1,))
def _unit_lower_inverse(a, seg):
    n = a.shape[-1]
    r = lax.broadcasted_iota(jnp.int32, (n, n), 0)
    c = lax.broadcasted_iota(jnp.int32, (n, n), 1)
    shift = min(INVERSE_BASE, seg).bit_length() - 1
    power = jnp.where((r >> shift) == (c >> shift), a, 0.0)
    inv = (r == c).astype(F32) - power
    span = 2
    while span < (1 << shift):
        power = _hdot(power, power, BNN)
        inv = inv + _hdot(inv, power, BNN)
        span *= 2
    while (1 << shift) < seg:
        below = jnp.logical_and((r >> (shift + 1)) == (c >> (shift + 1)), (r >> shift) != (c >> shift))
        inv = inv - _hdot(inv, _hdot(jnp.where(below, a, 0.0), inv, BNN), BNN)
        shift += 1
    return inv


def _unit_lower_inverse_fwd(a, seg):
    inv = _unit_lower_inverse(a, seg)
    return inv, inv


def _unit_lower_inverse_bwd(seg, inv, ct):
    return (-_hdot(_hdot(inv, ct, BTN), inv, BNT),)


_unit_lower_inverse.defvjp(_unit_lower_inverse_fwd, _unit_lower_inverse_bwd)


def gdn_step(hq0, state, q, k, v, z, small, p_dtb, p_alog, nw):
    n, chunk = q.shape[0], GDN_CHUNK
    hb = state.shape[0]
    nb = 2 * hb
    cur = state.reshape(nb, LANES, LANES)
    causal, strict = _tril(n, seg=chunk), _tril(n, True, seg=chunk)
    beta_all = _sigmoid(small)
    g_all = -jnp.exp(p_alog) * _softplus(small + p_dtb)
    gcum_all = _cumsum_rows(g_all, chunk)
    gcum_t = gcum_all.T
    split = lambda a: [a[:, i * LANES:(i + 1) * LANES] for i in range(a.shape[1] // LANES)]
    per_value_head = lambda a: jnp.stack([a[i // 2] for i in range(nb)])
    qh, kh = jnp.stack(split(q)) * (GDN_HEAD ** -0.5), jnp.stack(split(k))
    q2, k2 = per_value_head(qh), per_value_head(kh)
    v2, z2 = jnp.stack(split(v)), jnp.stack(split(z))
    gcum = jnp.stack([_lane_col(gcum_all, LANE_A + 2 * hq0 + i) for i in range(nb)])
    gcum_row = jnp.stack([_head_rows(gcum_t, LANE_A + 2 * hq0 + i) for i in range(nb)])
    beta = jnp.stack([_lane_col(beta_all, LANE_B + 2 * hq0 + i) for i in range(nb)])
    dmat = jnp.exp(jnp.where(causal, gcum - gcum_row, -jnp.inf))
    a_low = jnp.where(strict, beta * per_value_head(_bdot(kh, kh, BNT)) * dmat, 0.0)
    inv = _unit_lower_inverse(a_low, chunk)
    egc = jnp.exp(gcum)
    u = _hdot(inv, v2 * beta, BNN)
    w = _hdot(inv, k2 * (beta * egc), BNN)
    q_dec = q2 * egc
    v_new, o_state = [], []
    for s in range(n // chunk):
        rows = slice(s * chunk, (s + 1) * chunk)
        v_new.append(u[:, rows] - _bdot(w[:, rows], cur, BNN))
        o_state.append(_bdot(q_dec[:, rows], cur, BNN))
        g_last = gcum[:, (s + 1) * chunk - 1:(s + 1) * chunk, :]
        k_dec = k2[:, rows] * jnp.exp(g_last - gcum[:, rows])
        cur = cur * jnp.exp(g_last) + _bdot(k_dec, v_new[-1], BTN)
    o = jnp.concatenate(o_state, axis=1) + _bdot(per_value_head(_bdot(qh, kh, BNT)) * dmat, jnp.concatenate(v_new, axis=1), BNN)
    out = _rms(o, nw) * _silu(z2)
    return cur.reshape(state.shape), jnp.concatenate([out[i] for i in range(nb)], axis=1)


STATE_SHAPE = (2, LANES, LANES)
SSD_GROUPS_PER_STEP = 4
GDN_HEADS_PER_STEP = 4
GDN_CHUNKS_PER_STEP = 2
INVERSE_BASE = 16


def _scan_specs(rows, consts, chunk, chunk_of, hb):
    specs = []
    for _, n, off, per_group in rows:
        if per_group:
            assert off % (n * hb) == 0
            specs.append(pl.BlockSpec((chunk, n * hb), lambda c, g, cb=off // (n * hb): (chunk_of(c), cb + g)))
        else:
            assert off % n == 0
            specs.append(pl.BlockSpec((chunk, n), lambda c, g, cb=off // n: (chunk_of(c), cb)))
    for arr, per_group in consts:
        if per_group:
            specs.append(pl.BlockSpec((hb, 1, arr.shape[2]), lambda c, g: (g, 0, 0)))
        else:
            specs.append(pl.BlockSpec(arr.shape, lambda c, g, nd=arr.ndim: (0,) * nd))
    return specs


def scan_fwd(name, step, chunk, n_grp, rows, consts, out_cols, hb):
    t = rows[0][0].shape[0]
    nc = t // chunk
    n_rows, n_consts = len(rows), len(consts)

    def body(*refs):
        row_refs, const_refs = refs[:n_rows], refs[n_rows:n_rows + n_consts]
        y_ref, st_ref, state = refs[n_rows + n_consts:]
        c, g = pl.program_id(0), pl.program_id(1)

        @pl.when(c == 0)
        def _():
            state[g] = jnp.zeros((hb,) + STATE_SHAPE, F32)

        st = state[g]
        st_ref[...] = st
        new, y = step(g * hb, st, *[r[...] for r in row_refs], *[r[...] for r in const_refs])
        state[g] = new
        y_ref[...] = y.astype(y_ref.dtype)

    return pl.pallas_call(
        body, name=name, grid=(nc, n_grp // hb),
        in_specs=_scan_specs(rows, consts, chunk, lambda c: c, hb),
        out_specs=[pl.BlockSpec((chunk, out_cols * hb), lambda c, g: (c, g)),
                   pl.BlockSpec((None, None, hb) + STATE_SHAPE, lambda c, g: (c, g, 0, 0, 0, 0))],
        out_shape=[jax.ShapeDtypeStruct((t, n_grp * out_cols), BF16),
                   jax.ShapeDtypeStruct((nc, n_grp // hb, hb) + STATE_SHAPE, F32)],
        scratch_shapes=[pltpu.VMEM((n_grp // hb, hb) + STATE_SHAPE, F32)],
        compiler_params=_params(("arbitrary", "arbitrary")),
    )(*[r[0] for r in rows], *[c[0] for c in consts])


def scan_bwd(name, step, chunk, n_grp, rows, consts, states, dy, row_dtypes, hb):
    t = rows[0][0].shape[0]
    nc = t // chunk
    n_rows, n_consts = len(rows), len(consts)
    out_cols = dy.shape[1] // n_grp

    def body(*refs):
        row_refs, const_refs = refs[:n_rows], refs[n_rows:n_rows + n_consts]
        st_ref, dy_ref = refs[n_rows + n_consts:n_rows + n_consts + 2]
        outs = refs[n_rows + n_consts + 2:-1]
        dstate = refs[-1]
        c, g = pl.program_id(0), pl.program_id(1)

        @pl.when(c == 0)
        def _():
            dstate[g] = jnp.zeros((hb,) + STATE_SHAPE, F32)

        @pl.when(jnp.logical_and(c == 0, g == 0))
        def _():
            for r in outs[n_rows:]:
                r[...] = jnp.zeros(r.shape, r.dtype)

        _, vjp = jax.vjp(functools.partial(step, g * hb), st_ref[...], *[r[...] for r in row_refs], *[r[...] for r in const_refs])
        grads = vjp((dstate[g], dy_ref[...].astype(F32)))
        dstate[g] = grads[0]
        for (_, _, _, per_group), r, d in zip(rows, outs[:n_rows], grads[1:1 + n_rows]):
            if per_group:
                r[...] = d.astype(r.dtype)
            else:
                @pl.when(g == 0)
                def _(r=r):
                    r[...] = jnp.zeros(r.shape, r.dtype)

                r[...] += d.astype(r.dtype)
        for (_, per_group), r, d in zip(consts, outs[n_rows:], grads[1 + n_rows:]):
            if per_group:
                r[pl.ds(g * hb, hb)] += d
            else:
                r[...] += d

    rev = lambda c: nc - 1 - c
    out_specs, out_shape = [], []
    for (_, n, _, per_group), dt in zip(rows, row_dtypes):
        if per_group:
            out_specs.append(pl.BlockSpec((chunk, n * hb), lambda c, g: (rev(c), g)))
            out_shape.append(jax.ShapeDtypeStruct((t, n_grp * n), dt))
        else:
            out_specs.append(pl.BlockSpec((chunk, n), lambda c, g: (rev(c), 0)))
            out_shape.append(jax.ShapeDtypeStruct((t, n), dt))
    for arr, _ in consts:
        out_specs.append(pl.BlockSpec(arr.shape, lambda c, g, nd=arr.ndim: (0,) * nd))
        out_shape.append(jax.ShapeDtypeStruct(arr.shape, F32))
    return pl.pallas_call(
        body, name=name, grid=(nc, n_grp // hb),
        in_specs=_scan_specs(rows, consts, chunk, rev, hb)
        + [pl.BlockSpec((None, None, hb) + STATE_SHAPE, lambda c, g: (rev(c), g, 0, 0, 0, 0)),
           pl.BlockSpec((chunk, out_cols * hb), lambda c, g: (rev(c), g))],
        out_specs=out_specs, out_shape=out_shape,
        scratch_shapes=[pltpu.VMEM((n_grp // hb, hb) + STATE_SHAPE, F32)],
        compiler_params=_params(("arbitrary", "arbitrary")),
    )(*[r[0] for r in rows], *[c[0] for c in consts], states, dy)


@functools.partial(jax.custom_vjp, nondiff_argnums=(1,))
def _shift_rows(x, k):
    t = x.shape[0]
    row = lax.broadcasted_iota(jnp.int32, x.shape, 0)
    rolled = pltpu.roll(x, k % t, 0)
    return jnp.where(jnp.logical_and(row >= k, row < t + k), rolled, 0.0)


def _shift_rows_fwd(x, k):
    return _shift_rows(x, k), None


def _shift_rows_bwd(k, _, dy):
    return (_shift_rows(dy, -k),)


_shift_rows.defvjp(_shift_rows_fwd, _shift_rows_bwd)


def _conv_silu(x, cw, cb):
    pre = cb + sum(cw[j:j + 1, :] * _shift_rows(x, CONV_K - 1 - j) for j in range(CONV_K))
    return _silu(pre)


def _conv_silu_l2(x, cw, cb):
    y = _conv_silu(x, cw, cb)
    return y * lax.rsqrt(jnp.sum(y * y, axis=-1, keepdims=True) + EPS)


def conv_fwd(name, fn, src, n, off, cw, cb, cw_off):
    t = src.shape[0]
    sb, wb = off // LANES, cw_off // LANES
    ins = [(src, (t, LANES), lambda i: (0, sb + i)), (cw, (CONV_K, LANES), lambda i: (0, wb + i)),
           (cb, (1, LANES), lambda i: (0, wb + i))]
    return blockmap(name, fn, (n // LANES,), ins, [((t, n), F32, (t, LANES), lambda i: (0, i))])[0]


def conv_bwd(name, fn, src, n, off, cw, cb, cw_off, dy):
    t = src.shape[0]
    sb, wb = off // LANES, cw_off // LANES

    def bwd(x, w, b, d):
        _, vjp = jax.vjp(fn, x, w, b)
        return vjp(d.astype(F32))

    ins = [(src, (t, LANES), lambda i: (0, sb + i)), (cw, (CONV_K, LANES), lambda i: (0, wb + i)),
           (cb, (1, LANES), lambda i: (0, wb + i)), (dy, (t, LANES), lambda i: (0, i))]
    outs = [((t, n), BF16, (t, LANES), lambda i: (0, i)), ((CONV_K, n), F32, (CONV_K, LANES), lambda i: (0, i)),
            ((1, n), F32, (1, LANES), lambda i: (0, i))]
    return blockmap(name, bwd, (n // LANES,), ins, outs)


def _place():
    return lax.axis_index("x"), lax.axis_index("y"), lax.axis_index("c")


def _other_chips(x, y):
    return [(1 - x, y), (x, 1 - y), (1 - x, 1 - y)]


ANY = pl.BlockSpec(memory_space=pl.ANY)


def all_gather8(name, v):
    m_per, n = v.shape

    def body(x_ref, out_ref, send_sems, recv_sems, local_sem):
        x, y, c = _place()
        me, sibling = (x, y, c), (x, y, 1 - c)
        chips = _other_chips(x, y)

        def rows(px, py, pc):
            return out_ref.at[pl.ds((4 * px + 2 * py + pc) * m_per, m_per), :]

        def copy(k, block, to, src=None):
            return pltpu.make_async_remote_copy(
                src_ref=rows(*block) if src is None else src, dst_ref=rows(*block),
                send_sem=send_sems.at[k], recv_sem=recv_sems.at[k], device_id=to, device_id_type=MESH)

        mine = pltpu.make_async_copy(x_ref, rows(*me), local_sem)
        mine.start()
        first = [copy(0, me, sibling, src=x_ref)]
        first += [copy(1 + q, me, (*chip, c), src=x_ref) for q, chip in enumerate(chips)]
        for cp in first:
            cp.start()
        passed = [copy(4 + q, (*chip, c), sibling) for q, chip in enumerate(chips)]
        for q, chip in enumerate(chips):
            copy(1 + q, (*chip, c), me).wait_recv()
            passed[q].start()
        copy(0, sibling, me).wait_recv()
        for q, chip in enumerate(chips):
            copy(4 + q, (*chip, 1 - c), me).wait_recv()
        for cp in first + passed:
            cp.wait_send()
        mine.wait()

    return pl.pallas_call(
        body, name=name, out_shape=jax.ShapeDtypeStruct((N_DEV * m_per, n), v.dtype),
        in_specs=[pl.BlockSpec(memory_space=pltpu.VMEM)], out_specs=pl.BlockSpec(memory_space=pltpu.VMEM),
        scratch_shapes=[pltpu.SemaphoreType.DMA((7,)), pltpu.SemaphoreType.DMA((7,)), pltpu.SemaphoreType.DMA],
    )(v)


def gather_flat(name, vec):
    n = vec.shape[0]
    n_pad = -(-n // (8 * LANES)) * (8 * LANES)
    v = jnp.pad(vec, (0, n_pad - n)).reshape(8, n_pad // 8)
    return all_gather8(name, v).reshape(N_DEV, n_pad)[:, :n]


class Exchange:
    def __init__(self, operands, out_shapes, sem_shape, start, finish):
        self.operands, self.out_shapes, self.sem_shape, self.start, self.finish = list(operands), out_shapes, sem_shape, start, finish


def _start_all_wait_all(make_copies):
    def start(*refs):
        for cp in make_copies(*refs):
            cp.start()

    def finish(*refs):
        for cp in make_copies(*refs):
            cp.wait()

    return start, finish


def run_exchange(name, ex):
    n_in, n_out = len(ex.operands), len(ex.out_shapes)

    def body(*refs):
        ins, outs = refs[:n_in], refs[n_in:n_in + n_out]
        ex.start(ins, outs, *refs[n_in + n_out:])
        ex.finish(ins, outs, *refs[n_in + n_out:])

    return pl.pallas_call(
        body, name=name, out_shape=ex.out_shapes, in_specs=[ANY] * n_in, out_specs=[ANY] * n_out,
        scratch_shapes=[pltpu.SemaphoreType.DMA(ex.sem_shape), pltpu.SemaphoreType.DMA(ex.sem_shape)],
    )(*ex.operands)


def all_gather_shards(shards):
    n_t = len(shards)

    def copies(ins, outs, send_sems, recv_sems):
        x, y, c = _place()
        chips = _other_chips(x, y)

        def copy(t, k, quarter, pc, to, src=None):
            h = ins[t].shape[0] // 2
            dst = outs[t].at[quarter, pl.ds(pc * h, h)]
            return pltpu.make_async_remote_copy(
                src_ref=dst if src is None else ins[t].at[pl.ds(pc * h, h)], dst_ref=dst,
                send_sem=send_sems.at[t, k], recv_sem=recv_sems.at[t, k], device_id=to, device_id_type=MESH)

        first = [copy(t, q, 2 * x + y, c, (*chip, c), src=True) for t in range(n_t) for q, chip in enumerate(chips)]
        landed = [copy(t, q, 2 * px + py, c, (x, y, c)) for t in range(n_t) for q, (px, py) in enumerate(chips)]
        passed = [copy(t, 3 + q, 2 * px + py, c, (x, y, 1 - c)) for t in range(n_t) for q, (px, py) in enumerate(chips)]
        from_sibling = [copy(t, 3 + q, 2 * px + py, 1 - c, (x, y, c)) for t in range(n_t) for q, (px, py) in enumerate(chips)]
        return first, landed, passed, from_sibling

    def start(*refs):
        for cp in copies(*refs)[0]:
            cp.start()

    def finish(*refs):
        first, landed, passed, from_sibling = copies(*refs)
        for arrived, onward in zip(landed, passed):
            arrived.wait_recv()
            onward.start()
        for cp in from_sibling:
            cp.wait_recv()
        for cp in first + passed:
            cp.wait_send()

    return Exchange(shards, [jax.ShapeDtypeStruct((4,) + s.shape, s.dtype) for s in shards], (n_t, 6), start, finish)


def exchange_halves_d2d(grads):
    n_t = len(grads)

    def copies(ins, outs, send_sems, recv_sems):
        x, y, c = _place()
        return [pltpu.make_async_remote_copy(
            src_ref=ins[t].at[:, pl.ds((1 - c) * (ins[t].shape[1] // 2), ins[t].shape[1] // 2), :], dst_ref=outs[t],
            send_sem=send_sems.at[t], recv_sem=recv_sems.at[t], device_id=(x, y, 1 - c), device_id_type=MESH) for t in range(n_t)]

    shapes = [jax.ShapeDtypeStruct((4, g.shape[1] // 2, g.shape[2]), g.dtype) for g in grads]
    return Exchange(grads, shapes, (n_t,), *_start_all_wait_all(copies))


def exchange_quarters_ici(parts):
    n_t = len(parts)

    def copies(ins, outs, send_sems, recv_sems):
        x, y, c = _place()
        return [pltpu.make_async_remote_copy(
            src_ref=ins[t].at[2 * px + py], dst_ref=outs[t].at[q],
            send_sem=send_sems.at[t, q], recv_sem=recv_sems.at[t, q], device_id=(px, py, c), device_id_type=MESH)
            for t in range(n_t) for q, (px, py) in enumerate(_other_chips(x, y))]

    shapes = [jax.ShapeDtypeStruct((3,) + p.shape[1:], p.dtype) for p in parts]
    return Exchange(parts, shapes, (n_t, 3), *_start_all_wait_all(copies))


def swap_d2d(halves):
    n_t = len(halves)

    def copies(ins, outs, send_sems, recv_sems):
        x, y, c = _place()
        return [pltpu.make_async_remote_copy(
            src_ref=ins[t], dst_ref=outs[t], send_sem=send_sems.at[t], recv_sem=recv_sems.at[t],
            device_id=(x, y, 1 - c), device_id_type=MESH) for t in range(n_t)]

    return Exchange(halves, [jax.ShapeDtypeStruct(h.shape, h.dtype) for h in halves], (n_t,), *_start_all_wait_all(copies))


BLOCK_BYTES = 1 << 20


def _row_block(r, c):
    rb = r
    while rb * c * 4 > BLOCK_BYTES and rb % 16 == 0:
        rb //= 2
    return rb


def _place_scalars():
    x, y, c = _place()
    return jnp.stack([c, 2 * x + y]).astype(jnp.int32)


def reduce_on_chip(tag, grads):
    from_sibling = run_exchange(f"rs_d2d_{tag}", exchange_halves_d2d(grads))
    parts, parts_bf16 = [], []
    for t, (g, s) in enumerate(zip(grads, from_sibling)):
        _, h, cols = s.shape
        rb = _row_block(h, cols)
        nb = h // rb
        blk = lambda k, i, s_ref: (k, i, 0)
        p32, p16 = blockmap(
            f"rs_add_{tag}{t}", lambda a, b: (a + b, a + b), (4, nb),
            [(g, (None, rb, cols), lambda k, i, s_ref, nb=nb: (k, s_ref[0] * nb + i, 0)), (s, (None, rb, cols), blk)],
            [(s.shape, F32, (None, rb, cols), blk), (s.shape, BF16, (None, rb, cols), blk)], scalars=_place_scalars())
        parts.append(p32)
        parts_bf16.append(p16)
    return parts, parts_bf16


def reduce_across_chips(parts, from_chips):
    halves = []
    for t, (p, q) in enumerate(zip(parts, from_chips)):
        _, h, cols = p.shape
        rb = _row_block(h, cols)
        halves.append(blockmap(
            f"rs_sum{t}", lambda a, b: a + b[0].astype(F32) + b[1].astype(F32) + b[2].astype(F32), (h // rb,),
            [(p, (None, rb, cols), lambda i, s_ref: (s_ref[1], i, 0)), (q, (3, rb, cols), lambda i, s_ref: (0, i, 0))],
            [((h, cols), F32, (rb, cols), lambda i, s_ref: (i, 0))], scalars=_place_scalars())[0])
    others = run_exchange("rs_swap", swap_d2d(halves))
    south = lax.axis_index("c") == 0
    return [jnp.concatenate([lax.select(south, mine, other), lax.select(south, other, mine)], axis=0)
            for mine, other in zip(halves, others)]


def _adamw(w, g, m, v):
    m = ADAM_B1 * m + (1.0 - ADAM_B1) * g
    v = ADAM_B2 * v + (1.0 - ADAM_B2) * jnp.square(g)
    m_hat = m / (1.0 - ADAM_B1 ** ADAM_STEP)
    v_hat = v / (1.0 - ADAM_B2 ** ADAM_STEP)
    delta = -ADAM_LR * (m_hat / (jnp.sqrt(v_hat) + ADAM_EPS) + ADAM_WD * w)
    return delta, m, v


def adamw(name, w, g, m, v):
    r, c = w.shape
    rb = _row_block(r, c)
    blk = lambda a: (a, (rb, c), lambda i: (i, 0))
    return blockmap(name, _adamw, (r // rb,), [blk(w), blk(g), blk(m), blk(v)], [((r, c), F32, (rb, c), lambda i: (i, 0))] * 3)


def _whole(name, fn, ins, outs):
    return blockmap(name, fn, (1,), [(a, a.shape, lambda i, nd=a.ndim: (0,) * nd) for a in ins],
                    [(s, d, s, lambda i, nd=len(s): (0,) * nd) for s, d in outs])


def _premix(x, w, sc, sh):
    return _rms(x, w) * (1.0 + sc) + sh


def _postmix(x, u, w_post, g1, w_pre2, sc2, sh2):
    x1 = x + g1 * _rms(u, w_post)
    return x1, _premix(x1, w_pre2, sc2, sh2)


def _merge(gs, gg, ys, yg):
    return _sigmoid(gs) * ys + _sigmoid(gg) * yg


def _final(x1, y2, w_post2, g2):
    return x1 + g2 * _rms(y2, w_post2)


def kernel(x, c, w_ada, b_ada, norm_mix_pre, norm_mix_post, w_in, ssm_conv_w, ssm_conv_b, ssm_dt_bias, ssm_A_log, ssm_D, ssm_norm_w, gdn_conv_w, gdn_dt_bias, gdn_A_log, gdn_norm_w, w_ssm_up, w_gdn_up, w_out, norm_mlp_pre, norm_mlp_post, w_mlp_up, w_mlp_down, loss_target, m_w_ada, m_b_ada, m_norm_mix_pre, m_norm_mix_post, m_w_in, m_ssm_conv_w, m_ssm_conv_b, m_ssm_dt_bias, m_ssm_A_log, m_ssm_D, m_ssm_norm_w, m_gdn_conv_w, m_gdn_dt_bias, m_gdn_A_log, m_gdn_norm_w, m_w_ssm_up, m_w_gdn_up, m_w_out, m_norm_mlp_pre, m_norm_mlp_post, m_w_mlp_up, m_w_mlp_down, v_w_ada, v_b_ada, v_norm_mix_pre, v_norm_mix_post, v_w_in, v_ssm_conv_w, v_ssm_conv_b, v_ssm_dt_bias, v_ssm_A_log, v_ssm_D, v_ssm_norm_w, v_gdn_conv_w, v_gdn_dt_bias, v_gdn_A_log, v_gdn_norm_w, v_w_ssm_up, v_w_gdn_up, v_w_out, v_norm_mlp_pre, v_norm_mlp_post, v_w_mlp_up, v_w_mlp_down):
    args = dict(locals())
    xi, yi, ci = _place()
    quarter = 2 * xi + yi
    batch = 4 * xi + 2 * yi + ci

    xt, target = x[0], loss_target[0]
    t, d = xt.shape
    hs, hv = ssm_dt_bias.shape[-1], gdn_dt_bias.shape[-1]
    d_inner = hs * SSM_HEAD_DIM
    n_grp = hs // SSM_HEADS_PER_GROUP
    gn = n_grp * SSM_D_STATE
    conv_ssm = d_inner + 2 * gn
    hq = hv // 2
    key, val = hq * GDN_HEAD, hv * GDN_HEAD
    conv_gdn = 2 * key + val
    hidden = 4 * w_mlp_up.shape[-1]
    o_dt = d_inner + conv_ssm
    o_qkv = o_dt + hs
    o_b = o_qkv + conv_gdn + val
    o_a = o_b + hv
    o_gs = o_a + hv
    n_proj = o_gs + 2 * d
    a_z, a_xs, a_bm, a_cm = 0, d_inner, 2 * d_inner, 2 * d_inner + gn
    a_q = o_dt
    a_k, a_v, a_zg = a_q + key, a_q + 2 * key, a_q + conv_gdn
    a_gs = a_zg + val
    a_gg = a_gs + d
    a_small = a_gg + d
    n_al = -(-(a_small + LANES) // MM_TILE_N) * MM_TILE_N

    def to_aligned(w):
        z = lambda n: jnp.zeros((w.shape[0], n), w.dtype)
        return jnp.concatenate([
            w[:, :o_dt], w[:, o_qkv:o_b], w[:, o_gs:],
            w[:, o_dt:o_qkv], z(LANE_B - hs), w[:, o_b:o_a], z(LANE_A - LANE_B - hv), w[:, o_a:o_gs], z(LANES - LANE_A - hv),
            z(n_al - a_small - LANES)], axis=1)

    def from_aligned(w):
        s = a_small
        return jnp.concatenate([
            w[:, :o_dt], w[:, s + LANE_DT:s + LANE_DT + hs], w[:, a_q:a_gs], w[:, s + LANE_B:s + LANE_B + hv],
            w[:, s + LANE_A:s + LANE_A + hv], w[:, a_gs:a_small]], axis=1)

    def lanes(vec, at):
        return jnp.zeros((1, LANES), F32).at[:, at:at + vec.shape[-1]].set(vec.reshape(1, -1))

    n_cw = CONV_K * ssm_conv_w.shape[-1]
    small_in = gather_flat("ag_small", jnp.concatenate([c.reshape(-1), ssm_conv_w.reshape(-1), gdn_conv_w.reshape(-1)]))
    c_all = small_in[:, :d]
    by_chip = small_in[0::2]

    def whole_conv_w(lo):
        return jnp.transpose(by_chip[:, lo:lo + n_cw].reshape(4, CONV_K, -1), (1, 0, 2)).reshape(CONV_K, -1)

    cw_ssm, cw_gdn = whole_conv_w(d), whole_conv_w(d + n_cw)
    cb_ssm = ssm_conv_b
    cb_gdn = jnp.zeros((1, conv_gdn), F32)

    n_ada = w_ada.shape[-1]
    b_q = lax.dynamic_slice_in_dim(b_ada, quarter * n_ada, n_ada, axis=1)
    mod_q = _whole("ada_fwd", lambda ca, w, b: _bdot(_silu(ca), w) + b, [c_all, w_ada[0], b_q], [((N_DEV, n_ada), F32)])[0]
    mod_all = gather_flat("ag_mod", mod_q.reshape(-1)).reshape(N_DEV, N_DEV, n_ada)[0::2]
    mod = lax.dynamic_index_in_dim(mod_all, batch, axis=1, keepdims=False).reshape(1, 4 * n_ada)
    sh1, sc1, g1, sh2, sc2, g2 = [mod[:, i * d:(i + 1) * d] for i in range(6)]

    own = [w.astype(BF16) for w in (w_in[0], w_ssm_up[0], w_gdn_up[0], w_out[0], w_mlp_up[0], w_mlp_down[0])]
    with_own = lambda gs, ws: [lax.dynamic_update_index_in_dim(g, w, quarter, 0) for g, w in zip(gs, ws)]
    cols_major = lambda g: jnp.transpose(g, (1, 0, 2)).reshape(g.shape[1], -1)
    rows_major = lambda g: g.reshape(-1, g.shape[2])
    wb_in = to_aligned(cols_major(with_own(run_exchange("ag_w_in", all_gather_shards(own[:1])), own[:1])[0]))

    h1 = rowmap("premix", _premix, [xt], [norm_mix_pre, sc1, sh1], [(d, BF16)])[0]
    proj, gathered = matmul("in_proj", h1, wb_in, comm=all_gather_shards(own[1:]))
    gathered = with_own(gathered, own[1:])
    wb_ssm_up, wb_gdn_up, wb_out = rows_major(gathered[0]), rows_major(gathered[1]), rows_major(gathered[2])
    wb_up, wb_down = cols_major(gathered[3]), rows_major(gathered[4])
    xs = conv_fwd("conv_xs", _conv_silu, proj, d_inner, a_xs, cw_ssm, cb_ssm, 0)
    bm = conv_fwd("conv_bm", _conv_silu, proj, gn, a_bm, cw_ssm, cb_ssm, d_inner)
    cm = conv_fwd("conv_cm", _conv_silu, proj, gn, a_cm, cw_ssm, cb_ssm, d_inner + gn)
    q = conv_fwd("conv_q", _conv_silu_l2, proj, key, a_q, cw_gdn, cb_gdn, 0)
    k = conv_fwd("conv_k", _conv_silu_l2, proj, key, a_k, cw_gdn, cb_gdn, key)
    v = conv_fwd("conv_v", _conv_silu, proj, val, a_v, cw_gdn, cb_gdn, 2 * key)

    wide = 2 * LANES
    ssd_rows = [(xs, wide, 0, True), (bm, LANES, 0, True), (cm, LANES, 0, True), (proj, LANES, a_small, False), (proj, wide, a_z, True)]
    ssd_consts = [(lanes(ssm_dt_bias, LANE_DT), False), (lanes(ssm_A_log, LANE_DT), False), (lanes(ssm_D, LANE_DT), False),
                  (ssm_norm_w.reshape(n_grp, 1, wide), True)]
    hb_ssd, hb_gdn = min(SSD_GROUPS_PER_STEP, n_grp), min(GDN_HEADS_PER_STEP, hq)
    y_ssm_n, st_ssm = scan_fwd("ssd_fwd", ssd_step, SSM_CHUNK, n_grp, ssd_rows, ssd_consts, wide, hb_ssd)
    gdn_rows = [(q, LANES, 0, True), (k, LANES, 0, True), (v, wide, 0, True), (proj, wide, a_zg, True), (proj, LANES, a_small, False)]
    gdn_consts = [(lanes(gdn_dt_bias, LANE_A), False), (lanes(gdn_A_log, LANE_A), False), (gdn_norm_w, False)]
    gdn_rows_per_step = GDN_CHUNK * GDN_CHUNKS_PER_STEP
    y_gdn_n, st_gdn = scan_fwd("gdn_fwd", gdn_step, gdn_rows_per_step, hq, gdn_rows, gdn_consts, wide, hb_gdn)

    y_ssm = matmul("ssm_up", y_ssm_n, wb_ssm_up)
    y_gdn = matmul("gdn_up", y_gdn_n, wb_gdn_up)
    gates = [(proj, d, a_gs), (proj, d, a_gg)]
    merged = rowmap("merge", _merge, gates + [y_ssm, y_gdn], [], [(d, BF16)])[0]
    u = matmul("w_out", merged, wb_out)
    post_consts = [norm_mix_post, g1, norm_mlp_pre, sc2, sh2]
    x1, h2 = rowmap("postmix", _postmix, [xt, u], post_consts, [(d, F32), (d, BF16)])
    relu2 = lambda acc: (acc, jnp.square(jnp.maximum(acc, 0.0)))
    a_up, act = matmul("mlp_up", h2, wb_up, out_dtypes=(BF16, BF16), epi=relu2)
    y2 = matmul("mlp_down", act, wb_down)

    def final_bwd(x1_, y2_, tgt, w_, g_):
        x2, vjp = jax.vjp(_final, x1_, y2_, w_, g_)
        err = x2 - tgt
        loss = 0.5 * jnp.sum(jnp.mean(err * err, axis=-1, keepdims=True), axis=0, keepdims=True)
        dx1, dy2, dw, dg = vjp(err / d)
        return dx1, dy2, loss, dw, dg

    dx1, dy2, loss_part, d_norm_mlp_post, dg2 = rowmap(
        "final", final_bwd, [x1, y2, target], [norm_mlp_post, g2], [(d, F32), (d, BF16)], [((1, 1), F32), ((1, d), F32), ((1, d), F32)])
    loss = lax.psum(loss_part[0, 0], ("x", "y", "c"))

    d_a = matmul("mlp_down_dx", dy2, wb_down, tb=True, out_dtypes=(BF16,), extras=[a_up],
                 epi=lambda acc, a: acc * 2.0 * jnp.maximum(a.astype(F32), 0.0))
    gw_down = matmul("mlp_down_dw", act, dy2, ta=True)
    dh2 = matmul("mlp_up_dx", d_a, wb_up, tb=True)
    gw_up = matmul("mlp_up_dw", h2, d_a, ta=True)

    def postmix_bwd(x_, u_, dx1_, dh2_, *cs):
        _, vjp = jax.vjp(_postmix, x_, u_, *cs)
        return vjp((dx1_, dh2_))

    dxa, du, d_norm_mix_post, dg1, d_norm_mlp_pre, dsc2, dsh2 = rowmap(
        "postmix_bwd", postmix_bwd, [xt, u, dx1, dh2], post_consts, [(d, F32), (d, BF16)], [((1, d), F32)] * 5)
    d_merged = matmul("w_out_dx", du, wb_out, tb=True)
    gw_out = matmul("w_out_dw", merged, du, ta=True)

    def merge_bwd(gs, gg, ys, yg, dm):
        _, vjp = jax.vjp(_merge, gs, gg, ys, yg)
        dgs, dgg, dys, dyg = vjp(dm)
        return dys, dyg, dgs, dgg

    dy_ssm, dy_gdn, dgs, dgg = rowmap("merge_bwd", merge_bwd, gates + [y_ssm, y_gdn, d_merged], [], [(d, BF16)] * 4)
    dy_ssm_n = matmul("ssm_up_dx", dy_ssm, wb_ssm_up, tb=True, out_dtypes=(BF16,))
    gw_ssm_up = matmul("ssm_up_dw", y_ssm_n, dy_ssm, ta=True)
    dy_gdn_n = matmul("gdn_up_dx", dy_gdn, wb_gdn_up, tb=True, out_dtypes=(BF16,))
    gw_gdn_up = matmul("gdn_up_dw", y_gdn_n, dy_gdn, ta=True)

    dxs, dbm, dcm, dsmall_ssm, dz_ssm, d_sdtb, d_salog, d_sdsk, d_snw = scan_bwd(
        "ssd_bwd", ssd_step, SSM_CHUNK, n_grp, ssd_rows, ssd_consts, st_ssm, dy_ssm_n, [BF16, BF16, BF16, F32, BF16], hb_ssd)
    dq, dk, dv, dz_gdn, dsmall_gdn, d_gdtb, d_galog, d_gnw = scan_bwd(
        "gdn_bwd", gdn_step, gdn_rows_per_step, hq, gdn_rows, gdn_consts, st_gdn, dy_gdn_n, [BF16, BF16, BF16, BF16, F32], hb_gdn)

    dxs_p, dcw_xs, dcb_xs = conv_bwd("conv_xs_bwd", _conv_silu, proj, d_inner, a_xs, cw_ssm, cb_ssm, 0, dxs)
    dbm_p, dcw_bm, dcb_bm = conv_bwd("conv_bm_bwd", _conv_silu, proj, gn, a_bm, cw_ssm, cb_ssm, d_inner, dbm)
    dcm_p, dcw_cm, dcb_cm = conv_bwd("conv_cm_bwd", _conv_silu, proj, gn, a_cm, cw_ssm, cb_ssm, d_inner + gn, dcm)
    dq_p, dcw_q, _ = conv_bwd("conv_q_bwd", _conv_silu_l2, proj, key, a_q, cw_gdn, cb_gdn, 0, dq)
    dk_p, dcw_k, _ = conv_bwd("conv_k_bwd", _conv_silu_l2, proj, key, a_k, cw_gdn, cb_gdn, key, dk)
    dv_p, dcw_v, _ = conv_bwd("conv_v_bwd", _conv_silu, proj, val, a_v, cw_gdn, cb_gdn, 2 * key, dv)
    dsmall = rowmap("small_sum", lambda a, b: a + b, [dsmall_ssm, dsmall_gdn], [], [(LANES, BF16)])[0]
    dproj = jnp.concatenate([dz_ssm, dxs_p, dbm_p, dcm_p, dq_p, dk_p, dv_p, dz_gdn, dgs, dgg, dsmall,
                             jnp.zeros((t, n_al - a_small - LANES), BF16)], axis=1)
    quarters_cols = lambda g: jnp.transpose(g.reshape(g.shape[0], 4, -1), (1, 0, 2))
    quarters_rows = lambda g: g.reshape(4, g.shape[0] // 4, g.shape[1])
    rest32, rest16 = reduce_on_chip("rest", [quarters_rows(gw_ssm_up), quarters_rows(gw_gdn_up), quarters_rows(gw_out),
                                             quarters_cols(gw_up), quarters_rows(gw_down)])
    gw_in_al, rest_chips = matmul("in_proj_dw", h1, dproj, ta=True, comm=exchange_quarters_ici(rest16))
    in32, in16 = reduce_on_chip("in", [quarters_cols(from_aligned(gw_in_al))])
    dh1, in_chips = matmul("in_proj_dx", dproj, wb_in, tb=True, comm=exchange_quarters_ici(in16))

    def premix_bwd(x_, dxa_, dh1_, w_, sc_, sh_):
        _, vjp = jax.vjp(_premix, x_, w_, sc_, sh_)
        dx, dw, dsc, dsh = vjp(dh1_)
        return dx + dxa_, dw, dsc, dsh

    grad_x, d_norm_mix_pre, dsc1, dsh1 = rowmap(
        "premix_bwd", premix_bwd, [xt, dxa, dh1], [norm_mix_pre, sc1, sh1], [(d, F32)], [((1, d), F32)] * 3)

    dmod_all = gather_flat("ag_dmod", jnp.concatenate([dsh1, dsc1, dg1, dsh2, dsc2, dg2], axis=1).reshape(-1))
    dmod_q = lax.dynamic_slice_in_dim(dmod_all, quarter * n_ada, n_ada, axis=1)
    gw_ada, gb_ada = _whole(
        "ada_bwd", lambda ca, dq_, da_: (_bdot(_silu(ca), dq_, TN), jnp.sum(da_, axis=0, keepdims=True)),
        [c_all, dmod_q, dmod_all], [((d, n_ada), F32), ((1, 4 * n_ada), F32)])

    dcw_ssm = jnp.concatenate([dcw_xs, dcw_bm, dcw_cm], axis=1)
    dcb_ssm = jnp.concatenate([dcb_xs, dcb_bm, dcb_cm], axis=1)
    dcw_gdn = jnp.concatenate([dcw_q, dcw_k, dcw_v], axis=1)
    partial = [d_norm_mix_pre, d_norm_mix_post, dcw_ssm, dcb_ssm, d_sdtb[:, LANE_DT:LANE_DT + hs], d_salog[:, LANE_DT:LANE_DT + hs],
               d_sdsk[:, LANE_DT:LANE_DT + hs], d_snw, dcw_gdn, d_gdtb[:, LANE_A:LANE_A + hv], d_galog[:, LANE_A:LANE_A + hv], d_gnw,
               d_norm_mlp_pre, d_norm_mlp_post]
    sizes = [p.size for p in partial]
    stacked = gather_flat("ag_grads", jnp.concatenate([p.reshape(-1) for p in partial]))
    summed = _whole("small_sum8", lambda s: jnp.sum(s, axis=0, keepdims=True), [stacked], [((1, stacked.shape[1]), F32)])[0][0]
    offs = [0]
    for s in sizes:
        offs.append(offs[-1] + s)
    red = [summed[offs[i]:offs[i + 1]] for i in range(len(sizes))]
    my_cols = lambda full: lax.dynamic_slice_in_dim(full.reshape(CONV_K, -1), quarter * (n_cw // CONV_K), n_cw // CONV_K, axis=1)
    small_grads = {
        "b_ada": gb_ada, "norm_mix_pre": red[0], "norm_mix_post": red[1], "ssm_conv_w": my_cols(red[2]), "ssm_conv_b": red[3],
        "ssm_dt_bias": red[4], "ssm_A_log": red[5], "ssm_D": red[6], "ssm_norm_w": red[7], "gdn_conv_w": my_cols(red[8]),
        "gdn_dt_bias": red[9], "gdn_A_log": red[10], "gdn_norm_w": red[11], "norm_mlp_pre": red[12], "norm_mlp_post": red[13]}

    big_names = ["w_in", "w_ssm_up", "w_gdn_up", "w_out", "w_mlp_up", "w_mlp_down"]
    big_grads = dict(zip(big_names, reduce_across_chips(in32 + rest32, in_chips + rest_chips)))
    big_grads["w_ada"] = gw_ada

    names = ['w_ada', 'b_ada', 'norm_mix_pre', 'norm_mix_post', 'w_in', 'ssm_conv_w', 'ssm_conv_b', 'ssm_dt_bias', 'ssm_A_log', 'ssm_D',
             'ssm_norm_w', 'gdn_conv_w', 'gdn_dt_bias', 'gdn_A_log', 'gdn_norm_w', 'w_ssm_up', 'w_gdn_up', 'w_out', 'norm_mlp_pre',
             'norm_mlp_post', 'w_mlp_up', 'w_mlp_down']
    grad, delta, new_m, new_v = {}, {}, {}, {}
    for n in big_grads:
        shape = args[n].shape
        g2d = big_grads[n]
        dl, nm, nv = adamw("adamw_" + n, args[n][0], g2d, args["m_" + n][0], args["v_" + n][0])
        grad[n], delta[n], new_m[n], new_v[n] = [a.reshape(shape) for a in (g2d, dl, nm, nv)]
    small_names = [n for n in names if n not in big_grads]
    flat = lambda pre: jnp.concatenate([args[pre + n].reshape(-1) for n in small_names]).reshape(1, -1)
    g_flat = jnp.concatenate([small_grads[n].reshape(-1) for n in small_names]).reshape(1, -1)
    dl, nm, nv = adamw("adamw_small", flat(""), g_flat, flat("m_"), flat("v_"))
    off = 0
    for n in small_names:
        shape = args[n].shape
        size = args[n].size
        grad[n], delta[n], new_m[n], new_v[n] = [a[0, off:off + size].reshape(shape) for a in (g_flat, dl, nm, nv)]
        off += size

    return (loss, grad_x.reshape(x.shape), *[grad[n] for n in names], *[delta[n] for n in names],
            *[new_m[n] for n in names], *[new_v[n] for n in names])
```

```python
import functools

import jax
import jax.numpy as jnp
from jax import lax
from jax.experimental import pallas as pl
from jax.experimental.pallas import tpu as pltpu

F32 = jnp.float32
BF16 = jnp.bfloat16
MESH = pl.DeviceIdType.MESH

EPS = 1e-6
SSM_HEAD_DIM = 64
SSM_HEADS_PER_GROUP = 4
SSM_D_STATE = 128
SSM_CHUNK = 128
GDN_HEAD = 128
GDN_CHUNK = 64
CONV_K = 4
LANE_DT, LANE_B, LANE_A = 0, 32, 48
ADAM_LR, ADAM_B1, ADAM_B2, ADAM_EPS, ADAM_WD, ADAM_STEP = 0.001, 0.9, 0.999, 1e-08, 0.01, 10

VMEM_LIMIT_BYTES = 56 * 1024 * 1024
LANES = 128
N_DEV = 8

NN = (((1,), (0,)), ((), ()))
NT = (((1,), (1,)), ((), ()))
TN = (((0,), (0,)), ((), ()))


BNN = (((2,), (1,)), ((0,), (0,)))
BNT = (((2,), (2,)), ((0,), (0,)))
BTN = (((1,), (1,)), ((0,), (0,)))
_KIND = {NN: ("NN", 0), NT: ("NT", 0), TN: ("TN", 0), BNN: ("NN", 1), BNT: ("NT", 1), BTN: ("TN", 1)}
_DIMS = {"NN": (NN, BNN), "NT": (NT, BNT), "TN": (TN, BTN)}


def _dg(a, b, dims):
    return lax.dot_general(a, b, dims, preferred_element_type=F32)


def _raw_bf16(a, b, dims):
    return _dg(a.astype(BF16), b.astype(BF16), dims)


def _raw_bf16x3(a, b, dims):
    ah, bh = a.astype(BF16), b.astype(BF16)
    al, bl = (a - ah.astype(F32)).astype(BF16), (b - bh.astype(F32)).astype(BF16)
    return _dg(ah, bh, dims) + (_dg(ah, bl, dims) + _dg(al, bh, dims))


def _make_dot(raw):
    @functools.partial(jax.custom_vjp, nondiff_argnums=(2,))
    def dot(a, b, dims):
        return raw(a, b, dims)

    def fwd(a, b, dims):
        return raw(a, b, dims), (a, b)

    def bwd(dims, res, ct):
        a, b = res
        kind, batched = _KIND[dims]
        d = lambda k: _DIMS[k][batched]
        if kind == "NN":
            da, db = raw(ct, b, d("NT")), raw(a, ct, d("TN"))
        elif kind == "NT":
            da, db = raw(ct, b, d("NN")), raw(ct, a, d("TN"))
        else:
            da, db = raw(b, ct, d("NT")), raw(a, ct, d("NN"))
        return da.astype(a.dtype), db.astype(b.dtype)

    dot.defvjp(fwd, bwd)
    return lambda a, b, dims=NN: dot(a, b, dims)


_bdot = _make_dot(_raw_bf16)
_hdot = _make_dot(_raw_bf16x3)


def _mask_dot(mask, x, dims, mask_first=True):
    m = mask.astype(BF16)
    hi = x.astype(BF16)
    r = x - hi.astype(F32)
    mid = r.astype(BF16)
    lo = (r - mid.astype(F32)).astype(BF16)
    return sum(_dg(m, p, dims) if mask_first else _dg(p, m, dims) for p in (hi, mid, lo))


def _sigmoid(x):
    return 1.0 / (1.0 + jnp.exp(-x))


def _silu(x):
    return x * _sigmoid(x)


def _softplus(x):
    return jnp.maximum(x, 0.0) + jnp.log(1.0 + jnp.exp(-jnp.abs(x)))


def _rms(x, w):
    return x * lax.rsqrt(jnp.mean(x * x, axis=-1, keepdims=True) + EPS) * w


def _lane_col(m, idx):
    lane = lax.broadcasted_iota(jnp.int32, m.shape, 1)
    return jnp.sum(jnp.where(lane == idx, m, 0.0), axis=1, keepdims=True)


def _tril(n, strict=False, seg=None):
    r = lax.broadcasted_iota(jnp.int32, (n, n), 0)
    c = lax.broadcasted_iota(jnp.int32, (n, n), 1)
    low = (r > c) if strict else (r >= c)
    if seg is None or seg >= n:
        return low
    shift = seg.bit_length() - 1
    return jnp.logical_and(low, (r >> shift) == (c >> shift))


def _first_lane(shape):
    return lax.broadcasted_iota(jnp.int32, shape, len(shape) - 1) == 0


@jax.custom_vjp
def _row_form(col):
    shape = col.shape[:-1] + (LANES,)
    return _mask_dot(_first_lane(shape), jnp.broadcast_to(col, shape), NT if col.ndim == 2 else BNT)


def _row_form_fwd(col):
    return _row_form(col), None


def _row_form_bwd(_, ct):
    shape = ct.shape[:-1] + (LANES,)
    sums = _mask_dot(_first_lane(shape), ct, TN if ct.ndim == 2 else BTN, mask_first=False)
    return (jnp.sum(sums, axis=-1, keepdims=True),)


_row_form.defvjp(_row_form_fwd, _row_form_bwd)


@functools.partial(jax.custom_vjp, nondiff_argnums=(1,))
def _cumsum_rows(x, seg):
    return _mask_dot(_tril(x.shape[0], seg=seg), x, NN)


def _cumsum_rows_fwd(x, seg):
    return _cumsum_rows(x, seg), None


def _cumsum_rows_bwd(seg, _, ct):
    return (_mask_dot(_tril(ct.shape[0], seg=seg), ct, TN),)


_cumsum_rows.defvjp(_cumsum_rows_fwd, _cumsum_rows_bwd)


def _head_rows(m_t, idx):
    sub = lax.broadcasted_iota(jnp.int32, m_t.shape, 0)
    return jnp.sum(jnp.where(sub == idx, m_t, 0.0), axis=0, keepdims=True)


def _params(sem):
    return pltpu.CompilerParams(dimension_semantics=sem, vmem_limit_bytes=VMEM_LIMIT_BYTES)


def _into_plumbing(outs, first_input):
    arrays, aliases = [], {}
    for k, o in enumerate(outs):
        if len(o) > 4 and not isinstance(o[4], jax.ShapeDtypeStruct):
            aliases[first_input + len(arrays)] = k
            arrays.append(o[4])
    return arrays, aliases


def blockmap(name, fn, grid, ins, outs, accs=(), scalars=None):
    n_in, n_out, n_acc = len(ins), len(outs), len(accs)
    n_grid = len(grid)
    n_pre = 0 if scalars is None else 1
    into_arrays, aliases = _into_plumbing(outs, n_pre + n_in)
    n_into = len(into_arrays)

    def body(*refs):
        refs = refs[n_pre:n_pre + n_in] + refs[n_pre + n_in + n_into:]
        vals = fn(*[r[...] for r in refs[:n_in]])
        if not isinstance(vals, (tuple, list)):
            vals = (vals,)
        for r, v in zip(refs[n_in:n_in + n_out], vals[:n_out]):
            r[...] = v.astype(r.dtype)
        if n_acc:
            first = functools.reduce(jnp.logical_and, [pl.program_id(a) == 0 for a in range(n_grid)])
            acc_refs = refs[n_in + n_out:]

            @pl.when(first)
            def _():
                for r in acc_refs:
                    r[...] = jnp.zeros(r.shape, r.dtype)

            for r, v in zip(acc_refs, vals[n_out:]):
                r[...] += v.astype(r.dtype)

    zeros = lambda nd: (lambda *_: (0,) * nd)
    in_specs = [pl.BlockSpec(b, im) for _, b, im in ins] + [pl.BlockSpec(memory_space=pl.ANY)] * n_into
    out_specs = [pl.BlockSpec(o[2], o[3]) for o in outs] + [pl.BlockSpec(s, zeros(len(s))) for s, _ in accs]
    out_shape = [jax.ShapeDtypeStruct(o[0], o[1]) for o in outs] + [jax.ShapeDtypeStruct(s, d) for s, d in accs]
    cparams = _params(("arbitrary",) * n_grid if n_acc else ("parallel",) * n_grid)
    arrays = [a for a, _, _ in ins] + into_arrays
    if scalars is None:
        return pl.pallas_call(body, name=name, grid=grid, in_specs=in_specs, out_specs=out_specs, out_shape=out_shape,
                              input_output_aliases=aliases, compiler_params=cparams)(*arrays)
    spec = pltpu.PrefetchScalarGridSpec(num_scalar_prefetch=1, grid=grid, in_specs=in_specs, out_specs=out_specs)
    return pl.pallas_call(body, name=name, grid_spec=spec, out_shape=out_shape, input_output_aliases=aliases,
                          compiler_params=cparams)(scalars, *arrays)


def rowmap(name, fn, rows, consts, outs, accs=(), rb=256):
    norm = [(r, r.shape[1], 0) if not isinstance(r, tuple) else (r[0], r[1], r[2] // r[1]) for r in rows]
    assert all(not isinstance(r, tuple) or r[2] % r[1] == 0 for r in rows)
    t = norm[0][0].shape[0]
    rb = min(rb, t)
    ins = [(a, (rb, n), (lambda i, cb=cb: (i, cb))) for a, n, cb in norm]
    ins += [(cst, cst.shape, (lambda i, nd=cst.ndim: (0,) * nd)) for cst in consts]
    o = []
    for out in outs:
        if len(out) == 2:
            o.append(((t, out[0]), out[1], (rb, out[0]), lambda i: (i, 0)))
        else:
            n, d, into, off = out
            assert off % n == 0 and into.dtype == d
            o.append((into.shape, d, (rb, n), (lambda i, cb=off // n: (i, cb)), into))
    return blockmap(name, fn, (t // rb,), ins, o, accs)


MM_TILE_M, MM_TILE_N, MM_TILE_K = 1024, 1024, 2048


def _tile(dim, cap):
    if dim <= cap:
        return dim
    best = max(t for t in range(LANES, cap + 1, LANES) if dim % t == 0)
    return best


def matmul(name, a, b, ta=False, tb=False, out_dtypes=(F32,), epi=None, extras=(), comm=None):
    (k_dim, m_dim) = a.shape if ta else a.shape[::-1]
    n_dim = b.shape[0] if tb else b.shape[1]
    assert (b.shape[1] if tb else b.shape[0]) == k_dim, (name, a.shape, b.shape)
    tm, tn, tk = _tile(m_dim, MM_TILE_M), _tile(n_dim, MM_TILE_N), _tile(k_dim, MM_TILE_K)
    grid = (m_dim // tm, n_dim // tn, k_dim // tk)
    k_steps = grid[2]
    n_extra, n_out = len(extras), len(out_dtypes)
    n_cin, n_cout = (len(comm.operands), len(comm.out_shapes)) if comm else (0, 0)
    dims = (((0 if ta else 1,), (1 if tb else 0,)), ((), ()))

    def body(*refs):
        ins, outs, scratch = refs[:2 + n_extra + n_cin], refs[2 + n_extra + n_cin:][:n_out + n_cout], refs[2 + n_extra + n_cin + n_out + n_cout:]
        extra_refs, out_refs = ins[2:2 + n_extra], outs[:n_out]
        ids = [pl.program_id(ax) for ax in range(3)]
        if comm:
            comm_refs = (ins[2 + n_extra:], outs[n_out:], scratch[-2], scratch[-1])

            @pl.when(functools.reduce(jnp.logical_and, [i == 0 for i in ids]))
            def _():
                comm.start(*comm_refs)

        def finish(acc):
            vals = (acc,) if epi is None else epi(acc, *[r[...] for r in extra_refs])
            if not isinstance(vals, (tuple, list)):
                vals = (vals,)
            for r, v in zip(out_refs, vals):
                r[...] = v.astype(r.dtype)

        prod = lax.dot_general(ins[0][...].astype(BF16), ins[1][...].astype(BF16), dims, preferred_element_type=F32)
        if k_steps == 1:
            finish(prod)
        else:
            acc_ref = scratch[0]

            @pl.when(ids[2] == 0)
            def _():
                acc_ref[...] = jnp.zeros(acc_ref.shape, F32)

            acc_ref[...] += prod

            @pl.when(ids[2] == k_steps - 1)
            def _():
                finish(acc_ref[...])

        if comm:
            @pl.when(functools.reduce(jnp.logical_and, [i == g - 1 for i, g in zip(ids, grid)]))
            def _():
                comm.finish(*comm_refs)

    a_spec = pl.BlockSpec((tk, tm), lambda i, j, k: (k, i)) if ta else pl.BlockSpec((tm, tk), lambda i, j, k: (i, k))
    b_spec = pl.BlockSpec((tn, tk), lambda i, j, k: (j, k)) if tb else pl.BlockSpec((tk, tn), lambda i, j, k: (k, j))
    mn_spec = pl.BlockSpec((tm, tn), lambda i, j, k: (i, j))
    scratch_shapes = [] if k_steps == 1 else [pltpu.VMEM((tm, tn), F32)]
    if comm:
        scratch_shapes += [pltpu.SemaphoreType.DMA(comm.sem_shape), pltpu.SemaphoreType.DMA(comm.sem_shape)]
    res = pl.pallas_call(
        body, name=name, grid=grid,
        in_specs=[a_spec, b_spec] + [mn_spec] * n_extra + [ANY] * n_cin,
        out_specs=[mn_spec] * n_out + [ANY] * n_cout,
        out_shape=[jax.ShapeDtypeStruct((m_dim, n_dim), d) for d in out_dtypes] + (comm.out_shapes if comm else []),
        scratch_shapes=scratch_shapes,
        compiler_params=_params(("arbitrary",) * 3 if comm else ("parallel", "parallel", "arbitrary")),
    )(a, b, *extras, *(comm.operands if comm else []))
    main = res[:n_out] if n_out > 1 else res[0]
    return (main, list(res[n_out:])) if comm else main


def ssd_step(g0, state, xs, bm, cm, small, z, p_dtb, p_alog, p_dsk, nw):
    hb, n = state.shape[0], xs.shape[0]
    n_pair, n_head = 2 * hb, 4 * hb
    causal = _tril(n)
    dt_all = _softplus(small + p_dtb)
    a_all = dt_all * (-jnp.exp(p_alog))
    acum_all = _cumsum_rows(a_all, n)
    acum_t = acum_all.T
    lane0 = LANE_DT + SSM_HEADS_PER_GROUP * g0
    sub = lax.broadcasted_iota(jnp.int32, acum_t.shape, 0)
    heads = range(n_head)
    acum = jnp.stack([_lane_col(acum_all, lane0 + i) for i in heads])
    acum_row = jnp.stack([jnp.sum(jnp.where(sub == lane0 + i, acum_t, 0.0), axis=0, keepdims=True) for i in heads])
    dt = jnp.stack([_lane_col(dt_all, lane0 + i) for i in heads])
    dsk = jnp.stack([_lane_col(p_dsk, lane0 + i) for i in heads])
    decay = jnp.exp(jnp.where(causal, acum - acum_row, -jnp.inf))
    a_last = acum[:, n - 1:n, :]

    def split(a):
        return [a[:, i * LANES:(i + 1) * LANES] for i in range(a.shape[1] // LANES)]

    def pairs(a, axis=2):
        even = jnp.stack([a[2 * p] for p in range(n_pair)])
        odd = jnp.stack([a[2 * p + 1] for p in range(n_pair)])
        shape = (n_pair, LANES, LANES) if axis == 1 else (n_pair, a.shape[1], LANES)
        return jnp.where(lax.broadcasted_iota(jnp.int32, shape, axis) < SSM_HEAD_DIM, even, odd)

    bms, cms = split(bm), split(cm)
    cb = _bdot(jnp.stack(cms), jnp.stack(bms), BNT)
    cbd = jnp.stack([cb[i // SSM_HEADS_PER_GROUP] for i in heads]) * decay
    xp = jnp.stack(split(xs))
    xdt = xp * pairs(dt)
    yd = _bdot(cbd, jnp.stack([xdt[i // 2] for i in heads]), BNN)
    lane = lax.broadcasted_iota(jnp.int32, (n_pair, n, LANES), 2)
    y_diag = jnp.where(lane < SSM_HEAD_DIM, jnp.stack([yd[2 * p] for p in range(n_pair)]), jnp.stack([yd[2 * p + 1] for p in range(n_pair)]))
    st = state.reshape(n_pair, LANES, LANES)
    cm2 = jnp.stack([cms[p // 2] for p in range(n_pair)])
    bm2 = jnp.stack([bms[p // 2] for p in range(n_pair)])
    y_off = _bdot(cm2, st, BNT) * pairs(jnp.exp(acum))
    new = st * pairs(jnp.exp(a_last), axis=1) + _bdot(xdt * pairs(jnp.exp(a_last - acum)), bm2, BTN)
    y = y_diag + y_off + pairs(dsk) * xp
    y = jnp.concatenate([y[p] for p in range(n_pair)], axis=1) * _silu(z)
    wide = 2 * LANES
    y = jnp.concatenate([_rms(y[:, i * wide:(i + 1) * wide], nw[i]) for i in range(hb)], axis=1)
    return new.reshape(state.shape), y


@functools.partial(jax.custom_vjp, nondiff_argnums=(1,))
def _unit_lower_inverse(a, seg):
    n = a.shape[-1]
    r = lax.broadcasted_iota(jnp.int32, (n, n), 0)
    c = lax.broadcasted_iota(jnp.int32, (n, n), 1)
    shift = min(INVERSE_BASE, seg).bit_length() - 1
    power = jnp.where((r >> shift) == (c >> shift), a, 0.0)
    inv = (r == c).astype(F32) - power
    span = 2
    while span < (1 << shift):
        power = _hdot(power, power, BNN)
        inv = inv + _hdot(inv, power, BNN)
        span *= 2
    while (1 << shift) < seg:
        below = jnp.logical_and((r >> (shift + 1)) == (c >> (shift + 1)), (r >> shift) != (c >> shift))
        inv = inv - _hdot(inv, _hdot(jnp.where(below, a, 0.0), inv, BNN), BNN)
        shift += 1
    return inv


def _unit_lower_inverse_fwd(a, seg):
    inv = _unit_lower_inverse(a, seg)
    return inv, inv


def _unit_lower_inverse_bwd(seg, inv, ct):
    return (-_hdot(_hdot(inv, ct, BTN), inv, BNT),)


_unit_lower_inverse.defvjp(_unit_lower_inverse_fwd, _unit_lower_inverse_bwd)


def gdn_step(hq0, state, q, k, v, z, small, p_dtb, p_alog, nw):
    n, chunk = q.shape[0], GDN_CHUNK
    hb = state.shape[0]
    nb = 2 * hb
    cur = state.reshape(nb, LANES, LANES)
    causal, strict = _tril(n, seg=chunk), _tril(n, True, seg=chunk)
    beta_all = _sigmoid(small)
    g_all = -jnp.exp(p_alog) * _softplus(small + p_dtb)
    gcum_all = _cumsum_rows(g_all, chunk)
    gcum_t = gcum_all.T
    split = lambda a: [a[:, i * LANES:(i + 1) * LANES] for i in range(a.shape[1] // LANES)]
    per_value_head = lambda a: jnp.stack([a[i // 2] for i in range(nb)])
    qh, kh = jnp.stack(split(q)) * (GDN_HEAD ** -0.5), jnp.stack(split(k))
    q2, k2 = per_value_head(qh), per_value_head(kh)
    v2, z2 = jnp.stack(split(v)), jnp.stack(split(z))
    gcum = jnp.stack([_lane_col(gcum_all, LANE_A + 2 * hq0 + i) for i in range(nb)])
    gcum_row = jnp.stack([_head_rows(gcum_t, LANE_A + 2 * hq0 + i) for i in range(nb)])
    beta = jnp.stack([_lane_col(beta_all, LANE_B + 2 * hq0 + i) for i in range(nb)])
    dmat = jnp.exp(jnp.where(causal, gcum - gcum_row, -jnp.inf))
    a_low = jnp.where(strict, beta * per_value_head(_bdot(kh, kh, BNT)) * dmat, 0.0)
    inv = _unit_lower_inverse(a_low, chunk)
    egc = jnp.exp(gcum)
    u = _hdot(inv, v2 * beta, BNN)
    w = _hdot(inv, k2 * (beta * egc), BNN)
    q_dec = q2 * egc
    v_new, o_state = [], []
    for s in range(n // chunk):
        rows = slice(s * chunk, (s + 1) * chunk)
        v_new.append(u[:, rows] - _bdot(w[:, rows], cur, BNN))
        o_state.append(_bdot(q_dec[:, rows], cur, BNN))
        g_last = gcum[:, (s + 1) * chunk - 1:(s + 1) * chunk, :]
        k_dec = k2[:, rows] * jnp.exp(g_last - gcum[:, rows])
        cur = cur * jnp.exp(g_last) + _bdot(k_dec, v_new[-1], BTN)
    o = jnp.concatenate(o_state, axis=1) + _bdot(per_value_head(_bdot(qh, kh, BNT)) * dmat, jnp.concatenate(v_new, axis=1), BNN)
    out = _rms(o, nw) * _silu(z2)
    return cur.reshape(state.shape), jnp.concatenate([out[i] for i in range(nb)], axis=1)


STATE_SHAPE = (2, LANES, LANES)
SSD_GROUPS_PER_STEP = 4
GDN_HEADS_PER_STEP = 4
GDN_CHUNKS_PER_STEP = 2
INVERSE_BASE = 16


def _scan_specs(rows, consts, chunk, chunk_of, hb):
    specs = []
    for _, n, off, per_group in rows:
        if per_group:
            assert off % (n * hb) == 0
            specs.append(pl.BlockSpec((chunk, n * hb), lambda c, g, cb=off // (n * hb): (chunk_of(c), cb + g)))
        else:
            assert off % n == 0
            specs.append(pl.BlockSpec((chunk, n), lambda c, g, cb=off // n: (chunk_of(c), cb)))
    for arr, per_group in consts:
        if per_group:
            specs.append(pl.BlockSpec((hb, 1, arr.shape[2]), lambda c, g: (g, 0, 0)))
        else:
            specs.append(pl.BlockSpec(arr.shape, lambda c, g, nd=arr.ndim: (0,) * nd))
    return specs


def scan_fwd(name, step, chunk, n_grp, rows, consts, out_cols, hb):
    t = rows[0][0].shape[0]
    nc = t // chunk
    n_rows, n_consts = len(rows), len(consts)

    def body(*refs):
        row_refs, const_refs = refs[:n_rows], refs[n_rows:n_rows + n_consts]
        y_ref, st_ref, state = refs[n_rows + n_consts:]
        c, g = pl.program_id(0), pl.program_id(1)

        @pl.when(c == 0)
        def _():
            state[g] = jnp.zeros((hb,) + STATE_SHAPE, F32)

        st = state[g]
        st_ref[...] = st
        new, y = step(g * hb, st, *[r[...] for r in row_refs], *[r[...] for r in const_refs])
        state[g] = new
        y_ref[...] = y.astype(y_ref.dtype)

    return pl.pallas_call(
        body, name=name, grid=(nc, n_grp // hb),
        in_specs=_scan_specs(rows, consts, chunk, lambda c: c, hb),
        out_specs=[pl.BlockSpec((chunk, out_cols * hb), lambda c, g: (c, g)),
                   pl.BlockSpec((None, None, hb) + STATE_SHAPE, lambda c, g: (c, g, 0, 0, 0, 0))],
        out_shape=[jax.ShapeDtypeStruct((t, n_grp * out_cols), BF16),
                   jax.ShapeDtypeStruct((nc, n_grp // hb, hb) + STATE_SHAPE, F32)],
        scratch_shapes=[pltpu.VMEM((n_grp // hb, hb) + STATE_SHAPE, F32)],
        compiler_params=_params(("arbitrary", "arbitrary")),
    )(*[r[0] for r in rows], *[c[0] for c in consts])


def scan_bwd(name, step, chunk, n_grp, rows, consts, states, dy, row_dtypes, hb, into):
    t = rows[0][0].shape[0]
    nc = t // chunk
    n_rows, n_consts = len(rows), len(consts)
    out_cols = dy.shape[1] // n_grp
    n_alias = sum(not isinstance(v, jax.ShapeDtypeStruct) for v in into.values())

    def body(*refs):
        row_refs, const_refs = refs[:n_rows], refs[n_rows:n_rows + n_consts]
        st_ref, dy_ref = refs[n_rows + n_consts:n_rows + n_consts + 2]
        outs = refs[n_rows + n_consts + 2 + n_alias:-1]
        dstate = refs[-1]
        c, g = pl.program_id(0), pl.program_id(1)

        @pl.when(c == 0)
        def _():
            dstate[g] = jnp.zeros((hb,) + STATE_SHAPE, F32)

        @pl.when(jnp.logical_and(c == 0, g == 0))
        def _():
            for r in outs[n_rows:]:
                r[...] = jnp.zeros(r.shape, r.dtype)

        _, vjp = jax.vjp(functools.partial(step, g * hb), st_ref[...], *[r[...] for r in row_refs], *[r[...] for r in const_refs])
        grads = vjp((dstate[g], dy_ref[...].astype(F32)))
        dstate[g] = grads[0]
        for (_, _, _, per_group), r, d in zip(rows, outs[:n_rows], grads[1:1 + n_rows]):
            if per_group:
                r[...] = d.astype(r.dtype)
            else:
                @pl.when(g == 0)
                def _(r=r):
                    r[...] = jnp.zeros(r.shape, r.dtype)

                r[...] += d.astype(r.dtype)
        for (_, per_group), r, d in zip(consts, outs[n_rows:], grads[1 + n_rows:]):
            if per_group:
                r[pl.ds(g * hb, hb)] += d
            else:
                r[...] += d

    rev = lambda c: nc - 1 - c
    out_specs, out_shape = [], []
    into_arrays, aliases = [], {}
    first_into = n_rows + n_consts + 2
    for k, ((_, n, off, per_group), dt) in enumerate(zip(rows, row_dtypes)):
        if k in into:
            assert per_group and off % (n * hb) == 0 and into[k].dtype == dt
            out_specs.append(pl.BlockSpec((chunk, n * hb), lambda c, g, cb=off // (n * hb): (rev(c), cb + g)))
            out_shape.append(jax.ShapeDtypeStruct(into[k].shape, dt))
            if not isinstance(into[k], jax.ShapeDtypeStruct):
                aliases[first_into + len(into_arrays)] = k
                into_arrays.append(into[k])
        elif per_group:
            out_specs.append(pl.BlockSpec((chunk, n * hb), lambda c, g: (rev(c), g)))
            out_shape.append(jax.ShapeDtypeStruct((t, n_grp * n), dt))
        else:
            out_specs.append(pl.BlockSpec((chunk, n), lambda c, g: (rev(c), 0)))
            out_shape.append(jax.ShapeDtypeStruct((t, n), dt))
    for arr, _ in consts:
        out_specs.append(pl.BlockSpec(arr.shape, lambda c, g, nd=arr.ndim: (0,) * nd))
        out_shape.append(jax.ShapeDtypeStruct(arr.shape, F32))
    return pl.pallas_call(
        body, name=name, grid=(nc, n_grp // hb),
        in_specs=_scan_specs(rows, consts, chunk, rev, hb)
        + [pl.BlockSpec((None, None, hb) + STATE_SHAPE, lambda c, g: (rev(c), g, 0, 0, 0, 0)),
           pl.BlockSpec((chunk, out_cols * hb), lambda c, g: (rev(c), g))] + [pl.BlockSpec(memory_space=pl.ANY)] * len(into_arrays),
        out_specs=out_specs, out_shape=out_shape, input_output_aliases=aliases,
        scratch_shapes=[pltpu.VMEM((n_grp // hb, hb) + STATE_SHAPE, F32)],
        compiler_params=_params(("arbitrary", "arbitrary")),
    )(*[r[0] for r in rows], *[c[0] for c in consts], states, dy, *into_arrays)


@functools.partial(jax.custom_vjp, nondiff_argnums=(1,))
def _shift_rows(x, k):
    t = x.shape[0]
    row = lax.broadcasted_iota(jnp.int32, x.shape, 0)
    rolled = pltpu.roll(x, k % t, 0)
    return jnp.where(jnp.logical_and(row >= k, row < t + k), rolled, 0.0)


def _shift_rows_fwd(x, k):
    return _shift_rows(x, k), None


def _shift_rows_bwd(k, _, dy):
    return (_shift_rows(dy, -k),)


_shift_rows.defvjp(_shift_rows_fwd, _shift_rows_bwd)


def _conv_silu(x, cw, cb):
    pre = cb + sum(cw[j:j + 1, :] * _shift_rows(x, CONV_K - 1 - j) for j in range(CONV_K))
    return _silu(pre)


def _conv_silu_l2(x, cw, cb):
    y = _conv_silu(x, cw, cb)
    return y * lax.rsqrt(jnp.sum(y * y, axis=-1, keepdims=True) + EPS)


def conv_fwd(name, fn, src, n, off, cw, cb, cw_off):
    t = src.shape[0]
    sb, wb = off // LANES, cw_off // LANES
    ins = [(src, (t, LANES), lambda i: (0, sb + i)), (cw, (CONV_K, LANES), lambda i: (0, wb + i)),
           (cb, (1, LANES), lambda i: (0, wb + i))]
    return blockmap(name, fn, (n // LANES,), ins, [((t, n), F32, (t, LANES), lambda i: (0, i))])[0]


def conv_bwd(name, fn, src, n, off, cw, cb, cw_off, dy, into):
    t = src.shape[0]
    sb, wb = off // LANES, cw_off // LANES

    def bwd(x, w, b, d):
        _, vjp = jax.vjp(fn, x, w, b)
        return vjp(d.astype(F32))

    ins = [(src, (t, LANES), lambda i: (0, sb + i)), (cw, (CONV_K, LANES), lambda i: (0, wb + i)),
           (cb, (1, LANES), lambda i: (0, wb + i)), (dy, (t, LANES), lambda i: (0, i))]
    outs = [(into.shape, BF16, (t, LANES), lambda i: (0, sb + i), into), ((CONV_K, n), F32, (CONV_K, LANES), lambda i: (0, i)),
            ((1, n), F32, (1, LANES), lambda i: (0, i))]
    return blockmap(name, bwd, (n // LANES,), ins, outs)


def _place():
    return lax.axis_index("x"), lax.axis_index("y"), lax.axis_index("c")


def _other_chips(x, y):
    return [(1 - x, y), (x, 1 - y), (1 - x, 1 - y)]


ANY = pl.BlockSpec(memory_space=pl.ANY)


def all_gather8(name, v):
    m_per, n = v.shape

    def body(x_ref, out_ref, send_sems, recv_sems, local_sem):
        x, y, c = _place()
        me, sibling = (x, y, c), (x, y, 1 - c)
        chips = _other_chips(x, y)

        def rows(px, py, pc):
            return out_ref.at[pl.ds((4 * px + 2 * py + pc) * m_per, m_per), :]

        def copy(k, block, to, src=None):
            return pltpu.make_async_remote_copy(
                src_ref=rows(*block) if src is None else src, dst_ref=rows(*block),
                send_sem=send_sems.at[k], recv_sem=recv_sems.at[k], device_id=to, device_id_type=MESH)

        mine = pltpu.make_async_copy(x_ref, rows(*me), local_sem)
        mine.start()
        first = [copy(0, me, sibling, src=x_ref)]
        first += [copy(1 + q, me, (*chip, c), src=x_ref) for q, chip in enumerate(chips)]
        for cp in first:
            cp.start()
        passed = [copy(4 + q, (*chip, c), sibling) for q, chip in enumerate(chips)]
        for q, chip in enumerate(chips):
            copy(1 + q, (*chip, c), me).wait_recv()
            passed[q].start()
        copy(0, sibling, me).wait_recv()
        for q, chip in enumerate(chips):
            copy(4 + q, (*chip, 1 - c), me).wait_recv()
        for cp in first + passed:
            cp.wait_send()
        mine.wait()

    return pl.pallas_call(
        body, name=name, out_shape=jax.ShapeDtypeStruct((N_DEV * m_per, n), v.dtype),
        in_specs=[pl.BlockSpec(memory_space=pltpu.VMEM)], out_specs=pl.BlockSpec(memory_space=pltpu.VMEM),
        scratch_shapes=[pltpu.SemaphoreType.DMA((7,)), pltpu.SemaphoreType.DMA((7,)), pltpu.SemaphoreType.DMA],
    )(v)


def gather_flat(name, vec):
    n = vec.shape[0]
    n_pad = -(-n // (8 * LANES)) * (8 * LANES)
    v = jnp.pad(vec, (0, n_pad - n)).reshape(8, n_pad // 8)
    return all_gather8(name, v).reshape(N_DEV, n_pad)[:, :n]


class Exchange:
    def __init__(self, operands, out_shapes, sem_shape, start, finish):
        self.operands, self.out_shapes, self.sem_shape, self.start, self.finish = list(operands), out_shapes, sem_shape, start, finish


def _start_all_wait_all(make_copies):
    def start(*refs):
        for cp in make_copies(*refs):
            cp.start()

    def finish(*refs):
        for cp in make_copies(*refs):
            cp.wait()

    return start, finish


def run_exchange(name, ex):
    n_in, n_out = len(ex.operands), len(ex.out_shapes)

    def body(*refs):
        ins, outs = refs[:n_in], refs[n_in:n_in + n_out]
        ex.start(ins, outs, *refs[n_in + n_out:])
        ex.finish(ins, outs, *refs[n_in + n_out:])

    return pl.pallas_call(
        body, name=name, out_shape=ex.out_shapes, in_specs=[ANY] * n_in, out_specs=[ANY] * n_out,
        scratch_shapes=[pltpu.SemaphoreType.DMA(ex.sem_shape), pltpu.SemaphoreType.DMA(ex.sem_shape)],
    )(*ex.operands)


def all_gather_shards(shards):
    n_t = len(shards)

    def copies(ins, outs, send_sems, recv_sems):
        x, y, c = _place()
        chips = _other_chips(x, y)

        def copy(t, k, quarter, pc, to, src=None):
            h = ins[t].shape[0] // 2
            dst = outs[t].at[quarter, pl.ds(pc * h, h)]
            return pltpu.make_async_remote_copy(
                src_ref=dst if src is None else ins[t].at[pl.ds(pc * h, h)], dst_ref=dst,
                send_sem=send_sems.at[t, k], recv_sem=recv_sems.at[t, k], device_id=to, device_id_type=MESH)

        first = [copy(t, q, 2 * x + y, c, (*chip, c), src=True) for t in range(n_t) for q, chip in enumerate(chips)]
        landed = [copy(t, q, 2 * px + py, c, (x, y, c)) for t in range(n_t) for q, (px, py) in enumerate(chips)]
        passed = [copy(t, 3 + q, 2 * px + py, c, (x, y, 1 - c)) for t in range(n_t) for q, (px, py) in enumerate(chips)]
        from_sibling = [copy(t, 3 + q, 2 * px + py, 1 - c, (x, y, c)) for t in range(n_t) for q, (px, py) in enumerate(chips)]
        return first, landed, passed, from_sibling

    def start(*refs):
        for cp in copies(*refs)[0]:
            cp.start()

    def finish(*refs):
        first, landed, passed, from_sibling = copies(*refs)
        for arrived, onward in zip(landed, passed):
            arrived.wait_recv()
            onward.start()
        for cp in from_sibling:
            cp.wait_recv()
        for cp in first + passed:
            cp.wait_send()

    return Exchange(shards, [jax.ShapeDtypeStruct((4,) + s.shape, s.dtype) for s in shards], (n_t, 6), start, finish)


def exchange_halves_d2d(grads):
    n_t = len(grads)

    def copies(ins, outs, send_sems, recv_sems):
        x, y, c = _place()
        return [pltpu.make_async_remote_copy(
            src_ref=ins[t].at[:, pl.ds((1 - c) * (ins[t].shape[1] // 2), ins[t].shape[1] // 2), :], dst_ref=outs[t],
            send_sem=send_sems.at[t], recv_sem=recv_sems.at[t], device_id=(x, y, 1 - c), device_id_type=MESH) for t in range(n_t)]

    shapes = [jax.ShapeDtypeStruct((4, g.shape[1] // 2, g.shape[2]), g.dtype) for g in grads]
    return Exchange(grads, shapes, (n_t,), *_start_all_wait_all(copies))


def exchange_quarters_ici(parts):
    n_t = len(parts)

    def copies(ins, outs, send_sems, recv_sems):
        x, y, c = _place()
        return [pltpu.make_async_remote_copy(
            src_ref=ins[t].at[2 * px + py], dst_ref=outs[t].at[q],
            send_sem=send_sems.at[t, q], recv_sem=recv_sems.at[t, q], device_id=(px, py, c), device_id_type=MESH)
            for t in range(n_t) for q, (px, py) in enumerate(_other_chips(x, y))]

    shapes = [jax.ShapeDtypeStruct((3,) + p.shape[1:], p.dtype) for p in parts]
    return Exchange(parts, shapes, (n_t, 3), *_start_all_wait_all(copies))


def swap_d2d(halves):
    n_t = len(halves)

    def copies(ins, outs, send_sems, recv_sems):
        x, y, c = _place()
        return [pltpu.make_async_remote_copy(
            src_ref=ins[t], dst_ref=outs[t], send_sem=send_sems.at[t], recv_sem=recv_sems.at[t],
            device_id=(x, y, 1 - c), device_id_type=MESH) for t in range(n_t)]

    return Exchange(halves, [jax.ShapeDtypeStruct(h.shape, h.dtype) for h in halves], (n_t,), *_start_all_wait_all(copies))


BLOCK_BYTES = 1 << 20


def _row_block(r, c):
    rb = r
    while rb * c * 4 > BLOCK_BYTES and rb % 16 == 0:
        rb //= 2
    return rb


def _place_scalars():
    x, y, c = _place()
    return jnp.stack([c, 2 * x + y]).astype(jnp.int32)


def reduce_on_chip(tag, grads):
    from_sibling = run_exchange(f"rs_d2d_{tag}", exchange_halves_d2d(grads))
    parts, parts_bf16 = [], []
    for t, (g, s) in enumerate(zip(grads, from_sibling)):
        _, h, cols = s.shape
        rb = _row_block(h, cols)
        nb = h // rb
        blk = lambda k, i, s_ref: (k, i, 0)
        p32, p16 = blockmap(
            f"rs_add_{tag}{t}", lambda a, b: (a + b, a + b), (4, nb),
            [(g, (None, rb, cols), lambda k, i, s_ref, nb=nb: (k, s_ref[0] * nb + i, 0)), (s, (None, rb, cols), blk)],
            [(s.shape, F32, (None, rb, cols), blk), (s.shape, BF16, (None, rb, cols), blk)], scalars=_place_scalars())
        parts.append(p32)
        parts_bf16.append(p16)
    return parts, parts_bf16


def reduce_across_chips(parts, from_chips):
    halves = []
    for t, (p, q) in enumerate(zip(parts, from_chips)):
        _, h, cols = p.shape
        rb = _row_block(h, cols)
        halves.append(blockmap(
            f"rs_sum{t}", lambda a, b: a + b[0].astype(F32) + b[1].astype(F32) + b[2].astype(F32), (h // rb,),
            [(p, (None, rb, cols), lambda i, s_ref: (s_ref[1], i, 0)), (q, (3, rb, cols), lambda i, s_ref: (0, i, 0))],
            [((h, cols), F32, (rb, cols), lambda i, s_ref: (i, 0))], scalars=_place_scalars())[0])
    return list(zip(halves, run_exchange("rs_swap", swap_d2d(halves))))


def _adamw(w, g, m, v):
    m = ADAM_B1 * m + (1.0 - ADAM_B1) * g
    v = ADAM_B2 * v + (1.0 - ADAM_B2) * jnp.square(g)
    m_hat = m / (1.0 - ADAM_B1 ** ADAM_STEP)
    v_hat = v / (1.0 - ADAM_B2 ** ADAM_STEP)
    delta = -ADAM_LR * (m_hat / (jnp.sqrt(v_hat) + ADAM_EPS) + ADAM_WD * w)
    return delta, m, v


def adamw(name, w, g, m, v):
    _, r, c = w.shape
    rb = _row_block(r, c)
    blk3 = lambda a: (a, (None, rb, c), lambda i: (0, i, 0))
    return blockmap(name, _adamw, (r // rb,), [blk3(w), (g, (rb, c), lambda i: (i, 0)), blk3(m), blk3(v)],
                    [(w.shape, F32, (None, rb, c), lambda i: (0, i, 0))] * 3)


def adamw_halves(name, w, mine, other, m, v):
    _, r, c = w.shape
    h = r // 2
    rb = _row_block(h, c)
    nb = h // rb

    def body(s_ref, w_ref, mine_ref, other_ref, m_ref, v_ref, g_out, d_out, m_out, v_out):
        g = jnp.where(pl.program_id(0) == s_ref[0], mine_ref[...], other_ref[...])
        d, nm, nv = _adamw(w_ref[...], g, m_ref[...], v_ref[...])
        g_out[...], d_out[...], m_out[...], v_out[...] = g, d, nm, nv

    spec3 = pl.BlockSpec((None, rb, c), lambda k, i, s_ref: (0, k * nb + i, 0))
    spec2 = pl.BlockSpec((rb, c), lambda k, i, s_ref: (i, 0))
    grid_spec = pltpu.PrefetchScalarGridSpec(num_scalar_prefetch=1, grid=(2, nb), in_specs=[spec3, spec2, spec2, spec3, spec3],
                                             out_specs=[spec3] * 4)
    return pl.pallas_call(body, name=name, grid_spec=grid_spec, out_shape=[jax.ShapeDtypeStruct(w.shape, F32)] * 4,
                          compiler_params=_params(("parallel", "parallel")))(_place_scalars(), w, mine, other, m, v)


def _whole(name, fn, ins, outs):
    return blockmap(name, fn, (1,), [(a, a.shape, lambda i, nd=a.ndim: (0,) * nd) for a in ins],
                    [(s, d, s, lambda i, nd=len(s): (0,) * nd) for s, d in outs])


def _premix(x, w, sc, sh):
    return _rms(x, w) * (1.0 + sc) + sh


def _postmix(x, u, w_post, g1, w_pre2, sc2, sh2):
    x1 = x + g1 * _rms(u, w_post)
    return x1, _premix(x1, w_pre2, sc2, sh2)


def _merge(gs, gg, ys, yg):
    return _sigmoid(gs) * ys + _sigmoid(gg) * yg


def _final(x1, y2, w_post2, g2):
    return x1 + g2 * _rms(y2, w_post2)


def kernel(x, c, w_ada, b_ada, norm_mix_pre, norm_mix_post, w_in, ssm_conv_w, ssm_conv_b, ssm_dt_bias, ssm_A_log, ssm_D, ssm_norm_w, gdn_conv_w, gdn_dt_bias, gdn_A_log, gdn_norm_w, w_ssm_up, w_gdn_up, w_out, norm_mlp_pre, norm_mlp_post, w_mlp_up, w_mlp_down, loss_target, m_w_ada, m_b_ada, m_norm_mix_pre, m_norm_mix_post, m_w_in, m_ssm_conv_w, m_ssm_conv_b, m_ssm_dt_bias, m_ssm_A_log, m_ssm_D, m_ssm_norm_w, m_gdn_conv_w, m_gdn_dt_bias, m_gdn_A_log, m_gdn_norm_w, m_w_ssm_up, m_w_gdn_up, m_w_out, m_norm_mlp_pre, m_norm_mlp_post, m_w_mlp_up, m_w_mlp_down, v_w_ada, v_b_ada, v_norm_mix_pre, v_norm_mix_post, v_w_in, v_ssm_conv_w, v_ssm_conv_b, v_ssm_dt_bias, v_ssm_A_log, v_ssm_D, v_ssm_norm_w, v_gdn_conv_w, v_gdn_dt_bias, v_gdn_A_log, v_gdn_norm_w, v_w_ssm_up, v_w_gdn_up, v_w_out, v_norm_mlp_pre, v_norm_mlp_post, v_w_mlp_up, v_w_mlp_down):
    args = dict(locals())
    xi, yi, ci = _place()
    quarter = 2 * xi + yi
    batch = 4 * xi + 2 * yi + ci

    xt, target = x[0], loss_target[0]
    t, d = xt.shape
    hs, hv = ssm_dt_bias.shape[-1], gdn_dt_bias.shape[-1]
    d_inner = hs * SSM_HEAD_DIM
    n_grp = hs // SSM_HEADS_PER_GROUP
    gn = n_grp * SSM_D_STATE
    conv_ssm = d_inner + 2 * gn
    hq = hv // 2
    key, val = hq * GDN_HEAD, hv * GDN_HEAD
    conv_gdn = 2 * key + val
    hidden = 4 * w_mlp_up.shape[-1]
    o_dt = d_inner + conv_ssm
    o_qkv = o_dt + hs
    o_b = o_qkv + conv_gdn + val
    o_a = o_b + hv
    o_gs = o_a + hv
    n_proj = o_gs + 2 * d
    a_z, a_xs, a_bm, a_cm = 0, d_inner, 2 * d_inner, 2 * d_inner + gn
    a_q = o_dt
    a_k, a_v, a_zg = a_q + key, a_q + 2 * key, a_q + conv_gdn
    a_gs = a_zg + val
    a_gg = a_gs + d
    a_small = a_gg + d
    n_al = -(-(a_small + LANES) // MM_TILE_N) * MM_TILE_N

    def to_aligned(w):
        z = lambda n: jnp.zeros((w.shape[0], n), w.dtype)
        return jnp.concatenate([
            w[:, :o_dt], w[:, o_qkv:o_b], w[:, o_gs:],
            w[:, o_dt:o_qkv], z(LANE_B - hs), w[:, o_b:o_a], z(LANE_A - LANE_B - hv), w[:, o_a:o_gs], z(LANES - LANE_A - hv),
            z(n_al - a_small - LANES)], axis=1)

    def from_aligned(w):
        s = a_small
        return jnp.concatenate([
            w[:, :o_dt], w[:, s + LANE_DT:s + LANE_DT + hs], w[:, a_q:a_gs], w[:, s + LANE_B:s + LANE_B + hv],
            w[:, s + LANE_A:s + LANE_A + hv], w[:, a_gs:a_small]], axis=1)

    def lanes(vec, at):
        return jnp.zeros((1, LANES), F32).at[:, at:at + vec.shape[-1]].set(vec.reshape(1, -1))

    n_cw = CONV_K * ssm_conv_w.shape[-1]
    small_in = gather_flat("ag_small", jnp.concatenate([c.reshape(-1), ssm_conv_w.reshape(-1), gdn_conv_w.reshape(-1)]))
    c_all = small_in[:, :d]
    by_chip = small_in[0::2]

    def whole_conv_w(lo):
        return jnp.transpose(by_chip[:, lo:lo + n_cw].reshape(4, CONV_K, -1), (1, 0, 2)).reshape(CONV_K, -1)

    cw_ssm, cw_gdn = whole_conv_w(d), whole_conv_w(d + n_cw)
    cb_ssm = ssm_conv_b
    cb_gdn = jnp.zeros((1, conv_gdn), F32)

    n_ada = w_ada.shape[-1]
    b_q = lax.dynamic_slice_in_dim(b_ada, quarter * n_ada, n_ada, axis=1)
    mod_q = _whole("ada_fwd", lambda ca, w, b: _bdot(_silu(ca), w) + b, [c_all, w_ada[0], b_q], [((N_DEV, n_ada), F32)])[0]
    mod_all = gather_flat("ag_mod", mod_q.reshape(-1)).reshape(N_DEV, N_DEV, n_ada)[0::2]
    mod = lax.dynamic_index_in_dim(mod_all, batch, axis=1, keepdims=False).reshape(1, 4 * n_ada)
    sh1, sc1, g1, sh2, sc2, g2 = [mod[:, i * d:(i + 1) * d] for i in range(6)]

    own = [w.astype(BF16) for w in (w_in[0], w_ssm_up[0], w_gdn_up[0], w_out[0], w_mlp_up[0], w_mlp_down[0])]
    with_own = lambda gs, ws: [lax.dynamic_update_index_in_dim(g, w, quarter, 0) for g, w in zip(gs, ws)]
    cols_major = lambda g: jnp.transpose(g, (1, 0, 2)).reshape(g.shape[1], -1)
    rows_major = lambda g: g.reshape(-1, g.shape[2])
    wb_in = to_aligned(cols_major(with_own(run_exchange("ag_w_in", all_gather_shards(own[:1])), own[:1])[0]))

    h1 = rowmap("premix", _premix, [xt], [norm_mix_pre, sc1, sh1], [(d, BF16)])[0]
    proj, gathered = matmul("in_proj", h1, wb_in, comm=all_gather_shards(own[1:]))
    gathered = with_own(gathered, own[1:])
    wb_ssm_up, wb_gdn_up, wb_out = rows_major(gathered[0]), rows_major(gathered[1]), rows_major(gathered[2])
    wb_up, wb_down = cols_major(gathered[3]), rows_major(gathered[4])
    xs = conv_fwd("conv_xs", _conv_silu, proj, d_inner, a_xs, cw_ssm, cb_ssm, 0)
    bm = conv_fwd("conv_bm", _conv_silu, proj, gn, a_bm, cw_ssm, cb_ssm, d_inner)
    cm = conv_fwd("conv_cm", _conv_silu, proj, gn, a_cm, cw_ssm, cb_ssm, d_inner + gn)
    q = conv_fwd("conv_q", _conv_silu_l2, proj, key, a_q, cw_gdn, cb_gdn, 0)
    k = conv_fwd("conv_k", _conv_silu_l2, proj, key, a_k, cw_gdn, cb_gdn, key)
    v = conv_fwd("conv_v", _conv_silu, proj, val, a_v, cw_gdn, cb_gdn, 2 * key)

    wide = 2 * LANES
    ssd_rows = [(xs, wide, 0, True), (bm, LANES, 0, True), (cm, LANES, 0, True), (proj, LANES, a_small, False), (proj, wide, a_z, True)]
    ssd_consts = [(lanes(ssm_dt_bias, LANE_DT), False), (lanes(ssm_A_log, LANE_DT), False), (lanes(ssm_D, LANE_DT), False),
                  (ssm_norm_w.reshape(n_grp, 1, wide), True)]
    hb_ssd, hb_gdn = min(SSD_GROUPS_PER_STEP, n_grp), min(GDN_HEADS_PER_STEP, hq)
    y_ssm_n, st_ssm = scan_fwd("ssd_fwd", ssd_step, SSM_CHUNK, n_grp, ssd_rows, ssd_consts, wide, hb_ssd)
    gdn_rows = [(q, LANES, 0, True), (k, LANES, 0, True), (v, wide, 0, True), (proj, wide, a_zg, True), (proj, LANES, a_small, False)]
    gdn_consts = [(lanes(gdn_dt_bias, LANE_A), False), (lanes(gdn_A_log, LANE_A), False), (gdn_norm_w, False)]
    gdn_rows_per_step = GDN_CHUNK * GDN_CHUNKS_PER_STEP
    y_gdn_n, st_gdn = scan_fwd("gdn_fwd", gdn_step, gdn_rows_per_step, hq, gdn_rows, gdn_consts, wide, hb_gdn)

    y_ssm = matmul("ssm_up", y_ssm_n, wb_ssm_up)
    y_gdn = matmul("gdn_up", y_gdn_n, wb_gdn_up)
    gates = [(proj, d, a_gs), (proj, d, a_gg)]
    merged = rowmap("merge", _merge, gates + [y_ssm, y_gdn], [], [(d, BF16)])[0]
    u = matmul("w_out", merged, wb_out)
    post_consts = [norm_mix_post, g1, norm_mlp_pre, sc2, sh2]
    x1, h2 = rowmap("postmix", _postmix, [xt, u], post_consts, [(d, F32), (d, BF16)])
    relu2 = lambda acc: (acc, jnp.square(jnp.maximum(acc, 0.0)))
    a_up, act = matmul("mlp_up", h2, wb_up, out_dtypes=(BF16, BF16), epi=relu2)
    y2 = matmul("mlp_down", act, wb_down)

    def final_bwd(x1_, y2_, tgt, w_, g_):
        x2, vjp = jax.vjp(_final, x1_, y2_, w_, g_)
        err = x2 - tgt
        loss = 0.5 * jnp.sum(jnp.mean(err * err, axis=-1, keepdims=True), axis=0, keepdims=True)
        dx1, dy2, dw, dg = vjp(err / d)
        return dx1, dy2, loss, dw, dg

    dx1, dy2, loss_part, d_norm_mlp_post, dg2 = rowmap(
        "final", final_bwd, [x1, y2, target], [norm_mlp_post, g2], [(d, F32), (d, BF16)], [((1, 1), F32), ((1, d), F32), ((1, d), F32)])
    loss = lax.psum(loss_part[0, 0], ("x", "y", "c"))

    d_a = matmul("mlp_down_dx", dy2, wb_down, tb=True, out_dtypes=(BF16,), extras=[a_up],
                 epi=lambda acc, a: acc * 2.0 * jnp.maximum(a.astype(F32), 0.0))
    gw_down = matmul("mlp_down_dw", act, dy2, ta=True)
    dh2 = matmul("mlp_up_dx", d_a, wb_up, tb=True)
    gw_up = matmul("mlp_up_dw", h2, d_a, ta=True)

    def postmix_bwd(x_, u_, dx1_, dh2_, *cs):
        _, vjp = jax.vjp(_postmix, x_, u_, *cs)
        return vjp((dx1_, dh2_))

    dxa, du, d_norm_mix_post, dg1, d_norm_mlp_pre, dsc2, dsh2 = rowmap(
        "postmix_bwd", postmix_bwd, [xt, u, dx1, dh2], post_consts, [(d, F32), (d, BF16)], [((1, d), F32)] * 5)
    d_merged = matmul("w_out_dx", du, wb_out, tb=True)
    gw_out = matmul("w_out_dw", merged, du, ta=True)

    def merge_bwd(gs, gg, ys, yg, dm):
        _, vjp = jax.vjp(_merge, gs, gg, ys, yg)
        dgs, dgg, dys, dyg = vjp(dm)
        return dys, dyg, jnp.concatenate([dgs, dgg], axis=1)

    dy_ssm, dy_gdn, dproj = rowmap("merge_bwd", merge_bwd, gates + [y_ssm, y_gdn, d_merged], [],
                                   [(d, BF16), (d, BF16), (2 * d, BF16, jax.ShapeDtypeStruct((t, n_al), BF16), a_gs)])
    dy_ssm_n = matmul("ssm_up_dx", dy_ssm, wb_ssm_up, tb=True, out_dtypes=(BF16,))
    gw_ssm_up = matmul("ssm_up_dw", y_ssm_n, dy_ssm, ta=True)
    dy_gdn_n = matmul("gdn_up_dx", dy_gdn, wb_gdn_up, tb=True, out_dtypes=(BF16,))
    gw_gdn_up = matmul("gdn_up_dw", y_gdn_n, dy_gdn, ta=True)

    dxs, dbm, dcm, dsmall_ssm, dproj, d_sdtb, d_salog, d_sdsk, d_snw = scan_bwd(
        "ssd_bwd", ssd_step, SSM_CHUNK, n_grp, ssd_rows, ssd_consts, st_ssm, dy_ssm_n, [BF16, BF16, BF16, F32, BF16], hb_ssd,
        {4: dproj})
    dq, dk, dv, dproj, dsmall_gdn, d_gdtb, d_galog, d_gnw = scan_bwd(
        "gdn_bwd", gdn_step, gdn_rows_per_step, hq, gdn_rows, gdn_consts, st_gdn, dy_gdn_n, [BF16, BF16, BF16, BF16, F32], hb_gdn,
        {3: dproj})
    dproj, dcw_xs, dcb_xs = conv_bwd("conv_xs_bwd", _conv_silu, proj, d_inner, a_xs, cw_ssm, cb_ssm, 0, dxs, dproj)
    dproj, dcw_bm, dcb_bm = conv_bwd("conv_bm_bwd", _conv_silu, proj, gn, a_bm, cw_ssm, cb_ssm, d_inner, dbm, dproj)
    dproj, dcw_cm, dcb_cm = conv_bwd("conv_cm_bwd", _conv_silu, proj, gn, a_cm, cw_ssm, cb_ssm, d_inner + gn, dcm, dproj)
    dproj, dcw_q, _ = conv_bwd("conv_q_bwd", _conv_silu_l2, proj, key, a_q, cw_gdn, cb_gdn, 0, dq, dproj)
    dproj, dcw_k, _ = conv_bwd("conv_k_bwd", _conv_silu_l2, proj, key, a_k, cw_gdn, cb_gdn, key, dk, dproj)
    dproj, dcw_v, _ = conv_bwd("conv_v_bwd", _conv_silu, proj, val, a_v, cw_gdn, cb_gdn, 2 * key, dv, dproj)
    tail = n_al - a_small
    dproj = rowmap("small_sum", lambda a, b: jnp.concatenate([a + b, jnp.zeros((a.shape[0], tail - LANES), F32)], axis=1),
                   [dsmall_ssm, dsmall_gdn], [], [(tail, BF16, dproj, a_small)])[0]
    quarters_cols = lambda g: jnp.transpose(g.reshape(g.shape[0], 4, -1), (1, 0, 2))
    quarters_rows = lambda g: g.reshape(4, g.shape[0] // 4, g.shape[1])
    rest32, rest16 = reduce_on_chip("rest", [quarters_rows(gw_ssm_up), quarters_rows(gw_gdn_up), quarters_rows(gw_out),
                                             quarters_cols(gw_up), quarters_rows(gw_down)])
    gw_in_al, rest_chips = matmul("in_proj_dw", h1, dproj, ta=True, comm=exchange_quarters_ici(rest16))
    in32, in16 = reduce_on_chip("in", [quarters_cols(from_aligned(gw_in_al))])
    dh1, in_chips = matmul("in_proj_dx", dproj, wb_in, tb=True, comm=exchange_quarters_ici(in16))

    def premix_bwd(x_, dxa_, dh1_, w_, sc_, sh_):
        _, vjp = jax.vjp(_premix, x_, w_, sc_, sh_)
        dx, dw, dsc, dsh = vjp(dh1_)
        return dx + dxa_, dw, dsc, dsh

    grad_x, d_norm_mix_pre, dsc1, dsh1 = rowmap(
        "premix_bwd", premix_bwd, [xt, dxa, dh1], [norm_mix_pre, sc1, sh1], [(d, F32)], [((1, d), F32)] * 3)

    dmod_all = gather_flat("ag_dmod", jnp.concatenate([dsh1, dsc1, dg1, dsh2, dsc2, dg2], axis=1).reshape(-1))
    dmod_q = lax.dynamic_slice_in_dim(dmod_all, quarter * n_ada, n_ada, axis=1)
    gw_ada, gb_ada = _whole(
        "ada_bwd", lambda ca, dq_, da_: (_bdot(_silu(ca), dq_, TN), jnp.sum(da_, axis=0, keepdims=True)),
        [c_all, dmod_q, dmod_all], [((d, n_ada), F32), ((1, 4 * n_ada), F32)])

    dcw_ssm = jnp.concatenate([dcw_xs, dcw_bm, dcw_cm], axis=1)
    dcb_ssm = jnp.concatenate([dcb_xs, dcb_bm, dcb_cm], axis=1)
    dcw_gdn = jnp.concatenate([dcw_q, dcw_k, dcw_v], axis=1)
    partial = [d_norm_mix_pre, d_norm_mix_post, dcw_ssm, dcb_ssm, d_sdtb[:, LANE_DT:LANE_DT + hs], d_salog[:, LANE_DT:LANE_DT + hs],
               d_sdsk[:, LANE_DT:LANE_DT + hs], d_snw, dcw_gdn, d_gdtb[:, LANE_A:LANE_A + hv], d_galog[:, LANE_A:LANE_A + hv], d_gnw,
               d_norm_mlp_pre, d_norm_mlp_post]
    sizes = [p.size for p in partial]
    stacked = gather_flat("ag_grads", jnp.concatenate([p.reshape(-1) for p in partial]))
    summed = _whole("small_sum8", lambda s: jnp.sum(s, axis=0, keepdims=True), [stacked], [((1, stacked.shape[1]), F32)])[0][0]
    offs = [0]
    for s in sizes:
        offs.append(offs[-1] + s)
    red = [summed[offs[i]:offs[i + 1]] for i in range(len(sizes))]
    my_cols = lambda full: lax.dynamic_slice_in_dim(full.reshape(CONV_K, -1), quarter * (n_cw // CONV_K), n_cw // CONV_K, axis=1)
    small_grads = {
        "b_ada": gb_ada, "norm_mix_pre": red[0], "norm_mix_post": red[1], "ssm_conv_w": my_cols(red[2]), "ssm_conv_b": red[3],
        "ssm_dt_bias": red[4], "ssm_A_log": red[5], "ssm_D": red[6], "ssm_norm_w": red[7], "gdn_conv_w": my_cols(red[8]),
        "gdn_dt_bias": red[9], "gdn_A_log": red[10], "gdn_norm_w": red[11], "norm_mlp_pre": red[12], "norm_mlp_post": red[13]}

    big_names = ["w_in", "w_ssm_up", "w_gdn_up", "w_out", "w_mlp_up", "w_mlp_down"]
    big_grads = dict(zip(big_names, reduce_across_chips(in32 + rest32, in_chips + rest_chips)))

    names = ['w_ada', 'b_ada', 'norm_mix_pre', 'norm_mix_post', 'w_in', 'ssm_conv_w', 'ssm_conv_b', 'ssm_dt_bias', 'ssm_A_log', 'ssm_D',
             'ssm_norm_w', 'gdn_conv_w', 'gdn_dt_bias', 'gdn_A_log', 'gdn_norm_w', 'w_ssm_up', 'w_gdn_up', 'w_out', 'norm_mlp_pre',
             'norm_mlp_post', 'w_mlp_up', 'w_mlp_down']
    grad, delta, new_m, new_v = {}, {}, {}, {}
    for n, (mine, other) in big_grads.items():
        grad[n], delta[n], new_m[n], new_v[n] = adamw_halves("adamw_" + n, args[n], mine, other, args["m_" + n], args["v_" + n])
    grad["w_ada"] = gw_ada.reshape(w_ada.shape)
    delta["w_ada"], new_m["w_ada"], new_v["w_ada"] = adamw("adamw_w_ada", w_ada, gw_ada, m_w_ada, v_w_ada)
    small_names = [n for n in names if n not in grad]
    flat = lambda pre: jnp.concatenate([args[pre + n].reshape(-1) for n in small_names]).reshape(1, 1, -1)
    g_flat = jnp.concatenate([small_grads[n].reshape(-1) for n in small_names]).reshape(1, -1)
    dl, nm, nv = adamw("adamw_small", flat(""), g_flat, flat("m_"), flat("v_"))
    off = 0
    for n in small_names:
        shape = args[n].shape
        size = args[n].size
        grad[n], delta[n], new_m[n], new_v[n] = [a.reshape(-1)[off:off + size].reshape(shape) for a in (g_flat, dl, nm, nv)]
        off += size

    return (loss, grad_x.reshape(x.shape), *[grad[n] for n in names], *[delta[n] for n in names],
            *[new_m[n] for n in names], *[new_v[n] for n in names])
```

```python
import functools

import jax
import jax.numpy as jnp
from jax import lax
from jax.experimental import pallas as pl
from jax.experimental.pallas import tpu as pltpu

F32 = jnp.float32
BF16 = jnp.bfloat16
MESH = pl.DeviceIdType.MESH

EPS = 1e-6
SSM_HEAD_DIM = 64
SSM_HEADS_PER_GROUP = 4
SSM_D_STATE = 128
SSM_CHUNK = 128
GDN_HEAD = 128
GDN_CHUNK = 64
CONV_K = 4
LANE_DT, LANE_B, LANE_A = 0, 32, 48
ADAM_LR, ADAM_B1, ADAM_B2, ADAM_EPS, ADAM_WD, ADAM_STEP = 0.001, 0.9, 0.999, 1e-08, 0.01, 10

VMEM_LIMIT_BYTES = 56 * 1024 * 1024
LANES = 128
N_DEV = 8

NN = (((1,), (0,)), ((), ()))
NT = (((1,), (1,)), ((), ()))
TN = (((0,), (0,)), ((), ()))


BNN = (((2,), (1,)), ((0,), (0,)))
BNT = (((2,), (2,)), ((0,), (0,)))
BTN = (((1,), (1,)), ((0,), (0,)))
_KIND = {NN: ("NN", 0), NT: ("NT", 0), TN: ("TN", 0), BNN: ("NN", 1), BNT: ("NT", 1), BTN: ("TN", 1)}
_DIMS = {"NN": (NN, BNN), "NT": (NT, BNT), "TN": (TN, BTN)}


def _dg(a, b, dims):
    return lax.dot_general(a, b, dims, preferred_element_type=F32)


def _raw_bf16(a, b, dims):
    return _dg(a.astype(BF16), b.astype(BF16), dims)


def _raw_bf16x3(a, b, dims):
    ah, bh = a.astype(BF16), b.astype(BF16)
    al, bl = (a - ah.astype(F32)).astype(BF16), (b - bh.astype(F32)).astype(BF16)
    return _dg(ah, bh, dims) + (_dg(ah, bl, dims) + _dg(al, bh, dims))


def _make_dot(raw):
    @functools.partial(jax.custom_vjp, nondiff_argnums=(2,))
    def dot(a, b, dims):
        return raw(a, b, dims)

    def fwd(a, b, dims):
        return raw(a, b, dims), (a, b)

    def bwd(dims, res, ct):
        a, b = res
        kind, batched = _KIND[dims]
        d = lambda k: _DIMS[k][batched]
        if kind == "NN":
            da, db = raw(ct, b, d("NT")), raw(a, ct, d("TN"))
        elif kind == "NT":
            da, db = raw(ct, b, d("NN")), raw(ct, a, d("TN"))
        else:
            da, db = raw(b, ct, d("NT")), raw(a, ct, d("NN"))
        return da.astype(a.dtype), db.astype(b.dtype)

    dot.defvjp(fwd, bwd)
    return lambda a, b, dims=NN: dot(a, b, dims)


_bdot = _make_dot(_raw_bf16)
_hdot = _make_dot(_raw_bf16x3)


def _mask_dot(mask, x, dims, mask_first=True):
    m = mask.astype(BF16)
    hi = x.astype(BF16)
    r = x - hi.astype(F32)
    mid = r.astype(BF16)
    lo = (r - mid.astype(F32)).astype(BF16)
    return sum(_dg(m, p, dims) if mask_first else _dg(p, m, dims) for p in (hi, mid, lo))


def _sigmoid(x):
    return 1.0 / (1.0 + jnp.exp(-x))


def _silu(x):
    return x * _sigmoid(x)


def _softplus(x):
    return jnp.maximum(x, 0.0) + jnp.log(1.0 + jnp.exp(-jnp.abs(x)))


def _rms(x, w):
    return x * lax.rsqrt(jnp.mean(x * x, axis=-1, keepdims=True) + EPS) * w


def _lane_col(m, idx):
    lane = lax.broadcasted_iota(jnp.int32, m.shape, 1)
    return jnp.sum(jnp.where(lane == idx, m, 0.0), axis=1, keepdims=True)


def _tril(n, strict=False, seg=None):
    r = lax.broadcasted_iota(jnp.int32, (n, n), 0)
    c = lax.broadcasted_iota(jnp.int32, (n, n), 1)
    low = (r > c) if strict else (r >= c)
    if seg is None or seg >= n:
        return low
    shift = seg.bit_length() - 1
    return jnp.logical_and(low, (r >> shift) == (c >> shift))


def _first_lane(shape):
    return lax.broadcasted_iota(jnp.int32, shape, len(shape) - 1) == 0


@jax.custom_vjp
def _row_form(col):
    shape = col.shape[:-1] + (LANES,)
    return _mask_dot(_first_lane(shape), jnp.broadcast_to(col, shape), NT if col.ndim == 2 else BNT)


def _row_form_fwd(col):
    return _row_form(col), None


def _row_form_bwd(_, ct):
    shape = ct.shape[:-1] + (LANES,)
    sums = _mask_dot(_first_lane(shape), ct, TN if ct.ndim == 2 else BTN, mask_first=False)
    return (jnp.sum(sums, axis=-1, keepdims=True),)


_row_form.defvjp(_row_form_fwd, _row_form_bwd)


@functools.partial(jax.custom_vjp, nondiff_argnums=(1,))
def _cumsum_rows(x, seg):
    return _mask_dot(_tril(x.shape[0], seg=seg), x, NN)


def _cumsum_rows_fwd(x, seg):
    return _cumsum_rows(x, seg), None


def _cumsum_rows_bwd(seg, _, ct):
    return (_mask_dot(_tril(ct.shape[0], seg=seg), ct, TN),)


_cumsum_rows.defvjp(_cumsum_rows_fwd, _cumsum_rows_bwd)


def _head_rows(m_t, idx):
    sub = lax.broadcasted_iota(jnp.int32, m_t.shape, 0)
    return jnp.sum(jnp.where(sub == idx, m_t, 0.0), axis=0, keepdims=True)


def _params(sem):
    return pltpu.CompilerParams(dimension_semantics=sem, vmem_limit_bytes=VMEM_LIMIT_BYTES)


def _into_plumbing(outs, first_input):
    arrays, aliases = [], {}
    for k, o in enumerate(outs):
        if len(o) > 4 and not isinstance(o[4], jax.ShapeDtypeStruct):
            aliases[first_input + len(arrays)] = k
            arrays.append(o[4])
    return arrays, aliases


def blockmap(name, fn, grid, ins, outs, accs=(), scalars=None):
    n_in, n_out, n_acc = len(ins), len(outs), len(accs)
    n_grid = len(grid)
    n_pre = 0 if scalars is None else 1
    into_arrays, aliases = _into_plumbing(outs, n_pre + n_in)
    n_into = len(into_arrays)

    def body(*refs):
        refs = refs[n_pre:n_pre + n_in] + refs[n_pre + n_in + n_into:]
        vals = fn(*[r[...] for r in refs[:n_in]])
        if not isinstance(vals, (tuple, list)):
            vals = (vals,)
        for r, v in zip(refs[n_in:n_in + n_out], vals[:n_out]):
            r[...] = v.astype(r.dtype)
        if n_acc:
            first = functools.reduce(jnp.logical_and, [pl.program_id(a) == 0 for a in range(n_grid)])
            acc_refs = refs[n_in + n_out:]

            @pl.when(first)
            def _():
                for r in acc_refs:
                    r[...] = jnp.zeros(r.shape, r.dtype)

            for r, v in zip(acc_refs, vals[n_out:]):
                r[...] += v.astype(r.dtype)

    zeros = lambda nd: (lambda *_: (0,) * nd)
    in_specs = [pl.BlockSpec(b, im) for _, b, im in ins] + [pl.BlockSpec(memory_space=pl.ANY)] * n_into
    out_specs = [pl.BlockSpec(o[2], o[3]) for o in outs] + [pl.BlockSpec(s, zeros(len(s))) for s, _ in accs]
    out_shape = [jax.ShapeDtypeStruct(o[0], o[1]) for o in outs] + [jax.ShapeDtypeStruct(s, d) for s, d in accs]
    cparams = _params(("arbitrary",) * n_grid if n_acc else ("parallel",) * n_grid)
    arrays = [a for a, _, _ in ins] + into_arrays
    if scalars is None:
        return pl.pallas_call(body, name=name, grid=grid, in_specs=in_specs, out_specs=out_specs, out_shape=out_shape,
                              input_output_aliases=aliases, compiler_params=cparams)(*arrays)
    spec = pltpu.PrefetchScalarGridSpec(num_scalar_prefetch=1, grid=grid, in_specs=in_specs, out_specs=out_specs)
    return pl.pallas_call(body, name=name, grid_spec=spec, out_shape=out_shape, input_output_aliases=aliases,
                          compiler_params=cparams)(scalars, *arrays)


def rowmap(name, fn, rows, consts, outs, accs=(), rb=256):
    norm = [(r, r.shape[1], 0) if not isinstance(r, tuple) else (r[0], r[1], r[2] // r[1]) for r in rows]
    assert all(not isinstance(r, tuple) or r[2] % r[1] == 0 for r in rows)
    t = norm[0][0].shape[0]
    rb = min(rb, t)
    ins = [(a, (rb, n), (lambda i, cb=cb: (i, cb))) for a, n, cb in norm]
    ins += [(cst, cst.shape, (lambda i, nd=cst.ndim: (0,) * nd)) for cst in consts]
    o = []
    for out in outs:
        if len(out) == 2:
            o.append(((t, out[0]), out[1], (rb, out[0]), lambda i: (i, 0)))
        else:
            n, d, into, off = out
            assert off % n == 0 and into.dtype == d
            o.append((into.shape, d, (rb, n), (lambda i, cb=off // n: (i, cb)), into))
    return blockmap(name, fn, (t // rb,), ins, o, accs)


MM_TILE_M, MM_TILE_N, MM_TILE_K = 1024, 1024, 2048


def _tile(dim, cap):
    if dim <= cap:
        return dim
    best = max(t for t in range(LANES, cap + 1, LANES) if dim % t == 0)
    return best


def matmul(name, a, b, ta=False, tb=False, out_dtypes=(F32,), epi=None, extras=(), comm=None):
    (k_dim, m_dim) = a.shape if ta else a.shape[::-1]
    n_dim = b.shape[0] if tb else b.shape[1]
    assert (b.shape[1] if tb else b.shape[0]) == k_dim, (name, a.shape, b.shape)
    tm, tn, tk = _tile(m_dim, MM_TILE_M), _tile(n_dim, MM_TILE_N), _tile(k_dim, MM_TILE_K)
    grid = (m_dim // tm, n_dim // tn, k_dim // tk)
    k_steps = grid[2]
    n_extra, n_out = len(extras), len(out_dtypes)
    n_cin, n_cout = (len(comm.operands), len(comm.out_shapes)) if comm else (0, 0)
    dims = (((0 if ta else 1,), (1 if tb else 0,)), ((), ()))

    def body(*refs):
        ins, outs, scratch = refs[:2 + n_extra + n_cin], refs[2 + n_extra + n_cin:][:n_out + n_cout], refs[2 + n_extra + n_cin + n_out + n_cout:]
        extra_refs, out_refs = ins[2:2 + n_extra], outs[:n_out]
        ids = [pl.program_id(ax) for ax in range(3)]
        if comm:
            comm_refs = (ins[2 + n_extra:], outs[n_out:], scratch[-2], scratch[-1])

            @pl.when(functools.reduce(jnp.logical_and, [i == 0 for i in ids]))
            def _():
                comm.start(*comm_refs)

        def finish(acc):
            vals = (acc,) if epi is None else epi(acc, *[r[...] for r in extra_refs])
            if not isinstance(vals, (tuple, list)):
                vals = (vals,)
            for r, v in zip(out_refs, vals):
                r[...] = v.astype(r.dtype)

        prod = lax.dot_general(ins[0][...].astype(BF16), ins[1][...].astype(BF16), dims, preferred_element_type=F32)
        if k_steps == 1:
            finish(prod)
        else:
            acc_ref = scratch[0]

            @pl.when(ids[2] == 0)
            def _():
                acc_ref[...] = jnp.zeros(acc_ref.shape, F32)

            acc_ref[...] += prod

            @pl.when(ids[2] == k_steps - 1)
            def _():
                finish(acc_ref[...])

        if comm:
            @pl.when(functools.reduce(jnp.logical_and, [i == g - 1 for i, g in zip(ids, grid)]))
            def _():
                comm.finish(*comm_refs)

    a_spec = pl.BlockSpec((tk, tm), lambda i, j, k: (k, i)) if ta else pl.BlockSpec((tm, tk), lambda i, j, k: (i, k))
    b_spec = pl.BlockSpec((tn, tk), lambda i, j, k: (j, k)) if tb else pl.BlockSpec((tk, tn), lambda i, j, k: (k, j))
    mn_spec = pl.BlockSpec((tm, tn), lambda i, j, k: (i, j))
    scratch_shapes = [] if k_steps == 1 else [pltpu.VMEM((tm, tn), F32)]
    if comm:
        scratch_shapes += [pltpu.SemaphoreType.DMA(comm.sem_shape), pltpu.SemaphoreType.DMA(comm.sem_shape)]
    res = pl.pallas_call(
        body, name=name, grid=grid,
        in_specs=[a_spec, b_spec] + [mn_spec] * n_extra + [ANY] * n_cin,
        out_specs=[mn_spec] * n_out + [ANY] * n_cout,
        out_shape=[jax.ShapeDtypeStruct((m_dim, n_dim), d) for d in out_dtypes] + (comm.out_shapes if comm else []),
        scratch_shapes=scratch_shapes,
        compiler_params=_params(("arbitrary",) * 3 if comm else ("parallel", "parallel", "arbitrary")),
    )(a, b, *extras, *(comm.operands if comm else []))
    main = res[:n_out] if n_out > 1 else res[0]
    return (main, list(res[n_out:])) if comm else main


def ssd_step(g0, state, xs, bm, cm, small, z, p_dtb, p_alog, p_dsk, nw):
    hb, n = state.shape[0], xs.shape[0]
    n_pair, n_head = 2 * hb, 4 * hb
    causal = _tril(n)
    dt_all = _softplus(small + p_dtb)
    a_all = dt_all * (-jnp.exp(p_alog))
    acum_all = _cumsum_rows(a_all, n)
    acum_t = acum_all.T
    lane0 = LANE_DT + SSM_HEADS_PER_GROUP * g0
    sub = lax.broadcasted_iota(jnp.int32, acum_t.shape, 0)
    heads = range(n_head)
    acum = jnp.stack([_lane_col(acum_all, lane0 + i) for i in heads])
    acum_row = jnp.stack([jnp.sum(jnp.where(sub == lane0 + i, acum_t, 0.0), axis=0, keepdims=True) for i in heads])
    dt = jnp.stack([_lane_col(dt_all, lane0 + i) for i in heads])
    dsk = jnp.stack([_lane_col(p_dsk, lane0 + i) for i in heads])
    decay = jnp.exp(jnp.where(causal, acum - acum_row, -jnp.inf))
    a_last = acum[:, n - 1:n, :]

    def split(a):
        return [a[:, i * LANES:(i + 1) * LANES] for i in range(a.shape[1] // LANES)]

    def pairs(a, axis=2):
        even = jnp.stack([a[2 * p] for p in range(n_pair)])
        odd = jnp.stack([a[2 * p + 1] for p in range(n_pair)])
        shape = (n_pair, LANES, LANES) if axis == 1 else (n_pair, a.shape[1], LANES)
        return jnp.where(lax.broadcasted_iota(jnp.int32, shape, axis) < SSM_HEAD_DIM, even, odd)

    bms, cms = split(bm), split(cm)
    cb = _bdot(jnp.stack(cms), jnp.stack(bms), BNT)
    cbd = jnp.stack([cb[i // SSM_HEADS_PER_GROUP] for i in heads]) * decay
    xp = jnp.stack(split(xs))
    xdt = xp * pairs(dt)
    yd = _bdot(cbd, jnp.stack([xdt[i // 2] for i in heads]), BNN)
    lane = lax.broadcasted_iota(jnp.int32, (n_pair, n, LANES), 2)
    y_diag = jnp.where(lane < SSM_HEAD_DIM, jnp.stack([yd[2 * p] for p in range(n_pair)]), jnp.stack([yd[2 * p + 1] for p in range(n_pair)]))
    st = state.reshape(n_pair, LANES, LANES)
    cm2 = jnp.stack([cms[p // 2] for p in range(n_pair)])
    bm2 = jnp.stack([bms[p // 2] for p in range(n_pair)])
    y_off = _bdot(cm2, st, BNT) * pairs(jnp.exp(acum))
    new = st * pairs(jnp.exp(a_last), axis=1) + _bdot(xdt * pairs(jnp.exp(a_last - acum)), bm2, BTN)
    y = y_diag + y_off + pairs(dsk) * xp
    y = jnp.concatenate([y[p] for p in range(n_pair)], axis=1) * _silu(z)
    wide = 2 * LANES
    y = jnp.concatenate([_rms(y[:, i * wide:(i + 1) * wide], nw[i]) for i in range(hb)], axis=1)
    return new.reshape(state.shape), y


@functools.partial(jax.custom_vjp, nondiff_argnums=(1,))
def _unit_lower_inverse(a, seg):
    n = a.shape[-1]
    r = lax.broadcasted_iota(jnp.int32, (n, n), 0)
    c = lax.broadcasted_iota(jnp.int32, (n, n), 1)
    shift = min(INVERSE_BASE, seg).bit_length() - 1
    power = jnp.where((r >> shift) == (c >> shift), a, 0.0)
    inv = (r == c).astype(F32) - power
    span = 2
    while span < (1 << shift):
        power = _hdot(power, power, BNN)
        inv = inv + _hdot(inv, power, BNN)
        span *= 2
    while (1 << shift) < seg:
        below = jnp.logical_and((r >> (shift + 1)) == (c >> (shift + 1)), (r >> shift) != (c >> shift))
        inv = inv - _hdot(inv, _hdot(jnp.where(below, a, 0.0), inv, BNN), BNN)
        shift += 1
    return inv


def _unit_lower_inverse_fwd(a, seg):
    inv = _unit_lower_inverse(a, seg)
    return inv, inv


def _unit_lower_inverse_bwd(seg, inv, ct):
    return (-_hdot(_hdot(inv, ct, BTN), inv, BNT),)


_unit_lower_inverse.defvjp(_unit_lower_inverse_fwd, _unit_lower_inverse_bwd)


def gdn_step(hq0, state, q, k, v, z, small, p_dtb, p_alog, nw):
    n, chunk = q.shape[0], GDN_CHUNK
    hb = state.shape[0]
    nb = 2 * hb
    cur = state.reshape(nb, LANES, LANES)
    causal, strict = _tril(n, seg=chunk), _tril(n, True, seg=chunk)
    beta_all = _sigmoid(small)
    g_all = -jnp.exp(p_alog) * _softplus(small + p_dtb)
    gcum_all = _cumsum_rows(g_all, chunk)
    gcum_t = gcum_all.T
    split = lambda a: [a[:, i * LANES:(i + 1) * LANES] for i in range(a.shape[1] // LANES)]
    per_value_head = lambda a: jnp.stack([a[i // 2] for i in range(nb)])
    qh, kh = jnp.stack(split(q)) * (GDN_HEAD ** -0.5), jnp.stack(split(k))
    q2, k2 = per_value_head(qh), per_value_head(kh)
    v2, z2 = jnp.stack(split(v)), jnp.stack(split(z))
    gcum = jnp.stack([_lane_col(gcum_all, LANE_A + 2 * hq0 + i) for i in range(nb)])
    gcum_row = jnp.stack([_head_rows(gcum_t, LANE_A + 2 * hq0 + i) for i in range(nb)])
    beta = jnp.stack([_lane_col(beta_all, LANE_B + 2 * hq0 + i) for i in range(nb)])
    dmat = jnp.exp(jnp.where(causal, gcum - gcum_row, -jnp.inf))
    a_low = jnp.where(strict, beta * per_value_head(_bdot(kh, kh, BNT)) * dmat, 0.0)
    inv = _unit_lower_inverse(a_low, chunk)
    egc = jnp.exp(gcum)
    u = _hdot(inv, v2 * beta, BNN)
    w = _hdot(inv, k2 * (beta * egc), BNN)
    q_dec = q2 * egc
    v_new, o_state = [], []
    for s in range(n // chunk):
        rows = slice(s * chunk, (s + 1) * chunk)
        v_new.append(u[:, rows] - _bdot(w[:, rows], cur, BNN))
        o_state.append(_bdot(q_dec[:, rows], cur, BNN))
        g_last = gcum[:, (s + 1) * chunk - 1:(s + 1) * chunk, :]
        k_dec = k2[:, rows] * jnp.exp(g_last - gcum[:, rows])
        cur = cur * jnp.exp(g_last) + _bdot(k_dec, v_new[-1], BTN)
    o = jnp.concatenate(o_state, axis=1) + _bdot(per_value_head(_bdot(qh, kh, BNT)) * dmat, jnp.concatenate(v_new, axis=1), BNN)
    out = _rms(o, nw) * _silu(z2)
    return cur.reshape(state.shape), jnp.concatenate([out[i] for i in range(nb)], axis=1)


STATE_SHAPE = (2, LANES, LANES)
SSD_GROUPS_PER_STEP = 4
GDN_HEADS_PER_STEP = 4
GDN_CHUNKS_PER_STEP = 2
INVERSE_BASE = 16


def _scan_specs(rows, consts, chunk, chunk_of, hb):
    specs = []
    for _, n, off, per_group in rows:
        if per_group:
            assert off % (n * hb) == 0
            specs.append(pl.BlockSpec((chunk, n * hb), lambda c, g, cb=off // (n * hb): (chunk_of(c), cb + g)))
        else:
            assert off % n == 0
            specs.append(pl.BlockSpec((chunk, n), lambda c, g, cb=off // n: (chunk_of(c), cb)))
    for arr, per_group in consts:
        if per_group:
            specs.append(pl.BlockSpec((hb, 1, arr.shape[2]), lambda c, g: (g, 0, 0)))
        else:
            specs.append(pl.BlockSpec(arr.shape, lambda c, g, nd=arr.ndim: (0,) * nd))
    return specs


def scan_fwd(name, step, chunk, n_grp, rows, consts, out_cols, hb):
    t = rows[0][0].shape[0]
    nc = t // chunk
    n_rows, n_consts = len(rows), len(consts)

    def body(*refs):
        row_refs, const_refs = refs[:n_rows], refs[n_rows:n_rows + n_consts]
        y_ref, st_ref, state = refs[n_rows + n_consts:]
        c, g = pl.program_id(0), pl.program_id(1)

        @pl.when(c == 0)
        def _():
            state[g] = jnp.zeros((hb,) + STATE_SHAPE, F32)

        st = state[g]
        st_ref[...] = st
        new, y = step(g * hb, st, *[r[...] for r in row_refs], *[r[...] for r in const_refs])
        state[g] = new
        y_ref[...] = y.astype(y_ref.dtype)

    return pl.pallas_call(
        body, name=name, grid=(nc, n_grp // hb),
        in_specs=_scan_specs(rows, consts, chunk, lambda c: c, hb),
        out_specs=[pl.BlockSpec((chunk, out_cols * hb), lambda c, g: (c, g)),
                   pl.BlockSpec((None, None, hb) + STATE_SHAPE, lambda c, g: (c, g, 0, 0, 0, 0))],
        out_shape=[jax.ShapeDtypeStruct((t, n_grp * out_cols), BF16),
                   jax.ShapeDtypeStruct((nc, n_grp // hb, hb) + STATE_SHAPE, F32)],
        scratch_shapes=[pltpu.VMEM((n_grp // hb, hb) + STATE_SHAPE, F32)],
        compiler_params=_params(("arbitrary", "arbitrary")),
    )(*[r[0] for r in rows], *[c[0] for c in consts])


def scan_bwd(name, step, chunk, n_grp, rows, consts, states, dy, row_dtypes, hb, into):
    t = rows[0][0].shape[0]
    nc = t // chunk
    n_rows, n_consts = len(rows), len(consts)
    out_cols = dy.shape[1] // n_grp
    n_alias = sum(not isinstance(v, jax.ShapeDtypeStruct) for v in into.values())

    def body(*refs):
        row_refs, const_refs = refs[:n_rows], refs[n_rows:n_rows + n_consts]
        st_ref, dy_ref = refs[n_rows + n_consts:n_rows + n_consts + 2]
        outs = refs[n_rows + n_consts + 2 + n_alias:-1]
        dstate = refs[-1]
        c, g = pl.program_id(0), pl.program_id(1)

        @pl.when(c == 0)
        def _():
            dstate[g] = jnp.zeros((hb,) + STATE_SHAPE, F32)

        @pl.when(jnp.logical_and(c == 0, g == 0))
        def _():
            for r in outs[n_rows:]:
                r[...] = jnp.zeros(r.shape, r.dtype)

        _, vjp = jax.vjp(functools.partial(step, g * hb), st_ref[...], *[r[...] for r in row_refs], *[r[...] for r in const_refs])
        grads = vjp((dstate[g], dy_ref[...].astype(F32)))
        dstate[g] = grads[0]
        for (_, _, _, per_group), r, d in zip(rows, outs[:n_rows], grads[1:1 + n_rows]):
            if per_group:
                r[...] = d.astype(r.dtype)
            else:
                @pl.when(g == 0)
                def _(r=r):
                    r[...] = jnp.zeros(r.shape, r.dtype)

                r[...] += d.astype(r.dtype)
        for (_, per_group), r, d in zip(consts, outs[n_rows:], grads[1 + n_rows:]):
            if per_group:
                r[pl.ds(g * hb, hb)] += d
            else:
                r[...] += d

    rev = lambda c: nc - 1 - c
    out_specs, out_shape = [], []
    into_arrays, aliases = [], {}
    first_into = n_rows + n_consts + 2
    for k, ((_, n, off, per_group), dt) in enumerate(zip(rows, row_dtypes)):
        if k in into:
            assert per_group and off % (n * hb) == 0 and into[k].dtype == dt
            out_specs.append(pl.BlockSpec((chunk, n * hb), lambda c, g, cb=off // (n * hb): (rev(c), cb + g)))
            out_shape.append(jax.ShapeDtypeStruct(into[k].shape, dt))
            if not isinstance(into[k], jax.ShapeDtypeStruct):
                aliases[first_into + len(into_arrays)] = k
                into_arrays.append(into[k])
        elif per_group:
            out_specs.append(pl.BlockSpec((chunk, n * hb), lambda c, g: (rev(c), g)))
            out_shape.append(jax.ShapeDtypeStruct((t, n_grp * n), dt))
        else:
            out_specs.append(pl.BlockSpec((chunk, n), lambda c, g: (rev(c), 0)))
            out_shape.append(jax.ShapeDtypeStruct((t, n), dt))
    for arr, _ in consts:
        out_specs.append(pl.BlockSpec(arr.shape, lambda c, g, nd=arr.ndim: (0,) * nd))
        out_shape.append(jax.ShapeDtypeStruct(arr.shape, F32))
    return pl.pallas_call(
        body, name=name, grid=(nc, n_grp // hb),
        in_specs=_scan_specs(rows, consts, chunk, rev, hb)
        + [pl.BlockSpec((None, None, hb) + STATE_SHAPE, lambda c, g: (rev(c), g, 0, 0, 0, 0)),
           pl.BlockSpec((chunk, out_cols * hb), lambda c, g: (rev(c), g))] + [pl.BlockSpec(memory_space=pl.ANY)] * len(into_arrays),
        out_specs=out_specs, out_shape=out_shape, input_output_aliases=aliases,
        scratch_shapes=[pltpu.VMEM((n_grp // hb, hb) + STATE_SHAPE, F32)],
        compiler_params=_params(("arbitrary", "arbitrary")),
    )(*[r[0] for r in rows], *[c[0] for c in consts], states, dy, *into_arrays)


@functools.partial(jax.custom_vjp, nondiff_argnums=(1,))
def _shift_rows(x, k):
    t = x.shape[0]
    row = lax.broadcasted_iota(jnp.int32, x.shape, 0)
    rolled = pltpu.roll(x, k % t, 0)
    return jnp.where(jnp.logical_and(row >= k, row < t + k), rolled, 0.0)


def _shift_rows_fwd(x, k):
    return _shift_rows(x, k), None


def _shift_rows_bwd(k, _, dy):
    return (_shift_rows(dy, -k),)


_shift_rows.defvjp(_shift_rows_fwd, _shift_rows_bwd)


def _conv_silu(x, cw, cb):
    pre = cb + sum(cw[j:j + 1, :] * _shift_rows(x, CONV_K - 1 - j) for j in range(CONV_K))
    return _silu(pre)


def _conv_silu_l2(x, cw, cb):
    y = _conv_silu(x, cw, cb)
    return y * lax.rsqrt(jnp.sum(y * y, axis=-1, keepdims=True) + EPS)


def conv_fwd(name, fn, src, n, off, cw, cb, cw_off):
    t = src.shape[0]
    sb, wb = off // LANES, cw_off // LANES
    ins = [(src, (t, LANES), lambda i: (0, sb + i)), (cw, (CONV_K, LANES), lambda i: (0, wb + i)),
           (cb, (1, LANES), lambda i: (0, wb + i))]
    return blockmap(name, fn, (n // LANES,), ins, [((t, n), F32, (t, LANES), lambda i: (0, i))])[0]


def conv_bwd(name, fn, src, n, off, cw, cb, cw_off, dy, into):
    t = src.shape[0]
    sb, wb = off // LANES, cw_off // LANES

    def bwd(x, w, b, d):
        _, vjp = jax.vjp(fn, x, w, b)
        return vjp(d.astype(F32))

    ins = [(src, (t, LANES), lambda i: (0, sb + i)), (cw, (CONV_K, LANES), lambda i: (0, wb + i)),
           (cb, (1, LANES), lambda i: (0, wb + i)), (dy, (t, LANES), lambda i: (0, i))]
    outs = [(into.shape, BF16, (t, LANES), lambda i: (0, sb + i), into), ((CONV_K, n), F32, (CONV_K, LANES), lambda i: (0, i)),
            ((1, n), F32, (1, LANES), lambda i: (0, i))]
    return blockmap(name, bwd, (n // LANES,), ins, outs)


def _place():
    return lax.axis_index("x"), lax.axis_index("y"), lax.axis_index("c")


def _other_chips(x, y):
    return [(1 - x, y), (x, 1 - y), (1 - x, 1 - y)]


ANY = pl.BlockSpec(memory_space=pl.ANY)


def all_gather8(name, v):
    m_per, n = v.shape

    def body(x_ref, out_ref, send_sems, recv_sems, local_sem):
        x, y, c = _place()
        me, sibling = (x, y, c), (x, y, 1 - c)
        chips = _other_chips(x, y)

        def rows(px, py, pc):
            return out_ref.at[pl.ds((4 * px + 2 * py + pc) * m_per, m_per), :]

        def copy(k, block, to, src=None):
            return pltpu.make_async_remote_copy(
                src_ref=rows(*block) if src is None else src, dst_ref=rows(*block),
                send_sem=send_sems.at[k], recv_sem=recv_sems.at[k], device_id=to, device_id_type=MESH)

        mine = pltpu.make_async_copy(x_ref, rows(*me), local_sem)
        mine.start()
        first = [copy(0, me, sibling, src=x_ref)]
        first += [copy(1 + q, me, (*chip, c), src=x_ref) for q, chip in enumerate(chips)]
        for cp in first:
            cp.start()
        passed = [copy(4 + q, (*chip, c), sibling) for q, chip in enumerate(chips)]
        for q, chip in enumerate(chips):
            copy(1 + q, (*chip, c), me).wait_recv()
            passed[q].start()
        copy(0, sibling, me).wait_recv()
        for q, chip in enumerate(chips):
            copy(4 + q, (*chip, 1 - c), me).wait_recv()
        for cp in first + passed:
            cp.wait_send()
        mine.wait()

    return pl.pallas_call(
        body, name=name, out_shape=jax.ShapeDtypeStruct((N_DEV * m_per, n), v.dtype),
        in_specs=[pl.BlockSpec(memory_space=pltpu.VMEM)], out_specs=pl.BlockSpec(memory_space=pltpu.VMEM),
        scratch_shapes=[pltpu.SemaphoreType.DMA((7,)), pltpu.SemaphoreType.DMA((7,)), pltpu.SemaphoreType.DMA],
    )(v)


def gather_flat(name, vec):
    n = vec.shape[0]
    n_pad = -(-n // (8 * LANES)) * (8 * LANES)
    v = jnp.pad(vec, (0, n_pad - n)).reshape(8, n_pad // 8)
    return all_gather8(name, v).reshape(N_DEV, n_pad)[:, :n]


class Exchange:
    def __init__(self, operands, out_shapes, sem_shape, start, finish):
        self.operands, self.out_shapes, self.sem_shape, self.start, self.finish = list(operands), out_shapes, sem_shape, start, finish


def _start_all_wait_all(make_copies):
    def start(*refs):
        for cp in make_copies(*refs):
            cp.start()

    def finish(*refs):
        for cp in make_copies(*refs):
            cp.wait()

    return start, finish


def run_exchange(name, ex):
    n_in, n_out = len(ex.operands), len(ex.out_shapes)

    def body(*refs):
        ins, outs = refs[:n_in], refs[n_in:n_in + n_out]
        ex.start(ins, outs, *refs[n_in + n_out:])
        ex.finish(ins, outs, *refs[n_in + n_out:])

    return pl.pallas_call(
        body, name=name, out_shape=ex.out_shapes, in_specs=[ANY] * n_in, out_specs=[ANY] * n_out,
        scratch_shapes=[pltpu.SemaphoreType.DMA(ex.sem_shape), pltpu.SemaphoreType.DMA(ex.sem_shape)],
    )(*ex.operands)


def _half(shape, axis, pc):
    h = shape[axis] // 2
    return (pl.ds(pc * h, h), slice(None)) if axis == 0 else (slice(None), pl.ds(pc * h, h))


def _half_shape(shape, axis):
    return tuple(s // 2 if a == axis else s for a, s in enumerate(shape))


def all_gather_shards(shards, axes):
    n_t = len(shards)

    def copies(ins, outs, send_sems, recv_sems):
        x, y, c = _place()
        chips = _other_chips(x, y)

        def copy(t, k, quarter, pc, to, src=None):
            part = _half(ins[t].shape, axes[t], pc)
            dst = outs[t].at[(quarter,) + part]
            return pltpu.make_async_remote_copy(
                src_ref=dst if src is None else ins[t].at[part], dst_ref=dst,
                send_sem=send_sems.at[t, k], recv_sem=recv_sems.at[t, k], device_id=to, device_id_type=MESH)

        first = [copy(t, q, 2 * x + y, c, (*chip, c), src=True) for t in range(n_t) for q, chip in enumerate(chips)]
        landed = [copy(t, q, 2 * px + py, c, (x, y, c)) for t in range(n_t) for q, (px, py) in enumerate(chips)]
        passed = [copy(t, 3 + q, 2 * px + py, c, (x, y, 1 - c)) for t in range(n_t) for q, (px, py) in enumerate(chips)]
        from_sibling = [copy(t, 3 + q, 2 * px + py, 1 - c, (x, y, c)) for t in range(n_t) for q, (px, py) in enumerate(chips)]
        return first, landed, passed, from_sibling

    def start(*refs):
        for cp in copies(*refs)[0]:
            cp.start()

    def finish(*refs):
        first, landed, passed, from_sibling = copies(*refs)
        for arrived, onward in zip(landed, passed):
            arrived.wait_recv()
            onward.start()
        for cp in from_sibling:
            cp.wait_recv()
        for cp in first + passed:
            cp.wait_send()

    return Exchange(shards, [jax.ShapeDtypeStruct((4,) + s.shape, s.dtype) for s in shards], (n_t, 6), start, finish)


def exchange_halves_d2d(grads, axes):
    n_t = len(grads)

    def copies(ins, outs, send_sems, recv_sems):
        x, y, c = _place()
        return [pltpu.make_async_remote_copy(
            src_ref=ins[t].at[(slice(None),) + _half(ins[t].shape[1:], axes[t], 1 - c)], dst_ref=outs[t],
            send_sem=send_sems.at[t], recv_sem=recv_sems.at[t], device_id=(x, y, 1 - c), device_id_type=MESH) for t in range(n_t)]

    shapes = [jax.ShapeDtypeStruct((4,) + _half_shape(g.shape[1:], a), g.dtype) for g, a in zip(grads, axes)]
    return Exchange(grads, shapes, (n_t,), *_start_all_wait_all(copies))


def exchange_quarters_ici(parts):
    n_t = len(parts)

    def copies(ins, outs, send_sems, recv_sems):
        x, y, c = _place()
        return [pltpu.make_async_remote_copy(
            src_ref=ins[t].at[2 * px + py], dst_ref=outs[t].at[q],
            send_sem=send_sems.at[t, q], recv_sem=recv_sems.at[t, q], device_id=(px, py, c), device_id_type=MESH)
            for t in range(n_t) for q, (px, py) in enumerate(_other_chips(x, y))]

    shapes = [jax.ShapeDtypeStruct((3,) + p.shape[1:], p.dtype) for p in parts]
    return Exchange(parts, shapes, (n_t, 3), *_start_all_wait_all(copies))


def swap_d2d(halves):
    n_t = len(halves)

    def copies(ins, outs, send_sems, recv_sems):
        x, y, c = _place()
        return [pltpu.make_async_remote_copy(
            src_ref=ins[t], dst_ref=outs[t], send_sem=send_sems.at[t], recv_sem=recv_sems.at[t],
            device_id=(x, y, 1 - c), device_id_type=MESH) for t in range(n_t)]

    return Exchange(halves, [jax.ShapeDtypeStruct(h.shape, h.dtype) for h in halves], (n_t,), *_start_all_wait_all(copies))


BLOCK_BYTES = 1 << 20


def _row_block(r, c):
    fits = [rb for rb in range(16, r + 1, 16) if r % rb == 0 and rb * c * 4 <= BLOCK_BYTES]
    return max(fits) if fits else r


def _place_scalars():
    x, y, c = _place()
    return jnp.stack([c, 2 * x + y]).astype(jnp.int32)


def reduce_on_chip(tag, grads, axes):
    from_sibling = run_exchange(f"rs_d2d_{tag}", exchange_halves_d2d(grads, axes))
    parts, parts_bf16 = [], []
    for t, (g, s, axis) in enumerate(zip(grads, from_sibling, axes)):
        _, h, cols = s.shape
        rb = _row_block(h, cols)
        nb = h // rb
        blk = lambda k, i, s_ref: (k, i, 0)
        mine = (lambda k, i, s_ref, nb=nb: (k, s_ref[0] * nb + i, 0)) if axis == 0 else (lambda k, i, s_ref: (k, i, s_ref[0]))
        p32, p16 = blockmap(
            f"rs_add_{tag}{t}", lambda a, b: (a + b, a + b), (4, nb),
            [(g, (None, rb, cols), mine), (s, (None, rb, cols), blk)],
            [(s.shape, F32, (None, rb, cols), blk), (s.shape, BF16, (None, rb, cols), blk)], scalars=_place_scalars())
        parts.append(p32)
        parts_bf16.append(p16)
    return parts, parts_bf16


def reduce_across_chips(parts, from_chips):
    halves = []
    for t, (p, q) in enumerate(zip(parts, from_chips)):
        _, h, cols = p.shape
        rb = _row_block(h, cols)
        halves.append(blockmap(
            f"rs_sum{t}", lambda a, b: a + b[0].astype(F32) + b[1].astype(F32) + b[2].astype(F32), (h // rb,),
            [(p, (None, rb, cols), lambda i, s_ref: (s_ref[1], i, 0)), (q, (3, rb, cols), lambda i, s_ref: (0, i, 0))],
            [((h, cols), F32, (rb, cols), lambda i, s_ref: (i, 0))], scalars=_place_scalars())[0])
    return list(zip(halves, run_exchange("rs_swap", swap_d2d(halves))))


def _adamw(w, g, m, v):
    m = ADAM_B1 * m + (1.0 - ADAM_B1) * g
    v = ADAM_B2 * v + (1.0 - ADAM_B2) * jnp.square(g)
    m_hat = m / (1.0 - ADAM_B1 ** ADAM_STEP)
    v_hat = v / (1.0 - ADAM_B2 ** ADAM_STEP)
    delta = -ADAM_LR * (m_hat / (jnp.sqrt(v_hat) + ADAM_EPS) + ADAM_WD * w)
    return delta, m, v


def adamw(name, w, g, m, v):
    _, r, c = w.shape
    rb = _row_block(r, c)
    blk3 = lambda a: (a, (None, rb, c), lambda i: (0, i, 0))
    return blockmap(name, _adamw, (r // rb,), [blk3(w), (g, (rb, c), lambda i: (i, 0)), blk3(m), blk3(v)],
                    [(w.shape, F32, (None, rb, c), lambda i: (0, i, 0))] * 3)


def adamw_halves(name, w, mine, other, m, v, axis):
    _, r, c = w.shape
    h, c = mine.shape
    rb = _row_block(h, c)
    nb = h // rb

    def body(s_ref, w_ref, mine_ref, other_ref, m_ref, v_ref, g_out, d_out, m_out, v_out):
        g = jnp.where(pl.program_id(0) == s_ref[0], mine_ref[...], other_ref[...])
        d, nm, nv = _adamw(w_ref[...], g, m_ref[...], v_ref[...])
        g_out[...], d_out[...], m_out[...], v_out[...] = g, d, nm, nv

    spec3 = pl.BlockSpec((None, rb, c), (lambda k, i, s_ref: (0, k * nb + i, 0)) if axis == 0 else (lambda k, i, s_ref: (0, i, k)))
    spec2 = pl.BlockSpec((rb, c), lambda k, i, s_ref: (i, 0))
    grid_spec = pltpu.PrefetchScalarGridSpec(num_scalar_prefetch=1, grid=(2, nb), in_specs=[spec3, spec2, spec2, spec3, spec3],
                                             out_specs=[spec3] * 4)
    return pl.pallas_call(body, name=name, grid_spec=grid_spec, out_shape=[jax.ShapeDtypeStruct(w.shape, F32)] * 4,
                          compiler_params=_params(("parallel", "parallel")))(_place_scalars(), w, mine, other, m, v)


def _whole(name, fn, ins, outs):
    return blockmap(name, fn, (1,), [(a, a.shape, lambda i, nd=a.ndim: (0,) * nd) for a in ins],
                    [(s, d, s, lambda i, nd=len(s): (0,) * nd) for s, d in outs])


def _premix(x, w, sc, sh):
    return _rms(x, w) * (1.0 + sc) + sh


def _postmix(x, u, w_post, g1, w_pre2, sc2, sh2):
    x1 = x + g1 * _rms(u, w_post)
    return x1, _premix(x1, w_pre2, sc2, sh2)


def _merge(gs, gg, ys, yg):
    return _sigmoid(gs) * ys + _sigmoid(gg) * yg


def _final(x1, y2, w_post2, g2):
    return x1 + g2 * _rms(y2, w_post2)


def kernel(x, c, w_ada, b_ada, norm_mix_pre, norm_mix_post, w_in, ssm_conv_w, ssm_conv_b, ssm_dt_bias, ssm_A_log, ssm_D, ssm_norm_w, gdn_conv_w, gdn_dt_bias, gdn_A_log, gdn_norm_w, w_ssm_up, w_gdn_up, w_out, norm_mlp_pre, norm_mlp_post, w_mlp_up, w_mlp_down, loss_target, m_w_ada, m_b_ada, m_norm_mix_pre, m_norm_mix_post, m_w_in, m_ssm_conv_w, m_ssm_conv_b, m_ssm_dt_bias, m_ssm_A_log, m_ssm_D, m_ssm_norm_w, m_gdn_conv_w, m_gdn_dt_bias, m_gdn_A_log, m_gdn_norm_w, m_w_ssm_up, m_w_gdn_up, m_w_out, m_norm_mlp_pre, m_norm_mlp_post, m_w_mlp_up, m_w_mlp_down, v_w_ada, v_b_ada, v_norm_mix_pre, v_norm_mix_post, v_w_in, v_ssm_conv_w, v_ssm_conv_b, v_ssm_dt_bias, v_ssm_A_log, v_ssm_D, v_ssm_norm_w, v_gdn_conv_w, v_gdn_dt_bias, v_gdn_A_log, v_gdn_norm_w, v_w_ssm_up, v_w_gdn_up, v_w_out, v_norm_mlp_pre, v_norm_mlp_post, v_w_mlp_up, v_w_mlp_down):
    args = dict(locals())
    xi, yi, ci = _place()
    quarter = 2 * xi + yi
    batch = 4 * xi + 2 * yi + ci

    xt, target = x[0], loss_target[0]
    t, d = xt.shape
    hs, hv = ssm_dt_bias.shape[-1], gdn_dt_bias.shape[-1]
    d_inner = hs * SSM_HEAD_DIM
    n_grp = hs // SSM_HEADS_PER_GROUP
    gn = n_grp * SSM_D_STATE
    conv_ssm = d_inner + 2 * gn
    hq = hv // 2
    key, val = hq * GDN_HEAD, hv * GDN_HEAD
    conv_gdn = 2 * key + val
    hidden = 4 * w_mlp_up.shape[-1]
    o_dt = d_inner + conv_ssm
    o_qkv = o_dt + hs
    o_b = o_qkv + conv_gdn + val
    o_a = o_b + hv
    o_gs = o_a + hv
    n_proj = o_gs + 2 * d
    a_z, a_xs, a_bm, a_cm = 0, d_inner, 2 * d_inner, 2 * d_inner + gn
    a_q = o_dt
    a_k, a_v, a_zg = a_q + key, a_q + 2 * key, a_q + conv_gdn
    a_gs = a_zg + val
    a_gg = a_gs + d
    a_small = a_gg + d
    n_al = -(-(a_small + LANES) // MM_TILE_N) * MM_TILE_N

    def to_aligned(w):
        z = lambda n: jnp.zeros((n, w.shape[1]), w.dtype)
        return jnp.concatenate([
            w[:o_dt], w[o_qkv:o_b], w[o_gs:],
            w[o_dt:o_qkv], z(LANE_B - hs), w[o_b:o_a], z(LANE_A - LANE_B - hv), w[o_a:o_gs], z(LANES - LANE_A - hv),
            z(n_al - a_small - LANES)], axis=0)

    def from_aligned(w):
        s = a_small
        return jnp.concatenate([
            w[:o_dt], w[s + LANE_DT:s + LANE_DT + hs], w[a_q:a_gs], w[s + LANE_B:s + LANE_B + hv],
            w[s + LANE_A:s + LANE_A + hv], w[a_gs:a_small]], axis=0)

    def lanes(vec, at):
        return jnp.zeros((1, LANES), F32).at[:, at:at + vec.shape[-1]].set(vec.reshape(1, -1))

    n_cw = CONV_K * ssm_conv_w.shape[-1]
    small_in = gather_flat("ag_small", jnp.concatenate([c.reshape(-1), ssm_conv_w.reshape(-1), gdn_conv_w.reshape(-1)]))
    c_all = small_in[:, :d]
    by_chip = small_in[0::2]

    def whole_conv_w(lo):
        return jnp.transpose(by_chip[:, lo:lo + n_cw].reshape(4, CONV_K, -1), (1, 0, 2)).reshape(CONV_K, -1)

    cw_ssm, cw_gdn = whole_conv_w(d), whole_conv_w(d + n_cw)
    cb_ssm = ssm_conv_b
    cb_gdn = jnp.zeros((1, conv_gdn), F32)

    n_ada = w_ada.shape[-1]
    b_q = lax.dynamic_slice_in_dim(b_ada, quarter * n_ada, n_ada, axis=1)
    mod_q = _whole("ada_fwd", lambda ca, w, b: _bdot(_silu(ca), w) + b, [c_all, w_ada[0], b_q], [((N_DEV, n_ada), F32)])[0]
    mod_all = gather_flat("ag_mod", mod_q.reshape(-1)).reshape(N_DEV, N_DEV, n_ada)[0::2]
    mod = lax.dynamic_index_in_dim(mod_all, batch, axis=1, keepdims=False).reshape(1, 4 * n_ada)
    sh1, sc1, g1, sh2, sc2, g2 = [mod[:, i * d:(i + 1) * d] for i in range(6)]

    transposed = lambda a: jnp.swapaxes(a, 1, 2)
    own = [w.astype(BF16) for w in (transposed(w_in)[0], w_ssm_up[0], w_gdn_up[0], w_out[0], w_mlp_up[0], w_mlp_down[0])]
    with_own = lambda gs, ws: [lax.dynamic_update_index_in_dim(g, w, quarter, 0) for g, w in zip(gs, ws)]
    cols_major = lambda g: jnp.transpose(g, (1, 0, 2)).reshape(g.shape[1], -1)
    rows_major = lambda g: g.reshape(-1, g.shape[2])
    wb_in = to_aligned(rows_major(with_own(run_exchange("ag_w_in", all_gather_shards(own[:1], [1])), own[:1])[0]))

    h1 = rowmap("premix", _premix, [xt], [norm_mix_pre, sc1, sh1], [(d, BF16)])[0]
    proj, gathered = matmul("in_proj", h1, wb_in, tb=True, comm=all_gather_shards(own[1:], [0] * 5))
    gathered = with_own(gathered, own[1:])
    wb_ssm_up, wb_gdn_up, wb_out = rows_major(gathered[0]), rows_major(gathered[1]), rows_major(gathered[2])
    wb_up, wb_down = cols_major(gathered[3]), rows_major(gathered[4])
    xs = conv_fwd("conv_xs", _conv_silu, proj, d_inner, a_xs, cw_ssm, cb_ssm, 0)
    bm = conv_fwd("conv_bm", _conv_silu, proj, gn, a_bm, cw_ssm, cb_ssm, d_inner)
    cm = conv_fwd("conv_cm", _conv_silu, proj, gn, a_cm, cw_ssm, cb_ssm, d_inner + gn)
    q = conv_fwd("conv_q", _conv_silu_l2, proj, key, a_q, cw_gdn, cb_gdn, 0)
    k = conv_fwd("conv_k", _conv_silu_l2, proj, key, a_k, cw_gdn, cb_gdn, key)
    v = conv_fwd("conv_v", _conv_silu, proj, val, a_v, cw_gdn, cb_gdn, 2 * key)

    wide = 2 * LANES
    ssd_rows = [(xs, wide, 0, True), (bm, LANES, 0, True), (cm, LANES, 0, True), (proj, LANES, a_small, False), (proj, wide, a_z, True)]
    ssd_consts = [(lanes(ssm_dt_bias, LANE_DT), False), (lanes(ssm_A_log, LANE_DT), False), (lanes(ssm_D, LANE_DT), False),
                  (ssm_norm_w.reshape(n_grp, 1, wide), True)]
    hb_ssd, hb_gdn = min(SSD_GROUPS_PER_STEP, n_grp), min(GDN_HEADS_PER_STEP, hq)
    y_ssm_n, st_ssm = scan_fwd("ssd_fwd", ssd_step, SSM_CHUNK, n_grp, ssd_rows, ssd_consts, wide, hb_ssd)
    gdn_rows = [(q, LANES, 0, True), (k, LANES, 0, True), (v, wide, 0, True), (proj, wide, a_zg, True), (proj, LANES, a_small, False)]
    gdn_consts = [(lanes(gdn_dt_bias, LANE_A), False), (lanes(gdn_A_log, LANE_A), False), (gdn_norm_w, False)]
    gdn_rows_per_step = GDN_CHUNK * GDN_CHUNKS_PER_STEP
    y_gdn_n, st_gdn = scan_fwd("gdn_fwd", gdn_step, gdn_rows_per_step, hq, gdn_rows, gdn_consts, wide, hb_gdn)

    y_ssm = matmul("ssm_up", y_ssm_n, wb_ssm_up)
    y_gdn = matmul("gdn_up", y_gdn_n, wb_gdn_up)
    gates = [(proj, d, a_gs), (proj, d, a_gg)]
    merged = rowmap("merge", _merge, gates + [y_ssm, y_gdn], [], [(d, BF16)])[0]
    u = matmul("w_out", merged, wb_out)
    post_consts = [norm_mix_post, g1, norm_mlp_pre, sc2, sh2]
    x1, h2 = rowmap("postmix", _postmix, [xt, u], post_consts, [(d, F32), (d, BF16)])
    relu2 = lambda acc: (acc, jnp.square(jnp.maximum(acc, 0.0)))
    a_up, act = matmul("mlp_up", h2, wb_up, out_dtypes=(BF16, BF16), epi=relu2)
    y2 = matmul("mlp_down", act, wb_down)

    def final_bwd(x1_, y2_, tgt, w_, g_):
        x2, vjp = jax.vjp(_final, x1_, y2_, w_, g_)
        err = x2 - tgt
        loss = 0.5 * jnp.sum(jnp.mean(err * err, axis=-1, keepdims=True), axis=0, keepdims=True)
        dx1, dy2, dw, dg = vjp(err / d)
        return dx1, dy2, loss, dw, dg

    dx1, dy2, loss_part, d_norm_mlp_post, dg2 = rowmap(
        "final", final_bwd, [x1, y2, target], [norm_mlp_post, g2], [(d, F32), (d, BF16)], [((1, 1), F32), ((1, d), F32), ((1, d), F32)])
    loss = lax.psum(loss_part[0, 0], ("x", "y", "c"))

    d_a = matmul("mlp_down_dx", dy2, wb_down, tb=True, out_dtypes=(BF16,), extras=[a_up],
                 epi=lambda acc, a: acc * 2.0 * jnp.maximum(a.astype(F32), 0.0))
    gw_down = matmul("mlp_down_dw", act, dy2, ta=True)
    dh2 = matmul("mlp_up_dx", d_a, wb_up, tb=True)
    gw_up = matmul("mlp_up_dw", h2, d_a, ta=True)

    def postmix_bwd(x_, u_, dx1_, dh2_, *cs):
        _, vjp = jax.vjp(_postmix, x_, u_, *cs)
        return vjp((dx1_, dh2_))

    dxa, du, d_norm_mix_post, dg1, d_norm_mlp_pre, dsc2, dsh2 = rowmap(
        "postmix_bwd", postmix_bwd, [xt, u, dx1, dh2], post_consts, [(d, F32), (d, BF16)], [((1, d), F32)] * 5)
    d_merged = matmul("w_out_dx", du, wb_out, tb=True)
    gw_out = matmul("w_out_dw", merged, du, ta=True)

    def merge_bwd(gs, gg, ys, yg, dm):
        _, vjp = jax.vjp(_merge, gs, gg, ys, yg)
        dgs, dgg, dys, dyg = vjp(dm)
        return dys, dyg, jnp.concatenate([dgs, dgg], axis=1)

    dy_ssm, dy_gdn, dproj = rowmap("merge_bwd", merge_bwd, gates + [y_ssm, y_gdn, d_merged], [],
                                   [(d, BF16), (d, BF16), (2 * d, BF16, jax.ShapeDtypeStruct((t, n_al), BF16), a_gs)])
    dy_ssm_n = matmul("ssm_up_dx", dy_ssm, wb_ssm_up, tb=True, out_dtypes=(BF16,))
    gw_ssm_up = matmul("ssm_up_dw", y_ssm_n, dy_ssm, ta=True)
    dy_gdn_n = matmul("gdn_up_dx", dy_gdn, wb_gdn_up, tb=True, out_dtypes=(BF16,))
    gw_gdn_up = matmul("gdn_up_dw", y_gdn_n, dy_gdn, ta=True)

    dxs, dbm, dcm, dsmall_ssm, dproj, d_sdtb, d_salog, d_sdsk, d_snw = scan_bwd(
        "ssd_bwd", ssd_step, SSM_CHUNK, n_grp, ssd_rows, ssd_consts, st_ssm, dy_ssm_n, [BF16, BF16, BF16, F32, BF16], hb_ssd,
        {4: dproj})
    dq, dk, dv, dproj, dsmall_gdn, d_gdtb, d_galog, d_gnw = scan_bwd(
        "gdn_bwd", gdn_step, gdn_rows_per_step, hq, gdn_rows, gdn_consts, st_gdn, dy_gdn_n, [BF16, BF16, BF16, BF16, F32], hb_gdn,
        {3: dproj})
    dproj, dcw_xs, dcb_xs = conv_bwd("conv_xs_bwd", _conv_silu, proj, d_inner, a_xs, cw_ssm, cb_ssm, 0, dxs, dproj)
    dproj, dcw_bm, dcb_bm = conv_bwd("conv_bm_bwd", _conv_silu, proj, gn, a_bm, cw_ssm, cb_ssm, d_inner, dbm, dproj)
    dproj, dcw_cm, dcb_cm = conv_bwd("conv_cm_bwd", _conv_silu, proj, gn, a_cm, cw_ssm, cb_ssm, d_inner + gn, dcm, dproj)
    dproj, dcw_q, _ = conv_bwd("conv_q_bwd", _conv_silu_l2, proj, key, a_q, cw_gdn, cb_gdn, 0, dq, dproj)
    dproj, dcw_k, _ = conv_bwd("conv_k_bwd", _conv_silu_l2, proj, key, a_k, cw_gdn, cb_gdn, key, dk, dproj)
    dproj, dcw_v, _ = conv_bwd("conv_v_bwd", _conv_silu, proj, val, a_v, cw_gdn, cb_gdn, 2 * key, dv, dproj)
    tail = n_al - a_small
    dproj = rowmap("small_sum", lambda a, b: jnp.concatenate([a + b, jnp.zeros((a.shape[0], tail - LANES), F32)], axis=1),
                   [dsmall_ssm, dsmall_gdn], [], [(tail, BF16, dproj, a_small)])[0]
    quarters_cols = lambda g: jnp.transpose(g.reshape(g.shape[0], 4, -1), (1, 0, 2))
    quarters_rows = lambda g: g.reshape(4, g.shape[0] // 4, g.shape[1])
    rest32, rest16 = reduce_on_chip("rest", [quarters_rows(gw_ssm_up), quarters_rows(gw_gdn_up), quarters_rows(gw_out),
                                             quarters_cols(gw_up), quarters_rows(gw_down)], [0] * 5)
    gw_in_al, rest_chips = matmul("in_proj_dw", dproj, h1, ta=True, comm=exchange_quarters_ici(rest16))
    in32, in16 = reduce_on_chip("in", [quarters_rows(from_aligned(gw_in_al))], [1])
    dh1, in_chips = matmul("in_proj_dx", dproj, wb_in, comm=exchange_quarters_ici(in16))

    def premix_bwd(x_, dxa_, dh1_, w_, sc_, sh_):
        _, vjp = jax.vjp(_premix, x_, w_, sc_, sh_)
        dx, dw, dsc, dsh = vjp(dh1_)
        return dx + dxa_, dw, dsc, dsh

    grad_x, d_norm_mix_pre, dsc1, dsh1 = rowmap(
        "premix_bwd", premix_bwd, [xt, dxa, dh1], [norm_mix_pre, sc1, sh1], [(d, F32)], [((1, d), F32)] * 3)

    dmod_all = gather_flat("ag_dmod", jnp.concatenate([dsh1, dsc1, dg1, dsh2, dsc2, dg2], axis=1).reshape(-1))
    dmod_q = lax.dynamic_slice_in_dim(dmod_all, quarter * n_ada, n_ada, axis=1)
    gw_ada, gb_ada = _whole(
        "ada_bwd", lambda ca, dq_, da_: (_bdot(_silu(ca), dq_, TN), jnp.sum(da_, axis=0, keepdims=True)),
        [c_all, dmod_q, dmod_all], [((d, n_ada), F32), ((1, 4 * n_ada), F32)])

    dcw_ssm = jnp.concatenate([dcw_xs, dcw_bm, dcw_cm], axis=1)
    dcb_ssm = jnp.concatenate([dcb_xs, dcb_bm, dcb_cm], axis=1)
    dcw_gdn = jnp.concatenate([dcw_q, dcw_k, dcw_v], axis=1)
    partial = [d_norm_mix_pre, d_norm_mix_post, dcw_ssm, dcb_ssm, d_sdtb[:, LANE_DT:LANE_DT + hs], d_salog[:, LANE_DT:LANE_DT + hs],
               d_sdsk[:, LANE_DT:LANE_DT + hs], d_snw, dcw_gdn, d_gdtb[:, LANE_A:LANE_A + hv], d_galog[:, LANE_A:LANE_A + hv], d_gnw,
               d_norm_mlp_pre, d_norm_mlp_post]
    sizes = [p.size for p in partial]
    stacked = gather_flat("ag_grads", jnp.concatenate([p.reshape(-1) for p in partial]))
    summed = _whole("small_sum8", lambda s: jnp.sum(s, axis=0, keepdims=True), [stacked], [((1, stacked.shape[1]), F32)])[0][0]
    offs = [0]
    for s in sizes:
        offs.append(offs[-1] + s)
    red = [summed[offs[i]:offs[i + 1]] for i in range(len(sizes))]
    my_cols = lambda full: lax.dynamic_slice_in_dim(full.reshape(CONV_K, -1), quarter * (n_cw // CONV_K), n_cw // CONV_K, axis=1)
    small_grads = {
        "b_ada": gb_ada, "norm_mix_pre": red[0], "norm_mix_post": red[1], "ssm_conv_w": my_cols(red[2]), "ssm_conv_b": red[3],
        "ssm_dt_bias": red[4], "ssm_A_log": red[5], "ssm_D": red[6], "ssm_norm_w": red[7], "gdn_conv_w": my_cols(red[8]),
        "gdn_dt_bias": red[9], "gdn_A_log": red[10], "gdn_norm_w": red[11], "norm_mlp_pre": red[12], "norm_mlp_post": red[13]}

    big_names = ["w_in", "w_ssm_up", "w_gdn_up", "w_out", "w_mlp_up", "w_mlp_down"]
    big_grads = dict(zip(big_names, reduce_across_chips(in32 + rest32, in_chips + rest_chips)))

    names = ['w_ada', 'b_ada', 'norm_mix_pre', 'norm_mix_post', 'w_in', 'ssm_conv_w', 'ssm_conv_b', 'ssm_dt_bias', 'ssm_A_log', 'ssm_D',
             'ssm_norm_w', 'gdn_conv_w', 'gdn_dt_bias', 'gdn_A_log', 'gdn_norm_w', 'w_ssm_up', 'w_gdn_up', 'w_out', 'norm_mlp_pre',
             'norm_mlp_post', 'w_mlp_up', 'w_mlp_down']
    grad, delta, new_m, new_v = {}, {}, {}, {}
    for n, (mine, other) in big_grads.items():
        view, axis = (transposed, 1) if n == "w_in" else ((lambda a: a), 0)
        res = adamw_halves("adamw_" + n, view(args[n]), mine, other, view(args["m_" + n]), view(args["v_" + n]), axis)
        grad[n], delta[n], new_m[n], new_v[n] = [view(a) for a in res]
    grad["w_ada"] = gw_ada.reshape(w_ada.shape)
    delta["w_ada"], new_m["w_ada"], new_v["w_ada"] = adamw("adamw_w_ada", w_ada, gw_ada, m_w_ada, v_w_ada)
    small_names = [n for n in names if n not in grad]
    flat = lambda pre: jnp.concatenate([args[pre + n].reshape(-1) for n in small_names]).reshape(1, 1, -1)
    g_flat = jnp.concatenate([small_grads[n].reshape(-1) for n in small_names]).reshape(1, -1)
    dl, nm, nv = adamw("adamw_small", flat(""), g_flat, flat("m_"), flat("v_"))
    off = 0
    for n in small_names:
        shape = args[n].shape
        size = args[n].size
        grad[n], delta[n], new_m[n], new_v[n] = [a.reshape(-1)[off:off + size].reshape(shape) for a in (g_flat, dl, nm, nv)]
        off += size

    return (loss, grad_x.reshape(x.shape), *[grad[n] for n in names], *[delta[n] for n in names],
            *[new_m[n] for n in names], *[new_v[n] for n in names])
```

```python
import functools

import jax
import jax.numpy as jnp
from jax import lax
from jax.experimental import pallas as pl
from jax.experimental.pallas import tpu as pltpu

F32 = jnp.float32
BF16 = jnp.bfloat16
MESH = pl.DeviceIdType.MESH

EPS = 1e-6
SSM_HEAD_DIM = 64
SSM_HEADS_PER_GROUP = 4
SSM_D_STATE = 128
SSM_CHUNK = 128
GDN_HEAD = 128
GDN_CHUNK = 64
CONV_K = 4
LANE_DT, LANE_B, LANE_A = 0, 32, 48
ADAM_LR, ADAM_B1, ADAM_B2, ADAM_EPS, ADAM_WD, ADAM_STEP = 0.001, 0.9, 0.999, 1e-08, 0.01, 10

VMEM_LIMIT_BYTES = 56 * 1024 * 1024
LANES = 128
N_DEV = 8

NN = (((1,), (0,)), ((), ()))
NT = (((1,), (1,)), ((), ()))
TN = (((0,), (0,)), ((), ()))


BNN = (((2,), (1,)), ((0,), (0,)))
BNT = (((2,), (2,)), ((0,), (0,)))
BTN = (((1,), (1,)), ((0,), (0,)))
_KIND = {NN: ("NN", 0), NT: ("NT", 0), TN: ("TN", 0), BNN: ("NN", 1), BNT: ("NT", 1), BTN: ("TN", 1)}
_DIMS = {"NN": (NN, BNN), "NT": (NT, BNT), "TN": (TN, BTN)}


def _dg(a, b, dims):
    return lax.dot_general(a, b, dims, preferred_element_type=F32)


def _raw_bf16(a, b, dims):
    return _dg(a.astype(BF16), b.astype(BF16), dims)


def _raw_bf16x3(a, b, dims):
    ah, bh = a.astype(BF16), b.astype(BF16)
    al, bl = (a - ah.astype(F32)).astype(BF16), (b - bh.astype(F32)).astype(BF16)
    return _dg(ah, bh, dims) + (_dg(ah, bl, dims) + _dg(al, bh, dims))


def _make_dot(raw):
    @functools.partial(jax.custom_vjp, nondiff_argnums=(2,))
    def dot(a, b, dims):
        return raw(a, b, dims)

    def fwd(a, b, dims):
        return raw(a, b, dims), (a, b)

    def bwd(dims, res, ct):
        a, b = res
        kind, batched = _KIND[dims]
        d = lambda k: _DIMS[k][batched]
        if kind == "NN":
            da, db = raw(ct, b, d("NT")), raw(a, ct, d("TN"))
        elif kind == "NT":
            da, db = raw(ct, b, d("NN")), raw(ct, a, d("TN"))
        else:
            da, db = raw(b, ct, d("NT")), raw(a, ct, d("NN"))
        return da.astype(a.dtype), db.astype(b.dtype)

    dot.defvjp(fwd, bwd)
    return lambda a, b, dims=NN: dot(a, b, dims)


_bdot = _make_dot(_raw_bf16)
_hdot = _make_dot(_raw_bf16x3)


def _mask_dot(mask, x, dims, mask_first=True):
    m = mask.astype(BF16)
    hi = x.astype(BF16)
    r = x - hi.astype(F32)
    mid = r.astype(BF16)
    lo = (r - mid.astype(F32)).astype(BF16)
    return sum(_dg(m, p, dims) if mask_first else _dg(p, m, dims) for p in (hi, mid, lo))


def _sigmoid(x):
    return 1.0 / (1.0 + jnp.exp(-x))


def _silu(x):
    return x * _sigmoid(x)


def _softplus(x):
    return jnp.maximum(x, 0.0) + jnp.log(1.0 + jnp.exp(-jnp.abs(x)))


def _rms(x, w):
    return x * lax.rsqrt(jnp.mean(x * x, axis=-1, keepdims=True) + EPS) * w


def _lane_col(m, idx):
    lane = lax.broadcasted_iota(jnp.int32, m.shape, 1)
    return jnp.sum(jnp.where(lane == idx, m, 0.0), axis=1, keepdims=True)


def _tril(n, strict=False, seg=None):
    r = lax.broadcasted_iota(jnp.int32, (n, n), 0)
    c = lax.broadcasted_iota(jnp.int32, (n, n), 1)
    low = (r > c) if strict else (r >= c)
    if seg is None or seg >= n:
        return low
    shift = seg.bit_length() - 1
    return jnp.logical_and(low, (r >> shift) == (c >> shift))


def _first_lane(shape):
    return lax.broadcasted_iota(jnp.int32, shape, len(shape) - 1) == 0


@jax.custom_vjp
def _row_form(col):
    shape = col.shape[:-1] + (LANES,)
    return _mask_dot(_first_lane(shape), jnp.broadcast_to(col, shape), NT if col.ndim == 2 else BNT)


def _row_form_fwd(col):
    return _row_form(col), None


def _row_form_bwd(_, ct):
    shape = ct.shape[:-1] + (LANES,)
    sums = _mask_dot(_first_lane(shape), ct, TN if ct.ndim == 2 else BTN, mask_first=False)
    return (jnp.sum(sums, axis=-1, keepdims=True),)


_row_form.defvjp(_row_form_fwd, _row_form_bwd)


@functools.partial(jax.custom_vjp, nondiff_argnums=(1,))
def _cumsum_rows(x, seg):
    return _mask_dot(_tril(x.shape[0], seg=seg), x, NN)


def _cumsum_rows_fwd(x, seg):
    return _cumsum_rows(x, seg), None


def _cumsum_rows_bwd(seg, _, ct):
    return (_mask_dot(_tril(ct.shape[0], seg=seg), ct, TN),)


_cumsum_rows.defvjp(_cumsum_rows_fwd, _cumsum_rows_bwd)


def _head_rows(m_t, idx):
    sub = lax.broadcasted_iota(jnp.int32, m_t.shape, 0)
    return jnp.sum(jnp.where(sub == idx, m_t, 0.0), axis=0, keepdims=True)


def _params(sem):
    return pltpu.CompilerParams(dimension_semantics=sem, vmem_limit_bytes=VMEM_LIMIT_BYTES)


def _into_plumbing(outs, first_input):
    arrays, aliases = [], {}
    for k, o in enumerate(outs):
        if len(o) > 4 and not isinstance(o[4], jax.ShapeDtypeStruct):
            aliases[first_input + len(arrays)] = k
            arrays.append(o[4])
    return arrays, aliases


def blockmap(name, fn, grid, ins, outs, accs=(), scalars=None):
    n_in, n_out, n_acc = len(ins), len(outs), len(accs)
    n_grid = len(grid)
    n_pre = 0 if scalars is None else 1
    into_arrays, aliases = _into_plumbing(outs, n_pre + n_in)
    n_into = len(into_arrays)

    def body(*refs):
        refs = refs[n_pre:n_pre + n_in] + refs[n_pre + n_in + n_into:]
        vals = fn(*[r[...] for r in refs[:n_in]])
        if not isinstance(vals, (tuple, list)):
            vals = (vals,)
        for r, v in zip(refs[n_in:n_in + n_out], vals[:n_out]):
            r[...] = v.astype(r.dtype)
        if n_acc:
            first = functools.reduce(jnp.logical_and, [pl.program_id(a) == 0 for a in range(n_grid)])
            acc_refs = refs[n_in + n_out:]

            @pl.when(first)
            def _():
                for r in acc_refs:
                    r[...] = jnp.zeros(r.shape, r.dtype)

            for r, v in zip(acc_refs, vals[n_out:]):
                r[...] += v.astype(r.dtype)

    zeros = lambda nd: (lambda *_: (0,) * nd)
    in_specs = [pl.BlockSpec(b, im) for _, b, im in ins] + [pl.BlockSpec(memory_space=pl.ANY)] * n_into
    out_specs = [pl.BlockSpec(o[2], o[3]) for o in outs] + [pl.BlockSpec(s, zeros(len(s))) for s, _ in accs]
    out_shape = [jax.ShapeDtypeStruct(o[0], o[1]) for o in outs] + [jax.ShapeDtypeStruct(s, d) for s, d in accs]
    cparams = _params(("arbitrary",) * n_grid if n_acc else ("parallel",) * n_grid)
    arrays = [a for a, _, _ in ins] + into_arrays
    if scalars is None:
        return pl.pallas_call(body, name=name, grid=grid, in_specs=in_specs, out_specs=out_specs, out_shape=out_shape,
                              input_output_aliases=aliases, compiler_params=cparams)(*arrays)
    spec = pltpu.PrefetchScalarGridSpec(num_scalar_prefetch=1, grid=grid, in_specs=in_specs, out_specs=out_specs)
    return pl.pallas_call(body, name=name, grid_spec=spec, out_shape=out_shape, input_output_aliases=aliases,
                          compiler_params=cparams)(scalars, *arrays)


def rowmap(name, fn, rows, consts, outs, accs=(), rb=256):
    norm = [(r, r.shape[1], 0) if not isinstance(r, tuple) else (r[0], r[1], r[2] // r[1]) for r in rows]
    assert all(not isinstance(r, tuple) or r[2] % r[1] == 0 for r in rows)
    t = norm[0][0].shape[0]
    rb = min(rb, t)
    ins = [(a, (rb, n), (lambda i, cb=cb: (i, cb))) for a, n, cb in norm]
    ins += [(cst, cst.shape, (lambda i, nd=cst.ndim: (0,) * nd)) for cst in consts]
    o = []
    for out in outs:
        if len(out) == 2:
            o.append(((t, out[0]), out[1], (rb, out[0]), lambda i: (i, 0)))
        else:
            n, d, into, off = out
            assert off % n == 0 and into.dtype == d
            o.append((into.shape, d, (rb, n), (lambda i, cb=off // n: (i, cb)), into))
    return blockmap(name, fn, (t // rb,), ins, o, accs)


MM_TILE_M, MM_TILE_N, MM_TILE_K = 1024, 1024, 2048


def _tile(dim, cap):
    if dim <= cap:
        return dim
    best = max(t for t in range(LANES, cap + 1, LANES) if dim % t == 0)
    return best


def matmul(name, a, b, ta=False, tb=False, out_dtypes=(F32,), epi=None, extras=(), comm=None):
    (k_dim, m_dim) = a.shape if ta else a.shape[::-1]
    n_dim = b.shape[0] if tb else b.shape[1]
    assert (b.shape[1] if tb else b.shape[0]) == k_dim, (name, a.shape, b.shape)
    tm, tn, tk = _tile(m_dim, MM_TILE_M), _tile(n_dim, MM_TILE_N), _tile(k_dim, MM_TILE_K)
    grid = (m_dim // tm, n_dim // tn, k_dim // tk)
    k_steps = grid[2]
    n_extra, n_out = len(extras), len(out_dtypes)
    n_cin, n_cout = (len(comm.operands), len(comm.out_shapes)) if comm else (0, 0)
    dims = (((0 if ta else 1,), (1 if tb else 0,)), ((), ()))

    def body(*refs):
        ins, outs, scratch = refs[:2 + n_extra + n_cin], refs[2 + n_extra + n_cin:][:n_out + n_cout], refs[2 + n_extra + n_cin + n_out + n_cout:]
        extra_refs, out_refs = ins[2:2 + n_extra], outs[:n_out]
        ids = [pl.program_id(ax) for ax in range(3)]
        if comm:
            comm_refs = (ins[2 + n_extra:], outs[n_out:], scratch[-2], scratch[-1])

            @pl.when(functools.reduce(jnp.logical_and, [i == 0 for i in ids]))
            def _():
                comm.start(*comm_refs)

        def finish(acc):
            vals = (acc,) if epi is None else epi(acc, *[r[...] for r in extra_refs])
            if not isinstance(vals, (tuple, list)):
                vals = (vals,)
            for r, v in zip(out_refs, vals):
                r[...] = v.astype(r.dtype)

        prod = lax.dot_general(ins[0][...].astype(BF16), ins[1][...].astype(BF16), dims, preferred_element_type=F32)
        if k_steps == 1:
            finish(prod)
        else:
            acc_ref = scratch[0]

            @pl.when(ids[2] == 0)
            def _():
                acc_ref[...] = jnp.zeros(acc_ref.shape, F32)

            acc_ref[...] += prod

            @pl.when(ids[2] == k_steps - 1)
            def _():
                finish(acc_ref[...])

        if comm:
            @pl.when(functools.reduce(jnp.logical_and, [i == g - 1 for i, g in zip(ids, grid)]))
            def _():
                comm.finish(*comm_refs)

    a_spec = pl.BlockSpec((tk, tm), lambda i, j, k: (k, i)) if ta else pl.BlockSpec((tm, tk), lambda i, j, k: (i, k))
    b_spec = pl.BlockSpec((tn, tk), lambda i, j, k: (j, k)) if tb else pl.BlockSpec((tk, tn), lambda i, j, k: (k, j))
    mn_spec = pl.BlockSpec((tm, tn), lambda i, j, k: (i, j))
    scratch_shapes = [] if k_steps == 1 else [pltpu.VMEM((tm, tn), F32)]
    if comm:
        scratch_shapes += [pltpu.SemaphoreType.DMA(comm.sem_shape), pltpu.SemaphoreType.DMA(comm.sem_shape)]
    res = pl.pallas_call(
        body, name=name, grid=grid,
        in_specs=[a_spec, b_spec] + [mn_spec] * n_extra + [ANY] * n_cin,
        out_specs=[mn_spec] * n_out + [ANY] * n_cout,
        out_shape=[jax.ShapeDtypeStruct((m_dim, n_dim), d) for d in out_dtypes] + (comm.out_shapes if comm else []),
        scratch_shapes=scratch_shapes,
        compiler_params=_params(("arbitrary",) * 3 if comm else ("parallel", "parallel", "arbitrary")),
    )(a, b, *extras, *(comm.operands if comm else []))
    main = res[:n_out] if n_out > 1 else res[0]
    return (main, list(res[n_out:])) if comm else main


def ssd_step(g0, state, xs, bm, cm, small, z, p_dtb, p_alog, p_dsk, nw):
    hb, n = state.shape[0], xs.shape[0]
    n_pair, n_head = 2 * hb, 4 * hb
    causal = _tril(n)
    dt_all = _softplus(small + p_dtb)
    a_all = dt_all * (-jnp.exp(p_alog))
    acum_all = _cumsum_rows(a_all, n)
    acum_t = acum_all.T
    lane0 = LANE_DT + SSM_HEADS_PER_GROUP * g0
    sub = lax.broadcasted_iota(jnp.int32, acum_t.shape, 0)
    heads = range(n_head)
    acum = jnp.stack([_lane_col(acum_all, lane0 + i) for i in heads])
    acum_row = jnp.stack([jnp.sum(jnp.where(sub == lane0 + i, acum_t, 0.0), axis=0, keepdims=True) for i in heads])
    dt = jnp.stack([_lane_col(dt_all, lane0 + i) for i in heads])
    dsk = jnp.stack([_lane_col(p_dsk, lane0 + i) for i in heads])
    decay = jnp.exp(jnp.where(causal, acum - acum_row, -jnp.inf))
    a_last = acum[:, n - 1:n, :]

    def split(a):
        return [a[:, i * LANES:(i + 1) * LANES] for i in range(a.shape[1] // LANES)]

    def pairs(a, axis=2):
        even = jnp.stack([a[2 * p] for p in range(n_pair)])
        odd = jnp.stack([a[2 * p + 1] for p in range(n_pair)])
        shape = (n_pair, LANES, LANES) if axis == 1 else (n_pair, a.shape[1], LANES)
        return jnp.where(lax.broadcasted_iota(jnp.int32, shape, axis) < SSM_HEAD_DIM, even, odd)

    bms, cms = split(bm), split(cm)
    cb = _bdot(jnp.stack(cms), jnp.stack(bms), BNT)
    cbd = jnp.stack([cb[i // SSM_HEADS_PER_GROUP] for i in heads]) * decay
    xp = jnp.stack(split(xs))
    xdt = xp * pairs(dt)
    yd = _bdot(cbd, jnp.stack([xdt[i // 2] for i in heads]), BNN)
    lane = lax.broadcasted_iota(jnp.int32, (n_pair, n, LANES), 2)
    y_diag = jnp.where(lane < SSM_HEAD_DIM, jnp.stack([yd[2 * p] for p in range(n_pair)]), jnp.stack([yd[2 * p + 1] for p in range(n_pair)]))
    st = state.reshape(n_pair, LANES, LANES)
    cm2 = jnp.stack([cms[p // 2] for p in range(n_pair)])
    bm2 = jnp.stack([bms[p // 2] for p in range(n_pair)])
    y_off = _bdot(cm2, st, BNT) * pairs(jnp.exp(acum))
    new = st * pairs(jnp.exp(a_last), axis=1) + _bdot(xdt * pairs(jnp.exp(a_last - acum)), bm2, BTN)
    y = y_diag + y_off + pairs(dsk) * xp
    y = jnp.concatenate([y[p] for p in range(n_pair)], axis=1) * _silu(z)
    wide = 2 * LANES
    y = jnp.concatenate([_rms(y[:, i * wide:(i + 1) * wide], nw[i]) for i in range(hb)], axis=1)
    return new.reshape(state.shape), y


@functools.partial(jax.custom_vjp, nondiff_argnums=(1,))
def _unit_lower_inverse(a, seg):
    n = a.shape[-1]
    r = lax.broadcasted_iota(jnp.int32, (n, n), 0)
    c = lax.broadcasted_iota(jnp.int32, (n, n), 1)
    shift = min(INVERSE_BASE, seg).bit_length() - 1
    power = jnp.where((r >> shift) == (c >> shift), a, 0.0)
    inv = (r == c).astype(F32) - power
    span = 2
    while span < (1 << shift):
        power = _hdot(power, power, BNN)
        inv = inv + _hdot(inv, power, BNN)
        span *= 2
    while (1 << shift) < seg:
        below = jnp.logical_and((r >> (shift + 1)) == (c >> (shift + 1)), (r >> shift) != (c >> shift))
        inv = inv - _hdot(inv, _hdot(jnp.where(below, a, 0.0), inv, BNN), BNN)
        shift += 1
    return inv


def _unit_lower_inverse_fwd(a, seg):
    inv = _unit_lower_inverse(a, seg)
    return inv, inv


def _unit_lower_inverse_bwd(seg, inv, ct):
    return (-_hdot(_hdot(inv, ct, BTN), inv, BNT),)


_unit_lower_inverse.defvjp(_unit_lower_inverse_fwd, _unit_lower_inverse_bwd)


@jax.custom_vjp
def _known_inverse(a, inv):
    return inv


def _known_inverse_fwd(a, inv):
    return inv, inv


def _known_inverse_bwd(inv, ct):
    return _unit_lower_inverse_bwd(None, inv, ct)[0], jnp.zeros_like(inv)


_known_inverse.defvjp(_known_inverse_fwd, _known_inverse_bwd)


def gdn_step(hq0, state, q, k, v, z, small, p_dtb, p_alog, nw, keep=False, kept=None):
    n, chunk = q.shape[0], GDN_CHUNK
    hb = state.shape[0]
    nb = 2 * hb
    cur = state.reshape(nb, LANES, LANES)
    causal, strict = _tril(n, seg=chunk), _tril(n, True, seg=chunk)
    beta_all = _sigmoid(small)
    g_all = -jnp.exp(p_alog) * _softplus(small + p_dtb)
    gcum_all = _cumsum_rows(g_all, chunk)
    gcum_t = gcum_all.T
    split = lambda a: [a[:, i * LANES:(i + 1) * LANES] for i in range(a.shape[1] // LANES)]
    per_value_head = lambda a: jnp.stack([a[i // 2] for i in range(nb)])
    qh, kh = jnp.stack(split(q)) * (GDN_HEAD ** -0.5), jnp.stack(split(k))
    q2, k2 = per_value_head(qh), per_value_head(kh)
    v2, z2 = jnp.stack(split(v)), jnp.stack(split(z))
    gcum = jnp.stack([_lane_col(gcum_all, LANE_A + 2 * hq0 + i) for i in range(nb)])
    gcum_row = jnp.stack([_head_rows(gcum_t, LANE_A + 2 * hq0 + i) for i in range(nb)])
    beta = jnp.stack([_lane_col(beta_all, LANE_B + 2 * hq0 + i) for i in range(nb)])
    dmat = jnp.exp(jnp.where(causal, gcum - gcum_row, -jnp.inf))
    a_low = jnp.where(strict, beta * per_value_head(_bdot(kh, kh, BNT)) * dmat, 0.0)
    inv = _unit_lower_inverse(a_low, chunk) if kept is None else _known_inverse(a_low, kept)
    egc = jnp.exp(gcum)
    u = _hdot(inv, v2 * beta, BNN)
    w = _hdot(inv, k2 * (beta * egc), BNN)
    q_dec = q2 * egc
    v_new, o_state = [], []
    for s in range(n // chunk):
        rows = slice(s * chunk, (s + 1) * chunk)
        v_new.append(u[:, rows] - _bdot(w[:, rows], cur, BNN))
        o_state.append(_bdot(q_dec[:, rows], cur, BNN))
        g_last = gcum[:, (s + 1) * chunk - 1:(s + 1) * chunk, :]
        k_dec = k2[:, rows] * jnp.exp(g_last - gcum[:, rows])
        cur = cur * jnp.exp(g_last) + _bdot(k_dec, v_new[-1], BTN)
    o = jnp.concatenate(o_state, axis=1) + _bdot(per_value_head(_bdot(qh, kh, BNT)) * dmat, jnp.concatenate(v_new, axis=1), BNN)
    out = _rms(o, nw) * _silu(z2)
    res = (cur.reshape(state.shape), jnp.concatenate([out[i] for i in range(nb)], axis=1))
    return res + (inv,) if keep else res


STATE_SHAPE = (2, LANES, LANES)
SSD_GROUPS_PER_STEP = 4
GDN_HEADS_PER_STEP = 8
GDN_CHUNKS_PER_STEP = 2
INVERSE_BASE = 16


def _scan_specs(rows, consts, chunk, chunk_of, hb):
    specs = []
    for _, n, off, per_group in rows:
        if per_group:
            assert off % (n * hb) == 0
            specs.append(pl.BlockSpec((chunk, n * hb), lambda c, g, cb=off // (n * hb): (chunk_of(c), cb + g)))
        else:
            assert off % n == 0
            specs.append(pl.BlockSpec((chunk, n), lambda c, g, cb=off // n: (chunk_of(c), cb)))
    for arr, per_group in consts:
        if per_group:
            specs.append(pl.BlockSpec((hb, 1, arr.shape[2]), lambda c, g: (g, 0, 0)))
        else:
            specs.append(pl.BlockSpec(arr.shape, lambda c, g, nd=arr.ndim: (0,) * nd))
    return specs


def scan_fwd(name, step, chunk, n_grp, rows, consts, out_cols, hb, keep=None):
    t = rows[0][0].shape[0]
    nc = t // chunk
    n_rows, n_consts = len(rows), len(consts)

    def body(*refs):
        row_refs, const_refs = refs[:n_rows], refs[n_rows:n_rows + n_consts]
        y_ref, st_ref = refs[n_rows + n_consts:n_rows + n_consts + 2]
        state = refs[-1]
        c, g = pl.program_id(0), pl.program_id(1)

        @pl.when(c == 0)
        def _():
            state[g] = jnp.zeros((hb,) + STATE_SHAPE, F32)

        st = state[g]
        st_ref[...] = st
        vals = [r[...] for r in row_refs] + [r[...] for r in const_refs]
        if keep is None:
            new, y = step(g * hb, st, *vals)
        else:
            new, y, refs[-2][...] = step(g * hb, st, *vals, keep=True)
        state[g] = new
        y_ref[...] = y.astype(y_ref.dtype)

    out_specs = [pl.BlockSpec((chunk, out_cols * hb), lambda c, g: (c, g)),
                 pl.BlockSpec((None, None, hb) + STATE_SHAPE, lambda c, g: (c, g, 0, 0, 0, 0))]
    out_shape = [jax.ShapeDtypeStruct((t, n_grp * out_cols), BF16), jax.ShapeDtypeStruct((nc, n_grp // hb, hb) + STATE_SHAPE, F32)]
    if keep is not None:
        out_specs.append(pl.BlockSpec((None, None) + keep, lambda c, g: (c, g) + (0,) * len(keep)))
        out_shape.append(jax.ShapeDtypeStruct((nc, n_grp // hb) + keep, F32))
    return pl.pallas_call(
        body, name=name, grid=(nc, n_grp // hb),
        in_specs=_scan_specs(rows, consts, chunk, lambda c: c, hb),
        out_specs=out_specs, out_shape=out_shape,
        scratch_shapes=[pltpu.VMEM((n_grp // hb, hb) + STATE_SHAPE, F32)],
        compiler_params=_params(("arbitrary", "arbitrary")),
    )(*[r[0] for r in rows], *[c[0] for c in consts])


def scan_bwd(name, step, chunk, n_grp, rows, consts, states, dy, row_dtypes, hb, into, kept=None):
    t = rows[0][0].shape[0]
    nc = t // chunk
    n_rows, n_consts = len(rows), len(consts)
    out_cols = dy.shape[1] // n_grp
    n_alias = sum(not isinstance(v, jax.ShapeDtypeStruct) for v in into.values())
    n_kept = 0 if kept is None else 1

    def body(*refs):
        row_refs, const_refs = refs[:n_rows], refs[n_rows:n_rows + n_consts]
        st_ref, dy_ref = refs[n_rows + n_consts:n_rows + n_consts + 2]
        outs = refs[n_rows + n_consts + 2 + n_kept + n_alias:-1]
        dstate = refs[-1]
        c, g = pl.program_id(0), pl.program_id(1)

        @pl.when(c == 0)
        def _():
            dstate[g] = jnp.zeros((hb,) + STATE_SHAPE, F32)

        @pl.when(jnp.logical_and(c == 0, g == 0))
        def _():
            for r in outs[n_rows:]:
                r[...] = jnp.zeros(r.shape, r.dtype)

        f = functools.partial(step, g * hb) if kept is None else functools.partial(step, g * hb, kept=refs[n_rows + n_consts + 2][...])
        _, vjp = jax.vjp(f, st_ref[...], *[r[...] for r in row_refs], *[r[...] for r in const_refs])
        grads = vjp((dstate[g], dy_ref[...].astype(F32)))
        dstate[g] = grads[0]
        for (_, _, _, per_group), r, d in zip(rows, outs[:n_rows], grads[1:1 + n_rows]):
            if per_group:
                r[...] = d.astype(r.dtype)
            else:
                @pl.when(g == 0)
                def _(r=r):
                    r[...] = jnp.zeros(r.shape, r.dtype)

                r[...] += d.astype(r.dtype)
        for (_, per_group), r, d in zip(consts, outs[n_rows:], grads[1 + n_rows:]):
            if per_group:
                r[pl.ds(g * hb, hb)] += d
            else:
                r[...] += d

    rev = lambda c: nc - 1 - c
    out_specs, out_shape = [], []
    into_arrays, aliases = [], {}
    first_into = n_rows + n_consts + 2 + n_kept
    kept_arrays = [] if kept is None else [kept]
    kept_specs = [pl.BlockSpec((None, None) + k.shape[2:], lambda c, g, nd=k.ndim - 2: (rev(c), g) + (0,) * nd) for k in kept_arrays]
    for k, ((_, n, off, per_group), dt) in enumerate(zip(rows, row_dtypes)):
        if k in into:
            assert per_group and off % (n * hb) == 0 and into[k].dtype == dt
            out_specs.append(pl.BlockSpec((chunk, n * hb), lambda c, g, cb=off // (n * hb): (rev(c), cb + g)))
            out_shape.append(jax.ShapeDtypeStruct(into[k].shape, dt))
            if not isinstance(into[k], jax.ShapeDtypeStruct):
                aliases[first_into + len(into_arrays)] = k
                into_arrays.append(into[k])
        elif per_group:
            out_specs.append(pl.BlockSpec((chunk, n * hb), lambda c, g: (rev(c), g)))
            out_shape.append(jax.ShapeDtypeStruct((t, n_grp * n), dt))
        else:
            out_specs.append(pl.BlockSpec((chunk, n), lambda c, g: (rev(c), 0)))
            out_shape.append(jax.ShapeDtypeStruct((t, n), dt))
    for arr, _ in consts:
        out_specs.append(pl.BlockSpec(arr.shape, lambda c, g, nd=arr.ndim: (0,) * nd))
        out_shape.append(jax.ShapeDtypeStruct(arr.shape, F32))
    return pl.pallas_call(
        body, name=name, grid=(nc, n_grp // hb),
        in_specs=_scan_specs(rows, consts, chunk, rev, hb)
        + [pl.BlockSpec((None, None, hb) + STATE_SHAPE, lambda c, g: (rev(c), g, 0, 0, 0, 0)),
           pl.BlockSpec((chunk, out_cols * hb), lambda c, g: (rev(c), g))] + kept_specs
        + [pl.BlockSpec(memory_space=pl.ANY)] * len(into_arrays),
        out_specs=out_specs, out_shape=out_shape, input_output_aliases=aliases,
        scratch_shapes=[pltpu.VMEM((n_grp // hb, hb) + STATE_SHAPE, F32)],
        compiler_params=_params(("arbitrary", "arbitrary")),
    )(*[r[0] for r in rows], *[c[0] for c in consts], states, dy, *kept_arrays, *into_arrays)


@functools.partial(jax.custom_vjp, nondiff_argnums=(1,))
def _shift_rows(x, k):
    t = x.shape[0]
    row = lax.broadcasted_iota(jnp.int32, x.shape, 0)
    rolled = pltpu.roll(x, k % t, 0)
    return jnp.where(jnp.logical_and(row >= k, row < t + k), rolled, 0.0)


def _shift_rows_fwd(x, k):
    return _shift_rows(x, k), None


def _shift_rows_bwd(k, _, dy):
    return (_shift_rows(dy, -k),)


_shift_rows.defvjp(_shift_rows_fwd, _shift_rows_bwd)


def _conv_silu(x, cw, cb):
    pre = cb + sum(cw[j:j + 1, :] * _shift_rows(x, CONV_K - 1 - j) for j in range(CONV_K))
    return _silu(pre)


def _conv_silu_l2(x, cw, cb):
    y = _conv_silu(x, cw, cb)
    return y * lax.rsqrt(jnp.sum(y * y, axis=-1, keepdims=True) + EPS)


def conv_fwd(name, fn, src, n, off, cw, cb, cw_off):
    t = src.shape[0]
    sb, wb = off // LANES, cw_off // LANES
    ins = [(src, (t, LANES), lambda i: (0, sb + i)), (cw, (CONV_K, LANES), lambda i: (0, wb + i)),
           (cb, (1, LANES), lambda i: (0, wb + i))]
    return blockmap(name, fn, (n // LANES,), ins, [((t, n), F32, (t, LANES), lambda i: (0, i))])[0]


def conv_bwd(name, fn, src, n, off, cw, cb, cw_off, dy, into):
    t = src.shape[0]
    sb, wb = off // LANES, cw_off // LANES

    def bwd(x, w, b, d):
        _, vjp = jax.vjp(fn, x, w, b)
        return vjp(d.astype(F32))

    ins = [(src, (t, LANES), lambda i: (0, sb + i)), (cw, (CONV_K, LANES), lambda i: (0, wb + i)),
           (cb, (1, LANES), lambda i: (0, wb + i)), (dy, (t, LANES), lambda i: (0, i))]
    outs = [(into.shape, BF16, (t, LANES), lambda i: (0, sb + i), into), ((CONV_K, n), F32, (CONV_K, LANES), lambda i: (0, i)),
            ((1, n), F32, (1, LANES), lambda i: (0, i))]
    return blockmap(name, bwd, (n // LANES,), ins, outs)


def _place():
    return lax.axis_index("x"), lax.axis_index("y"), lax.axis_index("c")


def _other_chips(x, y):
    return [(1 - x, y), (x, 1 - y), (1 - x, 1 - y)]


ANY = pl.BlockSpec(memory_space=pl.ANY)


def all_gather8(name, v):
    m_per, n = v.shape

    def body(x_ref, out_ref, send_sems, recv_sems, local_sem):
        x, y, c = _place()
        me, sibling = (x, y, c), (x, y, 1 - c)
        chips = _other_chips(x, y)

        def rows(px, py, pc):
            return out_ref.at[pl.ds((4 * px + 2 * py + pc) * m_per, m_per), :]

        def copy(k, block, to, src=None):
            return pltpu.make_async_remote_copy(
                src_ref=rows(*block) if src is None else src, dst_ref=rows(*block),
                send_sem=send_sems.at[k], recv_sem=recv_sems.at[k], device_id=to, device_id_type=MESH)

        mine = pltpu.make_async_copy(x_ref, rows(*me), local_sem)
        mine.start()
        first = [copy(0, me, sibling, src=x_ref)]
        first += [copy(1 + q, me, (*chip, c), src=x_ref) for q, chip in enumerate(chips)]
        for cp in first:
            cp.start()
        passed = [copy(4 + q, (*chip, c), sibling) for q, chip in enumerate(chips)]
        for q, chip in enumerate(chips):
            copy(1 + q, (*chip, c), me).wait_recv()
            passed[q].start()
        copy(0, sibling, me).wait_recv()
        for q, chip in enumerate(chips):
            copy(4 + q, (*chip, 1 - c), me).wait_recv()
        for cp in first + passed:
            cp.wait_send()
        mine.wait()

    return pl.pallas_call(
        body, name=name, out_shape=jax.ShapeDtypeStruct((N_DEV * m_per, n), v.dtype),
        in_specs=[pl.BlockSpec(memory_space=pltpu.VMEM)], out_specs=pl.BlockSpec(memory_space=pltpu.VMEM),
        scratch_shapes=[pltpu.SemaphoreType.DMA((7,)), pltpu.SemaphoreType.DMA((7,)), pltpu.SemaphoreType.DMA],
    )(v)


def gather_flat(name, vec):
    n = vec.shape[0]
    n_pad = -(-n // (8 * LANES)) * (8 * LANES)
    v = jnp.pad(vec, (0, n_pad - n)).reshape(8, n_pad // 8)
    return all_gather8(name, v).reshape(N_DEV, n_pad)[:, :n]


class Exchange:
    def __init__(self, operands, out_shapes, sem_shape, start, finish):
        self.operands, self.out_shapes, self.sem_shape, self.start, self.finish = list(operands), out_shapes, sem_shape, start, finish


def _start_all_wait_all(make_copies):
    def start(*refs):
        for cp in make_copies(*refs):
            cp.start()

    def finish(*refs):
        for cp in make_copies(*refs):
            cp.wait()

    return start, finish


def run_exchange(name, ex):
    n_in, n_out = len(ex.operands), len(ex.out_shapes)

    def body(*refs):
        ins, outs = refs[:n_in], refs[n_in:n_in + n_out]
        ex.start(ins, outs, *refs[n_in + n_out:])
        ex.finish(ins, outs, *refs[n_in + n_out:])

    return pl.pallas_call(
        body, name=name, out_shape=ex.out_shapes, in_specs=[ANY] * n_in, out_specs=[ANY] * n_out,
        scratch_shapes=[pltpu.SemaphoreType.DMA(ex.sem_shape), pltpu.SemaphoreType.DMA(ex.sem_shape)],
    )(*ex.operands)


def _half(shape, axis, pc):
    h = shape[axis] // 2
    return (pl.ds(pc * h, h), slice(None)) if axis == 0 else (slice(None), pl.ds(pc * h, h))


def _half_shape(shape, axis):
    return tuple(s // 2 if a == axis else s for a, s in enumerate(shape))


def all_gather_shards(shards, axes):
    n_t = len(shards)

    def copies(ins, outs, send_sems, recv_sems):
        x, y, c = _place()
        chips = _other_chips(x, y)

        def copy(t, k, quarter, pc, to, src=None):
            part = _half(ins[t].shape, axes[t], pc)
            dst = outs[t].at[(quarter,) + part]
            return pltpu.make_async_remote_copy(
                src_ref=dst if src is None else ins[t].at[part], dst_ref=dst,
                send_sem=send_sems.at[t, k], recv_sem=recv_sems.at[t, k], device_id=to, device_id_type=MESH)

        first = [copy(t, q, 2 * x + y, c, (*chip, c), src=True) for t in range(n_t) for q, chip in enumerate(chips)]
        landed = [copy(t, q, 2 * px + py, c, (x, y, c)) for t in range(n_t) for q, (px, py) in enumerate(chips)]
        passed = [copy(t, 3 + q, 2 * px + py, c, (x, y, 1 - c)) for t in range(n_t) for q, (px, py) in enumerate(chips)]
        from_sibling = [copy(t, 3 + q, 2 * px + py, 1 - c, (x, y, c)) for t in range(n_t) for q, (px, py) in enumerate(chips)]
        return first, landed, passed, from_sibling

    def start(*refs):
        for cp in copies(*refs)[0]:
            cp.start()

    def finish(*refs):
        first, landed, passed, from_sibling = copies(*refs)
        for arrived, onward in zip(landed, passed):
            arrived.wait_recv()
            onward.start()
        for cp in from_sibling:
            cp.wait_recv()
        for cp in first + passed:
            cp.wait_send()

    return Exchange(shards, [jax.ShapeDtypeStruct((4,) + s.shape, s.dtype) for s in shards], (n_t, 6), start, finish)


def exchange_halves_d2d(grads, axes):
    n_t = len(grads)

    def copies(ins, outs, send_sems, recv_sems):
        x, y, c = _place()
        return [pltpu.make_async_remote_copy(
            src_ref=ins[t].at[(slice(None),) + _half(ins[t].shape[1:], axes[t], 1 - c)], dst_ref=outs[t],
            send_sem=send_sems.at[t], recv_sem=recv_sems.at[t], device_id=(x, y, 1 - c), device_id_type=MESH) for t in range(n_t)]

    shapes = [jax.ShapeDtypeStruct((4,) + _half_shape(g.shape[1:], a), g.dtype) for g, a in zip(grads, axes)]
    return Exchange(grads, shapes, (n_t,), *_start_all_wait_all(copies))


def exchange_quarters_ici(parts):
    n_t = len(parts)

    def copies(ins, outs, send_sems, recv_sems):
        x, y, c = _place()
        return [pltpu.make_async_remote_copy(
            src_ref=ins[t].at[2 * px + py], dst_ref=outs[t].at[q],
            send_sem=send_sems.at[t, q], recv_sem=recv_sems.at[t, q], device_id=(px, py, c), device_id_type=MESH)
            for t in range(n_t) for q, (px, py) in enumerate(_other_chips(x, y))]

    shapes = [jax.ShapeDtypeStruct((3,) + p.shape[1:], p.dtype) for p in parts]
    return Exchange(parts, shapes, (n_t, 3), *_start_all_wait_all(copies))


def swap_d2d(halves):
    n_t = len(halves)

    def copies(ins, outs, send_sems, recv_sems):
        x, y, c = _place()
        return [pltpu.make_async_remote_copy(
            src_ref=ins[t], dst_ref=outs[t], send_sem=send_sems.at[t], recv_sem=recv_sems.at[t],
            device_id=(x, y, 1 - c), device_id_type=MESH) for t in range(n_t)]

    return Exchange(halves, [jax.ShapeDtypeStruct(h.shape, h.dtype) for h in halves], (n_t,), *_start_all_wait_all(copies))


BLOCK_BYTES = 1 << 20


def _row_block(r, c):
    fits = [rb for rb in range(16, r + 1, 16) if r % rb == 0 and rb * c * 4 <= BLOCK_BYTES]
    return max(fits) if fits else r


def _place_scalars():
    x, y, c = _place()
    return jnp.stack([c, 2 * x + y]).astype(jnp.int32)


def reduce_on_chip(tag, grads, axes):
    from_sibling = run_exchange(f"rs_d2d_{tag}", exchange_halves_d2d(grads, axes))
    parts, parts_bf16 = [], []
    for t, (g, s, axis) in enumerate(zip(grads, from_sibling, axes)):
        _, h, cols = s.shape
        rb = _row_block(h, cols)
        nb = h // rb
        blk = lambda k, i, s_ref: (k, i, 0)
        mine = (lambda k, i, s_ref, nb=nb: (k, s_ref[0] * nb + i, 0)) if axis == 0 else (lambda k, i, s_ref: (k, i, s_ref[0]))
        p32, p16 = blockmap(
            f"rs_add_{tag}{t}", lambda a, b: (a + b, a + b), (4, nb),
            [(g, (None, rb, cols), mine), (s, (None, rb, cols), blk)],
            [(s.shape, F32, (None, rb, cols), blk), (s.shape, BF16, (None, rb, cols), blk)], scalars=_place_scalars())
        parts.append(p32)
        parts_bf16.append(p16)
    return parts, parts_bf16


def reduce_across_chips(parts, from_chips):
    halves = []
    for t, (p, q) in enumerate(zip(parts, from_chips)):
        _, h, cols = p.shape
        rb = _row_block(h, cols)
        halves.append(blockmap(
            f"rs_sum{t}", lambda a, b: a + b[0].astype(F32) + b[1].astype(F32) + b[2].astype(F32), (h // rb,),
            [(p, (None, rb, cols), lambda i, s_ref: (s_ref[1], i, 0)), (q, (3, rb, cols), lambda i, s_ref: (0, i, 0))],
            [((h, cols), F32, (rb, cols), lambda i, s_ref: (i, 0))], scalars=_place_scalars())[0])
    return list(zip(halves, run_exchange("rs_swap", swap_d2d(halves))))


def _adamw(w, g, m, v):
    m = ADAM_B1 * m + (1.0 - ADAM_B1) * g
    v = ADAM_B2 * v + (1.0 - ADAM_B2) * jnp.square(g)
    m_hat = m / (1.0 - ADAM_B1 ** ADAM_STEP)
    v_hat = v / (1.0 - ADAM_B2 ** ADAM_STEP)
    delta = -ADAM_LR * (m_hat / (jnp.sqrt(v_hat) + ADAM_EPS) + ADAM_WD * w)
    return delta, m, v


def adamw(name, w, g, m, v):
    _, r, c = w.shape
    rb = _row_block(r, c)
    blk3 = lambda a: (a, (None, rb, c), lambda i: (0, i, 0))
    return blockmap(name, _adamw, (r // rb,), [blk3(w), (g, (rb, c), lambda i: (i, 0)), blk3(m), blk3(v)],
                    [(w.shape, F32, (None, rb, c), lambda i: (0, i, 0))] * 3)


def adamw_halves(name, w, mine, other, m, v, axis):
    _, r, c = w.shape
    h, c = mine.shape
    rb = _row_block(h, c)
    nb = h // rb

    def body(s_ref, w_ref, mine_ref, other_ref, m_ref, v_ref, g_out, d_out, m_out, v_out):
        g = jnp.where(pl.program_id(0) == s_ref[0], mine_ref[...], other_ref[...])
        d, nm, nv = _adamw(w_ref[...], g, m_ref[...], v_ref[...])
        g_out[...], d_out[...], m_out[...], v_out[...] = g, d, nm, nv

    spec3 = pl.BlockSpec((None, rb, c), (lambda k, i, s_ref: (0, k * nb + i, 0)) if axis == 0 else (lambda k, i, s_ref: (0, i, k)))
    spec2 = pl.BlockSpec((rb, c), lambda k, i, s_ref: (i, 0))
    grid_spec = pltpu.PrefetchScalarGridSpec(num_scalar_prefetch=1, grid=(2, nb), in_specs=[spec3, spec2, spec2, spec3, spec3],
                                             out_specs=[spec3] * 4)
    return pl.pallas_call(body, name=name, grid_spec=grid_spec, out_shape=[jax.ShapeDtypeStruct(w.shape, F32)] * 4,
                          compiler_params=_params(("parallel", "parallel")))(_place_scalars(), w, mine, other, m, v)


def _whole(name, fn, ins, outs):
    return blockmap(name, fn, (1,), [(a, a.shape, lambda i, nd=a.ndim: (0,) * nd) for a in ins],
                    [(s, d, s, lambda i, nd=len(s): (0,) * nd) for s, d in outs])


def _premix(x, w, sc, sh):
    return _rms(x, w) * (1.0 + sc) + sh


def _postmix(x, u, w_post, g1, w_pre2, sc2, sh2):
    x1 = x + g1 * _rms(u, w_post)
    return x1, _premix(x1, w_pre2, sc2, sh2)


def _merge(gs, gg, ys, yg):
    return _sigmoid(gs) * ys + _sigmoid(gg) * yg


def _final(x1, y2, w_post2, g2):
    return x1 + g2 * _rms(y2, w_post2)


def kernel(x, c, w_ada, b_ada, norm_mix_pre, norm_mix_post, w_in, ssm_conv_w, ssm_conv_b, ssm_dt_bias, ssm_A_log, ssm_D, ssm_norm_w, gdn_conv_w, gdn_dt_bias, gdn_A_log, gdn_norm_w, w_ssm_up, w_gdn_up, w_out, norm_mlp_pre, norm_mlp_post, w_mlp_up, w_mlp_down, loss_target, m_w_ada, m_b_ada, m_norm_mix_pre, m_norm_mix_post, m_w_in, m_ssm_conv_w, m_ssm_conv_b, m_ssm_dt_bias, m_ssm_A_log, m_ssm_D, m_ssm_norm_w, m_gdn_conv_w, m_gdn_dt_bias, m_gdn_A_log, m_gdn_norm_w, m_w_ssm_up, m_w_gdn_up, m_w_out, m_norm_mlp_pre, m_norm_mlp_post, m_w_mlp_up, m_w_mlp_down, v_w_ada, v_b_ada, v_norm_mix_pre, v_norm_mix_post, v_w_in, v_ssm_conv_w, v_ssm_conv_b, v_ssm_dt_bias, v_ssm_A_log, v_ssm_D, v_ssm_norm_w, v_gdn_conv_w, v_gdn_dt_bias, v_gdn_A_log, v_gdn_norm_w, v_w_ssm_up, v_w_gdn_up, v_w_out, v_norm_mlp_pre, v_norm_mlp_post, v_w_mlp_up, v_w_mlp_down):
    args = dict(locals())
    xi, yi, ci = _place()
    quarter = 2 * xi + yi
    batch = 4 * xi + 2 * yi + ci

    xt, target = x[0], loss_target[0]
    t, d = xt.shape
    hs, hv = ssm_dt_bias.shape[-1], gdn_dt_bias.shape[-1]
    d_inner = hs * SSM_HEAD_DIM
    n_grp = hs // SSM_HEADS_PER_GROUP
    gn = n_grp * SSM_D_STATE
    conv_ssm = d_inner + 2 * gn
    hq = hv // 2
    key, val = hq * GDN_HEAD, hv * GDN_HEAD
    conv_gdn = 2 * key + val
    hidden = 4 * w_mlp_up.shape[-1]
    o_dt = d_inner + conv_ssm
    o_qkv = o_dt + hs
    o_b = o_qkv + conv_gdn + val
    o_a = o_b + hv
    o_gs = o_a + hv
    n_proj = o_gs + 2 * d
    a_z, a_xs, a_bm, a_cm = 0, d_inner, 2 * d_inner, 2 * d_inner + gn
    a_q = o_dt
    a_k, a_v, a_zg = a_q + key, a_q + 2 * key, a_q + conv_gdn
    a_gs = a_zg + val
    a_gg = a_gs + d
    a_small = a_gg + d
    n_al = -(-(a_small + LANES) // MM_TILE_N) * MM_TILE_N

    def to_aligned(w):
        z = lambda n: jnp.zeros((n, w.shape[1]), w.dtype)
        return jnp.concatenate([
            w[:o_dt], w[o_qkv:o_b], w[o_gs:],
            w[o_dt:o_qkv], z(LANE_B - hs), w[o_b:o_a], z(LANE_A - LANE_B - hv), w[o_a:o_gs], z(LANES - LANE_A - hv),
            z(n_al - a_small - LANES)], axis=0)

    def from_aligned(w):
        s = a_small
        return jnp.concatenate([
            w[:o_dt], w[s + LANE_DT:s + LANE_DT + hs], w[a_q:a_gs], w[s + LANE_B:s + LANE_B + hv],
            w[s + LANE_A:s + LANE_A + hv], w[a_gs:a_small]], axis=0)

    def lanes(vec, at):
        return jnp.zeros((1, LANES), F32).at[:, at:at + vec.shape[-1]].set(vec.reshape(1, -1))

    n_cw = CONV_K * ssm_conv_w.shape[-1]
    small_in = gather_flat("ag_small", jnp.concatenate([c.reshape(-1), ssm_conv_w.reshape(-1), gdn_conv_w.reshape(-1)]))
    c_all = small_in[:, :d]
    by_chip = small_in[0::2]

    def whole_conv_w(lo):
        return jnp.transpose(by_chip[:, lo:lo + n_cw].reshape(4, CONV_K, -1), (1, 0, 2)).reshape(CONV_K, -1)

    cw_ssm, cw_gdn = whole_conv_w(d), whole_conv_w(d + n_cw)
    cb_ssm = ssm_conv_b
    cb_gdn = jnp.zeros((1, conv_gdn), F32)

    n_ada = w_ada.shape[-1]
    b_q = lax.dynamic_slice_in_dim(b_ada, quarter * n_ada, n_ada, axis=1)
    mod_q = _whole("ada_fwd", lambda ca, w, b: _bdot(_silu(ca), w) + b, [c_all, w_ada[0], b_q], [((N_DEV, n_ada), F32)])[0]
    mod_all = gather_flat("ag_mod", mod_q.reshape(-1)).reshape(N_DEV, N_DEV, n_ada)[0::2]
    mod = lax.dynamic_index_in_dim(mod_all, batch, axis=1, keepdims=False).reshape(1, 4 * n_ada)
    sh1, sc1, g1, sh2, sc2, g2 = [mod[:, i * d:(i + 1) * d] for i in range(6)]

    transposed = lambda a: jnp.swapaxes(a, 1, 2)
    own = [w.astype(BF16) for w in (transposed(w_in)[0], w_ssm_up[0], w_gdn_up[0], w_out[0], w_mlp_up[0], w_mlp_down[0])]
    with_own = lambda gs, ws: [lax.dynamic_update_index_in_dim(g, w, quarter, 0) for g, w in zip(gs, ws)]
    cols_major = lambda g: jnp.transpose(g, (1, 0, 2)).reshape(g.shape[1], -1)
    rows_major = lambda g: g.reshape(-1, g.shape[2])
    wb_in = to_aligned(rows_major(with_own(run_exchange("ag_w_in", all_gather_shards(own[:1], [1])), own[:1])[0]))

    h1 = rowmap("premix", _premix, [xt], [norm_mix_pre, sc1, sh1], [(d, BF16)])[0]
    proj, gathered = matmul("in_proj", h1, wb_in, tb=True, comm=all_gather_shards(own[1:], [0] * 5))
    gathered = with_own(gathered, own[1:])
    wb_ssm_up, wb_gdn_up, wb_out = rows_major(gathered[0]), rows_major(gathered[1]), rows_major(gathered[2])
    wb_up, wb_down = cols_major(gathered[3]), rows_major(gathered[4])
    xs = conv_fwd("conv_xs", _conv_silu, proj, d_inner, a_xs, cw_ssm, cb_ssm, 0)
    bm = conv_fwd("conv_bm", _conv_silu, proj, gn, a_bm, cw_ssm, cb_ssm, d_inner)
    cm = conv_fwd("conv_cm", _conv_silu, proj, gn, a_cm, cw_ssm, cb_ssm, d_inner + gn)
    q = conv_fwd("conv_q", _conv_silu_l2, proj, key, a_q, cw_gdn, cb_gdn, 0)
    k = conv_fwd("conv_k", _conv_silu_l2, proj, key, a_k, cw_gdn, cb_gdn, key)
    v = conv_fwd("conv_v", _conv_silu, proj, val, a_v, cw_gdn, cb_gdn, 2 * key)

    wide = 2 * LANES
    ssd_rows = [(xs, wide, 0, True), (bm, LANES, 0, True), (cm, LANES, 0, True), (proj, LANES, a_small, False), (proj, wide, a_z, True)]
    ssd_consts = [(lanes(ssm_dt_bias, LANE_DT), False), (lanes(ssm_A_log, LANE_DT), False), (lanes(ssm_D, LANE_DT), False),
                  (ssm_norm_w.reshape(n_grp, 1, wide), True)]
    hb_ssd, hb_gdn = min(SSD_GROUPS_PER_STEP, n_grp), min(GDN_HEADS_PER_STEP, hq)
    y_ssm_n, st_ssm = scan_fwd("ssd_fwd", ssd_step, SSM_CHUNK, n_grp, ssd_rows, ssd_consts, wide, hb_ssd)
    gdn_rows = [(q, LANES, 0, True), (k, LANES, 0, True), (v, wide, 0, True), (proj, wide, a_zg, True), (proj, LANES, a_small, False)]
    gdn_consts = [(lanes(gdn_dt_bias, LANE_A), False), (lanes(gdn_A_log, LANE_A), False), (gdn_norm_w, False)]
    gdn_rows_per_step = GDN_CHUNK * GDN_CHUNKS_PER_STEP
    y_gdn_n, st_gdn, inv_gdn = scan_fwd("gdn_fwd", gdn_step, gdn_rows_per_step, hq, gdn_rows, gdn_consts, wide, hb_gdn,
                                        keep=(2 * hb_gdn, gdn_rows_per_step, gdn_rows_per_step))

    y_ssm = matmul("ssm_up", y_ssm_n, wb_ssm_up)
    y_gdn = matmul("gdn_up", y_gdn_n, wb_gdn_up)
    gates = [(proj, d, a_gs), (proj, d, a_gg)]
    merged = rowmap("merge", _merge, gates + [y_ssm, y_gdn], [], [(d, BF16)])[0]
    u = matmul("w_out", merged, wb_out)
    post_consts = [norm_mix_post, g1, norm_mlp_pre, sc2, sh2]
    x1, h2 = rowmap("postmix", _postmix, [xt, u], post_consts, [(d, F32), (d, BF16)])
    relu2 = lambda acc: (acc, jnp.square(jnp.maximum(acc, 0.0)))
    a_up, act = matmul("mlp_up", h2, wb_up, out_dtypes=(BF16, BF16), epi=relu2)
    y2 = matmul("mlp_down", act, wb_down)

    def final_bwd(x1_, y2_, tgt, w_, g_):
        x2, vjp = jax.vjp(_final, x1_, y2_, w_, g_)
        err = x2 - tgt
        loss = 0.5 * jnp.sum(jnp.mean(err * err, axis=-1, keepdims=True), axis=0, keepdims=True)
        dx1, dy2, dw, dg = vjp(err / d)
        return dx1, dy2, loss, dw, dg

    dx1, dy2, loss_part, d_norm_mlp_post, dg2 = rowmap(
        "final", final_bwd, [x1, y2, target], [norm_mlp_post, g2], [(d, F32), (d, BF16)], [((1, 1), F32), ((1, d), F32), ((1, d), F32)])
    loss = lax.psum(loss_part[0, 0], ("x", "y", "c"))

    d_a = matmul("mlp_down_dx", dy2, wb_down, tb=True, out_dtypes=(BF16,), extras=[a_up],
                 epi=lambda acc, a: acc * 2.0 * jnp.maximum(a.astype(F32), 0.0))
    gw_down = matmul("mlp_down_dw", act, dy2, ta=True)
    dh2 = matmul("mlp_up_dx", d_a, wb_up, tb=True)
    gw_up = matmul("mlp_up_dw", h2, d_a, ta=True)

    def postmix_bwd(x_, u_, dx1_, dh2_, *cs):
        _, vjp = jax.vjp(_postmix, x_, u_, *cs)
        return vjp((dx1_, dh2_))

    dxa, du, d_norm_mix_post, dg1, d_norm_mlp_pre, dsc2, dsh2 = rowmap(
        "postmix_bwd", postmix_bwd, [xt, u, dx1, dh2], post_consts, [(d, F32), (d, BF16)], [((1, d), F32)] * 5)
    d_merged = matmul("w_out_dx", du, wb_out, tb=True)
    gw_out = matmul("w_out_dw", merged, du, ta=True)

    def merge_bwd(gs, gg, ys, yg, dm):
        _, vjp = jax.vjp(_merge, gs, gg, ys, yg)
        dgs, dgg, dys, dyg = vjp(dm)
        return dys, dyg, jnp.concatenate([dgs, dgg], axis=1)

    dy_ssm, dy_gdn, dproj = rowmap("merge_bwd", merge_bwd, gates + [y_ssm, y_gdn, d_merged], [],
                                   [(d, BF16), (d, BF16), (2 * d, BF16, jax.ShapeDtypeStruct((t, n_al), BF16), a_gs)])
    dy_ssm_n = matmul("ssm_up_dx", dy_ssm, wb_ssm_up, tb=True, out_dtypes=(BF16,))
    gw_ssm_up = matmul("ssm_up_dw", y_ssm_n, dy_ssm, ta=True)
    dy_gdn_n = matmul("gdn_up_dx", dy_gdn, wb_gdn_up, tb=True, out_dtypes=(BF16,))
    gw_gdn_up = matmul("gdn_up_dw", y_gdn_n, dy_gdn, ta=True)

    dxs, dbm, dcm, dsmall_ssm, dproj, d_sdtb, d_salog, d_sdsk, d_snw = scan_bwd(
        "ssd_bwd", ssd_step, SSM_CHUNK, n_grp, ssd_rows, ssd_consts, st_ssm, dy_ssm_n, [BF16, BF16, BF16, F32, BF16], hb_ssd,
        {4: dproj})
    dq, dk, dv, dproj, dsmall_gdn, d_gdtb, d_galog, d_gnw = scan_bwd(
        "gdn_bwd", gdn_step, gdn_rows_per_step, hq, gdn_rows, gdn_consts, st_gdn, dy_gdn_n, [BF16, BF16, BF16, BF16, F32], hb_gdn,
        {3: dproj}, kept=inv_gdn)
    dproj, dcw_xs, dcb_xs = conv_bwd("conv_xs_bwd", _conv_silu, proj, d_inner, a_xs, cw_ssm, cb_ssm, 0, dxs, dproj)
    dproj, dcw_bm, dcb_bm = conv_bwd("conv_bm_bwd", _conv_silu, proj, gn, a_bm, cw_ssm, cb_ssm, d_inner, dbm, dproj)
    dproj, dcw_cm, dcb_cm = conv_bwd("conv_cm_bwd", _conv_silu, proj, gn, a_cm, cw_ssm, cb_ssm, d_inner + gn, dcm, dproj)
    dproj, dcw_q, _ = conv_bwd("conv_q_bwd", _conv_silu_l2, proj, key, a_q, cw_gdn, cb_gdn, 0, dq, dproj)
    dproj, dcw_k, _ = conv_bwd("conv_k_bwd", _conv_silu_l2, proj, key, a_k, cw_gdn, cb_gdn, key, dk, dproj)
    dproj, dcw_v, _ = conv_bwd("conv_v_bwd", _conv_silu, proj, val, a_v, cw_gdn, cb_gdn, 2 * key, dv, dproj)
    tail = n_al - a_small
    dproj = rowmap("small_sum", lambda a, b: jnp.concatenate([a + b, jnp.zeros((a.shape[0], tail - LANES), F32)], axis=1),
                   [dsmall_ssm, dsmall_gdn], [], [(tail, BF16, dproj, a_small)])[0]
    quarters_cols = lambda g: jnp.transpose(g.reshape(g.shape[0], 4, -1), (1, 0, 2))
    quarters_rows = lambda g: g.reshape(4, g.shape[0] // 4, g.shape[1])
    rest32, rest16 = reduce_on_chip("rest", [quarters_rows(gw_ssm_up), quarters_rows(gw_gdn_up), quarters_rows(gw_out),
                                             quarters_cols(gw_up), quarters_rows(gw_down)], [0] * 5)
    gw_in_al, rest_chips = matmul("in_proj_dw", dproj, h1, ta=True, comm=exchange_quarters_ici(rest16))
    in32, in16 = reduce_on_chip("in", [quarters_rows(from_aligned(gw_in_al))], [1])
    dh1, in_chips = matmul("in_proj_dx", dproj, wb_in, comm=exchange_quarters_ici(in16))

    def premix_bwd(x_, dxa_, dh1_, w_, sc_, sh_):
        _, vjp = jax.vjp(_premix, x_, w_, sc_, sh_)
        dx, dw, dsc, dsh = vjp(dh1_)
        return dx + dxa_, dw, dsc, dsh

    grad_x, d_norm_mix_pre, dsc1, dsh1 = rowmap(
        "premix_bwd", premix_bwd, [xt, dxa, dh1], [norm_mix_pre, sc1, sh1], [(d, F32)], [((1, d), F32)] * 3)

    dmod_all = gather_flat("ag_dmod", jnp.concatenate([dsh1, dsc1, dg1, dsh2, dsc2, dg2], axis=1).reshape(-1))
    dmod_q = lax.dynamic_slice_in_dim(dmod_all, quarter * n_ada, n_ada, axis=1)
    gw_ada, gb_ada = _whole(
        "ada_bwd", lambda ca, dq_, da_: (_bdot(_silu(ca), dq_, TN), jnp.sum(da_, axis=0, keepdims=True)),
        [c_all, dmod_q, dmod_all], [((d, n_ada), F32), ((1, 4 * n_ada), F32)])

    dcw_ssm = jnp.concatenate([dcw_xs, dcw_bm, dcw_cm], axis=1)
    dcb_ssm = jnp.concatenate([dcb_xs, dcb_bm, dcb_cm], axis=1)
    dcw_gdn = jnp.concatenate([dcw_q, dcw_k, dcw_v], axis=1)
    partial = [d_norm_mix_pre, d_norm_mix_post, dcw_ssm, dcb_ssm, d_sdtb[:, LANE_DT:LANE_DT + hs], d_salog[:, LANE_DT:LANE_DT + hs],
               d_sdsk[:, LANE_DT:LANE_DT + hs], d_snw, dcw_gdn, d_gdtb[:, LANE_A:LANE_A + hv], d_galog[:, LANE_A:LANE_A + hv], d_gnw,
               d_norm_mlp_pre, d_norm_mlp_post]
    sizes = [p.size for p in partial]
    stacked = gather_flat("ag_grads", jnp.concatenate([p.reshape(-1) for p in partial]))
    summed = _whole("small_sum8", lambda s: jnp.sum(s, axis=0, keepdims=True), [stacked], [((1, stacked.shape[1]), F32)])[0][0]
    offs = [0]
    for s in sizes:
        offs.append(offs[-1] + s)
    red = [summed[offs[i]:offs[i + 1]] for i in range(len(sizes))]
    my_cols = lambda full: lax.dynamic_slice_in_dim(full.reshape(CONV_K, -1), quarter * (n_cw // CONV_K), n_cw // CONV_K, axis=1)
    small_grads = {
        "b_ada": gb_ada, "norm_mix_pre": red[0], "norm_mix_post": red[1], "ssm_conv_w": my_cols(red[2]), "ssm_conv_b": red[3],
        "ssm_dt_bias": red[4], "ssm_A_log": red[5], "ssm_D": red[6], "ssm_norm_w": red[7], "gdn_conv_w": my_cols(red[8]),
        "gdn_dt_bias": red[9], "gdn_A_log": red[10], "gdn_norm_w": red[11], "norm_mlp_pre": red[12], "norm_mlp_post": red[13]}

    big_names = ["w_in", "w_ssm_up", "w_gdn_up", "w_out", "w_mlp_up", "w_mlp_down"]
    big_grads = dict(zip(big_names, reduce_across_chips(in32 + rest32, in_chips + rest_chips)))

    names = ['w_ada', 'b_ada', 'norm_mix_pre', 'norm_mix_post', 'w_in', 'ssm_conv_w', 'ssm_conv_b', 'ssm_dt_bias', 'ssm_A_log', 'ssm_D',
             'ssm_norm_w', 'gdn_conv_w', 'gdn_dt_bias', 'gdn_A_log', 'gdn_norm_w', 'w_ssm_up', 'w_gdn_up', 'w_out', 'norm_mlp_pre',
             'norm_mlp_post', 'w_mlp_up', 'w_mlp_down']
    grad, delta, new_m, new_v = {}, {}, {}, {}
    for n, (mine, other) in big_grads.items():
        view, axis = (transposed, 1) if n == "w_in" else ((lambda a: a), 0)
        res = adamw_halves("adamw_" + n, view(args[n]), mine, other, view(args["m_" + n]), view(args["v_" + n]), axis)
        grad[n], delta[n], new_m[n], new_v[n] = [view(a) for a in res]
    grad["w_ada"] = gw_ada.reshape(w_ada.shape)
    delta["w_ada"], new_m["w_ada"], new_v["w_ada"] = adamw("adamw_w_ada", w_ada, gw_ada, m_w_ada, v_w_ada)
    small_names = [n for n in names if n not in grad]
    flat = lambda pre: jnp.concatenate([args[pre + n].reshape(-1) for n in small_names]).reshape(1, 1, -1)
    g_flat = jnp.concatenate([small_grads[n].reshape(-1) for n in small_names]).reshape(1, -1)
    dl, nm, nv = adamw("adamw_small", flat(""), g_flat, flat("m_"), flat("v_"))
    off = 0
    for n in small_names:
        shape = args[n].shape
        size = args[n].size
        grad[n], delta[n], new_m[n], new_v[n] = [a.reshape(-1)[off:off + size].reshape(shape) for a in (g_flat, dl, nm, nv)]
        off += size

    return (loss, grad_x.reshape(x.shape), *[grad[n] for n in names], *[delta[n] for n in names],
            *[new_m[n] for n in names], *[new_v[n] for n in names])
```

```python
import functools

import jax
import jax.numpy as jnp
from jax import lax
from jax.experimental import pallas as pl
from jax.experimental.pallas import tpu as pltpu

F32 = jnp.float32
BF16 = jnp.bfloat16
MESH = pl.DeviceIdType.MESH

EPS = 1e-6
SSM_HEAD_DIM = 64
SSM_HEADS_PER_GROUP = 4
SSM_D_STATE = 128
SSM_CHUNK = 128
GDN_HEAD = 128
GDN_CHUNK = 64
CONV_K = 4
LANE_DT, LANE_B, LANE_A = 0, 32, 48
ADAM_LR, ADAM_B1, ADAM_B2, ADAM_EPS, ADAM_WD, ADAM_STEP = 0.001, 0.9, 0.999, 1e-08, 0.01, 10

VMEM_LIMIT_BYTES = 56 * 1024 * 1024
LANES = 128
N_DEV = 8

NN = (((1,), (0,)), ((), ()))
NT = (((1,), (1,)), ((), ()))
TN = (((0,), (0,)), ((), ()))


BNN = (((2,), (1,)), ((0,), (0,)))
BNT = (((2,), (2,)), ((0,), (0,)))
BTN = (((1,), (1,)), ((0,), (0,)))
_KIND = {NN: ("NN", 0), NT: ("NT", 0), TN: ("TN", 0), BNN: ("NN", 1), BNT: ("NT", 1), BTN: ("TN", 1)}
_DIMS = {"NN": (NN, BNN), "NT": (NT, BNT), "TN": (TN, BTN)}


def _dg(a, b, dims):
    return lax.dot_general(a, b, dims, preferred_element_type=F32)


def _raw_bf16(a, b, dims):
    return _dg(a.astype(BF16), b.astype(BF16), dims)


def _raw_bf16x3(a, b, dims):
    ah, bh = a.astype(BF16), b.astype(BF16)
    al, bl = (a - ah.astype(F32)).astype(BF16), (b - bh.astype(F32)).astype(BF16)
    return _dg(ah, bh, dims) + (_dg(ah, bl, dims) + _dg(al, bh, dims))


def _make_dot(raw):
    @functools.partial(jax.custom_vjp, nondiff_argnums=(2,))
    def dot(a, b, dims):
        return raw(a, b, dims)

    def fwd(a, b, dims):
        return raw(a, b, dims), (a, b)

    def bwd(dims, res, ct):
        a, b = res
        kind, batched = _KIND[dims]
        d = lambda k: _DIMS[k][batched]
        if kind == "NN":
            da, db = raw(ct, b, d("NT")), raw(a, ct, d("TN"))
        elif kind == "NT":
            da, db = raw(ct, b, d("NN")), raw(ct, a, d("TN"))
        else:
            da, db = raw(b, ct, d("NT")), raw(a, ct, d("NN"))
        return da.astype(a.dtype), db.astype(b.dtype)

    dot.defvjp(fwd, bwd)
    return lambda a, b, dims=NN: dot(a, b, dims)


_bdot = _make_dot(_raw_bf16)
_hdot = _make_dot(_raw_bf16x3)


def _mask_dot(mask, x, dims, mask_first=True):
    m = mask.astype(BF16)
    hi = x.astype(BF16)
    r = x - hi.astype(F32)
    mid = r.astype(BF16)
    lo = (r - mid.astype(F32)).astype(BF16)
    return sum(_dg(m, p, dims) if mask_first else _dg(p, m, dims) for p in (hi, mid, lo))


def _sigmoid(x):
    return 0.5 * jnp.tanh(0.5 * x) + 0.5


def _silu(x):
    return x * _sigmoid(x)


def _softplus(x):
    return jnp.maximum(x, 0.0) + jnp.log(1.0 + jnp.exp(-jnp.abs(x)))


def _rms(x, w):
    return x * lax.rsqrt(jnp.mean(x * x, axis=-1, keepdims=True) + EPS) * w


def _lane_col(m, idx):
    lane = lax.broadcasted_iota(jnp.int32, m.shape, 1)
    return jnp.sum(jnp.where(lane == idx, m, 0.0), axis=1, keepdims=True)


def _tril(n, strict=False, seg=None):
    r = lax.broadcasted_iota(jnp.int32, (n, n), 0)
    c = lax.broadcasted_iota(jnp.int32, (n, n), 1)
    low = (r > c) if strict else (r >= c)
    if seg is None or seg >= n:
        return low
    shift = seg.bit_length() - 1
    return jnp.logical_and(low, (r >> shift) == (c >> shift))


def _first_lane(shape):
    return lax.broadcasted_iota(jnp.int32, shape, len(shape) - 1) == 0


@jax.custom_vjp
def _row_form(col):
    shape = col.shape[:-1] + (LANES,)
    return _mask_dot(_first_lane(shape), jnp.broadcast_to(col, shape), NT if col.ndim == 2 else BNT)


def _row_form_fwd(col):
    return _row_form(col), None


def _row_form_bwd(_, ct):
    shape = ct.shape[:-1] + (LANES,)
    sums = _mask_dot(_first_lane(shape), ct, TN if ct.ndim == 2 else BTN, mask_first=False)
    return (jnp.sum(sums, axis=-1, keepdims=True),)


_row_form.defvjp(_row_form_fwd, _row_form_bwd)


@functools.partial(jax.custom_vjp, nondiff_argnums=(1,))
def _cumsum_rows(x, seg):
    return _mask_dot(_tril(x.shape[0], seg=seg), x, NN)


def _cumsum_rows_fwd(x, seg):
    return _cumsum_rows(x, seg), None


def _cumsum_rows_bwd(seg, _, ct):
    return (_mask_dot(_tril(ct.shape[0], seg=seg), ct, TN),)


_cumsum_rows.defvjp(_cumsum_rows_fwd, _cumsum_rows_bwd)


def _head_rows(m_t, idx):
    sub = lax.broadcasted_iota(jnp.int32, m_t.shape, 0)
    return jnp.sum(jnp.where(sub == idx, m_t, 0.0), axis=0, keepdims=True)


def _params(sem):
    return pltpu.CompilerParams(dimension_semantics=sem, vmem_limit_bytes=VMEM_LIMIT_BYTES)


def _into_plumbing(outs, first_input):
    arrays, aliases = [], {}
    for k, o in enumerate(outs):
        if len(o) > 4 and not isinstance(o[4], jax.ShapeDtypeStruct):
            aliases[first_input + len(arrays)] = k
            arrays.append(o[4])
    return arrays, aliases


def blockmap(name, fn, grid, ins, outs, accs=(), scalars=None):
    n_in, n_out, n_acc = len(ins), len(outs), len(accs)
    n_grid = len(grid)
    n_pre = 0 if scalars is None else 1
    into_arrays, aliases = _into_plumbing(outs, n_pre + n_in)
    n_into = len(into_arrays)

    def body(*refs):
        refs = refs[n_pre:n_pre + n_in] + refs[n_pre + n_in + n_into:]
        vals = fn(*[r[...] for r in refs[:n_in]])
        if not isinstance(vals, (tuple, list)):
            vals = (vals,)
        for r, v in zip(refs[n_in:n_in + n_out], vals[:n_out]):
            r[...] = v.astype(r.dtype)
        if n_acc:
            first = functools.reduce(jnp.logical_and, [pl.program_id(a) == 0 for a in range(n_grid)])
            acc_refs = refs[n_in + n_out:]

            @pl.when(first)
            def _():
                for r in acc_refs:
                    r[...] = jnp.zeros(r.shape, r.dtype)

            for r, v in zip(acc_refs, vals[n_out:]):
                r[...] += v.astype(r.dtype)

    zeros = lambda nd: (lambda *_: (0,) * nd)
    in_specs = [pl.BlockSpec(b, im) for _, b, im in ins] + [pl.BlockSpec(memory_space=pl.ANY)] * n_into
    out_specs = [pl.BlockSpec(o[2], o[3]) for o in outs] + [pl.BlockSpec(s, zeros(len(s))) for s, _ in accs]
    out_shape = [jax.ShapeDtypeStruct(o[0], o[1]) for o in outs] + [jax.ShapeDtypeStruct(s, d) for s, d in accs]
    cparams = _params(("arbitrary",) * n_grid if n_acc else ("parallel",) * n_grid)
    arrays = [a for a, _, _ in ins] + into_arrays
    if scalars is None:
        return pl.pallas_call(body, name=name, grid=grid, in_specs=in_specs, out_specs=out_specs, out_shape=out_shape,
                              input_output_aliases=aliases, compiler_params=cparams)(*arrays)
    spec = pltpu.PrefetchScalarGridSpec(num_scalar_prefetch=1, grid=grid, in_specs=in_specs, out_specs=out_specs)
    return pl.pallas_call(body, name=name, grid_spec=spec, out_shape=out_shape, input_output_aliases=aliases,
                          compiler_params=cparams)(scalars, *arrays)


def rowmap(name, fn, rows, consts, outs, accs=(), rb=256):
    norm = [(r, r.shape[1], 0) if not isinstance(r, tuple) else (r[0], r[1], r[2] // r[1]) for r in rows]
    assert all(not isinstance(r, tuple) or r[2] % r[1] == 0 for r in rows)
    t = norm[0][0].shape[0]
    rb = min(rb, t)
    ins = [(a, (rb, n), (lambda i, cb=cb: (i, cb))) for a, n, cb in norm]
    ins += [(cst, cst.shape, (lambda i, nd=cst.ndim: (0,) * nd)) for cst in consts]
    o = []
    for out in outs:
        if len(out) == 2:
            o.append(((t, out[0]), out[1], (rb, out[0]), lambda i: (i, 0)))
        else:
            n, d, into, off = out
            assert off % n == 0 and into.dtype == d
            o.append((into.shape, d, (rb, n), (lambda i, cb=off // n: (i, cb)), into))
    return blockmap(name, fn, (t // rb,), ins, o, accs)


MM_TILE_M, MM_TILE_N, MM_TILE_K = 1024, 1024, 2048


def _tile(dim, cap):
    if dim <= cap:
        return dim
    best = max(t for t in range(LANES, cap + 1, LANES) if dim % t == 0)
    return best


def matmul(name, a, b, ta=False, tb=False, out_dtypes=(F32,), epi=None, extras=(), comm=None):
    (k_dim, m_dim) = a.shape if ta else a.shape[::-1]
    n_dim = b.shape[0] if tb else b.shape[1]
    assert (b.shape[1] if tb else b.shape[0]) == k_dim, (name, a.shape, b.shape)
    tm, tn, tk = _tile(m_dim, MM_TILE_M), _tile(n_dim, MM_TILE_N), _tile(k_dim, MM_TILE_K)
    grid = (m_dim // tm, n_dim // tn, k_dim // tk)
    k_steps = grid[2]
    n_extra, n_out = len(extras), len(out_dtypes)
    n_cin, n_cout = (len(comm.operands), len(comm.out_shapes)) if comm else (0, 0)
    dims = (((0 if ta else 1,), (1 if tb else 0,)), ((), ()))

    def body(*refs):
        ins, outs, scratch = refs[:2 + n_extra + n_cin], refs[2 + n_extra + n_cin:][:n_out + n_cout], refs[2 + n_extra + n_cin + n_out + n_cout:]
        extra_refs, out_refs = ins[2:2 + n_extra], outs[:n_out]
        ids = [pl.program_id(ax) for ax in range(3)]
        if comm:
            comm_refs = (ins[2 + n_extra:], outs[n_out:], scratch[-2], scratch[-1])

            @pl.when(functools.reduce(jnp.logical_and, [i == 0 for i in ids]))
            def _():
                comm.start(*comm_refs)

        def finish(acc):
            vals = (acc,) if epi is None else epi(acc, *[r[...] for r in extra_refs])
            if not isinstance(vals, (tuple, list)):
                vals = (vals,)
            for r, v in zip(out_refs, vals):
                r[...] = v.astype(r.dtype)

        prod = lax.dot_general(ins[0][...].astype(BF16), ins[1][...].astype(BF16), dims, preferred_element_type=F32)
        if k_steps == 1:
            finish(prod)
        else:
            acc_ref = scratch[0]

            @pl.when(ids[2] == 0)
            def _():
                acc_ref[...] = jnp.zeros(acc_ref.shape, F32)

            acc_ref[...] += prod

            @pl.when(ids[2] == k_steps - 1)
            def _():
                finish(acc_ref[...])

        if comm:
            @pl.when(functools.reduce(jnp.logical_and, [i == g - 1 for i, g in zip(ids, grid)]))
            def _():
                comm.finish(*comm_refs)

    a_spec = pl.BlockSpec((tk, tm), lambda i, j, k: (k, i)) if ta else pl.BlockSpec((tm, tk), lambda i, j, k: (i, k))
    b_spec = pl.BlockSpec((tn, tk), lambda i, j, k: (j, k)) if tb else pl.BlockSpec((tk, tn), lambda i, j, k: (k, j))
    mn_spec = pl.BlockSpec((tm, tn), lambda i, j, k: (i, j))
    scratch_shapes = [] if k_steps == 1 else [pltpu.VMEM((tm, tn), F32)]
    if comm:
        scratch_shapes += [pltpu.SemaphoreType.DMA(comm.sem_shape), pltpu.SemaphoreType.DMA(comm.sem_shape)]
    res = pl.pallas_call(
        body, name=name, grid=grid,
        in_specs=[a_spec, b_spec] + [mn_spec] * n_extra + [ANY] * n_cin,
        out_specs=[mn_spec] * n_out + [ANY] * n_cout,
        out_shape=[jax.ShapeDtypeStruct((m_dim, n_dim), d) for d in out_dtypes] + (comm.out_shapes if comm else []),
        scratch_shapes=scratch_shapes,
        compiler_params=_params(("arbitrary",) * 3 if comm else ("parallel", "parallel", "arbitrary")),
    )(a, b, *extras, *(comm.operands if comm else []))
    main = res[:n_out] if n_out > 1 else res[0]
    return (main, list(res[n_out:])) if comm else main


def ssd_step(g0, state, tail, zxbc, small, cw, cb, p_dtb, p_alog, p_dsk, nw):
    d_in, gn = state.shape[0] * 2 * LANES, state.shape[0] * LANES
    z = zxbc[:, :d_in]
    act, new_tail = _conv_silu_carried(tail, zxbc[:, d_in:], cw, cb)
    xs, bm, cm = act[:, :d_in], act[:, d_in:d_in + gn], act[:, d_in + gn:]
    hb, n = state.shape[0], xs.shape[0]
    n_pair, n_head = 2 * hb, 4 * hb
    causal = _tril(n)
    dt_all = _softplus(small + p_dtb)
    a_all = dt_all * (-jnp.exp(p_alog))
    acum_all = _cumsum_rows(a_all, n)
    acum_t = acum_all.T
    lane0 = LANE_DT + SSM_HEADS_PER_GROUP * g0
    sub = lax.broadcasted_iota(jnp.int32, acum_t.shape, 0)
    heads = range(n_head)
    acum = jnp.stack([_lane_col(acum_all, lane0 + i) for i in heads])
    acum_row = jnp.stack([jnp.sum(jnp.where(sub == lane0 + i, acum_t, 0.0), axis=0, keepdims=True) for i in heads])
    dt = jnp.stack([_lane_col(dt_all, lane0 + i) for i in heads])
    dsk = jnp.stack([_lane_col(p_dsk, lane0 + i) for i in heads])
    decay = jnp.exp(jnp.where(causal, acum - acum_row, -jnp.inf))
    a_last = acum[:, n - 1:n, :]

    def split(a):
        return [a[:, i * LANES:(i + 1) * LANES] for i in range(a.shape[1] // LANES)]

    def pairs(a, axis=2):
        even = jnp.stack([a[2 * p] for p in range(n_pair)])
        odd = jnp.stack([a[2 * p + 1] for p in range(n_pair)])
        shape = (n_pair, LANES, LANES) if axis == 1 else (n_pair, a.shape[1], LANES)
        return jnp.where(lax.broadcasted_iota(jnp.int32, shape, axis) < SSM_HEAD_DIM, even, odd)

    bms, cms = split(bm), split(cm)
    cb = _bdot(jnp.stack(cms), jnp.stack(bms), BNT)
    cbd = jnp.stack([cb[i // SSM_HEADS_PER_GROUP] for i in heads]) * decay
    xp = jnp.stack(split(xs))
    xdt = xp * pairs(dt)
    yd = _bdot(cbd, jnp.stack([xdt[i // 2] for i in heads]), BNN)
    lane = lax.broadcasted_iota(jnp.int32, (n_pair, n, LANES), 2)
    y_diag = jnp.where(lane < SSM_HEAD_DIM, jnp.stack([yd[2 * p] for p in range(n_pair)]), jnp.stack([yd[2 * p + 1] for p in range(n_pair)]))
    st = state.reshape(n_pair, LANES, LANES)
    cm2 = jnp.stack([cms[p // 2] for p in range(n_pair)])
    bm2 = jnp.stack([bms[p // 2] for p in range(n_pair)])
    y_off = _bdot(cm2, st, BNT) * pairs(jnp.exp(acum))
    new = st * pairs(jnp.exp(a_last), axis=1) + _bdot(xdt * pairs(jnp.exp(a_last - acum)), bm2, BTN)
    y = y_diag + y_off + pairs(dsk) * xp
    y = jnp.concatenate([y[p] for p in range(n_pair)], axis=1) * _silu(z)
    wide = 2 * LANES
    y = jnp.concatenate([_rms(y[:, i * wide:(i + 1) * wide], nw[i]) for i in range(hb)], axis=1)
    return new.reshape(state.shape), new_tail, y


@functools.partial(jax.custom_vjp, nondiff_argnums=(1,))
def _unit_lower_inverse(a, seg):
    n = a.shape[-1]
    r = lax.broadcasted_iota(jnp.int32, (n, n), 0)
    c = lax.broadcasted_iota(jnp.int32, (n, n), 1)
    shift = min(INVERSE_BASE, seg).bit_length() - 1
    power = jnp.where((r >> shift) == (c >> shift), a, 0.0)
    inv = (r == c).astype(F32) - power
    span = 2
    while span < (1 << shift):
        power = _hdot(power, power, BNN)
        inv = inv + _hdot(inv, power, BNN)
        span *= 2
    while (1 << shift) < seg:
        below = jnp.logical_and((r >> (shift + 1)) == (c >> (shift + 1)), (r >> shift) != (c >> shift))
        inv = inv - _hdot(inv, _hdot(jnp.where(below, a, 0.0), inv, BNN), BNN)
        shift += 1
    return inv


def _unit_lower_inverse_fwd(a, seg):
    inv = _unit_lower_inverse(a, seg)
    return inv, inv


def _unit_lower_inverse_bwd(seg, inv, ct):
    return (-_hdot(_hdot(inv, ct, BTN), inv, BNT),)


_unit_lower_inverse.defvjp(_unit_lower_inverse_fwd, _unit_lower_inverse_bwd)


@jax.custom_vjp
def _known_inverse(a, inv):
    return inv


def _known_inverse_fwd(a, inv):
    return inv, inv


def _known_inverse_bwd(inv, ct):
    return _unit_lower_inverse_bwd(None, inv, ct)[0], jnp.zeros_like(inv)


_known_inverse.defvjp(_known_inverse_fwd, _known_inverse_bwd)


@functools.partial(jax.custom_vjp, nondiff_argnums=(1,))
def _rotate_rows(x, k):
    return x if k == 0 else pltpu.roll(x, k % x.shape[0], 0)


def _rotate_rows_fwd(x, k):
    return _rotate_rows(x, k), None


def _rotate_rows_bwd(k, _, ct):
    return (_rotate_rows(ct, -k),)


_rotate_rows.defvjp(_rotate_rows_fwd, _rotate_rows_bwd)


def _conv_silu_carried(tail, x, cw, cb=0.0):
    n = x.shape[0]
    ext = jnp.concatenate([tail, x], axis=0)
    pre = cb + sum(cw[j:j + 1, :] * _rotate_rows(ext, CONV_K - 1 - j)[8:] for j in range(CONV_K))
    return _silu(pre), x[n - 8:]


def _l2norm(x):
    return x * lax.rsqrt(jnp.sum(x * x, axis=-1, keepdims=True) + EPS)


def gdn_step(hq0, state, tail, qkvz, small, cw, p_dtb, p_alog, nw, keep=False, kept=None):
    n, chunk = qkvz.shape[0], GDN_CHUNK
    hb = state.shape[0]
    nb = 2 * hb
    cur = state.reshape(nb, LANES, LANES)
    conv_cols = 4 * hb * LANES
    act, new_tail = _conv_silu_carried(tail, qkvz[:, :conv_cols], cw)
    q, k, v = act[:, :hb * LANES], act[:, hb * LANES:2 * hb * LANES], act[:, 2 * hb * LANES:]
    z = qkvz[:, conv_cols:]
    causal, strict = _tril(n, seg=chunk), _tril(n, True, seg=chunk)
    beta_all = _sigmoid(small)
    g_all = -jnp.exp(p_alog) * _softplus(small + p_dtb)
    gcum_all = _cumsum_rows(g_all, chunk)
    gcum_t = gcum_all.T
    split = lambda a: [a[:, i * LANES:(i + 1) * LANES] for i in range(a.shape[1] // LANES)]
    per_value_head = lambda a: jnp.stack([a[i // 2] for i in range(nb)])
    qh, kh = _l2norm(jnp.stack(split(q))) * (GDN_HEAD ** -0.5), _l2norm(jnp.stack(split(k)))
    q2, k2 = per_value_head(qh), per_value_head(kh)
    v2, z2 = jnp.stack(split(v)), jnp.stack(split(z))
    gcum = jnp.stack([_lane_col(gcum_all, LANE_A + 2 * hq0 + i) for i in range(nb)])
    gcum_row = jnp.stack([_head_rows(gcum_t, LANE_A + 2 * hq0 + i) for i in range(nb)])
    beta = jnp.stack([_lane_col(beta_all, LANE_B + 2 * hq0 + i) for i in range(nb)])
    dmat = jnp.exp(jnp.where(causal, gcum - gcum_row, -jnp.inf))
    a_low = jnp.where(strict, beta * per_value_head(_bdot(kh, kh, BNT)) * dmat, 0.0)
    inv = _unit_lower_inverse(a_low, chunk) if kept is None else _known_inverse(a_low, kept)
    egc = jnp.exp(gcum)
    u = _hdot(inv, v2 * beta, BNN)
    w = _hdot(inv, k2 * (beta * egc), BNN)
    q_dec = q2 * egc
    v_new, o_state = [], []
    for s in range(n // chunk):
        rows = slice(s * chunk, (s + 1) * chunk)
        v_new.append(u[:, rows] - _bdot(w[:, rows], cur, BNN))
        o_state.append(_bdot(q_dec[:, rows], cur, BNN))
        g_last = gcum[:, (s + 1) * chunk - 1:(s + 1) * chunk, :]
        k_dec = k2[:, rows] * jnp.exp(g_last - gcum[:, rows])
        cur = cur * jnp.exp(g_last) + _bdot(k_dec, v_new[-1], BTN)
    o = jnp.concatenate(o_state, axis=1) + _bdot(per_value_head(_bdot(qh, kh, BNT)) * dmat, jnp.concatenate(v_new, axis=1), BNN)
    out = _rms(o, nw) * _silu(z2)
    res = (cur.reshape(state.shape), new_tail, jnp.concatenate([out[i] for i in range(nb)], axis=1))
    return res + (inv,) if keep else res


STATE_SHAPE = (2, LANES, LANES)
SSD_GROUPS_PER_STEP = 4
GDN_HEADS_PER_STEP = 8
GDN_CHUNKS_PER_STEP = 2
INVERSE_BASE = 16


def _scan_specs(rows, consts, chunk, chunk_of, hb):
    specs = []
    for _, n, off, per_group in rows:
        if per_group:
            assert off % (n * hb) == 0
            specs.append(pl.BlockSpec((chunk, n * hb), lambda c, g, cb=off // (n * hb): (chunk_of(c), cb + g)))
        else:
            assert off % n == 0
            specs.append(pl.BlockSpec((chunk, n), lambda c, g, cb=off // n: (chunk_of(c), cb)))
    for arr, per_group in consts:
        if per_group:
            specs.append(pl.BlockSpec((hb, 1, arr.shape[2]), lambda c, g: (g, 0, 0)))
        else:
            specs.append(pl.BlockSpec(arr.shape, lambda c, g, nd=arr.ndim: (0,) * nd))
    return specs


def scan_fwd(name, step, chunk, n_grp, rows, consts, out_cols, hb, keep=None, state_shape=None, tail_cols=None):
    t = rows[0][0].shape[0]
    nc = t // chunk
    n_rows, n_consts = len(rows), len(consts)
    state_shape = state_shape or (hb,) + STATE_SHAPE
    carried = [state_shape] + ([(8, tail_cols)] if tail_cols else [])
    n_car = len(carried)

    def body(*refs):
        row_refs, const_refs = refs[:n_rows], refs[n_rows:n_rows + n_consts]
        y_ref = refs[n_rows + n_consts]
        saved_refs = refs[n_rows + n_consts + 1:n_rows + n_consts + 1 + n_car]
        scratch = refs[-n_car:]
        c, g = pl.program_id(0), pl.program_id(1)

        @pl.when(c == 0)
        def _():
            for s, shape in zip(scratch, carried):
                s[g] = jnp.zeros(shape, F32)

        cur = [s[g] for s in scratch]
        for r, v in zip(saved_refs, cur):
            r[...] = v
        vals = [r[...] for r in row_refs] + [r[...] for r in const_refs]
        res = step(g * hb, *cur, *vals) if keep is None else step(g * hb, *cur, *vals, keep=True)
        for s, v in zip(scratch, res[:n_car]):
            s[g] = v
        y_ref[...] = res[n_car].astype(y_ref.dtype)
        if keep is not None:
            refs[n_rows + n_consts + 1 + n_car][...] = res[n_car + 1]

    lead = (nc, n_grp // hb)
    out_specs = [pl.BlockSpec((chunk, out_cols * hb), lambda c, g: (c, g))]
    out_shape = [jax.ShapeDtypeStruct((t, n_grp * out_cols), BF16)]
    for shape in carried + ([keep] if keep is not None else []):
        out_specs.append(pl.BlockSpec((None, None) + shape, lambda c, g, nd=len(shape): (c, g) + (0,) * nd))
        out_shape.append(jax.ShapeDtypeStruct(lead + shape, F32))
    return pl.pallas_call(
        body, name=name, grid=lead,
        in_specs=_scan_specs(rows, consts, chunk, lambda c: c, hb),
        out_specs=out_specs, out_shape=out_shape,
        scratch_shapes=[pltpu.VMEM((n_grp // hb,) + shape, F32) for shape in carried],
        compiler_params=_params(("arbitrary", "arbitrary")),
    )(*[r[0] for r in rows], *[c[0] for c in consts])


def scan_bwd(name, step, chunk, n_grp, rows, consts, saved, dy, row_dtypes, hb, into, kept=None):
    t = rows[0][0].shape[0]
    nc = t // chunk
    n_rows, n_consts, n_car = len(rows), len(consts), len(saved)
    carried = [s.shape[2:] for s in saved]
    out_cols = dy.shape[1] // n_grp
    n_alias = sum(not isinstance(v, jax.ShapeDtypeStruct) for v in into.values())
    n_kept = 0 if kept is None else 1
    n_in = n_rows + n_consts + n_car + 1 + n_kept + n_alias

    def body(*refs):
        row_refs, const_refs = refs[:n_rows], refs[n_rows:n_rows + n_consts]
        saved_refs = refs[n_rows + n_consts:n_rows + n_consts + n_car]
        dy_ref = refs[n_rows + n_consts + n_car]
        outs = refs[n_in:-n_car]
        scratch = refs[-n_car:]
        c, g = pl.program_id(0), pl.program_id(1)

        @pl.when(c == 0)
        def _():
            for s, shape in zip(scratch, carried):
                s[g] = jnp.zeros(shape, F32)

        @pl.when(jnp.logical_and(c == 0, g == 0))
        def _():
            for r in outs[n_rows:]:
                r[...] = jnp.zeros(r.shape, r.dtype)

        f = functools.partial(step, g * hb) if kept is None else functools.partial(step, g * hb, kept=refs[n_rows + n_consts + n_car + 1][...])
        _, vjp = jax.vjp(f, *[r[...] for r in saved_refs], *[r[...] for r in row_refs], *[r[...] for r in const_refs])
        grads = vjp(tuple(s[g] for s in scratch) + (dy_ref[...].astype(F32),))
        for s, d in zip(scratch, grads[:n_car]):
            s[g] = d
        for (_, _, _, per_group), r, d in zip(rows, outs[:n_rows], grads[n_car:n_car + n_rows]):
            if per_group:
                r[...] = d.astype(r.dtype)
            else:
                @pl.when(g == 0)
                def _(r=r):
                    r[...] = jnp.zeros(r.shape, r.dtype)

                r[...] += d.astype(r.dtype)
        for (_, per_group), r, d in zip(consts, outs[n_rows:], grads[n_car + n_rows:]):
            if per_group:
                r[pl.ds(g * hb, hb)] += d
            else:
                r[...] += d

    rev = lambda c: nc - 1 - c
    out_specs, out_shape = [], []
    into_arrays, aliases = [], {}
    first_into = n_in - n_alias
    kept_arrays = [] if kept is None else [kept]
    by_step = lambda a: pl.BlockSpec((None, None) + a.shape[2:], lambda c, g, nd=a.ndim - 2: (rev(c), g) + (0,) * nd)
    for k, ((_, n, off, per_group), dt) in enumerate(zip(rows, row_dtypes)):
        if k in into:
            assert per_group and off % (n * hb) == 0 and into[k].dtype == dt
            out_specs.append(pl.BlockSpec((chunk, n * hb), lambda c, g, cb=off // (n * hb): (rev(c), cb + g)))
            out_shape.append(jax.ShapeDtypeStruct(into[k].shape, dt))
            if not isinstance(into[k], jax.ShapeDtypeStruct):
                aliases[first_into + len(into_arrays)] = k
                into_arrays.append(into[k])
        elif per_group:
            out_specs.append(pl.BlockSpec((chunk, n * hb), lambda c, g: (rev(c), g)))
            out_shape.append(jax.ShapeDtypeStruct((t, n_grp * n), dt))
        else:
            out_specs.append(pl.BlockSpec((chunk, n), lambda c, g: (rev(c), 0)))
            out_shape.append(jax.ShapeDtypeStruct((t, n), dt))
    for arr, _ in consts:
        out_specs.append(pl.BlockSpec(arr.shape, lambda c, g, nd=arr.ndim: (0,) * nd))
        out_shape.append(jax.ShapeDtypeStruct(arr.shape, F32))
    return pl.pallas_call(
        body, name=name, grid=(nc, n_grp // hb),
        in_specs=_scan_specs(rows, consts, chunk, rev, hb) + [by_step(s) for s in saved]
        + [pl.BlockSpec((chunk, out_cols * hb), lambda c, g: (rev(c), g))] + [by_step(k) for k in kept_arrays]
        + [pl.BlockSpec(memory_space=pl.ANY)] * len(into_arrays),
        out_specs=out_specs, out_shape=out_shape, input_output_aliases=aliases,
        scratch_shapes=[pltpu.VMEM((n_grp // hb,) + shape, F32) for shape in carried],
        compiler_params=_params(("arbitrary", "arbitrary")),
    )(*[r[0] for r in rows], *[c[0] for c in consts], *saved, dy, *kept_arrays, *into_arrays)


@functools.partial(jax.custom_vjp, nondiff_argnums=(1,))
def _shift_rows(x, k):
    if k == 0:
        return x
    t = x.shape[0]
    row = lax.broadcasted_iota(jnp.int32, x.shape, 0)
    rolled = pltpu.roll(x, k % t, 0)
    return jnp.where(row >= k if k > 0 else row < t + k, rolled, 0.0)


def _shift_rows_fwd(x, k):
    return _shift_rows(x, k), None


def _shift_rows_bwd(k, _, dy):
    return (_shift_rows(dy, -k),)


_shift_rows.defvjp(_shift_rows_fwd, _shift_rows_bwd)


def _conv_silu(x, cw, cb):
    pre = cb + sum(cw[j:j + 1, :] * _shift_rows(x, CONV_K - 1 - j) for j in range(CONV_K))
    return _silu(pre)


def _conv_silu_l2(x, cw, cb):
    y = _conv_silu(x, cw, cb)
    return y * lax.rsqrt(jnp.sum(y * y, axis=-1, keepdims=True) + EPS)


def conv_fwd(name, fn, src, n, off, cw, cb, cw_off):
    t = src.shape[0]
    sb, wb = off // LANES, cw_off // LANES
    ins = [(src, (t, LANES), lambda i: (0, sb + i)), (cw, (CONV_K, LANES), lambda i: (0, wb + i)),
           (cb, (1, LANES), lambda i: (0, wb + i))]
    return blockmap(name, fn, (n // LANES,), ins, [((t, n), F32, (t, LANES), lambda i: (0, i))])[0]


def conv_bwd(name, fn, src, n, off, cw, cb, cw_off, dy, into):
    t = src.shape[0]
    sb, wb = off // LANES, cw_off // LANES

    def bwd(x, w, b, d):
        _, vjp = jax.vjp(fn, x, w, b)
        return vjp(d.astype(F32))

    ins = [(src, (t, LANES), lambda i: (0, sb + i)), (cw, (CONV_K, LANES), lambda i: (0, wb + i)),
           (cb, (1, LANES), lambda i: (0, wb + i)), (dy, (t, LANES), lambda i: (0, i))]
    outs = [(into.shape, BF16, (t, LANES), lambda i: (0, sb + i), into), ((CONV_K, n), F32, (CONV_K, LANES), lambda i: (0, i)),
            ((1, n), F32, (1, LANES), lambda i: (0, i))]
    return blockmap(name, bwd, (n // LANES,), ins, outs)


def _place():
    return lax.axis_index("x"), lax.axis_index("y"), lax.axis_index("c")


def _other_chips(x, y):
    return [(1 - x, y), (x, 1 - y), (1 - x, 1 - y)]


ANY = pl.BlockSpec(memory_space=pl.ANY)


def all_gather8(name, v):
    m_per, n = v.shape

    def body(x_ref, out_ref, send_sems, recv_sems, local_sem):
        x, y, c = _place()
        me, sibling = (x, y, c), (x, y, 1 - c)
        chips = _other_chips(x, y)

        def rows(px, py, pc):
            return out_ref.at[pl.ds((4 * px + 2 * py + pc) * m_per, m_per), :]

        def copy(k, block, to, src=None):
            return pltpu.make_async_remote_copy(
                src_ref=rows(*block) if src is None else src, dst_ref=rows(*block),
                send_sem=send_sems.at[k], recv_sem=recv_sems.at[k], device_id=to, device_id_type=MESH)

        mine = pltpu.make_async_copy(x_ref, rows(*me), local_sem)
        mine.start()
        first = [copy(0, me, sibling, src=x_ref)]
        first += [copy(1 + q, me, (*chip, c), src=x_ref) for q, chip in enumerate(chips)]
        for cp in first:
            cp.start()
        passed = [copy(4 + q, (*chip, c), sibling) for q, chip in enumerate(chips)]
        for q, chip in enumerate(chips):
            copy(1 + q, (*chip, c), me).wait_recv()
            passed[q].start()
        copy(0, sibling, me).wait_recv()
        for q, chip in enumerate(chips):
            copy(4 + q, (*chip, 1 - c), me).wait_recv()
        for cp in first + passed:
            cp.wait_send()
        mine.wait()

    return pl.pallas_call(
        body, name=name, out_shape=jax.ShapeDtypeStruct((N_DEV * m_per, n), v.dtype),
        in_specs=[pl.BlockSpec(memory_space=pltpu.VMEM)], out_specs=pl.BlockSpec(memory_space=pltpu.VMEM),
        scratch_shapes=[pltpu.SemaphoreType.DMA((7,)), pltpu.SemaphoreType.DMA((7,)), pltpu.SemaphoreType.DMA],
    )(v)


def gather_flat(name, vec):
    n = vec.shape[0]
    n_pad = -(-n // (8 * LANES)) * (8 * LANES)
    v = jnp.pad(vec, (0, n_pad - n)).reshape(8, n_pad // 8)
    return all_gather8(name, v).reshape(N_DEV, n_pad)[:, :n]


class Exchange:
    def __init__(self, operands, out_shapes, sem_shape, start, finish):
        self.operands, self.out_shapes, self.sem_shape, self.start, self.finish = list(operands), out_shapes, sem_shape, start, finish


def _start_all_wait_all(make_copies):
    def start(*refs):
        for cp in make_copies(*refs):
            cp.start()

    def finish(*refs):
        for cp in make_copies(*refs):
            cp.wait()

    return start, finish


def run_exchange(name, ex):
    n_in, n_out = len(ex.operands), len(ex.out_shapes)

    def body(*refs):
        ins, outs = refs[:n_in], refs[n_in:n_in + n_out]
        ex.start(ins, outs, *refs[n_in + n_out:])
        ex.finish(ins, outs, *refs[n_in + n_out:])

    return pl.pallas_call(
        body, name=name, out_shape=ex.out_shapes, in_specs=[ANY] * n_in, out_specs=[ANY] * n_out,
        scratch_shapes=[pltpu.SemaphoreType.DMA(ex.sem_shape), pltpu.SemaphoreType.DMA(ex.sem_shape)],
    )(*ex.operands)


def _half(shape, axis, pc):
    h = shape[axis] // 2
    return (pl.ds(pc * h, h), slice(None)) if axis == 0 else (slice(None), pl.ds(pc * h, h))


def _half_shape(shape, axis):
    return tuple(s // 2 if a == axis else s for a, s in enumerate(shape))


def all_gather_shards(shards, axes):
    n_t = len(shards)

    def copies(ins, outs, send_sems, recv_sems):
        x, y, c = _place()
        chips = _other_chips(x, y)

        def copy(t, k, quarter, pc, to, src=None):
            part = _half(ins[t].shape, axes[t], pc)
            dst = outs[t].at[(quarter,) + part]
            return pltpu.make_async_remote_copy(
                src_ref=dst if src is None else ins[t].at[part], dst_ref=dst,
                send_sem=send_sems.at[t, k], recv_sem=recv_sems.at[t, k], device_id=to, device_id_type=MESH)

        first = [copy(t, q, 2 * x + y, c, (*chip, c), src=True) for t in range(n_t) for q, chip in enumerate(chips)]
        landed = [copy(t, q, 2 * px + py, c, (x, y, c)) for t in range(n_t) for q, (px, py) in enumerate(chips)]
        passed = [copy(t, 3 + q, 2 * px + py, c, (x, y, 1 - c)) for t in range(n_t) for q, (px, py) in enumerate(chips)]
        from_sibling = [copy(t, 3 + q, 2 * px + py, 1 - c, (x, y, c)) for t in range(n_t) for q, (px, py) in enumerate(chips)]
        return first, landed, passed, from_sibling

    def start(*refs):
        for cp in copies(*refs)[0]:
            cp.start()

    def finish(*refs):
        first, landed, passed, from_sibling = copies(*refs)
        for arrived, onward in zip(landed, passed):
            arrived.wait_recv()
            onward.start()
        for cp in from_sibling:
            cp.wait_recv()
        for cp in first + passed:
            cp.wait_send()

    return Exchange(shards, [jax.ShapeDtypeStruct((4,) + s.shape, s.dtype) for s in shards], (n_t, 6), start, finish)


def exchange_halves_d2d(grads, axes):
    n_t = len(grads)

    def copies(ins, outs, send_sems, recv_sems):
        x, y, c = _place()
        return [pltpu.make_async_remote_copy(
            src_ref=ins[t].at[(slice(None),) + _half(ins[t].shape[1:], axes[t], 1 - c)], dst_ref=outs[t],
            send_sem=send_sems.at[t], recv_sem=recv_sems.at[t], device_id=(x, y, 1 - c), device_id_type=MESH) for t in range(n_t)]

    shapes = [jax.ShapeDtypeStruct((4,) + _half_shape(g.shape[1:], a), g.dtype) for g, a in zip(grads, axes)]
    return Exchange(grads, shapes, (n_t,), *_start_all_wait_all(copies))


def exchange_quarters_ici(parts):
    n_t = len(parts)

    def copies(ins, outs, send_sems, recv_sems):
        x, y, c = _place()
        return [pltpu.make_async_remote_copy(
            src_ref=ins[t].at[2 * px + py], dst_ref=outs[t].at[q],
            send_sem=send_sems.at[t, q], recv_sem=recv_sems.at[t, q], device_id=(px, py, c), device_id_type=MESH)
            for t in range(n_t) for q, (px, py) in enumerate(_other_chips(x, y))]

    shapes = [jax.ShapeDtypeStruct((3,) + p.shape[1:], p.dtype) for p in parts]
    return Exchange(parts, shapes, (n_t, 3), *_start_all_wait_all(copies))


def swap_d2d(halves):
    n_t = len(halves)

    def copies(ins, outs, send_sems, recv_sems):
        x, y, c = _place()
        return [pltpu.make_async_remote_copy(
            src_ref=ins[t], dst_ref=outs[t], send_sem=send_sems.at[t], recv_sem=recv_sems.at[t],
            device_id=(x, y, 1 - c), device_id_type=MESH) for t in range(n_t)]

    return Exchange(halves, [jax.ShapeDtypeStruct(h.shape, h.dtype) for h in halves], (n_t,), *_start_all_wait_all(copies))


BLOCK_BYTES = 1 << 20


def _row_block(r, c):
    fits = [rb for rb in range(16, r + 1, 16) if r % rb == 0 and rb * c * 4 <= BLOCK_BYTES]
    return max(fits) if fits else r


def _place_scalars():
    x, y, c = _place()
    return jnp.stack([c, 2 * x + y]).astype(jnp.int32)


def reduce_on_chip(tag, grads, axes):
    from_sibling = run_exchange(f"rs_d2d_{tag}", exchange_halves_d2d(grads, axes))
    parts, parts_bf16 = [], []
    for t, (g, s, axis) in enumerate(zip(grads, from_sibling, axes)):
        _, h, cols = s.shape
        rb = _row_block(h, cols)
        nb = h // rb
        blk = lambda k, i, s_ref: (k, i, 0)
        mine = (lambda k, i, s_ref, nb=nb: (k, s_ref[0] * nb + i, 0)) if axis == 0 else (lambda k, i, s_ref: (k, i, s_ref[0]))
        p32, p16 = blockmap(
            f"rs_add_{tag}{t}", lambda a, b: (a + b, a + b), (4, nb),
            [(g, (None, rb, cols), mine), (s, (None, rb, cols), blk)],
            [(s.shape, F32, (None, rb, cols), blk), (s.shape, BF16, (None, rb, cols), blk)], scalars=_place_scalars())
        parts.append(p32)
        parts_bf16.append(p16)
    return parts, parts_bf16


def reduce_across_chips(parts, from_chips):
    halves = []
    for t, (p, q) in enumerate(zip(parts, from_chips)):
        _, h, cols = p.shape
        rb = _row_block(h, cols)
        halves.append(blockmap(
            f"rs_sum{t}", lambda a, b: a + b[0].astype(F32) + b[1].astype(F32) + b[2].astype(F32), (h // rb,),
            [(p, (None, rb, cols), lambda i, s_ref: (s_ref[1], i, 0)), (q, (3, rb, cols), lambda i, s_ref: (0, i, 0))],
            [((h, cols), F32, (rb, cols), lambda i, s_ref: (i, 0))], scalars=_place_scalars())[0])
    return list(zip(halves, run_exchange("rs_swap", swap_d2d(halves))))


def _adamw(w, g, m, v):
    m = ADAM_B1 * m + (1.0 - ADAM_B1) * g
    v = ADAM_B2 * v + (1.0 - ADAM_B2) * jnp.square(g)
    m_hat = m / (1.0 - ADAM_B1 ** ADAM_STEP)
    v_hat = v / (1.0 - ADAM_B2 ** ADAM_STEP)
    delta = -ADAM_LR * (m_hat / (jnp.sqrt(v_hat) + ADAM_EPS) + ADAM_WD * w)
    return delta, m, v


def adamw(name, w, g, m, v):
    _, r, c = w.shape
    rb = _row_block(r, c)
    blk3 = lambda a: (a, (None, rb, c), lambda i: (0, i, 0))
    return blockmap(name, _adamw, (r // rb,), [blk3(w), (g, (rb, c), lambda i: (i, 0)), blk3(m), blk3(v)],
                    [(w.shape, F32, (None, rb, c), lambda i: (0, i, 0))] * 3)


def adamw_halves(name, w, mine, other, m, v, axis):
    _, r, c = w.shape
    h, c = mine.shape
    rb = _row_block(h, c)
    nb = h // rb

    def body(s_ref, w_ref, mine_ref, other_ref, m_ref, v_ref, g_out, d_out, m_out, v_out):
        g = jnp.where(pl.program_id(0) == s_ref[0], mine_ref[...], other_ref[...])
        d, nm, nv = _adamw(w_ref[...], g, m_ref[...], v_ref[...])
        g_out[...], d_out[...], m_out[...], v_out[...] = g, d, nm, nv

    spec3 = pl.BlockSpec((None, rb, c), (lambda k, i, s_ref: (0, k * nb + i, 0)) if axis == 0 else (lambda k, i, s_ref: (0, i, k)))
    spec2 = pl.BlockSpec((rb, c), lambda k, i, s_ref: (i, 0))
    grid_spec = pltpu.PrefetchScalarGridSpec(num_scalar_prefetch=1, grid=(2, nb), in_specs=[spec3, spec2, spec2, spec3, spec3],
                                             out_specs=[spec3] * 4)
    return pl.pallas_call(body, name=name, grid_spec=grid_spec, out_shape=[jax.ShapeDtypeStruct(w.shape, F32)] * 4,
                          compiler_params=_params(("parallel", "parallel")))(_place_scalars(), w, mine, other, m, v)


def _whole(name, fn, ins, outs):
    return blockmap(name, fn, (1,), [(a, a.shape, lambda i, nd=a.ndim: (0,) * nd) for a in ins],
                    [(s, d, s, lambda i, nd=len(s): (0,) * nd) for s, d in outs])


def _premix(x, w, sc, sh):
    return _rms(x, w) * (1.0 + sc) + sh


def _postmix(x, u, w_post, g1, w_pre2, sc2, sh2):
    x1 = x + g1 * _rms(u, w_post)
    return x1, _premix(x1, w_pre2, sc2, sh2)


def _merge(gs, gg, ys, yg):
    return _sigmoid(gs) * ys + _sigmoid(gg) * yg


def _final(x1, y2, w_post2, g2):
    return x1 + g2 * _rms(y2, w_post2)


def kernel(x, c, w_ada, b_ada, norm_mix_pre, norm_mix_post, w_in, ssm_conv_w, ssm_conv_b, ssm_dt_bias, ssm_A_log, ssm_D, ssm_norm_w, gdn_conv_w, gdn_dt_bias, gdn_A_log, gdn_norm_w, w_ssm_up, w_gdn_up, w_out, norm_mlp_pre, norm_mlp_post, w_mlp_up, w_mlp_down, loss_target, m_w_ada, m_b_ada, m_norm_mix_pre, m_norm_mix_post, m_w_in, m_ssm_conv_w, m_ssm_conv_b, m_ssm_dt_bias, m_ssm_A_log, m_ssm_D, m_ssm_norm_w, m_gdn_conv_w, m_gdn_dt_bias, m_gdn_A_log, m_gdn_norm_w, m_w_ssm_up, m_w_gdn_up, m_w_out, m_norm_mlp_pre, m_norm_mlp_post, m_w_mlp_up, m_w_mlp_down, v_w_ada, v_b_ada, v_norm_mix_pre, v_norm_mix_post, v_w_in, v_ssm_conv_w, v_ssm_conv_b, v_ssm_dt_bias, v_ssm_A_log, v_ssm_D, v_ssm_norm_w, v_gdn_conv_w, v_gdn_dt_bias, v_gdn_A_log, v_gdn_norm_w, v_w_ssm_up, v_w_gdn_up, v_w_out, v_norm_mlp_pre, v_norm_mlp_post, v_w_mlp_up, v_w_mlp_down):
    args = dict(locals())
    xi, yi, ci = _place()
    quarter = 2 * xi + yi
    batch = 4 * xi + 2 * yi + ci

    xt, target = x[0], loss_target[0]
    t, d = xt.shape
    hs, hv = ssm_dt_bias.shape[-1], gdn_dt_bias.shape[-1]
    d_inner = hs * SSM_HEAD_DIM
    n_grp = hs // SSM_HEADS_PER_GROUP
    gn = n_grp * SSM_D_STATE
    conv_ssm = d_inner + 2 * gn
    hq = hv // 2
    key, val = hq * GDN_HEAD, hv * GDN_HEAD
    conv_gdn = 2 * key + val
    hidden = 4 * w_mlp_up.shape[-1]
    o_dt = d_inner + conv_ssm
    o_qkv = o_dt + hs
    o_b = o_qkv + conv_gdn + val
    o_a = o_b + hv
    o_gs = o_a + hv
    n_proj = o_gs + 2 * d
    a_z, a_xs, a_bm, a_cm = 0, d_inner, 2 * d_inner, 2 * d_inner + gn
    a_q = o_dt
    a_k, a_v, a_zg = a_q + key, a_q + 2 * key, a_q + conv_gdn
    a_gs = a_zg + val
    a_gg = a_gs + d
    a_small = a_gg + d
    n_al = -(-(a_small + LANES) // MM_TILE_N) * MM_TILE_N

    def to_aligned(w):
        z = lambda n: jnp.zeros((n, w.shape[1]), w.dtype)
        return jnp.concatenate([
            w[:o_dt], w[o_qkv:o_b], w[o_gs:],
            w[o_dt:o_qkv], z(LANE_B - hs), w[o_b:o_a], z(LANE_A - LANE_B - hv), w[o_a:o_gs], z(LANES - LANE_A - hv),
            z(n_al - a_small - LANES)], axis=0)

    def from_aligned(w):
        s = a_small
        return jnp.concatenate([
            w[:o_dt], w[s + LANE_DT:s + LANE_DT + hs], w[a_q:a_gs], w[s + LANE_B:s + LANE_B + hv],
            w[s + LANE_A:s + LANE_A + hv], w[a_gs:a_small]], axis=0)

    def lanes(vec, at):
        return jnp.zeros((1, LANES), F32).at[:, at:at + vec.shape[-1]].set(vec.reshape(1, -1))

    n_cw = CONV_K * ssm_conv_w.shape[-1]
    small_in = gather_flat("ag_small", jnp.concatenate([c.reshape(-1), ssm_conv_w.reshape(-1), gdn_conv_w.reshape(-1)]))
    c_all = small_in[:, :d]
    by_chip = small_in[0::2]

    def whole_conv_w(lo):
        return jnp.transpose(by_chip[:, lo:lo + n_cw].reshape(4, CONV_K, -1), (1, 0, 2)).reshape(CONV_K, -1)

    cw_ssm, cw_gdn = whole_conv_w(d), whole_conv_w(d + n_cw)
    cb_ssm = ssm_conv_b
    cb_gdn = jnp.zeros((1, conv_gdn), F32)

    n_ada = w_ada.shape[-1]
    b_q = lax.dynamic_slice_in_dim(b_ada, quarter * n_ada, n_ada, axis=1)
    mod_q = _whole("ada_fwd", lambda ca, w, b: _bdot(_silu(ca), w) + b, [c_all, w_ada[0], b_q], [((N_DEV, n_ada), F32)])[0]
    mod_all = gather_flat("ag_mod", mod_q.reshape(-1)).reshape(N_DEV, N_DEV, n_ada)[0::2]
    mod = lax.dynamic_index_in_dim(mod_all, batch, axis=1, keepdims=False).reshape(1, 4 * n_ada)
    sh1, sc1, g1, sh2, sc2, g2 = [mod[:, i * d:(i + 1) * d] for i in range(6)]

    transposed = lambda a: jnp.swapaxes(a, 1, 2)
    own = [w.astype(BF16) for w in (transposed(w_in)[0], w_ssm_up[0], w_gdn_up[0], w_out[0], w_mlp_up[0], w_mlp_down[0])]
    with_own = lambda gs, ws: [lax.dynamic_update_index_in_dim(g, w, quarter, 0) for g, w in zip(gs, ws)]
    cols_major = lambda g: jnp.transpose(g, (1, 0, 2)).reshape(g.shape[1], -1)
    rows_major = lambda g: g.reshape(-1, g.shape[2])
    wb_in = to_aligned(rows_major(with_own(run_exchange("ag_w_in", all_gather_shards(own[:1], [1])), own[:1])[0]))

    h1 = rowmap("premix", _premix, [xt], [norm_mix_pre, sc1, sh1], [(d, BF16)])[0]
    proj, gathered = matmul("in_proj", h1, wb_in, tb=True, comm=all_gather_shards(own[1:], [0] * 5))
    gathered = with_own(gathered, own[1:])
    wb_ssm_up, wb_gdn_up, wb_out = rows_major(gathered[0]), rows_major(gathered[1]), rows_major(gathered[2])
    wb_up, wb_down = cols_major(gathered[3]), rows_major(gathered[4])

    wide = 2 * LANES
    ssd_rows = [(proj, d_inner + conv_ssm, a_z, True), (proj, LANES, a_small, False)]
    ssd_consts = [(cw_ssm, False), (cb_ssm, False), (lanes(ssm_dt_bias, LANE_DT), False), (lanes(ssm_A_log, LANE_DT), False),
                  (lanes(ssm_D, LANE_DT), False), (ssm_norm_w.reshape(n_grp, 1, wide), False)]
    y_ssm_n, st_ssm, tails_ssm = scan_fwd("ssd_fwd", ssd_step, SSM_CHUNK, 1, ssd_rows, ssd_consts, d_inner, 1,
                                          state_shape=(n_grp,) + STATE_SHAPE, tail_cols=conv_ssm)
    gdn_rows = [(proj, conv_gdn + val, a_q, True), (proj, LANES, a_small, False)]
    gdn_consts = [(cw_gdn, False), (lanes(gdn_dt_bias, LANE_A), False), (lanes(gdn_A_log, LANE_A), False), (gdn_norm_w, False)]
    gdn_rows_per_step = GDN_CHUNK * GDN_CHUNKS_PER_STEP
    y_gdn_n, st_gdn, tails_gdn, inv_gdn = scan_fwd(
        "gdn_fwd", gdn_step, gdn_rows_per_step, 1, gdn_rows, gdn_consts, val, 1, keep=(hv, gdn_rows_per_step, gdn_rows_per_step),
        state_shape=(hq,) + STATE_SHAPE, tail_cols=conv_gdn)

    y_ssm = matmul("ssm_up", y_ssm_n, wb_ssm_up)
    y_gdn = matmul("gdn_up", y_gdn_n, wb_gdn_up)
    gates = [(proj, d, a_gs), (proj, d, a_gg)]
    merged = rowmap("merge", _merge, gates + [y_ssm, y_gdn], [], [(d, BF16)])[0]
    u = matmul("w_out", merged, wb_out)
    post_consts = [norm_mix_post, g1, norm_mlp_pre, sc2, sh2]
    x1, h2 = rowmap("postmix", _postmix, [xt, u], post_consts, [(d, F32), (d, BF16)])
    relu2 = lambda acc: (acc, jnp.square(jnp.maximum(acc, 0.0)))
    a_up, act = matmul("mlp_up", h2, wb_up, out_dtypes=(BF16, BF16), epi=relu2)
    y2 = matmul("mlp_down", act, wb_down)

    def final_bwd(x1_, y2_, tgt, w_, g_):
        x2, vjp = jax.vjp(_final, x1_, y2_, w_, g_)
        err = x2 - tgt
        loss = 0.5 * jnp.sum(jnp.mean(err * err, axis=-1, keepdims=True), axis=0, keepdims=True)
        dx1, dy2, dw, dg = vjp(err / d)
        return dx1, dy2, loss, dw, dg

    dx1, dy2, loss_part, d_norm_mlp_post, dg2 = rowmap(
        "final", final_bwd, [x1, y2, target], [norm_mlp_post, g2], [(d, F32), (d, BF16)], [((1, 1), F32), ((1, d), F32), ((1, d), F32)])

    d_a = matmul("mlp_down_dx", dy2, wb_down, tb=True, out_dtypes=(BF16,), extras=[a_up],
                 epi=lambda acc, a: acc * 2.0 * jnp.maximum(a.astype(F32), 0.0))
    gw_down = matmul("mlp_down_dw", act, dy2, ta=True)
    dh2 = matmul("mlp_up_dx", d_a, wb_up, tb=True)
    gw_up = matmul("mlp_up_dw", h2, d_a, ta=True)

    def postmix_bwd(x_, u_, dx1_, dh2_, *cs):
        _, vjp = jax.vjp(_postmix, x_, u_, *cs)
        return vjp((dx1_, dh2_))

    dxa, du, d_norm_mix_post, dg1, d_norm_mlp_pre, dsc2, dsh2 = rowmap(
        "postmix_bwd", postmix_bwd, [xt, u, dx1, dh2], post_consts, [(d, F32), (d, BF16)], [((1, d), F32)] * 5)
    d_merged = matmul("w_out_dx", du, wb_out, tb=True)
    gw_out = matmul("w_out_dw", merged, du, ta=True)

    def merge_bwd(gs, gg, ys, yg, dm):
        _, vjp = jax.vjp(_merge, gs, gg, ys, yg)
        dgs, dgg, dys, dyg = vjp(dm)
        return dys, dyg, jnp.concatenate([dgs, dgg], axis=1)

    dy_ssm, dy_gdn, dproj = rowmap("merge_bwd", merge_bwd, gates + [y_ssm, y_gdn, d_merged], [],
                                   [(d, BF16), (d, BF16), (2 * d, BF16, jax.ShapeDtypeStruct((t, n_al), BF16), a_gs)])
    dy_ssm_n = matmul("ssm_up_dx", dy_ssm, wb_ssm_up, tb=True, out_dtypes=(BF16,))
    gw_ssm_up = matmul("ssm_up_dw", y_ssm_n, dy_ssm, ta=True)
    dy_gdn_n = matmul("gdn_up_dx", dy_gdn, wb_gdn_up, tb=True, out_dtypes=(BF16,))
    gw_gdn_up = matmul("gdn_up_dw", y_gdn_n, dy_gdn, ta=True)

    dproj, dsmall_ssm, dcw_ssm, dcb_ssm, d_sdtb, d_salog, d_sdsk, d_snw = scan_bwd(
        "ssd_bwd", ssd_step, SSM_CHUNK, 1, ssd_rows, ssd_consts, [st_ssm, tails_ssm], dy_ssm_n, [BF16, F32], 1, {0: dproj})
    dproj, dsmall_gdn, dcw_gdn, d_gdtb, d_galog, d_gnw = scan_bwd(
        "gdn_bwd", gdn_step, gdn_rows_per_step, 1, gdn_rows, gdn_consts, [st_gdn, tails_gdn], dy_gdn_n, [BF16, F32], 1,
        {0: dproj}, kept=inv_gdn)
    tail = n_al - a_small
    dproj = rowmap("small_sum", lambda a, b: jnp.concatenate([a + b, jnp.zeros((a.shape[0], tail - LANES), F32)], axis=1),
                   [dsmall_ssm, dsmall_gdn], [], [(tail, BF16, dproj, a_small)])[0]
    quarters_cols = lambda g: jnp.transpose(g.reshape(g.shape[0], 4, -1), (1, 0, 2))
    quarters_rows = lambda g: g.reshape(4, g.shape[0] // 4, g.shape[1])
    rest32, rest16 = reduce_on_chip("rest", [quarters_rows(gw_ssm_up), quarters_rows(gw_gdn_up), quarters_rows(gw_out),
                                             quarters_cols(gw_up), quarters_rows(gw_down)], [0] * 5)
    gw_in_al, rest_chips = matmul("in_proj_dw", dproj, h1, ta=True, comm=exchange_quarters_ici(rest16))
    in32, in16 = reduce_on_chip("in", [quarters_rows(from_aligned(gw_in_al))], [1])
    dh1, in_chips = matmul("in_proj_dx", dproj, wb_in, comm=exchange_quarters_ici(in16))

    def premix_bwd(x_, dxa_, dh1_, w_, sc_, sh_):
        _, vjp = jax.vjp(_premix, x_, w_, sc_, sh_)
        dx, dw, dsc, dsh = vjp(dh1_)
        return dx + dxa_, dw, dsc, dsh

    grad_x, d_norm_mix_pre, dsc1, dsh1 = rowmap(
        "premix_bwd", premix_bwd, [xt, dxa, dh1], [norm_mix_pre, sc1, sh1], [(d, F32)], [((1, d), F32)] * 3)

    dmod_all = gather_flat("ag_dmod", jnp.concatenate([dsh1, dsc1, dg1, dsh2, dsc2, dg2], axis=1).reshape(-1))
    dmod_q = lax.dynamic_slice_in_dim(dmod_all, quarter * n_ada, n_ada, axis=1)
    gw_ada, gb_ada = _whole(
        "ada_bwd", lambda ca, dq_, da_: (_bdot(_silu(ca), dq_, TN), jnp.sum(da_, axis=0, keepdims=True)),
        [c_all, dmod_q, dmod_all], [((d, n_ada), F32), ((1, 4 * n_ada), F32)])

    partial = [d_norm_mix_pre, d_norm_mix_post, dcw_ssm, dcb_ssm, d_sdtb[:, LANE_DT:LANE_DT + hs], d_salog[:, LANE_DT:LANE_DT + hs],
               d_sdsk[:, LANE_DT:LANE_DT + hs], d_snw, dcw_gdn, d_gdtb[:, LANE_A:LANE_A + hv], d_galog[:, LANE_A:LANE_A + hv], d_gnw,
               d_norm_mlp_pre, d_norm_mlp_post, loss_part]
    sizes = [p.size for p in partial]
    stacked = gather_flat("ag_grads", jnp.concatenate([p.reshape(-1) for p in partial]))
    summed = _whole("small_sum8", lambda s: jnp.sum(s, axis=0, keepdims=True), [stacked], [((1, stacked.shape[1]), F32)])[0][0]
    offs = [0]
    for s in sizes:
        offs.append(offs[-1] + s)
    red = [summed[offs[i]:offs[i + 1]] for i in range(len(sizes))]
    loss = red[-1][0]
    my_cols = lambda full: lax.dynamic_slice_in_dim(full.reshape(CONV_K, -1), quarter * (n_cw // CONV_K), n_cw // CONV_K, axis=1)
    small_grads = {
        "b_ada": gb_ada, "norm_mix_pre": red[0], "norm_mix_post": red[1], "ssm_conv_w": my_cols(red[2]), "ssm_conv_b": red[3],
        "ssm_dt_bias": red[4], "ssm_A_log": red[5], "ssm_D": red[6], "ssm_norm_w": red[7], "gdn_conv_w": my_cols(red[8]),
        "gdn_dt_bias": red[9], "gdn_A_log": red[10], "gdn_norm_w": red[11], "norm_mlp_pre": red[12], "norm_mlp_post": red[13]}

    big_names = ["w_in", "w_ssm_up", "w_gdn_up", "w_out", "w_mlp_up", "w_mlp_down"]
    big_grads = dict(zip(big_names, reduce_across_chips(in32 + rest32, in_chips + rest_chips)))

    names = ['w_ada', 'b_ada', 'norm_mix_pre', 'norm_mix_post', 'w_in', 'ssm_conv_w', 'ssm_conv_b', 'ssm_dt_bias', 'ssm_A_log', 'ssm_D',
             'ssm_norm_w', 'gdn_conv_w', 'gdn_dt_bias', 'gdn_A_log', 'gdn_norm_w', 'w_ssm_up', 'w_gdn_up', 'w_out', 'norm_mlp_pre',
             'norm_mlp_post', 'w_mlp_up', 'w_mlp_down']
    grad, delta, new_m, new_v = {}, {}, {}, {}
    for n, (mine, other) in big_grads.items():
        view, axis = (transposed, 1) if n == "w_in" else ((lambda a: a), 0)
        res = adamw_halves("adamw_" + n, view(args[n]), mine, other, view(args["m_" + n]), view(args["v_" + n]), axis)
        grad[n], delta[n], new_m[n], new_v[n] = [view(a) for a in res]
    grad["w_ada"] = gw_ada.reshape(w_ada.shape)
    delta["w_ada"], new_m["w_ada"], new_v["w_ada"] = adamw("adamw_w_ada", w_ada, gw_ada, m_w_ada, v_w_ada)
    small_names = [n for n in names if n not in grad]
    flat = lambda pre: jnp.concatenate([args[pre + n].reshape(-1) for n in small_names]).reshape(1, 1, -1)
    g_flat = jnp.concatenate([small_grads[n].reshape(-1) for n in small_names]).reshape(1, -1)
    dl, nm, nv = adamw("adamw_small", flat(""), g_flat, flat("m_"), flat("v_"))
    off = 0
    for n in small_names:
        shape = args[n].shape
        size = args[n].size
        grad[n], delta[n], new_m[n], new_v[n] = [a.reshape(-1)[off:off + size].reshape(shape) for a in (g_flat, dl, nm, nv)]
        off += size

    return (loss, grad_x.reshape(x.shape), *[grad[n] for n in names], *[delta[n] for n in names],
            *[new_m[n] for n in names], *[new_v[n] for n in names])
```

```python
import functools

import jax
import jax.numpy as jnp
from jax import lax
from jax.experimental import pallas as pl
from jax.experimental.pallas import tpu as pltpu

F32 = jnp.float32
BF16 = jnp.bfloat16
MESH = pl.DeviceIdType.MESH

EPS = 1e-6
SSM_HEAD_DIM = 64
SSM_HEADS_PER_GROUP = 4
SSM_D_STATE = 128
SSM_CHUNK = 128
GDN_HEAD = 128
GDN_CHUNK = 64
CONV_K = 4
LANE_DT, LANE_B, LANE_A = 0, 32, 48
ADAM_LR, ADAM_B1, ADAM_B2, ADAM_EPS, ADAM_WD, ADAM_STEP = 0.001, 0.9, 0.999, 1e-08, 0.01, 10

VMEM_LIMIT_BYTES = 56 * 1024 * 1024
LANES = 128
N_DEV = 8

NN = (((1,), (0,)), ((), ()))
NT = (((1,), (1,)), ((), ()))
TN = (((0,), (0,)), ((), ()))


BNN = (((2,), (1,)), ((0,), (0,)))
BNT = (((2,), (2,)), ((0,), (0,)))
BTN = (((1,), (1,)), ((0,), (0,)))
_KIND = {NN: ("NN", 0), NT: ("NT", 0), TN: ("TN", 0), BNN: ("NN", 1), BNT: ("NT", 1), BTN: ("TN", 1)}
_DIMS = {"NN": (NN, BNN), "NT": (NT, BNT), "TN": (TN, BTN)}


def _dg(a, b, dims):
    return lax.dot_general(a, b, dims, preferred_element_type=F32)


def _raw_bf16(a, b, dims):
    return _dg(a.astype(BF16), b.astype(BF16), dims)


def _raw_bf16x3(a, b, dims):
    ah, bh = a.astype(BF16), b.astype(BF16)
    al, bl = (a - ah.astype(F32)).astype(BF16), (b - bh.astype(F32)).astype(BF16)
    return _dg(ah, bh, dims) + (_dg(ah, bl, dims) + _dg(al, bh, dims))


def _make_dot(raw):
    @functools.partial(jax.custom_vjp, nondiff_argnums=(2,))
    def dot(a, b, dims):
        return raw(a, b, dims)

    def fwd(a, b, dims):
        return raw(a, b, dims), (a, b)

    def bwd(dims, res, ct):
        a, b = res
        kind, batched = _KIND[dims]
        d = lambda k: _DIMS[k][batched]
        if kind == "NN":
            da, db = raw(ct, b, d("NT")), raw(a, ct, d("TN"))
        elif kind == "NT":
            da, db = raw(ct, b, d("NN")), raw(ct, a, d("TN"))
        else:
            da, db = raw(b, ct, d("NT")), raw(a, ct, d("NN"))
        return da.astype(a.dtype), db.astype(b.dtype)

    dot.defvjp(fwd, bwd)
    return lambda a, b, dims=NN: dot(a, b, dims)


_bdot = _make_dot(_raw_bf16)
_hdot = _make_dot(_raw_bf16x3)


def _mask_dot(mask, x, dims):
    m = mask.astype(BF16)
    hi = x.astype(BF16)
    r = x - hi.astype(F32)
    mid = r.astype(BF16)
    lo = (r - mid.astype(F32)).astype(BF16)
    return sum(_dg(m, p, dims) for p in (hi, mid, lo))


def _sigmoid(x):
    return 0.5 * jnp.tanh(0.5 * x) + 0.5


def _silu(x):
    return x * _sigmoid(x)


def _softplus(x):
    return jnp.maximum(x, 0.0) + jnp.log(1.0 + jnp.exp(-jnp.abs(x)))


def _rms(x, w):
    return x * lax.rsqrt(jnp.mean(x * x, axis=-1, keepdims=True) + EPS) * w


def _lane_col(m, idx):
    lane = lax.broadcasted_iota(jnp.int32, m.shape, 1)
    return jnp.sum(jnp.where(lane == idx, m, 0.0), axis=1, keepdims=True)


def _tril(n, strict=False, seg=None):
    r = lax.broadcasted_iota(jnp.int32, (n, n), 0)
    c = lax.broadcasted_iota(jnp.int32, (n, n), 1)
    low = (r > c) if strict else (r >= c)
    if seg is None or seg >= n:
        return low
    shift = seg.bit_length() - 1
    return jnp.logical_and(low, (r >> shift) == (c >> shift))


@functools.partial(jax.custom_vjp, nondiff_argnums=(1,))
def _cumsum_rows(x, seg):
    return _mask_dot(_tril(x.shape[0], seg=seg), x, NN)


def _cumsum_rows_fwd(x, seg):
    return _cumsum_rows(x, seg), None


def _cumsum_rows_bwd(seg, _, ct):
    return (_mask_dot(_tril(ct.shape[0], seg=seg), ct, TN),)


_cumsum_rows.defvjp(_cumsum_rows_fwd, _cumsum_rows_bwd)


def _head_rows(m_t, idx):
    sub = lax.broadcasted_iota(jnp.int32, m_t.shape, 0)
    return jnp.sum(jnp.where(sub == idx, m_t, 0.0), axis=0, keepdims=True)


def _params(sem):
    return pltpu.CompilerParams(dimension_semantics=sem, vmem_limit_bytes=VMEM_LIMIT_BYTES)


def _into_plumbing(outs, first_input):
    arrays, aliases = [], {}
    for k, o in enumerate(outs):
        if len(o) > 4 and not isinstance(o[4], jax.ShapeDtypeStruct):
            aliases[first_input + len(arrays)] = k
            arrays.append(o[4])
    return arrays, aliases


def blockmap(name, fn, grid, ins, outs, accs=(), scalars=None):
    n_in, n_out, n_acc = len(ins), len(outs), len(accs)
    n_grid = len(grid)
    n_pre = 0 if scalars is None else 1
    into_arrays, aliases = _into_plumbing(outs, n_pre + n_in)
    n_into = len(into_arrays)

    def body(*refs):
        refs = refs[n_pre:n_pre + n_in] + refs[n_pre + n_in + n_into:]
        vals = fn(*[r[...] for r in refs[:n_in]])
        if not isinstance(vals, (tuple, list)):
            vals = (vals,)
        for r, v in zip(refs[n_in:n_in + n_out], vals[:n_out]):
            r[...] = v.astype(r.dtype)
        if n_acc:
            first = functools.reduce(jnp.logical_and, [pl.program_id(a) == 0 for a in range(n_grid)])
            acc_refs = refs[n_in + n_out:]

            @pl.when(first)
            def _():
                for r in acc_refs:
                    r[...] = jnp.zeros(r.shape, r.dtype)

            for r, v in zip(acc_refs, vals[n_out:]):
                r[...] += v.astype(r.dtype)

    zeros = lambda nd: (lambda *_: (0,) * nd)
    in_specs = [pl.BlockSpec(b, im) for _, b, im in ins] + [pl.BlockSpec(memory_space=pl.ANY)] * n_into
    out_specs = [pl.BlockSpec(o[2], o[3]) for o in outs] + [pl.BlockSpec(s, zeros(len(s))) for s, _ in accs]
    out_shape = [jax.ShapeDtypeStruct(o[0], o[1]) for o in outs] + [jax.ShapeDtypeStruct(s, d) for s, d in accs]
    cparams = _params(("arbitrary",) * n_grid if n_acc else ("parallel",) * n_grid)
    arrays = [a for a, _, _ in ins] + into_arrays
    if scalars is None:
        return pl.pallas_call(body, name=name, grid=grid, in_specs=in_specs, out_specs=out_specs, out_shape=out_shape,
                              input_output_aliases=aliases, compiler_params=cparams)(*arrays)
    spec = pltpu.PrefetchScalarGridSpec(num_scalar_prefetch=1, grid=grid, in_specs=in_specs, out_specs=out_specs)
    return pl.pallas_call(body, name=name, grid_spec=spec, out_shape=out_shape, input_output_aliases=aliases,
                          compiler_params=cparams)(scalars, *arrays)


def rowmap(name, fn, rows, consts, outs, accs=(), rb=256):
    norm = [(r, r.shape[1], 0) if not isinstance(r, tuple) else (r[0], r[1], r[2] // r[1]) for r in rows]
    assert all(not isinstance(r, tuple) or r[2] % r[1] == 0 for r in rows)
    t = norm[0][0].shape[0]
    rb = min(rb, t)
    ins = [(a, (rb, n), (lambda i, cb=cb: (i, cb))) for a, n, cb in norm]
    ins += [(cst, cst.shape, (lambda i, nd=cst.ndim: (0,) * nd)) for cst in consts]
    o = []
    for out in outs:
        if len(out) == 2:
            o.append(((t, out[0]), out[1], (rb, out[0]), lambda i: (i, 0)))
        else:
            n, d, into, off = out
            assert off % n == 0 and into.dtype == d
            o.append((into.shape, d, (rb, n), (lambda i, cb=off // n: (i, cb)), into))
    return blockmap(name, fn, (t // rb,), ins, o, accs)


MM_TILE_M, MM_TILE_N, MM_TILE_K = 1024, 1024, 2048


def _tile(dim, cap):
    if dim <= cap:
        return dim
    best = max(t for t in range(LANES, cap + 1, LANES) if dim % t == 0)
    return best


def matmul(name, a, b, ta=False, tb=False, out_dtypes=(F32,), epi=None, extras=(), comm=None):
    (k_dim, m_dim) = a.shape if ta else a.shape[::-1]
    n_dim = b.shape[0] if tb else b.shape[1]
    assert (b.shape[1] if tb else b.shape[0]) == k_dim, (name, a.shape, b.shape)
    tm, tn, tk = _tile(m_dim, MM_TILE_M), _tile(n_dim, MM_TILE_N), _tile(k_dim, MM_TILE_K)
    grid = (m_dim // tm, n_dim // tn, k_dim // tk)
    k_steps = grid[2]
    n_extra, n_out = len(extras), len(out_dtypes)
    n_cin, n_cout = (len(comm.operands), len(comm.out_shapes)) if comm else (0, 0)
    dims = (((0 if ta else 1,), (1 if tb else 0,)), ((), ()))

    def body(*refs):
        ins, outs, scratch = refs[:2 + n_extra + n_cin], refs[2 + n_extra + n_cin:][:n_out + n_cout], refs[2 + n_extra + n_cin + n_out + n_cout:]
        extra_refs, out_refs = ins[2:2 + n_extra], outs[:n_out]
        ids = [pl.program_id(ax) for ax in range(3)]
        if comm:
            comm_refs = (ins[2 + n_extra:], outs[n_out:], scratch[-2], scratch[-1])

            @pl.when(functools.reduce(jnp.logical_and, [i == 0 for i in ids]))
            def _():
                comm.start(*comm_refs)

        def finish(acc):
            vals = (acc,) if epi is None else epi(acc, *[r[...] for r in extra_refs])
            if not isinstance(vals, (tuple, list)):
                vals = (vals,)
            for r, v in zip(out_refs, vals):
                r[...] = v.astype(r.dtype)

        prod = lax.dot_general(ins[0][...].astype(BF16), ins[1][...].astype(BF16), dims, preferred_element_type=F32)
        if k_steps == 1:
            finish(prod)
        else:
            acc_ref = scratch[0]

            @pl.when(ids[2] == 0)
            def _():
                acc_ref[...] = jnp.zeros(acc_ref.shape, F32)

            acc_ref[...] += prod

            @pl.when(ids[2] == k_steps - 1)
            def _():
                finish(acc_ref[...])

        if comm:
            @pl.when(functools.reduce(jnp.logical_and, [i == g - 1 for i, g in zip(ids, grid)]))
            def _():
                comm.finish(*comm_refs)

    a_spec = pl.BlockSpec((tk, tm), lambda i, j, k: (k, i)) if ta else pl.BlockSpec((tm, tk), lambda i, j, k: (i, k))
    b_spec = pl.BlockSpec((tn, tk), lambda i, j, k: (j, k)) if tb else pl.BlockSpec((tk, tn), lambda i, j, k: (k, j))
    mn_spec = pl.BlockSpec((tm, tn), lambda i, j, k: (i, j))
    scratch_shapes = [] if k_steps == 1 else [pltpu.VMEM((tm, tn), F32)]
    if comm:
        scratch_shapes += [pltpu.SemaphoreType.DMA(comm.sem_shape), pltpu.SemaphoreType.DMA(comm.sem_shape)]
    res = pl.pallas_call(
        body, name=name, grid=grid,
        in_specs=[a_spec, b_spec] + [mn_spec] * n_extra + [ANY] * n_cin,
        out_specs=[mn_spec] * n_out + [ANY] * n_cout,
        out_shape=[jax.ShapeDtypeStruct((m_dim, n_dim), d) for d in out_dtypes] + (comm.out_shapes if comm else []),
        scratch_shapes=scratch_shapes,
        compiler_params=_params(("arbitrary",) * 3 if comm else ("parallel", "parallel", "arbitrary")),
    )(a, b, *extras, *(comm.operands if comm else []))
    main = res[:n_out] if n_out > 1 else res[0]
    return (main, list(res[n_out:])) if comm else main


def ssd_step(g0, state, tail, zxbc, small, cw, cb, p_dtb, p_alog, p_dsk, nw):
    d_in, gn = state.shape[0] * 2 * LANES, state.shape[0] * LANES
    z = zxbc[:, :d_in]
    act, new_tail = _conv_silu_carried(tail, zxbc[:, d_in:], cw, cb)
    xs, bm, cm = act[:, :d_in], act[:, d_in:d_in + gn], act[:, d_in + gn:]
    hb, n = state.shape[0], xs.shape[0]
    n_pair, n_head = 2 * hb, 4 * hb
    causal = _tril(n)
    dt_all = _softplus(small + p_dtb)
    a_all = dt_all * (-jnp.exp(p_alog))
    acum_all = _cumsum_rows(a_all, n)
    acum_t = acum_all.T
    lane0 = LANE_DT + SSM_HEADS_PER_GROUP * g0
    sub = lax.broadcasted_iota(jnp.int32, acum_t.shape, 0)
    heads = range(n_head)
    acum = jnp.stack([_lane_col(acum_all, lane0 + i) for i in heads])
    acum_row = jnp.stack([jnp.sum(jnp.where(sub == lane0 + i, acum_t, 0.0), axis=0, keepdims=True) for i in heads])
    dt = jnp.stack([_lane_col(dt_all, lane0 + i) for i in heads])
    dsk = jnp.stack([_lane_col(p_dsk, lane0 + i) for i in heads])
    decay = jnp.exp(jnp.where(causal, acum - acum_row, -jnp.inf))
    a_last = acum[:, n - 1:n, :]

    def split(a):
        return [a[:, i * LANES:(i + 1) * LANES] for i in range(a.shape[1] // LANES)]

    def pairs(a, axis=2):
        even = jnp.stack([a[2 * p] for p in range(n_pair)])
        odd = jnp.stack([a[2 * p + 1] for p in range(n_pair)])
        shape = (n_pair, LANES, LANES) if axis == 1 else (n_pair, a.shape[1], LANES)
        return jnp.where(lax.broadcasted_iota(jnp.int32, shape, axis) < SSM_HEAD_DIM, even, odd)

    bms, cms = split(bm), split(cm)
    cb = _bdot(jnp.stack(cms), jnp.stack(bms), BNT)
    cbd = jnp.stack([cb[i // SSM_HEADS_PER_GROUP] for i in heads]) * decay
    xp = jnp.stack(split(xs))
    xdt = xp * pairs(dt)
    yd = _bdot(cbd, jnp.stack([xdt[i // 2] for i in heads]), BNN)
    lane = lax.broadcasted_iota(jnp.int32, (n_pair, n, LANES), 2)
    y_diag = jnp.where(lane < SSM_HEAD_DIM, jnp.stack([yd[2 * p] for p in range(n_pair)]), jnp.stack([yd[2 * p + 1] for p in range(n_pair)]))
    st = state.reshape(n_pair, LANES, LANES)
    cm2 = jnp.stack([cms[p // 2] for p in range(n_pair)])
    bm2 = jnp.stack([bms[p // 2] for p in range(n_pair)])
    y_off = _bdot(cm2, st, BNT) * pairs(jnp.exp(acum))
    new = st * pairs(jnp.exp(a_last), axis=1) + _bdot(xdt * pairs(jnp.exp(a_last - acum)), bm2, BTN)
    y = y_diag + y_off + pairs(dsk) * xp
    y = jnp.concatenate([y[p] for p in range(n_pair)], axis=1) * _silu(z)
    wide = 2 * LANES
    y = jnp.concatenate([_rms(y[:, i * wide:(i + 1) * wide], nw[i]) for i in range(hb)], axis=1)
    return new.reshape(state.shape), new_tail, y


@functools.partial(jax.custom_vjp, nondiff_argnums=(1,))
def _unit_lower_inverse(a, seg):
    n = a.shape[-1]
    r = lax.broadcasted_iota(jnp.int32, (n, n), 0)
    c = lax.broadcasted_iota(jnp.int32, (n, n), 1)
    shift = min(INVERSE_BASE, seg).bit_length() - 1
    power = jnp.where((r >> shift) == (c >> shift), a, 0.0)
    inv = (r == c).astype(F32) - power
    span = 2
    while span < (1 << shift):
        power = _hdot(power, power, BNN)
        inv = inv + _hdot(inv, power, BNN)
        span *= 2
    while (1 << shift) < seg:
        below = jnp.logical_and((r >> (shift + 1)) == (c >> (shift + 1)), (r >> shift) != (c >> shift))
        inv = inv - _hdot(inv, _hdot(jnp.where(below, a, 0.0), inv, BNN), BNN)
        shift += 1
    return inv


def _unit_lower_inverse_fwd(a, seg):
    inv = _unit_lower_inverse(a, seg)
    return inv, inv


def _unit_lower_inverse_bwd(seg, inv, ct):
    return (-_hdot(_hdot(inv, ct, BTN), inv, BNT),)


_unit_lower_inverse.defvjp(_unit_lower_inverse_fwd, _unit_lower_inverse_bwd)


@jax.custom_vjp
def _known_inverse(a, inv):
    return inv


def _known_inverse_fwd(a, inv):
    return inv, inv


def _known_inverse_bwd(inv, ct):
    return _unit_lower_inverse_bwd(None, inv, ct)[0], jnp.zeros_like(inv)


_known_inverse.defvjp(_known_inverse_fwd, _known_inverse_bwd)


@functools.partial(jax.custom_vjp, nondiff_argnums=(1,))
def _rotate_rows(x, k):
    return x if k == 0 else pltpu.roll(x, k % x.shape[0], 0)


def _rotate_rows_fwd(x, k):
    return _rotate_rows(x, k), None


def _rotate_rows_bwd(k, _, ct):
    return (_rotate_rows(ct, -k),)


_rotate_rows.defvjp(_rotate_rows_fwd, _rotate_rows_bwd)


def _conv_silu_carried(tail, x, cw, cb=0.0):
    n = x.shape[0]
    ext = jnp.concatenate([tail, x], axis=0)
    pre = cb + sum(cw[j:j + 1, :] * _rotate_rows(ext, CONV_K - 1 - j)[8:] for j in range(CONV_K))
    return _silu(pre), x[n - 8:]


def _l2norm(x):
    return x * lax.rsqrt(jnp.sum(x * x, axis=-1, keepdims=True) + EPS)


def gdn_step(hq0, state, tail, qkvz, small, cw, p_dtb, p_alog, nw, keep=False, kept=None):
    n, chunk = qkvz.shape[0], GDN_CHUNK
    hb = state.shape[0]
    nb = 2 * hb
    cur = state.reshape(nb, LANES, LANES)
    conv_cols = 4 * hb * LANES
    act, new_tail = _conv_silu_carried(tail, qkvz[:, :conv_cols], cw)
    q, k, v = act[:, :hb * LANES], act[:, hb * LANES:2 * hb * LANES], act[:, 2 * hb * LANES:]
    z = qkvz[:, conv_cols:]
    causal, strict = _tril(n, seg=chunk), _tril(n, True, seg=chunk)
    beta_all = _sigmoid(small)
    g_all = -jnp.exp(p_alog) * _softplus(small + p_dtb)
    gcum_all = _cumsum_rows(g_all, chunk)
    gcum_t = gcum_all.T
    split = lambda a: [a[:, i * LANES:(i + 1) * LANES] for i in range(a.shape[1] // LANES)]
    per_value_head = lambda a: jnp.stack([a[i // 2] for i in range(nb)])
    qh, kh = _l2norm(jnp.stack(split(q))) * (GDN_HEAD ** -0.5), _l2norm(jnp.stack(split(k)))
    q2, k2 = per_value_head(qh), per_value_head(kh)
    v2, z2 = jnp.stack(split(v)), jnp.stack(split(z))
    gcum = jnp.stack([_lane_col(gcum_all, LANE_A + 2 * hq0 + i) for i in range(nb)])
    gcum_row = jnp.stack([_head_rows(gcum_t, LANE_A + 2 * hq0 + i) for i in range(nb)])
    beta = jnp.stack([_lane_col(beta_all, LANE_B + 2 * hq0 + i) for i in range(nb)])
    dmat = jnp.exp(jnp.where(causal, gcum - gcum_row, -jnp.inf))
    a_low = jnp.where(strict, beta * per_value_head(_bdot(kh, kh, BNT)) * dmat, 0.0)
    inv = _unit_lower_inverse(a_low, chunk) if kept is None else _known_inverse(a_low, kept)
    egc = jnp.exp(gcum)
    u = _hdot(inv, v2 * beta, BNN)
    w = _hdot(inv, k2 * (beta * egc), BNN)
    q_dec = q2 * egc
    v_new, o_state = [], []
    for s in range(n // chunk):
        rows = slice(s * chunk, (s + 1) * chunk)
        v_new.append(u[:, rows] - _bdot(w[:, rows], cur, BNN))
        o_state.append(_bdot(q_dec[:, rows], cur, BNN))
        g_last = gcum[:, (s + 1) * chunk - 1:(s + 1) * chunk, :]
        k_dec = k2[:, rows] * jnp.exp(g_last - gcum[:, rows])
        cur = cur * jnp.exp(g_last) + _bdot(k_dec, v_new[-1], BTN)
    o = jnp.concatenate(o_state, axis=1) + _bdot(per_value_head(_bdot(qh, kh, BNT)) * dmat, jnp.concatenate(v_new, axis=1), BNN)
    out = _rms(o, nw) * _silu(z2)
    res = (cur.reshape(state.shape), new_tail, jnp.concatenate([out[i] for i in range(nb)], axis=1))
    return res + (inv,) if keep else res


STATE_SHAPE = (2, LANES, LANES)
GDN_CHUNKS_PER_STEP = 2
INVERSE_BASE = 16


def _scan_specs(rows, consts, chunk, chunk_of, hb):
    specs = []
    for _, n, off, per_group in rows:
        if per_group:
            assert off % (n * hb) == 0
            specs.append(pl.BlockSpec((chunk, n * hb), lambda c, g, cb=off // (n * hb): (chunk_of(c), cb + g)))
        else:
            assert off % n == 0
            specs.append(pl.BlockSpec((chunk, n), lambda c, g, cb=off // n: (chunk_of(c), cb)))
    for arr, per_group in consts:
        if per_group:
            specs.append(pl.BlockSpec((hb, 1, arr.shape[2]), lambda c, g: (g, 0, 0)))
        else:
            specs.append(pl.BlockSpec(arr.shape, lambda c, g, nd=arr.ndim: (0,) * nd))
    return specs


def scan_fwd(name, step, chunk, n_grp, rows, consts, out_cols, hb, keep=None, state_shape=None, tail_cols=None):
    t = rows[0][0].shape[0]
    nc = t // chunk
    n_rows, n_consts = len(rows), len(consts)
    state_shape = state_shape or (hb,) + STATE_SHAPE
    carried = [state_shape] + ([(8, tail_cols)] if tail_cols else [])
    n_car = len(carried)

    def body(*refs):
        row_refs, const_refs = refs[:n_rows], refs[n_rows:n_rows + n_consts]
        y_ref = refs[n_rows + n_consts]
        saved_refs = refs[n_rows + n_consts + 1:n_rows + n_consts + 1 + n_car]
        scratch = refs[-n_car:]
        c, g = pl.program_id(0), pl.program_id(1)

        @pl.when(c == 0)
        def _():
            for s, shape in zip(scratch, carried):
                s[g] = jnp.zeros(shape, F32)

        cur = [s[g] for s in scratch]
        for r, v in zip(saved_refs, cur):
            r[...] = v
        vals = [r[...] for r in row_refs] + [r[...] for r in const_refs]
        res = step(g * hb, *cur, *vals) if keep is None else step(g * hb, *cur, *vals, keep=True)
        for s, v in zip(scratch, res[:n_car]):
            s[g] = v
        y_ref[...] = res[n_car].astype(y_ref.dtype)
        if keep is not None:
            refs[n_rows + n_consts + 1 + n_car][...] = res[n_car + 1]

    lead = (nc, n_grp // hb)
    out_specs = [pl.BlockSpec((chunk, out_cols * hb), lambda c, g: (c, g))]
    out_shape = [jax.ShapeDtypeStruct((t, n_grp * out_cols), BF16)]
    for shape in carried + ([keep] if keep is not None else []):
        out_specs.append(pl.BlockSpec((None, None) + shape, lambda c, g, nd=len(shape): (c, g) + (0,) * nd))
        out_shape.append(jax.ShapeDtypeStruct(lead + shape, F32))
    return pl.pallas_call(
        body, name=name, grid=lead,
        in_specs=_scan_specs(rows, consts, chunk, lambda c: c, hb),
        out_specs=out_specs, out_shape=out_shape,
        scratch_shapes=[pltpu.VMEM((n_grp // hb,) + shape, F32) for shape in carried],
        compiler_params=_params(("arbitrary", "arbitrary")),
    )(*[r[0] for r in rows], *[c[0] for c in consts])


def scan_bwd(name, step, chunk, n_grp, rows, consts, saved, dy, row_dtypes, hb, into, kept=None):
    t = rows[0][0].shape[0]
    nc = t // chunk
    n_rows, n_consts, n_car = len(rows), len(consts), len(saved)
    carried = [s.shape[2:] for s in saved]
    out_cols = dy.shape[1] // n_grp
    n_alias = sum(not isinstance(v, jax.ShapeDtypeStruct) for v in into.values())
    n_kept = 0 if kept is None else 1
    n_in = n_rows + n_consts + n_car + 1 + n_kept + n_alias

    def body(*refs):
        row_refs, const_refs = refs[:n_rows], refs[n_rows:n_rows + n_consts]
        saved_refs = refs[n_rows + n_consts:n_rows + n_consts + n_car]
        dy_ref = refs[n_rows + n_consts + n_car]
        outs = refs[n_in:-n_car]
        scratch = refs[-n_car:]
        c, g = pl.program_id(0), pl.program_id(1)

        @pl.when(c == 0)
        def _():
            for s, shape in zip(scratch, carried):
                s[g] = jnp.zeros(shape, F32)

        @pl.when(jnp.logical_and(c == 0, g == 0))
        def _():
            for r in outs[n_rows:]:
                r[...] = jnp.zeros(r.shape, r.dtype)

        f = functools.partial(step, g * hb) if kept is None else functools.partial(step, g * hb, kept=refs[n_rows + n_consts + n_car + 1][...])
        _, vjp = jax.vjp(f, *[r[...] for r in saved_refs], *[r[...] for r in row_refs], *[r[...] for r in const_refs])
        grads = vjp(tuple(s[g] for s in scratch) + (dy_ref[...].astype(F32),))
        for s, d in zip(scratch, grads[:n_car]):
            s[g] = d
        for (_, _, _, per_group), r, d in zip(rows, outs[:n_rows], grads[n_car:n_car + n_rows]):
            if per_group:
                r[...] = d.astype(r.dtype)
            else:
                @pl.when(g == 0)
                def _(r=r):
                    r[...] = jnp.zeros(r.shape, r.dtype)

                r[...] += d.astype(r.dtype)
        for (_, per_group), r, d in zip(consts, outs[n_rows:], grads[n_car + n_rows:]):
            if per_group:
                r[pl.ds(g * hb, hb)] += d
            else:
                r[...] += d

    rev = lambda c: nc - 1 - c
    out_specs, out_shape = [], []
    into_arrays, aliases = [], {}
    first_into = n_in - n_alias
    kept_arrays = [] if kept is None else [kept]
    by_step = lambda a: pl.BlockSpec((None, None) + a.shape[2:], lambda c, g, nd=a.ndim - 2: (rev(c), g) + (0,) * nd)
    for k, ((_, n, off, per_group), dt) in enumerate(zip(rows, row_dtypes)):
        if k in into:
            assert per_group and off % (n * hb) == 0 and into[k].dtype == dt
            out_specs.append(pl.BlockSpec((chunk, n * hb), lambda c, g, cb=off // (n * hb): (rev(c), cb + g)))
            out_shape.append(jax.ShapeDtypeStruct(into[k].shape, dt))
            if not isinstance(into[k], jax.ShapeDtypeStruct):
                aliases[first_into + len(into_arrays)] = k
                into_arrays.append(into[k])
        elif per_group:
            out_specs.append(pl.BlockSpec((chunk, n * hb), lambda c, g: (rev(c), g)))
            out_shape.append(jax.ShapeDtypeStruct((t, n_grp * n), dt))
        else:
            out_specs.append(pl.BlockSpec((chunk, n), lambda c, g: (rev(c), 0)))
            out_shape.append(jax.ShapeDtypeStruct((t, n), dt))
    for arr, _ in consts:
        out_specs.append(pl.BlockSpec(arr.shape, lambda c, g, nd=arr.ndim: (0,) * nd))
        out_shape.append(jax.ShapeDtypeStruct(arr.shape, F32))
    return pl.pallas_call(
        body, name=name, grid=(nc, n_grp // hb),
        in_specs=_scan_specs(rows, consts, chunk, rev, hb) + [by_step(s) for s in saved]
        + [pl.BlockSpec((chunk, out_cols * hb), lambda c, g: (rev(c), g))] + [by_step(k) for k in kept_arrays]
        + [pl.BlockSpec(memory_space=pl.ANY)] * len(into_arrays),
        out_specs=out_specs, out_shape=out_shape, input_output_aliases=aliases,
        scratch_shapes=[pltpu.VMEM((n_grp // hb,) + shape, F32) for shape in carried],
        compiler_params=_params(("arbitrary", "arbitrary")),
    )(*[r[0] for r in rows], *[c[0] for c in consts], *saved, dy, *kept_arrays, *into_arrays)


def _place():
    return lax.axis_index("x"), lax.axis_index("y"), lax.axis_index("c")


def _other_chips(x, y):
    return [(1 - x, y), (x, 1 - y), (1 - x, 1 - y)]


ANY = pl.BlockSpec(memory_space=pl.ANY)


def all_gather8(name, v):
    m_per, n = v.shape

    def body(x_ref, out_ref, send_sems, recv_sems, local_sem):
        x, y, c = _place()
        me, sibling = (x, y, c), (x, y, 1 - c)
        chips = _other_chips(x, y)

        def rows(px, py, pc):
            return out_ref.at[pl.ds((4 * px + 2 * py + pc) * m_per, m_per), :]

        def copy(k, block, to, src=None):
            return pltpu.make_async_remote_copy(
                src_ref=rows(*block) if src is None else src, dst_ref=rows(*block),
                send_sem=send_sems.at[k], recv_sem=recv_sems.at[k], device_id=to, device_id_type=MESH)

        mine = pltpu.make_async_copy(x_ref, rows(*me), local_sem)
        mine.start()
        first = [copy(0, me, sibling, src=x_ref)]
        first += [copy(1 + q, me, (*chip, c), src=x_ref) for q, chip in enumerate(chips)]
        for cp in first:
            cp.start()
        passed = [copy(4 + q, (*chip, c), sibling) for q, chip in enumerate(chips)]
        for q, chip in enumerate(chips):
            copy(1 + q, (*chip, c), me).wait_recv()
            passed[q].start()
        copy(0, sibling, me).wait_recv()
        for q, chip in enumerate(chips):
            copy(4 + q, (*chip, 1 - c), me).wait_recv()
        for cp in first + passed:
            cp.wait_send()
        mine.wait()

    return pl.pallas_call(
        body, name=name, out_shape=jax.ShapeDtypeStruct((N_DEV * m_per, n), v.dtype),
        in_specs=[pl.BlockSpec(memory_space=pltpu.VMEM)], out_specs=pl.BlockSpec(memory_space=pltpu.VMEM),
        scratch_shapes=[pltpu.SemaphoreType.DMA((7,)), pltpu.SemaphoreType.DMA((7,)), pltpu.SemaphoreType.DMA],
    )(v)


def gather_flat(name, vec):
    n = vec.shape[0]
    n_pad = -(-n // (8 * LANES)) * (8 * LANES)
    v = jnp.pad(vec, (0, n_pad - n)).reshape(8, n_pad // 8)
    return all_gather8(name, v).reshape(N_DEV, n_pad)[:, :n]


class Exchange:
    def __init__(self, operands, out_shapes, sem_shape, start, finish):
        self.operands, self.out_shapes, self.sem_shape, self.start, self.finish = list(operands), out_shapes, sem_shape, start, finish


def _start_all_wait_all(make_copies):
    def start(*refs):
        for cp in make_copies(*refs):
            cp.start()

    def finish(*refs):
        for cp in make_copies(*refs):
            cp.wait()

    return start, finish


def run_exchange(name, ex):
    n_in, n_out = len(ex.operands), len(ex.out_shapes)

    def body(*refs):
        ins, outs = refs[:n_in], refs[n_in:n_in + n_out]
        ex.start(ins, outs, *refs[n_in + n_out:])
        ex.finish(ins, outs, *refs[n_in + n_out:])

    return pl.pallas_call(
        body, name=name, out_shape=ex.out_shapes, in_specs=[ANY] * n_in, out_specs=[ANY] * n_out,
        scratch_shapes=[pltpu.SemaphoreType.DMA(ex.sem_shape), pltpu.SemaphoreType.DMA(ex.sem_shape)],
    )(*ex.operands)


def _half(shape, axis, pc):
    h = shape[axis] // 2
    return (pl.ds(pc * h, h), slice(None)) if axis == 0 else (slice(None), pl.ds(pc * h, h))


def _half_shape(shape, axis):
    return tuple(s // 2 if a == axis else s for a, s in enumerate(shape))


def all_gather_shards(shards, axes):
    n_t = len(shards)
    n_sem = 12

    def copies(ins, outs, send_sems, recv_sems):
        x, y, c = _place()
        me, sibling, x_nbr, y_nbr = (x, y, c), (x, y, 1 - c), (1 - x, y, c), (x, 1 - y, c)
        own, of_x, of_y, of_diag = 2 * x + y, 2 * (1 - x) + y, 2 * x + 1 - y, 2 * (1 - x) + 1 - y

        def copy(t, k, quarter, pc, piece, to, from_input=False):
            axis = axes[t]
            h = ins[t].shape[axis] // 4
            cut = pl.ds((2 * pc + piece) * h, h)
            part = (cut, slice(None)) if axis == 0 else (slice(None), cut)
            dst = outs[t].at[(quarter,) + part]
            return pltpu.make_async_remote_copy(
                src_ref=ins[t].at[part] if from_input else dst, dst_ref=dst,
                send_sem=send_sems.at[t, k], recv_sem=recv_sems.at[t, k], device_id=to, device_id_type=MESH)

        stages = []
        for t in range(n_t):
            direct = [copy(t, 0, own, c, 0, x_nbr, True), copy(t, 2, own, c, 1, y_nbr, True),
                      copy(t, 1, own, c, 1, x_nbr, True), copy(t, 3, own, c, 0, y_nbr, True)]
            landing = [
                (copy(t, 0, of_x, c, 0, me), [copy(t, 4, of_x, c, 0, y_nbr), copy(t, 6, of_x, c, 0, sibling)]),
                (copy(t, 2, of_y, c, 1, me), [copy(t, 5, of_y, c, 1, x_nbr), copy(t, 8, of_y, c, 1, sibling)]),
                (copy(t, 1, of_x, c, 1, me), [copy(t, 7, of_x, c, 1, sibling)]),
                (copy(t, 3, of_y, c, 0, me), [copy(t, 9, of_y, c, 0, sibling)]),
                (copy(t, 4, of_diag, c, 0, me), [copy(t, 10, of_diag, c, 0, sibling)]),
                (copy(t, 5, of_diag, c, 1, me), [copy(t, 11, of_diag, c, 1, sibling)])]
            from_sibling = [copy(t, 6, of_x, 1 - c, 0, me), copy(t, 8, of_y, 1 - c, 1, me), copy(t, 7, of_x, 1 - c, 1, me),
                            copy(t, 9, of_y, 1 - c, 0, me), copy(t, 10, of_diag, 1 - c, 0, me), copy(t, 11, of_diag, 1 - c, 1, me)]
            stages.append((direct, landing, from_sibling))
        return stages

    def start(*refs):
        for direct, _, _ in copies(*refs):
            for cp in direct:
                cp.start()

    def finish(*refs):
        stages = copies(*refs)
        for _, landing, _ in stages:
            for arrived, onward in landing:
                arrived.wait_recv()
                for cp in onward:
                    cp.start()
        for direct, landing, from_sibling in stages:
            for cp in from_sibling:
                cp.wait_recv()
            for cp in direct + [cp for _, onward in landing for cp in onward]:
                cp.wait_send()

    return Exchange(shards, [jax.ShapeDtypeStruct((4,) + s.shape, s.dtype) for s in shards], (n_t, n_sem), start, finish)


def exchange_halves_d2d(grads, axes):
    n_t = len(grads)

    def copies(ins, outs, send_sems, recv_sems):
        x, y, c = _place()
        return [pltpu.make_async_remote_copy(
            src_ref=ins[t].at[(slice(None),) + _half(ins[t].shape[1:], axes[t], 1 - c)], dst_ref=outs[t],
            send_sem=send_sems.at[t], recv_sem=recv_sems.at[t], device_id=(x, y, 1 - c), device_id_type=MESH) for t in range(n_t)]

    shapes = [jax.ShapeDtypeStruct((4,) + _half_shape(g.shape[1:], a), g.dtype) for g, a in zip(grads, axes)]
    return Exchange(grads, shapes, (n_t,), *_start_all_wait_all(copies))


def exchange_quarters_ici(parts):
    n_t = len(parts)

    def copies(ins, outs, send_sems, recv_sems):
        x, y, c = _place()
        return [pltpu.make_async_remote_copy(
            src_ref=ins[t].at[2 * px + py], dst_ref=outs[t].at[q],
            send_sem=send_sems.at[t, q], recv_sem=recv_sems.at[t, q], device_id=(px, py, c), device_id_type=MESH)
            for t in range(n_t) for q, (px, py) in enumerate(_other_chips(x, y))]

    shapes = [jax.ShapeDtypeStruct((3,) + p.shape[1:], p.dtype) for p in parts]
    return Exchange(parts, shapes, (n_t, 3), *_start_all_wait_all(copies))


def swap_d2d(halves):
    n_t = len(halves)

    def copies(ins, outs, send_sems, recv_sems):
        x, y, c = _place()
        return [pltpu.make_async_remote_copy(
            src_ref=ins[t], dst_ref=outs[t], send_sem=send_sems.at[t], recv_sem=recv_sems.at[t],
            device_id=(x, y, 1 - c), device_id_type=MESH) for t in range(n_t)]

    return Exchange(halves, [jax.ShapeDtypeStruct(h.shape, h.dtype) for h in halves], (n_t,), *_start_all_wait_all(copies))


BLOCK_BYTES = 1 << 20


def _row_block(r, c):
    fits = [rb for rb in range(16, r + 1, 16) if r % rb == 0 and rb * c * 4 <= BLOCK_BYTES]
    return max(fits) if fits else r


def _place_scalars():
    x, y, c = _place()
    return jnp.stack([c, 2 * x + y]).astype(jnp.int32)


def reduce_on_chip(tag, grads, axes):
    from_sibling = run_exchange(f"rs_d2d_{tag}", exchange_halves_d2d(grads, axes))
    parts, parts_bf16 = [], []
    for t, (g, s, axis) in enumerate(zip(grads, from_sibling, axes)):
        _, h, cols = s.shape
        rb = _row_block(h, cols)
        nb = h // rb
        blk = lambda k, i, s_ref: (k, i, 0)
        mine = (lambda k, i, s_ref, nb=nb: (k, s_ref[0] * nb + i, 0)) if axis == 0 else (lambda k, i, s_ref: (k, i, s_ref[0]))
        p32, p16 = blockmap(
            f"rs_add_{tag}{t}", lambda a, b: (a + b, a + b), (4, nb),
            [(g, (None, rb, cols), mine), (s, (None, rb, cols), blk)],
            [(s.shape, F32, (None, rb, cols), blk), (s.shape, BF16, (None, rb, cols), blk)], scalars=_place_scalars())
        parts.append(p32)
        parts_bf16.append(p16)
    return parts, parts_bf16


def reduce_across_chips(parts, from_chips):
    halves = []
    for t, (p, q) in enumerate(zip(parts, from_chips)):
        _, h, cols = p.shape
        rb = _row_block(h, cols)
        halves.append(blockmap(
            f"rs_sum{t}", lambda a, b: a + b[0].astype(F32) + b[1].astype(F32) + b[2].astype(F32), (h // rb,),
            [(p, (None, rb, cols), lambda i, s_ref: (s_ref[1], i, 0)), (q, (3, rb, cols), lambda i, s_ref: (0, i, 0))],
            [((h, cols), F32, (rb, cols), lambda i, s_ref: (i, 0))], scalars=_place_scalars())[0])
    return list(zip(halves, run_exchange("rs_swap", swap_d2d(halves))))


def _adamw(w, g, m, v):
    m = ADAM_B1 * m + (1.0 - ADAM_B1) * g
    v = ADAM_B2 * v + (1.0 - ADAM_B2) * jnp.square(g)
    m_hat = m / (1.0 - ADAM_B1 ** ADAM_STEP)
    v_hat = v / (1.0 - ADAM_B2 ** ADAM_STEP)
    delta = -ADAM_LR * (m_hat / (jnp.sqrt(v_hat) + ADAM_EPS) + ADAM_WD * w)
    return delta, m, v


def adamw(name, w, g, m, v):
    _, r, c = w.shape
    rb = _row_block(r, c)
    blk3 = lambda a: (a, (None, rb, c), lambda i: (0, i, 0))
    return blockmap(name, _adamw, (r // rb,), [blk3(w), (g, (rb, c), lambda i: (i, 0)), blk3(m), blk3(v)],
                    [(w.shape, F32, (None, rb, c), lambda i: (0, i, 0))] * 3)


def adamw_halves(name, w, mine, other, m, v, axis):
    _, r, c = w.shape
    h, c = mine.shape
    rb = _row_block(h, c)
    nb = h // rb

    def body(s_ref, w_ref, mine_ref, other_ref, m_ref, v_ref, g_out, d_out, m_out, v_out):
        g = jnp.where(pl.program_id(0) == s_ref[0], mine_ref[...], other_ref[...])
        d, nm, nv = _adamw(w_ref[...], g, m_ref[...], v_ref[...])
        g_out[...], d_out[...], m_out[...], v_out[...] = g, d, nm, nv

    spec3 = pl.BlockSpec((None, rb, c), (lambda k, i, s_ref: (0, k * nb + i, 0)) if axis == 0 else (lambda k, i, s_ref: (0, i, k)))
    spec2 = pl.BlockSpec((rb, c), lambda k, i, s_ref: (i, 0))
    grid_spec = pltpu.PrefetchScalarGridSpec(num_scalar_prefetch=1, grid=(2, nb), in_specs=[spec3, spec2, spec2, spec3, spec3],
                                             out_specs=[spec3] * 4)
    return pl.pallas_call(body, name=name, grid_spec=grid_spec, out_shape=[jax.ShapeDtypeStruct(w.shape, F32)] * 4,
                          compiler_params=_params(("parallel", "parallel")))(_place_scalars(), w, mine, other, m, v)


def _whole(name, fn, ins, outs):
    return blockmap(name, fn, (1,), [(a, a.shape, lambda i, nd=a.ndim: (0,) * nd) for a in ins],
                    [(s, d, s, lambda i, nd=len(s): (0,) * nd) for s, d in outs])


def _premix(x, w, sc, sh):
    return _rms(x, w) * (1.0 + sc) + sh


def _postmix(x, u, w_post, g1, w_pre2, sc2, sh2):
    x1 = x + g1 * _rms(u, w_post)
    return x1, _premix(x1, w_pre2, sc2, sh2)


def _merge(gs, gg, ys, yg):
    return _sigmoid(gs) * ys + _sigmoid(gg) * yg


def _final(x1, y2, w_post2, g2):
    return x1 + g2 * _rms(y2, w_post2)


def kernel(x, c, w_ada, b_ada, norm_mix_pre, norm_mix_post, w_in, ssm_conv_w, ssm_conv_b, ssm_dt_bias, ssm_A_log, ssm_D, ssm_norm_w, gdn_conv_w, gdn_dt_bias, gdn_A_log, gdn_norm_w, w_ssm_up, w_gdn_up, w_out, norm_mlp_pre, norm_mlp_post, w_mlp_up, w_mlp_down, loss_target, m_w_ada, m_b_ada, m_norm_mix_pre, m_norm_mix_post, m_w_in, m_ssm_conv_w, m_ssm_conv_b, m_ssm_dt_bias, m_ssm_A_log, m_ssm_D, m_ssm_norm_w, m_gdn_conv_w, m_gdn_dt_bias, m_gdn_A_log, m_gdn_norm_w, m_w_ssm_up, m_w_gdn_up, m_w_out, m_norm_mlp_pre, m_norm_mlp_post, m_w_mlp_up, m_w_mlp_down, v_w_ada, v_b_ada, v_norm_mix_pre, v_norm_mix_post, v_w_in, v_ssm_conv_w, v_ssm_conv_b, v_ssm_dt_bias, v_ssm_A_log, v_ssm_D, v_ssm_norm_w, v_gdn_conv_w, v_gdn_dt_bias, v_gdn_A_log, v_gdn_norm_w, v_w_ssm_up, v_w_gdn_up, v_w_out, v_norm_mlp_pre, v_norm_mlp_post, v_w_mlp_up, v_w_mlp_down):
    args = dict(locals())
    xi, yi, ci = _place()
    quarter = 2 * xi + yi
    batch = 4 * xi + 2 * yi + ci

    xt, target = x[0], loss_target[0]
    t, d = xt.shape
    hs, hv = ssm_dt_bias.shape[-1], gdn_dt_bias.shape[-1]
    d_inner = hs * SSM_HEAD_DIM
    n_grp = hs // SSM_HEADS_PER_GROUP
    gn = n_grp * SSM_D_STATE
    conv_ssm = d_inner + 2 * gn
    hq = hv // 2
    key, val = hq * GDN_HEAD, hv * GDN_HEAD
    conv_gdn = 2 * key + val
    o_dt = d_inner + conv_ssm
    o_qkv = o_dt + hs
    o_b = o_qkv + conv_gdn + val
    o_a = o_b + hv
    o_gs = o_a + hv
    n_proj = o_gs + 2 * d
    a_z, a_q = 0, o_dt
    a_gs = a_q + conv_gdn + val
    a_gg = a_gs + d
    a_small = a_gg + d
    n_al = -(-(a_small + LANES) // MM_TILE_N) * MM_TILE_N

    def to_aligned(w):
        z = lambda n: jnp.zeros((n, w.shape[1]), w.dtype)
        return jnp.concatenate([
            w[:o_dt], w[o_qkv:o_b], w[o_gs:],
            w[o_dt:o_qkv], z(LANE_B - hs), w[o_b:o_a], z(LANE_A - LANE_B - hv), w[o_a:o_gs], z(LANES - LANE_A - hv),
            z(n_al - a_small - LANES)], axis=0)

    def from_aligned(w):
        s = a_small
        return jnp.concatenate([
            w[:o_dt], w[s + LANE_DT:s + LANE_DT + hs], w[a_q:a_gs], w[s + LANE_B:s + LANE_B + hv],
            w[s + LANE_A:s + LANE_A + hv], w[a_gs:a_small]], axis=0)

    def lanes(vec, at):
        return jnp.zeros((1, LANES), F32).at[:, at:at + vec.shape[-1]].set(vec.reshape(1, -1))

    n_cw = CONV_K * ssm_conv_w.shape[-1]
    small_in = gather_flat("ag_small", jnp.concatenate([c.reshape(-1), ssm_conv_w.reshape(-1), gdn_conv_w.reshape(-1)]))
    c_all = small_in[:, :d]
    by_chip = small_in[0::2]

    def whole_conv_w(lo):
        return jnp.transpose(by_chip[:, lo:lo + n_cw].reshape(4, CONV_K, -1), (1, 0, 2)).reshape(CONV_K, -1)

    cw_ssm, cw_gdn = whole_conv_w(d), whole_conv_w(d + n_cw)
    cb_ssm = ssm_conv_b

    n_ada = w_ada.shape[-1]
    b_q = lax.dynamic_slice_in_dim(b_ada, quarter * n_ada, n_ada, axis=1)
    mod_q = _whole("ada_fwd", lambda ca, w, b: _bdot(_silu(ca), w) + b, [c_all, w_ada[0], b_q], [((N_DEV, n_ada), F32)])[0]
    mod_all = gather_flat("ag_mod", mod_q.reshape(-1)).reshape(N_DEV, N_DEV, n_ada)[0::2]
    mod = lax.dynamic_index_in_dim(mod_all, batch, axis=1, keepdims=False).reshape(1, 4 * n_ada)
    sh1, sc1, g1, sh2, sc2, g2 = [mod[:, i * d:(i + 1) * d] for i in range(6)]

    transposed = lambda a: jnp.swapaxes(a, 1, 2)
    own = [w.astype(BF16) for w in (transposed(w_in)[0], w_ssm_up[0], w_gdn_up[0], w_out[0], w_mlp_up[0], w_mlp_down[0])]
    with_own = lambda gs, ws: [lax.dynamic_update_index_in_dim(g, w, quarter, 0) for g, w in zip(gs, ws)]
    cols_major = lambda g: jnp.transpose(g, (1, 0, 2)).reshape(g.shape[1], -1)
    rows_major = lambda g: g.reshape(-1, g.shape[2])
    wb_in = to_aligned(rows_major(with_own(run_exchange("ag_w_in", all_gather_shards(own[:1], [1])), own[:1])[0]))

    h1 = rowmap("premix", _premix, [xt], [norm_mix_pre, sc1, sh1], [(d, BF16)])[0]
    proj, gathered = matmul("in_proj", h1, wb_in, tb=True, comm=all_gather_shards(own[1:], [0] * 5))
    gathered = with_own(gathered, own[1:])
    wb_ssm_up, wb_gdn_up, wb_out = rows_major(gathered[0]), rows_major(gathered[1]), rows_major(gathered[2])
    wb_up, wb_down = cols_major(gathered[3]), rows_major(gathered[4])

    wide = 2 * LANES
    ssd_rows = [(proj, d_inner + conv_ssm, a_z, True), (proj, LANES, a_small, False)]
    ssd_consts = [(cw_ssm, False), (cb_ssm, False), (lanes(ssm_dt_bias, LANE_DT), False), (lanes(ssm_A_log, LANE_DT), False),
                  (lanes(ssm_D, LANE_DT), False), (ssm_norm_w.reshape(n_grp, 1, wide), False)]
    y_ssm_n, st_ssm, tails_ssm = scan_fwd("ssd_fwd", ssd_step, SSM_CHUNK, 1, ssd_rows, ssd_consts, d_inner, 1,
                                          state_shape=(n_grp,) + STATE_SHAPE, tail_cols=conv_ssm)
    gdn_rows = [(proj, conv_gdn + val, a_q, True), (proj, LANES, a_small, False)]
    gdn_consts = [(cw_gdn, False), (lanes(gdn_dt_bias, LANE_A), False), (lanes(gdn_A_log, LANE_A), False), (gdn_norm_w, False)]
    gdn_rows_per_step = GDN_CHUNK * GDN_CHUNKS_PER_STEP
    y_gdn_n, st_gdn, tails_gdn, inv_gdn = scan_fwd(
        "gdn_fwd", gdn_step, gdn_rows_per_step, 1, gdn_rows, gdn_consts, val, 1, keep=(hv, gdn_rows_per_step, gdn_rows_per_step),
        state_shape=(hq,) + STATE_SHAPE, tail_cols=conv_gdn)

    y_ssm = matmul("ssm_up", y_ssm_n, wb_ssm_up)
    y_gdn = matmul("gdn_up", y_gdn_n, wb_gdn_up)
    gates = [(proj, d, a_gs), (proj, d, a_gg)]
    merged = rowmap("merge", _merge, gates + [y_ssm, y_gdn], [], [(d, BF16)])[0]
    u = matmul("w_out", merged, wb_out)
    post_consts = [norm_mix_post, g1, norm_mlp_pre, sc2, sh2]
    x1, h2 = rowmap("postmix", _postmix, [xt, u], post_consts, [(d, F32), (d, BF16)])
    relu2 = lambda acc: (acc, jnp.square(jnp.maximum(acc, 0.0)))
    a_up, act = matmul("mlp_up", h2, wb_up, out_dtypes=(BF16, BF16), epi=relu2)
    y2 = matmul("mlp_down", act, wb_down)

    def final_bwd(x1_, y2_, tgt, w_, g_):
        x2, vjp = jax.vjp(_final, x1_, y2_, w_, g_)
        err = x2 - tgt
        loss = 0.5 * jnp.sum(jnp.mean(err * err, axis=-1, keepdims=True), axis=0, keepdims=True)
        dx1, dy2, dw, dg = vjp(err / d)
        return dx1, dy2, loss, dw, dg

    dx1, dy2, loss_part, d_norm_mlp_post, dg2 = rowmap(
        "final", final_bwd, [x1, y2, target], [norm_mlp_post, g2], [(d, F32), (d, BF16)], [((1, 1), F32), ((1, d), F32), ((1, d), F32)])

    d_a = matmul("mlp_down_dx", dy2, wb_down, tb=True, out_dtypes=(BF16,), extras=[a_up],
                 epi=lambda acc, a: acc * 2.0 * jnp.maximum(a.astype(F32), 0.0))
    gw_down = matmul("mlp_down_dw", act, dy2, ta=True)
    dh2 = matmul("mlp_up_dx", d_a, wb_up, tb=True)
    gw_up = matmul("mlp_up_dw", h2, d_a, ta=True)

    def postmix_bwd(x_, u_, dx1_, dh2_, *cs):
        _, vjp = jax.vjp(_postmix, x_, u_, *cs)
        return vjp((dx1_, dh2_))

    dxa, du, d_norm_mix_post, dg1, d_norm_mlp_pre, dsc2, dsh2 = rowmap(
        "postmix_bwd", postmix_bwd, [xt, u, dx1, dh2], post_consts, [(d, F32), (d, BF16)], [((1, d), F32)] * 5)
    d_merged = matmul("w_out_dx", du, wb_out, tb=True)
    gw_out = matmul("w_out_dw", merged, du, ta=True)

    def merge_bwd(gs, gg, ys, yg, dm):
        _, vjp = jax.vjp(_merge, gs, gg, ys, yg)
        dgs, dgg, dys, dyg = vjp(dm)
        return dys, dyg, jnp.concatenate([dgs, dgg], axis=1)

    dy_ssm, dy_gdn, dproj = rowmap("merge_bwd", merge_bwd, gates + [y_ssm, y_gdn, d_merged], [],
                                   [(d, BF16), (d, BF16), (2 * d, BF16, jax.ShapeDtypeStruct((t, n_al), BF16), a_gs)])
    dy_ssm_n = matmul("ssm_up_dx", dy_ssm, wb_ssm_up, tb=True, out_dtypes=(BF16,))
    gw_ssm_up = matmul("ssm_up_dw", y_ssm_n, dy_ssm, ta=True)
    dy_gdn_n = matmul("gdn_up_dx", dy_gdn, wb_gdn_up, tb=True, out_dtypes=(BF16,))
    gw_gdn_up = matmul("gdn_up_dw", y_gdn_n, dy_gdn, ta=True)

    dproj, dsmall_ssm, dcw_ssm, dcb_ssm, d_sdtb, d_salog, d_sdsk, d_snw = scan_bwd(
        "ssd_bwd", ssd_step, SSM_CHUNK, 1, ssd_rows, ssd_consts, [st_ssm, tails_ssm], dy_ssm_n, [BF16, F32], 1, {0: dproj})
    dproj, dsmall_gdn, dcw_gdn, d_gdtb, d_galog, d_gnw = scan_bwd(
        "gdn_bwd", gdn_step, gdn_rows_per_step, 1, gdn_rows, gdn_consts, [st_gdn, tails_gdn], dy_gdn_n, [BF16, F32], 1,
        {0: dproj}, kept=inv_gdn)
    tail = n_al - a_small
    dproj = rowmap("small_sum", lambda a, b: jnp.concatenate([a + b, jnp.zeros((a.shape[0], tail - LANES), F32)], axis=1),
                   [dsmall_ssm, dsmall_gdn], [], [(tail, BF16, dproj, a_small)])[0]
    quarters_cols = lambda g: jnp.transpose(g.reshape(g.shape[0], 4, -1), (1, 0, 2))
    quarters_rows = lambda g: g.reshape(4, g.shape[0] // 4, g.shape[1])
    rest32, rest16 = reduce_on_chip("rest", [quarters_rows(gw_ssm_up), quarters_rows(gw_gdn_up), quarters_rows(gw_out),
                                             quarters_cols(gw_up), quarters_rows(gw_down)], [0] * 5)
    gw_in_al, rest_chips = matmul("in_proj_dw", dproj, h1, ta=True, comm=exchange_quarters_ici(rest16))
    in32, in16 = reduce_on_chip("in", [quarters_rows(from_aligned(gw_in_al))], [1])
    dh1, in_chips = matmul("in_proj_dx", dproj, wb_in, comm=exchange_quarters_ici(in16))

    def premix_bwd(x_, dxa_, dh1_, w_, sc_, sh_):
        _, vjp = jax.vjp(_premix, x_, w_, sc_, sh_)
        dx, dw, dsc, dsh = vjp(dh1_)
        return dx + dxa_, dw, dsc, dsh

    grad_x, d_norm_mix_pre, dsc1, dsh1 = rowmap(
        "premix_bwd", premix_bwd, [xt, dxa, dh1], [norm_mix_pre, sc1, sh1], [(d, F32)], [((1, d), F32)] * 3)

    dmod_all = gather_flat("ag_dmod", jnp.concatenate([dsh1, dsc1, dg1, dsh2, dsc2, dg2], axis=1).reshape(-1))
    dmod_q = lax.dynamic_slice_in_dim(dmod_all, quarter * n_ada, n_ada, axis=1)
    gw_ada, gb_ada = _whole(
        "ada_bwd", lambda ca, dq_, da_: (_bdot(_silu(ca), dq_, TN), jnp.sum(da_, axis=0, keepdims=True)),
        [c_all, dmod_q, dmod_all], [((d, n_ada), F32), ((1, 4 * n_ada), F32)])

    partial = [d_norm_mix_pre, d_norm_mix_post, dcw_ssm, dcb_ssm, d_sdtb[:, LANE_DT:LANE_DT + hs], d_salog[:, LANE_DT:LANE_DT + hs],
               d_sdsk[:, LANE_DT:LANE_DT + hs], d_snw, dcw_gdn, d_gdtb[:, LANE_A:LANE_A + hv], d_galog[:, LANE_A:LANE_A + hv], d_gnw,
               d_norm_mlp_pre, d_norm_mlp_post, loss_part]
    sizes = [p.size for p in partial]
    stacked = gather_flat("ag_grads", jnp.concatenate([p.reshape(-1) for p in partial]))
    summed = _whole("small_sum8", lambda s: jnp.sum(s, axis=0, keepdims=True), [stacked], [((1, stacked.shape[1]), F32)])[0][0]
    offs = [0]
    for s in sizes:
        offs.append(offs[-1] + s)
    red = [summed[offs[i]:offs[i + 1]] for i in range(len(sizes))]
    loss = red[-1][0]
    my_cols = lambda full: lax.dynamic_slice_in_dim(full.reshape(CONV_K, -1), quarter * (n_cw // CONV_K), n_cw // CONV_K, axis=1)
    small_grads = {
        "b_ada": gb_ada, "norm_mix_pre": red[0], "norm_mix_post": red[1], "ssm_conv_w": my_cols(red[2]), "ssm_conv_b": red[3],
        "ssm_dt_bias": red[4], "ssm_A_log": red[5], "ssm_D": red[6], "ssm_norm_w": red[7], "gdn_conv_w": my_cols(red[8]),
        "gdn_dt_bias": red[9], "gdn_A_log": red[10], "gdn_norm_w": red[11], "norm_mlp_pre": red[12], "norm_mlp_post": red[13]}

    big_names = ["w_in", "w_ssm_up", "w_gdn_up", "w_out", "w_mlp_up", "w_mlp_down"]
    big_grads = dict(zip(big_names, reduce_across_chips(in32 + rest32, in_chips + rest_chips)))

    names = ['w_ada', 'b_ada', 'norm_mix_pre', 'norm_mix_post', 'w_in', 'ssm_conv_w', 'ssm_conv_b', 'ssm_dt_bias', 'ssm_A_log', 'ssm_D',
             'ssm_norm_w', 'gdn_conv_w', 'gdn_dt_bias', 'gdn_A_log', 'gdn_norm_w', 'w_ssm_up', 'w_gdn_up', 'w_out', 'norm_mlp_pre',
             'norm_mlp_post', 'w_mlp_up', 'w_mlp_down']
    grad, delta, new_m, new_v = {}, {}, {}, {}
    for n, (mine, other) in big_grads.items():
        view, axis = (transposed, 1) if n == "w_in" else ((lambda a: a), 0)
        res = adamw_halves("adamw_" + n, view(args[n]), mine, other, view(args["m_" + n]), view(args["v_" + n]), axis)
        grad[n], delta[n], new_m[n], new_v[n] = [view(a) for a in res]
    grad["w_ada"] = gw_ada.reshape(w_ada.shape)
    delta["w_ada"], new_m["w_ada"], new_v["w_ada"] = adamw("adamw_w_ada", w_ada, gw_ada, m_w_ada, v_w_ada)
    small_names = [n for n in names if n not in grad]
    flat = lambda pre: jnp.concatenate([args[pre + n].reshape(-1) for n in small_names]).reshape(1, 1, -1)
    g_flat = jnp.concatenate([small_grads[n].reshape(-1) for n in small_names]).reshape(1, -1)
    dl, nm, nv = adamw("adamw_small", flat(""), g_flat, flat("m_"), flat("v_"))
    off = 0
    for n in small_names:
        shape = args[n].shape
        size = args[n].size
        grad[n], delta[n], new_m[n], new_v[n] = [a.reshape(-1)[off:off + size].reshape(shape) for a in (g_flat, dl, nm, nv)]
        off += size

    return (loss, grad_x.reshape(x.shape), *[grad[n] for n in names], *[delta[n] for n in names],
            *[new_m[n] for n in names], *[new_v[n] for n in names])
```

```python
import functools

import jax
import jax.numpy as jnp
from jax import lax
from jax.experimental import pallas as pl
from jax.experimental.pallas import tpu as pltpu

F32 = jnp.float32
BF16 = jnp.bfloat16
MESH = pl.DeviceIdType.MESH

EPS = 1e-6
SSM_HEAD_DIM = 64
SSM_HEADS_PER_GROUP = 4
SSM_D_STATE = 128
SSM_CHUNK = 128
GDN_HEAD = 128
GDN_CHUNK = 64
CONV_K = 4
LANE_DT, LANE_B, LANE_A = 0, 32, 48
ADAM_LR, ADAM_B1, ADAM_B2, ADAM_EPS, ADAM_WD, ADAM_STEP = 0.001, 0.9, 0.999, 1e-08, 0.01, 10

VMEM_LIMIT_BYTES = 56 * 1024 * 1024
LANES = 128
N_DEV = 8

NN = (((1,), (0,)), ((), ()))
NT = (((1,), (1,)), ((), ()))
TN = (((0,), (0,)), ((), ()))


BNN = (((2,), (1,)), ((0,), (0,)))
BNT = (((2,), (2,)), ((0,), (0,)))
BTN = (((1,), (1,)), ((0,), (0,)))
_KIND = {NN: ("NN", 0), NT: ("NT", 0), TN: ("TN", 0), BNN: ("NN", 1), BNT: ("NT", 1), BTN: ("TN", 1)}
_DIMS = {"NN": (NN, BNN), "NT": (NT, BNT), "TN": (TN, BTN)}


def _dg(a, b, dims):
    return lax.dot_general(a, b, dims, preferred_element_type=F32)


def _raw_bf16(a, b, dims):
    return _dg(a.astype(BF16), b.astype(BF16), dims)


def _raw_bf16x3(a, b, dims):
    ah, bh = a.astype(BF16), b.astype(BF16)
    al, bl = (a - ah.astype(F32)).astype(BF16), (b - bh.astype(F32)).astype(BF16)
    return _dg(ah, bh, dims) + (_dg(ah, bl, dims) + _dg(al, bh, dims))


def _make_dot(raw):
    @functools.partial(jax.custom_vjp, nondiff_argnums=(2,))
    def dot(a, b, dims):
        return raw(a, b, dims)

    def fwd(a, b, dims):
        return raw(a, b, dims), (a, b)

    def bwd(dims, res, ct):
        a, b = res
        kind, batched = _KIND[dims]
        d = lambda k: _DIMS[k][batched]
        if kind == "NN":
            da, db = raw(ct, b, d("NT")), raw(a, ct, d("TN"))
        elif kind == "NT":
            da, db = raw(ct, b, d("NN")), raw(ct, a, d("TN"))
        else:
            da, db = raw(b, ct, d("NT")), raw(a, ct, d("NN"))
        return da.astype(a.dtype), db.astype(b.dtype)

    dot.defvjp(fwd, bwd)
    return lambda a, b, dims=NN: dot(a, b, dims)


_bdot = _make_dot(_raw_bf16)
_hdot = _make_dot(_raw_bf16x3)


def _mask_dot(mask, x, dims):
    m = mask.astype(BF16)
    hi = x.astype(BF16)
    r = x - hi.astype(F32)
    mid = r.astype(BF16)
    lo = (r - mid.astype(F32)).astype(BF16)
    return sum(_dg(m, p, dims) for p in (hi, mid, lo))


def _sigmoid(x):
    return 0.5 * jnp.tanh(0.5 * x) + 0.5


def _silu(x):
    return x * _sigmoid(x)


def _softplus(x):
    return jnp.maximum(x, 0.0) + jnp.log(1.0 + jnp.exp(-jnp.abs(x)))


def _rms(x, w):
    return x * lax.rsqrt(jnp.mean(x * x, axis=-1, keepdims=True) + EPS) * w


def _lane_col(m, idx):
    lane = lax.broadcasted_iota(jnp.int32, m.shape, 1)
    return jnp.sum(jnp.where(lane == idx, m, 0.0), axis=1, keepdims=True)


def _tril(n, strict=False, seg=None):
    r = lax.broadcasted_iota(jnp.int32, (n, n), 0)
    c = lax.broadcasted_iota(jnp.int32, (n, n), 1)
    low = (r > c) if strict else (r >= c)
    if seg is None or seg >= n:
        return low
    shift = seg.bit_length() - 1
    return jnp.logical_and(low, (r >> shift) == (c >> shift))


@functools.partial(jax.custom_vjp, nondiff_argnums=(1,))
def _cumsum_rows(x, seg):
    return _mask_dot(_tril(x.shape[0], seg=seg), x, NN)


def _cumsum_rows_fwd(x, seg):
    return _cumsum_rows(x, seg), None


def _cumsum_rows_bwd(seg, _, ct):
    return (_mask_dot(_tril(ct.shape[0], seg=seg), ct, TN),)


_cumsum_rows.defvjp(_cumsum_rows_fwd, _cumsum_rows_bwd)


def _head_rows(m_t, idx):
    sub = lax.broadcasted_iota(jnp.int32, m_t.shape, 0)
    return jnp.sum(jnp.where(sub == idx, m_t, 0.0), axis=0, keepdims=True)


def _params(sem):
    return pltpu.CompilerParams(dimension_semantics=sem, vmem_limit_bytes=VMEM_LIMIT_BYTES)


def _into_plumbing(outs, first_input):
    arrays, aliases = [], {}
    for k, o in enumerate(outs):
        if len(o) > 4 and not isinstance(o[4], jax.ShapeDtypeStruct):
            aliases[first_input + len(arrays)] = k
            arrays.append(o[4])
    return arrays, aliases


def blockmap(name, fn, grid, ins, outs, accs=(), scalars=None):
    n_in, n_out, n_acc = len(ins), len(outs), len(accs)
    n_grid = len(grid)
    n_pre = 0 if scalars is None else 1
    into_arrays, aliases = _into_plumbing(outs, n_pre + n_in)
    n_into = len(into_arrays)

    def body(*refs):
        refs = refs[n_pre:n_pre + n_in] + refs[n_pre + n_in + n_into:]
        vals = fn(*[r[...] for r in refs[:n_in]])
        if not isinstance(vals, (tuple, list)):
            vals = (vals,)
        for r, v in zip(refs[n_in:n_in + n_out], vals[:n_out]):
            r[...] = v.astype(r.dtype)
        if n_acc:
            first = functools.reduce(jnp.logical_and, [pl.program_id(a) == 0 for a in range(n_grid)])
            acc_refs = refs[n_in + n_out:]

            @pl.when(first)
            def _():
                for r in acc_refs:
                    r[...] = jnp.zeros(r.shape, r.dtype)

            for r, v in zip(acc_refs, vals[n_out:]):
                r[...] += v.astype(r.dtype)

    zeros = lambda nd: (lambda *_: (0,) * nd)
    in_specs = [pl.BlockSpec(b, im) for _, b, im in ins] + [pl.BlockSpec(memory_space=pl.ANY)] * n_into
    out_specs = [pl.BlockSpec(o[2], o[3]) for o in outs] + [pl.BlockSpec(s, zeros(len(s))) for s, _ in accs]
    out_shape = [jax.ShapeDtypeStruct(o[0], o[1]) for o in outs] + [jax.ShapeDtypeStruct(s, d) for s, d in accs]
    cparams = _params(("arbitrary",) * n_grid if n_acc else ("parallel",) * n_grid)
    arrays = [a for a, _, _ in ins] + into_arrays
    if scalars is None:
        return pl.pallas_call(body, name=name, grid=grid, in_specs=in_specs, out_specs=out_specs, out_shape=out_shape,
                              input_output_aliases=aliases, compiler_params=cparams)(*arrays)
    spec = pltpu.PrefetchScalarGridSpec(num_scalar_prefetch=1, grid=grid, in_specs=in_specs, out_specs=out_specs)
    return pl.pallas_call(body, name=name, grid_spec=spec, out_shape=out_shape, input_output_aliases=aliases,
                          compiler_params=cparams)(scalars, *arrays)


def rowmap(name, fn, rows, consts, outs, accs=(), rb=256):
    norm = [(r, r.shape[1], 0) if not isinstance(r, tuple) else (r[0], r[1], r[2] // r[1]) for r in rows]
    assert all(not isinstance(r, tuple) or r[2] % r[1] == 0 for r in rows)
    t = norm[0][0].shape[0]
    rb = min(rb, t)
    ins = [(a, (rb, n), (lambda i, cb=cb: (i, cb))) for a, n, cb in norm]
    ins += [(cst, cst.shape, (lambda i, nd=cst.ndim: (0,) * nd)) for cst in consts]
    o = []
    for out in outs:
        if len(out) == 2:
            o.append(((t, out[0]), out[1], (rb, out[0]), lambda i: (i, 0)))
        else:
            n, d, into, off = out
            assert off % n == 0 and into.dtype == d
            o.append((into.shape, d, (rb, n), (lambda i, cb=off // n: (i, cb)), into))
    return blockmap(name, fn, (t // rb,), ins, o, accs)


MM_TILE_M, MM_TILE_N, MM_TILE_K = 1024, 1024, 2048


def _tile(dim, cap):
    if dim <= cap:
        return dim
    best = max(t for t in range(LANES, cap + 1, LANES) if dim % t == 0)
    return best


def matmul(name, a, b, ta=False, tb=False, out_dtypes=(F32,), epi=None, extras=(), comm=None):
    (k_dim, m_dim) = a.shape if ta else a.shape[::-1]
    n_dim = b.shape[0] if tb else b.shape[1]
    assert (b.shape[1] if tb else b.shape[0]) == k_dim, (name, a.shape, b.shape)
    tm, tn, tk = _tile(m_dim, MM_TILE_M), _tile(n_dim, MM_TILE_N), _tile(k_dim, MM_TILE_K)
    grid = (m_dim // tm, n_dim // tn, k_dim // tk)
    k_steps = grid[2]
    n_extra, n_out = len(extras), len(out_dtypes)
    n_cin, n_cout = (len(comm.operands), len(comm.out_shapes)) if comm else (0, 0)
    dims = (((0 if ta else 1,), (1 if tb else 0,)), ((), ()))

    def body(*refs):
        ins, outs, scratch = refs[:2 + n_extra + n_cin], refs[2 + n_extra + n_cin:][:n_out + n_cout], refs[2 + n_extra + n_cin + n_out + n_cout:]
        extra_refs, out_refs = ins[2:2 + n_extra], outs[:n_out]
        ids = [pl.program_id(ax) for ax in range(3)]
        if comm:
            comm_refs = (ins[2 + n_extra:], outs[n_out:], scratch[-2], scratch[-1])

            step = (ids[0] * grid[1] + ids[1]) * grid[2] + ids[2]

            @pl.when(step == 0)
            def _():
                comm.start(*comm_refs)

            @pl.when(step == (grid[0] * grid[1] * grid[2]) // 2)
            def _():
                comm.middle(*comm_refs)

        def finish(acc):
            vals = (acc,) if epi is None else epi(acc, *[r[...] for r in extra_refs])
            if not isinstance(vals, (tuple, list)):
                vals = (vals,)
            for r, v in zip(out_refs, vals):
                r[...] = v.astype(r.dtype)

        prod = lax.dot_general(ins[0][...].astype(BF16), ins[1][...].astype(BF16), dims, preferred_element_type=F32)
        if k_steps == 1:
            finish(prod)
        else:
            acc_ref = scratch[0]

            @pl.when(ids[2] == 0)
            def _():
                acc_ref[...] = jnp.zeros(acc_ref.shape, F32)

            acc_ref[...] += prod

            @pl.when(ids[2] == k_steps - 1)
            def _():
                finish(acc_ref[...])

        if comm:
            @pl.when(functools.reduce(jnp.logical_and, [i == g - 1 for i, g in zip(ids, grid)]))
            def _():
                comm.finish(*comm_refs)

    a_spec = pl.BlockSpec((tk, tm), lambda i, j, k: (k, i)) if ta else pl.BlockSpec((tm, tk), lambda i, j, k: (i, k))
    b_spec = pl.BlockSpec((tn, tk), lambda i, j, k: (j, k)) if tb else pl.BlockSpec((tk, tn), lambda i, j, k: (k, j))
    mn_spec = pl.BlockSpec((tm, tn), lambda i, j, k: (i, j))
    scratch_shapes = [] if k_steps == 1 else [pltpu.VMEM((tm, tn), F32)]
    if comm:
        scratch_shapes += [pltpu.SemaphoreType.DMA(comm.sem_shape), pltpu.SemaphoreType.DMA(comm.sem_shape)]
    res = pl.pallas_call(
        body, name=name, grid=grid,
        in_specs=[a_spec, b_spec] + [mn_spec] * n_extra + [ANY] * n_cin,
        out_specs=[mn_spec] * n_out + [ANY] * n_cout,
        out_shape=[jax.ShapeDtypeStruct((m_dim, n_dim), d) for d in out_dtypes] + (comm.out_shapes if comm else []),
        scratch_shapes=scratch_shapes,
        compiler_params=_params(("arbitrary",) * 3 if comm else ("parallel", "parallel", "arbitrary")),
    )(a, b, *extras, *(comm.operands if comm else []))
    main = res[:n_out] if n_out > 1 else res[0]
    return (main, list(res[n_out:])) if comm else main


def ssd_step(g0, state, tail, zxbc, small, cw, cb, p_dtb, p_alog, p_dsk, nw):
    d_in, gn = state.shape[0] * 2 * LANES, state.shape[0] * LANES
    z = zxbc[:, :d_in]
    act, new_tail = _conv_silu_carried(tail, zxbc[:, d_in:], cw, cb)
    xs, bm, cm = act[:, :d_in], act[:, d_in:d_in + gn], act[:, d_in + gn:]
    hb, n = state.shape[0], xs.shape[0]
    n_pair, n_head = 2 * hb, 4 * hb
    causal = _tril(n)
    dt_all = _softplus(small + p_dtb)
    a_all = dt_all * (-jnp.exp(p_alog))
    acum_all = _cumsum_rows(a_all, n)
    acum_t = acum_all.T
    lane0 = LANE_DT + SSM_HEADS_PER_GROUP * g0
    sub = lax.broadcasted_iota(jnp.int32, acum_t.shape, 0)
    heads = range(n_head)
    acum = jnp.stack([_lane_col(acum_all, lane0 + i) for i in heads])
    acum_row = jnp.stack([jnp.sum(jnp.where(sub == lane0 + i, acum_t, 0.0), axis=0, keepdims=True) for i in heads])
    dt = jnp.stack([_lane_col(dt_all, lane0 + i) for i in heads])
    dsk = jnp.stack([_lane_col(p_dsk, lane0 + i) for i in heads])
    decay = jnp.exp(jnp.where(causal, acum - acum_row, -jnp.inf))
    a_last = acum[:, n - 1:n, :]

    def split(a):
        return [a[:, i * LANES:(i + 1) * LANES] for i in range(a.shape[1] // LANES)]

    def pairs(a, axis=2):
        even = jnp.stack([a[2 * p] for p in range(n_pair)])
        odd = jnp.stack([a[2 * p + 1] for p in range(n_pair)])
        shape = (n_pair, LANES, LANES) if axis == 1 else (n_pair, a.shape[1], LANES)
        return jnp.where(lax.broadcasted_iota(jnp.int32, shape, axis) < SSM_HEAD_DIM, even, odd)

    bms, cms = split(bm), split(cm)
    cb = _bdot(jnp.stack(cms), jnp.stack(bms), BNT)
    cbd = jnp.stack([cb[i // SSM_HEADS_PER_GROUP] for i in heads]) * decay
    xp = jnp.stack(split(xs))
    xdt = xp * pairs(dt)
    yd = _bdot(cbd, jnp.stack([xdt[i // 2] for i in heads]), BNN)
    lane = lax.broadcasted_iota(jnp.int32, (n_pair, n, LANES), 2)
    y_diag = jnp.where(lane < SSM_HEAD_DIM, jnp.stack([yd[2 * p] for p in range(n_pair)]), jnp.stack([yd[2 * p + 1] for p in range(n_pair)]))
    st = state.reshape(n_pair, LANES, LANES)
    cm2 = jnp.stack([cms[p // 2] for p in range(n_pair)])
    bm2 = jnp.stack([bms[p // 2] for p in range(n_pair)])
    y_off = _bdot(cm2, st, BNT) * pairs(jnp.exp(acum))
    new = st * pairs(jnp.exp(a_last), axis=1) + _bdot(xdt * pairs(jnp.exp(a_last - acum)), bm2, BTN)
    y = y_diag + y_off + pairs(dsk) * xp
    y = jnp.concatenate([y[p] for p in range(n_pair)], axis=1) * _silu(z)
    wide = 2 * LANES
    y = jnp.concatenate([_rms(y[:, i * wide:(i + 1) * wide], nw[i]) for i in range(hb)], axis=1)
    return new.reshape(state.shape), new_tail, y


@functools.partial(jax.custom_vjp, nondiff_argnums=(1,))
def _unit_lower_inverse(a, seg):
    n = a.shape[-1]
    r = lax.broadcasted_iota(jnp.int32, (n, n), 0)
    c = lax.broadcasted_iota(jnp.int32, (n, n), 1)
    shift = min(INVERSE_BASE, seg).bit_length() - 1
    power = jnp.where((r >> shift) == (c >> shift), a, 0.0)
    inv = (r == c).astype(F32) - power
    span = 2
    while span < (1 << shift):
        power = _hdot(power, power, BNN)
        inv = inv + _hdot(inv, power, BNN)
        span *= 2
    while (1 << shift) < seg:
        below = jnp.logical_and((r >> (shift + 1)) == (c >> (shift + 1)), (r >> shift) != (c >> shift))
        inv = inv - _hdot(inv, _hdot(jnp.where(below, a, 0.0), inv, BNN), BNN)
        shift += 1
    return inv


def _unit_lower_inverse_fwd(a, seg):
    inv = _unit_lower_inverse(a, seg)
    return inv, inv


def _unit_lower_inverse_bwd(seg, inv, ct):
    return (-_hdot(_hdot(inv, ct, BTN), inv, BNT),)


_unit_lower_inverse.defvjp(_unit_lower_inverse_fwd, _unit_lower_inverse_bwd)


@jax.custom_vjp
def _known_inverse(a, inv):
    return inv


def _known_inverse_fwd(a, inv):
    return inv, inv


def _known_inverse_bwd(inv, ct):
    return _unit_lower_inverse_bwd(None, inv, ct)[0], jnp.zeros_like(inv)


_known_inverse.defvjp(_known_inverse_fwd, _known_inverse_bwd)


@functools.partial(jax.custom_vjp, nondiff_argnums=(1,))
def _rotate_rows(x, k):
    return x if k == 0 else pltpu.roll(x, k % x.shape[0], 0)


def _rotate_rows_fwd(x, k):
    return _rotate_rows(x, k), None


def _rotate_rows_bwd(k, _, ct):
    return (_rotate_rows(ct, -k),)


_rotate_rows.defvjp(_rotate_rows_fwd, _rotate_rows_bwd)


def _conv_silu_carried(tail, x, cw, cb=0.0):
    n = x.shape[0]
    ext = jnp.concatenate([tail, x], axis=0)
    pre = cb + sum(cw[j:j + 1, :] * _rotate_rows(ext, CONV_K - 1 - j)[8:] for j in range(CONV_K))
    return _silu(pre), x[n - 8:]


def _l2norm(x):
    return x * lax.rsqrt(jnp.sum(x * x, axis=-1, keepdims=True) + EPS)


def gdn_step(hq0, state, tail, qkvz, small, cw, p_dtb, p_alog, nw, keep=False, kept=None):
    n, chunk = qkvz.shape[0], GDN_CHUNK
    hb = state.shape[0]
    nb = 2 * hb
    cur = state.reshape(nb, LANES, LANES)
    conv_cols = 4 * hb * LANES
    act, new_tail = _conv_silu_carried(tail, qkvz[:, :conv_cols], cw)
    q, k, v = act[:, :hb * LANES], act[:, hb * LANES:2 * hb * LANES], act[:, 2 * hb * LANES:]
    z = qkvz[:, conv_cols:]
    causal, strict = _tril(n, seg=chunk), _tril(n, True, seg=chunk)
    beta_all = _sigmoid(small)
    g_all = -jnp.exp(p_alog) * _softplus(small + p_dtb)
    gcum_all = _cumsum_rows(g_all, chunk)
    gcum_t = gcum_all.T
    split = lambda a: [a[:, i * LANES:(i + 1) * LANES] for i in range(a.shape[1] // LANES)]
    per_value_head = lambda a: jnp.stack([a[i // 2] for i in range(nb)])
    qh, kh = _l2norm(jnp.stack(split(q))) * (GDN_HEAD ** -0.5), _l2norm(jnp.stack(split(k)))
    q2, k2 = per_value_head(qh), per_value_head(kh)
    v2, z2 = jnp.stack(split(v)), jnp.stack(split(z))
    gcum = jnp.stack([_lane_col(gcum_all, LANE_A + 2 * hq0 + i) for i in range(nb)])
    gcum_row = jnp.stack([_head_rows(gcum_t, LANE_A + 2 * hq0 + i) for i in range(nb)])
    beta = jnp.stack([_lane_col(beta_all, LANE_B + 2 * hq0 + i) for i in range(nb)])
    dmat = jnp.exp(jnp.where(causal, gcum - gcum_row, -jnp.inf))
    a_low = jnp.where(strict, beta * per_value_head(_bdot(kh, kh, BNT)) * dmat, 0.0)
    inv = _unit_lower_inverse(a_low, chunk) if kept is None else _known_inverse(a_low, kept)
    egc = jnp.exp(gcum)
    u = _hdot(inv, v2 * beta, BNN)
    w = _hdot(inv, k2 * (beta * egc), BNN)
    q_dec = q2 * egc
    v_new, o_state = [], []
    for s in range(n // chunk):
        rows = slice(s * chunk, (s + 1) * chunk)
        v_new.append(u[:, rows] - _bdot(w[:, rows], cur, BNN))
        o_state.append(_bdot(q_dec[:, rows], cur, BNN))
        g_last = gcum[:, (s + 1) * chunk - 1:(s + 1) * chunk, :]
        k_dec = k2[:, rows] * jnp.exp(g_last - gcum[:, rows])
        cur = cur * jnp.exp(g_last) + _bdot(k_dec, v_new[-1], BTN)
    o = jnp.concatenate(o_state, axis=1) + _bdot(per_value_head(_bdot(qh, kh, BNT)) * dmat, jnp.concatenate(v_new, axis=1), BNN)
    out = _rms(o, nw) * _silu(z2)
    res = (cur.reshape(state.shape), new_tail, jnp.concatenate([out[i] for i in range(nb)], axis=1))
    return res + (inv,) if keep else res


STATE_SHAPE = (2, LANES, LANES)
GDN_CHUNKS_PER_STEP = 2
INVERSE_BASE = 16


def _scan_specs(rows, consts, chunk, chunk_of, hb):
    specs = []
    for _, n, off, per_group in rows:
        if per_group:
            assert off % (n * hb) == 0
            specs.append(pl.BlockSpec((chunk, n * hb), lambda c, g, cb=off // (n * hb): (chunk_of(c), cb + g)))
        else:
            assert off % n == 0
            specs.append(pl.BlockSpec((chunk, n), lambda c, g, cb=off // n: (chunk_of(c), cb)))
    for arr, per_group in consts:
        if per_group:
            specs.append(pl.BlockSpec((hb, 1, arr.shape[2]), lambda c, g: (g, 0, 0)))
        else:
            specs.append(pl.BlockSpec(arr.shape, lambda c, g, nd=arr.ndim: (0,) * nd))
    return specs


def scan_fwd(name, step, chunk, n_grp, rows, consts, out_cols, hb, keep=None, state_shape=None, tail_cols=None):
    t = rows[0][0].shape[0]
    nc = t // chunk
    n_rows, n_consts = len(rows), len(consts)
    state_shape = state_shape or (hb,) + STATE_SHAPE
    carried = [state_shape] + ([(8, tail_cols)] if tail_cols else [])
    n_car = len(carried)

    def body(*refs):
        row_refs, const_refs = refs[:n_rows], refs[n_rows:n_rows + n_consts]
        y_ref = refs[n_rows + n_consts]
        saved_refs = refs[n_rows + n_consts + 1:n_rows + n_consts + 1 + n_car]
        scratch = refs[-n_car:]
        c, g = pl.program_id(0), pl.program_id(1)

        @pl.when(c == 0)
        def _():
            for s, shape in zip(scratch, carried):
                s[g] = jnp.zeros(shape, F32)

        cur = [s[g] for s in scratch]
        for r, v in zip(saved_refs, cur):
            r[...] = v
        vals = [r[...] for r in row_refs] + [r[...] for r in const_refs]
        res = step(g * hb, *cur, *vals) if keep is None else step(g * hb, *cur, *vals, keep=True)
        for s, v in zip(scratch, res[:n_car]):
            s[g] = v
        y_ref[...] = res[n_car].astype(y_ref.dtype)
        if keep is not None:
            refs[n_rows + n_consts + 1 + n_car][...] = res[n_car + 1]

    lead = (nc, n_grp // hb)
    out_specs = [pl.BlockSpec((chunk, out_cols * hb), lambda c, g: (c, g))]
    out_shape = [jax.ShapeDtypeStruct((t, n_grp * out_cols), BF16)]
    for shape in carried + ([keep] if keep is not None else []):
        out_specs.append(pl.BlockSpec((None, None) + shape, lambda c, g, nd=len(shape): (c, g) + (0,) * nd))
        out_shape.append(jax.ShapeDtypeStruct(lead + shape, F32))
    return pl.pallas_call(
        body, name=name, grid=lead,
        in_specs=_scan_specs(rows, consts, chunk, lambda c: c, hb),
        out_specs=out_specs, out_shape=out_shape,
        scratch_shapes=[pltpu.VMEM((n_grp // hb,) + shape, F32) for shape in carried],
        compiler_params=_params(("arbitrary", "arbitrary")),
    )(*[r[0] for r in rows], *[c[0] for c in consts])


def scan_bwd(name, step, chunk, n_grp, rows, consts, saved, dy, row_dtypes, hb, into, kept=None):
    t = rows[0][0].shape[0]
    nc = t // chunk
    n_rows, n_consts, n_car = len(rows), len(consts), len(saved)
    carried = [s.shape[2:] for s in saved]
    out_cols = dy.shape[1] // n_grp
    n_alias = sum(not isinstance(v, jax.ShapeDtypeStruct) for v in into.values())
    n_kept = 0 if kept is None else 1
    n_in = n_rows + n_consts + n_car + 1 + n_kept + n_alias

    def body(*refs):
        row_refs, const_refs = refs[:n_rows], refs[n_rows:n_rows + n_consts]
        saved_refs = refs[n_rows + n_consts:n_rows + n_consts + n_car]
        dy_ref = refs[n_rows + n_consts + n_car]
        outs = refs[n_in:-n_car]
        scratch = refs[-n_car:]
        c, g = pl.program_id(0), pl.program_id(1)

        @pl.when(c == 0)
        def _():
            for s, shape in zip(scratch, carried):
                s[g] = jnp.zeros(shape, F32)

        @pl.when(jnp.logical_and(c == 0, g == 0))
        def _():
            for r in outs[n_rows:]:
                r[...] = jnp.zeros(r.shape, r.dtype)

        f = functools.partial(step, g * hb) if kept is None else functools.partial(step, g * hb, kept=refs[n_rows + n_consts + n_car + 1][...])
        _, vjp = jax.vjp(f, *[r[...] for r in saved_refs], *[r[...] for r in row_refs], *[r[...] for r in const_refs])
        grads = vjp(tuple(s[g] for s in scratch) + (dy_ref[...].astype(F32),))
        for s, d in zip(scratch, grads[:n_car]):
            s[g] = d
        for (_, _, _, per_group), r, d in zip(rows, outs[:n_rows], grads[n_car:n_car + n_rows]):
            if per_group:
                r[...] = d.astype(r.dtype)
            else:
                @pl.when(g == 0)
                def _(r=r):
                    r[...] = jnp.zeros(r.shape, r.dtype)

                r[...] += d.astype(r.dtype)
        for (_, per_group), r, d in zip(consts, outs[n_rows:], grads[n_car + n_rows:]):
            if per_group:
                r[pl.ds(g * hb, hb)] += d
            else:
                r[...] += d

    rev = lambda c: nc - 1 - c
    out_specs, out_shape = [], []
    into_arrays, aliases = [], {}
    first_into = n_in - n_alias
    kept_arrays = [] if kept is None else [kept]
    by_step = lambda a: pl.BlockSpec((None, None) + a.shape[2:], lambda c, g, nd=a.ndim - 2: (rev(c), g) + (0,) * nd)
    for k, ((_, n, off, per_group), dt) in enumerate(zip(rows, row_dtypes)):
        if k in into:
            assert per_group and off % (n * hb) == 0 and into[k].dtype == dt
            out_specs.append(pl.BlockSpec((chunk, n * hb), lambda c, g, cb=off // (n * hb): (rev(c), cb + g)))
            out_shape.append(jax.ShapeDtypeStruct(into[k].shape, dt))
            if not isinstance(into[k], jax.ShapeDtypeStruct):
                aliases[first_into + len(into_arrays)] = k
                into_arrays.append(into[k])
        elif per_group:
            out_specs.append(pl.BlockSpec((chunk, n * hb), lambda c, g: (rev(c), g)))
            out_shape.append(jax.ShapeDtypeStruct((t, n_grp * n), dt))
        else:
            out_specs.append(pl.BlockSpec((chunk, n), lambda c, g: (rev(c), 0)))
            out_shape.append(jax.ShapeDtypeStruct((t, n), dt))
    for arr, _ in consts:
        out_specs.append(pl.BlockSpec(arr.shape, lambda c, g, nd=arr.ndim: (0,) * nd))
        out_shape.append(jax.ShapeDtypeStruct(arr.shape, F32))
    return pl.pallas_call(
        body, name=name, grid=(nc, n_grp // hb),
        in_specs=_scan_specs(rows, consts, chunk, rev, hb) + [by_step(s) for s in saved]
        + [pl.BlockSpec((chunk, out_cols * hb), lambda c, g: (rev(c), g))] + [by_step(k) for k in kept_arrays]
        + [pl.BlockSpec(memory_space=pl.ANY)] * len(into_arrays),
        out_specs=out_specs, out_shape=out_shape, input_output_aliases=aliases,
        scratch_shapes=[pltpu.VMEM((n_grp // hb,) + shape, F32) for shape in carried],
        compiler_params=_params(("arbitrary", "arbitrary")),
    )(*[r[0] for r in rows], *[c[0] for c in consts], *saved, dy, *kept_arrays, *into_arrays)


def _place():
    return lax.axis_index("x"), lax.axis_index("y"), lax.axis_index("c")


def _other_chips(x, y):
    return [(1 - x, y), (x, 1 - y), (1 - x, 1 - y)]


ANY = pl.BlockSpec(memory_space=pl.ANY)


def all_gather8(name, v):
    m_per, n = v.shape

    def body(x_ref, out_ref, send_sems, recv_sems, local_sem):
        x, y, c = _place()
        me, sibling = (x, y, c), (x, y, 1 - c)
        chips = _other_chips(x, y)

        def rows(px, py, pc):
            return out_ref.at[pl.ds((4 * px + 2 * py + pc) * m_per, m_per), :]

        def copy(k, block, to, src=None):
            return pltpu.make_async_remote_copy(
                src_ref=rows(*block) if src is None else src, dst_ref=rows(*block),
                send_sem=send_sems.at[k], recv_sem=recv_sems.at[k], device_id=to, device_id_type=MESH)

        mine = pltpu.make_async_copy(x_ref, rows(*me), local_sem)
        mine.start()
        first = [copy(0, me, sibling, src=x_ref)]
        first += [copy(1 + q, me, (*chip, c), src=x_ref) for q, chip in enumerate(chips)]
        for cp in first:
            cp.start()
        passed = [copy(4 + q, (*chip, c), sibling) for q, chip in enumerate(chips)]
        for q, chip in enumerate(chips):
            copy(1 + q, (*chip, c), me).wait_recv()
            passed[q].start()
        copy(0, sibling, me).wait_recv()
        for q, chip in enumerate(chips):
            copy(4 + q, (*chip, 1 - c), me).wait_recv()
        for cp in first + passed:
            cp.wait_send()
        mine.wait()

    return pl.pallas_call(
        body, name=name, out_shape=jax.ShapeDtypeStruct((N_DEV * m_per, n), v.dtype),
        in_specs=[pl.BlockSpec(memory_space=pltpu.VMEM)], out_specs=pl.BlockSpec(memory_space=pltpu.VMEM),
        scratch_shapes=[pltpu.SemaphoreType.DMA((7,)), pltpu.SemaphoreType.DMA((7,)), pltpu.SemaphoreType.DMA],
    )(v)


def gather_flat(name, vec):
    n = vec.shape[0]
    n_pad = -(-n // (8 * LANES)) * (8 * LANES)
    v = jnp.pad(vec, (0, n_pad - n)).reshape(8, n_pad // 8)
    return all_gather8(name, v).reshape(N_DEV, n_pad)[:, :n]


class Exchange:
    def __init__(self, operands, out_shapes, sem_shape, start, finish, middle=None):
        self.operands, self.out_shapes, self.sem_shape = list(operands), out_shapes, sem_shape
        self.start, self.middle, self.finish = start, middle or (lambda *refs: None), finish


def _start_all_wait_all(make_copies):
    def start(*refs):
        for cp in make_copies(*refs):
            cp.start()

    def finish(*refs):
        for cp in make_copies(*refs):
            cp.wait()

    return start, finish


def run_exchange(name, ex):
    n_in, n_out = len(ex.operands), len(ex.out_shapes)

    def body(*refs):
        ins, outs = refs[:n_in], refs[n_in:n_in + n_out]
        ex.start(ins, outs, *refs[n_in + n_out:])
        ex.middle(ins, outs, *refs[n_in + n_out:])
        ex.finish(ins, outs, *refs[n_in + n_out:])

    return pl.pallas_call(
        body, name=name, out_shape=ex.out_shapes, in_specs=[ANY] * n_in, out_specs=[ANY] * n_out,
        scratch_shapes=[pltpu.SemaphoreType.DMA(ex.sem_shape), pltpu.SemaphoreType.DMA(ex.sem_shape)],
    )(*ex.operands)


def _half(shape, axis, pc):
    h = shape[axis] // 2
    return (pl.ds(pc * h, h), slice(None)) if axis == 0 else (slice(None), pl.ds(pc * h, h))


def _half_shape(shape, axis):
    return tuple(s // 2 if a == axis else s for a, s in enumerate(shape))


def all_gather_shards(shards, axes):
    n_t = len(shards)
    n_sem = 12

    def copies(ins, outs, send_sems, recv_sems):
        x, y, c = _place()
        me, sibling, x_nbr, y_nbr = (x, y, c), (x, y, 1 - c), (1 - x, y, c), (x, 1 - y, c)
        own, of_x, of_y, of_diag = 2 * x + y, 2 * (1 - x) + y, 2 * x + 1 - y, 2 * (1 - x) + 1 - y

        def copy(t, k, quarter, pc, piece, to, from_input=False):
            axis = axes[t]
            h = ins[t].shape[axis] // 4
            cut = pl.ds((2 * pc + piece) * h, h)
            part = (cut, slice(None)) if axis == 0 else (slice(None), cut)
            dst = outs[t].at[(quarter,) + part]
            return pltpu.make_async_remote_copy(
                src_ref=ins[t].at[part] if from_input else dst, dst_ref=dst,
                send_sem=send_sems.at[t, k], recv_sem=recv_sems.at[t, k], device_id=to, device_id_type=MESH)

        stages = []
        for t in range(n_t):
            direct = [copy(t, 0, own, c, 0, x_nbr, True), copy(t, 2, own, c, 1, y_nbr, True),
                      copy(t, 1, own, c, 1, x_nbr, True), copy(t, 3, own, c, 0, y_nbr, True)]
            landing = [
                (copy(t, 0, of_x, c, 0, me), [copy(t, 4, of_x, c, 0, y_nbr), copy(t, 6, of_x, c, 0, sibling)]),
                (copy(t, 2, of_y, c, 1, me), [copy(t, 5, of_y, c, 1, x_nbr), copy(t, 8, of_y, c, 1, sibling)]),
                (copy(t, 1, of_x, c, 1, me), [copy(t, 7, of_x, c, 1, sibling)]),
                (copy(t, 3, of_y, c, 0, me), [copy(t, 9, of_y, c, 0, sibling)]),
                (copy(t, 4, of_diag, c, 0, me), [copy(t, 10, of_diag, c, 0, sibling)]),
                (copy(t, 5, of_diag, c, 1, me), [copy(t, 11, of_diag, c, 1, sibling)])]
            from_sibling = [copy(t, 6, of_x, 1 - c, 0, me), copy(t, 8, of_y, 1 - c, 1, me), copy(t, 7, of_x, 1 - c, 1, me),
                            copy(t, 9, of_y, 1 - c, 0, me), copy(t, 10, of_diag, 1 - c, 0, me), copy(t, 11, of_diag, 1 - c, 1, me)]
            stages.append((direct, landing, from_sibling))
        return stages

    def start(*refs):
        for direct, _, _ in copies(*refs):
            for cp in direct:
                cp.start()

    def pass_on(landing):
        for arrived, onward in landing:
            arrived.wait_recv()
            for cp in onward:
                cp.start()

    def middle(*refs):
        for _, landing, _ in copies(*refs):
            pass_on(landing[:4])

    def finish(*refs):
        stages = copies(*refs)
        for _, landing, _ in stages:
            pass_on(landing[4:])
        for direct, landing, from_sibling in stages:
            for cp in from_sibling:
                cp.wait_recv()
            for cp in direct + [cp for _, onward in landing for cp in onward]:
                cp.wait_send()

    return Exchange(shards, [jax.ShapeDtypeStruct((4,) + s.shape, s.dtype) for s in shards], (n_t, n_sem), start, finish, middle)


def exchange_halves_d2d(grads, axes):
    n_t = len(grads)

    def copies(ins, outs, send_sems, recv_sems):
        x, y, c = _place()
        return [pltpu.make_async_remote_copy(
            src_ref=ins[t].at[(slice(None),) + _half(ins[t].shape[1:], axes[t], 1 - c)], dst_ref=outs[t],
            send_sem=send_sems.at[t], recv_sem=recv_sems.at[t], device_id=(x, y, 1 - c), device_id_type=MESH) for t in range(n_t)]

    shapes = [jax.ShapeDtypeStruct((4,) + _half_shape(g.shape[1:], a), g.dtype) for g, a in zip(grads, axes)]
    return Exchange(grads, shapes, (n_t,), *_start_all_wait_all(copies))


def exchange_quarters_ici(parts):
    n_t = len(parts)

    def copies(ins, outs, send_sems, recv_sems):
        x, y, c = _place()
        return [pltpu.make_async_remote_copy(
            src_ref=ins[t].at[2 * px + py], dst_ref=outs[t].at[q],
            send_sem=send_sems.at[t, q], recv_sem=recv_sems.at[t, q], device_id=(px, py, c), device_id_type=MESH)
            for t in range(n_t) for q, (px, py) in enumerate(_other_chips(x, y))]

    shapes = [jax.ShapeDtypeStruct((3,) + p.shape[1:], p.dtype) for p in parts]
    return Exchange(parts, shapes, (n_t, 3), *_start_all_wait_all(copies))


def swap_d2d(halves):
    n_t = len(halves)

    def copies(ins, outs, send_sems, recv_sems):
        x, y, c = _place()
        return [pltpu.make_async_remote_copy(
            src_ref=ins[t], dst_ref=outs[t], send_sem=send_sems.at[t], recv_sem=recv_sems.at[t],
            device_id=(x, y, 1 - c), device_id_type=MESH) for t in range(n_t)]

    return Exchange(halves, [jax.ShapeDtypeStruct(h.shape, h.dtype) for h in halves], (n_t,), *_start_all_wait_all(copies))


BLOCK_BYTES = 1 << 20


def _row_block(r, c):
    fits = [rb for rb in range(16, r + 1, 16) if r % rb == 0 and rb * c * 4 <= BLOCK_BYTES]
    return max(fits) if fits else r


def _place_scalars():
    x, y, c = _place()
    return jnp.stack([c, 2 * x + y]).astype(jnp.int32)


def reduce_on_chip(tag, grads, axes):
    from_sibling = run_exchange(f"rs_d2d_{tag}", exchange_halves_d2d(grads, axes))
    parts, parts_bf16 = [], []
    for t, (g, s, axis) in enumerate(zip(grads, from_sibling, axes)):
        _, h, cols = s.shape
        rb = _row_block(h, cols)
        nb = h // rb
        blk = lambda k, i, s_ref: (k, i, 0)
        mine = (lambda k, i, s_ref, nb=nb: (k, s_ref[0] * nb + i, 0)) if axis == 0 else (lambda k, i, s_ref: (k, i, s_ref[0]))
        p32, p16 = blockmap(
            f"rs_add_{tag}{t}", lambda a, b: (a + b, a + b), (4, nb),
            [(g, (None, rb, cols), mine), (s, (None, rb, cols), blk)],
            [(s.shape, F32, (None, rb, cols), blk), (s.shape, BF16, (None, rb, cols), blk)], scalars=_place_scalars())
        parts.append(p32)
        parts_bf16.append(p16)
    return parts, parts_bf16


def reduce_across_chips(parts, from_chips):
    halves = []
    for t, (p, q) in enumerate(zip(parts, from_chips)):
        _, h, cols = p.shape
        rb = _row_block(h, cols)
        halves.append(blockmap(
            f"rs_sum{t}", lambda a, b: a + b[0].astype(F32) + b[1].astype(F32) + b[2].astype(F32), (h // rb,),
            [(p, (None, rb, cols), lambda i, s_ref: (s_ref[1], i, 0)), (q, (3, rb, cols), lambda i, s_ref: (0, i, 0))],
            [((h, cols), F32, (rb, cols), lambda i, s_ref: (i, 0))], scalars=_place_scalars())[0])
    return list(zip(halves, run_exchange("rs_swap", swap_d2d(halves))))


def _adamw(w, g, m, v):
    m = ADAM_B1 * m + (1.0 - ADAM_B1) * g
    v = ADAM_B2 * v + (1.0 - ADAM_B2) * jnp.square(g)
    m_hat = m / (1.0 - ADAM_B1 ** ADAM_STEP)
    v_hat = v / (1.0 - ADAM_B2 ** ADAM_STEP)
    delta = -ADAM_LR * (m_hat / (jnp.sqrt(v_hat) + ADAM_EPS) + ADAM_WD * w)
    return delta, m, v


def adamw(name, w, g, m, v):
    _, r, c = w.shape
    rb = _row_block(r, c)
    blk3 = lambda a: (a, (None, rb, c), lambda i: (0, i, 0))
    return blockmap(name, _adamw, (r // rb,), [blk3(w), (g, (rb, c), lambda i: (i, 0)), blk3(m), blk3(v)],
                    [(w.shape, F32, (None, rb, c), lambda i: (0, i, 0))] * 3)


def adamw_halves(name, w, mine, other, m, v, axis):
    _, r, c = w.shape
    h, c = mine.shape
    rb = _row_block(h, c)
    nb = h // rb

    def body(s_ref, w_ref, mine_ref, other_ref, m_ref, v_ref, g_out, d_out, m_out, v_out):
        g = jnp.where(pl.program_id(0) == s_ref[0], mine_ref[...], other_ref[...])
        d, nm, nv = _adamw(w_ref[...], g, m_ref[...], v_ref[...])
        g_out[...], d_out[...], m_out[...], v_out[...] = g, d, nm, nv

    spec3 = pl.BlockSpec((None, rb, c), (lambda k, i, s_ref: (0, k * nb + i, 0)) if axis == 0 else (lambda k, i, s_ref: (0, i, k)))
    spec2 = pl.BlockSpec((rb, c), lambda k, i, s_ref: (i, 0))
    grid_spec = pltpu.PrefetchScalarGridSpec(num_scalar_prefetch=1, grid=(2, nb), in_specs=[spec3, spec2, spec2, spec3, spec3],
                                             out_specs=[spec3] * 4)
    return pl.pallas_call(body, name=name, grid_spec=grid_spec, out_shape=[jax.ShapeDtypeStruct(w.shape, F32)] * 4,
                          compiler_params=_params(("parallel", "parallel")))(_place_scalars(), w, mine, other, m, v)


def _whole(name, fn, ins, outs):
    return blockmap(name, fn, (1,), [(a, a.shape, lambda i, nd=a.ndim: (0,) * nd) for a in ins],
                    [(s, d, s, lambda i, nd=len(s): (0,) * nd) for s, d in outs])


def _premix(x, w, sc, sh):
    return _rms(x, w) * (1.0 + sc) + sh


def _postmix(x, u, w_post, g1, w_pre2, sc2, sh2):
    x1 = x + g1 * _rms(u, w_post)
    return x1, _premix(x1, w_pre2, sc2, sh2)


def _merge(gs, gg, ys, yg):
    return _sigmoid(gs) * ys + _sigmoid(gg) * yg


def _final(x1, y2, w_post2, g2):
    return x1 + g2 * _rms(y2, w_post2)


def kernel(x, c, w_ada, b_ada, norm_mix_pre, norm_mix_post, w_in, ssm_conv_w, ssm_conv_b, ssm_dt_bias, ssm_A_log, ssm_D, ssm_norm_w, gdn_conv_w, gdn_dt_bias, gdn_A_log, gdn_norm_w, w_ssm_up, w_gdn_up, w_out, norm_mlp_pre, norm_mlp_post, w_mlp_up, w_mlp_down, loss_target, m_w_ada, m_b_ada, m_norm_mix_pre, m_norm_mix_post, m_w_in, m_ssm_conv_w, m_ssm_conv_b, m_ssm_dt_bias, m_ssm_A_log, m_ssm_D, m_ssm_norm_w, m_gdn_conv_w, m_gdn_dt_bias, m_gdn_A_log, m_gdn_norm_w, m_w_ssm_up, m_w_gdn_up, m_w_out, m_norm_mlp_pre, m_norm_mlp_post, m_w_mlp_up, m_w_mlp_down, v_w_ada, v_b_ada, v_norm_mix_pre, v_norm_mix_post, v_w_in, v_ssm_conv_w, v_ssm_conv_b, v_ssm_dt_bias, v_ssm_A_log, v_ssm_D, v_ssm_norm_w, v_gdn_conv_w, v_gdn_dt_bias, v_gdn_A_log, v_gdn_norm_w, v_w_ssm_up, v_w_gdn_up, v_w_out, v_norm_mlp_pre, v_norm_mlp_post, v_w_mlp_up, v_w_mlp_down):
    args = dict(locals())
    xi, yi, ci = _place()
    quarter = 2 * xi + yi
    batch = 4 * xi + 2 * yi + ci

    xt, target = x[0], loss_target[0]
    t, d = xt.shape
    hs, hv = ssm_dt_bias.shape[-1], gdn_dt_bias.shape[-1]
    d_inner = hs * SSM_HEAD_DIM
    n_grp = hs // SSM_HEADS_PER_GROUP
    gn = n_grp * SSM_D_STATE
    conv_ssm = d_inner + 2 * gn
    hq = hv // 2
    key, val = hq * GDN_HEAD, hv * GDN_HEAD
    conv_gdn = 2 * key + val
    o_dt = d_inner + conv_ssm
    o_qkv = o_dt + hs
    o_b = o_qkv + conv_gdn + val
    o_a = o_b + hv
    o_gs = o_a + hv
    n_proj = o_gs + 2 * d
    a_z, a_q = 0, o_dt
    a_gs = a_q + conv_gdn + val
    a_gg = a_gs + d
    a_small = a_gg + d
    n_al = -(-(a_small + LANES) // MM_TILE_N) * MM_TILE_N

    def to_aligned(w):
        z = lambda n: jnp.zeros((n, w.shape[1]), w.dtype)
        return jnp.concatenate([
            w[:o_dt], w[o_qkv:o_b], w[o_gs:],
            w[o_dt:o_qkv], z(LANE_B - hs), w[o_b:o_a], z(LANE_A - LANE_B - hv), w[o_a:o_gs], z(LANES - LANE_A - hv),
            z(n_al - a_small - LANES)], axis=0)

    def from_aligned(w):
        s = a_small
        return jnp.concatenate([
            w[:o_dt], w[s + LANE_DT:s + LANE_DT + hs], w[a_q:a_gs], w[s + LANE_B:s + LANE_B + hv],
            w[s + LANE_A:s + LANE_A + hv], w[a_gs:a_small]], axis=0)

    def lanes(vec, at):
        return jnp.zeros((1, LANES), F32).at[:, at:at + vec.shape[-1]].set(vec.reshape(1, -1))

    n_cw = CONV_K * ssm_conv_w.shape[-1]
    small_in = gather_flat("ag_small", jnp.concatenate([c.reshape(-1), ssm_conv_w.reshape(-1), gdn_conv_w.reshape(-1)]))
    c_all = small_in[:, :d]
    by_chip = small_in[0::2]

    def whole_conv_w(lo):
        return jnp.transpose(by_chip[:, lo:lo + n_cw].reshape(4, CONV_K, -1), (1, 0, 2)).reshape(CONV_K, -1)

    cw_ssm, cw_gdn = whole_conv_w(d), whole_conv_w(d + n_cw)
    cb_ssm = ssm_conv_b

    n_ada = w_ada.shape[-1]
    b_q = lax.dynamic_slice_in_dim(b_ada, quarter * n_ada, n_ada, axis=1)
    mod_q = _whole("ada_fwd", lambda ca, w, b: _bdot(_silu(ca), w) + b, [c_all, w_ada[0], b_q], [((N_DEV, n_ada), F32)])[0]
    mod_all = gather_flat("ag_mod", mod_q.reshape(-1)).reshape(N_DEV, N_DEV, n_ada)[0::2]
    mod = lax.dynamic_index_in_dim(mod_all, batch, axis=1, keepdims=False).reshape(1, 4 * n_ada)
    sh1, sc1, g1, sh2, sc2, g2 = [mod[:, i * d:(i + 1) * d] for i in range(6)]

    transposed = lambda a: jnp.swapaxes(a, 1, 2)
    own = [w.astype(BF16) for w in (transposed(w_in)[0], w_ssm_up[0], w_gdn_up[0], w_out[0], w_mlp_up[0], w_mlp_down[0])]
    with_own = lambda gs, ws: [lax.dynamic_update_index_in_dim(g, w, quarter, 0) for g, w in zip(gs, ws)]
    cols_major = lambda g: jnp.transpose(g, (1, 0, 2)).reshape(g.shape[1], -1)
    rows_major = lambda g: g.reshape(-1, g.shape[2])
    wb_in = to_aligned(rows_major(with_own(run_exchange("ag_w_in", all_gather_shards(own[:1], [1])), own[:1])[0]))

    h1 = rowmap("premix", _premix, [xt], [norm_mix_pre, sc1, sh1], [(d, BF16)])[0]
    proj, gathered = matmul("in_proj", h1, wb_in, tb=True, comm=all_gather_shards(own[1:], [0] * 5))
    gathered = with_own(gathered, own[1:])
    wb_ssm_up, wb_gdn_up, wb_out = rows_major(gathered[0]), rows_major(gathered[1]), rows_major(gathered[2])
    wb_up, wb_down = cols_major(gathered[3]), rows_major(gathered[4])

    wide = 2 * LANES
    ssd_rows = [(proj, d_inner + conv_ssm, a_z, True), (proj, LANES, a_small, False)]
    ssd_consts = [(cw_ssm, False), (cb_ssm, False), (lanes(ssm_dt_bias, LANE_DT), False), (lanes(ssm_A_log, LANE_DT), False),
                  (lanes(ssm_D, LANE_DT), False), (ssm_norm_w.reshape(n_grp, 1, wide), False)]
    y_ssm_n, st_ssm, tails_ssm = scan_fwd("ssd_fwd", ssd_step, SSM_CHUNK, 1, ssd_rows, ssd_consts, d_inner, 1,
                                          state_shape=(n_grp,) + STATE_SHAPE, tail_cols=conv_ssm)
    gdn_rows = [(proj, conv_gdn + val, a_q, True), (proj, LANES, a_small, False)]
    gdn_consts = [(cw_gdn, False), (lanes(gdn_dt_bias, LANE_A), False), (lanes(gdn_A_log, LANE_A), False), (gdn_norm_w, False)]
    gdn_rows_per_step = GDN_CHUNK * GDN_CHUNKS_PER_STEP
    y_gdn_n, st_gdn, tails_gdn, inv_gdn = scan_fwd(
        "gdn_fwd", gdn_step, gdn_rows_per_step, 1, gdn_rows, gdn_consts, val, 1, keep=(hv, gdn_rows_per_step, gdn_rows_per_step),
        state_shape=(hq,) + STATE_SHAPE, tail_cols=conv_gdn)

    y_ssm = matmul("ssm_up", y_ssm_n, wb_ssm_up)
    y_gdn = matmul("gdn_up", y_gdn_n, wb_gdn_up)
    gates = [(proj, d, a_gs), (proj, d, a_gg)]
    merged = rowmap("merge", _merge, gates + [y_ssm, y_gdn], [], [(d, BF16)])[0]
    u = matmul("w_out", merged, wb_out)
    post_consts = [norm_mix_post, g1, norm_mlp_pre, sc2, sh2]
    x1, h2 = rowmap("postmix", _postmix, [xt, u], post_consts, [(d, F32), (d, BF16)])
    relu2 = lambda acc: (acc, jnp.square(jnp.maximum(acc, 0.0)))
    a_up, act = matmul("mlp_up", h2, wb_up, out_dtypes=(BF16, BF16), epi=relu2)
    y2 = matmul("mlp_down", act, wb_down)

    def final_bwd(x1_, y2_, tgt, w_, g_):
        x2, vjp = jax.vjp(_final, x1_, y2_, w_, g_)
        err = x2 - tgt
        loss = 0.5 * jnp.sum(jnp.mean(err * err, axis=-1, keepdims=True), axis=0, keepdims=True)
        dx1, dy2, dw, dg = vjp(err / d)
        return dx1, dy2, loss, dw, dg

    dx1, dy2, loss_part, d_norm_mlp_post, dg2 = rowmap(
        "final", final_bwd, [x1, y2, target], [norm_mlp_post, g2], [(d, F32), (d, BF16)], [((1, 1), F32), ((1, d), F32), ((1, d), F32)])

    d_a = matmul("mlp_down_dx", dy2, wb_down, tb=True, out_dtypes=(BF16,), extras=[a_up],
                 epi=lambda acc, a: acc * 2.0 * jnp.maximum(a.astype(F32), 0.0))
    gw_down = matmul("mlp_down_dw", act, dy2, ta=True)
    dh2 = matmul("mlp_up_dx", d_a, wb_up, tb=True)
    gw_up = matmul("mlp_up_dw", h2, d_a, ta=True)

    def postmix_bwd(x_, u_, dx1_, dh2_, *cs):
        _, vjp = jax.vjp(_postmix, x_, u_, *cs)
        return vjp((dx1_, dh2_))

    dxa, du, d_norm_mix_post, dg1, d_norm_mlp_pre, dsc2, dsh2 = rowmap(
        "postmix_bwd", postmix_bwd, [xt, u, dx1, dh2], post_consts, [(d, F32), (d, BF16)], [((1, d), F32)] * 5)
    d_merged = matmul("w_out_dx", du, wb_out, tb=True)
    gw_out = matmul("w_out_dw", merged, du, ta=True)

    def merge_bwd(gs, gg, ys, yg, dm):
        _, vjp = jax.vjp(_merge, gs, gg, ys, yg)
        dgs, dgg, dys, dyg = vjp(dm)
        return dys, dyg, jnp.concatenate([dgs, dgg], axis=1)

    dy_ssm, dy_gdn, dproj = rowmap("merge_bwd", merge_bwd, gates + [y_ssm, y_gdn, d_merged], [],
                                   [(d, BF16), (d, BF16), (2 * d, BF16, jax.ShapeDtypeStruct((t, n_al), BF16), a_gs)])
    dy_ssm_n = matmul("ssm_up_dx", dy_ssm, wb_ssm_up, tb=True, out_dtypes=(BF16,))
    gw_ssm_up = matmul("ssm_up_dw", y_ssm_n, dy_ssm, ta=True)
    dy_gdn_n = matmul("gdn_up_dx", dy_gdn, wb_gdn_up, tb=True, out_dtypes=(BF16,))
    gw_gdn_up = matmul("gdn_up_dw", y_gdn_n, dy_gdn, ta=True)

    dproj, dsmall_ssm, dcw_ssm, dcb_ssm, d_sdtb, d_salog, d_sdsk, d_snw = scan_bwd(
        "ssd_bwd", ssd_step, SSM_CHUNK, 1, ssd_rows, ssd_consts, [st_ssm, tails_ssm], dy_ssm_n, [BF16, F32], 1, {0: dproj})
    dproj, dsmall_gdn, dcw_gdn, d_gdtb, d_galog, d_gnw = scan_bwd(
        "gdn_bwd", gdn_step, gdn_rows_per_step, 1, gdn_rows, gdn_consts, [st_gdn, tails_gdn], dy_gdn_n, [BF16, F32], 1,
        {0: dproj}, kept=inv_gdn)
    tail = n_al - a_small
    dproj = rowmap("small_sum", lambda a, b: jnp.concatenate([a + b, jnp.zeros((a.shape[0], tail - LANES), F32)], axis=1),
                   [dsmall_ssm, dsmall_gdn], [], [(tail, BF16, dproj, a_small)])[0]
    quarters_cols = lambda g: jnp.transpose(g.reshape(g.shape[0], 4, -1), (1, 0, 2))
    quarters_rows = lambda g: g.reshape(4, g.shape[0] // 4, g.shape[1])
    rest32, rest16 = reduce_on_chip("rest", [quarters_rows(gw_ssm_up), quarters_rows(gw_gdn_up), quarters_rows(gw_out),
                                             quarters_cols(gw_up), quarters_rows(gw_down)], [0] * 5)
    gw_in_al, rest_chips = matmul("in_proj_dw", dproj, h1, ta=True, comm=exchange_quarters_ici(rest16))
    in32, in16 = reduce_on_chip("in", [quarters_rows(from_aligned(gw_in_al))], [1])
    dh1, in_chips = matmul("in_proj_dx", dproj, wb_in, comm=exchange_quarters_ici(in16))

    def premix_bwd(x_, dxa_, dh1_, w_, sc_, sh_):
        _, vjp = jax.vjp(_premix, x_, w_, sc_, sh_)
        dx, dw, dsc, dsh = vjp(dh1_)
        return dx + dxa_, dw, dsc, dsh

    grad_x, d_norm_mix_pre, dsc1, dsh1 = rowmap(
        "premix_bwd", premix_bwd, [xt, dxa, dh1], [norm_mix_pre, sc1, sh1], [(d, F32)], [((1, d), F32)] * 3)

    dmod_all = gather_flat("ag_dmod", jnp.concatenate([dsh1, dsc1, dg1, dsh2, dsc2, dg2], axis=1).reshape(-1))
    dmod_q = lax.dynamic_slice_in_dim(dmod_all, quarter * n_ada, n_ada, axis=1)
    gw_ada, gb_ada = _whole(
        "ada_bwd", lambda ca, dq_, da_: (_bdot(_silu(ca), dq_, TN), jnp.sum(da_, axis=0, keepdims=True)),
        [c_all, dmod_q, dmod_all], [((d, n_ada), F32), ((1, 4 * n_ada), F32)])

    partial = [d_norm_mix_pre, d_norm_mix_post, dcw_ssm, dcb_ssm, d_sdtb[:, LANE_DT:LANE_DT + hs], d_salog[:, LANE_DT:LANE_DT + hs],
               d_sdsk[:, LANE_DT:LANE_DT + hs], d_snw, dcw_gdn, d_gdtb[:, LANE_A:LANE_A + hv], d_galog[:, LANE_A:LANE_A + hv], d_gnw,
               d_norm_mlp_pre, d_norm_mlp_post, loss_part]
    sizes = [p.size for p in partial]
    stacked = gather_flat("ag_grads", jnp.concatenate([p.reshape(-1) for p in partial]))
    summed = _whole("small_sum8", lambda s: jnp.sum(s, axis=0, keepdims=True), [stacked], [((1, stacked.shape[1]), F32)])[0][0]
    offs = [0]
    for s in sizes:
        offs.append(offs[-1] + s)
    red = [summed[offs[i]:offs[i + 1]] for i in range(len(sizes))]
    loss = red[-1][0]
    my_cols = lambda full: lax.dynamic_slice_in_dim(full.reshape(CONV_K, -1), quarter * (n_cw // CONV_K), n_cw // CONV_K, axis=1)
    small_grads = {
        "b_ada": gb_ada, "norm_mix_pre": red[0], "norm_mix_post": red[1], "ssm_conv_w": my_cols(red[2]), "ssm_conv_b": red[3],
        "ssm_dt_bias": red[4], "ssm_A_log": red[5], "ssm_D": red[6], "ssm_norm_w": red[7], "gdn_conv_w": my_cols(red[8]),
        "gdn_dt_bias": red[9], "gdn_A_log": red[10], "gdn_norm_w": red[11], "norm_mlp_pre": red[12], "norm_mlp_post": red[13]}

    big_names = ["w_in", "w_ssm_up", "w_gdn_up", "w_out", "w_mlp_up", "w_mlp_down"]
    big_grads = dict(zip(big_names, reduce_across_chips(in32 + rest32, in_chips + rest_chips)))

    names = ['w_ada', 'b_ada', 'norm_mix_pre', 'norm_mix_post', 'w_in', 'ssm_conv_w', 'ssm_conv_b', 'ssm_dt_bias', 'ssm_A_log', 'ssm_D',
             'ssm_norm_w', 'gdn_conv_w', 'gdn_dt_bias', 'gdn_A_log', 'gdn_norm_w', 'w_ssm_up', 'w_gdn_up', 'w_out', 'norm_mlp_pre',
             'norm_mlp_post', 'w_mlp_up', 'w_mlp_down']
    grad, delta, new_m, new_v = {}, {}, {}, {}
    for n, (mine, other) in big_grads.items():
        view, axis = (transposed, 1) if n == "w_in" else ((lambda a: a), 0)
        res = adamw_halves("adamw_" + n, view(args[n]), mine, other, view(args["m_" + n]), view(args["v_" + n]), axis)
        grad[n], delta[n], new_m[n], new_v[n] = [view(a) for a in res]
    grad["w_ada"] = gw_ada.reshape(w_ada.shape)
    delta["w_ada"], new_m["w_ada"], new_v["w_ada"] = adamw("adamw_w_ada", w_ada, gw_ada, m_w_ada, v_w_ada)
    small_names = [n for n in names if n not in grad]
    flat = lambda pre: jnp.concatenate([args[pre + n].reshape(-1) for n in small_names]).reshape(1, 1, -1)
    g_flat = jnp.concatenate([small_grads[n].reshape(-1) for n in small_names]).reshape(1, -1)
    dl, nm, nv = adamw("adamw_small", flat(""), g_flat, flat("m_"), flat("v_"))
    off = 0
    for n in small_names:
        shape = args[n].shape
        size = args[n].size
        grad[n], delta[n], new_m[n], new_v[n] = [a.reshape(-1)[off:off + size].reshape(shape) for a in (g_flat, dl, nm, nv)]
        off += size

    return (loss, grad_x.reshape(x.shape), *[grad[n] for n in names], *[delta[n] for n in names],
            *[new_m[n] for n in names], *[new_v[n] for n in names])
```

```python
import functools

import jax
import jax.numpy as jnp
from jax import lax
from jax.experimental import pallas as pl
from jax.experimental.pallas import tpu as pltpu

F32 = jnp.float32
BF16 = jnp.bfloat16
MESH = pl.DeviceIdType.MESH

EPS = 1e-6
SSM_HEAD_DIM = 64
SSM_HEADS_PER_GROUP = 4
SSM_D_STATE = 128
SSM_CHUNK = 128
GDN_HEAD = 128
GDN_CHUNK = 64
CONV_K = 4
TAIL_ROWS = 8
LANE_DT, LANE_B, LANE_A = 0, 32, 48
ADAM_LR, ADAM_B1, ADAM_B2, ADAM_EPS, ADAM_WD, ADAM_STEP = 0.001, 0.9, 0.999, 1e-08, 0.01, 10

VMEM_LIMIT_BYTES = 56 * 1024 * 1024
LANES = 128
N_DEV = 8

NN = (((1,), (0,)), ((), ()))
NT = (((1,), (1,)), ((), ()))
TN = (((0,), (0,)), ((), ()))


BNN = (((2,), (1,)), ((0,), (0,)))
BNT = (((2,), (2,)), ((0,), (0,)))
BTN = (((1,), (1,)), ((0,), (0,)))
_KIND = {NN: ("NN", 0), NT: ("NT", 0), TN: ("TN", 0), BNN: ("NN", 1), BNT: ("NT", 1), BTN: ("TN", 1)}
_DIMS = {"NN": (NN, BNN), "NT": (NT, BNT), "TN": (TN, BTN)}


def _dg(a, b, dims):
    return lax.dot_general(a, b, dims, preferred_element_type=F32)


def _raw_bf16(a, b, dims):
    return _dg(a.astype(BF16), b.astype(BF16), dims)


def _raw_bf16x3(a, b, dims):
    ah, bh = a.astype(BF16), b.astype(BF16)
    al, bl = (a - ah.astype(F32)).astype(BF16), (b - bh.astype(F32)).astype(BF16)
    return _dg(ah, bh, dims) + (_dg(ah, bl, dims) + _dg(al, bh, dims))


def _make_dot(raw):
    @functools.partial(jax.custom_vjp, nondiff_argnums=(2,))
    def dot(a, b, dims):
        return raw(a, b, dims)

    def fwd(a, b, dims):
        return raw(a, b, dims), (a, b)

    def bwd(dims, res, ct):
        a, b = res
        kind, batched = _KIND[dims]
        d = lambda k: _DIMS[k][batched]
        if kind == "NN":
            da, db = raw(ct, b, d("NT")), raw(a, ct, d("TN"))
        elif kind == "NT":
            da, db = raw(ct, b, d("NN")), raw(ct, a, d("TN"))
        else:
            da, db = raw(b, ct, d("NT")), raw(a, ct, d("NN"))
        return da.astype(a.dtype), db.astype(b.dtype)

    dot.defvjp(fwd, bwd)
    return lambda a, b, dims=NN: dot(a, b, dims)


_bdot = _make_dot(_raw_bf16)
_hdot = _make_dot(_raw_bf16x3)


def _mask_dot(mask, x, dims):
    m = mask.astype(BF16)
    hi = x.astype(BF16)
    r = x - hi.astype(F32)
    mid = r.astype(BF16)
    lo = (r - mid.astype(F32)).astype(BF16)
    return sum(_dg(m, p, dims) for p in (hi, mid, lo))


def _sigmoid(x):
    return 0.5 * jnp.tanh(0.5 * x) + 0.5


def _silu(x):
    return x * _sigmoid(x)


def _softplus(x):
    return jnp.maximum(x, 0.0) + jnp.log(1.0 + jnp.exp(-jnp.abs(x)))


def _rms(x, w):
    return x * lax.rsqrt(jnp.mean(x * x, axis=-1, keepdims=True) + EPS) * w


def _lane_col(m, idx):
    lane = lax.broadcasted_iota(jnp.int32, m.shape, 1)
    return jnp.sum(jnp.where(lane == idx, m, 0.0), axis=1, keepdims=True)


def _tril(n, strict=False, seg=None):
    r = lax.broadcasted_iota(jnp.int32, (n, n), 0)
    c = lax.broadcasted_iota(jnp.int32, (n, n), 1)
    low = (r > c) if strict else (r >= c)
    if seg is None or seg >= n:
        return low
    shift = seg.bit_length() - 1
    return jnp.logical_and(low, (r >> shift) == (c >> shift))


@functools.partial(jax.custom_vjp, nondiff_argnums=(1,))
def _cumsum_rows(x, seg):
    return _mask_dot(_tril(x.shape[0], seg=seg), x, NN)


def _cumsum_rows_fwd(x, seg):
    return _cumsum_rows(x, seg), None


def _cumsum_rows_bwd(seg, _, ct):
    return (_mask_dot(_tril(ct.shape[0], seg=seg), ct, TN),)


_cumsum_rows.defvjp(_cumsum_rows_fwd, _cumsum_rows_bwd)


def _head_rows(m_t, idx):
    sub = lax.broadcasted_iota(jnp.int32, m_t.shape, 0)
    return jnp.sum(jnp.where(sub == idx, m_t, 0.0), axis=0, keepdims=True)


def _params(sem):
    return pltpu.CompilerParams(dimension_semantics=sem, vmem_limit_bytes=VMEM_LIMIT_BYTES)


def _into_plumbing(outs, first_input):
    arrays, aliases = [], {}
    for k, o in enumerate(outs):
        if len(o) > 4 and not isinstance(o[4], jax.ShapeDtypeStruct):
            aliases[first_input + len(arrays)] = k
            arrays.append(o[4])
    return arrays, aliases


def blockmap(name, fn, grid, ins, outs, accs=(), scalars=None):
    n_in, n_out, n_acc = len(ins), len(outs), len(accs)
    n_grid = len(grid)
    n_pre = 0 if scalars is None else 1
    into_arrays, aliases = _into_plumbing(outs, n_pre + n_in)
    n_into = len(into_arrays)

    def body(*refs):
        refs = refs[n_pre:n_pre + n_in] + refs[n_pre + n_in + n_into:]
        vals = fn(*[r[...] for r in refs[:n_in]])
        if not isinstance(vals, (tuple, list)):
            vals = (vals,)
        for r, v in zip(refs[n_in:n_in + n_out], vals[:n_out]):
            r[...] = v.astype(r.dtype)
        if n_acc:
            first = functools.reduce(jnp.logical_and, [pl.program_id(a) == 0 for a in range(n_grid)])
            acc_refs = refs[n_in + n_out:]

            @pl.when(first)
            def _():
                for r in acc_refs:
                    r[...] = jnp.zeros(r.shape, r.dtype)

            for r, v in zip(acc_refs, vals[n_out:]):
                r[...] += v.astype(r.dtype)

    zeros = lambda nd: (lambda *_: (0,) * nd)
    in_specs = [pl.BlockSpec(b, im) for _, b, im in ins] + [pl.BlockSpec(memory_space=pl.ANY)] * n_into
    out_specs = [pl.BlockSpec(o[2], o[3]) for o in outs] + [pl.BlockSpec(s, zeros(len(s))) for s, _ in accs]
    out_shape = [jax.ShapeDtypeStruct(o[0], o[1]) for o in outs] + [jax.ShapeDtypeStruct(s, d) for s, d in accs]
    cparams = _params(("arbitrary",) * n_grid if n_acc else ("parallel",) * n_grid)
    arrays = [a for a, _, _ in ins] + into_arrays
    if scalars is None:
        return pl.pallas_call(body, name=name, grid=grid, in_specs=in_specs, out_specs=out_specs, out_shape=out_shape,
                              input_output_aliases=aliases, compiler_params=cparams)(*arrays)
    spec = pltpu.PrefetchScalarGridSpec(num_scalar_prefetch=1, grid=grid, in_specs=in_specs, out_specs=out_specs)
    return pl.pallas_call(body, name=name, grid_spec=spec, out_shape=out_shape, input_output_aliases=aliases,
                          compiler_params=cparams)(scalars, *arrays)


def rowmap(name, fn, rows, consts, outs, accs=(), rb=512):
    norm = [(r, r.shape[1], 0) if not isinstance(r, tuple) else (r[0], r[1], r[2] // r[1]) for r in rows]
    assert all(not isinstance(r, tuple) or r[2] % r[1] == 0 for r in rows)
    t = norm[0][0].shape[0]
    rb = min(rb, t)
    ins = [(a, (rb, n), (lambda i, cb=cb: (i, cb))) for a, n, cb in norm]
    ins += [(cst, cst.shape, (lambda i, nd=cst.ndim: (0,) * nd)) for cst in consts]
    o = []
    for out in outs:
        if len(out) == 2:
            o.append(((t, out[0]), out[1], (rb, out[0]), lambda i: (i, 0)))
        else:
            n, d, into, off = out
            assert off % n == 0 and into.dtype == d
            o.append((into.shape, d, (rb, n), (lambda i, cb=off // n: (i, cb)), into))
    return blockmap(name, fn, (t // rb,), ins, o, accs)


MM_TILE_M, MM_TILE_N, MM_TILE_K = 1024, 1024, 2048


def _tile(dim, cap):
    if dim <= cap:
        return dim
    best = max(t for t in range(LANES, cap + 1, LANES) if dim % t == 0)
    return best


def matmul(name, a, b, ta=False, tb=False, out_dtypes=(F32,), epi=None, extras=(), comm=None):
    (k_dim, m_dim) = a.shape if ta else a.shape[::-1]
    n_dim = b.shape[0] if tb else b.shape[1]
    assert (b.shape[1] if tb else b.shape[0]) == k_dim, (name, a.shape, b.shape)
    tm, tn, tk = _tile(m_dim, MM_TILE_M), _tile(n_dim, MM_TILE_N), _tile(k_dim, MM_TILE_K)
    grid = (m_dim // tm, n_dim // tn, k_dim // tk)
    k_steps = grid[2]
    n_extra, n_out = len(extras), len(out_dtypes)
    n_cin, n_cout = (len(comm.operands), len(comm.out_shapes)) if comm else (0, 0)
    dims = (((0 if ta else 1,), (1 if tb else 0,)), ((), ()))

    def body(*refs):
        ins, outs, scratch = refs[:2 + n_extra + n_cin], refs[2 + n_extra + n_cin:][:n_out + n_cout], refs[2 + n_extra + n_cin + n_out + n_cout:]
        extra_refs, out_refs = ins[2:2 + n_extra], outs[:n_out]
        ids = [pl.program_id(ax) for ax in range(3)]
        if comm:
            comm_refs = (ins[2 + n_extra:], outs[n_out:], scratch[-2], scratch[-1])

            step = (ids[0] * grid[1] + ids[1]) * grid[2] + ids[2]

            @pl.when(step == 0)
            def _():
                comm.start(*comm_refs)

            @pl.when(step == (grid[0] * grid[1] * grid[2]) // 2)
            def _():
                comm.middle(*comm_refs)

        def finish(acc):
            vals = (acc,) if epi is None else epi(acc, *[r[...] for r in extra_refs])
            if not isinstance(vals, (tuple, list)):
                vals = (vals,)
            for r, v in zip(out_refs, vals):
                r[...] = v.astype(r.dtype)

        prod = lax.dot_general(ins[0][...].astype(BF16), ins[1][...].astype(BF16), dims, preferred_element_type=F32)
        if k_steps == 1:
            finish(prod)
        else:
            acc_ref = scratch[0]

            @pl.when(ids[2] == 0)
            def _():
                acc_ref[...] = jnp.zeros(acc_ref.shape, F32)

            acc_ref[...] += prod

            @pl.when(ids[2] == k_steps - 1)
            def _():
                finish(acc_ref[...])

        if comm:
            @pl.when(functools.reduce(jnp.logical_and, [i == g - 1 for i, g in zip(ids, grid)]))
            def _():
                comm.finish(*comm_refs)

    a_spec = pl.BlockSpec((tk, tm), lambda i, j, k: (k, i)) if ta else pl.BlockSpec((tm, tk), lambda i, j, k: (i, k))
    b_spec = pl.BlockSpec((tn, tk), lambda i, j, k: (j, k)) if tb else pl.BlockSpec((tk, tn), lambda i, j, k: (k, j))
    mn_spec = pl.BlockSpec((tm, tn), lambda i, j, k: (i, j))
    scratch_shapes = [] if k_steps == 1 else [pltpu.VMEM((tm, tn), F32)]
    if comm:
        scratch_shapes += [pltpu.SemaphoreType.DMA(comm.sem_shape), pltpu.SemaphoreType.DMA(comm.sem_shape)]
    res = pl.pallas_call(
        body, name=name, grid=grid,
        in_specs=[a_spec, b_spec] + [mn_spec] * n_extra + [ANY] * n_cin,
        out_specs=[mn_spec] * n_out + [ANY] * n_cout,
        out_shape=[jax.ShapeDtypeStruct((m_dim, n_dim), d) for d in out_dtypes] + (comm.out_shapes if comm else []),
        scratch_shapes=scratch_shapes,
        compiler_params=_params(("arbitrary",) * 3 if comm else ("parallel", "parallel", "arbitrary")),
    )(a, b, *extras, *(comm.operands if comm else []))
    main = res[:n_out] if n_out > 1 else res[0]
    return (main, list(res[n_out:])) if comm else main


def ssd_step(g0, state, tail, zxbc, small, cw, cb, p_dtb, p_alog, p_dsk, nw):
    d_in, gn = state.shape[0] * 2 * LANES, state.shape[0] * LANES
    z = zxbc[:, :d_in]
    act, new_tail = _conv_silu_carried(tail, zxbc[:, d_in:], cw, cb)
    xs, bm, cm = act[:, :d_in], act[:, d_in:d_in + gn], act[:, d_in + gn:]
    hb, n = state.shape[0], xs.shape[0]
    n_pair, n_head = 2 * hb, 4 * hb
    causal = _tril(n)
    dt_all = _softplus(small + p_dtb)
    a_all = dt_all * (-jnp.exp(p_alog))
    acum_all = _cumsum_rows(a_all, n)
    acum_t = acum_all.T
    lane0 = LANE_DT + SSM_HEADS_PER_GROUP * g0
    sub = lax.broadcasted_iota(jnp.int32, acum_t.shape, 0)
    heads = range(n_head)
    acum = jnp.stack([_lane_col(acum_all, lane0 + i) for i in heads])
    acum_row = jnp.stack([jnp.sum(jnp.where(sub == lane0 + i, acum_t, 0.0), axis=0, keepdims=True) for i in heads])
    dt = jnp.stack([_lane_col(dt_all, lane0 + i) for i in heads])
    dsk = jnp.stack([_lane_col(p_dsk, lane0 + i) for i in heads])
    decay = jnp.exp(jnp.where(causal, acum - acum_row, -jnp.inf))
    a_last = acum[:, n - 1:n, :]

    def split(a):
        return [a[:, i * LANES:(i + 1) * LANES] for i in range(a.shape[1] // LANES)]

    def pairs(a, axis=2):
        even = jnp.stack([a[2 * p] for p in range(n_pair)])
        odd = jnp.stack([a[2 * p + 1] for p in range(n_pair)])
        shape = (n_pair, LANES, LANES) if axis == 1 else (n_pair, a.shape[1], LANES)
        return jnp.where(lax.broadcasted_iota(jnp.int32, shape, axis) < SSM_HEAD_DIM, even, odd)

    bms, cms = split(bm), split(cm)
    cb = _bdot(jnp.stack(cms), jnp.stack(bms), BNT)
    cbd = jnp.stack([cb[i // SSM_HEADS_PER_GROUP] for i in heads]) * decay
    xp = jnp.stack(split(xs))
    xdt = xp * pairs(dt)
    yd = _bdot(cbd, jnp.stack([xdt[i // 2] for i in heads]), BNN)
    lane = lax.broadcasted_iota(jnp.int32, (n_pair, n, LANES), 2)
    y_diag = jnp.where(lane < SSM_HEAD_DIM, jnp.stack([yd[2 * p] for p in range(n_pair)]), jnp.stack([yd[2 * p + 1] for p in range(n_pair)]))
    st = state.reshape(n_pair, LANES, LANES)
    cm2 = jnp.stack([cms[p // 2] for p in range(n_pair)])
    bm2 = jnp.stack([bms[p // 2] for p in range(n_pair)])
    y_off = _bdot(cm2, st, BNT) * pairs(jnp.exp(acum))
    new = st * pairs(jnp.exp(a_last), axis=1) + _bdot(xdt * pairs(jnp.exp(a_last - acum)), bm2, BTN)
    y = y_diag + y_off + pairs(dsk) * xp
    y = jnp.concatenate([y[p] for p in range(n_pair)], axis=1) * _silu(z)
    wide = 2 * LANES
    y = jnp.concatenate([_rms(y[:, i * wide:(i + 1) * wide], nw[i]) for i in range(hb)], axis=1)
    return new.reshape(state.shape), new_tail, y


@functools.partial(jax.custom_vjp, nondiff_argnums=(1,))
def _unit_lower_inverse(a, seg):
    n = a.shape[-1]
    r = lax.broadcasted_iota(jnp.int32, (n, n), 0)
    c = lax.broadcasted_iota(jnp.int32, (n, n), 1)
    shift = min(INVERSE_BASE, seg).bit_length() - 1
    power = jnp.where((r >> shift) == (c >> shift), a, 0.0)
    inv = (r == c).astype(F32) - power
    span = 2
    while span < (1 << shift):
        power = _hdot(power, power, BNN)
        inv = inv + _hdot(inv, power, BNN)
        span *= 2
    while (1 << shift) < seg:
        below = jnp.logical_and((r >> (shift + 1)) == (c >> (shift + 1)), (r >> shift) != (c >> shift))
        inv = inv - _hdot(inv, _hdot(jnp.where(below, a, 0.0), inv, BNN), BNN)
        shift += 1
    return inv


def _unit_lower_inverse_fwd(a, seg):
    inv = _unit_lower_inverse(a, seg)
    return inv, inv


def _unit_lower_inverse_bwd(seg, inv, ct):
    return (-_hdot(_hdot(inv, ct, BTN), inv, BNT),)


_unit_lower_inverse.defvjp(_unit_lower_inverse_fwd, _unit_lower_inverse_bwd)


@jax.custom_vjp
def _known_inverse(a, inv):
    return inv


def _known_inverse_fwd(a, inv):
    return inv, inv


def _known_inverse_bwd(inv, ct):
    return _unit_lower_inverse_bwd(None, inv, ct)[0], jnp.zeros_like(inv)


_known_inverse.defvjp(_known_inverse_fwd, _known_inverse_bwd)


@functools.partial(jax.custom_vjp, nondiff_argnums=(1,))
def _rotate_rows(x, k):
    return x if k == 0 else pltpu.roll(x, k % x.shape[0], 0)


def _rotate_rows_fwd(x, k):
    return _rotate_rows(x, k), None


def _rotate_rows_bwd(k, _, ct):
    return (_rotate_rows(ct, -k),)


_rotate_rows.defvjp(_rotate_rows_fwd, _rotate_rows_bwd)


def _conv_silu_carried(tail, x, cw, cb=0.0):
    n = x.shape[0]
    ext = jnp.concatenate([tail, x], axis=0)
    pre = cb + sum(cw[j:j + 1, :] * _rotate_rows(ext, CONV_K - 1 - j)[TAIL_ROWS:] for j in range(CONV_K))
    return _silu(pre), x[n - TAIL_ROWS:]


def _l2norm(x):
    return x * lax.rsqrt(jnp.sum(x * x, axis=-1, keepdims=True) + EPS)


def gdn_step(hq0, state, tail, qkvz, small, cw, p_dtb, p_alog, nw, keep=False, kept=None):
    n, chunk = qkvz.shape[0], GDN_CHUNK
    hb = state.shape[0]
    nb = 2 * hb
    cur = state.reshape(nb, LANES, LANES)
    conv_cols = 4 * hb * LANES
    act, new_tail = _conv_silu_carried(tail, qkvz[:, :conv_cols], cw)
    q, k, v = act[:, :hb * LANES], act[:, hb * LANES:2 * hb * LANES], act[:, 2 * hb * LANES:]
    z = qkvz[:, conv_cols:]
    causal, strict = _tril(n, seg=chunk), _tril(n, True, seg=chunk)
    beta_all = _sigmoid(small)
    g_all = -jnp.exp(p_alog) * _softplus(small + p_dtb)
    gcum_all = _cumsum_rows(g_all, chunk)
    gcum_t = gcum_all.T
    split = lambda a: [a[:, i * LANES:(i + 1) * LANES] for i in range(a.shape[1] // LANES)]
    per_value_head = lambda a: jnp.stack([a[i // 2] for i in range(nb)])
    qh, kh = _l2norm(jnp.stack(split(q))) * (GDN_HEAD ** -0.5), _l2norm(jnp.stack(split(k)))
    q2, k2 = per_value_head(qh), per_value_head(kh)
    v2, z2 = jnp.stack(split(v)), jnp.stack(split(z))
    gcum = jnp.stack([_lane_col(gcum_all, LANE_A + 2 * hq0 + i) for i in range(nb)])
    gcum_row = jnp.stack([_head_rows(gcum_t, LANE_A + 2 * hq0 + i) for i in range(nb)])
    beta = jnp.stack([_lane_col(beta_all, LANE_B + 2 * hq0 + i) for i in range(nb)])
    dmat = jnp.exp(jnp.where(causal, gcum - gcum_row, -jnp.inf))
    a_low = jnp.where(strict, beta * per_value_head(_bdot(kh, kh, BNT)) * dmat, 0.0)
    inv = _unit_lower_inverse(a_low, chunk) if kept is None else _known_inverse(a_low, kept)
    egc = jnp.exp(gcum)
    u = _hdot(inv, v2 * beta, BNN)
    w = _hdot(inv, k2 * (beta * egc), BNN)
    q_dec = q2 * egc
    v_new, o_state = [], []
    for s in range(n // chunk):
        rows = slice(s * chunk, (s + 1) * chunk)
        v_new.append(u[:, rows] - _bdot(w[:, rows], cur, BNN))
        o_state.append(_bdot(q_dec[:, rows], cur, BNN))
        g_last = gcum[:, (s + 1) * chunk - 1:(s + 1) * chunk, :]
        k_dec = k2[:, rows] * jnp.exp(g_last - gcum[:, rows])
        cur = cur * jnp.exp(g_last) + _bdot(k_dec, v_new[-1], BTN)
    o = jnp.concatenate(o_state, axis=1) + _bdot(per_value_head(_bdot(qh, kh, BNT)) * dmat, jnp.concatenate(v_new, axis=1), BNN)
    out = _rms(o, nw) * _silu(z2)
    res = (cur.reshape(state.shape), new_tail, jnp.concatenate([out[i] for i in range(nb)], axis=1))
    return res + (inv,) if keep else res


STATE_SHAPE = (2, LANES, LANES)
GDN_CHUNKS_PER_STEP = 2
INVERSE_BASE = 16


def _scan_specs(rows, consts, chunk, chunk_of, hb):
    specs = []
    for _, n, off, per_group in rows:
        if per_group:
            assert off % (n * hb) == 0
            specs.append(pl.BlockSpec((chunk, n * hb), lambda c, g, cb=off // (n * hb): (chunk_of(c), cb + g)))
        else:
            assert off % n == 0
            specs.append(pl.BlockSpec((chunk, n), lambda c, g, cb=off // n: (chunk_of(c), cb)))
    for arr, per_group in consts:
        if per_group:
            specs.append(pl.BlockSpec((hb, 1, arr.shape[2]), lambda c, g: (g, 0, 0)))
        else:
            specs.append(pl.BlockSpec(arr.shape, lambda c, g, nd=arr.ndim: (0,) * nd))
    return specs


def scan_fwd(name, step, chunk, n_grp, rows, consts, out_cols, hb, keep=None, state_shape=None, tail_cols=None):
    t = rows[0][0].shape[0]
    nc = t // chunk
    n_rows, n_consts = len(rows), len(consts)
    state_shape = state_shape or (hb,) + STATE_SHAPE
    carried = [state_shape] + ([(TAIL_ROWS, tail_cols)] if tail_cols else [])
    n_car = len(carried)

    def body(*refs):
        row_refs, const_refs = refs[:n_rows], refs[n_rows:n_rows + n_consts]
        y_ref = refs[n_rows + n_consts]
        saved_refs = refs[n_rows + n_consts + 1:n_rows + n_consts + 1 + n_car]
        scratch = refs[-n_car:]
        c, g = pl.program_id(0), pl.program_id(1)

        @pl.when(c == 0)
        def _():
            for s, shape in zip(scratch, carried):
                s[g] = jnp.zeros(shape, F32)

        cur = [s[g] for s in scratch]
        for r, v in zip(saved_refs, cur):
            r[...] = v
        vals = [r[...] for r in row_refs] + [r[...] for r in const_refs]
        res = step(g * hb, *cur, *vals) if keep is None else step(g * hb, *cur, *vals, keep=True)
        for s, v in zip(scratch, res[:n_car]):
            s[g] = v
        y_ref[...] = res[n_car].astype(y_ref.dtype)
        if keep is not None:
            refs[n_rows + n_consts + 1 + n_car][...] = res[n_car + 1]

    lead = (nc, n_grp // hb)
    out_specs = [pl.BlockSpec((chunk, out_cols * hb), lambda c, g: (c, g))]
    out_shape = [jax.ShapeDtypeStruct((t, n_grp * out_cols), BF16)]
    for shape in carried + ([keep] if keep is not None else []):
        out_specs.append(pl.BlockSpec((None, None) + shape, lambda c, g, nd=len(shape): (c, g) + (0,) * nd))
        out_shape.append(jax.ShapeDtypeStruct(lead + shape, F32))
    return pl.pallas_call(
        body, name=name, grid=lead,
        in_specs=_scan_specs(rows, consts, chunk, lambda c: c, hb),
        out_specs=out_specs, out_shape=out_shape,
        scratch_shapes=[pltpu.VMEM((n_grp // hb,) + shape, F32) for shape in carried],
        compiler_params=_params(("arbitrary", "arbitrary")),
    )(*[r[0] for r in rows], *[c[0] for c in consts])


def scan_bwd(name, step, chunk, n_grp, rows, consts, saved, dy, row_dtypes, hb, into, kept=None):
    t = rows[0][0].shape[0]
    nc = t // chunk
    n_rows, n_consts, n_car = len(rows), len(consts), len(saved)
    carried = [s.shape[2:] for s in saved]
    out_cols = dy.shape[1] // n_grp
    n_alias = sum(not isinstance(v, jax.ShapeDtypeStruct) for v in into.values())
    n_kept = 0 if kept is None else 1
    n_in = n_rows + n_consts + n_car + 1 + n_kept + n_alias

    def body(*refs):
        row_refs, const_refs = refs[:n_rows], refs[n_rows:n_rows + n_consts]
        saved_refs = refs[n_rows + n_consts:n_rows + n_consts + n_car]
        dy_ref = refs[n_rows + n_consts + n_car]
        outs = refs[n_in:-n_car]
        scratch = refs[-n_car:]
        c, g = pl.program_id(0), pl.program_id(1)

        @pl.when(c == 0)
        def _():
            for s, shape in zip(scratch, carried):
                s[g] = jnp.zeros(shape, F32)

        @pl.when(jnp.logical_and(c == 0, g == 0))
        def _():
            for r in outs[n_rows:]:
                r[...] = jnp.zeros(r.shape, r.dtype)

        f = functools.partial(step, g * hb) if kept is None else functools.partial(step, g * hb, kept=refs[n_rows + n_consts + n_car + 1][...])
        _, vjp = jax.vjp(f, *[r[...] for r in saved_refs], *[r[...] for r in row_refs], *[r[...] for r in const_refs])
        grads = vjp(tuple(s[g] for s in scratch) + (dy_ref[...].astype(F32),))
        for s, d in zip(scratch, grads[:n_car]):
            s[g] = d
        for (_, _, _, per_group), r, d in zip(rows, outs[:n_rows], grads[n_car:n_car + n_rows]):
            if per_group:
                r[...] = d.astype(r.dtype)
            else:
                @pl.when(g == 0)
                def _(r=r):
                    r[...] = jnp.zeros(r.shape, r.dtype)

                r[...] += d.astype(r.dtype)
        for (_, per_group), r, d in zip(consts, outs[n_rows:], grads[n_car + n_rows:]):
            if per_group:
                r[pl.ds(g * hb, hb)] += d
            else:
                r[...] += d

    rev = lambda c: nc - 1 - c
    out_specs, out_shape = [], []
    into_arrays, aliases = [], {}
    first_into = n_in - n_alias
    kept_arrays = [] if kept is None else [kept]
    by_step = lambda a: pl.BlockSpec((None, None) + a.shape[2:], lambda c, g, nd=a.ndim - 2: (rev(c), g) + (0,) * nd)
    for k, ((_, n, off, per_group), dt) in enumerate(zip(rows, row_dtypes)):
        if k in into:
            assert per_group and off % (n * hb) == 0 and into[k].dtype == dt
            out_specs.append(pl.BlockSpec((chunk, n * hb), lambda c, g, cb=off // (n * hb): (rev(c), cb + g)))
            out_shape.append(jax.ShapeDtypeStruct(into[k].shape, dt))
            if not isinstance(into[k], jax.ShapeDtypeStruct):
                aliases[first_into + len(into_arrays)] = k
                into_arrays.append(into[k])
        elif per_group:
            out_specs.append(pl.BlockSpec((chunk, n * hb), lambda c, g: (rev(c), g)))
            out_shape.append(jax.ShapeDtypeStruct((t, n_grp * n), dt))
        else:
            out_specs.append(pl.BlockSpec((chunk, n), lambda c, g: (rev(c), 0)))
            out_shape.append(jax.ShapeDtypeStruct((t, n), dt))
    for arr, _ in consts:
        out_specs.append(pl.BlockSpec(arr.shape, lambda c, g, nd=arr.ndim: (0,) * nd))
        out_shape.append(jax.ShapeDtypeStruct(arr.shape, F32))
    return pl.pallas_call(
        body, name=name, grid=(nc, n_grp // hb),
        in_specs=_scan_specs(rows, consts, chunk, rev, hb) + [by_step(s) for s in saved]
        + [pl.BlockSpec((chunk, out_cols * hb), lambda c, g: (rev(c), g))] + [by_step(k) for k in kept_arrays]
        + [pl.BlockSpec(memory_space=pl.ANY)] * len(into_arrays),
        out_specs=out_specs, out_shape=out_shape, input_output_aliases=aliases,
        scratch_shapes=[pltpu.VMEM((n_grp // hb,) + shape, F32) for shape in carried],
        compiler_params=_params(("arbitrary", "arbitrary")),
    )(*[r[0] for r in rows], *[c[0] for c in consts], *saved, dy, *kept_arrays, *into_arrays)


def _place():
    return lax.axis_index("x"), lax.axis_index("y"), lax.axis_index("c")


def _other_chips(x, y):
    return [(1 - x, y), (x, 1 - y), (1 - x, 1 - y)]


ANY = pl.BlockSpec(memory_space=pl.ANY)


def all_gather8(name, v):
    m_per, n = v.shape

    def body(x_ref, out_ref, send_sems, recv_sems, local_sem):
        x, y, c = _place()
        me, sibling = (x, y, c), (x, y, 1 - c)
        chips = _other_chips(x, y)

        def rows(px, py, pc):
            return out_ref.at[pl.ds((4 * px + 2 * py + pc) * m_per, m_per), :]

        def copy(k, block, to, src=None):
            return pltpu.make_async_remote_copy(
                src_ref=rows(*block) if src is None else src, dst_ref=rows(*block),
                send_sem=send_sems.at[k], recv_sem=recv_sems.at[k], device_id=to, device_id_type=MESH)

        mine = pltpu.make_async_copy(x_ref, rows(*me), local_sem)
        mine.start()
        first = [copy(0, me, sibling, src=x_ref)]
        first += [copy(1 + q, me, (*chip, c), src=x_ref) for q, chip in enumerate(chips)]
        for cp in first:
            cp.start()
        passed = [copy(4 + q, (*chip, c), sibling) for q, chip in enumerate(chips)]
        for q, chip in enumerate(chips):
            copy(1 + q, (*chip, c), me).wait_recv()
            passed[q].start()
        copy(0, sibling, me).wait_recv()
        for q, chip in enumerate(chips):
            copy(4 + q, (*chip, 1 - c), me).wait_recv()
        for cp in first + passed:
            cp.wait_send()
        mine.wait()

    return pl.pallas_call(
        body, name=name, out_shape=jax.ShapeDtypeStruct((N_DEV * m_per, n), v.dtype),
        in_specs=[pl.BlockSpec(memory_space=pltpu.VMEM)], out_specs=pl.BlockSpec(memory_space=pltpu.VMEM),
        scratch_shapes=[pltpu.SemaphoreType.DMA((7,)), pltpu.SemaphoreType.DMA((7,)), pltpu.SemaphoreType.DMA],
    )(v)


def gather_flat(name, vec):
    n = vec.shape[0]
    n_pad = -(-n // (8 * LANES)) * (8 * LANES)
    v = jnp.pad(vec, (0, n_pad - n)).reshape(8, n_pad // 8)
    return all_gather8(name, v).reshape(N_DEV, n_pad)[:, :n]


class Exchange:
    def __init__(self, operands, out_shapes, sem_shape, start, finish, middle=None):
        self.operands, self.out_shapes, self.sem_shape = list(operands), out_shapes, sem_shape
        self.start, self.middle, self.finish = start, middle or (lambda *refs: None), finish


def _start_all_wait_all(make_copies):
    def start(*refs):
        for cp in make_copies(*refs):
            cp.start()

    def finish(*refs):
        for cp in make_copies(*refs):
            cp.wait()

    return start, finish


def run_exchange(name, ex):
    n_in, n_out = len(ex.operands), len(ex.out_shapes)

    def body(*refs):
        ins, outs = refs[:n_in], refs[n_in:n_in + n_out]
        ex.start(ins, outs, *refs[n_in + n_out:])
        ex.middle(ins, outs, *refs[n_in + n_out:])
        ex.finish(ins, outs, *refs[n_in + n_out:])

    return pl.pallas_call(
        body, name=name, out_shape=ex.out_shapes, in_specs=[ANY] * n_in, out_specs=[ANY] * n_out,
        scratch_shapes=[pltpu.SemaphoreType.DMA(ex.sem_shape), pltpu.SemaphoreType.DMA(ex.sem_shape)],
    )(*ex.operands)


def _half(shape, axis, pc):
    h = shape[axis] // 2
    return (pl.ds(pc * h, h), slice(None)) if axis == 0 else (slice(None), pl.ds(pc * h, h))


def _half_shape(shape, axis):
    return tuple(s // 2 if a == axis else s for a, s in enumerate(shape))


def all_gather_shards(shards, axes):
    n_t = len(shards)
    n_sem = 12

    def copies(ins, outs, send_sems, recv_sems):
        x, y, c = _place()
        me, sibling, x_nbr, y_nbr = (x, y, c), (x, y, 1 - c), (1 - x, y, c), (x, 1 - y, c)
        own, of_x, of_y, of_diag = 2 * x + y, 2 * (1 - x) + y, 2 * x + 1 - y, 2 * (1 - x) + 1 - y

        def copy(t, k, quarter, pc, piece, to, from_input=False):
            axis = axes[t]
            h = ins[t].shape[axis] // 4
            cut = pl.ds((2 * pc + piece) * h, h)
            part = (cut, slice(None)) if axis == 0 else (slice(None), cut)
            dst = outs[t].at[(quarter,) + part]
            return pltpu.make_async_remote_copy(
                src_ref=ins[t].at[part] if from_input else dst, dst_ref=dst,
                send_sem=send_sems.at[t, k], recv_sem=recv_sems.at[t, k], device_id=to, device_id_type=MESH)

        stages = []
        for t in range(n_t):
            direct = [copy(t, 0, own, c, 0, x_nbr, True), copy(t, 2, own, c, 1, y_nbr, True),
                      copy(t, 1, own, c, 1, x_nbr, True), copy(t, 3, own, c, 0, y_nbr, True)]
            landing = [
                (copy(t, 0, of_x, c, 0, me), [copy(t, 4, of_x, c, 0, y_nbr), copy(t, 6, of_x, c, 0, sibling)]),
                (copy(t, 2, of_y, c, 1, me), [copy(t, 5, of_y, c, 1, x_nbr), copy(t, 8, of_y, c, 1, sibling)]),
                (copy(t, 1, of_x, c, 1, me), [copy(t, 7, of_x, c, 1, sibling)]),
                (copy(t, 3, of_y, c, 0, me), [copy(t, 9, of_y, c, 0, sibling)]),
                (copy(t, 4, of_diag, c, 0, me), [copy(t, 10, of_diag, c, 0, sibling)]),
                (copy(t, 5, of_diag, c, 1, me), [copy(t, 11, of_diag, c, 1, sibling)])]
            from_sibling = [copy(t, 6, of_x, 1 - c, 0, me), copy(t, 8, of_y, 1 - c, 1, me), copy(t, 7, of_x, 1 - c, 1, me),
                            copy(t, 9, of_y, 1 - c, 0, me), copy(t, 10, of_diag, 1 - c, 0, me), copy(t, 11, of_diag, 1 - c, 1, me)]
            stages.append((direct, landing, from_sibling))
        return stages

    def start(*refs):
        for direct, _, _ in copies(*refs):
            for cp in direct:
                cp.start()

    def pass_on(landing):
        for arrived, onward in landing:
            arrived.wait_recv()
            for cp in onward:
                cp.start()

    def middle(*refs):
        for _, landing, _ in copies(*refs):
            pass_on(landing[:4])

    def finish(*refs):
        stages = copies(*refs)
        for _, landing, _ in stages:
            pass_on(landing[4:])
        for direct, landing, from_sibling in stages:
            for cp in from_sibling:
                cp.wait_recv()
            for cp in direct + [cp for _, onward in landing for cp in onward]:
                cp.wait_send()

    return Exchange(shards, [jax.ShapeDtypeStruct((4,) + s.shape, s.dtype) for s in shards], (n_t, n_sem), start, finish, middle)


def exchange_halves_d2d(grads, axes):
    n_t = len(grads)

    def copies(ins, outs, send_sems, recv_sems):
        x, y, c = _place()
        return [pltpu.make_async_remote_copy(
            src_ref=ins[t].at[(slice(None),) + _half(ins[t].shape[1:], axes[t], 1 - c)], dst_ref=outs[t],
            send_sem=send_sems.at[t], recv_sem=recv_sems.at[t], device_id=(x, y, 1 - c), device_id_type=MESH) for t in range(n_t)]

    shapes = [jax.ShapeDtypeStruct((4,) + _half_shape(g.shape[1:], a), g.dtype) for g, a in zip(grads, axes)]
    return Exchange(grads, shapes, (n_t,), *_start_all_wait_all(copies))


def exchange_quarters_ici(parts):
    n_t = len(parts)

    def copies(ins, outs, send_sems, recv_sems):
        x, y, c = _place()
        return [pltpu.make_async_remote_copy(
            src_ref=ins[t].at[2 * px + py], dst_ref=outs[t].at[q],
            send_sem=send_sems.at[t, q], recv_sem=recv_sems.at[t, q], device_id=(px, py, c), device_id_type=MESH)
            for t in range(n_t) for q, (px, py) in enumerate(_other_chips(x, y))]

    shapes = [jax.ShapeDtypeStruct((3,) + p.shape[1:], p.dtype) for p in parts]
    return Exchange(parts, shapes, (n_t, 3), *_start_all_wait_all(copies))


def swap_d2d(halves):
    n_t = len(halves)

    def copies(ins, outs, send_sems, recv_sems):
        x, y, c = _place()
        return [pltpu.make_async_remote_copy(
            src_ref=ins[t], dst_ref=outs[t], send_sem=send_sems.at[t], recv_sem=recv_sems.at[t],
            device_id=(x, y, 1 - c), device_id_type=MESH) for t in range(n_t)]

    return Exchange(halves, [jax.ShapeDtypeStruct(h.shape, h.dtype) for h in halves], (n_t,), *_start_all_wait_all(copies))


BLOCK_BYTES = 1 << 20


def _row_block(r, c):
    fits = [rb for rb in range(16, r + 1, 16) if r % rb == 0 and rb * c * 4 <= BLOCK_BYTES]
    return max(fits) if fits else r


def _place_scalars():
    x, y, c = _place()
    return jnp.stack([c, 2 * x + y]).astype(jnp.int32)


def reduce_on_chip(tag, grads, axes):
    from_sibling = run_exchange(f"rs_d2d_{tag}", exchange_halves_d2d(grads, axes))
    parts, parts_bf16 = [], []
    for t, (g, s, axis) in enumerate(zip(grads, from_sibling, axes)):
        _, h, cols = s.shape
        rb = _row_block(h, cols)
        nb = h // rb
        blk = lambda k, i, s_ref: (k, i, 0)
        mine = (lambda k, i, s_ref, nb=nb: (k, s_ref[0] * nb + i, 0)) if axis == 0 else (lambda k, i, s_ref: (k, i, s_ref[0]))
        p32, p16 = blockmap(
            f"rs_add_{tag}{t}", lambda a, b: (a + b, a + b), (4, nb),
            [(g, (None, rb, cols), mine), (s, (None, rb, cols), blk)],
            [(s.shape, F32, (None, rb, cols), blk), (s.shape, BF16, (None, rb, cols), blk)], scalars=_place_scalars())
        parts.append(p32)
        parts_bf16.append(p16)
    return parts, parts_bf16


def reduce_across_chips(parts, from_chips):
    halves = []
    for t, (p, q) in enumerate(zip(parts, from_chips)):
        _, h, cols = p.shape
        rb = _row_block(h, cols)
        halves.append(blockmap(
            f"rs_sum{t}", lambda a, b: a + b[0].astype(F32) + b[1].astype(F32) + b[2].astype(F32), (h // rb,),
            [(p, (None, rb, cols), lambda i, s_ref: (s_ref[1], i, 0)), (q, (3, rb, cols), lambda i, s_ref: (0, i, 0))],
            [((h, cols), F32, (rb, cols), lambda i, s_ref: (i, 0))], scalars=_place_scalars())[0])
    return list(zip(halves, run_exchange("rs_swap", swap_d2d(halves))))


def _adamw(w, g, m, v):
    m = ADAM_B1 * m + (1.0 - ADAM_B1) * g
    v = ADAM_B2 * v + (1.0 - ADAM_B2) * jnp.square(g)
    m_hat = m / (1.0 - ADAM_B1 ** ADAM_STEP)
    v_hat = v / (1.0 - ADAM_B2 ** ADAM_STEP)
    delta = -ADAM_LR * (m_hat / (jnp.sqrt(v_hat) + ADAM_EPS) + ADAM_WD * w)
    return delta, m, v


def adamw(name, w, g, m, v):
    _, r, c = w.shape
    rb = _row_block(r, c)
    blk3 = lambda a: (a, (None, rb, c), lambda i: (0, i, 0))
    return blockmap(name, _adamw, (r // rb,), [blk3(w), (g, (rb, c), lambda i: (i, 0)), blk3(m), blk3(v)],
                    [(w.shape, F32, (None, rb, c), lambda i: (0, i, 0))] * 3)


def adamw_halves(name, w, mine, other, m, v, axis):
    _, r, c = w.shape
    h, c = mine.shape
    rb = _row_block(h, c)
    nb = h // rb

    def body(s_ref, w_ref, mine_ref, other_ref, m_ref, v_ref, g_out, d_out, m_out, v_out):
        g = jnp.where(pl.program_id(0) == s_ref[0], mine_ref[...], other_ref[...])
        d, nm, nv = _adamw(w_ref[...], g, m_ref[...], v_ref[...])
        g_out[...], d_out[...], m_out[...], v_out[...] = g, d, nm, nv

    spec3 = pl.BlockSpec((None, rb, c), (lambda k, i, s_ref: (0, k * nb + i, 0)) if axis == 0 else (lambda k, i, s_ref: (0, i, k)))
    spec2 = pl.BlockSpec((rb, c), lambda k, i, s_ref: (i, 0))
    grid_spec = pltpu.PrefetchScalarGridSpec(num_scalar_prefetch=1, grid=(2, nb), in_specs=[spec3, spec2, spec2, spec3, spec3],
                                             out_specs=[spec3] * 4)
    return pl.pallas_call(body, name=name, grid_spec=grid_spec, out_shape=[jax.ShapeDtypeStruct(w.shape, F32)] * 4,
                          compiler_params=_params(("parallel", "parallel")))(_place_scalars(), w, mine, other, m, v)


def _whole(name, fn, ins, outs):
    return blockmap(name, fn, (1,), [(a, a.shape, lambda i, nd=a.ndim: (0,) * nd) for a in ins],
                    [(s, d, s, lambda i, nd=len(s): (0,) * nd) for s, d in outs])


def _premix(x, w, sc, sh):
    return _rms(x, w) * (1.0 + sc) + sh


def _postmix(x, u, w_post, g1, w_pre2, sc2, sh2):
    x1 = x + g1 * _rms(u, w_post)
    return x1, _premix(x1, w_pre2, sc2, sh2)


def _merge(gs, gg, ys, yg):
    return _sigmoid(gs) * ys + _sigmoid(gg) * yg


def _final(x1, y2, w_post2, g2):
    return x1 + g2 * _rms(y2, w_post2)


def kernel(x, c, w_ada, b_ada, norm_mix_pre, norm_mix_post, w_in, ssm_conv_w, ssm_conv_b, ssm_dt_bias, ssm_A_log, ssm_D, ssm_norm_w, gdn_conv_w, gdn_dt_bias, gdn_A_log, gdn_norm_w, w_ssm_up, w_gdn_up, w_out, norm_mlp_pre, norm_mlp_post, w_mlp_up, w_mlp_down, loss_target, m_w_ada, m_b_ada, m_norm_mix_pre, m_norm_mix_post, m_w_in, m_ssm_conv_w, m_ssm_conv_b, m_ssm_dt_bias, m_ssm_A_log, m_ssm_D, m_ssm_norm_w, m_gdn_conv_w, m_gdn_dt_bias, m_gdn_A_log, m_gdn_norm_w, m_w_ssm_up, m_w_gdn_up, m_w_out, m_norm_mlp_pre, m_norm_mlp_post, m_w_mlp_up, m_w_mlp_down, v_w_ada, v_b_ada, v_norm_mix_pre, v_norm_mix_post, v_w_in, v_ssm_conv_w, v_ssm_conv_b, v_ssm_dt_bias, v_ssm_A_log, v_ssm_D, v_ssm_norm_w, v_gdn_conv_w, v_gdn_dt_bias, v_gdn_A_log, v_gdn_norm_w, v_w_ssm_up, v_w_gdn_up, v_w_out, v_norm_mlp_pre, v_norm_mlp_post, v_w_mlp_up, v_w_mlp_down):
    args = dict(locals())
    xi, yi, ci = _place()
    quarter = 2 * xi + yi
    batch = 4 * xi + 2 * yi + ci

    xt, target = x[0], loss_target[0]
    t, d = xt.shape
    hs, hv = ssm_dt_bias.shape[-1], gdn_dt_bias.shape[-1]
    assert hs <= LANE_B - LANE_DT and hv <= LANE_A - LANE_B and hv % 2 == 0 and hs % SSM_HEADS_PER_GROUP == 0
    assert t % SSM_CHUNK == 0 and t % (GDN_CHUNK * GDN_CHUNKS_PER_STEP) == 0 and d % LANES == 0
    d_inner = hs * SSM_HEAD_DIM
    n_grp = hs // SSM_HEADS_PER_GROUP
    gn = n_grp * SSM_D_STATE
    conv_ssm = d_inner + 2 * gn
    hq = hv // 2
    key, val = hq * GDN_HEAD, hv * GDN_HEAD
    conv_gdn = 2 * key + val
    o_dt = d_inner + conv_ssm
    o_qkv = o_dt + hs
    o_b = o_qkv + conv_gdn + val
    o_a = o_b + hv
    o_gs = o_a + hv
    n_proj = o_gs + 2 * d
    assert 4 * w_in.shape[-1] == n_proj and ssm_conv_w.shape[-1] * 4 == conv_ssm == conv_gdn
    a_z, a_q = 0, o_dt
    a_gs = a_q + conv_gdn + val
    a_gg = a_gs + d
    a_small = a_gg + d
    n_al = -(-(a_small + LANES) // MM_TILE_N) * MM_TILE_N

    def to_aligned(w):
        z = lambda n: jnp.zeros((n, w.shape[1]), w.dtype)
        return jnp.concatenate([
            w[:o_dt], w[o_qkv:o_b], w[o_gs:],
            w[o_dt:o_qkv], z(LANE_B - hs), w[o_b:o_a], z(LANE_A - LANE_B - hv), w[o_a:o_gs], z(LANES - LANE_A - hv),
            z(n_al - a_small - LANES)], axis=0)

    def from_aligned(w):
        s = a_small
        return jnp.concatenate([
            w[:o_dt], w[s + LANE_DT:s + LANE_DT + hs], w[a_q:a_gs], w[s + LANE_B:s + LANE_B + hv],
            w[s + LANE_A:s + LANE_A + hv], w[a_gs:a_small]], axis=0)

    def lanes(vec, at):
        return jnp.zeros((1, LANES), F32).at[:, at:at + vec.shape[-1]].set(vec.reshape(1, -1))

    n_cw = CONV_K * ssm_conv_w.shape[-1]
    small_in = gather_flat("ag_small", jnp.concatenate([c.reshape(-1), ssm_conv_w.reshape(-1), gdn_conv_w.reshape(-1)]))
    c_all = small_in[:, :d]
    by_chip = small_in[0::2]

    def whole_conv_w(lo):
        return jnp.transpose(by_chip[:, lo:lo + n_cw].reshape(4, CONV_K, -1), (1, 0, 2)).reshape(CONV_K, -1)

    cw_ssm, cw_gdn = whole_conv_w(d), whole_conv_w(d + n_cw)
    cb_ssm = ssm_conv_b

    n_ada = w_ada.shape[-1]
    b_q = lax.dynamic_slice_in_dim(b_ada, quarter * n_ada, n_ada, axis=1)
    mod_q = _whole("ada_fwd", lambda ca, w, b: _bdot(_silu(ca), w) + b, [c_all, w_ada[0], b_q], [((N_DEV, n_ada), F32)])[0]
    mod_all = gather_flat("ag_mod", mod_q.reshape(-1)).reshape(N_DEV, N_DEV, n_ada)[0::2]
    mod = lax.dynamic_index_in_dim(mod_all, batch, axis=1, keepdims=False).reshape(1, 4 * n_ada)
    sh1, sc1, g1, sh2, sc2, g2 = [mod[:, i * d:(i + 1) * d] for i in range(6)]

    transposed = lambda a: jnp.swapaxes(a, 1, 2)
    own = [w.astype(BF16) for w in (transposed(w_in)[0], w_ssm_up[0], w_gdn_up[0], w_out[0], w_mlp_up[0], w_mlp_down[0])]
    with_own = lambda gs, ws: [lax.dynamic_update_index_in_dim(g, w, quarter, 0) for g, w in zip(gs, ws)]
    cols_major = lambda g: jnp.transpose(g, (1, 0, 2)).reshape(g.shape[1], -1)
    rows_major = lambda g: g.reshape(-1, g.shape[2])
    wb_in = to_aligned(rows_major(with_own(run_exchange("ag_w_in", all_gather_shards(own[:1], [1])), own[:1])[0]))

    h1 = rowmap("premix", _premix, [xt], [norm_mix_pre, sc1, sh1], [(d, BF16)])[0]
    proj, gathered = matmul("in_proj", h1, wb_in, tb=True, comm=all_gather_shards(own[1:], [0] * 5))
    gathered = with_own(gathered, own[1:])
    wb_ssm_up, wb_gdn_up, wb_out = rows_major(gathered[0]), rows_major(gathered[1]), rows_major(gathered[2])
    wb_up, wb_down = cols_major(gathered[3]), rows_major(gathered[4])

    wide = 2 * LANES
    ssd_rows = [(proj, d_inner + conv_ssm, a_z, True), (proj, LANES, a_small, False)]
    ssd_consts = [(cw_ssm, False), (cb_ssm, False), (lanes(ssm_dt_bias, LANE_DT), False), (lanes(ssm_A_log, LANE_DT), False),
                  (lanes(ssm_D, LANE_DT), False), (ssm_norm_w.reshape(n_grp, 1, wide), False)]
    y_ssm_n, st_ssm, tails_ssm = scan_fwd("ssd_fwd", ssd_step, SSM_CHUNK, 1, ssd_rows, ssd_consts, d_inner, 1,
                                          state_shape=(n_grp,) + STATE_SHAPE, tail_cols=conv_ssm)
    gdn_rows = [(proj, conv_gdn + val, a_q, True), (proj, LANES, a_small, False)]
    gdn_consts = [(cw_gdn, False), (lanes(gdn_dt_bias, LANE_A), False), (lanes(gdn_A_log, LANE_A), False), (gdn_norm_w, False)]
    gdn_rows_per_step = GDN_CHUNK * GDN_CHUNKS_PER_STEP
    y_gdn_n, st_gdn, tails_gdn, inv_gdn = scan_fwd(
        "gdn_fwd", gdn_step, gdn_rows_per_step, 1, gdn_rows, gdn_consts, val, 1, keep=(hv, gdn_rows_per_step, gdn_rows_per_step),
        state_shape=(hq,) + STATE_SHAPE, tail_cols=conv_gdn)

    y_ssm = matmul("ssm_up", y_ssm_n, wb_ssm_up)
    y_gdn = matmul("gdn_up", y_gdn_n, wb_gdn_up)
    gates = [(proj, d, a_gs), (proj, d, a_gg)]
    merged = rowmap("merge", _merge, gates + [y_ssm, y_gdn], [], [(d, BF16)])[0]
    u = matmul("w_out", merged, wb_out)
    post_consts = [norm_mix_post, g1, norm_mlp_pre, sc2, sh2]
    x1, h2 = rowmap("postmix", _postmix, [xt, u], post_consts, [(d, F32), (d, BF16)])
    relu2 = lambda acc: (acc, jnp.square(jnp.maximum(acc, 0.0)))
    a_up, act = matmul("mlp_up", h2, wb_up, out_dtypes=(BF16, BF16), epi=relu2)
    y2 = matmul("mlp_down", act, wb_down)

    def final_bwd(x1_, y2_, tgt, w_, g_):
        x2, vjp = jax.vjp(_final, x1_, y2_, w_, g_)
        err = x2 - tgt
        loss = 0.5 * jnp.sum(jnp.mean(err * err, axis=-1, keepdims=True), axis=0, keepdims=True)
        dx1, dy2, dw, dg = vjp(err / d)
        return dx1, dy2, loss, dw, dg

    dx1, dy2, loss_part, d_norm_mlp_post, dg2 = rowmap(
        "final", final_bwd, [x1, y2, target], [norm_mlp_post, g2], [(d, F32), (d, BF16)], [((1, 1), F32), ((1, d), F32), ((1, d), F32)])

    d_a = matmul("mlp_down_dx", dy2, wb_down, tb=True, out_dtypes=(BF16,), extras=[a_up],
                 epi=lambda acc, a: acc * 2.0 * jnp.maximum(a.astype(F32), 0.0))
    gw_down = matmul("mlp_down_dw", act, dy2, ta=True)
    dh2 = matmul("mlp_up_dx", d_a, wb_up, tb=True)
    gw_up = matmul("mlp_up_dw", h2, d_a, ta=True)

    def postmix_bwd(x_, u_, dx1_, dh2_, *cs):
        _, vjp = jax.vjp(_postmix, x_, u_, *cs)
        return vjp((dx1_, dh2_))

    dxa, du, d_norm_mix_post, dg1, d_norm_mlp_pre, dsc2, dsh2 = rowmap(
        "postmix_bwd", postmix_bwd, [xt, u, dx1, dh2], post_consts, [(d, F32), (d, BF16)], [((1, d), F32)] * 5)
    d_merged = matmul("w_out_dx", du, wb_out, tb=True)
    gw_out = matmul("w_out_dw", merged, du, ta=True)

    def merge_bwd(gs, gg, ys, yg, dm):
        _, vjp = jax.vjp(_merge, gs, gg, ys, yg)
        dgs, dgg, dys, dyg = vjp(dm)
        return dys, dyg, jnp.concatenate([dgs, dgg], axis=1)

    dy_ssm, dy_gdn, dproj = rowmap("merge_bwd", merge_bwd, gates + [y_ssm, y_gdn, d_merged], [],
                                   [(d, BF16), (d, BF16), (2 * d, BF16, jax.ShapeDtypeStruct((t, n_al), BF16), a_gs)])
    dy_ssm_n = matmul("ssm_up_dx", dy_ssm, wb_ssm_up, tb=True, out_dtypes=(BF16,))
    gw_ssm_up = matmul("ssm_up_dw", y_ssm_n, dy_ssm, ta=True)
    dy_gdn_n = matmul("gdn_up_dx", dy_gdn, wb_gdn_up, tb=True, out_dtypes=(BF16,))
    gw_gdn_up = matmul("gdn_up_dw", y_gdn_n, dy_gdn, ta=True)

    dproj, dsmall_ssm, dcw_ssm, dcb_ssm, d_sdtb, d_salog, d_sdsk, d_snw = scan_bwd(
        "ssd_bwd", ssd_step, SSM_CHUNK, 1, ssd_rows, ssd_consts, [st_ssm, tails_ssm], dy_ssm_n, [BF16, F32], 1, {0: dproj})
    dproj, dsmall_gdn, dcw_gdn, d_gdtb, d_galog, d_gnw = scan_bwd(
        "gdn_bwd", gdn_step, gdn_rows_per_step, 1, gdn_rows, gdn_consts, [st_gdn, tails_gdn], dy_gdn_n, [BF16, F32], 1,
        {0: dproj}, kept=inv_gdn)
    tail = n_al - a_small
    dproj = rowmap("small_sum", lambda a, b: jnp.concatenate([a + b, jnp.zeros((a.shape[0], tail - LANES), F32)], axis=1),
                   [dsmall_ssm, dsmall_gdn], [], [(tail, BF16, dproj, a_small)])[0]
    quarters_cols = lambda g: jnp.transpose(g.reshape(g.shape[0], 4, -1), (1, 0, 2))
    quarters_rows = lambda g: g.reshape(4, g.shape[0] // 4, g.shape[1])
    rest32, rest16 = reduce_on_chip("rest", [quarters_rows(gw_ssm_up), quarters_rows(gw_gdn_up), quarters_rows(gw_out),
                                             quarters_cols(gw_up), quarters_rows(gw_down)], [0] * 5)
    gw_in_al, rest_chips = matmul("in_proj_dw", dproj, h1, ta=True, comm=exchange_quarters_ici(rest16))
    in32, in16 = reduce_on_chip("in", [quarters_rows(from_aligned(gw_in_al))], [1])
    dh1, in_chips = matmul("in_proj_dx", dproj, wb_in, comm=exchange_quarters_ici(in16))

    def premix_bwd(x_, dxa_, dh1_, w_, sc_, sh_):
        _, vjp = jax.vjp(_premix, x_, w_, sc_, sh_)
        dx, dw, dsc, dsh = vjp(dh1_)
        return dx + dxa_, dw, dsc, dsh

    grad_x, d_norm_mix_pre, dsc1, dsh1 = rowmap(
        "premix_bwd", premix_bwd, [xt, dxa, dh1], [norm_mix_pre, sc1, sh1], [(d, F32)], [((1, d), F32)] * 3)

    dmod_all = gather_flat("ag_dmod", jnp.concatenate([dsh1, dsc1, dg1, dsh2, dsc2, dg2], axis=1).reshape(-1))
    dmod_q = lax.dynamic_slice_in_dim(dmod_all, quarter * n_ada, n_ada, axis=1)
    gw_ada, gb_ada = _whole(
        "ada_bwd", lambda ca, dq_, da_: (_bdot(_silu(ca), dq_, TN), jnp.sum(da_, axis=0, keepdims=True)),
        [c_all, dmod_q, dmod_all], [((d, n_ada), F32), ((1, 4 * n_ada), F32)])

    partial = [d_norm_mix_pre, d_norm_mix_post, dcw_ssm, dcb_ssm, d_sdtb[:, LANE_DT:LANE_DT + hs], d_salog[:, LANE_DT:LANE_DT + hs],
               d_sdsk[:, LANE_DT:LANE_DT + hs], d_snw, dcw_gdn, d_gdtb[:, LANE_A:LANE_A + hv], d_galog[:, LANE_A:LANE_A + hv], d_gnw,
               d_norm_mlp_pre, d_norm_mlp_post, loss_part]
    sizes = [p.size for p in partial]
    stacked = gather_flat("ag_grads", jnp.concatenate([p.reshape(-1) for p in partial]))
    summed = _whole("small_sum8", lambda s: jnp.sum(s, axis=0, keepdims=True), [stacked], [((1, stacked.shape[1]), F32)])[0][0]
    offs = [0]
    for s in sizes:
        offs.append(offs[-1] + s)
    red = [summed[offs[i]:offs[i + 1]] for i in range(len(sizes))]
    loss = red[-1][0]
    my_cols = lambda full: lax.dynamic_slice_in_dim(full.reshape(CONV_K, -1), quarter * (n_cw // CONV_K), n_cw // CONV_K, axis=1)
    small_grads = {
        "b_ada": gb_ada, "norm_mix_pre": red[0], "norm_mix_post": red[1], "ssm_conv_w": my_cols(red[2]), "ssm_conv_b": red[3],
        "ssm_dt_bias": red[4], "ssm_A_log": red[5], "ssm_D": red[6], "ssm_norm_w": red[7], "gdn_conv_w": my_cols(red[8]),
        "gdn_dt_bias": red[9], "gdn_A_log": red[10], "gdn_norm_w": red[11], "norm_mlp_pre": red[12], "norm_mlp_post": red[13]}

    big_names = ["w_in", "w_ssm_up", "w_gdn_up", "w_out", "w_mlp_up", "w_mlp_down"]
    big_grads = dict(zip(big_names, reduce_across_chips(in32 + rest32, in_chips + rest_chips)))

    names = ['w_ada', 'b_ada', 'norm_mix_pre', 'norm_mix_post', 'w_in', 'ssm_conv_w', 'ssm_conv_b', 'ssm_dt_bias', 'ssm_A_log', 'ssm_D',
             'ssm_norm_w', 'gdn_conv_w', 'gdn_dt_bias', 'gdn_A_log', 'gdn_norm_w', 'w_ssm_up', 'w_gdn_up', 'w_out', 'norm_mlp_pre',
             'norm_mlp_post', 'w_mlp_up', 'w_mlp_down']
    grad, delta, new_m, new_v = {}, {}, {}, {}
    for n, (mine, other) in big_grads.items():
        view, axis = (transposed, 1) if n == "w_in" else ((lambda a: a), 0)
        res = adamw_halves("adamw_" + n, view(args[n]), mine, other, view(args["m_" + n]), view(args["v_" + n]), axis)
        grad[n], delta[n], new_m[n], new_v[n] = [view(a) for a in res]
    grad["w_ada"] = gw_ada.reshape(w_ada.shape)
    delta["w_ada"], new_m["w_ada"], new_v["w_ada"] = adamw("adamw_w_ada", w_ada, gw_ada, m_w_ada, v_w_ada)
    small_names = [n for n in names if n not in grad]
    flat = lambda pre: jnp.concatenate([args[pre + n].reshape(-1) for n in small_names]).reshape(1, 1, -1)
    g_flat = jnp.concatenate([small_grads[n].reshape(-1) for n in small_names]).reshape(1, -1)
    dl, nm, nv = adamw("adamw_small", flat(""), g_flat, flat("m_"), flat("v_"))
    off = 0
    for n in small_names:
        shape = args[n].shape
        size = args[n].size
        grad[n], delta[n], new_m[n], new_v[n] = [a.reshape(-1)[off:off + size].reshape(shape) for a in (g_flat, dl, nm, nv)]
        off += size

    return (loss, grad_x.reshape(x.shape), *[grad[n] for n in names], *[delta[n] for n in names],
            *[new_m[n] for n in names], *[new_v[n] for n in names])
```

```python
import functools

import jax
import jax.numpy as jnp
from jax import lax
from jax.experimental import pallas as pl
from jax.experimental.pallas import tpu as pltpu

F32 = jnp.float32
BF16 = jnp.bfloat16
MESH = pl.DeviceIdType.MESH

EPS = 1e-6
SSM_HEAD_DIM = 64
SSM_HEADS_PER_GROUP = 4
SSM_D_STATE = 128
SSM_CHUNK = 128
GDN_HEAD = 128
GDN_CHUNK = 64
CONV_K = 4
TAIL_ROWS = 8
LANE_DT, LANE_B, LANE_A = 0, 32, 48
ADAM_LR, ADAM_B1, ADAM_B2, ADAM_EPS, ADAM_WD, ADAM_STEP = 0.001, 0.9, 0.999, 1e-08, 0.01, 10

VMEM_LIMIT_BYTES = 56 * 1024 * 1024
LANES = 128
N_DEV = 8

NN = (((1,), (0,)), ((), ()))
NT = (((1,), (1,)), ((), ()))
TN = (((0,), (0,)), ((), ()))


BNN = (((2,), (1,)), ((0,), (0,)))
BNT = (((2,), (2,)), ((0,), (0,)))
BTN = (((1,), (1,)), ((0,), (0,)))
_KIND = {NN: ("NN", 0), NT: ("NT", 0), TN: ("TN", 0), BNN: ("NN", 1), BNT: ("NT", 1), BTN: ("TN", 1)}
_DIMS = {"NN": (NN, BNN), "NT": (NT, BNT), "TN": (TN, BTN)}


def _dg(a, b, dims):
    return lax.dot_general(a, b, dims, preferred_element_type=F32)


def _raw_bf16(a, b, dims):
    return _dg(a.astype(BF16), b.astype(BF16), dims)


def _raw_bf16x3(a, b, dims):
    ah, bh = a.astype(BF16), b.astype(BF16)
    al, bl = (a - ah.astype(F32)).astype(BF16), (b - bh.astype(F32)).astype(BF16)
    return _dg(ah, bh, dims) + (_dg(ah, bl, dims) + _dg(al, bh, dims))


def _make_dot(raw):
    @functools.partial(jax.custom_vjp, nondiff_argnums=(2,))
    def dot(a, b, dims):
        return raw(a, b, dims)

    def fwd(a, b, dims):
        return raw(a, b, dims), (a, b)

    def bwd(dims, res, ct):
        a, b = res
        kind, batched = _KIND[dims]
        d = lambda k: _DIMS[k][batched]
        if kind == "NN":
            da, db = raw(ct, b, d("NT")), raw(a, ct, d("TN"))
        elif kind == "NT":
            da, db = raw(ct, b, d("NN")), raw(ct, a, d("TN"))
        else:
            da, db = raw(b, ct, d("NT")), raw(a, ct, d("NN"))
        return da.astype(a.dtype), db.astype(b.dtype)

    dot.defvjp(fwd, bwd)
    return lambda a, b, dims=NN: dot(a, b, dims)


_bdot = _make_dot(_raw_bf16)
_hdot = _make_dot(_raw_bf16x3)


def _mask_dot(mask, x, dims):
    m = mask.astype(BF16)
    hi = x.astype(BF16)
    r = x - hi.astype(F32)
    mid = r.astype(BF16)
    lo = (r - mid.astype(F32)).astype(BF16)
    return sum(_dg(m, p, dims) for p in (hi, mid, lo))


def _sigmoid(x):
    return 0.5 * jnp.tanh(0.5 * x) + 0.5


def _silu(x):
    return x * _sigmoid(x)


def _softplus(x):
    return jnp.maximum(x, 0.0) + jnp.log(1.0 + jnp.exp(-jnp.abs(x)))


def _rms(x, w):
    return x * lax.rsqrt(jnp.mean(x * x, axis=-1, keepdims=True) + EPS) * w


def _lane_col(m, idx):
    lane = lax.broadcasted_iota(jnp.int32, m.shape, 1)
    return jnp.sum(jnp.where(lane == idx, m, 0.0), axis=1, keepdims=True)


def _tril(n, strict=False, seg=None):
    r = lax.broadcasted_iota(jnp.int32, (n, n), 0)
    c = lax.broadcasted_iota(jnp.int32, (n, n), 1)
    low = (r > c) if strict else (r >= c)
    if seg is None or seg >= n:
        return low
    shift = seg.bit_length() - 1
    return jnp.logical_and(low, (r >> shift) == (c >> shift))


@functools.partial(jax.custom_vjp, nondiff_argnums=(1,))
def _cumsum_rows(x, seg):
    return _mask_dot(_tril(x.shape[0], seg=seg), x, NN)


def _cumsum_rows_fwd(x, seg):
    return _cumsum_rows(x, seg), None


def _cumsum_rows_bwd(seg, _, ct):
    return (_mask_dot(_tril(ct.shape[0], seg=seg), ct, TN),)


_cumsum_rows.defvjp(_cumsum_rows_fwd, _cumsum_rows_bwd)


def _head_rows(m_t, idx):
    sub = lax.broadcasted_iota(jnp.int32, m_t.shape, 0)
    return jnp.sum(jnp.where(sub == idx, m_t, 0.0), axis=0, keepdims=True)


def _params(sem):
    return pltpu.CompilerParams(dimension_semantics=sem, vmem_limit_bytes=VMEM_LIMIT_BYTES)


def _into_plumbing(outs, first_input):
    arrays, aliases = [], {}
    for k, o in enumerate(outs):
        if len(o) > 4 and not isinstance(o[4], jax.ShapeDtypeStruct):
            aliases[first_input + len(arrays)] = k
            arrays.append(o[4])
    return arrays, aliases


def blockmap(name, fn, grid, ins, outs, accs=(), scalars=None):
    n_in, n_out, n_acc = len(ins), len(outs), len(accs)
    n_grid = len(grid)
    n_pre = 0 if scalars is None else 1
    into_arrays, aliases = _into_plumbing(outs, n_pre + n_in)
    n_into = len(into_arrays)

    def body(*refs):
        refs = refs[n_pre:n_pre + n_in] + refs[n_pre + n_in + n_into:]
        vals = fn(*[r[...] for r in refs[:n_in]])
        if not isinstance(vals, (tuple, list)):
            vals = (vals,)
        for r, v in zip(refs[n_in:n_in + n_out], vals[:n_out]):
            r[...] = v.astype(r.dtype)
        if n_acc:
            first = functools.reduce(jnp.logical_and, [pl.program_id(a) == 0 for a in range(n_grid)])
            acc_refs = refs[n_in + n_out:]

            @pl.when(first)
            def _():
                for r in acc_refs:
                    r[...] = jnp.zeros(r.shape, r.dtype)

            for r, v in zip(acc_refs, vals[n_out:]):
                r[...] += v.astype(r.dtype)

    zeros = lambda nd: (lambda *_: (0,) * nd)
    in_specs = [pl.BlockSpec(b, im) for _, b, im in ins] + [pl.BlockSpec(memory_space=pl.ANY)] * n_into
    out_specs = [pl.BlockSpec(o[2], o[3]) for o in outs] + [pl.BlockSpec(s, zeros(len(s))) for s, _ in accs]
    out_shape = [jax.ShapeDtypeStruct(o[0], o[1]) for o in outs] + [jax.ShapeDtypeStruct(s, d) for s, d in accs]
    cparams = _params(("arbitrary",) * n_grid if n_acc else ("parallel",) * n_grid)
    arrays = [a for a, _, _ in ins] + into_arrays
    if scalars is None:
        return pl.pallas_call(body, name=name, grid=grid, in_specs=in_specs, out_specs=out_specs, out_shape=out_shape,
                              input_output_aliases=aliases, compiler_params=cparams)(*arrays)
    spec = pltpu.PrefetchScalarGridSpec(num_scalar_prefetch=1, grid=grid, in_specs=in_specs, out_specs=out_specs)
    return pl.pallas_call(body, name=name, grid_spec=spec, out_shape=out_shape, input_output_aliases=aliases,
                          compiler_params=cparams)(scalars, *arrays)


def rowmap(name, fn, rows, consts, outs, accs=(), rb=512):
    norm = [(r, r.shape[1], 0) if not isinstance(r, tuple) else (r[0], r[1], r[2] // r[1]) for r in rows]
    assert all(not isinstance(r, tuple) or r[2] % r[1] == 0 for r in rows)
    t = norm[0][0].shape[0]
    rb = min(rb, t)
    ins = [(a, (rb, n), (lambda i, cb=cb: (i, cb))) for a, n, cb in norm]
    ins += [(cst, cst.shape, (lambda i, nd=cst.ndim: (0,) * nd)) for cst in consts]
    o = []
    for out in outs:
        if len(out) == 2:
            o.append(((t, out[0]), out[1], (rb, out[0]), lambda i: (i, 0)))
        else:
            n, d, into, off = out
            assert off % n == 0 and into.dtype == d
            o.append((into.shape, d, (rb, n), (lambda i, cb=off // n: (i, cb)), into))
    return blockmap(name, fn, (t // rb,), ins, o, accs)


MM_TILE_M, MM_TILE_N, MM_TILE_K = 1024, 1024, 2048


def _tile(dim, cap):
    if dim <= cap:
        return dim
    best = max(t for t in range(LANES, cap + 1, LANES) if dim % t == 0)
    return best


def matmul(name, a, b, ta=False, tb=False, out_dtypes=(F32,), epi=None, extras=(), comm=None, b_quarters=False, out_quarters=False):
    (k_dim, m_dim) = a.shape if ta else a.shape[::-1]
    if b_quarters:
        quarter = b.shape[2]
        b_rows, b_cols = b.shape[1], 4 * quarter
    else:
        b_rows, b_cols = b.shape
    n_dim = b_rows if tb else b_cols
    assert (b_cols if tb else b_rows) == k_dim, (name, a.shape, b.shape)
    tm, tn, tk = _tile(m_dim, MM_TILE_M), _tile(n_dim, MM_TILE_N), _tile(k_dim, MM_TILE_K)
    if b_quarters and tb:
        tk = quarter
    elif b_quarters or out_quarters:
        tn = quarter if b_quarters else n_dim // 4
    grid = (m_dim // tm, n_dim // tn, k_dim // tk)
    k_steps = grid[2]
    n_extra, n_out = len(extras), len(out_dtypes)
    n_cin, n_cout = (len(comm.operands), len(comm.out_shapes)) if comm else (0, 0)
    dims = (((0 if ta else 1,), (1 if tb else 0,)), ((), ()))

    def body(*refs):
        ins, outs, scratch = refs[:2 + n_extra + n_cin], refs[2 + n_extra + n_cin:][:n_out + n_cout], refs[2 + n_extra + n_cin + n_out + n_cout:]
        extra_refs, out_refs = ins[2:2 + n_extra], outs[:n_out]
        ids = [pl.program_id(ax) for ax in range(3)]
        if comm:
            comm_refs = (ins[2 + n_extra:], outs[n_out:], scratch[-2], scratch[-1])

            step = (ids[0] * grid[1] + ids[1]) * grid[2] + ids[2]

            @pl.when(step == 0)
            def _():
                comm.start(*comm_refs)

            @pl.when(step == (grid[0] * grid[1] * grid[2]) // 2)
            def _():
                comm.middle(*comm_refs)

        def finish(acc):
            vals = (acc,) if epi is None else epi(acc, *[r[...] for r in extra_refs])
            if not isinstance(vals, (tuple, list)):
                vals = (vals,)
            for r, v in zip(out_refs, vals):
                r[...] = v.astype(r.dtype)

        prod = lax.dot_general(ins[0][...].astype(BF16), ins[1][...].astype(BF16), dims, preferred_element_type=F32)
        if k_steps == 1:
            finish(prod)
        else:
            acc_ref = scratch[0]

            @pl.when(ids[2] == 0)
            def _():
                acc_ref[...] = jnp.zeros(acc_ref.shape, F32)

            acc_ref[...] += prod

            @pl.when(ids[2] == k_steps - 1)
            def _():
                finish(acc_ref[...])

        if comm:
            @pl.when(functools.reduce(jnp.logical_and, [i == g - 1 for i, g in zip(ids, grid)]))
            def _():
                comm.finish(*comm_refs)

    a_spec = pl.BlockSpec((tk, tm), lambda i, j, k: (k, i)) if ta else pl.BlockSpec((tm, tk), lambda i, j, k: (i, k))
    if b_quarters:
        b_spec = pl.BlockSpec((None, tn, tk), lambda i, j, k: (k, j, 0)) if tb else pl.BlockSpec((None, tk, tn), lambda i, j, k: (j, k, 0))
    else:
        b_spec = pl.BlockSpec((tn, tk), lambda i, j, k: (j, k)) if tb else pl.BlockSpec((tk, tn), lambda i, j, k: (k, j))
    mn_spec = pl.BlockSpec((tm, tn), lambda i, j, k: (i, j))
    out_spec = pl.BlockSpec((None, tm, tn), lambda i, j, k: (j, i, 0)) if out_quarters else mn_spec
    out_dims = (4, m_dim, tn) if out_quarters else (m_dim, n_dim)
    scratch_shapes = [] if k_steps == 1 else [pltpu.VMEM((tm, tn), F32)]
    if comm:
        scratch_shapes += [pltpu.SemaphoreType.DMA(comm.sem_shape), pltpu.SemaphoreType.DMA(comm.sem_shape)]
    res = pl.pallas_call(
        body, name=name, grid=grid,
        in_specs=[a_spec, b_spec] + [mn_spec] * n_extra + [ANY] * n_cin,
        out_specs=[out_spec] * n_out + [ANY] * n_cout,
        out_shape=[jax.ShapeDtypeStruct(out_dims, d) for d in out_dtypes] + (comm.out_shapes if comm else []),
        scratch_shapes=scratch_shapes,
        compiler_params=_params(("arbitrary",) * 3 if comm else ("parallel", "parallel", "arbitrary")),
    )(a, b, *extras, *(comm.operands if comm else []))
    main = res[:n_out] if n_out > 1 else res[0]
    return (main, list(res[n_out:])) if comm else main


def ssd_step(g0, state, tail, zxbc, small, cw, cb, p_dtb, p_alog, p_dsk, nw):
    d_in, gn = state.shape[0] * 2 * LANES, state.shape[0] * LANES
    z = zxbc[:, :d_in]
    act, new_tail = _conv_silu_carried(tail, zxbc[:, d_in:], cw, cb)
    xs, bm, cm = act[:, :d_in], act[:, d_in:d_in + gn], act[:, d_in + gn:]
    hb, n = state.shape[0], xs.shape[0]
    n_pair, n_head = 2 * hb, 4 * hb
    causal = _tril(n)
    dt_all = _softplus(small + p_dtb)
    a_all = dt_all * (-jnp.exp(p_alog))
    acum_all = _cumsum_rows(a_all, n)
    acum_t = acum_all.T
    lane0 = LANE_DT + SSM_HEADS_PER_GROUP * g0
    sub = lax.broadcasted_iota(jnp.int32, acum_t.shape, 0)
    heads = range(n_head)
    acum = jnp.stack([_lane_col(acum_all, lane0 + i) for i in heads])
    acum_row = jnp.stack([jnp.sum(jnp.where(sub == lane0 + i, acum_t, 0.0), axis=0, keepdims=True) for i in heads])
    dt = jnp.stack([_lane_col(dt_all, lane0 + i) for i in heads])
    dsk = jnp.stack([_lane_col(p_dsk, lane0 + i) for i in heads])
    decay = jnp.exp(jnp.where(causal, acum - acum_row, -jnp.inf))
    a_last = acum[:, n - 1:n, :]

    def split(a):
        return [a[:, i * LANES:(i + 1) * LANES] for i in range(a.shape[1] // LANES)]

    def pairs(a, axis=2):
        even = jnp.stack([a[2 * p] for p in range(n_pair)])
        odd = jnp.stack([a[2 * p + 1] for p in range(n_pair)])
        shape = (n_pair, LANES, LANES) if axis == 1 else (n_pair, a.shape[1], LANES)
        return jnp.where(lax.broadcasted_iota(jnp.int32, shape, axis) < SSM_HEAD_DIM, even, odd)

    bms, cms = split(bm), split(cm)
    cb = _bdot(jnp.stack(cms), jnp.stack(bms), BNT)
    cbd = jnp.stack([cb[i // SSM_HEADS_PER_GROUP] for i in heads]) * decay
    xp = jnp.stack(split(xs))
    xdt = xp * pairs(dt)
    yd = _bdot(cbd, jnp.stack([xdt[i // 2] for i in heads]), BNN)
    lane = lax.broadcasted_iota(jnp.int32, (n_pair, n, LANES), 2)
    y_diag = jnp.where(lane < SSM_HEAD_DIM, jnp.stack([yd[2 * p] for p in range(n_pair)]), jnp.stack([yd[2 * p + 1] for p in range(n_pair)]))
    st = state.reshape(n_pair, LANES, LANES)
    cm2 = jnp.stack([cms[p // 2] for p in range(n_pair)])
    bm2 = jnp.stack([bms[p // 2] for p in range(n_pair)])
    y_off = _bdot(cm2, st, BNT) * pairs(jnp.exp(acum))
    new = st * pairs(jnp.exp(a_last), axis=1) + _bdot(xdt * pairs(jnp.exp(a_last - acum)), bm2, BTN)
    y = y_diag + y_off + pairs(dsk) * xp
    y = jnp.concatenate([y[p] for p in range(n_pair)], axis=1) * _silu(z)
    wide = 2 * LANES
    y = jnp.concatenate([_rms(y[:, i * wide:(i + 1) * wide], nw[i]) for i in range(hb)], axis=1)
    return new.reshape(state.shape), new_tail, y


@functools.partial(jax.custom_vjp, nondiff_argnums=(1,))
def _unit_lower_inverse(a, seg):
    n = a.shape[-1]
    r = lax.broadcasted_iota(jnp.int32, (n, n), 0)
    c = lax.broadcasted_iota(jnp.int32, (n, n), 1)
    shift = min(INVERSE_BASE, seg).bit_length() - 1
    power = jnp.where((r >> shift) == (c >> shift), a, 0.0)
    inv = (r == c).astype(F32) - power
    span = 2
    while span < (1 << shift):
        power = _hdot(power, power, BNN)
        inv = inv + _hdot(inv, power, BNN)
        span *= 2
    while (1 << shift) < seg:
        below = jnp.logical_and((r >> (shift + 1)) == (c >> (shift + 1)), (r >> shift) != (c >> shift))
        inv = inv - _hdot(inv, _hdot(jnp.where(below, a, 0.0), inv, BNN), BNN)
        shift += 1
    return inv


def _unit_lower_inverse_fwd(a, seg):
    inv = _unit_lower_inverse(a, seg)
    return inv, inv


def _unit_lower_inverse_bwd(seg, inv, ct):
    return (-_hdot(_hdot(inv, ct, BTN), inv, BNT),)


_unit_lower_inverse.defvjp(_unit_lower_inverse_fwd, _unit_lower_inverse_bwd)


@jax.custom_vjp
def _known_inverse(a, inv):
    return inv


def _known_inverse_fwd(a, inv):
    return inv, inv


def _known_inverse_bwd(inv, ct):
    return _unit_lower_inverse_bwd(None, inv, ct)[0], jnp.zeros_like(inv)


_known_inverse.defvjp(_known_inverse_fwd, _known_inverse_bwd)


@functools.partial(jax.custom_vjp, nondiff_argnums=(1,))
def _rotate_rows(x, k):
    return x if k == 0 else pltpu.roll(x, k % x.shape[0], 0)


def _rotate_rows_fwd(x, k):
    return _rotate_rows(x, k), None


def _rotate_rows_bwd(k, _, ct):
    return (_rotate_rows(ct, -k),)


_rotate_rows.defvjp(_rotate_rows_fwd, _rotate_rows_bwd)


def _conv_silu_carried(tail, x, cw, cb=0.0):
    n = x.shape[0]
    ext = jnp.concatenate([tail, x], axis=0)
    pre = cb + sum(cw[j:j + 1, :] * _rotate_rows(ext, CONV_K - 1 - j)[TAIL_ROWS:] for j in range(CONV_K))
    return _silu(pre), x[n - TAIL_ROWS:]


def _l2norm(x):
    return x * lax.rsqrt(jnp.sum(x * x, axis=-1, keepdims=True) + EPS)


def gdn_step(hq0, state, tail, qkvz, small, cw, p_dtb, p_alog, nw, keep=False, kept=None):
    n, chunk = qkvz.shape[0], GDN_CHUNK
    hb = state.shape[0]
    nb = 2 * hb
    cur = state.reshape(nb, LANES, LANES)
    conv_cols = 4 * hb * LANES
    act, new_tail = _conv_silu_carried(tail, qkvz[:, :conv_cols], cw)
    q, k, v = act[:, :hb * LANES], act[:, hb * LANES:2 * hb * LANES], act[:, 2 * hb * LANES:]
    z = qkvz[:, conv_cols:]
    causal, strict = _tril(n, seg=chunk), _tril(n, True, seg=chunk)
    beta_all = _sigmoid(small)
    g_all = -jnp.exp(p_alog) * _softplus(small + p_dtb)
    gcum_all = _cumsum_rows(g_all, chunk)
    gcum_t = gcum_all.T
    split = lambda a: [a[:, i * LANES:(i + 1) * LANES] for i in range(a.shape[1] // LANES)]
    per_value_head = lambda a: jnp.stack([a[i // 2] for i in range(nb)])
    qh, kh = _l2norm(jnp.stack(split(q))) * (GDN_HEAD ** -0.5), _l2norm(jnp.stack(split(k)))
    q2, k2 = per_value_head(qh), per_value_head(kh)
    v2, z2 = jnp.stack(split(v)), jnp.stack(split(z))
    gcum = jnp.stack([_lane_col(gcum_all, LANE_A + 2 * hq0 + i) for i in range(nb)])
    gcum_row = jnp.stack([_head_rows(gcum_t, LANE_A + 2 * hq0 + i) for i in range(nb)])
    beta = jnp.stack([_lane_col(beta_all, LANE_B + 2 * hq0 + i) for i in range(nb)])
    dmat = jnp.exp(jnp.where(causal, gcum - gcum_row, -jnp.inf))
    a_low = jnp.where(strict, beta * per_value_head(_bdot(kh, kh, BNT)) * dmat, 0.0)
    inv = _unit_lower_inverse(a_low, chunk) if kept is None else _known_inverse(a_low, kept)
    egc = jnp.exp(gcum)
    u = _hdot(inv, v2 * beta, BNN)
    w = _hdot(inv, k2 * (beta * egc), BNN)
    q_dec = q2 * egc
    v_new, o_state = [], []
    for s in range(n // chunk):
        rows = slice(s * chunk, (s + 1) * chunk)
        v_new.append(u[:, rows] - _bdot(w[:, rows], cur, BNN))
        o_state.append(_bdot(q_dec[:, rows], cur, BNN))
        g_last = gcum[:, (s + 1) * chunk - 1:(s + 1) * chunk, :]
        k_dec = k2[:, rows] * jnp.exp(g_last - gcum[:, rows])
        cur = cur * jnp.exp(g_last) + _bdot(k_dec, v_new[-1], BTN)
    o = jnp.concatenate(o_state, axis=1) + _bdot(per_value_head(_bdot(qh, kh, BNT)) * dmat, jnp.concatenate(v_new, axis=1), BNN)
    out = _rms(o, nw) * _silu(z2)
    res = (cur.reshape(state.shape), new_tail, jnp.concatenate([out[i] for i in range(nb)], axis=1))
    return res + (inv,) if keep else res


STATE_SHAPE = (2, LANES, LANES)
GDN_CHUNKS_PER_STEP = 2
INVERSE_BASE = 16


def _scan_specs(rows, consts, chunk, chunk_of, hb):
    specs = []
    for _, n, off, per_group in rows:
        if per_group:
            assert off % (n * hb) == 0
            specs.append(pl.BlockSpec((chunk, n * hb), lambda c, g, cb=off // (n * hb): (chunk_of(c), cb + g)))
        else:
            assert off % n == 0
            specs.append(pl.BlockSpec((chunk, n), lambda c, g, cb=off // n: (chunk_of(c), cb)))
    for arr, per_group in consts:
        if per_group:
            specs.append(pl.BlockSpec((hb, 1, arr.shape[2]), lambda c, g: (g, 0, 0)))
        else:
            specs.append(pl.BlockSpec(arr.shape, lambda c, g, nd=arr.ndim: (0,) * nd))
    return specs


def scan_fwd(name, step, chunk, n_grp, rows, consts, out_cols, hb, keep=None, state_shape=None, tail_cols=None):
    t = rows[0][0].shape[0]
    nc = t // chunk
    n_rows, n_consts = len(rows), len(consts)
    state_shape = state_shape or (hb,) + STATE_SHAPE
    carried = [state_shape] + ([(TAIL_ROWS, tail_cols)] if tail_cols else [])
    n_car = len(carried)

    def body(*refs):
        row_refs, const_refs = refs[:n_rows], refs[n_rows:n_rows + n_consts]
        y_ref = refs[n_rows + n_consts]
        saved_refs = refs[n_rows + n_consts + 1:n_rows + n_consts + 1 + n_car]
        scratch = refs[-n_car:]
        c, g = pl.program_id(0), pl.program_id(1)

        @pl.when(c == 0)
        def _():
            for s, shape in zip(scratch, carried):
                s[g] = jnp.zeros(shape, F32)

        cur = [s[g] for s in scratch]
        for r, v in zip(saved_refs, cur):
            r[...] = v
        vals = [r[...] for r in row_refs] + [r[...] for r in const_refs]
        res = step(g * hb, *cur, *vals) if keep is None else step(g * hb, *cur, *vals, keep=True)
        for s, v in zip(scratch, res[:n_car]):
            s[g] = v
        y_ref[...] = res[n_car].astype(y_ref.dtype)
        if keep is not None:
            refs[n_rows + n_consts + 1 + n_car][...] = res[n_car + 1]

    lead = (nc, n_grp // hb)
    out_specs = [pl.BlockSpec((chunk, out_cols * hb), lambda c, g: (c, g))]
    out_shape = [jax.ShapeDtypeStruct((t, n_grp * out_cols), BF16)]
    for shape in carried + ([keep] if keep is not None else []):
        out_specs.append(pl.BlockSpec((None, None) + shape, lambda c, g, nd=len(shape): (c, g) + (0,) * nd))
        out_shape.append(jax.ShapeDtypeStruct(lead + shape, F32))
    return pl.pallas_call(
        body, name=name, grid=lead,
        in_specs=_scan_specs(rows, consts, chunk, lambda c: c, hb),
        out_specs=out_specs, out_shape=out_shape,
        scratch_shapes=[pltpu.VMEM((n_grp // hb,) + shape, F32) for shape in carried],
        compiler_params=_params(("arbitrary", "arbitrary")),
    )(*[r[0] for r in rows], *[c[0] for c in consts])


def scan_bwd(name, step, chunk, n_grp, rows, consts, saved, dy, row_dtypes, hb, into, kept=None):
    t = rows[0][0].shape[0]
    nc = t // chunk
    n_rows, n_consts, n_car = len(rows), len(consts), len(saved)
    carried = [s.shape[2:] for s in saved]
    out_cols = dy.shape[1] // n_grp
    n_alias = sum(not isinstance(v, jax.ShapeDtypeStruct) for v in into.values())
    n_kept = 0 if kept is None else 1
    n_in = n_rows + n_consts + n_car + 1 + n_kept + n_alias

    def body(*refs):
        row_refs, const_refs = refs[:n_rows], refs[n_rows:n_rows + n_consts]
        saved_refs = refs[n_rows + n_consts:n_rows + n_consts + n_car]
        dy_ref = refs[n_rows + n_consts + n_car]
        outs = refs[n_in:-n_car]
        scratch = refs[-n_car:]
        c, g = pl.program_id(0), pl.program_id(1)

        @pl.when(c == 0)
        def _():
            for s, shape in zip(scratch, carried):
                s[g] = jnp.zeros(shape, F32)

        @pl.when(jnp.logical_and(c == 0, g == 0))
        def _():
            for r in outs[n_rows:]:
                r[...] = jnp.zeros(r.shape, r.dtype)

        f = functools.partial(step, g * hb) if kept is None else functools.partial(step, g * hb, kept=refs[n_rows + n_consts + n_car + 1][...])
        _, vjp = jax.vjp(f, *[r[...] for r in saved_refs], *[r[...] for r in row_refs], *[r[...] for r in const_refs])
        grads = vjp(tuple(s[g] for s in scratch) + (dy_ref[...].astype(F32),))
        for s, d in zip(scratch, grads[:n_car]):
            s[g] = d
        for (_, _, _, per_group), r, d in zip(rows, outs[:n_rows], grads[n_car:n_car + n_rows]):
            if per_group:
                r[...] = d.astype(r.dtype)
            else:
                @pl.when(g == 0)
                def _(r=r):
                    r[...] = jnp.zeros(r.shape, r.dtype)

                r[...] += d.astype(r.dtype)
        for (_, per_group), r, d in zip(consts, outs[n_rows:], grads[n_car + n_rows:]):
            if per_group:
                r[pl.ds(g * hb, hb)] += d
            else:
                r[...] += d

    rev = lambda c: nc - 1 - c
    out_specs, out_shape = [], []
    into_arrays, aliases = [], {}
    first_into = n_in - n_alias
    kept_arrays = [] if kept is None else [kept]
    by_step = lambda a: pl.BlockSpec((None, None) + a.shape[2:], lambda c, g, nd=a.ndim - 2: (rev(c), g) + (0,) * nd)
    for k, ((_, n, off, per_group), dt) in enumerate(zip(rows, row_dtypes)):
        if k in into:
            assert per_group and off % (n * hb) == 0 and into[k].dtype == dt
            out_specs.append(pl.BlockSpec((chunk, n * hb), lambda c, g, cb=off // (n * hb): (rev(c), cb + g)))
            out_shape.append(jax.ShapeDtypeStruct(into[k].shape, dt))
            if not isinstance(into[k], jax.ShapeDtypeStruct):
                aliases[first_into + len(into_arrays)] = k
                into_arrays.append(into[k])
        elif per_group:
            out_specs.append(pl.BlockSpec((chunk, n * hb), lambda c, g: (rev(c), g)))
            out_shape.append(jax.ShapeDtypeStruct((t, n_grp * n), dt))
        else:
            out_specs.append(pl.BlockSpec((chunk, n), lambda c, g: (rev(c), 0)))
            out_shape.append(jax.ShapeDtypeStruct((t, n), dt))
    for arr, _ in consts:
        out_specs.append(pl.BlockSpec(arr.shape, lambda c, g, nd=arr.ndim: (0,) * nd))
        out_shape.append(jax.ShapeDtypeStruct(arr.shape, F32))
    return pl.pallas_call(
        body, name=name, grid=(nc, n_grp // hb),
        in_specs=_scan_specs(rows, consts, chunk, rev, hb) + [by_step(s) for s in saved]
        + [pl.BlockSpec((chunk, out_cols * hb), lambda c, g: (rev(c), g))] + [by_step(k) for k in kept_arrays]
        + [pl.BlockSpec(memory_space=pl.ANY)] * len(into_arrays),
        out_specs=out_specs, out_shape=out_shape, input_output_aliases=aliases,
        scratch_shapes=[pltpu.VMEM((n_grp // hb,) + shape, F32) for shape in carried],
        compiler_params=_params(("arbitrary", "arbitrary")),
    )(*[r[0] for r in rows], *[c[0] for c in consts], *saved, dy, *kept_arrays, *into_arrays)


def _place():
    return lax.axis_index("x"), lax.axis_index("y"), lax.axis_index("c")


def _other_chips(x, y):
    return [(1 - x, y), (x, 1 - y), (1 - x, 1 - y)]


ANY = pl.BlockSpec(memory_space=pl.ANY)


def all_gather8(name, v):
    m_per, n = v.shape

    def body(x_ref, out_ref, send_sems, recv_sems, local_sem):
        x, y, c = _place()
        me, sibling = (x, y, c), (x, y, 1 - c)
        chips = _other_chips(x, y)

        def rows(px, py, pc):
            return out_ref.at[pl.ds((4 * px + 2 * py + pc) * m_per, m_per), :]

        def copy(k, block, to, src=None):
            return pltpu.make_async_remote_copy(
                src_ref=rows(*block) if src is None else src, dst_ref=rows(*block),
                send_sem=send_sems.at[k], recv_sem=recv_sems.at[k], device_id=to, device_id_type=MESH)

        mine = pltpu.make_async_copy(x_ref, rows(*me), local_sem)
        mine.start()
        first = [copy(0, me, sibling, src=x_ref)]
        first += [copy(1 + q, me, (*chip, c), src=x_ref) for q, chip in enumerate(chips)]
        for cp in first:
            cp.start()
        passed = [copy(4 + q, (*chip, c), sibling) for q, chip in enumerate(chips)]
        for q, chip in enumerate(chips):
            copy(1 + q, (*chip, c), me).wait_recv()
            passed[q].start()
        copy(0, sibling, me).wait_recv()
        for q, chip in enumerate(chips):
            copy(4 + q, (*chip, 1 - c), me).wait_recv()
        for cp in first + passed:
            cp.wait_send()
        mine.wait()

    return pl.pallas_call(
        body, name=name, out_shape=jax.ShapeDtypeStruct((N_DEV * m_per, n), v.dtype),
        in_specs=[pl.BlockSpec(memory_space=pltpu.VMEM)], out_specs=pl.BlockSpec(memory_space=pltpu.VMEM),
        scratch_shapes=[pltpu.SemaphoreType.DMA((7,)), pltpu.SemaphoreType.DMA((7,)), pltpu.SemaphoreType.DMA],
    )(v)


def gather_flat(name, vec):
    n = vec.shape[0]
    n_pad = -(-n // (8 * LANES)) * (8 * LANES)
    v = jnp.pad(vec, (0, n_pad - n)).reshape(8, n_pad // 8)
    return all_gather8(name, v).reshape(N_DEV, n_pad)[:, :n]


class Exchange:
    def __init__(self, operands, out_shapes, sem_shape, start, finish, middle=None):
        self.operands, self.out_shapes, self.sem_shape = list(operands), out_shapes, sem_shape
        self.start, self.middle, self.finish = start, middle or (lambda *refs: None), finish


def _start_all_wait_all(make_copies):
    def start(*refs):
        for cp in make_copies(*refs):
            cp.start()

    def finish(*refs):
        for cp in make_copies(*refs):
            cp.wait()

    return start, finish


def run_exchange(name, ex):
    n_in, n_out = len(ex.operands), len(ex.out_shapes)

    def body(*refs):
        ins, outs = refs[:n_in], refs[n_in:n_in + n_out]
        ex.start(ins, outs, *refs[n_in + n_out:])
        ex.middle(ins, outs, *refs[n_in + n_out:])
        ex.finish(ins, outs, *refs[n_in + n_out:])

    return pl.pallas_call(
        body, name=name, out_shape=ex.out_shapes, in_specs=[ANY] * n_in, out_specs=[ANY] * n_out,
        scratch_shapes=[pltpu.SemaphoreType.DMA(ex.sem_shape), pltpu.SemaphoreType.DMA(ex.sem_shape)],
    )(*ex.operands)


def _half(shape, axis, pc):
    h = shape[axis] // 2
    return (pl.ds(pc * h, h), slice(None)) if axis == 0 else (slice(None), pl.ds(pc * h, h))


def _half_shape(shape, axis):
    return tuple(s // 2 if a == axis else s for a, s in enumerate(shape))


def all_gather_shards(shards, axes):
    n_t = len(shards)
    n_sem = 12

    def copies(ins, outs, send_sems, recv_sems):
        x, y, c = _place()
        me, sibling, x_nbr, y_nbr = (x, y, c), (x, y, 1 - c), (1 - x, y, c), (x, 1 - y, c)
        own, of_x, of_y, of_diag = 2 * x + y, 2 * (1 - x) + y, 2 * x + 1 - y, 2 * (1 - x) + 1 - y

        def copy(t, k, quarter, pc, piece, to, from_input=False):
            axis = axes[t]
            h = ins[t].shape[axis] // 4
            cut = pl.ds((2 * pc + piece) * h, h)
            part = (cut, slice(None)) if axis == 0 else (slice(None), cut)
            dst = outs[t].at[(quarter,) + part]
            return pltpu.make_async_remote_copy(
                src_ref=ins[t].at[part] if from_input else dst, dst_ref=dst,
                send_sem=send_sems.at[t, k], recv_sem=recv_sems.at[t, k], device_id=to, device_id_type=MESH)

        stages = []
        for t in range(n_t):
            direct = [copy(t, 0, own, c, 0, x_nbr, True), copy(t, 2, own, c, 1, y_nbr, True),
                      copy(t, 1, own, c, 1, x_nbr, True), copy(t, 3, own, c, 0, y_nbr, True)]
            landing = [
                (copy(t, 0, of_x, c, 0, me), [copy(t, 4, of_x, c, 0, y_nbr), copy(t, 6, of_x, c, 0, sibling)]),
                (copy(t, 2, of_y, c, 1, me), [copy(t, 5, of_y, c, 1, x_nbr), copy(t, 8, of_y, c, 1, sibling)]),
                (copy(t, 1, of_x, c, 1, me), [copy(t, 7, of_x, c, 1, sibling)]),
                (copy(t, 3, of_y, c, 0, me), [copy(t, 9, of_y, c, 0, sibling)]),
                (copy(t, 4, of_diag, c, 0, me), [copy(t, 10, of_diag, c, 0, sibling)]),
                (copy(t, 5, of_diag, c, 1, me), [copy(t, 11, of_diag, c, 1, sibling)])]
            from_sibling = [copy(t, 6, of_x, 1 - c, 0, me), copy(t, 8, of_y, 1 - c, 1, me), copy(t, 7, of_x, 1 - c, 1, me),
                            copy(t, 9, of_y, 1 - c, 0, me), copy(t, 10, of_diag, 1 - c, 0, me), copy(t, 11, of_diag, 1 - c, 1, me)]
            stages.append((direct, landing, from_sibling))
        return stages

    def start(*refs):
        for direct, _, _ in copies(*refs):
            for cp in direct:
                cp.start()

    def pass_on(landing):
        for arrived, onward in landing:
            arrived.wait_recv()
            for cp in onward:
                cp.start()

    def middle(*refs):
        for _, landing, _ in copies(*refs):
            pass_on(landing[:4])

    def finish(*refs):
        stages = copies(*refs)
        for _, landing, _ in stages:
            pass_on(landing[4:])
        for direct, landing, from_sibling in stages:
            for cp in from_sibling:
                cp.wait_recv()
            for cp in direct + [cp for _, onward in landing for cp in onward]:
                cp.wait_send()

    return Exchange(shards, [jax.ShapeDtypeStruct((4,) + s.shape, s.dtype) for s in shards], (n_t, n_sem), start, finish, middle)


def exchange_halves_d2d(grads, axes):
    n_t = len(grads)

    def copies(ins, outs, send_sems, recv_sems):
        x, y, c = _place()
        return [pltpu.make_async_remote_copy(
            src_ref=ins[t].at[(slice(None),) + _half(ins[t].shape[1:], axes[t], 1 - c)], dst_ref=outs[t],
            send_sem=send_sems.at[t], recv_sem=recv_sems.at[t], device_id=(x, y, 1 - c), device_id_type=MESH) for t in range(n_t)]

    shapes = [jax.ShapeDtypeStruct((4,) + _half_shape(g.shape[1:], a), g.dtype) for g, a in zip(grads, axes)]
    return Exchange(grads, shapes, (n_t,), *_start_all_wait_all(copies))


def exchange_quarters_ici(parts):
    n_t = len(parts)

    def copies(ins, outs, send_sems, recv_sems):
        x, y, c = _place()
        return [pltpu.make_async_remote_copy(
            src_ref=ins[t].at[2 * px + py], dst_ref=outs[t].at[q],
            send_sem=send_sems.at[t, q], recv_sem=recv_sems.at[t, q], device_id=(px, py, c), device_id_type=MESH)
            for t in range(n_t) for q, (px, py) in enumerate(_other_chips(x, y))]

    shapes = [jax.ShapeDtypeStruct((3,) + p.shape[1:], p.dtype) for p in parts]
    return Exchange(parts, shapes, (n_t, 3), *_start_all_wait_all(copies))


def swap_d2d(halves):
    n_t = len(halves)

    def copies(ins, outs, send_sems, recv_sems):
        x, y, c = _place()
        return [pltpu.make_async_remote_copy(
            src_ref=ins[t], dst_ref=outs[t], send_sem=send_sems.at[t], recv_sem=recv_sems.at[t],
            device_id=(x, y, 1 - c), device_id_type=MESH) for t in range(n_t)]

    return Exchange(halves, [jax.ShapeDtypeStruct(h.shape, h.dtype) for h in halves], (n_t,), *_start_all_wait_all(copies))


BLOCK_BYTES = 1 << 20


def _row_block(r, c):
    fits = [rb for rb in range(16, r + 1, 16) if r % rb == 0 and rb * c * 4 <= BLOCK_BYTES]
    return max(fits) if fits else r


def _place_scalars():
    x, y, c = _place()
    return jnp.stack([c, 2 * x + y]).astype(jnp.int32)


def reduce_on_chip(tag, grads, axes, from_sibling=None):
    if from_sibling is None:
        from_sibling = run_exchange(f"rs_d2d_{tag}", exchange_halves_d2d(grads, axes))
    parts, parts_bf16 = [], []
    for t, (g, s, axis) in enumerate(zip(grads, from_sibling, axes)):
        _, h, cols = s.shape
        rb = _row_block(h, cols)
        nb = h // rb
        blk = lambda k, i, s_ref: (k, i, 0)
        mine = (lambda k, i, s_ref, nb=nb: (k, s_ref[0] * nb + i, 0)) if axis == 0 else (lambda k, i, s_ref: (k, i, s_ref[0]))
        p32, p16 = blockmap(
            f"rs_add_{tag}{t}", lambda a, b: (a + b, a + b), (4, nb),
            [(g, (None, rb, cols), mine), (s, (None, rb, cols), blk)],
            [(s.shape, F32, (None, rb, cols), blk), (s.shape, BF16, (None, rb, cols), blk)], scalars=_place_scalars())
        parts.append(p32)
        parts_bf16.append(p16)
    return parts, parts_bf16


def reduce_across_chips(parts, from_chips):
    halves = []
    for t, (p, q) in enumerate(zip(parts, from_chips)):
        _, h, cols = p.shape
        rb = _row_block(h, cols)
        halves.append(blockmap(
            f"rs_sum{t}", lambda a, b: a + b[0].astype(F32) + b[1].astype(F32) + b[2].astype(F32), (h // rb,),
            [(p, (None, rb, cols), lambda i, s_ref: (s_ref[1], i, 0)), (q, (3, rb, cols), lambda i, s_ref: (0, i, 0))],
            [((h, cols), F32, (rb, cols), lambda i, s_ref: (i, 0))], scalars=_place_scalars())[0])
    return list(zip(halves, run_exchange("rs_swap", swap_d2d(halves))))


def _adamw(w, g, m, v):
    m = ADAM_B1 * m + (1.0 - ADAM_B1) * g
    v = ADAM_B2 * v + (1.0 - ADAM_B2) * jnp.square(g)
    m_hat = m / (1.0 - ADAM_B1 ** ADAM_STEP)
    v_hat = v / (1.0 - ADAM_B2 ** ADAM_STEP)
    delta = -ADAM_LR * (m_hat / (jnp.sqrt(v_hat) + ADAM_EPS) + ADAM_WD * w)
    return delta, m, v


def adamw(name, w, g, m, v):
    _, r, c = w.shape
    rb = _row_block(r, c)
    blk3 = lambda a: (a, (None, rb, c), lambda i: (0, i, 0))
    return blockmap(name, _adamw, (r // rb,), [blk3(w), (g, (rb, c), lambda i: (i, 0)), blk3(m), blk3(v)],
                    [(w.shape, F32, (None, rb, c), lambda i: (0, i, 0))] * 3)


def adamw_halves(name, w, mine, other, m, v, axis):
    _, r, c = w.shape
    h, c = mine.shape
    rb = _row_block(h, c)
    nb = h // rb

    def body(s_ref, w_ref, mine_ref, other_ref, m_ref, v_ref, g_out, d_out, m_out, v_out):
        g = jnp.where(pl.program_id(0) == s_ref[0], mine_ref[...], other_ref[...])
        d, nm, nv = _adamw(w_ref[...], g, m_ref[...], v_ref[...])
        g_out[...], d_out[...], m_out[...], v_out[...] = g, d, nm, nv

    spec3 = pl.BlockSpec((None, rb, c), (lambda k, i, s_ref: (0, k * nb + i, 0)) if axis == 0 else (lambda k, i, s_ref: (0, i, k)))
    spec2 = pl.BlockSpec((rb, c), lambda k, i, s_ref: (i, 0))
    grid_spec = pltpu.PrefetchScalarGridSpec(num_scalar_prefetch=1, grid=(2, nb), in_specs=[spec3, spec2, spec2, spec3, spec3],
                                             out_specs=[spec3] * 4)
    return pl.pallas_call(body, name=name, grid_spec=grid_spec, out_shape=[jax.ShapeDtypeStruct(w.shape, F32)] * 4,
                          compiler_params=_params(("parallel", "parallel")))(_place_scalars(), w, mine, other, m, v)


def _whole(name, fn, ins, outs):
    return blockmap(name, fn, (1,), [(a, a.shape, lambda i, nd=a.ndim: (0,) * nd) for a in ins],
                    [(s, d, s, lambda i, nd=len(s): (0,) * nd) for s, d in outs])


def _premix(x, w, sc, sh):
    return _rms(x, w) * (1.0 + sc) + sh


def _postmix(x, u, w_post, g1, w_pre2, sc2, sh2):
    x1 = x + g1 * _rms(u, w_post)
    return x1, _premix(x1, w_pre2, sc2, sh2)


def _merge(gs, gg, ys, yg):
    return _sigmoid(gs) * ys + _sigmoid(gg) * yg


def _final(x1, y2, w_post2, g2):
    return x1 + g2 * _rms(y2, w_post2)


def kernel(x, c, w_ada, b_ada, norm_mix_pre, norm_mix_post, w_in, ssm_conv_w, ssm_conv_b, ssm_dt_bias, ssm_A_log, ssm_D, ssm_norm_w, gdn_conv_w, gdn_dt_bias, gdn_A_log, gdn_norm_w, w_ssm_up, w_gdn_up, w_out, norm_mlp_pre, norm_mlp_post, w_mlp_up, w_mlp_down, loss_target, m_w_ada, m_b_ada, m_norm_mix_pre, m_norm_mix_post, m_w_in, m_ssm_conv_w, m_ssm_conv_b, m_ssm_dt_bias, m_ssm_A_log, m_ssm_D, m_ssm_norm_w, m_gdn_conv_w, m_gdn_dt_bias, m_gdn_A_log, m_gdn_norm_w, m_w_ssm_up, m_w_gdn_up, m_w_out, m_norm_mlp_pre, m_norm_mlp_post, m_w_mlp_up, m_w_mlp_down, v_w_ada, v_b_ada, v_norm_mix_pre, v_norm_mix_post, v_w_in, v_ssm_conv_w, v_ssm_conv_b, v_ssm_dt_bias, v_ssm_A_log, v_ssm_D, v_ssm_norm_w, v_gdn_conv_w, v_gdn_dt_bias, v_gdn_A_log, v_gdn_norm_w, v_w_ssm_up, v_w_gdn_up, v_w_out, v_norm_mlp_pre, v_norm_mlp_post, v_w_mlp_up, v_w_mlp_down):
    args = dict(locals())
    xi, yi, ci = _place()
    quarter = 2 * xi + yi
    batch = 4 * xi + 2 * yi + ci

    xt, target = x[0], loss_target[0]
    t, d = xt.shape
    hs, hv = ssm_dt_bias.shape[-1], gdn_dt_bias.shape[-1]
    assert hs <= LANE_B - LANE_DT and hv <= LANE_A - LANE_B and hv % 2 == 0 and hs % SSM_HEADS_PER_GROUP == 0
    assert t % SSM_CHUNK == 0 and t % (GDN_CHUNK * GDN_CHUNKS_PER_STEP) == 0 and d % LANES == 0
    d_inner = hs * SSM_HEAD_DIM
    n_grp = hs // SSM_HEADS_PER_GROUP
    gn = n_grp * SSM_D_STATE
    conv_ssm = d_inner + 2 * gn
    hq = hv // 2
    key, val = hq * GDN_HEAD, hv * GDN_HEAD
    conv_gdn = 2 * key + val
    o_dt = d_inner + conv_ssm
    o_qkv = o_dt + hs
    o_b = o_qkv + conv_gdn + val
    o_a = o_b + hv
    o_gs = o_a + hv
    n_proj = o_gs + 2 * d
    assert 4 * w_in.shape[-1] == n_proj and ssm_conv_w.shape[-1] * 4 == conv_ssm == conv_gdn
    a_z, a_q = 0, o_dt
    a_gs = a_q + conv_gdn + val
    a_gg = a_gs + d
    a_small = a_gg + d
    n_al = -(-(a_small + LANES) // MM_TILE_N) * MM_TILE_N

    def to_aligned(w):
        z = lambda n: jnp.zeros((n, w.shape[1]), w.dtype)
        return jnp.concatenate([
            w[:o_dt], w[o_qkv:o_b], w[o_gs:],
            w[o_dt:o_qkv], z(LANE_B - hs), w[o_b:o_a], z(LANE_A - LANE_B - hv), w[o_a:o_gs], z(LANES - LANE_A - hv),
            z(n_al - a_small - LANES)], axis=0)

    def from_aligned(w):
        s = a_small
        return jnp.concatenate([
            w[:o_dt], w[s + LANE_DT:s + LANE_DT + hs], w[a_q:a_gs], w[s + LANE_B:s + LANE_B + hv],
            w[s + LANE_A:s + LANE_A + hv], w[a_gs:a_small]], axis=0)

    def lanes(vec, at):
        return jnp.zeros((1, LANES), F32).at[:, at:at + vec.shape[-1]].set(vec.reshape(1, -1))

    n_cw = CONV_K * ssm_conv_w.shape[-1]
    small_in = gather_flat("ag_small", jnp.concatenate([c.reshape(-1), ssm_conv_w.reshape(-1), gdn_conv_w.reshape(-1)]))
    c_all = small_in[:, :d]
    by_chip = small_in[0::2]

    def whole_conv_w(lo):
        return jnp.transpose(by_chip[:, lo:lo + n_cw].reshape(4, CONV_K, -1), (1, 0, 2)).reshape(CONV_K, -1)

    cw_ssm, cw_gdn = whole_conv_w(d), whole_conv_w(d + n_cw)
    cb_ssm = ssm_conv_b

    n_ada = w_ada.shape[-1]
    b_q = lax.dynamic_slice_in_dim(b_ada, quarter * n_ada, n_ada, axis=1)
    mod_q = _whole("ada_fwd", lambda ca, w, b: _bdot(_silu(ca), w) + b, [c_all, w_ada[0], b_q], [((N_DEV, n_ada), F32)])[0]
    mod_all = gather_flat("ag_mod", mod_q.reshape(-1)).reshape(N_DEV, N_DEV, n_ada)[0::2]
    mod = lax.dynamic_index_in_dim(mod_all, batch, axis=1, keepdims=False).reshape(1, 4 * n_ada)
    sh1, sc1, g1, sh2, sc2, g2 = [mod[:, i * d:(i + 1) * d] for i in range(6)]

    transposed = lambda a: jnp.swapaxes(a, 1, 2)
    own = [w.astype(BF16) for w in (transposed(w_in)[0], w_ssm_up[0], w_gdn_up[0], w_out[0], w_mlp_up[0], w_mlp_down[0])]
    with_own = lambda gs, ws: [lax.dynamic_update_index_in_dim(g, w, quarter, 0) for g, w in zip(gs, ws)]
    rows_major = lambda g: g.reshape(-1, g.shape[2])
    wb_in = to_aligned(rows_major(with_own(run_exchange("ag_w_in", all_gather_shards(own[:1], [1])), own[:1])[0]))

    h1 = rowmap("premix", _premix, [xt], [norm_mix_pre, sc1, sh1], [(d, BF16)])[0]
    proj, gathered = matmul("in_proj", h1, wb_in, tb=True, comm=all_gather_shards(own[1:], [0] * 5))
    gathered = with_own(gathered, own[1:])
    wb_ssm_up, wb_gdn_up, wb_out = rows_major(gathered[0]), rows_major(gathered[1]), rows_major(gathered[2])
    wb_up, wb_down = gathered[3], rows_major(gathered[4])

    wide = 2 * LANES
    ssd_rows = [(proj, d_inner + conv_ssm, a_z, True), (proj, LANES, a_small, False)]
    ssd_consts = [(cw_ssm, False), (cb_ssm, False), (lanes(ssm_dt_bias, LANE_DT), False), (lanes(ssm_A_log, LANE_DT), False),
                  (lanes(ssm_D, LANE_DT), False), (ssm_norm_w.reshape(n_grp, 1, wide), False)]
    y_ssm_n, st_ssm, tails_ssm = scan_fwd("ssd_fwd", ssd_step, SSM_CHUNK, 1, ssd_rows, ssd_consts, d_inner, 1,
                                          state_shape=(n_grp,) + STATE_SHAPE, tail_cols=conv_ssm)
    gdn_rows = [(proj, conv_gdn + val, a_q, True), (proj, LANES, a_small, False)]
    gdn_consts = [(cw_gdn, False), (lanes(gdn_dt_bias, LANE_A), False), (lanes(gdn_A_log, LANE_A), False), (gdn_norm_w, False)]
    gdn_rows_per_step = GDN_CHUNK * GDN_CHUNKS_PER_STEP
    y_gdn_n, st_gdn, tails_gdn, inv_gdn = scan_fwd(
        "gdn_fwd", gdn_step, gdn_rows_per_step, 1, gdn_rows, gdn_consts, val, 1, keep=(hv, gdn_rows_per_step, gdn_rows_per_step),
        state_shape=(hq,) + STATE_SHAPE, tail_cols=conv_gdn)

    y_ssm = matmul("ssm_up", y_ssm_n, wb_ssm_up)
    y_gdn = matmul("gdn_up", y_gdn_n, wb_gdn_up)
    gates = [(proj, d, a_gs), (proj, d, a_gg)]
    merged = rowmap("merge", _merge, gates + [y_ssm, y_gdn], [], [(d, BF16)])[0]
    u = matmul("w_out", merged, wb_out)
    post_consts = [norm_mix_post, g1, norm_mlp_pre, sc2, sh2]
    x1, h2 = rowmap("postmix", _postmix, [xt, u], post_consts, [(d, F32), (d, BF16)])
    relu2 = lambda acc: (acc, jnp.square(jnp.maximum(acc, 0.0)))
    a_up, act = matmul("mlp_up", h2, wb_up, out_dtypes=(BF16, BF16), epi=relu2, b_quarters=True)
    y2 = matmul("mlp_down", act, wb_down)

    def final_bwd(x1_, y2_, tgt, w_, g_):
        x2, vjp = jax.vjp(_final, x1_, y2_, w_, g_)
        err = x2 - tgt
        loss = 0.5 * jnp.sum(jnp.mean(err * err, axis=-1, keepdims=True), axis=0, keepdims=True)
        dx1, dy2, dw, dg = vjp(err / d)
        return dx1, dy2, loss, dw, dg

    dx1, dy2, loss_part, d_norm_mlp_post, dg2 = rowmap(
        "final", final_bwd, [x1, y2, target], [norm_mlp_post, g2], [(d, F32), (d, BF16)], [((1, 1), F32), ((1, d), F32), ((1, d), F32)])

    d_a = matmul("mlp_down_dx", dy2, wb_down, tb=True, out_dtypes=(BF16,), extras=[a_up],
                 epi=lambda acc, a: acc * 2.0 * jnp.maximum(a.astype(F32), 0.0))
    gw_down = matmul("mlp_down_dw", act, dy2, ta=True)
    dh2 = matmul("mlp_up_dx", d_a, wb_up, tb=True, b_quarters=True)
    gw_up = matmul("mlp_up_dw", h2, d_a, ta=True, out_quarters=True)

    def postmix_bwd(x_, u_, dx1_, dh2_, *cs):
        _, vjp = jax.vjp(_postmix, x_, u_, *cs)
        return vjp((dx1_, dh2_))

    dxa, du, d_norm_mix_post, dg1, d_norm_mlp_pre, dsc2, dsh2 = rowmap(
        "postmix_bwd", postmix_bwd, [xt, u, dx1, dh2], post_consts, [(d, F32), (d, BF16)], [((1, d), F32)] * 5)
    d_merged = matmul("w_out_dx", du, wb_out, tb=True)
    gw_out = matmul("w_out_dw", merged, du, ta=True)

    def merge_bwd(gs, gg, ys, yg, dm):
        _, vjp = jax.vjp(_merge, gs, gg, ys, yg)
        dgs, dgg, dys, dyg = vjp(dm)
        return dys, dyg, jnp.concatenate([dgs, dgg], axis=1)

    dy_ssm, dy_gdn, dproj = rowmap("merge_bwd", merge_bwd, gates + [y_ssm, y_gdn, d_merged], [],
                                   [(d, BF16), (d, BF16), (2 * d, BF16, jax.ShapeDtypeStruct((t, n_al), BF16), a_gs)])
    dy_ssm_n = matmul("ssm_up_dx", dy_ssm, wb_ssm_up, tb=True, out_dtypes=(BF16,))
    quarters_rows = lambda g: g.reshape(4, g.shape[0] // 4, g.shape[1])
    mlp_grads = [gw_up, quarters_rows(gw_down)]
    gw_ssm_up, mlp_from_sibling = matmul("ssm_up_dw", y_ssm_n, dy_ssm, ta=True, comm=exchange_halves_d2d(mlp_grads, [0, 0]))
    dy_gdn_n = matmul("gdn_up_dx", dy_gdn, wb_gdn_up, tb=True, out_dtypes=(BF16,))
    gw_gdn_up = matmul("gdn_up_dw", y_gdn_n, dy_gdn, ta=True)

    dproj, dsmall_ssm, dcw_ssm, dcb_ssm, d_sdtb, d_salog, d_sdsk, d_snw = scan_bwd(
        "ssd_bwd", ssd_step, SSM_CHUNK, 1, ssd_rows, ssd_consts, [st_ssm, tails_ssm], dy_ssm_n, [BF16, F32], 1, {0: dproj})
    dproj, dsmall_gdn, dcw_gdn, d_gdtb, d_galog, d_gnw = scan_bwd(
        "gdn_bwd", gdn_step, gdn_rows_per_step, 1, gdn_rows, gdn_consts, [st_gdn, tails_gdn], dy_gdn_n, [BF16, F32], 1,
        {0: dproj}, kept=inv_gdn)
    tail = n_al - a_small
    dproj = rowmap("small_sum", lambda a, b: jnp.concatenate([a + b, jnp.zeros((a.shape[0], tail - LANES), F32)], axis=1),
                   [dsmall_ssm, dsmall_gdn], [], [(tail, BF16, dproj, a_small)])[0]
    mix32, mix16 = reduce_on_chip("mix", [quarters_rows(gw_ssm_up), quarters_rows(gw_gdn_up), quarters_rows(gw_out)], [0] * 3)
    mlp32, mlp16 = reduce_on_chip("mlp", mlp_grads, [0, 0], mlp_from_sibling)
    rest32, rest16 = mix32 + mlp32, mix16 + mlp16
    gw_in_al, rest_chips = matmul("in_proj_dw", dproj, h1, ta=True, comm=exchange_quarters_ici(rest16))
    in32, in16 = reduce_on_chip("in", [quarters_rows(from_aligned(gw_in_al))], [1])
    dh1, in_chips = matmul("in_proj_dx", dproj, wb_in, comm=exchange_quarters_ici(in16))

    def premix_bwd(x_, dxa_, dh1_, w_, sc_, sh_):
        _, vjp = jax.vjp(_premix, x_, w_, sc_, sh_)
        dx, dw, dsc, dsh = vjp(dh1_)
        return dx + dxa_, dw, dsc, dsh

    grad_x, d_norm_mix_pre, dsc1, dsh1 = rowmap(
        "premix_bwd", premix_bwd, [xt, dxa, dh1], [norm_mix_pre, sc1, sh1], [(d, F32)], [((1, d), F32)] * 3)

    dmod_all = gather_flat("ag_dmod", jnp.concatenate([dsh1, dsc1, dg1, dsh2, dsc2, dg2], axis=1).reshape(-1))
    dmod_q = lax.dynamic_slice_in_dim(dmod_all, quarter * n_ada, n_ada, axis=1)
    gw_ada, gb_ada = _whole(
        "ada_bwd", lambda ca, dq_, da_: (_bdot(_silu(ca), dq_, TN), jnp.sum(da_, axis=0, keepdims=True)),
        [c_all, dmod_q, dmod_all], [((d, n_ada), F32), ((1, 4 * n_ada), F32)])

    partial = [d_norm_mix_pre, d_norm_mix_post, dcw_ssm, dcb_ssm, d_sdtb[:, LANE_DT:LANE_DT + hs], d_salog[:, LANE_DT:LANE_DT + hs],
               d_sdsk[:, LANE_DT:LANE_DT + hs], d_snw, dcw_gdn, d_gdtb[:, LANE_A:LANE_A + hv], d_galog[:, LANE_A:LANE_A + hv], d_gnw,
               d_norm_mlp_pre, d_norm_mlp_post, loss_part]
    sizes = [p.size for p in partial]
    stacked = gather_flat("ag_grads", jnp.concatenate([p.reshape(-1) for p in partial]))
    summed = _whole("small_sum8", lambda s: jnp.sum(s, axis=0, keepdims=True), [stacked], [((1, stacked.shape[1]), F32)])[0][0]
    offs = [0]
    for s in sizes:
        offs.append(offs[-1] + s)
    red = [summed[offs[i]:offs[i + 1]] for i in range(len(sizes))]
    loss = red[-1][0]
    my_cols = lambda full: lax.dynamic_slice_in_dim(full.reshape(CONV_K, -1), quarter * (n_cw // CONV_K), n_cw // CONV_K, axis=1)
    small_grads = {
        "b_ada": gb_ada, "norm_mix_pre": red[0], "norm_mix_post": red[1], "ssm_conv_w": my_cols(red[2]), "ssm_conv_b": red[3],
        "ssm_dt_bias": red[4], "ssm_A_log": red[5], "ssm_D": red[6], "ssm_norm_w": red[7], "gdn_conv_w": my_cols(red[8]),
        "gdn_dt_bias": red[9], "gdn_A_log": red[10], "gdn_norm_w": red[11], "norm_mlp_pre": red[12], "norm_mlp_post": red[13]}

    big_names = ["w_in", "w_ssm_up", "w_gdn_up", "w_out", "w_mlp_up", "w_mlp_down"]
    big_grads = dict(zip(big_names, reduce_across_chips(in32 + rest32, in_chips + rest_chips)))

    names = ['w_ada', 'b_ada', 'norm_mix_pre', 'norm_mix_post', 'w_in', 'ssm_conv_w', 'ssm_conv_b', 'ssm_dt_bias', 'ssm_A_log', 'ssm_D',
             'ssm_norm_w', 'gdn_conv_w', 'gdn_dt_bias', 'gdn_A_log', 'gdn_norm_w', 'w_ssm_up', 'w_gdn_up', 'w_out', 'norm_mlp_pre',
             'norm_mlp_post', 'w_mlp_up', 'w_mlp_down']
    grad, delta, new_m, new_v = {}, {}, {}, {}
    for n, (mine, other) in big_grads.items():
        view, axis = (transposed, 1) if n == "w_in" else ((lambda a: a), 0)
        res = adamw_halves("adamw_" + n, view(args[n]), mine, other, view(args["m_" + n]), view(args["v_" + n]), axis)
        grad[n], delta[n], new_m[n], new_v[n] = [view(a) for a in res]
    grad["w_ada"] = gw_ada.reshape(w_ada.shape)
    delta["w_ada"], new_m["w_ada"], new_v["w_ada"] = adamw("adamw_w_ada", w_ada, gw_ada, m_w_ada, v_w_ada)
    small_names = [n for n in names if n not in grad]
    flat = lambda pre: jnp.concatenate([args[pre + n].reshape(-1) for n in small_names]).reshape(1, 1, -1)
    g_flat = jnp.concatenate([small_grads[n].reshape(-1) for n in small_names]).reshape(1, -1)
    dl, nm, nv = adamw("adamw_small", flat(""), g_flat, flat("m_"), flat("v_"))
    off = 0
    for n in small_names:
        shape = args[n].shape
        size = args[n].size
        grad[n], delta[n], new_m[n], new_v[n] = [a.reshape(-1)[off:off + size].reshape(shape) for a in (g_flat, dl, nm, nv)]
        off += size

    return (loss, grad_x.reshape(x.shape), *[grad[n] for n in names], *[delta[n] for n in names],
            *[new_m[n] for n in names], *[new_v[n] for n in names])
```

```python
import functools

import jax
import jax.numpy as jnp
from jax import lax
from jax.experimental import pallas as pl
from jax.experimental.pallas import tpu as pltpu

F32 = jnp.float32
BF16 = jnp.bfloat16
MESH = pl.DeviceIdType.MESH

EPS = 1e-6
SSM_HEAD_DIM = 64
SSM_HEADS_PER_GROUP = 4
SSM_D_STATE = 128
SSM_CHUNK = 128
GDN_HEAD = 128
GDN_CHUNK = 64
CONV_K = 4
TAIL_ROWS = 8
LANE_DT, LANE_B, LANE_A = 0, 32, 48
ADAM_LR, ADAM_B1, ADAM_B2, ADAM_EPS, ADAM_WD, ADAM_STEP = 0.001, 0.9, 0.999, 1e-08, 0.01, 10

VMEM_LIMIT_BYTES = 56 * 1024 * 1024
LANES = 128
N_DEV = 8

NN = (((1,), (0,)), ((), ()))
NT = (((1,), (1,)), ((), ()))
TN = (((0,), (0,)), ((), ()))


BNN = (((2,), (1,)), ((0,), (0,)))
BNT = (((2,), (2,)), ((0,), (0,)))
BTN = (((1,), (1,)), ((0,), (0,)))
_KIND = {NN: ("NN", 0), NT: ("NT", 0), TN: ("TN", 0), BNN: ("NN", 1), BNT: ("NT", 1), BTN: ("TN", 1)}
_DIMS = {"NN": (NN, BNN), "NT": (NT, BNT), "TN": (TN, BTN)}


def _dg(a, b, dims):
    return lax.dot_general(a, b, dims, preferred_element_type=F32)


def _raw_bf16(a, b, dims):
    return _dg(a.astype(BF16), b.astype(BF16), dims)


def _raw_bf16x3(a, b, dims):
    ah, bh = a.astype(BF16), b.astype(BF16)
    al, bl = (a - ah.astype(F32)).astype(BF16), (b - bh.astype(F32)).astype(BF16)
    return _dg(ah, bh, dims) + (_dg(ah, bl, dims) + _dg(al, bh, dims))


def _make_dot(raw):
    @functools.partial(jax.custom_vjp, nondiff_argnums=(2,))
    def dot(a, b, dims):
        return raw(a, b, dims)

    def fwd(a, b, dims):
        return raw(a, b, dims), (a, b)

    def bwd(dims, res, ct):
        a, b = res
        kind, batched = _KIND[dims]
        d = lambda k: _DIMS[k][batched]
        if kind == "NN":
            da, db = raw(ct, b, d("NT")), raw(a, ct, d("TN"))
        elif kind == "NT":
            da, db = raw(ct, b, d("NN")), raw(ct, a, d("TN"))
        else:
            da, db = raw(b, ct, d("NT")), raw(a, ct, d("NN"))
        return da.astype(a.dtype), db.astype(b.dtype)

    dot.defvjp(fwd, bwd)
    return lambda a, b, dims=NN: dot(a, b, dims)


_bdot = _make_dot(_raw_bf16)
_hdot = _make_dot(_raw_bf16x3)


def _mask_dot(mask, x, dims):
    m = mask.astype(BF16)
    hi = x.astype(BF16)
    r = x - hi.astype(F32)
    mid = r.astype(BF16)
    lo = (r - mid.astype(F32)).astype(BF16)
    return sum(_dg(m, p, dims) for p in (hi, mid, lo))


def _sigmoid(x):
    return 0.5 * jnp.tanh(0.5 * x) + 0.5


def _silu(x):
    return x * _sigmoid(x)


def _softplus(x):
    return jnp.maximum(x, 0.0) + jnp.log(1.0 + jnp.exp(-jnp.abs(x)))


def _rms(x, w):
    return x * lax.rsqrt(jnp.mean(x * x, axis=-1, keepdims=True) + EPS) * w


def _lane_col(m, idx):
    lane = lax.broadcasted_iota(jnp.int32, m.shape, 1)
    return jnp.sum(jnp.where(lane == idx, m, 0.0), axis=1, keepdims=True)


def _tril(n, strict=False, seg=None):
    r = lax.broadcasted_iota(jnp.int32, (n, n), 0)
    c = lax.broadcasted_iota(jnp.int32, (n, n), 1)
    low = (r > c) if strict else (r >= c)
    if seg is None or seg >= n:
        return low
    shift = seg.bit_length() - 1
    return jnp.logical_and(low, (r >> shift) == (c >> shift))


@functools.partial(jax.custom_vjp, nondiff_argnums=(1,))
def _cumsum_rows(x, seg):
    return _mask_dot(_tril(x.shape[0], seg=seg), x, NN)


def _cumsum_rows_fwd(x, seg):
    return _cumsum_rows(x, seg), None


def _cumsum_rows_bwd(seg, _, ct):
    return (_mask_dot(_tril(ct.shape[0], seg=seg), ct, TN),)


_cumsum_rows.defvjp(_cumsum_rows_fwd, _cumsum_rows_bwd)


def _head_rows(m_t, idx):
    sub = lax.broadcasted_iota(jnp.int32, m_t.shape, 0)
    return jnp.sum(jnp.where(sub == idx, m_t, 0.0), axis=0, keepdims=True)


def _params(sem):
    return pltpu.CompilerParams(dimension_semantics=sem, vmem_limit_bytes=VMEM_LIMIT_BYTES)


def _into_plumbing(outs, first_input):
    arrays, aliases = [], {}
    for k, o in enumerate(outs):
        if len(o) > 4 and not isinstance(o[4], jax.ShapeDtypeStruct):
            aliases[first_input + len(arrays)] = k
            arrays.append(o[4])
    return arrays, aliases


def blockmap(name, fn, grid, ins, outs, accs=(), scalars=None):
    n_in, n_out, n_acc = len(ins), len(outs), len(accs)
    n_grid = len(grid)
    n_pre = 0 if scalars is None else 1
    into_arrays, aliases = _into_plumbing(outs, n_pre + n_in)
    n_into = len(into_arrays)

    def body(*refs):
        refs = refs[n_pre:n_pre + n_in] + refs[n_pre + n_in + n_into:]
        vals = fn(*[r[...] for r in refs[:n_in]])
        if not isinstance(vals, (tuple, list)):
            vals = (vals,)
        for r, v in zip(refs[n_in:n_in + n_out], vals[:n_out]):
            r[...] = v.astype(r.dtype)
        if n_acc:
            first = functools.reduce(jnp.logical_and, [pl.program_id(a) == 0 for a in range(n_grid)])
            acc_refs = refs[n_in + n_out:]

            @pl.when(first)
            def _():
                for r in acc_refs:
                    r[...] = jnp.zeros(r.shape, r.dtype)

            for r, v in zip(acc_refs, vals[n_out:]):
                r[...] += v.astype(r.dtype)

    zeros = lambda nd: (lambda *_: (0,) * nd)
    in_specs = [pl.BlockSpec(b, im) for _, b, im in ins] + [pl.BlockSpec(memory_space=pl.ANY)] * n_into
    out_specs = [pl.BlockSpec(o[2], o[3]) for o in outs] + [pl.BlockSpec(s, zeros(len(s))) for s, _ in accs]
    out_shape = [jax.ShapeDtypeStruct(o[0], o[1]) for o in outs] + [jax.ShapeDtypeStruct(s, d) for s, d in accs]
    cparams = _params(("arbitrary",) * n_grid if n_acc else ("parallel",) * n_grid)
    arrays = [a for a, _, _ in ins] + into_arrays
    if scalars is None:
        return pl.pallas_call(body, name=name, grid=grid, in_specs=in_specs, out_specs=out_specs, out_shape=out_shape,
                              input_output_aliases=aliases, compiler_params=cparams)(*arrays)
    spec = pltpu.PrefetchScalarGridSpec(num_scalar_prefetch=1, grid=grid, in_specs=in_specs, out_specs=out_specs)
    return pl.pallas_call(body, name=name, grid_spec=spec, out_shape=out_shape, input_output_aliases=aliases,
                          compiler_params=cparams)(scalars, *arrays)


def rowmap(name, fn, rows, consts, outs, accs=(), rb=512):
    norm = [(r, r.shape[1], 0) if not isinstance(r, tuple) else (r[0], r[1], r[2] // r[1]) for r in rows]
    assert all(not isinstance(r, tuple) or r[2] % r[1] == 0 for r in rows)
    t = norm[0][0].shape[0]
    rb = min(rb, t)
    ins = [(a, (rb, n), (lambda i, cb=cb: (i, cb))) for a, n, cb in norm]
    ins += [(cst, cst.shape, (lambda i, nd=cst.ndim: (0,) * nd)) for cst in consts]
    o = []
    for out in outs:
        if len(out) == 2:
            o.append(((t, out[0]), out[1], (rb, out[0]), lambda i: (i, 0)))
        else:
            n, d, into, off = out
            assert off % n == 0 and into.dtype == d
            o.append((into.shape, d, (rb, n), (lambda i, cb=off // n: (i, cb)), into))
    return blockmap(name, fn, (t // rb,), ins, o, accs)


MM_TILE_M, MM_TILE_N, MM_TILE_K = 1024, 1024, 2048


def _tile(dim, cap):
    if dim <= cap:
        return dim
    best = max(t for t in range(LANES, cap + 1, LANES) if dim % t == 0)
    return best


def matmul(name, a, b, ta=False, tb=False, out_dtypes=(F32,), epi=None, extras=(), comm=None, b_quarters=False, out_quarters=False):
    (k_dim, m_dim) = a.shape if ta else a.shape[::-1]
    if b_quarters:
        quarter = b.shape[2]
        b_rows, b_cols = b.shape[1], 4 * quarter
    else:
        b_rows, b_cols = b.shape
    n_dim = b_rows if tb else b_cols
    assert (b_cols if tb else b_rows) == k_dim, (name, a.shape, b.shape)
    tm, tn, tk = _tile(m_dim, MM_TILE_M), _tile(n_dim, MM_TILE_N), _tile(k_dim, MM_TILE_K)
    if b_quarters and tb:
        tk = quarter
    elif b_quarters or out_quarters:
        tn = quarter if b_quarters else n_dim // 4
    grid = (m_dim // tm, n_dim // tn, k_dim // tk)
    k_steps = grid[2]
    n_extra, n_out = len(extras), len(out_dtypes)
    n_cin, n_cout = (len(comm.operands), len(comm.out_shapes)) if comm else (0, 0)
    dims = (((0 if ta else 1,), (1 if tb else 0,)), ((), ()))

    def body(*refs):
        ins, outs, scratch = refs[:2 + n_extra + n_cin], refs[2 + n_extra + n_cin:][:n_out + n_cout], refs[2 + n_extra + n_cin + n_out + n_cout:]
        extra_refs, out_refs = ins[2:2 + n_extra], outs[:n_out]
        ids = [pl.program_id(ax) for ax in range(3)]
        if comm:
            comm_refs = (ins[2 + n_extra:], outs[n_out:], scratch[-2], scratch[-1])

            step = (ids[0] * grid[1] + ids[1]) * grid[2] + ids[2]

            @pl.when(step == 0)
            def _():
                comm.start(*comm_refs)

            @pl.when(step == (grid[0] * grid[1] * grid[2]) // 2)
            def _():
                comm.middle(*comm_refs)

        def finish(acc):
            vals = (acc,) if epi is None else epi(acc, *[r[...] for r in extra_refs])
            if not isinstance(vals, (tuple, list)):
                vals = (vals,)
            for r, v in zip(out_refs, vals):
                r[...] = v.astype(r.dtype)

        prod = lax.dot_general(ins[0][...].astype(BF16), ins[1][...].astype(BF16), dims, preferred_element_type=F32)
        if k_steps == 1:
            finish(prod)
        else:
            acc_ref = scratch[0]

            @pl.when(ids[2] == 0)
            def _():
                acc_ref[...] = jnp.zeros(acc_ref.shape, F32)

            acc_ref[...] += prod

            @pl.when(ids[2] == k_steps - 1)
            def _():
                finish(acc_ref[...])

        if comm:
            @pl.when(functools.reduce(jnp.logical_and, [i == g - 1 for i, g in zip(ids, grid)]))
            def _():
                comm.finish(*comm_refs)

    a_spec = pl.BlockSpec((tk, tm), lambda i, j, k: (k, i)) if ta else pl.BlockSpec((tm, tk), lambda i, j, k: (i, k))
    if b_quarters:
        b_spec = pl.BlockSpec((None, tn, tk), lambda i, j, k: (k, j, 0)) if tb else pl.BlockSpec((None, tk, tn), lambda i, j, k: (j, k, 0))
    else:
        b_spec = pl.BlockSpec((tn, tk), lambda i, j, k: (j, k)) if tb else pl.BlockSpec((tk, tn), lambda i, j, k: (k, j))
    mn_spec = pl.BlockSpec((tm, tn), lambda i, j, k: (i, j))
    out_spec = pl.BlockSpec((None, tm, tn), lambda i, j, k: (j, i, 0)) if out_quarters else mn_spec
    out_dims = (4, m_dim, tn) if out_quarters else (m_dim, n_dim)
    scratch_shapes = [] if k_steps == 1 else [pltpu.VMEM((tm, tn), F32)]
    if comm:
        scratch_shapes += [pltpu.SemaphoreType.DMA(comm.sem_shape), pltpu.SemaphoreType.DMA(comm.sem_shape)]
    res = pl.pallas_call(
        body, name=name, grid=grid,
        in_specs=[a_spec, b_spec] + [mn_spec] * n_extra + [ANY] * n_cin,
        out_specs=[out_spec] * n_out + [ANY] * n_cout,
        out_shape=[jax.ShapeDtypeStruct(out_dims, d) for d in out_dtypes] + (comm.out_shapes if comm else []),
        scratch_shapes=scratch_shapes,
        compiler_params=_params(("arbitrary",) * 3 if comm else ("parallel", "parallel", "arbitrary")),
    )(a, b, *extras, *(comm.operands if comm else []))
    main = res[:n_out] if n_out > 1 else res[0]
    return (main, list(res[n_out:])) if comm else main


def ssd_step(g0, state, tail, zxbc, small, cw, cb, p_dtb, p_alog, p_dsk, nw):
    d_in, gn = state.shape[0] * 2 * LANES, state.shape[0] * LANES
    z = zxbc[:, :d_in]
    act, new_tail = _conv_silu_carried(tail, zxbc[:, d_in:], cw, cb)
    xs, bm, cm = act[:, :d_in], act[:, d_in:d_in + gn], act[:, d_in + gn:]
    hb, n = state.shape[0], xs.shape[0]
    n_pair, n_head = 2 * hb, 4 * hb
    causal = _tril(n)
    dt_all = _softplus(small + p_dtb)
    a_all = dt_all * (-jnp.exp(p_alog))
    acum_all = _cumsum_rows(a_all, n)
    acum_t = acum_all.T
    lane0 = LANE_DT + SSM_HEADS_PER_GROUP * g0
    sub = lax.broadcasted_iota(jnp.int32, acum_t.shape, 0)
    heads = range(n_head)
    acum = jnp.stack([_lane_col(acum_all, lane0 + i) for i in heads])
    acum_row = jnp.stack([jnp.sum(jnp.where(sub == lane0 + i, acum_t, 0.0), axis=0, keepdims=True) for i in heads])
    dt = jnp.stack([_lane_col(dt_all, lane0 + i) for i in heads])
    dsk = jnp.stack([_lane_col(p_dsk, lane0 + i) for i in heads])
    decay = jnp.exp(jnp.where(causal, acum - acum_row, -jnp.inf))
    a_last = acum[:, n - 1:n, :]

    def split(a):
        return [a[:, i * LANES:(i + 1) * LANES] for i in range(a.shape[1] // LANES)]

    def pairs(a, axis=2):
        even = jnp.stack([a[2 * p] for p in range(n_pair)])
        odd = jnp.stack([a[2 * p + 1] for p in range(n_pair)])
        shape = (n_pair, LANES, LANES) if axis == 1 else (n_pair, a.shape[1], LANES)
        return jnp.where(lax.broadcasted_iota(jnp.int32, shape, axis) < SSM_HEAD_DIM, even, odd)

    bms, cms = split(bm), split(cm)
    cb = _bdot(jnp.stack(cms), jnp.stack(bms), BNT)
    cbd = jnp.stack([cb[i // SSM_HEADS_PER_GROUP] for i in heads]) * decay
    xp = jnp.stack(split(xs))
    xdt = xp * pairs(dt)
    yd = _bdot(cbd, jnp.stack([xdt[i // 2] for i in heads]), BNN)
    lane = lax.broadcasted_iota(jnp.int32, (n_pair, n, LANES), 2)
    y_diag = jnp.where(lane < SSM_HEAD_DIM, jnp.stack([yd[2 * p] for p in range(n_pair)]), jnp.stack([yd[2 * p + 1] for p in range(n_pair)]))
    st = state.reshape(n_pair, LANES, LANES)
    cm2 = jnp.stack([cms[p // 2] for p in range(n_pair)])
    bm2 = jnp.stack([bms[p // 2] for p in range(n_pair)])
    y_off = _bdot(cm2, st, BNT) * pairs(jnp.exp(acum))
    new = st * pairs(jnp.exp(a_last), axis=1) + _bdot(xdt * pairs(jnp.exp(a_last - acum)), bm2, BTN)
    y = y_diag + y_off + pairs(dsk) * xp
    y = jnp.concatenate([y[p] for p in range(n_pair)], axis=1) * _silu(z)
    wide = 2 * LANES
    y = jnp.concatenate([_rms(y[:, i * wide:(i + 1) * wide], nw[i]) for i in range(hb)], axis=1)
    return new.reshape(state.shape), new_tail, y


@functools.partial(jax.custom_vjp, nondiff_argnums=(1,))
def _unit_lower_inverse(a, seg):
    n = a.shape[-1]
    r = lax.broadcasted_iota(jnp.int32, (n, n), 0)
    c = lax.broadcasted_iota(jnp.int32, (n, n), 1)
    shift = min(INVERSE_BASE, seg).bit_length() - 1
    power = jnp.where((r >> shift) == (c >> shift), a, 0.0)
    inv = (r == c).astype(F32) - power
    span = 2
    while span < (1 << shift):
        power = _hdot(power, power, BNN)
        inv = inv + _hdot(inv, power, BNN)
        span *= 2
    while (1 << shift) < seg:
        below = jnp.logical_and((r >> (shift + 1)) == (c >> (shift + 1)), (r >> shift) != (c >> shift))
        inv = inv - _hdot(inv, _hdot(jnp.where(below, a, 0.0), inv, BNN), BNN)
        shift += 1
    return inv


def _unit_lower_inverse_fwd(a, seg):
    inv = _unit_lower_inverse(a, seg)
    return inv, inv


def _unit_lower_inverse_bwd(seg, inv, ct):
    return (-_hdot(_hdot(inv, ct, BTN), inv, BNT),)


_unit_lower_inverse.defvjp(_unit_lower_inverse_fwd, _unit_lower_inverse_bwd)


@jax.custom_vjp
def _known_inverse(a, inv):
    return inv


def _known_inverse_fwd(a, inv):
    return inv, inv


def _known_inverse_bwd(inv, ct):
    return _unit_lower_inverse_bwd(None, inv, ct)[0], jnp.zeros_like(inv)


_known_inverse.defvjp(_known_inverse_fwd, _known_inverse_bwd)


@functools.partial(jax.custom_vjp, nondiff_argnums=(1,))
def _rotate_rows(x, k):
    return x if k == 0 else pltpu.roll(x, k % x.shape[0], 0)


def _rotate_rows_fwd(x, k):
    return _rotate_rows(x, k), None


def _rotate_rows_bwd(k, _, ct):
    return (_rotate_rows(ct, -k),)


_rotate_rows.defvjp(_rotate_rows_fwd, _rotate_rows_bwd)


def _conv_silu_carried(tail, x, cw, cb=0.0):
    n = x.shape[0]
    ext = jnp.concatenate([tail, x], axis=0)
    pre = cb + sum(cw[j:j + 1, :] * _rotate_rows(ext, CONV_K - 1 - j)[TAIL_ROWS:] for j in range(CONV_K))
    return _silu(pre), x[n - TAIL_ROWS:]


def _l2norm(x):
    return x * lax.rsqrt(jnp.sum(x * x, axis=-1, keepdims=True) + EPS)


def gdn_step(hq0, state, tail, qkvz, small, cw, p_dtb, p_alog, nw, keep=False, kept=None):
    n, chunk = qkvz.shape[0], GDN_CHUNK
    hb = state.shape[0]
    nb = 2 * hb
    cur = state.reshape(nb, LANES, LANES)
    conv_cols = 4 * hb * LANES
    act, new_tail = _conv_silu_carried(tail, qkvz[:, :conv_cols], cw)
    q, k, v = act[:, :hb * LANES], act[:, hb * LANES:2 * hb * LANES], act[:, 2 * hb * LANES:]
    z = qkvz[:, conv_cols:]
    causal, strict = _tril(n, seg=chunk), _tril(n, True, seg=chunk)
    beta_all = _sigmoid(small)
    g_all = -jnp.exp(p_alog) * _softplus(small + p_dtb)
    gcum_all = _cumsum_rows(g_all, chunk)
    gcum_t = gcum_all.T
    split = lambda a: [a[:, i * LANES:(i + 1) * LANES] for i in range(a.shape[1] // LANES)]
    per_value_head = lambda a: jnp.stack([a[i // 2] for i in range(nb)])
    qh, kh = _l2norm(jnp.stack(split(q))) * (GDN_HEAD ** -0.5), _l2norm(jnp.stack(split(k)))
    q2, k2 = per_value_head(qh), per_value_head(kh)
    v2, z2 = jnp.stack(split(v)), jnp.stack(split(z))
    gcum = jnp.stack([_lane_col(gcum_all, LANE_A + 2 * hq0 + i) for i in range(nb)])
    gcum_row = jnp.stack([_head_rows(gcum_t, LANE_A + 2 * hq0 + i) for i in range(nb)])
    beta = jnp.stack([_lane_col(beta_all, LANE_B + 2 * hq0 + i) for i in range(nb)])
    dmat = jnp.exp(jnp.where(causal, gcum - gcum_row, -jnp.inf))
    a_low = jnp.where(strict, beta * per_value_head(_bdot(kh, kh, BNT)) * dmat, 0.0)
    inv = _unit_lower_inverse(a_low, chunk) if kept is None else _known_inverse(a_low, kept)
    egc = jnp.exp(gcum)
    u = _hdot(inv, v2 * beta, BNN)
    w = _hdot(inv, k2 * (beta * egc), BNN)
    q_dec = q2 * egc
    v_new, o_state = [], []
    for s in range(n // chunk):
        rows = slice(s * chunk, (s + 1) * chunk)
        v_new.append(u[:, rows] - _bdot(w[:, rows], cur, BNN))
        o_state.append(_bdot(q_dec[:, rows], cur, BNN))
        g_last = gcum[:, (s + 1) * chunk - 1:(s + 1) * chunk, :]
        k_dec = k2[:, rows] * jnp.exp(g_last - gcum[:, rows])
        cur = cur * jnp.exp(g_last) + _bdot(k_dec, v_new[-1], BTN)
    o = jnp.concatenate(o_state, axis=1) + _bdot(per_value_head(_bdot(qh, kh, BNT)) * dmat, jnp.concatenate(v_new, axis=1), BNN)
    out = _rms(o, nw) * _silu(z2)
    res = (cur.reshape(state.shape), new_tail, jnp.concatenate([out[i] for i in range(nb)], axis=1))
    return res + (inv,) if keep else res


STATE_SHAPE = (2, LANES, LANES)
GDN_CHUNKS_PER_STEP = 2
INVERSE_BASE = 16


def _scan_specs(rows, consts, chunk, chunk_of, hb):
    specs = []
    for _, n, off, per_group in rows:
        if per_group:
            assert off % (n * hb) == 0
            specs.append(pl.BlockSpec((chunk, n * hb), lambda c, g, cb=off // (n * hb): (chunk_of(c), cb + g)))
        else:
            assert off % n == 0
            specs.append(pl.BlockSpec((chunk, n), lambda c, g, cb=off // n: (chunk_of(c), cb)))
    for arr, per_group in consts:
        if per_group:
            specs.append(pl.BlockSpec((hb, 1, arr.shape[2]), lambda c, g: (g, 0, 0)))
        else:
            specs.append(pl.BlockSpec(arr.shape, lambda c, g, nd=arr.ndim: (0,) * nd))
    return specs


def scan_fwd(name, step, chunk, n_grp, rows, consts, out_cols, hb, keep=None, state_shape=None, tail_cols=None):
    t = rows[0][0].shape[0]
    nc = t // chunk
    n_rows, n_consts = len(rows), len(consts)
    state_shape = state_shape or (hb,) + STATE_SHAPE
    carried = [state_shape] + ([(TAIL_ROWS, tail_cols)] if tail_cols else [])
    n_car = len(carried)

    def body(*refs):
        row_refs, const_refs = refs[:n_rows], refs[n_rows:n_rows + n_consts]
        y_ref = refs[n_rows + n_consts]
        saved_refs = refs[n_rows + n_consts + 1:n_rows + n_consts + 1 + n_car]
        scratch = refs[-n_car:]
        c, g = pl.program_id(0), pl.program_id(1)

        @pl.when(c == 0)
        def _():
            for s, shape in zip(scratch, carried):
                s[g] = jnp.zeros(shape, F32)

        cur = [s[g] for s in scratch]
        for r, v in zip(saved_refs, cur):
            r[...] = v
        vals = [r[...] for r in row_refs] + [r[...] for r in const_refs]
        res = step(g * hb, *cur, *vals) if keep is None else step(g * hb, *cur, *vals, keep=True)
        for s, v in zip(scratch, res[:n_car]):
            s[g] = v
        y_ref[...] = res[n_car].astype(y_ref.dtype)
        if keep is not None:
            refs[n_rows + n_consts + 1 + n_car][...] = res[n_car + 1]

    lead = (nc, n_grp // hb)
    out_specs = [pl.BlockSpec((chunk, out_cols * hb), lambda c, g: (c, g))]
    out_shape = [jax.ShapeDtypeStruct((t, n_grp * out_cols), BF16)]
    for shape in carried + ([keep] if keep is not None else []):
        out_specs.append(pl.BlockSpec((None, None) + shape, lambda c, g, nd=len(shape): (c, g) + (0,) * nd))
        out_shape.append(jax.ShapeDtypeStruct(lead + shape, F32))
    return pl.pallas_call(
        body, name=name, grid=lead,
        in_specs=_scan_specs(rows, consts, chunk, lambda c: c, hb),
        out_specs=out_specs, out_shape=out_shape,
        scratch_shapes=[pltpu.VMEM((n_grp // hb,) + shape, F32) for shape in carried],
        compiler_params=_params(("arbitrary", "arbitrary")),
    )(*[r[0] for r in rows], *[c[0] for c in consts])


def scan_bwd(name, step, chunk, n_grp, rows, consts, saved, dy, row_dtypes, hb, into, kept=None):
    t = rows[0][0].shape[0]
    nc = t // chunk
    n_rows, n_consts, n_car = len(rows), len(consts), len(saved)
    carried = [s.shape[2:] for s in saved]
    out_cols = dy.shape[1] // n_grp
    n_alias = sum(not isinstance(v, jax.ShapeDtypeStruct) for v in into.values())
    n_kept = 0 if kept is None else 1
    n_in = n_rows + n_consts + n_car + 1 + n_kept + n_alias

    def body(*refs):
        row_refs, const_refs = refs[:n_rows], refs[n_rows:n_rows + n_consts]
        saved_refs = refs[n_rows + n_consts:n_rows + n_consts + n_car]
        dy_ref = refs[n_rows + n_consts + n_car]
        outs = refs[n_in:-n_car]
        scratch = refs[-n_car:]
        c, g = pl.program_id(0), pl.program_id(1)

        @pl.when(c == 0)
        def _():
            for s, shape in zip(scratch, carried):
                s[g] = jnp.zeros(shape, F32)

        @pl.when(jnp.logical_and(c == 0, g == 0))
        def _():
            for r in outs[n_rows:]:
                r[...] = jnp.zeros(r.shape, r.dtype)

        f = functools.partial(step, g * hb) if kept is None else functools.partial(step, g * hb, kept=refs[n_rows + n_consts + n_car + 1][...])
        _, vjp = jax.vjp(f, *[r[...] for r in saved_refs], *[r[...] for r in row_refs], *[r[...] for r in const_refs])
        grads = vjp(tuple(s[g] for s in scratch) + (dy_ref[...].astype(F32),))
        for s, d in zip(scratch, grads[:n_car]):
            s[g] = d
        for (_, _, _, per_group), r, d in zip(rows, outs[:n_rows], grads[n_car:n_car + n_rows]):
            if per_group:
                r[...] = d.astype(r.dtype)
            else:
                @pl.when(g == 0)
                def _(r=r):
                    r[...] = jnp.zeros(r.shape, r.dtype)

                r[...] += d.astype(r.dtype)
        for (_, per_group), r, d in zip(consts, outs[n_rows:], grads[n_car + n_rows:]):
            if per_group:
                r[pl.ds(g * hb, hb)] += d
            else:
                r[...] += d

    rev = lambda c: nc - 1 - c
    out_specs, out_shape = [], []
    into_arrays, aliases = [], {}
    first_into = n_in - n_alias
    kept_arrays = [] if kept is None else [kept]
    by_step = lambda a: pl.BlockSpec((None, None) + a.shape[2:], lambda c, g, nd=a.ndim - 2: (rev(c), g) + (0,) * nd)
    for k, ((_, n, off, per_group), dt) in enumerate(zip(rows, row_dtypes)):
        if k in into:
            assert per_group and off % (n * hb) == 0 and into[k].dtype == dt
            out_specs.append(pl.BlockSpec((chunk, n * hb), lambda c, g, cb=off // (n * hb): (rev(c), cb + g)))
            out_shape.append(jax.ShapeDtypeStruct(into[k].shape, dt))
            if not isinstance(into[k], jax.ShapeDtypeStruct):
                aliases[first_into + len(into_arrays)] = k
                into_arrays.append(into[k])
        elif per_group:
            out_specs.append(pl.BlockSpec((chunk, n * hb), lambda c, g: (rev(c), g)))
            out_shape.append(jax.ShapeDtypeStruct((t, n_grp * n), dt))
        else:
            out_specs.append(pl.BlockSpec((chunk, n), lambda c, g: (rev(c), 0)))
            out_shape.append(jax.ShapeDtypeStruct((t, n), dt))
    for arr, _ in consts:
        out_specs.append(pl.BlockSpec(arr.shape, lambda c, g, nd=arr.ndim: (0,) * nd))
        out_shape.append(jax.ShapeDtypeStruct(arr.shape, F32))
    return pl.pallas_call(
        body, name=name, grid=(nc, n_grp // hb),
        in_specs=_scan_specs(rows, consts, chunk, rev, hb) + [by_step(s) for s in saved]
        + [pl.BlockSpec((chunk, out_cols * hb), lambda c, g: (rev(c), g))] + [by_step(k) for k in kept_arrays]
        + [pl.BlockSpec(memory_space=pl.ANY)] * len(into_arrays),
        out_specs=out_specs, out_shape=out_shape, input_output_aliases=aliases,
        scratch_shapes=[pltpu.VMEM((n_grp // hb,) + shape, F32) for shape in carried],
        compiler_params=_params(("arbitrary", "arbitrary")),
    )(*[r[0] for r in rows], *[c[0] for c in consts], *saved, dy, *kept_arrays, *into_arrays)


def _place():
    return lax.axis_index("x"), lax.axis_index("y"), lax.axis_index("c")


def _other_chips(x, y):
    return [(1 - x, y), (x, 1 - y), (1 - x, 1 - y)]


ANY = pl.BlockSpec(memory_space=pl.ANY)


def all_gather8(name, v):
    m_per, n = v.shape

    def body(x_ref, out_ref, send_sems, recv_sems, local_sem):
        x, y, c = _place()
        me, sibling = (x, y, c), (x, y, 1 - c)
        chips = _other_chips(x, y)

        def rows(px, py, pc):
            return out_ref.at[pl.ds((4 * px + 2 * py + pc) * m_per, m_per), :]

        def copy(k, block, to, src=None):
            return pltpu.make_async_remote_copy(
                src_ref=rows(*block) if src is None else src, dst_ref=rows(*block),
                send_sem=send_sems.at[k], recv_sem=recv_sems.at[k], device_id=to, device_id_type=MESH)

        mine = pltpu.make_async_copy(x_ref, rows(*me), local_sem)
        mine.start()
        first = [copy(0, me, sibling, src=x_ref)]
        first += [copy(1 + q, me, (*chip, c), src=x_ref) for q, chip in enumerate(chips)]
        for cp in first:
            cp.start()
        passed = [copy(4 + q, (*chip, c), sibling) for q, chip in enumerate(chips)]
        for q, chip in enumerate(chips):
            copy(1 + q, (*chip, c), me).wait_recv()
            passed[q].start()
        copy(0, sibling, me).wait_recv()
        for q, chip in enumerate(chips):
            copy(4 + q, (*chip, 1 - c), me).wait_recv()
        for cp in first + passed:
            cp.wait_send()
        mine.wait()

    return pl.pallas_call(
        body, name=name, out_shape=jax.ShapeDtypeStruct((N_DEV * m_per, n), v.dtype),
        in_specs=[pl.BlockSpec(memory_space=pltpu.VMEM)], out_specs=pl.BlockSpec(memory_space=pltpu.VMEM),
        scratch_shapes=[pltpu.SemaphoreType.DMA((7,)), pltpu.SemaphoreType.DMA((7,)), pltpu.SemaphoreType.DMA],
    )(v)


def gather_flat(name, vec):
    n = vec.shape[0]
    n_pad = -(-n // (8 * LANES)) * (8 * LANES)
    v = jnp.pad(vec, (0, n_pad - n)).reshape(8, n_pad // 8)
    return all_gather8(name, v).reshape(N_DEV, n_pad)[:, :n]


class Exchange:
    def __init__(self, operands, out_shapes, sem_shape, start, finish, middle=None):
        self.operands, self.out_shapes, self.sem_shape = list(operands), out_shapes, sem_shape
        self.start, self.middle, self.finish = start, middle or (lambda *refs: None), finish


def _start_all_wait_all(make_copies):
    def start(*refs):
        for cp in make_copies(*refs):
            cp.start()

    def finish(*refs):
        for cp in make_copies(*refs):
            cp.wait()

    return start, finish


def run_exchange(name, ex):
    n_in, n_out = len(ex.operands), len(ex.out_shapes)

    def body(*refs):
        ins, outs = refs[:n_in], refs[n_in:n_in + n_out]
        ex.start(ins, outs, *refs[n_in + n_out:])
        ex.middle(ins, outs, *refs[n_in + n_out:])
        ex.finish(ins, outs, *refs[n_in + n_out:])

    return pl.pallas_call(
        body, name=name, out_shape=ex.out_shapes, in_specs=[ANY] * n_in, out_specs=[ANY] * n_out,
        scratch_shapes=[pltpu.SemaphoreType.DMA(ex.sem_shape), pltpu.SemaphoreType.DMA(ex.sem_shape)],
    )(*ex.operands)


def _half(shape, axis, pc):
    h = shape[axis] // 2
    return (pl.ds(pc * h, h), slice(None)) if axis == 0 else (slice(None), pl.ds(pc * h, h))


def _half_shape(shape, axis):
    return tuple(s // 2 if a == axis else s for a, s in enumerate(shape))


def all_gather_shards(shards, axes):
    n_t = len(shards)
    n_sem = 12

    def copies(ins, outs, send_sems, recv_sems):
        x, y, c = _place()
        me, sibling, x_nbr, y_nbr = (x, y, c), (x, y, 1 - c), (1 - x, y, c), (x, 1 - y, c)
        own, of_x, of_y, of_diag = 2 * x + y, 2 * (1 - x) + y, 2 * x + 1 - y, 2 * (1 - x) + 1 - y

        def copy(t, k, quarter, pc, piece, to, from_input=False):
            axis = axes[t]
            h = ins[t].shape[axis] // 4
            cut = pl.ds((2 * pc + piece) * h, h)
            part = (cut, slice(None)) if axis == 0 else (slice(None), cut)
            dst = outs[t].at[(quarter,) + part]
            return pltpu.make_async_remote_copy(
                src_ref=ins[t].at[part] if from_input else dst, dst_ref=dst,
                send_sem=send_sems.at[t, k], recv_sem=recv_sems.at[t, k], device_id=to, device_id_type=MESH)

        stages = []
        for t in range(n_t):
            direct = [copy(t, 0, own, c, 0, x_nbr, True), copy(t, 2, own, c, 1, y_nbr, True),
                      copy(t, 1, own, c, 1, x_nbr, True), copy(t, 3, own, c, 0, y_nbr, True)]
            landing = [
                (copy(t, 0, of_x, c, 0, me), [copy(t, 4, of_x, c, 0, y_nbr), copy(t, 6, of_x, c, 0, sibling)]),
                (copy(t, 2, of_y, c, 1, me), [copy(t, 5, of_y, c, 1, x_nbr), copy(t, 8, of_y, c, 1, sibling)]),
                (copy(t, 1, of_x, c, 1, me), [copy(t, 7, of_x, c, 1, sibling)]),
                (copy(t, 3, of_y, c, 0, me), [copy(t, 9, of_y, c, 0, sibling)]),
                (copy(t, 4, of_diag, c, 0, me), [copy(t, 10, of_diag, c, 0, sibling)]),
                (copy(t, 5, of_diag, c, 1, me), [copy(t, 11, of_diag, c, 1, sibling)])]
            from_sibling = [copy(t, 6, of_x, 1 - c, 0, me), copy(t, 8, of_y, 1 - c, 1, me), copy(t, 7, of_x, 1 - c, 1, me),
                            copy(t, 9, of_y, 1 - c, 0, me), copy(t, 10, of_diag, 1 - c, 0, me), copy(t, 11, of_diag, 1 - c, 1, me)]
            stages.append((direct, landing, from_sibling))
        return stages

    def start(*refs):
        for direct, _, _ in copies(*refs):
            for cp in direct:
                cp.start()

    def pass_on(landing):
        for arrived, onward in landing:
            arrived.wait_recv()
            for cp in onward:
                cp.start()

    def middle(*refs):
        for _, landing, _ in copies(*refs):
            pass_on(landing[:4])

    def finish(*refs):
        stages = copies(*refs)
        for _, landing, _ in stages:
            pass_on(landing[4:])
        for direct, landing, from_sibling in stages:
            for cp in from_sibling:
                cp.wait_recv()
            for cp in direct + [cp for _, onward in landing for cp in onward]:
                cp.wait_send()

    return Exchange(shards, [jax.ShapeDtypeStruct((4,) + s.shape, s.dtype) for s in shards], (n_t, n_sem), start, finish, middle)


def exchange_halves_d2d(grads, axes):
    n_t = len(grads)

    def copies(ins, outs, send_sems, recv_sems):
        x, y, c = _place()
        return [pltpu.make_async_remote_copy(
            src_ref=ins[t].at[(slice(None),) + _half(ins[t].shape[1:], axes[t], 1 - c)], dst_ref=outs[t],
            send_sem=send_sems.at[t], recv_sem=recv_sems.at[t], device_id=(x, y, 1 - c), device_id_type=MESH) for t in range(n_t)]

    shapes = [jax.ShapeDtypeStruct((4,) + _half_shape(g.shape[1:], a), g.dtype) for g, a in zip(grads, axes)]
    return Exchange(grads, shapes, (n_t,), *_start_all_wait_all(copies))


def exchange_quarters_ici(parts):
    n_t = len(parts)

    def copies(ins, outs, send_sems, recv_sems):
        x, y, c = _place()
        return [pltpu.make_async_remote_copy(
            src_ref=ins[t].at[2 * px + py], dst_ref=outs[t].at[q],
            send_sem=send_sems.at[t, q], recv_sem=recv_sems.at[t, q], device_id=(px, py, c), device_id_type=MESH)
            for t in range(n_t) for q, (px, py) in enumerate(_other_chips(x, y))]

    shapes = [jax.ShapeDtypeStruct((3,) + p.shape[1:], p.dtype) for p in parts]
    return Exchange(parts, shapes, (n_t, 3), *_start_all_wait_all(copies))


def swap_d2d(halves):
    n_t = len(halves)

    def copies(ins, outs, send_sems, recv_sems):
        x, y, c = _place()
        return [pltpu.make_async_remote_copy(
            src_ref=ins[t], dst_ref=outs[t], send_sem=send_sems.at[t], recv_sem=recv_sems.at[t],
            device_id=(x, y, 1 - c), device_id_type=MESH) for t in range(n_t)]

    return Exchange(halves, [jax.ShapeDtypeStruct(h.shape, h.dtype) for h in halves], (n_t,), *_start_all_wait_all(copies))


BLOCK_BYTES = 1 << 20


def _row_block(r, c):
    fits = [rb for rb in range(16, r + 1, 16) if r % rb == 0 and rb * c * 4 <= BLOCK_BYTES]
    return max(fits) if fits else r


def _place_scalars():
    x, y, c = _place()
    return jnp.stack([c, 2 * x + y]).astype(jnp.int32)


def reduce_on_chip(tag, grads, axes, from_sibling=None):
    if from_sibling is None:
        from_sibling = run_exchange(f"rs_d2d_{tag}", exchange_halves_d2d(grads, axes))
    parts, parts_bf16 = [], []
    for t, (g, s, axis) in enumerate(zip(grads, from_sibling, axes)):
        _, h, cols = s.shape
        rb = _row_block(h, cols)
        nb = h // rb
        blk = lambda k, i, s_ref: (k, i, 0)
        mine = (lambda k, i, s_ref, nb=nb: (k, s_ref[0] * nb + i, 0)) if axis == 0 else (lambda k, i, s_ref: (k, i, s_ref[0]))
        p32, p16 = blockmap(
            f"rs_add_{tag}{t}", lambda a, b: (a + b, a + b), (4, nb),
            [(g, (None, rb, cols), mine), (s, (None, rb, cols), blk)],
            [(s.shape, F32, (None, rb, cols), blk), (s.shape, BF16, (None, rb, cols), blk)], scalars=_place_scalars())
        parts.append(p32)
        parts_bf16.append(p16)
    return parts, parts_bf16


def reduce_across_chips(parts, from_chips):
    halves = []
    for t, (p, q) in enumerate(zip(parts, from_chips)):
        _, h, cols = p.shape
        rb = _row_block(h, cols)
        halves.append(blockmap(
            f"rs_sum{t}", lambda a, b: a + b[0].astype(F32) + b[1].astype(F32) + b[2].astype(F32), (h // rb,),
            [(p, (None, rb, cols), lambda i, s_ref: (s_ref[1], i, 0)), (q, (3, rb, cols), lambda i, s_ref: (0, i, 0))],
            [((h, cols), F32, (rb, cols), lambda i, s_ref: (i, 0))], scalars=_place_scalars())[0])
    return list(zip(halves, run_exchange("rs_swap", swap_d2d(halves))))


def _adamw(w, g, m, v):
    m = ADAM_B1 * m + (1.0 - ADAM_B1) * g
    v = ADAM_B2 * v + (1.0 - ADAM_B2) * jnp.square(g)
    m_hat = m / (1.0 - ADAM_B1 ** ADAM_STEP)
    v_hat = v / (1.0 - ADAM_B2 ** ADAM_STEP)
    delta = -ADAM_LR * (m_hat / (jnp.sqrt(v_hat) + ADAM_EPS) + ADAM_WD * w)
    return delta, m, v


def adamw(name, w, g, m, v):
    _, r, c = w.shape
    rb = _row_block(r, c)
    blk3 = lambda a: (a, (None, rb, c), lambda i: (0, i, 0))
    return blockmap(name, _adamw, (r // rb,), [blk3(w), (g, (rb, c), lambda i: (i, 0)), blk3(m), blk3(v)],
                    [(w.shape, F32, (None, rb, c), lambda i: (0, i, 0))] * 3)


def adamw_halves(name, w, mine, other, m, v, axis):
    _, r, c = w.shape
    h, c = mine.shape
    rb = _row_block(h, c)
    nb = h // rb

    def body(s_ref, w_ref, mine_ref, other_ref, m_ref, v_ref, g_out, d_out, m_out, v_out):
        g = jnp.where(pl.program_id(0) == s_ref[0], mine_ref[...], other_ref[...])
        d, nm, nv = _adamw(w_ref[...], g, m_ref[...], v_ref[...])
        g_out[...], d_out[...], m_out[...], v_out[...] = g, d, nm, nv

    spec3 = pl.BlockSpec((None, rb, c), (lambda k, i, s_ref: (0, k * nb + i, 0)) if axis == 0 else (lambda k, i, s_ref: (0, i, k)))
    spec2 = pl.BlockSpec((rb, c), lambda k, i, s_ref: (i, 0))
    grid_spec = pltpu.PrefetchScalarGridSpec(num_scalar_prefetch=1, grid=(2, nb), in_specs=[spec3, spec2, spec2, spec3, spec3],
                                             out_specs=[spec3] * 4)
    return pl.pallas_call(body, name=name, grid_spec=grid_spec, out_shape=[jax.ShapeDtypeStruct(w.shape, F32)] * 4,
                          compiler_params=_params(("parallel", "parallel")))(_place_scalars(), w, mine, other, m, v)


def _whole(name, fn, ins, outs):
    return blockmap(name, fn, (1,), [(a, a.shape, lambda i, nd=a.ndim: (0,) * nd) for a in ins],
                    [(s, d, s, lambda i, nd=len(s): (0,) * nd) for s, d in outs])


def _premix(x, w, sc, sh):
    return _rms(x, w) * (1.0 + sc) + sh


def _postmix(x, u, w_post, g1, w_pre2, sc2, sh2):
    x1 = x + g1 * _rms(u, w_post)
    return x1, _premix(x1, w_pre2, sc2, sh2)


def _merge(gs, gg, ys, yg):
    return _sigmoid(gs) * ys + _sigmoid(gg) * yg


def _final(x1, y2, w_post2, g2):
    return x1 + g2 * _rms(y2, w_post2)


def kernel(x, c, w_ada, b_ada, norm_mix_pre, norm_mix_post, w_in, ssm_conv_w, ssm_conv_b, ssm_dt_bias, ssm_A_log, ssm_D, ssm_norm_w, gdn_conv_w, gdn_dt_bias, gdn_A_log, gdn_norm_w, w_ssm_up, w_gdn_up, w_out, norm_mlp_pre, norm_mlp_post, w_mlp_up, w_mlp_down, loss_target, m_w_ada, m_b_ada, m_norm_mix_pre, m_norm_mix_post, m_w_in, m_ssm_conv_w, m_ssm_conv_b, m_ssm_dt_bias, m_ssm_A_log, m_ssm_D, m_ssm_norm_w, m_gdn_conv_w, m_gdn_dt_bias, m_gdn_A_log, m_gdn_norm_w, m_w_ssm_up, m_w_gdn_up, m_w_out, m_norm_mlp_pre, m_norm_mlp_post, m_w_mlp_up, m_w_mlp_down, v_w_ada, v_b_ada, v_norm_mix_pre, v_norm_mix_post, v_w_in, v_ssm_conv_w, v_ssm_conv_b, v_ssm_dt_bias, v_ssm_A_log, v_ssm_D, v_ssm_norm_w, v_gdn_conv_w, v_gdn_dt_bias, v_gdn_A_log, v_gdn_norm_w, v_w_ssm_up, v_w_gdn_up, v_w_out, v_norm_mlp_pre, v_norm_mlp_post, v_w_mlp_up, v_w_mlp_down):
    args = dict(locals())
    xi, yi, ci = _place()
    quarter = 2 * xi + yi
    batch = 4 * xi + 2 * yi + ci

    xt, target = x[0], loss_target[0]
    t, d = xt.shape
    hs, hv = ssm_dt_bias.shape[-1], gdn_dt_bias.shape[-1]
    assert hs <= LANE_B - LANE_DT and hv <= LANE_A - LANE_B and hv % 2 == 0 and hs % SSM_HEADS_PER_GROUP == 0
    assert t % SSM_CHUNK == 0 and t % (GDN_CHUNK * GDN_CHUNKS_PER_STEP) == 0 and d % LANES == 0
    d_inner = hs * SSM_HEAD_DIM
    n_grp = hs // SSM_HEADS_PER_GROUP
    gn = n_grp * SSM_D_STATE
    conv_ssm = d_inner + 2 * gn
    hq = hv // 2
    key, val = hq * GDN_HEAD, hv * GDN_HEAD
    conv_gdn = 2 * key + val
    o_dt = d_inner + conv_ssm
    o_qkv = o_dt + hs
    o_b = o_qkv + conv_gdn + val
    o_a = o_b + hv
    o_gs = o_a + hv
    n_proj = o_gs + 2 * d
    assert 4 * w_in.shape[-1] == n_proj and ssm_conv_w.shape[-1] * 4 == conv_ssm == conv_gdn
    a_z, a_q = 0, o_dt
    a_gs = a_q + conv_gdn + val
    a_gg = a_gs + d
    a_small = a_gg + d
    n_al = -(-(a_small + LANES) // MM_TILE_N) * MM_TILE_N

    def to_aligned(w):
        z = lambda n: jnp.zeros((n, w.shape[1]), w.dtype)
        return jnp.concatenate([
            w[:o_dt], w[o_qkv:o_b], w[o_gs:],
            w[o_dt:o_qkv], z(LANE_B - hs), w[o_b:o_a], z(LANE_A - LANE_B - hv), w[o_a:o_gs], z(LANES - LANE_A - hv),
            z(n_al - a_small - LANES)], axis=0)

    def from_aligned(w):
        s = a_small
        return jnp.concatenate([
            w[:o_dt], w[s + LANE_DT:s + LANE_DT + hs], w[a_q:a_gs], w[s + LANE_B:s + LANE_B + hv],
            w[s + LANE_A:s + LANE_A + hv], w[a_gs:a_small]], axis=0)

    def lanes(vec, at):
        return jnp.zeros((1, LANES), F32).at[:, at:at + vec.shape[-1]].set(vec.reshape(1, -1))

    n_cw = CONV_K * ssm_conv_w.shape[-1]
    small_in = gather_flat("ag_small", jnp.concatenate([c.reshape(-1), ssm_conv_w.reshape(-1), gdn_conv_w.reshape(-1)]))
    c_all = small_in[:, :d]
    by_chip = small_in[0::2]

    def whole_conv_w(lo):
        return jnp.transpose(by_chip[:, lo:lo + n_cw].reshape(4, CONV_K, -1), (1, 0, 2)).reshape(CONV_K, -1)

    cw_ssm, cw_gdn = whole_conv_w(d), whole_conv_w(d + n_cw)
    cb_ssm = ssm_conv_b

    n_ada = w_ada.shape[-1]
    b_q = lax.dynamic_slice_in_dim(b_ada, quarter * n_ada, n_ada, axis=1)
    mod_q = _whole("ada_fwd", lambda ca, w, b: _bdot(_silu(ca), w) + b, [c_all, w_ada[0], b_q], [((N_DEV, n_ada), F32)])[0]
    mod_all = gather_flat("ag_mod", mod_q.reshape(-1)).reshape(N_DEV, N_DEV, n_ada)[0::2]
    mod = lax.dynamic_index_in_dim(mod_all, batch, axis=1, keepdims=False).reshape(1, 4 * n_ada)
    sh1, sc1, g1, sh2, sc2, g2 = [mod[:, i * d:(i + 1) * d] for i in range(6)]

    transposed = lambda a: jnp.swapaxes(a, 1, 2)
    own = [w.astype(BF16) for w in (transposed(w_in)[0], w_ssm_up[0], w_gdn_up[0], w_out[0], w_mlp_up[0], w_mlp_down[0])]
    with_own = lambda gs, ws: [lax.dynamic_update_index_in_dim(g, w, quarter, 0) for g, w in zip(gs, ws)]
    rows_major = lambda g: g.reshape(-1, g.shape[2])
    wb_in = to_aligned(rows_major(with_own(run_exchange("ag_w_in", all_gather_shards(own[:1], [1])), own[:1])[0]))

    h1 = rowmap("premix", _premix, [xt], [norm_mix_pre, sc1, sh1], [(d, BF16)])[0]
    proj, gathered = matmul("in_proj", h1, wb_in, tb=True, comm=all_gather_shards(own[1:], [0] * 5))
    gathered = with_own(gathered, own[1:])
    wb_ssm_up, wb_gdn_up, wb_out = rows_major(gathered[0]), rows_major(gathered[1]), rows_major(gathered[2])
    wb_up, wb_down = gathered[3], rows_major(gathered[4])

    wide = 2 * LANES
    ssd_rows = [(proj, d_inner + conv_ssm, a_z, True), (proj, LANES, a_small, False)]
    ssd_consts = [(cw_ssm, False), (cb_ssm, False), (lanes(ssm_dt_bias, LANE_DT), False), (lanes(ssm_A_log, LANE_DT), False),
                  (lanes(ssm_D, LANE_DT), False), (ssm_norm_w.reshape(n_grp, 1, wide), False)]
    y_ssm_n, st_ssm, tails_ssm = scan_fwd("ssd_fwd", ssd_step, SSM_CHUNK, 1, ssd_rows, ssd_consts, d_inner, 1,
                                          state_shape=(n_grp,) + STATE_SHAPE, tail_cols=conv_ssm)
    gdn_rows = [(proj, conv_gdn + val, a_q, True), (proj, LANES, a_small, False)]
    gdn_consts = [(cw_gdn, False), (lanes(gdn_dt_bias, LANE_A), False), (lanes(gdn_A_log, LANE_A), False), (gdn_norm_w, False)]
    gdn_rows_per_step = GDN_CHUNK * GDN_CHUNKS_PER_STEP
    y_gdn_n, st_gdn, tails_gdn, inv_gdn = scan_fwd(
        "gdn_fwd", gdn_step, gdn_rows_per_step, 1, gdn_rows, gdn_consts, val, 1, keep=(hv, gdn_rows_per_step, gdn_rows_per_step),
        state_shape=(hq,) + STATE_SHAPE, tail_cols=conv_gdn)

    y_ssm = matmul("ssm_up", y_ssm_n, wb_ssm_up)
    y_gdn = matmul("gdn_up", y_gdn_n, wb_gdn_up)
    gates = [(proj, d, a_gs), (proj, d, a_gg)]
    merged = rowmap("merge", _merge, gates + [y_ssm, y_gdn], [], [(d, BF16)])[0]
    u = matmul("w_out", merged, wb_out)
    post_consts = [norm_mix_post, g1, norm_mlp_pre, sc2, sh2]
    x1, h2 = rowmap("postmix", _postmix, [xt, u], post_consts, [(d, F32), (d, BF16)])
    relu2 = lambda acc: (acc, jnp.square(jnp.maximum(acc, 0.0)))
    a_up, act = matmul("mlp_up", h2, wb_up, out_dtypes=(BF16, BF16), epi=relu2, b_quarters=True)
    y2 = matmul("mlp_down", act, wb_down)

    def final_bwd(x1_, y2_, tgt, w_, g_):
        x2, vjp = jax.vjp(_final, x1_, y2_, w_, g_)
        err = x2 - tgt
        loss = 0.5 * jnp.sum(jnp.mean(err * err, axis=-1, keepdims=True), axis=0, keepdims=True)
        dx1, dy2, dw, dg = vjp(err / d)
        return dx1, dy2, loss, dw, dg

    dx1, dy2, loss_part, d_norm_mlp_post, dg2 = rowmap(
        "final", final_bwd, [x1, y2, target], [norm_mlp_post, g2], [(d, F32), (d, BF16)], [((1, 1), F32), ((1, d), F32), ((1, d), F32)])

    d_a = matmul("mlp_down_dx", dy2, wb_down, tb=True, out_dtypes=(BF16,), extras=[a_up],
                 epi=lambda acc, a: acc * 2.0 * jnp.maximum(a.astype(F32), 0.0))
    gw_down = matmul("mlp_down_dw", act, dy2, ta=True)
    dh2 = matmul("mlp_up_dx", d_a, wb_up, tb=True, b_quarters=True)
    gw_up = matmul("mlp_up_dw", h2, d_a, ta=True, out_quarters=True)

    def postmix_bwd(x_, u_, dx1_, dh2_, *cs):
        _, vjp = jax.vjp(_postmix, x_, u_, *cs)
        return vjp((dx1_, dh2_))

    dxa, du, d_norm_mix_post, dg1, d_norm_mlp_pre, dsc2, dsh2 = rowmap(
        "postmix_bwd", postmix_bwd, [xt, u, dx1, dh2], post_consts, [(d, F32), (d, BF16)], [((1, d), F32)] * 5)
    d_merged = matmul("w_out_dx", du, wb_out, tb=True)
    gw_out = matmul("w_out_dw", merged, du, ta=True)

    def merge_bwd(gs, gg, ys, yg, dm):
        _, vjp = jax.vjp(_merge, gs, gg, ys, yg)
        dgs, dgg, dys, dyg = vjp(dm)
        return dys, dyg, jnp.concatenate([dgs, dgg], axis=1)

    dy_ssm, dy_gdn, dproj = rowmap("merge_bwd", merge_bwd, gates + [y_ssm, y_gdn, d_merged], [],
                                   [(d, BF16), (d, BF16), (2 * d, BF16, jax.ShapeDtypeStruct((t, n_al), BF16), a_gs)])
    dy_ssm_n = matmul("ssm_up_dx", dy_ssm, wb_ssm_up, tb=True, out_dtypes=(BF16,))
    quarters_rows = lambda g: g.reshape(4, g.shape[0] // 4, g.shape[1])
    mlp_grads = [gw_up, quarters_rows(gw_down)]
    gw_ssm_up, mlp_from_sibling = matmul("ssm_up_dw", y_ssm_n, dy_ssm, ta=True, comm=exchange_halves_d2d(mlp_grads, [0, 0]))
    dy_gdn_n = matmul("gdn_up_dx", dy_gdn, wb_gdn_up, tb=True, out_dtypes=(BF16,))
    gw_gdn_up = matmul("gdn_up_dw", y_gdn_n, dy_gdn, ta=True)

    dproj, dsmall_ssm, dcw_ssm, dcb_ssm, d_sdtb, d_salog, d_sdsk, d_snw = scan_bwd(
        "ssd_bwd", ssd_step, SSM_CHUNK, 1, ssd_rows, ssd_consts, [st_ssm, tails_ssm], dy_ssm_n, [BF16, F32], 1, {0: dproj})
    dproj, dsmall_gdn, dcw_gdn, d_gdtb, d_galog, d_gnw = scan_bwd(
        "gdn_bwd", gdn_step, gdn_rows_per_step, 1, gdn_rows, gdn_consts, [st_gdn, tails_gdn], dy_gdn_n, [BF16, F32], 1,
        {0: dproj}, kept=inv_gdn)
    tail = n_al - a_small
    dproj = rowmap("small_sum", lambda a, b: jnp.concatenate([a + b, jnp.zeros((a.shape[0], tail - LANES), F32)], axis=1),
                   [dsmall_ssm, dsmall_gdn], [], [(tail, BF16, dproj, a_small)])[0]
    mix32, mix16 = reduce_on_chip("mix", [quarters_rows(gw_ssm_up), quarters_rows(gw_gdn_up), quarters_rows(gw_out)], [0] * 3)
    mlp32, mlp16 = reduce_on_chip("mlp", mlp_grads, [0, 0], mlp_from_sibling)
    rest32, rest16 = mix32 + mlp32, mix16 + mlp16
    gw_in_al, rest_chips = matmul("in_proj_dw", dproj, h1, ta=True, comm=exchange_quarters_ici(rest16))
    in32, in16 = reduce_on_chip("in", [quarters_rows(from_aligned(gw_in_al))], [1])
    dh1, in_chips = matmul("in_proj_dx", dproj, wb_in, comm=exchange_quarters_ici(in16))

    def premix_bwd(x_, dxa_, dh1_, w_, sc_, sh_):
        _, vjp = jax.vjp(_premix, x_, w_, sc_, sh_)
        dx, dw, dsc, dsh = vjp(dh1_)
        return dx + dxa_, dw, dsc, dsh

    grad_x, d_norm_mix_pre, dsc1, dsh1 = rowmap(
        "premix_bwd", premix_bwd, [xt, dxa, dh1], [norm_mix_pre, sc1, sh1], [(d, F32)], [((1, d), F32)] * 3)

    partial = [d_norm_mix_pre, d_norm_mix_post, dcw_ssm, dcb_ssm, d_sdtb[:, LANE_DT:LANE_DT + hs], d_salog[:, LANE_DT:LANE_DT + hs],
               d_sdsk[:, LANE_DT:LANE_DT + hs], d_snw, dcw_gdn, d_gdtb[:, LANE_A:LANE_A + hv], d_galog[:, LANE_A:LANE_A + hv], d_gnw,
               d_norm_mlp_pre, d_norm_mlp_post, loss_part]
    dmod = [dsh1, dsc1, dg1, dsh2, dsc2, dg2]
    gathered_small = gather_flat("ag_grads", jnp.concatenate([p.reshape(-1) for p in dmod + partial]))
    dmod_all, stacked = gathered_small[:, :4 * n_ada], gathered_small[:, 4 * n_ada:]

    dmod_q = lax.dynamic_slice_in_dim(dmod_all, quarter * n_ada, n_ada, axis=1)
    gw_ada, gb_ada = _whole(
        "ada_bwd", lambda ca, dq_, da_: (_bdot(_silu(ca), dq_, TN), jnp.sum(da_, axis=0, keepdims=True)),
        [c_all, dmod_q, dmod_all], [((d, n_ada), F32), ((1, 4 * n_ada), F32)])

    sizes = [p.size for p in partial]
    summed = _whole("small_sum8", lambda s: jnp.sum(s, axis=0, keepdims=True), [stacked], [((1, stacked.shape[1]), F32)])[0][0]
    offs = [0]
    for s in sizes:
        offs.append(offs[-1] + s)
    red = [summed[offs[i]:offs[i + 1]] for i in range(len(sizes))]
    loss = red[-1][0]
    my_cols = lambda full: lax.dynamic_slice_in_dim(full.reshape(CONV_K, -1), quarter * (n_cw // CONV_K), n_cw // CONV_K, axis=1)
    small_grads = {
        "b_ada": gb_ada, "norm_mix_pre": red[0], "norm_mix_post": red[1], "ssm_conv_w": my_cols(red[2]), "ssm_conv_b": red[3],
        "ssm_dt_bias": red[4], "ssm_A_log": red[5], "ssm_D": red[6], "ssm_norm_w": red[7], "gdn_conv_w": my_cols(red[8]),
        "gdn_dt_bias": red[9], "gdn_A_log": red[10], "gdn_norm_w": red[11], "norm_mlp_pre": red[12], "norm_mlp_post": red[13]}

    big_names = ["w_in", "w_ssm_up", "w_gdn_up", "w_out", "w_mlp_up", "w_mlp_down"]
    big_grads = dict(zip(big_names, reduce_across_chips(in32 + rest32, in_chips + rest_chips)))

    names = ['w_ada', 'b_ada', 'norm_mix_pre', 'norm_mix_post', 'w_in', 'ssm_conv_w', 'ssm_conv_b', 'ssm_dt_bias', 'ssm_A_log', 'ssm_D',
             'ssm_norm_w', 'gdn_conv_w', 'gdn_dt_bias', 'gdn_A_log', 'gdn_norm_w', 'w_ssm_up', 'w_gdn_up', 'w_out', 'norm_mlp_pre',
             'norm_mlp_post', 'w_mlp_up', 'w_mlp_down']
    grad, delta, new_m, new_v = {}, {}, {}, {}
    for n, (mine, other) in big_grads.items():
        view, axis = (transposed, 1) if n == "w_in" else ((lambda a: a), 0)
        res = adamw_halves("adamw_" + n, view(args[n]), mine, other, view(args["m_" + n]), view(args["v_" + n]), axis)
        grad[n], delta[n], new_m[n], new_v[n] = [view(a) for a in res]
    grad["w_ada"] = gw_ada.reshape(w_ada.shape)
    delta["w_ada"], new_m["w_ada"], new_v["w_ada"] = adamw("adamw_w_ada", w_ada, gw_ada, m_w_ada, v_w_ada)
    small_names = [n for n in names if n not in grad]
    flat = lambda pre: jnp.concatenate([args[pre + n].reshape(-1) for n in small_names]).reshape(1, 1, -1)
    g_flat = jnp.concatenate([small_grads[n].reshape(-1) for n in small_names]).reshape(1, -1)
    dl, nm, nv = adamw("adamw_small", flat(""), g_flat, flat("m_"), flat("v_"))
    off = 0
    for n in small_names:
        shape = args[n].shape
        size = args[n].size
        grad[n], delta[n], new_m[n], new_v[n] = [a.reshape(-1)[off:off + size].reshape(shape) for a in (g_flat, dl, nm, nv)]
        off += size

    return (loss, grad_x.reshape(x.shape), *[grad[n] for n in names], *[delta[n] for n in names],
            *[new_m[n] for n in names], *[new_v[n] for n in names])
```

```python
import functools

import jax
import jax.numpy as jnp
from jax import lax
from jax.experimental import pallas as pl
from jax.experimental.pallas import tpu as pltpu

F32 = jnp.float32
BF16 = jnp.bfloat16
MESH = pl.DeviceIdType.MESH

EPS = 1e-6
SSM_HEAD_DIM = 64
SSM_HEADS_PER_GROUP = 4
SSM_D_STATE = 128
SSM_CHUNK = 128
GDN_HEAD = 128
GDN_CHUNK = 64
CONV_K = 4
TAIL_ROWS = 8
LANE_DT, LANE_B, LANE_A = 0, 32, 48
ADAM_LR, ADAM_B1, ADAM_B2, ADAM_EPS, ADAM_WD, ADAM_STEP = 0.001, 0.9, 0.999, 1e-08, 0.01, 10

VMEM_LIMIT_BYTES = 56 * 1024 * 1024
LANES = 128
N_DEV = 8

NN = (((1,), (0,)), ((), ()))
NT = (((1,), (1,)), ((), ()))
TN = (((0,), (0,)), ((), ()))


BNN = (((2,), (1,)), ((0,), (0,)))
BNT = (((2,), (2,)), ((0,), (0,)))
BTN = (((1,), (1,)), ((0,), (0,)))
_KIND = {NN: ("NN", 0), NT: ("NT", 0), TN: ("TN", 0), BNN: ("NN", 1), BNT: ("NT", 1), BTN: ("TN", 1)}
_DIMS = {"NN": (NN, BNN), "NT": (NT, BNT), "TN": (TN, BTN)}


def _dg(a, b, dims):
    return lax.dot_general(a, b, dims, preferred_element_type=F32)


def _raw_bf16(a, b, dims):
    return _dg(a.astype(BF16), b.astype(BF16), dims)


def _raw_bf16x3(a, b, dims):
    ah, bh = a.astype(BF16), b.astype(BF16)
    al, bl = (a - ah.astype(F32)).astype(BF16), (b - bh.astype(F32)).astype(BF16)
    return _dg(ah, bh, dims) + (_dg(ah, bl, dims) + _dg(al, bh, dims))


def _make_dot(raw):
    @functools.partial(jax.custom_vjp, nondiff_argnums=(2,))
    def dot(a, b, dims):
        return raw(a, b, dims)

    def fwd(a, b, dims):
        return raw(a, b, dims), (a, b)

    def bwd(dims, res, ct):
        a, b = res
        kind, batched = _KIND[dims]
        d = lambda k: _DIMS[k][batched]
        if kind == "NN":
            da, db = raw(ct, b, d("NT")), raw(a, ct, d("TN"))
        elif kind == "NT":
            da, db = raw(ct, b, d("NN")), raw(ct, a, d("TN"))
        else:
            da, db = raw(b, ct, d("NT")), raw(a, ct, d("NN"))
        return da.astype(a.dtype), db.astype(b.dtype)

    dot.defvjp(fwd, bwd)
    return lambda a, b, dims=NN: dot(a, b, dims)


_bdot = _make_dot(_raw_bf16)
_hdot = _make_dot(_raw_bf16x3)


def _mask_dot(mask, x, dims):
    m = mask.astype(BF16)
    hi = x.astype(BF16)
    r = x - hi.astype(F32)
    mid = r.astype(BF16)
    lo = (r - mid.astype(F32)).astype(BF16)
    return sum(_dg(m, p, dims) for p in (hi, mid, lo))


def _sigmoid(x):
    return 0.5 * jnp.tanh(0.5 * x) + 0.5


def _silu(x):
    return x * _sigmoid(x)


def _softplus(x):
    return jnp.maximum(x, 0.0) + jnp.log(1.0 + jnp.exp(-jnp.abs(x)))


def _rms(x, w):
    return x * lax.rsqrt(jnp.mean(x * x, axis=-1, keepdims=True) + EPS) * w


def _lane_col(m, idx):
    lane = lax.broadcasted_iota(jnp.int32, m.shape, 1)
    return jnp.sum(jnp.where(lane == idx, m, 0.0), axis=1, keepdims=True)


def _tril(n, strict=False, seg=None):
    r = lax.broadcasted_iota(jnp.int32, (n, n), 0)
    c = lax.broadcasted_iota(jnp.int32, (n, n), 1)
    low = (r > c) if strict else (r >= c)
    if seg is None or seg >= n:
        return low
    shift = seg.bit_length() - 1
    return jnp.logical_and(low, (r >> shift) == (c >> shift))


@functools.partial(jax.custom_vjp, nondiff_argnums=(1,))
def _cumsum_rows(x, seg):
    return _mask_dot(_tril(x.shape[0], seg=seg), x, NN)


def _cumsum_rows_fwd(x, seg):
    return _cumsum_rows(x, seg), None


def _cumsum_rows_bwd(seg, _, ct):
    return (_mask_dot(_tril(ct.shape[0], seg=seg), ct, TN),)


_cumsum_rows.defvjp(_cumsum_rows_fwd, _cumsum_rows_bwd)


def _head_rows(m_t, idx):
    sub = lax.broadcasted_iota(jnp.int32, m_t.shape, 0)
    return jnp.sum(jnp.where(sub == idx, m_t, 0.0), axis=0, keepdims=True)


def _params(sem):
    return pltpu.CompilerParams(dimension_semantics=sem, vmem_limit_bytes=VMEM_LIMIT_BYTES)


def _into_plumbing(outs, first_input):
    arrays, aliases = [], {}
    for k, o in enumerate(outs):
        if len(o) > 4 and not isinstance(o[4], jax.ShapeDtypeStruct):
            aliases[first_input + len(arrays)] = k
            arrays.append(o[4])
    return arrays, aliases


def blockmap(name, fn, grid, ins, outs, accs=(), scalars=None):
    n_in, n_out, n_acc = len(ins), len(outs), len(accs)
    n_grid = len(grid)
    n_pre = 0 if scalars is None else 1
    into_arrays, aliases = _into_plumbing(outs, n_pre + n_in)
    n_into = len(into_arrays)

    def body(*refs):
        refs = refs[n_pre:n_pre + n_in] + refs[n_pre + n_in + n_into:]
        vals = fn(*[r[...] for r in refs[:n_in]])
        if not isinstance(vals, (tuple, list)):
            vals = (vals,)
        for r, v in zip(refs[n_in:n_in + n_out], vals[:n_out]):
            r[...] = v.astype(r.dtype)
        if n_acc:
            first = functools.reduce(jnp.logical_and, [pl.program_id(a) == 0 for a in range(n_grid)])
            acc_refs = refs[n_in + n_out:]

            @pl.when(first)
            def _():
                for r in acc_refs:
                    r[...] = jnp.zeros(r.shape, r.dtype)

            for r, v in zip(acc_refs, vals[n_out:]):
                r[...] += v.astype(r.dtype)

    zeros = lambda nd: (lambda *_: (0,) * nd)
    in_specs = [pl.BlockSpec(b, im) for _, b, im in ins] + [pl.BlockSpec(memory_space=pl.ANY)] * n_into
    out_specs = [pl.BlockSpec(o[2], o[3]) for o in outs] + [pl.BlockSpec(s, zeros(len(s))) for s, _ in accs]
    out_shape = [jax.ShapeDtypeStruct(o[0], o[1]) for o in outs] + [jax.ShapeDtypeStruct(s, d) for s, d in accs]
    cparams = _params(("arbitrary",) * n_grid if n_acc else ("parallel",) * n_grid)
    arrays = [a for a, _, _ in ins] + into_arrays
    if scalars is None:
        return pl.pallas_call(body, name=name, grid=grid, in_specs=in_specs, out_specs=out_specs, out_shape=out_shape,
                              input_output_aliases=aliases, compiler_params=cparams)(*arrays)
    spec = pltpu.PrefetchScalarGridSpec(num_scalar_prefetch=1, grid=grid, in_specs=in_specs, out_specs=out_specs)
    return pl.pallas_call(body, name=name, grid_spec=spec, out_shape=out_shape, input_output_aliases=aliases,
                          compiler_params=cparams)(scalars, *arrays)


def rowmap(name, fn, rows, consts, outs, accs=(), rb=512):
    norm = [(r, r.shape[1], 0) if not isinstance(r, tuple) else (r[0], r[1], r[2] // r[1]) for r in rows]
    assert all(not isinstance(r, tuple) or r[2] % r[1] == 0 for r in rows)
    t = norm[0][0].shape[0]
    rb = min(rb, t)
    ins = [(a, (rb, n), (lambda i, cb=cb: (i, cb))) for a, n, cb in norm]
    ins += [(cst, cst.shape, (lambda i, nd=cst.ndim: (0,) * nd)) for cst in consts]
    o = []
    for out in outs:
        if len(out) == 2:
            o.append(((t, out[0]), out[1], (rb, out[0]), lambda i: (i, 0)))
        else:
            n, d, into, off = out
            assert off % n == 0 and into.dtype == d
            o.append((into.shape, d, (rb, n), (lambda i, cb=off // n: (i, cb)), into))
    return blockmap(name, fn, (t // rb,), ins, o, accs)


MM_TILE_M, MM_TILE_N, MM_TILE_K = 1024, 1024, 2048


def _tile(dim, cap):
    if dim <= cap:
        return dim
    best = max(t for t in range(LANES, cap + 1, LANES) if dim % t == 0)
    return best


EPI_ROWS = 256


def matmul(name, a, b, ta=False, tb=False, out_dtypes=(F32,), epi=None, extras=(), comm=None, b_quarters=False, out_quarters=False,
           consts=(), accs=()):
    (k_dim, m_dim) = a.shape if ta else a.shape[::-1]
    if b_quarters:
        quarter = b.shape[2]
        b_rows, b_cols = b.shape[1], 4 * quarter
    else:
        b_rows, b_cols = b.shape
    n_dim = b_rows if tb else b_cols
    assert (b_cols if tb else b_rows) == k_dim, (name, a.shape, b.shape)
    tm, tn, tk = _tile(m_dim, MM_TILE_M), _tile(n_dim, MM_TILE_N), _tile(k_dim, MM_TILE_K)
    if b_quarters and tb:
        tk = quarter
    elif b_quarters or out_quarters:
        tn = quarter if b_quarters else n_dim // 4
    grid = (m_dim // tm, n_dim // tn, k_dim // tk)
    k_steps = grid[2]
    n_extra, n_out, n_const, n_acc = len(extras), len(out_dtypes), len(consts), len(accs)
    n_cin, n_cout = (len(comm.operands), len(comm.out_shapes)) if comm else (0, 0)
    dims = (((0 if ta else 1,), (1 if tb else 0,)), ((), ()))
    row_wise = bool(consts or accs)
    n_in = 2 + n_extra + n_const + n_cin

    def body(*refs):
        ins, outs, scratch = refs[:n_in], refs[n_in:][:n_out + n_acc + n_cout], refs[n_in + n_out + n_acc + n_cout:]
        extra_refs, const_refs = ins[2:2 + n_extra], ins[2 + n_extra:2 + n_extra + n_const]
        out_refs, acc_refs = outs[:n_out], outs[n_out:n_out + n_acc]
        ids = [pl.program_id(ax) for ax in range(3)]
        step = (ids[0] * grid[1] + ids[1]) * grid[2] + ids[2]
        if comm:
            comm_refs = (ins[2 + n_extra + n_const:], outs[n_out + n_acc:], scratch[-2], scratch[-1])

            @pl.when(step == 0)
            def _():
                comm.start(*comm_refs)

            @pl.when(step == (grid[0] * grid[1] * grid[2]) // 2)
            def _():
                comm.middle(*comm_refs)

        if n_acc:
            @pl.when(step == 0)
            def _():
                for r in acc_refs:
                    r[...] = jnp.zeros(r.shape, r.dtype)

        def finish(acc):
            pieces = [slice(r0, r0 + EPI_ROWS) for r0 in range(0, tm, EPI_ROWS)] if row_wise and tm % EPI_ROWS == 0 else [slice(None)]
            for rows in pieces:
                args = [acc[rows]] + [r[rows] for r in extra_refs] + [r[...] for r in const_refs]
                vals = (args[0],) if epi is None else epi(*args)
                if not isinstance(vals, (tuple, list)):
                    vals = (vals,)
                for r, v in zip(out_refs, vals[:n_out]):
                    r[rows] = v.astype(r.dtype)
                for r, v in zip(acc_refs, vals[n_out:]):
                    r[...] += v.astype(r.dtype)

        prod = lax.dot_general(ins[0][...].astype(BF16), ins[1][...].astype(BF16), dims, preferred_element_type=F32)
        if k_steps == 1:
            finish(prod)
        else:
            acc_ref = scratch[0]

            @pl.when(ids[2] == 0)
            def _():
                acc_ref[...] = jnp.zeros(acc_ref.shape, F32)

            acc_ref[...] += prod

            @pl.when(ids[2] == k_steps - 1)
            def _():
                finish(acc_ref)

        if comm:
            @pl.when(functools.reduce(jnp.logical_and, [i == g - 1 for i, g in zip(ids, grid)]))
            def _():
                comm.finish(*comm_refs)

    a_spec = pl.BlockSpec((tk, tm), lambda i, j, k: (k, i)) if ta else pl.BlockSpec((tm, tk), lambda i, j, k: (i, k))
    if b_quarters:
        b_spec = pl.BlockSpec((None, tn, tk), lambda i, j, k: (k, j, 0)) if tb else pl.BlockSpec((None, tk, tn), lambda i, j, k: (j, k, 0))
    else:
        b_spec = pl.BlockSpec((tn, tk), lambda i, j, k: (j, k)) if tb else pl.BlockSpec((tk, tn), lambda i, j, k: (k, j))
    mn_spec = pl.BlockSpec((tm, tn), lambda i, j, k: (i, j))
    out_spec = pl.BlockSpec((None, tm, tn), lambda i, j, k: (j, i, 0)) if out_quarters else mn_spec
    out_dims = (4, m_dim, tn) if out_quarters else (m_dim, n_dim)
    scratch_shapes = [] if k_steps == 1 else [pltpu.VMEM((tm, tn), F32)]
    if comm:
        scratch_shapes += [pltpu.SemaphoreType.DMA(comm.sem_shape), pltpu.SemaphoreType.DMA(comm.sem_shape)]
    whole = lambda shape: pl.BlockSpec(shape, lambda i, j, k, nd=len(shape): (0,) * nd)
    res = pl.pallas_call(
        body, name=name, grid=grid,
        in_specs=[a_spec, b_spec] + [mn_spec] * n_extra + [whole(c.shape) for c in consts] + [ANY] * n_cin,
        out_specs=[out_spec] * n_out + [whole(s) for s, _ in accs] + [ANY] * n_cout,
        out_shape=[jax.ShapeDtypeStruct(out_dims, d) for d in out_dtypes] + [jax.ShapeDtypeStruct(s, d) for s, d in accs]
        + (comm.out_shapes if comm else []),
        scratch_shapes=scratch_shapes,
        compiler_params=_params(("arbitrary",) * 3 if comm or accs else ("parallel", "parallel", "arbitrary")),
    )(a, b, *extras, *consts, *(comm.operands if comm else []))
    main = list(res[:n_out + n_acc]) if n_out + n_acc > 1 else res[0]
    return (main, list(res[n_out + n_acc:])) if comm else main


def ssd_step(g0, state, tail, zxbc, small, cw, cb, p_dtb, p_alog, p_dsk, nw):
    d_in, gn = state.shape[0] * 2 * LANES, state.shape[0] * LANES
    z = zxbc[:, :d_in]
    act, new_tail = _conv_silu_carried(tail, zxbc[:, d_in:], cw, cb)
    xs, bm, cm = act[:, :d_in], act[:, d_in:d_in + gn], act[:, d_in + gn:]
    hb, n = state.shape[0], xs.shape[0]
    n_pair, n_head = 2 * hb, 4 * hb
    causal = _tril(n)
    dt_all = _softplus(small + p_dtb)
    a_all = dt_all * (-jnp.exp(p_alog))
    acum_all = _cumsum_rows(a_all, n)
    acum_t = acum_all.T
    lane0 = LANE_DT + SSM_HEADS_PER_GROUP * g0
    sub = lax.broadcasted_iota(jnp.int32, acum_t.shape, 0)
    heads = range(n_head)
    acum = jnp.stack([_lane_col(acum_all, lane0 + i) for i in heads])
    acum_row = jnp.stack([jnp.sum(jnp.where(sub == lane0 + i, acum_t, 0.0), axis=0, keepdims=True) for i in heads])
    dt = jnp.stack([_lane_col(dt_all, lane0 + i) for i in heads])
    dsk = jnp.stack([_lane_col(p_dsk, lane0 + i) for i in heads])
    decay = jnp.exp(jnp.where(causal, acum - acum_row, -jnp.inf))
    a_last = acum[:, n - 1:n, :]

    def split(a):
        return [a[:, i * LANES:(i + 1) * LANES] for i in range(a.shape[1] // LANES)]

    def pairs(a, axis=2):
        even = jnp.stack([a[2 * p] for p in range(n_pair)])
        odd = jnp.stack([a[2 * p + 1] for p in range(n_pair)])
        shape = (n_pair, LANES, LANES) if axis == 1 else (n_pair, a.shape[1], LANES)
        return jnp.where(lax.broadcasted_iota(jnp.int32, shape, axis) < SSM_HEAD_DIM, even, odd)

    bms, cms = split(bm), split(cm)
    cb = _bdot(jnp.stack(cms), jnp.stack(bms), BNT)
    cbd = jnp.stack([cb[i // SSM_HEADS_PER_GROUP] for i in heads]) * decay
    xp = jnp.stack(split(xs))
    xdt = xp * pairs(dt)
    yd = _bdot(cbd, jnp.stack([xdt[i // 2] for i in heads]), BNN)
    lane = lax.broadcasted_iota(jnp.int32, (n_pair, n, LANES), 2)
    y_diag = jnp.where(lane < SSM_HEAD_DIM, jnp.stack([yd[2 * p] for p in range(n_pair)]), jnp.stack([yd[2 * p + 1] for p in range(n_pair)]))
    st = state.reshape(n_pair, LANES, LANES)
    cm2 = jnp.stack([cms[p // 2] for p in range(n_pair)])
    bm2 = jnp.stack([bms[p // 2] for p in range(n_pair)])
    y_off = _bdot(cm2, st, BNT) * pairs(jnp.exp(acum))
    new = st * pairs(jnp.exp(a_last), axis=1) + _bdot(xdt * pairs(jnp.exp(a_last - acum)), bm2, BTN)
    y = y_diag + y_off + pairs(dsk) * xp
    y = jnp.concatenate([y[p] for p in range(n_pair)], axis=1) * _silu(z)
    wide = 2 * LANES
    y = jnp.concatenate([_rms(y[:, i * wide:(i + 1) * wide], nw[i]) for i in range(hb)], axis=1)
    return new.reshape(state.shape), new_tail, y


@functools.partial(jax.custom_vjp, nondiff_argnums=(1,))
def _unit_lower_inverse(a, seg):
    n = a.shape[-1]
    r = lax.broadcasted_iota(jnp.int32, (n, n), 0)
    c = lax.broadcasted_iota(jnp.int32, (n, n), 1)
    shift = min(INVERSE_BASE, seg).bit_length() - 1
    power = jnp.where((r >> shift) == (c >> shift), a, 0.0)
    inv = (r == c).astype(F32) - power
    span = 2
    while span < (1 << shift):
        power = _hdot(power, power, BNN)
        inv = inv + _hdot(inv, power, BNN)
        span *= 2
    while (1 << shift) < seg:
        below = jnp.logical_and((r >> (shift + 1)) == (c >> (shift + 1)), (r >> shift) != (c >> shift))
        inv = inv - _hdot(inv, _hdot(jnp.where(below, a, 0.0), inv, BNN), BNN)
        shift += 1
    return inv


def _unit_lower_inverse_fwd(a, seg):
    inv = _unit_lower_inverse(a, seg)
    return inv, inv


def _unit_lower_inverse_bwd(seg, inv, ct):
    return (-_hdot(_hdot(inv, ct, BTN), inv, BNT),)


_unit_lower_inverse.defvjp(_unit_lower_inverse_fwd, _unit_lower_inverse_bwd)


@jax.custom_vjp
def _known_inverse(a, inv):
    return inv


def _known_inverse_fwd(a, inv):
    return inv, inv


def _known_inverse_bwd(inv, ct):
    return _unit_lower_inverse_bwd(None, inv, ct)[0], jnp.zeros_like(inv)


_known_inverse.defvjp(_known_inverse_fwd, _known_inverse_bwd)


@functools.partial(jax.custom_vjp, nondiff_argnums=(1,))
def _rotate_rows(x, k):
    return x if k == 0 else pltpu.roll(x, k % x.shape[0], 0)


def _rotate_rows_fwd(x, k):
    return _rotate_rows(x, k), None


def _rotate_rows_bwd(k, _, ct):
    return (_rotate_rows(ct, -k),)


_rotate_rows.defvjp(_rotate_rows_fwd, _rotate_rows_bwd)


def _conv_silu_carried(tail, x, cw, cb=0.0):
    n = x.shape[0]
    ext = jnp.concatenate([tail, x], axis=0)
    pre = cb + sum(cw[j:j + 1, :] * _rotate_rows(ext, CONV_K - 1 - j)[TAIL_ROWS:] for j in range(CONV_K))
    return _silu(pre), x[n - TAIL_ROWS:]


def _l2norm(x):
    return x * lax.rsqrt(jnp.sum(x * x, axis=-1, keepdims=True) + EPS)


def gdn_step(hq0, state, tail, qkvz, small, cw, p_dtb, p_alog, nw, keep=False, kept=None):
    n, chunk = qkvz.shape[0], GDN_CHUNK
    hb = state.shape[0]
    nb = 2 * hb
    cur = state.reshape(nb, LANES, LANES)
    conv_cols = 4 * hb * LANES
    act, new_tail = _conv_silu_carried(tail, qkvz[:, :conv_cols], cw)
    q, k, v = act[:, :hb * LANES], act[:, hb * LANES:2 * hb * LANES], act[:, 2 * hb * LANES:]
    z = qkvz[:, conv_cols:]
    causal, strict = _tril(n, seg=chunk), _tril(n, True, seg=chunk)
    beta_all = _sigmoid(small)
    g_all = -jnp.exp(p_alog) * _softplus(small + p_dtb)
    gcum_all = _cumsum_rows(g_all, chunk)
    gcum_t = gcum_all.T
    split = lambda a: [a[:, i * LANES:(i + 1) * LANES] for i in range(a.shape[1] // LANES)]
    per_value_head = lambda a: jnp.stack([a[i // 2] for i in range(nb)])
    qh, kh = _l2norm(jnp.stack(split(q))) * (GDN_HEAD ** -0.5), _l2norm(jnp.stack(split(k)))
    q2, k2 = per_value_head(qh), per_value_head(kh)
    v2, z2 = jnp.stack(split(v)), jnp.stack(split(z))
    gcum = jnp.stack([_lane_col(gcum_all, LANE_A + 2 * hq0 + i) for i in range(nb)])
    gcum_row = jnp.stack([_head_rows(gcum_t, LANE_A + 2 * hq0 + i) for i in range(nb)])
    beta = jnp.stack([_lane_col(beta_all, LANE_B + 2 * hq0 + i) for i in range(nb)])
    dmat = jnp.exp(jnp.where(causal, gcum - gcum_row, -jnp.inf))
    a_low = jnp.where(strict, beta * per_value_head(_bdot(kh, kh, BNT)) * dmat, 0.0)
    inv = _unit_lower_inverse(a_low, chunk) if kept is None else _known_inverse(a_low, kept)
    egc = jnp.exp(gcum)
    u = _hdot(inv, v2 * beta, BNN)
    w = _hdot(inv, k2 * (beta * egc), BNN)
    q_dec = q2 * egc
    v_new, o_state = [], []
    for s in range(n // chunk):
        rows = slice(s * chunk, (s + 1) * chunk)
        v_new.append(u[:, rows] - _bdot(w[:, rows], cur, BNN))
        o_state.append(_bdot(q_dec[:, rows], cur, BNN))
        g_last = gcum[:, (s + 1) * chunk - 1:(s + 1) * chunk, :]
        k_dec = k2[:, rows] * jnp.exp(g_last - gcum[:, rows])
        cur = cur * jnp.exp(g_last) + _bdot(k_dec, v_new[-1], BTN)
    o = jnp.concatenate(o_state, axis=1) + _bdot(per_value_head(_bdot(qh, kh, BNT)) * dmat, jnp.concatenate(v_new, axis=1), BNN)
    out = _rms(o, nw) * _silu(z2)
    res = (cur.reshape(state.shape), new_tail, jnp.concatenate([out[i] for i in range(nb)], axis=1))
    return res + (inv,) if keep else res


STATE_SHAPE = (2, LANES, LANES)
GDN_CHUNKS_PER_STEP = 2
INVERSE_BASE = 16


def _scan_specs(rows, consts, chunk, chunk_of, hb):
    specs = []
    for _, n, off, per_group in rows:
        if per_group:
            assert off % (n * hb) == 0
            specs.append(pl.BlockSpec((chunk, n * hb), lambda c, g, cb=off // (n * hb): (chunk_of(c), cb + g)))
        else:
            assert off % n == 0
            specs.append(pl.BlockSpec((chunk, n), lambda c, g, cb=off // n: (chunk_of(c), cb)))
    for arr, per_group in consts:
        if per_group:
            specs.append(pl.BlockSpec((hb, 1, arr.shape[2]), lambda c, g: (g, 0, 0)))
        else:
            specs.append(pl.BlockSpec(arr.shape, lambda c, g, nd=arr.ndim: (0,) * nd))
    return specs


def scan_fwd(name, step, chunk, n_grp, rows, consts, out_cols, hb, keep=None, state_shape=None, tail_cols=None):
    t = rows[0][0].shape[0]
    nc = t // chunk
    n_rows, n_consts = len(rows), len(consts)
    state_shape = state_shape or (hb,) + STATE_SHAPE
    carried = [state_shape] + ([(TAIL_ROWS, tail_cols)] if tail_cols else [])
    n_car = len(carried)

    def body(*refs):
        row_refs, const_refs = refs[:n_rows], refs[n_rows:n_rows + n_consts]
        y_ref = refs[n_rows + n_consts]
        saved_refs = refs[n_rows + n_consts + 1:n_rows + n_consts + 1 + n_car]
        scratch = refs[-n_car:]
        c, g = pl.program_id(0), pl.program_id(1)

        @pl.when(c == 0)
        def _():
            for s, shape in zip(scratch, carried):
                s[g] = jnp.zeros(shape, F32)

        cur = [s[g] for s in scratch]
        for r, v in zip(saved_refs, cur):
            r[...] = v
        vals = [r[...] for r in row_refs] + [r[...] for r in const_refs]
        res = step(g * hb, *cur, *vals) if keep is None else step(g * hb, *cur, *vals, keep=True)
        for s, v in zip(scratch, res[:n_car]):
            s[g] = v
        y_ref[...] = res[n_car].astype(y_ref.dtype)
        if keep is not None:
            refs[n_rows + n_consts + 1 + n_car][...] = res[n_car + 1]

    lead = (nc, n_grp // hb)
    out_specs = [pl.BlockSpec((chunk, out_cols * hb), lambda c, g: (c, g))]
    out_shape = [jax.ShapeDtypeStruct((t, n_grp * out_cols), BF16)]
    for shape in carried + ([keep] if keep is not None else []):
        out_specs.append(pl.BlockSpec((None, None) + shape, lambda c, g, nd=len(shape): (c, g) + (0,) * nd))
        out_shape.append(jax.ShapeDtypeStruct(lead + shape, F32))
    return pl.pallas_call(
        body, name=name, grid=lead,
        in_specs=_scan_specs(rows, consts, chunk, lambda c: c, hb),
        out_specs=out_specs, out_shape=out_shape,
        scratch_shapes=[pltpu.VMEM((n_grp // hb,) + shape, F32) for shape in carried],
        compiler_params=_params(("arbitrary", "arbitrary")),
    )(*[r[0] for r in rows], *[c[0] for c in consts])


def scan_bwd(name, step, chunk, n_grp, rows, consts, saved, dy, row_dtypes, hb, into, kept=None):
    t = rows[0][0].shape[0]
    nc = t // chunk
    n_rows, n_consts, n_car = len(rows), len(consts), len(saved)
    carried = [s.shape[2:] for s in saved]
    out_cols = dy.shape[1] // n_grp
    n_alias = sum(not isinstance(v, jax.ShapeDtypeStruct) for v in into.values())
    n_kept = 0 if kept is None else 1
    n_in = n_rows + n_consts + n_car + 1 + n_kept + n_alias

    def body(*refs):
        row_refs, const_refs = refs[:n_rows], refs[n_rows:n_rows + n_consts]
        saved_refs = refs[n_rows + n_consts:n_rows + n_consts + n_car]
        dy_ref = refs[n_rows + n_consts + n_car]
        outs = refs[n_in:-n_car]
        scratch = refs[-n_car:]
        c, g = pl.program_id(0), pl.program_id(1)

        @pl.when(c == 0)
        def _():
            for s, shape in zip(scratch, carried):
                s[g] = jnp.zeros(shape, F32)

        @pl.when(jnp.logical_and(c == 0, g == 0))
        def _():
            for r in outs[n_rows:]:
                r[...] = jnp.zeros(r.shape, r.dtype)

        f = functools.partial(step, g * hb) if kept is None else functools.partial(step, g * hb, kept=refs[n_rows + n_consts + n_car + 1][...])
        _, vjp = jax.vjp(f, *[r[...] for r in saved_refs], *[r[...] for r in row_refs], *[r[...] for r in const_refs])
        grads = vjp(tuple(s[g] for s in scratch) + (dy_ref[...].astype(F32),))
        for s, d in zip(scratch, grads[:n_car]):
            s[g] = d
        for (_, _, _, per_group), r, d in zip(rows, outs[:n_rows], grads[n_car:n_car + n_rows]):
            if per_group:
                r[...] = d.astype(r.dtype)
            else:
                @pl.when(g == 0)
                def _(r=r):
                    r[...] = jnp.zeros(r.shape, r.dtype)

                r[...] += d.astype(r.dtype)
        for (_, per_group), r, d in zip(consts, outs[n_rows:], grads[n_car + n_rows:]):
            if per_group:
                r[pl.ds(g * hb, hb)] += d
            else:
                r[...] += d

    rev = lambda c: nc - 1 - c
    out_specs, out_shape = [], []
    into_arrays, aliases = [], {}
    first_into = n_in - n_alias
    kept_arrays = [] if kept is None else [kept]
    by_step = lambda a: pl.BlockSpec((None, None) + a.shape[2:], lambda c, g, nd=a.ndim - 2: (rev(c), g) + (0,) * nd)
    for k, ((_, n, off, per_group), dt) in enumerate(zip(rows, row_dtypes)):
        if k in into:
            assert per_group and off % (n * hb) == 0 and into[k].dtype == dt
            out_specs.append(pl.BlockSpec((chunk, n * hb), lambda c, g, cb=off // (n * hb): (rev(c), cb + g)))
            out_shape.append(jax.ShapeDtypeStruct(into[k].shape, dt))
            if not isinstance(into[k], jax.ShapeDtypeStruct):
                aliases[first_into + len(into_arrays)] = k
                into_arrays.append(into[k])
        elif per_group:
            out_specs.append(pl.BlockSpec((chunk, n * hb), lambda c, g: (rev(c), g)))
            out_shape.append(jax.ShapeDtypeStruct((t, n_grp * n), dt))
        else:
            out_specs.append(pl.BlockSpec((chunk, n), lambda c, g: (rev(c), 0)))
            out_shape.append(jax.ShapeDtypeStruct((t, n), dt))
    for arr, _ in consts:
        out_specs.append(pl.BlockSpec(arr.shape, lambda c, g, nd=arr.ndim: (0,) * nd))
        out_shape.append(jax.ShapeDtypeStruct(arr.shape, F32))
    return pl.pallas_call(
        body, name=name, grid=(nc, n_grp // hb),
        in_specs=_scan_specs(rows, consts, chunk, rev, hb) + [by_step(s) for s in saved]
        + [pl.BlockSpec((chunk, out_cols * hb), lambda c, g: (rev(c), g))] + [by_step(k) for k in kept_arrays]
        + [pl.BlockSpec(memory_space=pl.ANY)] * len(into_arrays),
        out_specs=out_specs, out_shape=out_shape, input_output_aliases=aliases,
        scratch_shapes=[pltpu.VMEM((n_grp // hb,) + shape, F32) for shape in carried],
        compiler_params=_params(("arbitrary", "arbitrary")),
    )(*[r[0] for r in rows], *[c[0] for c in consts], *saved, dy, *kept_arrays, *into_arrays)


def _place():
    return lax.axis_index("x"), lax.axis_index("y"), lax.axis_index("c")


def _other_chips(x, y):
    return [(1 - x, y), (x, 1 - y), (1 - x, 1 - y)]


ANY = pl.BlockSpec(memory_space=pl.ANY)


def all_gather8(name, v):
    m_per, n = v.shape

    def body(x_ref, out_ref, send_sems, recv_sems, local_sem):
        x, y, c = _place()
        me, sibling = (x, y, c), (x, y, 1 - c)
        chips = _other_chips(x, y)

        def rows(px, py, pc):
            return out_ref.at[pl.ds((4 * px + 2 * py + pc) * m_per, m_per), :]

        def copy(k, block, to, src=None):
            return pltpu.make_async_remote_copy(
                src_ref=rows(*block) if src is None else src, dst_ref=rows(*block),
                send_sem=send_sems.at[k], recv_sem=recv_sems.at[k], device_id=to, device_id_type=MESH)

        mine = pltpu.make_async_copy(x_ref, rows(*me), local_sem)
        mine.start()
        first = [copy(0, me, sibling, src=x_ref)]
        first += [copy(1 + q, me, (*chip, c), src=x_ref) for q, chip in enumerate(chips)]
        for cp in first:
            cp.start()
        passed = [copy(4 + q, (*chip, c), sibling) for q, chip in enumerate(chips)]
        for q, chip in enumerate(chips):
            copy(1 + q, (*chip, c), me).wait_recv()
            passed[q].start()
        copy(0, sibling, me).wait_recv()
        for q, chip in enumerate(chips):
            copy(4 + q, (*chip, 1 - c), me).wait_recv()
        for cp in first + passed:
            cp.wait_send()
        mine.wait()

    return pl.pallas_call(
        body, name=name, out_shape=jax.ShapeDtypeStruct((N_DEV * m_per, n), v.dtype),
        in_specs=[pl.BlockSpec(memory_space=pltpu.VMEM)], out_specs=pl.BlockSpec(memory_space=pltpu.VMEM),
        scratch_shapes=[pltpu.SemaphoreType.DMA((7,)), pltpu.SemaphoreType.DMA((7,)), pltpu.SemaphoreType.DMA],
    )(v)


def gather_flat(name, vec):
    n = vec.shape[0]
    n_pad = -(-n // (8 * LANES)) * (8 * LANES)
    v = jnp.pad(vec, (0, n_pad - n)).reshape(8, n_pad // 8)
    return all_gather8(name, v).reshape(N_DEV, n_pad)[:, :n]


class Exchange:
    def __init__(self, operands, out_shapes, sem_shape, start, finish, middle=None):
        self.operands, self.out_shapes, self.sem_shape = list(operands), out_shapes, sem_shape
        self.start, self.middle, self.finish = start, middle or (lambda *refs: None), finish


def _start_all_wait_all(make_copies):
    def start(*refs):
        for cp in make_copies(*refs):
            cp.start()

    def finish(*refs):
        for cp in make_copies(*refs):
            cp.wait()

    return start, finish


def run_exchange(name, ex):
    n_in, n_out = len(ex.operands), len(ex.out_shapes)

    def body(*refs):
        ins, outs = refs[:n_in], refs[n_in:n_in + n_out]
        ex.start(ins, outs, *refs[n_in + n_out:])
        ex.middle(ins, outs, *refs[n_in + n_out:])
        ex.finish(ins, outs, *refs[n_in + n_out:])

    return pl.pallas_call(
        body, name=name, out_shape=ex.out_shapes, in_specs=[ANY] * n_in, out_specs=[ANY] * n_out,
        scratch_shapes=[pltpu.SemaphoreType.DMA(ex.sem_shape), pltpu.SemaphoreType.DMA(ex.sem_shape)],
    )(*ex.operands)


def _half(shape, axis, pc):
    h = shape[axis] // 2
    return (pl.ds(pc * h, h), slice(None)) if axis == 0 else (slice(None), pl.ds(pc * h, h))


def _half_shape(shape, axis):
    return tuple(s // 2 if a == axis else s for a, s in enumerate(shape))


def all_gather_shards(shards, axes):
    n_t = len(shards)
    n_sem = 12

    def copies(ins, outs, send_sems, recv_sems):
        x, y, c = _place()
        me, sibling, x_nbr, y_nbr = (x, y, c), (x, y, 1 - c), (1 - x, y, c), (x, 1 - y, c)
        own, of_x, of_y, of_diag = 2 * x + y, 2 * (1 - x) + y, 2 * x + 1 - y, 2 * (1 - x) + 1 - y

        def copy(t, k, quarter, pc, piece, to, from_input=False):
            axis = axes[t]
            h = ins[t].shape[axis] // 4
            cut = pl.ds((2 * pc + piece) * h, h)
            part = (cut, slice(None)) if axis == 0 else (slice(None), cut)
            dst = outs[t].at[(quarter,) + part]
            return pltpu.make_async_remote_copy(
                src_ref=ins[t].at[part] if from_input else dst, dst_ref=dst,
                send_sem=send_sems.at[t, k], recv_sem=recv_sems.at[t, k], device_id=to, device_id_type=MESH)

        stages = []
        for t in range(n_t):
            direct = [copy(t, 0, own, c, 0, x_nbr, True), copy(t, 2, own, c, 1, y_nbr, True),
                      copy(t, 1, own, c, 1, x_nbr, True), copy(t, 3, own, c, 0, y_nbr, True)]
            landing = [
                (copy(t, 0, of_x, c, 0, me), [copy(t, 4, of_x, c, 0, y_nbr), copy(t, 6, of_x, c, 0, sibling)]),
                (copy(t, 2, of_y, c, 1, me), [copy(t, 5, of_y, c, 1, x_nbr), copy(t, 8, of_y, c, 1, sibling)]),
                (copy(t, 1, of_x, c, 1, me), [copy(t, 7, of_x, c, 1, sibling)]),
                (copy(t, 3, of_y, c, 0, me), [copy(t, 9, of_y, c, 0, sibling)]),
                (copy(t, 4, of_diag, c, 0, me), [copy(t, 10, of_diag, c, 0, sibling)]),
                (copy(t, 5, of_diag, c, 1, me), [copy(t, 11, of_diag, c, 1, sibling)])]
            from_sibling = [copy(t, 6, of_x, 1 - c, 0, me), copy(t, 8, of_y, 1 - c, 1, me), copy(t, 7, of_x, 1 - c, 1, me),
                            copy(t, 9, of_y, 1 - c, 0, me), copy(t, 10, of_diag, 1 - c, 0, me), copy(t, 11, of_diag, 1 - c, 1, me)]
            stages.append((direct, landing, from_sibling))
        return stages

    def start(*refs):
        for direct, _, _ in copies(*refs):
            for cp in direct:
                cp.start()

    def pass_on(landing):
        for arrived, onward in landing:
            arrived.wait_recv()
            for cp in onward:
                cp.start()

    def middle(*refs):
        for _, landing, _ in copies(*refs):
            pass_on(landing[:4])

    def finish(*refs):
        stages = copies(*refs)
        for _, landing, _ in stages:
            pass_on(landing[4:])
        for direct, landing, from_sibling in stages:
            for cp in from_sibling:
                cp.wait_recv()
            for cp in direct + [cp for _, onward in landing for cp in onward]:
                cp.wait_send()

    return Exchange(shards, [jax.ShapeDtypeStruct((4,) + s.shape, s.dtype) for s in shards], (n_t, n_sem), start, finish, middle)


def exchange_halves_d2d(grads, axes):
    n_t = len(grads)

    def copies(ins, outs, send_sems, recv_sems):
        x, y, c = _place()
        return [pltpu.make_async_remote_copy(
            src_ref=ins[t].at[(slice(None),) + _half(ins[t].shape[1:], axes[t], 1 - c)], dst_ref=outs[t],
            send_sem=send_sems.at[t], recv_sem=recv_sems.at[t], device_id=(x, y, 1 - c), device_id_type=MESH) for t in range(n_t)]

    shapes = [jax.ShapeDtypeStruct((4,) + _half_shape(g.shape[1:], a), g.dtype) for g, a in zip(grads, axes)]
    return Exchange(grads, shapes, (n_t,), *_start_all_wait_all(copies))


def exchange_quarters_ici(parts):
    n_t = len(parts)

    def copies(ins, outs, send_sems, recv_sems):
        x, y, c = _place()
        return [pltpu.make_async_remote_copy(
            src_ref=ins[t].at[2 * px + py], dst_ref=outs[t].at[q],
            send_sem=send_sems.at[t, q], recv_sem=recv_sems.at[t, q], device_id=(px, py, c), device_id_type=MESH)
            for t in range(n_t) for q, (px, py) in enumerate(_other_chips(x, y))]

    shapes = [jax.ShapeDtypeStruct((3,) + p.shape[1:], p.dtype) for p in parts]
    return Exchange(parts, shapes, (n_t, 3), *_start_all_wait_all(copies))


def swap_d2d(halves):
    n_t = len(halves)

    def copies(ins, outs, send_sems, recv_sems):
        x, y, c = _place()
        return [pltpu.make_async_remote_copy(
            src_ref=ins[t], dst_ref=outs[t], send_sem=send_sems.at[t], recv_sem=recv_sems.at[t],
            device_id=(x, y, 1 - c), device_id_type=MESH) for t in range(n_t)]

    return Exchange(halves, [jax.ShapeDtypeStruct(h.shape, h.dtype) for h in halves], (n_t,), *_start_all_wait_all(copies))


BLOCK_BYTES = 1 << 20


def _row_block(r, c):
    fits = [rb for rb in range(16, r + 1, 16) if r % rb == 0 and rb * c * 4 <= BLOCK_BYTES]
    return max(fits) if fits else r


def _place_scalars():
    x, y, c = _place()
    return jnp.stack([c, 2 * x + y]).astype(jnp.int32)


def reduce_on_chip(tag, grads, axes, from_sibling=None):
    if from_sibling is None:
        from_sibling = run_exchange(f"rs_d2d_{tag}", exchange_halves_d2d(grads, axes))
    parts, parts_bf16 = [], []
    for t, (g, s, axis) in enumerate(zip(grads, from_sibling, axes)):
        _, h, cols = s.shape
        rb = _row_block(h, cols)
        nb = h // rb
        blk = lambda k, i, s_ref: (k, i, 0)
        mine = (lambda k, i, s_ref, nb=nb: (k, s_ref[0] * nb + i, 0)) if axis == 0 else (lambda k, i, s_ref: (k, i, s_ref[0]))
        p32, p16 = blockmap(
            f"rs_add_{tag}{t}", lambda a, b: (a + b, a + b), (4, nb),
            [(g, (None, rb, cols), mine), (s, (None, rb, cols), blk)],
            [(s.shape, F32, (None, rb, cols), blk), (s.shape, BF16, (None, rb, cols), blk)], scalars=_place_scalars())
        parts.append(p32)
        parts_bf16.append(p16)
    return parts, parts_bf16


def reduce_across_chips(parts, from_chips):
    halves = []
    for t, (p, q) in enumerate(zip(parts, from_chips)):
        _, h, cols = p.shape
        rb = _row_block(h, cols)
        halves.append(blockmap(
            f"rs_sum{t}", lambda a, b: a + b[0].astype(F32) + b[1].astype(F32) + b[2].astype(F32), (h // rb,),
            [(p, (None, rb, cols), lambda i, s_ref: (s_ref[1], i, 0)), (q, (3, rb, cols), lambda i, s_ref: (0, i, 0))],
            [((h, cols), F32, (rb, cols), lambda i, s_ref: (i, 0))], scalars=_place_scalars())[0])
    return list(zip(halves, run_exchange("rs_swap", swap_d2d(halves))))


def _adamw(w, g, m, v):
    m = ADAM_B1 * m + (1.0 - ADAM_B1) * g
    v = ADAM_B2 * v + (1.0 - ADAM_B2) * jnp.square(g)
    m_hat = m / (1.0 - ADAM_B1 ** ADAM_STEP)
    v_hat = v / (1.0 - ADAM_B2 ** ADAM_STEP)
    delta = -ADAM_LR * (m_hat / (jnp.sqrt(v_hat) + ADAM_EPS) + ADAM_WD * w)
    return delta, m, v


def adamw(name, w, g, m, v):
    _, r, c = w.shape
    rb = _row_block(r, c)
    blk3 = lambda a: (a, (None, rb, c), lambda i: (0, i, 0))
    return blockmap(name, _adamw, (r // rb,), [blk3(w), (g, (rb, c), lambda i: (i, 0)), blk3(m), blk3(v)],
                    [(w.shape, F32, (None, rb, c), lambda i: (0, i, 0))] * 3)


def adamw_halves(name, w, mine, other, m, v, axis):
    _, r, c = w.shape
    h, c = mine.shape
    rb = _row_block(h, c)
    nb = h // rb

    def body(s_ref, w_ref, mine_ref, other_ref, m_ref, v_ref, g_out, d_out, m_out, v_out):
        g = jnp.where(pl.program_id(0) == s_ref[0], mine_ref[...], other_ref[...])
        d, nm, nv = _adamw(w_ref[...], g, m_ref[...], v_ref[...])
        g_out[...], d_out[...], m_out[...], v_out[...] = g, d, nm, nv

    spec3 = pl.BlockSpec((None, rb, c), (lambda k, i, s_ref: (0, k * nb + i, 0)) if axis == 0 else (lambda k, i, s_ref: (0, i, k)))
    spec2 = pl.BlockSpec((rb, c), lambda k, i, s_ref: (i, 0))
    grid_spec = pltpu.PrefetchScalarGridSpec(num_scalar_prefetch=1, grid=(2, nb), in_specs=[spec3, spec2, spec2, spec3, spec3],
                                             out_specs=[spec3] * 4)
    return pl.pallas_call(body, name=name, grid_spec=grid_spec, out_shape=[jax.ShapeDtypeStruct(w.shape, F32)] * 4,
                          compiler_params=_params(("parallel", "parallel")))(_place_scalars(), w, mine, other, m, v)


def _whole(name, fn, ins, outs):
    return blockmap(name, fn, (1,), [(a, a.shape, lambda i, nd=a.ndim: (0,) * nd) for a in ins],
                    [(s, d, s, lambda i, nd=len(s): (0,) * nd) for s, d in outs])


def _premix(x, w, sc, sh):
    return _rms(x, w) * (1.0 + sc) + sh


def _postmix(x, u, w_post, g1, w_pre2, sc2, sh2):
    x1 = x + g1 * _rms(u, w_post)
    return x1, _premix(x1, w_pre2, sc2, sh2)


def _merge(gs, gg, ys, yg):
    return _sigmoid(gs) * ys + _sigmoid(gg) * yg


def _final(x1, y2, w_post2, g2):
    return x1 + g2 * _rms(y2, w_post2)


def kernel(x, c, w_ada, b_ada, norm_mix_pre, norm_mix_post, w_in, ssm_conv_w, ssm_conv_b, ssm_dt_bias, ssm_A_log, ssm_D, ssm_norm_w, gdn_conv_w, gdn_dt_bias, gdn_A_log, gdn_norm_w, w_ssm_up, w_gdn_up, w_out, norm_mlp_pre, norm_mlp_post, w_mlp_up, w_mlp_down, loss_target, m_w_ada, m_b_ada, m_norm_mix_pre, m_norm_mix_post, m_w_in, m_ssm_conv_w, m_ssm_conv_b, m_ssm_dt_bias, m_ssm_A_log, m_ssm_D, m_ssm_norm_w, m_gdn_conv_w, m_gdn_dt_bias, m_gdn_A_log, m_gdn_norm_w, m_w_ssm_up, m_w_gdn_up, m_w_out, m_norm_mlp_pre, m_norm_mlp_post, m_w_mlp_up, m_w_mlp_down, v_w_ada, v_b_ada, v_norm_mix_pre, v_norm_mix_post, v_w_in, v_ssm_conv_w, v_ssm_conv_b, v_ssm_dt_bias, v_ssm_A_log, v_ssm_D, v_ssm_norm_w, v_gdn_conv_w, v_gdn_dt_bias, v_gdn_A_log, v_gdn_norm_w, v_w_ssm_up, v_w_gdn_up, v_w_out, v_norm_mlp_pre, v_norm_mlp_post, v_w_mlp_up, v_w_mlp_down):
    args = dict(locals())
    xi, yi, ci = _place()
    quarter = 2 * xi + yi
    batch = 4 * xi + 2 * yi + ci

    xt, target = x[0], loss_target[0]
    t, d = xt.shape
    hs, hv = ssm_dt_bias.shape[-1], gdn_dt_bias.shape[-1]
    assert hs <= LANE_B - LANE_DT and hv <= LANE_A - LANE_B and hv % 2 == 0 and hs % SSM_HEADS_PER_GROUP == 0
    assert t % SSM_CHUNK == 0 and t % (GDN_CHUNK * GDN_CHUNKS_PER_STEP) == 0 and d % LANES == 0
    d_inner = hs * SSM_HEAD_DIM
    n_grp = hs // SSM_HEADS_PER_GROUP
    gn = n_grp * SSM_D_STATE
    conv_ssm = d_inner + 2 * gn
    hq = hv // 2
    key, val = hq * GDN_HEAD, hv * GDN_HEAD
    conv_gdn = 2 * key + val
    o_dt = d_inner + conv_ssm
    o_qkv = o_dt + hs
    o_b = o_qkv + conv_gdn + val
    o_a = o_b + hv
    o_gs = o_a + hv
    n_proj = o_gs + 2 * d
    assert 4 * w_in.shape[-1] == n_proj and ssm_conv_w.shape[-1] * 4 == conv_ssm == conv_gdn
    a_z, a_q = 0, o_dt
    a_gs = a_q + conv_gdn + val
    a_gg = a_gs + d
    a_small = a_gg + d
    n_al = -(-(a_small + LANES) // MM_TILE_N) * MM_TILE_N

    def to_aligned(w):
        z = lambda n: jnp.zeros((n, w.shape[1]), w.dtype)
        return jnp.concatenate([
            w[:o_dt], w[o_qkv:o_b], w[o_gs:],
            w[o_dt:o_qkv], z(LANE_B - hs), w[o_b:o_a], z(LANE_A - LANE_B - hv), w[o_a:o_gs], z(LANES - LANE_A - hv),
            z(n_al - a_small - LANES)], axis=0)

    def from_aligned(w):
        s = a_small
        return jnp.concatenate([
            w[:o_dt], w[s + LANE_DT:s + LANE_DT + hs], w[a_q:a_gs], w[s + LANE_B:s + LANE_B + hv],
            w[s + LANE_A:s + LANE_A + hv], w[a_gs:a_small]], axis=0)

    def lanes(vec, at):
        return jnp.zeros((1, LANES), F32).at[:, at:at + vec.shape[-1]].set(vec.reshape(1, -1))

    n_cw = CONV_K * ssm_conv_w.shape[-1]
    small_in = gather_flat("ag_small", jnp.concatenate([c.reshape(-1), ssm_conv_w.reshape(-1), gdn_conv_w.reshape(-1)]))
    c_all = small_in[:, :d]
    by_chip = small_in[0::2]

    def whole_conv_w(lo):
        return jnp.transpose(by_chip[:, lo:lo + n_cw].reshape(4, CONV_K, -1), (1, 0, 2)).reshape(CONV_K, -1)

    cw_ssm, cw_gdn = whole_conv_w(d), whole_conv_w(d + n_cw)
    cb_ssm = ssm_conv_b

    n_ada = w_ada.shape[-1]
    b_q = lax.dynamic_slice_in_dim(b_ada, quarter * n_ada, n_ada, axis=1)
    mod_q = _whole("ada_fwd", lambda ca, w, b: _bdot(_silu(ca), w) + b, [c_all, w_ada[0], b_q], [((N_DEV, n_ada), F32)])[0]
    mod_all = gather_flat("ag_mod", mod_q.reshape(-1)).reshape(N_DEV, N_DEV, n_ada)[0::2]
    mod = lax.dynamic_index_in_dim(mod_all, batch, axis=1, keepdims=False).reshape(1, 4 * n_ada)
    sh1, sc1, g1, sh2, sc2, g2 = [mod[:, i * d:(i + 1) * d] for i in range(6)]

    transposed = lambda a: jnp.swapaxes(a, 1, 2)
    own = [w.astype(BF16) for w in (transposed(w_in)[0], w_ssm_up[0], w_gdn_up[0], w_out[0], w_mlp_up[0], w_mlp_down[0])]
    with_own = lambda gs, ws: [lax.dynamic_update_index_in_dim(g, w, quarter, 0) for g, w in zip(gs, ws)]
    rows_major = lambda g: g.reshape(-1, g.shape[2])
    wb_in = to_aligned(rows_major(with_own(run_exchange("ag_w_in", all_gather_shards(own[:1], [1])), own[:1])[0]))

    h1 = rowmap("premix", _premix, [xt], [norm_mix_pre, sc1, sh1], [(d, BF16)])[0]
    proj, gathered = matmul("in_proj", h1, wb_in, tb=True, comm=all_gather_shards(own[1:], [0] * 5))
    gathered = with_own(gathered, own[1:])
    wb_ssm_up, wb_gdn_up, wb_out = rows_major(gathered[0]), rows_major(gathered[1]), rows_major(gathered[2])
    wb_up, wb_down = gathered[3], rows_major(gathered[4])

    wide = 2 * LANES
    ssd_rows = [(proj, d_inner + conv_ssm, a_z, True), (proj, LANES, a_small, False)]
    ssd_consts = [(cw_ssm, False), (cb_ssm, False), (lanes(ssm_dt_bias, LANE_DT), False), (lanes(ssm_A_log, LANE_DT), False),
                  (lanes(ssm_D, LANE_DT), False), (ssm_norm_w.reshape(n_grp, 1, wide), False)]
    y_ssm_n, st_ssm, tails_ssm = scan_fwd("ssd_fwd", ssd_step, SSM_CHUNK, 1, ssd_rows, ssd_consts, d_inner, 1,
                                          state_shape=(n_grp,) + STATE_SHAPE, tail_cols=conv_ssm)
    gdn_rows = [(proj, conv_gdn + val, a_q, True), (proj, LANES, a_small, False)]
    gdn_consts = [(cw_gdn, False), (lanes(gdn_dt_bias, LANE_A), False), (lanes(gdn_A_log, LANE_A), False), (gdn_norm_w, False)]
    gdn_rows_per_step = GDN_CHUNK * GDN_CHUNKS_PER_STEP
    y_gdn_n, st_gdn, tails_gdn, inv_gdn = scan_fwd(
        "gdn_fwd", gdn_step, gdn_rows_per_step, 1, gdn_rows, gdn_consts, val, 1, keep=(hv, gdn_rows_per_step, gdn_rows_per_step),
        state_shape=(hq,) + STATE_SHAPE, tail_cols=conv_gdn)

    y_ssm = matmul("ssm_up", y_ssm_n, wb_ssm_up)
    y_gdn = matmul("gdn_up", y_gdn_n, wb_gdn_up)
    gates = [(proj, d, a_gs), (proj, d, a_gg)]
    merged = rowmap("merge", _merge, gates + [y_ssm, y_gdn], [], [(d, BF16)])[0]
    u = matmul("w_out", merged, wb_out)
    post_consts = [norm_mix_post, g1, norm_mlp_pre, sc2, sh2]
    x1, h2 = rowmap("postmix", _postmix, [xt, u], post_consts, [(d, F32), (d, BF16)])
    relu2 = lambda acc: (acc, jnp.square(jnp.maximum(acc, 0.0)))
    a_up, act = matmul("mlp_up", h2, wb_up, out_dtypes=(BF16, BF16), epi=relu2, b_quarters=True)
    def final_bwd(y2_, x1_, tgt, w_, g_):
        x2, vjp = jax.vjp(_final, x1_, y2_, w_, g_)
        err = x2 - tgt
        loss = 0.5 * jnp.sum(jnp.mean(err * err, axis=-1, keepdims=True), axis=0, keepdims=True)
        dx1, dy2, dw, dg = vjp(err / d)
        return dx1, dy2, loss, dw, dg

    dx1, dy2, loss_part, d_norm_mlp_post, dg2 = matmul(
        "mlp_down_final", act, wb_down, out_dtypes=(F32, BF16), epi=final_bwd, extras=[x1, target], consts=[norm_mlp_post, g2],
        accs=[((1, 1), F32), ((1, d), F32), ((1, d), F32)])

    d_a = matmul("mlp_down_dx", dy2, wb_down, tb=True, out_dtypes=(BF16,), extras=[a_up],
                 epi=lambda acc, a: acc * 2.0 * jnp.maximum(a.astype(F32), 0.0))
    gw_down = matmul("mlp_down_dw", act, dy2, ta=True)
    dh2 = matmul("mlp_up_dx", d_a, wb_up, tb=True, b_quarters=True)
    gw_up = matmul("mlp_up_dw", h2, d_a, ta=True, out_quarters=True)

    def postmix_bwd(x_, u_, dx1_, dh2_, *cs):
        _, vjp = jax.vjp(_postmix, x_, u_, *cs)
        return vjp((dx1_, dh2_))

    dxa, du, d_norm_mix_post, dg1, d_norm_mlp_pre, dsc2, dsh2 = rowmap(
        "postmix_bwd", postmix_bwd, [xt, u, dx1, dh2], post_consts, [(d, F32), (d, BF16)], [((1, d), F32)] * 5)
    d_merged = matmul("w_out_dx", du, wb_out, tb=True)
    gw_out = matmul("w_out_dw", merged, du, ta=True)

    def merge_bwd(gs, gg, ys, yg, dm):
        _, vjp = jax.vjp(_merge, gs, gg, ys, yg)
        dgs, dgg, dys, dyg = vjp(dm)
        return dys, dyg, jnp.concatenate([dgs, dgg], axis=1)

    dy_ssm, dy_gdn, dproj = rowmap("merge_bwd", merge_bwd, gates + [y_ssm, y_gdn, d_merged], [],
                                   [(d, BF16), (d, BF16), (2 * d, BF16, jax.ShapeDtypeStruct((t, n_al), BF16), a_gs)])
    dy_ssm_n = matmul("ssm_up_dx", dy_ssm, wb_ssm_up, tb=True, out_dtypes=(BF16,))
    quarters_rows = lambda g: g.reshape(4, g.shape[0] // 4, g.shape[1])
    mlp_grads = [gw_up, quarters_rows(gw_down)]
    gw_ssm_up, mlp_from_sibling = matmul("ssm_up_dw", y_ssm_n, dy_ssm, ta=True, comm=exchange_halves_d2d(mlp_grads, [0, 0]))
    dy_gdn_n = matmul("gdn_up_dx", dy_gdn, wb_gdn_up, tb=True, out_dtypes=(BF16,))
    gw_gdn_up = matmul("gdn_up_dw", y_gdn_n, dy_gdn, ta=True)

    dproj, dsmall_ssm, dcw_ssm, dcb_ssm, d_sdtb, d_salog, d_sdsk, d_snw = scan_bwd(
        "ssd_bwd", ssd_step, SSM_CHUNK, 1, ssd_rows, ssd_consts, [st_ssm, tails_ssm], dy_ssm_n, [BF16, F32], 1, {0: dproj})
    dproj, dsmall_gdn, dcw_gdn, d_gdtb, d_galog, d_gnw = scan_bwd(
        "gdn_bwd", gdn_step, gdn_rows_per_step, 1, gdn_rows, gdn_consts, [st_gdn, tails_gdn], dy_gdn_n, [BF16, F32], 1,
        {0: dproj}, kept=inv_gdn)
    tail = n_al - a_small
    dproj = rowmap("small_sum", lambda a, b: jnp.concatenate([a + b, jnp.zeros((a.shape[0], tail - LANES), F32)], axis=1),
                   [dsmall_ssm, dsmall_gdn], [], [(tail, BF16, dproj, a_small)])[0]
    mix32, mix16 = reduce_on_chip("mix", [quarters_rows(gw_ssm_up), quarters_rows(gw_gdn_up), quarters_rows(gw_out)], [0] * 3)
    mlp32, mlp16 = reduce_on_chip("mlp", mlp_grads, [0, 0], mlp_from_sibling)
    rest32, rest16 = mix32 + mlp32, mix16 + mlp16
    gw_in_al, rest_chips = matmul("in_proj_dw", dproj, h1, ta=True, comm=exchange_quarters_ici(rest16))
    in32, in16 = reduce_on_chip("in", [quarters_rows(from_aligned(gw_in_al))], [1])
    def premix_bwd(dh1_, x_, dxa_, w_, sc_, sh_):
        _, vjp = jax.vjp(_premix, x_, w_, sc_, sh_)
        dx, dw, dsc, dsh = vjp(dh1_)
        return dx + dxa_, dw, dsc, dsh

    (grad_x, d_norm_mix_pre, dsc1, dsh1), in_chips = matmul(
        "in_proj_dx", dproj, wb_in, comm=exchange_quarters_ici(in16), epi=premix_bwd, extras=[xt, dxa],
        consts=[norm_mix_pre, sc1, sh1], accs=[((1, d), F32)] * 3)

    dmod_all = gather_flat("ag_dmod", jnp.concatenate([dsh1, dsc1, dg1, dsh2, dsc2, dg2], axis=1).reshape(-1))
    dmod_q = lax.dynamic_slice_in_dim(dmod_all, quarter * n_ada, n_ada, axis=1)
    gw_ada, gb_ada = _whole(
        "ada_bwd", lambda ca, dq_, da_: (_bdot(_silu(ca), dq_, TN), jnp.sum(da_, axis=0, keepdims=True)),
        [c_all, dmod_q, dmod_all], [((d, n_ada), F32), ((1, 4 * n_ada), F32)])

    partial = [d_norm_mix_pre, d_norm_mix_post, dcw_ssm, dcb_ssm, d_sdtb[:, LANE_DT:LANE_DT + hs], d_salog[:, LANE_DT:LANE_DT + hs],
               d_sdsk[:, LANE_DT:LANE_DT + hs], d_snw, dcw_gdn, d_gdtb[:, LANE_A:LANE_A + hv], d_galog[:, LANE_A:LANE_A + hv], d_gnw,
               d_norm_mlp_pre, d_norm_mlp_post, loss_part]
    sizes = [p.size for p in partial]
    stacked = gather_flat("ag_grads", jnp.concatenate([p.reshape(-1) for p in partial]))
    summed = _whole("small_sum8", lambda s: jnp.sum(s, axis=0, keepdims=True), [stacked], [((1, stacked.shape[1]), F32)])[0][0]
    offs = [0]
    for s in sizes:
        offs.append(offs[-1] + s)
    red = [summed[offs[i]:offs[i + 1]] for i in range(len(sizes))]
    loss = red[-1][0]
    my_cols = lambda full: lax.dynamic_slice_in_dim(full.reshape(CONV_K, -1), quarter * (n_cw // CONV_K), n_cw // CONV_K, axis=1)
    small_grads = {
        "b_ada": gb_ada, "norm_mix_pre": red[0], "norm_mix_post": red[1], "ssm_conv_w": my_cols(red[2]), "ssm_conv_b": red[3],
        "ssm_dt_bias": red[4], "ssm_A_log": red[5], "ssm_D": red[6], "ssm_norm_w": red[7], "gdn_conv_w": my_cols(red[8]),
        "gdn_dt_bias": red[9], "gdn_A_log": red[10], "gdn_norm_w": red[11], "norm_mlp_pre": red[12], "norm_mlp_post": red[13]}

    big_names = ["w_in", "w_ssm_up", "w_gdn_up", "w_out", "w_mlp_up", "w_mlp_down"]
    big_grads = dict(zip(big_names, reduce_across_chips(in32 + rest32, in_chips + rest_chips)))

    names = ['w_ada', 'b_ada', 'norm_mix_pre', 'norm_mix_post', 'w_in', 'ssm_conv_w', 'ssm_conv_b', 'ssm_dt_bias', 'ssm_A_log', 'ssm_D',
             'ssm_norm_w', 'gdn_conv_w', 'gdn_dt_bias', 'gdn_A_log', 'gdn_norm_w', 'w_ssm_up', 'w_gdn_up', 'w_out', 'norm_mlp_pre',
             'norm_mlp_post', 'w_mlp_up', 'w_mlp_down']
    grad, delta, new_m, new_v = {}, {}, {}, {}
    for n, (mine, other) in big_grads.items():
        view, axis = (transposed, 1) if n == "w_in" else ((lambda a: a), 0)
        res = adamw_halves("adamw_" + n, view(args[n]), mine, other, view(args["m_" + n]), view(args["v_" + n]), axis)
        grad[n], delta[n], new_m[n], new_v[n] = [view(a) for a in res]
    grad["w_ada"] = gw_ada.reshape(w_ada.shape)
    delta["w_ada"], new_m["w_ada"], new_v["w_ada"] = adamw("adamw_w_ada", w_ada, gw_ada, m_w_ada, v_w_ada)
    small_names = [n for n in names if n not in grad]
    flat = lambda pre: jnp.concatenate([args[pre + n].reshape(-1) for n in small_names]).reshape(1, 1, -1)
    g_flat = jnp.concatenate([small_grads[n].reshape(-1) for n in small_names]).reshape(1, -1)
    dl, nm, nv = adamw("adamw_small", flat(""), g_flat, flat("m_"), flat("v_"))
    off = 0
    for n in small_names:
        shape = args[n].shape
        size = args[n].size
        grad[n], delta[n], new_m[n], new_v[n] = [a.reshape(-1)[off:off + size].reshape(shape) for a in (g_flat, dl, nm, nv)]
        off += size

    return (loss, grad_x.reshape(x.shape), *[grad[n] for n in names], *[delta[n] for n in names],
            *[new_m[n] for n in names], *[new_v[n] for n in names])
```

```python
import functools

import jax
import jax.numpy as jnp
from jax import lax
from jax.experimental import pallas as pl
from jax.experimental.pallas import tpu as pltpu

F32 = jnp.float32
BF16 = jnp.bfloat16
MESH = pl.DeviceIdType.MESH

EPS = 1e-6
SSM_HEAD_DIM = 64
SSM_HEADS_PER_GROUP = 4
SSM_D_STATE = 128
SSM_CHUNK = 128
GDN_HEAD = 128
GDN_CHUNK = 64
CONV_K = 4
TAIL_ROWS = 8
LANE_DT, LANE_B, LANE_A = 0, 32, 48
ADAM_LR, ADAM_B1, ADAM_B2, ADAM_EPS, ADAM_WD, ADAM_STEP = 0.001, 0.9, 0.999, 1e-08, 0.01, 10

VMEM_LIMIT_BYTES = 56 * 1024 * 1024
LANES = 128
N_DEV = 8

NN = (((1,), (0,)), ((), ()))
NT = (((1,), (1,)), ((), ()))
TN = (((0,), (0,)), ((), ()))


BNN = (((2,), (1,)), ((0,), (0,)))
BNT = (((2,), (2,)), ((0,), (0,)))
BTN = (((1,), (1,)), ((0,), (0,)))
_KIND = {NN: ("NN", 0), NT: ("NT", 0), TN: ("TN", 0), BNN: ("NN", 1), BNT: ("NT", 1), BTN: ("TN", 1)}
_DIMS = {"NN": (NN, BNN), "NT": (NT, BNT), "TN": (TN, BTN)}


def _dg(a, b, dims):
    return lax.dot_general(a, b, dims, preferred_element_type=F32)


def _raw_bf16(a, b, dims):
    return _dg(a.astype(BF16), b.astype(BF16), dims)


def _raw_bf16x3(a, b, dims):
    ah, bh = a.astype(BF16), b.astype(BF16)
    al, bl = (a - ah.astype(F32)).astype(BF16), (b - bh.astype(F32)).astype(BF16)
    return _dg(ah, bh, dims) + (_dg(ah, bl, dims) + _dg(al, bh, dims))


def _make_dot(raw):
    @functools.partial(jax.custom_vjp, nondiff_argnums=(2,))
    def dot(a, b, dims):
        return raw(a, b, dims)

    def fwd(a, b, dims):
        return raw(a, b, dims), (a, b)

    def bwd(dims, res, ct):
        a, b = res
        kind, batched = _KIND[dims]
        d = lambda k: _DIMS[k][batched]
        if kind == "NN":
            da, db = raw(ct, b, d("NT")), raw(a, ct, d("TN"))
        elif kind == "NT":
            da, db = raw(ct, b, d("NN")), raw(ct, a, d("TN"))
        else:
            da, db = raw(b, ct, d("NT")), raw(a, ct, d("NN"))
        return da.astype(a.dtype), db.astype(b.dtype)

    dot.defvjp(fwd, bwd)
    return lambda a, b, dims=NN: dot(a, b, dims)


_bdot = _make_dot(_raw_bf16)
_hdot = _make_dot(_raw_bf16x3)


def _mask_dot(mask, x, dims):
    m = mask.astype(BF16)
    hi = x.astype(BF16)
    r = x - hi.astype(F32)
    mid = r.astype(BF16)
    lo = (r - mid.astype(F32)).astype(BF16)
    return sum(_dg(m, p, dims) for p in (hi, mid, lo))


def _sigmoid(x):
    return 0.5 * jnp.tanh(0.5 * x) + 0.5


def _silu(x):
    return x * _sigmoid(x)


def _softplus(x):
    return jnp.maximum(x, 0.0) + jnp.log(1.0 + jnp.exp(-jnp.abs(x)))


def _rms(x, w):
    return x * lax.rsqrt(jnp.mean(x * x, axis=-1, keepdims=True) + EPS) * w


def _lane_col(m, idx):
    lane = lax.broadcasted_iota(jnp.int32, m.shape, 1)
    return jnp.sum(jnp.where(lane == idx, m, 0.0), axis=1, keepdims=True)


def _tril(n, strict=False, seg=None):
    r = lax.broadcasted_iota(jnp.int32, (n, n), 0)
    c = lax.broadcasted_iota(jnp.int32, (n, n), 1)
    low = (r > c) if strict else (r >= c)
    if seg is None or seg >= n:
        return low
    shift = seg.bit_length() - 1
    return jnp.logical_and(low, (r >> shift) == (c >> shift))


@functools.partial(jax.custom_vjp, nondiff_argnums=(1,))
def _cumsum_rows(x, seg):
    return _mask_dot(_tril(x.shape[0], seg=seg), x, NN)


def _cumsum_rows_fwd(x, seg):
    return _cumsum_rows(x, seg), None


def _cumsum_rows_bwd(seg, _, ct):
    return (_mask_dot(_tril(ct.shape[0], seg=seg), ct, TN),)


_cumsum_rows.defvjp(_cumsum_rows_fwd, _cumsum_rows_bwd)


def _head_rows(m_t, idx):
    sub = lax.broadcasted_iota(jnp.int32, m_t.shape, 0)
    return jnp.sum(jnp.where(sub == idx, m_t, 0.0), axis=0, keepdims=True)


def _params(sem):
    return pltpu.CompilerParams(dimension_semantics=sem, vmem_limit_bytes=VMEM_LIMIT_BYTES)


def _into_plumbing(outs, first_input):
    arrays, aliases = [], {}
    for k, o in enumerate(outs):
        if len(o) > 4 and not isinstance(o[4], jax.ShapeDtypeStruct):
            aliases[first_input + len(arrays)] = k
            arrays.append(o[4])
    return arrays, aliases


def blockmap(name, fn, grid, ins, outs, accs=(), scalars=None):
    n_in, n_out, n_acc = len(ins), len(outs), len(accs)
    n_grid = len(grid)
    n_pre = 0 if scalars is None else 1
    into_arrays, aliases = _into_plumbing(outs, n_pre + n_in)
    n_into = len(into_arrays)

    def body(*refs):
        refs = refs[n_pre:n_pre + n_in] + refs[n_pre + n_in + n_into:]
        vals = fn(*[r[...] for r in refs[:n_in]])
        if not isinstance(vals, (tuple, list)):
            vals = (vals,)
        for r, v in zip(refs[n_in:n_in + n_out], vals[:n_out]):
            r[...] = v.astype(r.dtype)
        if n_acc:
            first = functools.reduce(jnp.logical_and, [pl.program_id(a) == 0 for a in range(n_grid)])
            acc_refs = refs[n_in + n_out:]

            @pl.when(first)
            def _():
                for r in acc_refs:
                    r[...] = jnp.zeros(r.shape, r.dtype)

            for r, v in zip(acc_refs, vals[n_out:]):
                r[...] += v.astype(r.dtype)

    zeros = lambda nd: (lambda *_: (0,) * nd)
    in_specs = [pl.BlockSpec(b, im) for _, b, im in ins] + [pl.BlockSpec(memory_space=pl.ANY)] * n_into
    out_specs = [pl.BlockSpec(o[2], o[3]) for o in outs] + [pl.BlockSpec(s, zeros(len(s))) for s, _ in accs]
    out_shape = [jax.ShapeDtypeStruct(o[0], o[1]) for o in outs] + [jax.ShapeDtypeStruct(s, d) for s, d in accs]
    cparams = _params(("arbitrary",) * n_grid if n_acc else ("parallel",) * n_grid)
    arrays = [a for a, _, _ in ins] + into_arrays
    if scalars is None:
        return pl.pallas_call(body, name=name, grid=grid, in_specs=in_specs, out_specs=out_specs, out_shape=out_shape,
                              input_output_aliases=aliases, compiler_params=cparams)(*arrays)
    spec = pltpu.PrefetchScalarGridSpec(num_scalar_prefetch=1, grid=grid, in_specs=in_specs, out_specs=out_specs)
    return pl.pallas_call(body, name=name, grid_spec=spec, out_shape=out_shape, input_output_aliases=aliases,
                          compiler_params=cparams)(scalars, *arrays)


def rowmap(name, fn, rows, consts, outs, accs=(), rb=512):
    norm = [(r, r.shape[1], 0) if not isinstance(r, tuple) else (r[0], r[1], r[2] // r[1]) for r in rows]
    assert all(not isinstance(r, tuple) or r[2] % r[1] == 0 for r in rows)
    t = norm[0][0].shape[0]
    rb = min(rb, t)
    ins = [(a, (rb, n), (lambda i, cb=cb: (i, cb))) for a, n, cb in norm]
    ins += [(cst, cst.shape, (lambda i, nd=cst.ndim: (0,) * nd)) for cst in consts]
    o = []
    for out in outs:
        if len(out) == 2:
            o.append(((t, out[0]), out[1], (rb, out[0]), lambda i: (i, 0)))
        else:
            n, d, into, off = out
            assert off % n == 0 and into.dtype == d
            o.append((into.shape, d, (rb, n), (lambda i, cb=off // n: (i, cb)), into))
    return blockmap(name, fn, (t // rb,), ins, o, accs)


MM_TILE_M, MM_TILE_N, MM_TILE_K = 1024, 1024, 2048


def _tile(dim, cap):
    if dim <= cap:
        return dim
    best = max(t for t in range(LANES, cap + 1, LANES) if dim % t == 0)
    return best


EPI_ROWS = 256


def matmul(name, a, b, ta=False, tb=False, out_dtypes=(F32,), epi=None, extras=(), comm=None, b_quarters=False, out_quarters=False,
           consts=(), accs=()):
    (k_dim, m_dim) = a.shape if ta else a.shape[::-1]
    if b_quarters:
        quarter = b.shape[2]
        b_rows, b_cols = b.shape[1], 4 * quarter
    else:
        b_rows, b_cols = b.shape
    n_dim = b_rows if tb else b_cols
    assert (b_cols if tb else b_rows) == k_dim, (name, a.shape, b.shape)
    tm, tn, tk = _tile(m_dim, MM_TILE_M), _tile(n_dim, MM_TILE_N), _tile(k_dim, MM_TILE_K)
    if b_quarters and tb:
        tk = quarter
    elif b_quarters or out_quarters:
        tn = quarter if b_quarters else n_dim // 4
    grid = (m_dim // tm, n_dim // tn, k_dim // tk)
    k_steps = grid[2]
    n_extra, n_out, n_const, n_acc = len(extras), len(out_dtypes), len(consts), len(accs)
    n_cin, n_cout = (len(comm.operands), len(comm.out_shapes)) if comm else (0, 0)
    dims = (((0 if ta else 1,), (1 if tb else 0,)), ((), ()))
    row_wise = bool(consts or accs)
    n_in = 2 + n_extra + n_const + n_cin

    def body(*refs):
        ins, outs, scratch = refs[:n_in], refs[n_in:][:n_out + n_acc + n_cout], refs[n_in + n_out + n_acc + n_cout:]
        extra_refs, const_refs = ins[2:2 + n_extra], ins[2 + n_extra:2 + n_extra + n_const]
        out_refs, acc_refs = outs[:n_out], outs[n_out:n_out + n_acc]
        ids = [pl.program_id(ax) for ax in range(3)]
        step = (ids[0] * grid[1] + ids[1]) * grid[2] + ids[2]
        if comm:
            comm_refs = (ins[2 + n_extra + n_const:], outs[n_out + n_acc:], scratch[-2], scratch[-1])

            @pl.when(step == 0)
            def _():
                comm.start(*comm_refs)

            @pl.when(step == (grid[0] * grid[1] * grid[2]) // 2)
            def _():
                comm.middle(*comm_refs)

        if n_acc:
            @pl.when(step == 0)
            def _():
                for r in acc_refs:
                    r[...] = jnp.zeros(r.shape, r.dtype)

        def finish(acc):
            pieces = [slice(r0, r0 + EPI_ROWS) for r0 in range(0, tm, EPI_ROWS)] if row_wise and tm % EPI_ROWS == 0 else [slice(None)]
            for rows in pieces:
                args = [acc[rows]] + [r[rows] for r in extra_refs] + [r[...] for r in const_refs]
                vals = (args[0],) if epi is None else epi(*args)
                if not isinstance(vals, (tuple, list)):
                    vals = (vals,)
                for r, v in zip(out_refs, vals[:n_out]):
                    r[rows] = v.astype(r.dtype)
                for r, v in zip(acc_refs, vals[n_out:]):
                    r[...] += v.astype(r.dtype)

        prod = lax.dot_general(ins[0][...].astype(BF16), ins[1][...].astype(BF16), dims, preferred_element_type=F32)
        if k_steps == 1:
            finish(prod)
        else:
            acc_ref = scratch[0]

            @pl.when(ids[2] == 0)
            def _():
                acc_ref[...] = jnp.zeros(acc_ref.shape, F32)

            acc_ref[...] += prod

            @pl.when(ids[2] == k_steps - 1)
            def _():
                finish(acc_ref)

        if comm:
            @pl.when(functools.reduce(jnp.logical_and, [i == g - 1 for i, g in zip(ids, grid)]))
            def _():
                comm.finish(*comm_refs)

    a_spec = pl.BlockSpec((tk, tm), lambda i, j, k: (k, i)) if ta else pl.BlockSpec((tm, tk), lambda i, j, k: (i, k))
    if b_quarters:
        b_spec = pl.BlockSpec((None, tn, tk), lambda i, j, k: (k, j, 0)) if tb else pl.BlockSpec((None, tk, tn), lambda i, j, k: (j, k, 0))
    else:
        b_spec = pl.BlockSpec((tn, tk), lambda i, j, k: (j, k)) if tb else pl.BlockSpec((tk, tn), lambda i, j, k: (k, j))
    mn_spec = pl.BlockSpec((tm, tn), lambda i, j, k: (i, j))
    out_spec = pl.BlockSpec((None, tm, tn), lambda i, j, k: (j, i, 0)) if out_quarters else mn_spec
    out_dims = (4, m_dim, tn) if out_quarters else (m_dim, n_dim)
    scratch_shapes = [] if k_steps == 1 else [pltpu.VMEM((tm, tn), F32)]
    if comm:
        scratch_shapes += [pltpu.SemaphoreType.DMA(comm.sem_shape), pltpu.SemaphoreType.DMA(comm.sem_shape)]
    whole = lambda shape: pl.BlockSpec(shape, lambda i, j, k, nd=len(shape): (0,) * nd)
    res = pl.pallas_call(
        body, name=name, grid=grid,
        in_specs=[a_spec, b_spec] + [mn_spec] * n_extra + [whole(c.shape) for c in consts] + [ANY] * n_cin,
        out_specs=[out_spec] * n_out + [whole(s) for s, _ in accs] + [ANY] * n_cout,
        out_shape=[jax.ShapeDtypeStruct(out_dims, d) for d in out_dtypes] + [jax.ShapeDtypeStruct(s, d) for s, d in accs]
        + (comm.out_shapes if comm else []),
        scratch_shapes=scratch_shapes,
        compiler_params=_params(("arbitrary",) * 3 if comm or accs else ("parallel", "parallel", "arbitrary")),
    )(a, b, *extras, *consts, *(comm.operands if comm else []))
    main = list(res[:n_out + n_acc]) if n_out + n_acc > 1 else res[0]
    return (main, list(res[n_out + n_acc:])) if comm else main


def ssd_step(g0, state, tail, zxbc, small, cw, cb, p_dtb, p_alog, p_dsk, nw):
    d_in, gn = state.shape[0] * 2 * LANES, state.shape[0] * LANES
    z = zxbc[:, :d_in]
    act, new_tail = _conv_silu_carried(tail, zxbc[:, d_in:], cw, cb)
    xs, bm, cm = act[:, :d_in], act[:, d_in:d_in + gn], act[:, d_in + gn:]
    hb, n = state.shape[0], xs.shape[0]
    n_pair, n_head = 2 * hb, 4 * hb
    causal = _tril(n)
    dt_all = _softplus(small + p_dtb)
    a_all = dt_all * (-jnp.exp(p_alog))
    acum_all = _cumsum_rows(a_all, n)
    acum_t = acum_all.T
    lane0 = LANE_DT + SSM_HEADS_PER_GROUP * g0
    sub = lax.broadcasted_iota(jnp.int32, acum_t.shape, 0)
    heads = range(n_head)
    acum = jnp.stack([_lane_col(acum_all, lane0 + i) for i in heads])
    acum_row = jnp.stack([jnp.sum(jnp.where(sub == lane0 + i, acum_t, 0.0), axis=0, keepdims=True) for i in heads])
    dt = jnp.stack([_lane_col(dt_all, lane0 + i) for i in heads])
    dsk = jnp.stack([_lane_col(p_dsk, lane0 + i) for i in heads])
    decay = jnp.exp(jnp.where(causal, acum - acum_row, -jnp.inf))
    a_last = acum[:, n - 1:n, :]

    def split(a):
        return [a[:, i * LANES:(i + 1) * LANES] for i in range(a.shape[1] // LANES)]

    def pairs(a, axis=2):
        even = jnp.stack([a[2 * p] for p in range(n_pair)])
        odd = jnp.stack([a[2 * p + 1] for p in range(n_pair)])
        shape = (n_pair, LANES, LANES) if axis == 1 else (n_pair, a.shape[1], LANES)
        return jnp.where(lax.broadcasted_iota(jnp.int32, shape, axis) < SSM_HEAD_DIM, even, odd)

    bms, cms = split(bm), split(cm)
    cb = _bdot(jnp.stack(cms), jnp.stack(bms), BNT)
    cbd = jnp.stack([cb[i // SSM_HEADS_PER_GROUP] for i in heads]) * decay
    xp = jnp.stack(split(xs))
    xdt = xp * pairs(dt)
    yd = _bdot(cbd, jnp.stack([xdt[i // 2] for i in heads]), BNN)
    lane = lax.broadcasted_iota(jnp.int32, (n_pair, n, LANES), 2)
    y_diag = jnp.where(lane < SSM_HEAD_DIM, jnp.stack([yd[2 * p] for p in range(n_pair)]), jnp.stack([yd[2 * p + 1] for p in range(n_pair)]))
    st = state.reshape(n_pair, LANES, LANES)
    cm2 = jnp.stack([cms[p // 2] for p in range(n_pair)])
    bm2 = jnp.stack([bms[p // 2] for p in range(n_pair)])
    y_off = _bdot(cm2, st, BNT) * pairs(jnp.exp(acum))
    new = st * pairs(jnp.exp(a_last), axis=1) + _bdot(xdt * pairs(jnp.exp(a_last - acum)), bm2, BTN)
    y = y_diag + y_off + pairs(dsk) * xp
    y = jnp.concatenate([y[p] for p in range(n_pair)], axis=1) * _silu(z)
    wide = 2 * LANES
    y = jnp.concatenate([_rms(y[:, i * wide:(i + 1) * wide], nw[i]) for i in range(hb)], axis=1)
    return new.reshape(state.shape), new_tail, y


@functools.partial(jax.custom_vjp, nondiff_argnums=(1,))
def _unit_lower_inverse(a, seg):
    n = a.shape[-1]
    r = lax.broadcasted_iota(jnp.int32, (n, n), 0)
    c = lax.broadcasted_iota(jnp.int32, (n, n), 1)
    shift = min(INVERSE_BASE, seg).bit_length() - 1
    power = jnp.where((r >> shift) == (c >> shift), a, 0.0)
    inv = (r == c).astype(F32) - power
    span = 2
    while span < (1 << shift):
        power = _hdot(power, power, BNN)
        inv = inv + _hdot(inv, power, BNN)
        span *= 2
    while (1 << shift) < seg:
        below = jnp.logical_and((r >> (shift + 1)) == (c >> (shift + 1)), (r >> shift) != (c >> shift))
        inv = inv - _hdot(inv, _hdot(jnp.where(below, a, 0.0), inv, BNN), BNN)
        shift += 1
    return inv


def _unit_lower_inverse_fwd(a, seg):
    inv = _unit_lower_inverse(a, seg)
    return inv, inv


def _unit_lower_inverse_bwd(seg, inv, ct):
    return (-_hdot(_hdot(inv, ct, BTN), inv, BNT),)


_unit_lower_inverse.defvjp(_unit_lower_inverse_fwd, _unit_lower_inverse_bwd)


@jax.custom_vjp
def _known_inverse(a, inv):
    return inv


def _known_inverse_fwd(a, inv):
    return inv, inv


def _known_inverse_bwd(inv, ct):
    return _unit_lower_inverse_bwd(None, inv, ct)[0], jnp.zeros_like(inv)


_known_inverse.defvjp(_known_inverse_fwd, _known_inverse_bwd)


@functools.partial(jax.custom_vjp, nondiff_argnums=(1,))
def _rotate_rows(x, k):
    return x if k == 0 else pltpu.roll(x, k % x.shape[0], 0)


def _rotate_rows_fwd(x, k):
    return _rotate_rows(x, k), None


def _rotate_rows_bwd(k, _, ct):
    return (_rotate_rows(ct, -k),)


_rotate_rows.defvjp(_rotate_rows_fwd, _rotate_rows_bwd)


def _conv_silu_carried(tail, x, cw, cb=0.0):
    n = x.shape[0]
    ext = jnp.concatenate([tail, x], axis=0)
    pre = cb + sum(cw[j:j + 1, :] * _rotate_rows(ext, CONV_K - 1 - j)[TAIL_ROWS:] for j in range(CONV_K))
    return _silu(pre), x[n - TAIL_ROWS:]


def _l2norm(x):
    return x * lax.rsqrt(jnp.sum(x * x, axis=-1, keepdims=True) + EPS)


def gdn_step(hq0, state, tail, qkvz, small, cw, p_dtb, p_alog, nw, keep=False, kept=None):
    n, chunk = qkvz.shape[0], GDN_CHUNK
    hb = state.shape[0]
    nb = 2 * hb
    cur = state.reshape(nb, LANES, LANES)
    conv_cols = 4 * hb * LANES
    act, new_tail = _conv_silu_carried(tail, qkvz[:, :conv_cols], cw)
    q, k, v = act[:, :hb * LANES], act[:, hb * LANES:2 * hb * LANES], act[:, 2 * hb * LANES:]
    z = qkvz[:, conv_cols:]
    causal, strict = _tril(n, seg=chunk), _tril(n, True, seg=chunk)
    beta_all = _sigmoid(small)
    g_all = -jnp.exp(p_alog) * _softplus(small + p_dtb)
    gcum_all = _cumsum_rows(g_all, chunk)
    gcum_t = gcum_all.T
    split = lambda a: [a[:, i * LANES:(i + 1) * LANES] for i in range(a.shape[1] // LANES)]
    per_value_head = lambda a: jnp.stack([a[i // 2] for i in range(nb)])
    qh, kh = _l2norm(jnp.stack(split(q))) * (GDN_HEAD ** -0.5), _l2norm(jnp.stack(split(k)))
    q2, k2 = per_value_head(qh), per_value_head(kh)
    v2, z2 = jnp.stack(split(v)), jnp.stack(split(z))
    gcum = jnp.stack([_lane_col(gcum_all, LANE_A + 2 * hq0 + i) for i in range(nb)])
    gcum_row = jnp.stack([_head_rows(gcum_t, LANE_A + 2 * hq0 + i) for i in range(nb)])
    beta = jnp.stack([_lane_col(beta_all, LANE_B + 2 * hq0 + i) for i in range(nb)])
    dmat = jnp.exp(jnp.where(causal, gcum - gcum_row, -jnp.inf))
    a_low = jnp.where(strict, beta * per_value_head(_bdot(kh, kh, BNT)) * dmat, 0.0)
    inv = _unit_lower_inverse(a_low, chunk) if kept is None else _known_inverse(a_low, kept)
    egc = jnp.exp(gcum)
    u = _hdot(inv, v2 * beta, BNN)
    w = _hdot(inv, k2 * (beta * egc), BNN)
    q_dec = q2 * egc
    v_new, o_state = [], []
    for s in range(n // chunk):
        rows = slice(s * chunk, (s + 1) * chunk)
        v_new.append(u[:, rows] - _bdot(w[:, rows], cur, BNN))
        o_state.append(_bdot(q_dec[:, rows], cur, BNN))
        g_last = gcum[:, (s + 1) * chunk - 1:(s + 1) * chunk, :]
        k_dec = k2[:, rows] * jnp.exp(g_last - gcum[:, rows])
        cur = cur * jnp.exp(g_last) + _bdot(k_dec, v_new[-1], BTN)
    o = jnp.concatenate(o_state, axis=1) + _bdot(per_value_head(_bdot(qh, kh, BNT)) * dmat, jnp.concatenate(v_new, axis=1), BNN)
    out = _rms(o, nw) * _silu(z2)
    res = (cur.reshape(state.shape), new_tail, jnp.concatenate([out[i] for i in range(nb)], axis=1))
    return res + (inv,) if keep else res


STATE_SHAPE = (2, LANES, LANES)
GDN_CHUNKS_PER_STEP = 2
INVERSE_BASE = 16


def _scan_specs(rows, consts, chunk, chunk_of, hb):
    specs = []
    for _, n, off, per_group in rows:
        if per_group:
            assert off % (n * hb) == 0
            specs.append(pl.BlockSpec((chunk, n * hb), lambda c, g, cb=off // (n * hb): (chunk_of(c), cb + g)))
        else:
            assert off % n == 0
            specs.append(pl.BlockSpec((chunk, n), lambda c, g, cb=off // n: (chunk_of(c), cb)))
    for arr, per_group in consts:
        if per_group:
            specs.append(pl.BlockSpec((hb, 1, arr.shape[2]), lambda c, g: (g, 0, 0)))
        else:
            specs.append(pl.BlockSpec(arr.shape, lambda c, g, nd=arr.ndim: (0,) * nd))
    return specs


def scan_fwd(name, step, chunk, n_grp, rows, consts, out_cols, hb, keep=None, state_shape=None, tail_cols=None):
    t = rows[0][0].shape[0]
    nc = t // chunk
    n_rows, n_consts = len(rows), len(consts)
    state_shape = state_shape or (hb,) + STATE_SHAPE
    carried = [state_shape] + ([(TAIL_ROWS, tail_cols)] if tail_cols else [])
    n_car = len(carried)

    def body(*refs):
        row_refs, const_refs = refs[:n_rows], refs[n_rows:n_rows + n_consts]
        y_ref = refs[n_rows + n_consts]
        saved_refs = refs[n_rows + n_consts + 1:n_rows + n_consts + 1 + n_car]
        scratch = refs[-n_car:]
        c, g = pl.program_id(0), pl.program_id(1)

        @pl.when(c == 0)
        def _():
            for s, shape in zip(scratch, carried):
                s[g] = jnp.zeros(shape, F32)

        cur = [s[g] for s in scratch]
        for r, v in zip(saved_refs, cur):
            r[...] = v
        vals = [r[...] for r in row_refs] + [r[...] for r in const_refs]
        res = step(g * hb, *cur, *vals) if keep is None else step(g * hb, *cur, *vals, keep=True)
        for s, v in zip(scratch, res[:n_car]):
            s[g] = v
        y_ref[...] = res[n_car].astype(y_ref.dtype)
        if keep is not None:
            refs[n_rows + n_consts + 1 + n_car][...] = res[n_car + 1]

    lead = (nc, n_grp // hb)
    out_specs = [pl.BlockSpec((chunk, out_cols * hb), lambda c, g: (c, g))]
    out_shape = [jax.ShapeDtypeStruct((t, n_grp * out_cols), BF16)]
    for shape in carried + ([keep] if keep is not None else []):
        out_specs.append(pl.BlockSpec((None, None) + shape, lambda c, g, nd=len(shape): (c, g) + (0,) * nd))
        out_shape.append(jax.ShapeDtypeStruct(lead + shape, F32))
    return pl.pallas_call(
        body, name=name, grid=lead,
        in_specs=_scan_specs(rows, consts, chunk, lambda c: c, hb),
        out_specs=out_specs, out_shape=out_shape,
        scratch_shapes=[pltpu.VMEM((n_grp // hb,) + shape, F32) for shape in carried],
        compiler_params=_params(("arbitrary", "arbitrary")),
    )(*[r[0] for r in rows], *[c[0] for c in consts])


def scan_bwd(name, step, chunk, n_grp, rows, consts, saved, dy, row_dtypes, hb, into, kept=None):
    t = rows[0][0].shape[0]
    nc = t // chunk
    n_rows, n_consts, n_car = len(rows), len(consts), len(saved)
    carried = [s.shape[2:] for s in saved]
    out_cols = dy.shape[1] // n_grp
    n_alias = sum(not isinstance(v, jax.ShapeDtypeStruct) for v in into.values())
    n_kept = 0 if kept is None else 1
    n_in = n_rows + n_consts + n_car + 1 + n_kept + n_alias

    def body(*refs):
        row_refs, const_refs = refs[:n_rows], refs[n_rows:n_rows + n_consts]
        saved_refs = refs[n_rows + n_consts:n_rows + n_consts + n_car]
        dy_ref = refs[n_rows + n_consts + n_car]
        outs = refs[n_in:-n_car]
        scratch = refs[-n_car:]
        c, g = pl.program_id(0), pl.program_id(1)

        @pl.when(c == 0)
        def _():
            for s, shape in zip(scratch, carried):
                s[g] = jnp.zeros(shape, F32)

        @pl.when(jnp.logical_and(c == 0, g == 0))
        def _():
            for r in outs[n_rows:]:
                r[...] = jnp.zeros(r.shape, r.dtype)

        f = functools.partial(step, g * hb) if kept is None else functools.partial(step, g * hb, kept=refs[n_rows + n_consts + n_car + 1][...])
        _, vjp = jax.vjp(f, *[r[...] for r in saved_refs], *[r[...] for r in row_refs], *[r[...] for r in const_refs])
        grads = vjp(tuple(s[g] for s in scratch) + (dy_ref[...].astype(F32),))
        for s, d in zip(scratch, grads[:n_car]):
            s[g] = d
        for (_, _, _, per_group), r, d in zip(rows, outs[:n_rows], grads[n_car:n_car + n_rows]):
            if per_group:
                r[...] = d.astype(r.dtype)
            else:
                @pl.when(g == 0)
                def _(r=r):
                    r[...] = jnp.zeros(r.shape, r.dtype)

                r[...] += d.astype(r.dtype)
        for (_, per_group), r, d in zip(consts, outs[n_rows:], grads[n_car + n_rows:]):
            if per_group:
                r[pl.ds(g * hb, hb)] += d
            else:
                r[...] += d

    rev = lambda c: nc - 1 - c
    out_specs, out_shape = [], []
    into_arrays, aliases = [], {}
    first_into = n_in - n_alias
    kept_arrays = [] if kept is None else [kept]
    by_step = lambda a: pl.BlockSpec((None, None) + a.shape[2:], lambda c, g, nd=a.ndim - 2: (rev(c), g) + (0,) * nd)
    for k, ((_, n, off, per_group), dt) in enumerate(zip(rows, row_dtypes)):
        if k in into:
            assert per_group and off % (n * hb) == 0 and into[k].dtype == dt
            out_specs.append(pl.BlockSpec((chunk, n * hb), lambda c, g, cb=off // (n * hb): (rev(c), cb + g)))
            out_shape.append(jax.ShapeDtypeStruct(into[k].shape, dt))
            if not isinstance(into[k], jax.ShapeDtypeStruct):
                aliases[first_into + len(into_arrays)] = k
                into_arrays.append(into[k])
        elif per_group:
            out_specs.append(pl.BlockSpec((chunk, n * hb), lambda c, g: (rev(c), g)))
            out_shape.append(jax.ShapeDtypeStruct((t, n_grp * n), dt))
        else:
            out_specs.append(pl.BlockSpec((chunk, n), lambda c, g: (rev(c), 0)))
            out_shape.append(jax.ShapeDtypeStruct((t, n), dt))
    for arr, _ in consts:
        out_specs.append(pl.BlockSpec(arr.shape, lambda c, g, nd=arr.ndim: (0,) * nd))
        out_shape.append(jax.ShapeDtypeStruct(arr.shape, F32))
    return pl.pallas_call(
        body, name=name, grid=(nc, n_grp // hb),
        in_specs=_scan_specs(rows, consts, chunk, rev, hb) + [by_step(s) for s in saved]
        + [pl.BlockSpec((chunk, out_cols * hb), lambda c, g: (rev(c), g))] + [by_step(k) for k in kept_arrays]
        + [pl.BlockSpec(memory_space=pl.ANY)] * len(into_arrays),
        out_specs=out_specs, out_shape=out_shape, input_output_aliases=aliases,
        scratch_shapes=[pltpu.VMEM((n_grp // hb,) + shape, F32) for shape in carried],
        compiler_params=_params(("arbitrary", "arbitrary")),
    )(*[r[0] for r in rows], *[c[0] for c in consts], *saved, dy, *kept_arrays, *into_arrays)


def _place():
    return lax.axis_index("x"), lax.axis_index("y"), lax.axis_index("c")


def _other_chips(x, y):
    return [(1 - x, y), (x, 1 - y), (1 - x, 1 - y)]


ANY = pl.BlockSpec(memory_space=pl.ANY)


def all_gather8(name, v):
    m_per, n = v.shape

    def body(x_ref, out_ref, send_sems, recv_sems, local_sem):
        x, y, c = _place()
        me, sibling = (x, y, c), (x, y, 1 - c)
        chips = _other_chips(x, y)

        def rows(px, py, pc):
            return out_ref.at[pl.ds((4 * px + 2 * py + pc) * m_per, m_per), :]

        def copy(k, block, to, src=None):
            return pltpu.make_async_remote_copy(
                src_ref=rows(*block) if src is None else src, dst_ref=rows(*block),
                send_sem=send_sems.at[k], recv_sem=recv_sems.at[k], device_id=to, device_id_type=MESH)

        mine = pltpu.make_async_copy(x_ref, rows(*me), local_sem)
        mine.start()
        first = [copy(0, me, sibling, src=x_ref)]
        first += [copy(1 + q, me, (*chip, c), src=x_ref) for q, chip in enumerate(chips)]
        for cp in first:
            cp.start()
        passed = [copy(4 + q, (*chip, c), sibling) for q, chip in enumerate(chips)]
        for q, chip in enumerate(chips):
            copy(1 + q, (*chip, c), me).wait_recv()
            passed[q].start()
        copy(0, sibling, me).wait_recv()
        for q, chip in enumerate(chips):
            copy(4 + q, (*chip, 1 - c), me).wait_recv()
        for cp in first + passed:
            cp.wait_send()
        mine.wait()

    return pl.pallas_call(
        body, name=name, out_shape=jax.ShapeDtypeStruct((N_DEV * m_per, n), v.dtype),
        in_specs=[pl.BlockSpec(memory_space=pltpu.VMEM)], out_specs=pl.BlockSpec(memory_space=pltpu.VMEM),
        scratch_shapes=[pltpu.SemaphoreType.DMA((7,)), pltpu.SemaphoreType.DMA((7,)), pltpu.SemaphoreType.DMA],
    )(v)


def gather_flat(name, vec):
    n = vec.shape[0]
    n_pad = -(-n // (8 * LANES)) * (8 * LANES)
    v = jnp.pad(vec, (0, n_pad - n)).reshape(8, n_pad // 8)
    return all_gather8(name, v).reshape(N_DEV, n_pad)[:, :n]


class Exchange:
    def __init__(self, operands, out_shapes, sem_shape, start, finish, middle=None):
        self.operands, self.out_shapes, self.sem_shape = list(operands), out_shapes, sem_shape
        self.start, self.middle, self.finish = start, middle or (lambda *refs: None), finish


def _start_all_wait_all(make_copies):
    def start(*refs):
        for cp in make_copies(*refs):
            cp.start()

    def finish(*refs):
        for cp in make_copies(*refs):
            cp.wait()

    return start, finish


def run_exchange(name, ex):
    n_in, n_out = len(ex.operands), len(ex.out_shapes)

    def body(*refs):
        ins, outs = refs[:n_in], refs[n_in:n_in + n_out]
        ex.start(ins, outs, *refs[n_in + n_out:])
        ex.middle(ins, outs, *refs[n_in + n_out:])
        ex.finish(ins, outs, *refs[n_in + n_out:])

    return pl.pallas_call(
        body, name=name, out_shape=ex.out_shapes, in_specs=[ANY] * n_in, out_specs=[ANY] * n_out,
        scratch_shapes=[pltpu.SemaphoreType.DMA(ex.sem_shape), pltpu.SemaphoreType.DMA(ex.sem_shape)],
    )(*ex.operands)


def _half(shape, axis, pc):
    h = shape[axis] // 2
    return (pl.ds(pc * h, h), slice(None)) if axis == 0 else (slice(None), pl.ds(pc * h, h))


def _half_shape(shape, axis):
    return tuple(s // 2 if a == axis else s for a, s in enumerate(shape))


def all_gather_shards(shards, axes):
    n_t = len(shards)
    n_sem = 12

    def copies(ins, outs, send_sems, recv_sems):
        x, y, c = _place()
        me, sibling, x_nbr, y_nbr = (x, y, c), (x, y, 1 - c), (1 - x, y, c), (x, 1 - y, c)
        own, of_x, of_y, of_diag = 2 * x + y, 2 * (1 - x) + y, 2 * x + 1 - y, 2 * (1 - x) + 1 - y

        def copy(t, k, quarter, pc, piece, to, from_input=False):
            axis = axes[t]
            h = ins[t].shape[axis] // 4
            cut = pl.ds((2 * pc + piece) * h, h)
            part = (cut, slice(None)) if axis == 0 else (slice(None), cut)
            dst = outs[t].at[(quarter,) + part]
            return pltpu.make_async_remote_copy(
                src_ref=ins[t].at[part] if from_input else dst, dst_ref=dst,
                send_sem=send_sems.at[t, k], recv_sem=recv_sems.at[t, k], device_id=to, device_id_type=MESH)

        stages = []
        for t in range(n_t):
            direct = [copy(t, 0, own, c, 0, x_nbr, True), copy(t, 2, own, c, 1, y_nbr, True),
                      copy(t, 1, own, c, 1, x_nbr, True), copy(t, 3, own, c, 0, y_nbr, True)]
            landing = [
                (copy(t, 0, of_x, c, 0, me), [copy(t, 4, of_x, c, 0, y_nbr), copy(t, 6, of_x, c, 0, sibling)]),
                (copy(t, 2, of_y, c, 1, me), [copy(t, 5, of_y, c, 1, x_nbr), copy(t, 8, of_y, c, 1, sibling)]),
                (copy(t, 1, of_x, c, 1, me), [copy(t, 7, of_x, c, 1, sibling)]),
                (copy(t, 3, of_y, c, 0, me), [copy(t, 9, of_y, c, 0, sibling)]),
                (copy(t, 4, of_diag, c, 0, me), [copy(t, 10, of_diag, c, 0, sibling)]),
                (copy(t, 5, of_diag, c, 1, me), [copy(t, 11, of_diag, c, 1, sibling)])]
            from_sibling = [copy(t, 6, of_x, 1 - c, 0, me), copy(t, 8, of_y, 1 - c, 1, me), copy(t, 7, of_x, 1 - c, 1, me),
                            copy(t, 9, of_y, 1 - c, 0, me), copy(t, 10, of_diag, 1 - c, 0, me), copy(t, 11, of_diag, 1 - c, 1, me)]
            stages.append((direct, landing, from_sibling))
        return stages

    def start(*refs):
        for direct, _, _ in copies(*refs):
            for cp in direct:
                cp.start()

    def pass_on(landing):
        for arrived, onward in landing:
            arrived.wait_recv()
            for cp in onward:
                cp.start()

    def middle(*refs):
        for _, landing, _ in copies(*refs):
            pass_on(landing[:4])

    def finish(*refs):
        stages = copies(*refs)
        for _, landing, _ in stages:
            pass_on(landing[4:])
        for direct, landing, from_sibling in stages:
            for cp in from_sibling:
                cp.wait_recv()
            for cp in direct + [cp for _, onward in landing for cp in onward]:
                cp.wait_send()

    return Exchange(shards, [jax.ShapeDtypeStruct((4,) + s.shape, s.dtype) for s in shards], (n_t, n_sem), start, finish, middle)


def exchange_halves_d2d(grads, axes):
    n_t = len(grads)

    def copies(ins, outs, send_sems, recv_sems):
        x, y, c = _place()
        return [pltpu.make_async_remote_copy(
            src_ref=ins[t].at[(slice(None),) + _half(ins[t].shape[1:], axes[t], 1 - c)], dst_ref=outs[t],
            send_sem=send_sems.at[t], recv_sem=recv_sems.at[t], device_id=(x, y, 1 - c), device_id_type=MESH) for t in range(n_t)]

    shapes = [jax.ShapeDtypeStruct((4,) + _half_shape(g.shape[1:], a), g.dtype) for g, a in zip(grads, axes)]
    return Exchange(grads, shapes, (n_t,), *_start_all_wait_all(copies))


def exchange_quarters_ici(parts):
    n_t = len(parts)

    def copies(ins, outs, send_sems, recv_sems):
        x, y, c = _place()
        return [pltpu.make_async_remote_copy(
            src_ref=ins[t].at[2 * px + py], dst_ref=outs[t].at[q],
            send_sem=send_sems.at[t, q], recv_sem=recv_sems.at[t, q], device_id=(px, py, c), device_id_type=MESH)
            for t in range(n_t) for q, (px, py) in enumerate(_other_chips(x, y))]

    shapes = [jax.ShapeDtypeStruct((3,) + p.shape[1:], p.dtype) for p in parts]
    return Exchange(parts, shapes, (n_t, 3), *_start_all_wait_all(copies))


def swap_d2d(halves):
    n_t = len(halves)

    def copies(ins, outs, send_sems, recv_sems):
        x, y, c = _place()
        return [pltpu.make_async_remote_copy(
            src_ref=ins[t], dst_ref=outs[t], send_sem=send_sems.at[t], recv_sem=recv_sems.at[t],
            device_id=(x, y, 1 - c), device_id_type=MESH) for t in range(n_t)]

    return Exchange(halves, [jax.ShapeDtypeStruct(h.shape, h.dtype) for h in halves], (n_t,), *_start_all_wait_all(copies))


BLOCK_BYTES = 1 << 20


def _row_block(r, c):
    fits = [rb for rb in range(16, r + 1, 16) if r % rb == 0 and rb * c * 4 <= BLOCK_BYTES]
    return max(fits) if fits else r


def _place_scalars():
    x, y, c = _place()
    return jnp.stack([c, 2 * x + y]).astype(jnp.int32)


def reduce_on_chip(tag, grads, axes, from_sibling=None):
    if from_sibling is None:
        from_sibling = run_exchange(f"rs_d2d_{tag}", exchange_halves_d2d(grads, axes))
    parts, parts_bf16 = [], []
    for t, (g, s, axis) in enumerate(zip(grads, from_sibling, axes)):
        _, h, cols = s.shape
        rb = _row_block(h, cols)
        nb = h // rb
        blk = lambda k, i, s_ref: (k, i, 0)
        mine = (lambda k, i, s_ref, nb=nb: (k, s_ref[0] * nb + i, 0)) if axis == 0 else (lambda k, i, s_ref: (k, i, s_ref[0]))
        p32, p16 = blockmap(
            f"rs_add_{tag}{t}", lambda a, b: (a + b, a + b), (4, nb),
            [(g, (None, rb, cols), mine), (s, (None, rb, cols), blk)],
            [(s.shape, F32, (None, rb, cols), blk), (s.shape, BF16, (None, rb, cols), blk)], scalars=_place_scalars())
        parts.append(p32)
        parts_bf16.append(p16)
    return parts, parts_bf16


def reduce_across_chips(parts, from_chips):
    halves = []
    for t, (p, q) in enumerate(zip(parts, from_chips)):
        _, h, cols = p.shape
        rb = _row_block(h, cols)
        halves.append(blockmap(
            f"rs_sum{t}", lambda a, b: a + b[0].astype(F32) + b[1].astype(F32) + b[2].astype(F32), (h // rb,),
            [(p, (None, rb, cols), lambda i, s_ref: (s_ref[1], i, 0)), (q, (3, rb, cols), lambda i, s_ref: (0, i, 0))],
            [((h, cols), F32, (rb, cols), lambda i, s_ref: (i, 0))], scalars=_place_scalars())[0])
    return list(zip(halves, run_exchange("rs_swap", swap_d2d(halves))))


def _adamw(w, g, m, v):
    m = ADAM_B1 * m + (1.0 - ADAM_B1) * g
    v = ADAM_B2 * v + (1.0 - ADAM_B2) * jnp.square(g)
    m_hat = m / (1.0 - ADAM_B1 ** ADAM_STEP)
    v_hat = v / (1.0 - ADAM_B2 ** ADAM_STEP)
    delta = -ADAM_LR * (m_hat / (jnp.sqrt(v_hat) + ADAM_EPS) + ADAM_WD * w)
    return delta, m, v


def adamw(name, w, g, m, v):
    _, r, c = w.shape
    rb = _row_block(r, c)
    blk3 = lambda a: (a, (None, rb, c), lambda i: (0, i, 0))
    return blockmap(name, _adamw, (r // rb,), [blk3(w), (g, (rb, c), lambda i: (i, 0)), blk3(m), blk3(v)],
                    [(w.shape, F32, (None, rb, c), lambda i: (0, i, 0))] * 3)


def adamw_halves(name, w, mine, other, m, v, axis):
    _, r, c = w.shape
    h, c = mine.shape
    rb = _row_block(h, c)
    nb = h // rb

    def body(s_ref, w_ref, mine_ref, other_ref, m_ref, v_ref, g_out, d_out, m_out, v_out):
        g = jnp.where(pl.program_id(0) == s_ref[0], mine_ref[...], other_ref[...])
        d, nm, nv = _adamw(w_ref[...], g, m_ref[...], v_ref[...])
        g_out[...], d_out[...], m_out[...], v_out[...] = g, d, nm, nv

    spec3 = pl.BlockSpec((None, rb, c), (lambda k, i, s_ref: (0, k * nb + i, 0)) if axis == 0 else (lambda k, i, s_ref: (0, i, k)))
    spec2 = pl.BlockSpec((rb, c), lambda k, i, s_ref: (i, 0))
    grid_spec = pltpu.PrefetchScalarGridSpec(num_scalar_prefetch=1, grid=(2, nb), in_specs=[spec3, spec2, spec2, spec3, spec3],
                                             out_specs=[spec3] * 4)
    return pl.pallas_call(body, name=name, grid_spec=grid_spec, out_shape=[jax.ShapeDtypeStruct(w.shape, F32)] * 4,
                          compiler_params=_params(("parallel", "parallel")))(_place_scalars(), w, mine, other, m, v)


def _whole(name, fn, ins, outs):
    return blockmap(name, fn, (1,), [(a, a.shape, lambda i, nd=a.ndim: (0,) * nd) for a in ins],
                    [(s, d, s, lambda i, nd=len(s): (0,) * nd) for s, d in outs])


def _premix(x, w, sc, sh):
    return _rms(x, w) * (1.0 + sc) + sh


def _postmix(x, u, w_post, g1, w_pre2, sc2, sh2):
    x1 = x + g1 * _rms(u, w_post)
    return x1, _premix(x1, w_pre2, sc2, sh2)


def _merge(gs, gg, ys, yg):
    return _sigmoid(gs) * ys + _sigmoid(gg) * yg


def _final(x1, y2, w_post2, g2):
    return x1 + g2 * _rms(y2, w_post2)


def kernel(x, c, w_ada, b_ada, norm_mix_pre, norm_mix_post, w_in, ssm_conv_w, ssm_conv_b, ssm_dt_bias, ssm_A_log, ssm_D, ssm_norm_w, gdn_conv_w, gdn_dt_bias, gdn_A_log, gdn_norm_w, w_ssm_up, w_gdn_up, w_out, norm_mlp_pre, norm_mlp_post, w_mlp_up, w_mlp_down, loss_target, m_w_ada, m_b_ada, m_norm_mix_pre, m_norm_mix_post, m_w_in, m_ssm_conv_w, m_ssm_conv_b, m_ssm_dt_bias, m_ssm_A_log, m_ssm_D, m_ssm_norm_w, m_gdn_conv_w, m_gdn_dt_bias, m_gdn_A_log, m_gdn_norm_w, m_w_ssm_up, m_w_gdn_up, m_w_out, m_norm_mlp_pre, m_norm_mlp_post, m_w_mlp_up, m_w_mlp_down, v_w_ada, v_b_ada, v_norm_mix_pre, v_norm_mix_post, v_w_in, v_ssm_conv_w, v_ssm_conv_b, v_ssm_dt_bias, v_ssm_A_log, v_ssm_D, v_ssm_norm_w, v_gdn_conv_w, v_gdn_dt_bias, v_gdn_A_log, v_gdn_norm_w, v_w_ssm_up, v_w_gdn_up, v_w_out, v_norm_mlp_pre, v_norm_mlp_post, v_w_mlp_up, v_w_mlp_down):
    args = dict(locals())
    xi, yi, ci = _place()
    quarter = 2 * xi + yi
    batch = 4 * xi + 2 * yi + ci

    xt, target = x[0], loss_target[0]
    t, d = xt.shape
    hs, hv = ssm_dt_bias.shape[-1], gdn_dt_bias.shape[-1]
    assert hs <= LANE_B - LANE_DT and hv <= LANE_A - LANE_B and hv % 2 == 0 and hs % SSM_HEADS_PER_GROUP == 0
    assert t % SSM_CHUNK == 0 and t % (GDN_CHUNK * GDN_CHUNKS_PER_STEP) == 0 and d % LANES == 0
    d_inner = hs * SSM_HEAD_DIM
    n_grp = hs // SSM_HEADS_PER_GROUP
    gn = n_grp * SSM_D_STATE
    conv_ssm = d_inner + 2 * gn
    hq = hv // 2
    key, val = hq * GDN_HEAD, hv * GDN_HEAD
    conv_gdn = 2 * key + val
    o_dt = d_inner + conv_ssm
    o_qkv = o_dt + hs
    o_b = o_qkv + conv_gdn + val
    o_a = o_b + hv
    o_gs = o_a + hv
    n_proj = o_gs + 2 * d
    assert 4 * w_in.shape[-1] == n_proj and ssm_conv_w.shape[-1] * 4 == conv_ssm == conv_gdn
    a_z, a_q = 0, o_dt
    a_gs = a_q + conv_gdn + val
    a_gg = a_gs + d
    a_small = a_gg + d
    n_al = -(-(a_small + LANES) // MM_TILE_N) * MM_TILE_N

    def to_aligned(w):
        z = lambda n: jnp.zeros((n, w.shape[1]), w.dtype)
        return jnp.concatenate([
            w[:o_dt], w[o_qkv:o_b], w[o_gs:],
            w[o_dt:o_qkv], z(LANE_B - hs), w[o_b:o_a], z(LANE_A - LANE_B - hv), w[o_a:o_gs], z(LANES - LANE_A - hv),
            z(n_al - a_small - LANES)], axis=0)

    def from_aligned(w):
        s = a_small
        return jnp.concatenate([
            w[:o_dt], w[s + LANE_DT:s + LANE_DT + hs], w[a_q:a_gs], w[s + LANE_B:s + LANE_B + hv],
            w[s + LANE_A:s + LANE_A + hv], w[a_gs:a_small]], axis=0)

    def lanes(vec, at):
        return jnp.zeros((1, LANES), F32).at[:, at:at + vec.shape[-1]].set(vec.reshape(1, -1))

    n_cw = CONV_K * ssm_conv_w.shape[-1]
    small_in = gather_flat("ag_small", jnp.concatenate([c.reshape(-1), ssm_conv_w.reshape(-1), gdn_conv_w.reshape(-1)]))
    c_all = small_in[:, :d]
    by_chip = small_in[0::2]

    def whole_conv_w(lo):
        return jnp.transpose(by_chip[:, lo:lo + n_cw].reshape(4, CONV_K, -1), (1, 0, 2)).reshape(CONV_K, -1)

    cw_ssm, cw_gdn = whole_conv_w(d), whole_conv_w(d + n_cw)
    cb_ssm = ssm_conv_b

    n_ada = w_ada.shape[-1]
    b_q = lax.dynamic_slice_in_dim(b_ada, quarter * n_ada, n_ada, axis=1)
    mod_q = _whole("ada_fwd", lambda ca, w, b: _bdot(_silu(ca), w) + b, [c_all, w_ada[0], b_q], [((N_DEV, n_ada), F32)])[0]
    mod_all = gather_flat("ag_mod", mod_q.reshape(-1)).reshape(N_DEV, N_DEV, n_ada)[0::2]
    mod = lax.dynamic_index_in_dim(mod_all, batch, axis=1, keepdims=False).reshape(1, 4 * n_ada)
    sh1, sc1, g1, sh2, sc2, g2 = [mod[:, i * d:(i + 1) * d] for i in range(6)]

    transposed = lambda a: jnp.swapaxes(a, 1, 2)
    own = [w.astype(BF16) for w in (transposed(w_in)[0], w_ssm_up[0], w_gdn_up[0], w_out[0], w_mlp_up[0], w_mlp_down[0])]
    with_own = lambda gs, ws: [lax.dynamic_update_index_in_dim(g, w, quarter, 0) for g, w in zip(gs, ws)]
    rows_major = lambda g: g.reshape(-1, g.shape[2])
    wb_in = to_aligned(rows_major(with_own(run_exchange("ag_w_in", all_gather_shards(own[:1], [1])), own[:1])[0]))

    h1 = rowmap("premix", _premix, [xt], [norm_mix_pre, sc1, sh1], [(d, BF16)])[0]
    proj, gathered = matmul("in_proj", h1, wb_in, tb=True, comm=all_gather_shards(own[1:], [0] * 5))
    gathered = with_own(gathered, own[1:])
    wb_ssm_up, wb_gdn_up, wb_out = rows_major(gathered[0]), rows_major(gathered[1]), rows_major(gathered[2])
    wb_up, wb_down = gathered[3], rows_major(gathered[4])

    wide = 2 * LANES
    ssd_rows = [(proj, d_inner + conv_ssm, a_z, True), (proj, LANES, a_small, False)]
    ssd_consts = [(cw_ssm, False), (cb_ssm, False), (lanes(ssm_dt_bias, LANE_DT), False), (lanes(ssm_A_log, LANE_DT), False),
                  (lanes(ssm_D, LANE_DT), False), (ssm_norm_w.reshape(n_grp, 1, wide), False)]
    y_ssm_n, st_ssm, tails_ssm = scan_fwd("ssd_fwd", ssd_step, SSM_CHUNK, 1, ssd_rows, ssd_consts, d_inner, 1,
                                          state_shape=(n_grp,) + STATE_SHAPE, tail_cols=conv_ssm)
    gdn_rows = [(proj, conv_gdn + val, a_q, True), (proj, LANES, a_small, False)]
    gdn_consts = [(cw_gdn, False), (lanes(gdn_dt_bias, LANE_A), False), (lanes(gdn_A_log, LANE_A), False), (gdn_norm_w, False)]
    gdn_rows_per_step = GDN_CHUNK * GDN_CHUNKS_PER_STEP
    y_gdn_n, st_gdn, tails_gdn, inv_gdn = scan_fwd(
        "gdn_fwd", gdn_step, gdn_rows_per_step, 1, gdn_rows, gdn_consts, val, 1, keep=(hv, gdn_rows_per_step, gdn_rows_per_step),
        state_shape=(hq,) + STATE_SHAPE, tail_cols=conv_gdn)

    y_ssm = matmul("ssm_up", y_ssm_n, wb_ssm_up)
    y_gdn = matmul("gdn_up", y_gdn_n, wb_gdn_up)
    gates = [(proj, d, a_gs), (proj, d, a_gg)]
    merged = rowmap("merge", _merge, gates + [y_ssm, y_gdn], [], [(d, BF16)])[0]
    post_consts = [norm_mix_post, g1, norm_mlp_pre, sc2, sh2]
    u, x1, h2 = matmul("w_out_postmix", merged, wb_out, out_dtypes=(F32, F32, BF16), extras=[xt], consts=post_consts,
                       epi=lambda u_, x_, *cs: (u_,) + _postmix(x_, u_, *cs))
    relu2 = lambda acc: (acc, jnp.square(jnp.maximum(acc, 0.0)))
    a_up, act = matmul("mlp_up", h2, wb_up, out_dtypes=(BF16, BF16), epi=relu2, b_quarters=True)
    def final_bwd(y2_, x1_, tgt, w_, g_):
        x2, vjp = jax.vjp(_final, x1_, y2_, w_, g_)
        err = x2 - tgt
        loss = 0.5 * jnp.sum(jnp.mean(err * err, axis=-1, keepdims=True), axis=0, keepdims=True)
        dx1, dy2, dw, dg = vjp(err / d)
        return dx1, dy2, loss, dw, dg

    dx1, dy2, loss_part, d_norm_mlp_post, dg2 = matmul(
        "mlp_down_final", act, wb_down, out_dtypes=(F32, BF16), epi=final_bwd, extras=[x1, target], consts=[norm_mlp_post, g2],
        accs=[((1, 1), F32), ((1, d), F32), ((1, d), F32)])

    d_a = matmul("mlp_down_dx", dy2, wb_down, tb=True, out_dtypes=(BF16,), extras=[a_up],
                 epi=lambda acc, a: acc * 2.0 * jnp.maximum(a.astype(F32), 0.0))
    gw_down = matmul("mlp_down_dw", act, dy2, ta=True)
    dh2 = matmul("mlp_up_dx", d_a, wb_up, tb=True, b_quarters=True)
    gw_up = matmul("mlp_up_dw", h2, d_a, ta=True, out_quarters=True)

    def postmix_bwd(x_, u_, dx1_, dh2_, *cs):
        _, vjp = jax.vjp(_postmix, x_, u_, *cs)
        return vjp((dx1_, dh2_))

    dxa, du, d_norm_mix_post, dg1, d_norm_mlp_pre, dsc2, dsh2 = rowmap(
        "postmix_bwd", postmix_bwd, [xt, u, dx1, dh2], post_consts, [(d, F32), (d, BF16)], [((1, d), F32)] * 5)
    d_merged = matmul("w_out_dx", du, wb_out, tb=True)
    gw_out = matmul("w_out_dw", merged, du, ta=True)

    def merge_bwd(gs, gg, ys, yg, dm):
        _, vjp = jax.vjp(_merge, gs, gg, ys, yg)
        dgs, dgg, dys, dyg = vjp(dm)
        return dys, dyg, jnp.concatenate([dgs, dgg], axis=1)

    dy_ssm, dy_gdn, dproj = rowmap("merge_bwd", merge_bwd, gates + [y_ssm, y_gdn, d_merged], [],
                                   [(d, BF16), (d, BF16), (2 * d, BF16, jax.ShapeDtypeStruct((t, n_al), BF16), a_gs)])
    dy_ssm_n = matmul("ssm_up_dx", dy_ssm, wb_ssm_up, tb=True, out_dtypes=(BF16,))
    quarters_rows = lambda g: g.reshape(4, g.shape[0] // 4, g.shape[1])
    mlp_grads = [gw_up, quarters_rows(gw_down)]
    gw_ssm_up, mlp_from_sibling = matmul("ssm_up_dw", y_ssm_n, dy_ssm, ta=True, comm=exchange_halves_d2d(mlp_grads, [0, 0]))
    dy_gdn_n = matmul("gdn_up_dx", dy_gdn, wb_gdn_up, tb=True, out_dtypes=(BF16,))
    gw_gdn_up = matmul("gdn_up_dw", y_gdn_n, dy_gdn, ta=True)

    dproj, dsmall_ssm, dcw_ssm, dcb_ssm, d_sdtb, d_salog, d_sdsk, d_snw = scan_bwd(
        "ssd_bwd", ssd_step, SSM_CHUNK, 1, ssd_rows, ssd_consts, [st_ssm, tails_ssm], dy_ssm_n, [BF16, F32], 1, {0: dproj})
    dproj, dsmall_gdn, dcw_gdn, d_gdtb, d_galog, d_gnw = scan_bwd(
        "gdn_bwd", gdn_step, gdn_rows_per_step, 1, gdn_rows, gdn_consts, [st_gdn, tails_gdn], dy_gdn_n, [BF16, F32], 1,
        {0: dproj}, kept=inv_gdn)
    tail = n_al - a_small
    dproj = rowmap("small_sum", lambda a, b: jnp.concatenate([a + b, jnp.zeros((a.shape[0], tail - LANES), F32)], axis=1),
                   [dsmall_ssm, dsmall_gdn], [], [(tail, BF16, dproj, a_small)])[0]
    mix32, mix16 = reduce_on_chip("mix", [quarters_rows(gw_ssm_up), quarters_rows(gw_gdn_up), quarters_rows(gw_out)], [0] * 3)
    mlp32, mlp16 = reduce_on_chip("mlp", mlp_grads, [0, 0], mlp_from_sibling)
    rest32, rest16 = mix32 + mlp32, mix16 + mlp16
    gw_in_al, rest_chips = matmul("in_proj_dw", dproj, h1, ta=True, comm=exchange_quarters_ici(rest16))
    in32, in16 = reduce_on_chip("in", [quarters_rows(from_aligned(gw_in_al))], [1])
    def premix_bwd(dh1_, x_, dxa_, w_, sc_, sh_):
        _, vjp = jax.vjp(_premix, x_, w_, sc_, sh_)
        dx, dw, dsc, dsh = vjp(dh1_)
        return dx + dxa_, dw, dsc, dsh

    (grad_x, d_norm_mix_pre, dsc1, dsh1), in_chips = matmul(
        "in_proj_dx", dproj, wb_in, comm=exchange_quarters_ici(in16), epi=premix_bwd, extras=[xt, dxa],
        consts=[norm_mix_pre, sc1, sh1], accs=[((1, d), F32)] * 3)

    dmod_all = gather_flat("ag_dmod", jnp.concatenate([dsh1, dsc1, dg1, dsh2, dsc2, dg2], axis=1).reshape(-1))
    dmod_q = lax.dynamic_slice_in_dim(dmod_all, quarter * n_ada, n_ada, axis=1)
    gw_ada, gb_ada = _whole(
        "ada_bwd", lambda ca, dq_, da_: (_bdot(_silu(ca), dq_, TN), jnp.sum(da_, axis=0, keepdims=True)),
        [c_all, dmod_q, dmod_all], [((d, n_ada), F32), ((1, 4 * n_ada), F32)])

    partial = [d_norm_mix_pre, d_norm_mix_post, dcw_ssm, dcb_ssm, d_sdtb[:, LANE_DT:LANE_DT + hs], d_salog[:, LANE_DT:LANE_DT + hs],
               d_sdsk[:, LANE_DT:LANE_DT + hs], d_snw, dcw_gdn, d_gdtb[:, LANE_A:LANE_A + hv], d_galog[:, LANE_A:LANE_A + hv], d_gnw,
               d_norm_mlp_pre, d_norm_mlp_post, loss_part]
    sizes = [p.size for p in partial]
    stacked = gather_flat("ag_grads", jnp.concatenate([p.reshape(-1) for p in partial]))
    summed = _whole("small_sum8", lambda s: jnp.sum(s, axis=0, keepdims=True), [stacked], [((1, stacked.shape[1]), F32)])[0][0]
    offs = [0]
    for s in sizes:
        offs.append(offs[-1] + s)
    red = [summed[offs[i]:offs[i + 1]] for i in range(len(sizes))]
    loss = red[-1][0]
    my_cols = lambda full: lax.dynamic_slice_in_dim(full.reshape(CONV_K, -1), quarter * (n_cw // CONV_K), n_cw // CONV_K, axis=1)
    small_grads = {
        "b_ada": gb_ada, "norm_mix_pre": red[0], "norm_mix_post": red[1], "ssm_conv_w": my_cols(red[2]), "ssm_conv_b": red[3],
        "ssm_dt_bias": red[4], "ssm_A_log": red[5], "ssm_D": red[6], "ssm_norm_w": red[7], "gdn_conv_w": my_cols(red[8]),
        "gdn_dt_bias": red[9], "gdn_A_log": red[10], "gdn_norm_w": red[11], "norm_mlp_pre": red[12], "norm_mlp_post": red[13]}

    big_names = ["w_in", "w_ssm_up", "w_gdn_up", "w_out", "w_mlp_up", "w_mlp_down"]
    big_grads = dict(zip(big_names, reduce_across_chips(in32 + rest32, in_chips + rest_chips)))

    names = ['w_ada', 'b_ada', 'norm_mix_pre', 'norm_mix_post', 'w_in', 'ssm_conv_w', 'ssm_conv_b', 'ssm_dt_bias', 'ssm_A_log', 'ssm_D',
             'ssm_norm_w', 'gdn_conv_w', 'gdn_dt_bias', 'gdn_A_log', 'gdn_norm_w', 'w_ssm_up', 'w_gdn_up', 'w_out', 'norm_mlp_pre',
             'norm_mlp_post', 'w_mlp_up', 'w_mlp_down']
    grad, delta, new_m, new_v = {}, {}, {}, {}
    for n, (mine, other) in big_grads.items():
        view, axis = (transposed, 1) if n == "w_in" else ((lambda a: a), 0)
        res = adamw_halves("adamw_" + n, view(args[n]), mine, other, view(args["m_" + n]), view(args["v_" + n]), axis)
        grad[n], delta[n], new_m[n], new_v[n] = [view(a) for a in res]
    grad["w_ada"] = gw_ada.reshape(w_ada.shape)
    delta["w_ada"], new_m["w_ada"], new_v["w_ada"] = adamw("adamw_w_ada", w_ada, gw_ada, m_w_ada, v_w_ada)
    small_names = [n for n in names if n not in grad]
    flat = lambda pre: jnp.concatenate([args[pre + n].reshape(-1) for n in small_names]).reshape(1, 1, -1)
    g_flat = jnp.concatenate([small_grads[n].reshape(-1) for n in small_names]).reshape(1, -1)
    dl, nm, nv = adamw("adamw_small", flat(""), g_flat, flat("m_"), flat("v_"))
    off = 0
    for n in small_names:
        shape = args[n].shape
        size = args[n].size
        grad[n], delta[n], new_m[n], new_v[n] = [a.reshape(-1)[off:off + size].reshape(shape) for a in (g_flat, dl, nm, nv)]
        off += size

    return (loss, grad_x.reshape(x.shape), *[grad[n] for n in names], *[delta[n] for n in names],
            *[new_m[n] for n in names], *[new_v[n] for n in names])
```

```python
import functools

import jax
import jax.numpy as jnp
from jax import lax
from jax.experimental import pallas as pl
from jax.experimental.pallas import tpu as pltpu

F32 = jnp.float32
BF16 = jnp.bfloat16
MESH = pl.DeviceIdType.MESH

EPS = 1e-6
SSM_HEAD_DIM = 64
SSM_HEADS_PER_GROUP = 4
SSM_D_STATE = 128
SSM_CHUNK = 128
GDN_HEAD = 128
GDN_CHUNK = 64
CONV_K = 4
TAIL_ROWS = 8
LANE_DT, LANE_B, LANE_A = 0, 32, 48
ADAM_LR, ADAM_B1, ADAM_B2, ADAM_EPS, ADAM_WD, ADAM_STEP = 0.001, 0.9, 0.999, 1e-08, 0.01, 10

VMEM_LIMIT_BYTES = 56 * 1024 * 1024
LANES = 128
N_DEV = 8

NN = (((1,), (0,)), ((), ()))
NT = (((1,), (1,)), ((), ()))
TN = (((0,), (0,)), ((), ()))


BNN = (((2,), (1,)), ((0,), (0,)))
BNT = (((2,), (2,)), ((0,), (0,)))
BTN = (((1,), (1,)), ((0,), (0,)))
_KIND = {NN: ("NN", 0), NT: ("NT", 0), TN: ("TN", 0), BNN: ("NN", 1), BNT: ("NT", 1), BTN: ("TN", 1)}
_DIMS = {"NN": (NN, BNN), "NT": (NT, BNT), "TN": (TN, BTN)}


def _dg(a, b, dims):
    return lax.dot_general(a, b, dims, preferred_element_type=F32)


def _raw_bf16(a, b, dims):
    return _dg(a.astype(BF16), b.astype(BF16), dims)


def _raw_bf16x3(a, b, dims):
    ah, bh = a.astype(BF16), b.astype(BF16)
    al, bl = (a - ah.astype(F32)).astype(BF16), (b - bh.astype(F32)).astype(BF16)
    return _dg(ah, bh, dims) + (_dg(ah, bl, dims) + _dg(al, bh, dims))


def _make_dot(raw):
    @functools.partial(jax.custom_vjp, nondiff_argnums=(2,))
    def dot(a, b, dims):
        return raw(a, b, dims)

    def fwd(a, b, dims):
        return raw(a, b, dims), (a, b)

    def bwd(dims, res, ct):
        a, b = res
        kind, batched = _KIND[dims]
        d = lambda k: _DIMS[k][batched]
        if kind == "NN":
            da, db = raw(ct, b, d("NT")), raw(a, ct, d("TN"))
        elif kind == "NT":
            da, db = raw(ct, b, d("NN")), raw(ct, a, d("TN"))
        else:
            da, db = raw(b, ct, d("NT")), raw(a, ct, d("NN"))
        return da.astype(a.dtype), db.astype(b.dtype)

    dot.defvjp(fwd, bwd)
    return lambda a, b, dims=NN: dot(a, b, dims)


_bdot = _make_dot(_raw_bf16)
_hdot = _make_dot(_raw_bf16x3)


def _mask_dot(mask, x, dims):
    m = mask.astype(BF16)
    hi = x.astype(BF16)
    r = x - hi.astype(F32)
    mid = r.astype(BF16)
    lo = (r - mid.astype(F32)).astype(BF16)
    return sum(_dg(m, p, dims) for p in (hi, mid, lo))


def _sigmoid(x):
    return 0.5 * jnp.tanh(0.5 * x) + 0.5


def _silu(x):
    return x * _sigmoid(x)


def _softplus(x):
    return jnp.maximum(x, 0.0) + jnp.log(1.0 + jnp.exp(-jnp.abs(x)))


def _rms(x, w):
    return x * lax.rsqrt(jnp.mean(x * x, axis=-1, keepdims=True) + EPS) * w


def _lane_col(m, idx):
    lane = lax.broadcasted_iota(jnp.int32, m.shape, 1)
    return jnp.sum(jnp.where(lane == idx, m, 0.0), axis=1, keepdims=True)


def _tril(n, strict=False, seg=None):
    r = lax.broadcasted_iota(jnp.int32, (n, n), 0)
    c = lax.broadcasted_iota(jnp.int32, (n, n), 1)
    low = (r > c) if strict else (r >= c)
    if seg is None or seg >= n:
        return low
    shift = seg.bit_length() - 1
    return jnp.logical_and(low, (r >> shift) == (c >> shift))


@functools.partial(jax.custom_vjp, nondiff_argnums=(1,))
def _cumsum_rows(x, seg):
    return _mask_dot(_tril(x.shape[0], seg=seg), x, NN)


def _cumsum_rows_fwd(x, seg):
    return _cumsum_rows(x, seg), None


def _cumsum_rows_bwd(seg, _, ct):
    return (_mask_dot(_tril(ct.shape[0], seg=seg), ct, TN),)


_cumsum_rows.defvjp(_cumsum_rows_fwd, _cumsum_rows_bwd)


def _head_rows(m_t, idx):
    sub = lax.broadcasted_iota(jnp.int32, m_t.shape, 0)
    return jnp.sum(jnp.where(sub == idx, m_t, 0.0), axis=0, keepdims=True)


def _params(sem):
    return pltpu.CompilerParams(dimension_semantics=sem, vmem_limit_bytes=VMEM_LIMIT_BYTES)


def _into_plumbing(outs, first_input):
    arrays, aliases = [], {}
    for k, o in enumerate(outs):
        if len(o) > 4 and not isinstance(o[4], jax.ShapeDtypeStruct):
            aliases[first_input + len(arrays)] = k
            arrays.append(o[4])
    return arrays, aliases


def blockmap(name, fn, grid, ins, outs, accs=(), scalars=None):
    n_in, n_out, n_acc = len(ins), len(outs), len(accs)
    n_grid = len(grid)
    n_pre = 0 if scalars is None else 1
    into_arrays, aliases = _into_plumbing(outs, n_pre + n_in)
    n_into = len(into_arrays)

    def body(*refs):
        refs = refs[n_pre:n_pre + n_in] + refs[n_pre + n_in + n_into:]
        vals = fn(*[r[...] for r in refs[:n_in]])
        if not isinstance(vals, (tuple, list)):
            vals = (vals,)
        for r, v in zip(refs[n_in:n_in + n_out], vals[:n_out]):
            r[...] = v.astype(r.dtype)
        if n_acc:
            first = functools.reduce(jnp.logical_and, [pl.program_id(a) == 0 for a in range(n_grid)])
            acc_refs = refs[n_in + n_out:]

            @pl.when(first)
            def _():
                for r in acc_refs:
                    r[...] = jnp.zeros(r.shape, r.dtype)

            for r, v in zip(acc_refs, vals[n_out:]):
                r[...] += v.astype(r.dtype)

    zeros = lambda nd: (lambda *_: (0,) * nd)
    in_specs = [pl.BlockSpec(b, im) for _, b, im in ins] + [pl.BlockSpec(memory_space=pl.ANY)] * n_into
    out_specs = [pl.BlockSpec(o[2], o[3]) for o in outs] + [pl.BlockSpec(s, zeros(len(s))) for s, _ in accs]
    out_shape = [jax.ShapeDtypeStruct(o[0], o[1]) for o in outs] + [jax.ShapeDtypeStruct(s, d) for s, d in accs]
    cparams = _params(("arbitrary",) * n_grid if n_acc else ("parallel",) * n_grid)
    arrays = [a for a, _, _ in ins] + into_arrays
    if scalars is None:
        return pl.pallas_call(body, name=name, grid=grid, in_specs=in_specs, out_specs=out_specs, out_shape=out_shape,
                              input_output_aliases=aliases, compiler_params=cparams)(*arrays)
    spec = pltpu.PrefetchScalarGridSpec(num_scalar_prefetch=1, grid=grid, in_specs=in_specs, out_specs=out_specs)
    return pl.pallas_call(body, name=name, grid_spec=spec, out_shape=out_shape, input_output_aliases=aliases,
                          compiler_params=cparams)(scalars, *arrays)


def rowmap(name, fn, rows, consts, outs, accs=(), rb=512):
    norm = [(r, r.shape[1], 0) if not isinstance(r, tuple) else (r[0], r[1], r[2] // r[1]) for r in rows]
    assert all(not isinstance(r, tuple) or r[2] % r[1] == 0 for r in rows)
    t = norm[0][0].shape[0]
    rb = min(rb, t)
    ins = [(a, (rb, n), (lambda i, cb=cb: (i, cb))) for a, n, cb in norm]
    ins += [(cst, cst.shape, (lambda i, nd=cst.ndim: (0,) * nd)) for cst in consts]
    o = []
    for out in outs:
        if len(out) == 2:
            o.append(((t, out[0]), out[1], (rb, out[0]), lambda i: (i, 0)))
        else:
            n, d, into, off = out
            assert off % n == 0 and into.dtype == d
            o.append((into.shape, d, (rb, n), (lambda i, cb=off // n: (i, cb)), into))
    return blockmap(name, fn, (t // rb,), ins, o, accs)


MM_TILE_M, MM_TILE_N, MM_TILE_K = 1024, 1024, 2048


def _tile(dim, cap):
    if dim <= cap:
        return dim
    best = max(t for t in range(LANES, cap + 1, LANES) if dim % t == 0)
    return best


EPI_ROWS = 256


def matmul(name, a, b, ta=False, tb=False, out_dtypes=(F32,), epi=None, extras=(), comm=None, b_quarters=False, out_quarters=False,
           consts=(), accs=()):
    (k_dim, m_dim) = a.shape if ta else a.shape[::-1]
    if b_quarters:
        quarter = b.shape[2]
        b_rows, b_cols = b.shape[1], 4 * quarter
    else:
        b_rows, b_cols = b.shape
    n_dim = b_rows if tb else b_cols
    assert (b_cols if tb else b_rows) == k_dim, (name, a.shape, b.shape)
    tm, tn, tk = _tile(m_dim, MM_TILE_M), _tile(n_dim, MM_TILE_N), _tile(k_dim, MM_TILE_K)
    if b_quarters and tb:
        tk = quarter
    elif b_quarters or out_quarters:
        tn = quarter if b_quarters else n_dim // 4
    grid = (m_dim // tm, n_dim // tn, k_dim // tk)
    k_steps = grid[2]
    n_extra, n_out, n_const, n_acc = len(extras), len(out_dtypes), len(consts), len(accs)
    n_cin, n_cout = (len(comm.operands), len(comm.out_shapes)) if comm else (0, 0)
    dims = (((0 if ta else 1,), (1 if tb else 0,)), ((), ()))
    row_wise = bool(consts or accs)
    n_in = 2 + n_extra + n_const + n_cin

    def body(*refs):
        ins, outs, scratch = refs[:n_in], refs[n_in:][:n_out + n_acc + n_cout], refs[n_in + n_out + n_acc + n_cout:]
        extra_refs, const_refs = ins[2:2 + n_extra], ins[2 + n_extra:2 + n_extra + n_const]
        out_refs, acc_refs = outs[:n_out], outs[n_out:n_out + n_acc]
        ids = [pl.program_id(ax) for ax in range(3)]
        step = (ids[0] * grid[1] + ids[1]) * grid[2] + ids[2]
        if comm:
            comm_refs = (ins[2 + n_extra + n_const:], outs[n_out + n_acc:], scratch[-2], scratch[-1])

            @pl.when(step == 0)
            def _():
                comm.start(*comm_refs)

            @pl.when(step == (grid[0] * grid[1] * grid[2]) // 2)
            def _():
                comm.middle(*comm_refs)

        if n_acc:
            @pl.when(step == 0)
            def _():
                for r in acc_refs:
                    r[...] = jnp.zeros(r.shape, r.dtype)

        def finish(acc):
            pieces = [slice(r0, r0 + EPI_ROWS) for r0 in range(0, tm, EPI_ROWS)] if row_wise and tm % EPI_ROWS == 0 else [slice(None)]
            for rows in pieces:
                args = [acc[rows]] + [r[rows] for r in extra_refs] + [r[...] for r in const_refs]
                vals = (args[0],) if epi is None else epi(*args)
                if not isinstance(vals, (tuple, list)):
                    vals = (vals,)
                for r, v in zip(out_refs, vals[:n_out]):
                    r[rows] = v.astype(r.dtype)
                for r, v in zip(acc_refs, vals[n_out:]):
                    r[...] += v.astype(r.dtype)

        prod = lax.dot_general(ins[0][...].astype(BF16), ins[1][...].astype(BF16), dims, preferred_element_type=F32)
        if k_steps == 1:
            finish(prod)
        else:
            acc_ref = scratch[0]

            @pl.when(ids[2] == 0)
            def _():
                acc_ref[...] = jnp.zeros(acc_ref.shape, F32)

            acc_ref[...] += prod

            @pl.when(ids[2] == k_steps - 1)
            def _():
                finish(acc_ref)

        if comm:
            @pl.when(functools.reduce(jnp.logical_and, [i == g - 1 for i, g in zip(ids, grid)]))
            def _():
                comm.finish(*comm_refs)

    a_spec = pl.BlockSpec((tk, tm), lambda i, j, k: (k, i)) if ta else pl.BlockSpec((tm, tk), lambda i, j, k: (i, k))
    if b_quarters:
        b_spec = pl.BlockSpec((None, tn, tk), lambda i, j, k: (k, j, 0)) if tb else pl.BlockSpec((None, tk, tn), lambda i, j, k: (j, k, 0))
    else:
        b_spec = pl.BlockSpec((tn, tk), lambda i, j, k: (j, k)) if tb else pl.BlockSpec((tk, tn), lambda i, j, k: (k, j))
    mn_spec = pl.BlockSpec((tm, tn), lambda i, j, k: (i, j))
    out_spec = pl.BlockSpec((None, tm, tn), lambda i, j, k: (j, i, 0)) if out_quarters else mn_spec
    out_dims = (4, m_dim, tn) if out_quarters else (m_dim, n_dim)
    scratch_shapes = [] if k_steps == 1 else [pltpu.VMEM((tm, tn), F32)]
    if comm:
        scratch_shapes += [pltpu.SemaphoreType.DMA(comm.sem_shape), pltpu.SemaphoreType.DMA(comm.sem_shape)]
    whole = lambda shape: pl.BlockSpec(shape, lambda i, j, k, nd=len(shape): (0,) * nd)
    res = pl.pallas_call(
        body, name=name, grid=grid,
        in_specs=[a_spec, b_spec] + [mn_spec] * n_extra + [whole(c.shape) for c in consts] + [ANY] * n_cin,
        out_specs=[out_spec] * n_out + [whole(s) for s, _ in accs] + [ANY] * n_cout,
        out_shape=[jax.ShapeDtypeStruct(out_dims, d) for d in out_dtypes] + [jax.ShapeDtypeStruct(s, d) for s, d in accs]
        + (comm.out_shapes if comm else []),
        scratch_shapes=scratch_shapes,
        compiler_params=_params(("arbitrary",) * 3 if comm or accs else ("parallel", "parallel", "arbitrary")),
    )(a, b, *extras, *consts, *(comm.operands if comm else []))
    main = list(res[:n_out + n_acc]) if n_out + n_acc > 1 else res[0]
    return (main, list(res[n_out + n_acc:])) if comm else main


def ssd_step(g0, state, tail, zxbc, small, cw, cb, p_dtb, p_alog, p_dsk, nw):
    d_in, gn = state.shape[0] * 2 * LANES, state.shape[0] * LANES
    z = zxbc[:, :d_in]
    act, new_tail = _conv_silu_carried(tail, zxbc[:, d_in:], cw, cb)
    xs, bm, cm = act[:, :d_in], act[:, d_in:d_in + gn], act[:, d_in + gn:]
    hb, n = state.shape[0], xs.shape[0]
    n_pair, n_head = 2 * hb, 4 * hb
    causal = _tril(n)
    dt_all = _softplus(small + p_dtb)
    a_all = dt_all * (-jnp.exp(p_alog))
    acum_all = _cumsum_rows(a_all, n)
    acum_t = acum_all.T
    lane0 = LANE_DT + SSM_HEADS_PER_GROUP * g0
    sub = lax.broadcasted_iota(jnp.int32, acum_t.shape, 0)
    heads = range(n_head)
    acum = jnp.stack([_lane_col(acum_all, lane0 + i) for i in heads])
    acum_row = jnp.stack([jnp.sum(jnp.where(sub == lane0 + i, acum_t, 0.0), axis=0, keepdims=True) for i in heads])
    dt = jnp.stack([_lane_col(dt_all, lane0 + i) for i in heads])
    dsk = jnp.stack([_lane_col(p_dsk, lane0 + i) for i in heads])
    decay = jnp.exp(jnp.where(causal, acum - acum_row, -jnp.inf))
    a_last = acum[:, n - 1:n, :]

    def split(a):
        return [a[:, i * LANES:(i + 1) * LANES] for i in range(a.shape[1] // LANES)]

    def pairs(a, axis=2):
        even = jnp.stack([a[2 * p] for p in range(n_pair)])
        odd = jnp.stack([a[2 * p + 1] for p in range(n_pair)])
        shape = (n_pair, LANES, LANES) if axis == 1 else (n_pair, a.shape[1], LANES)
        return jnp.where(lax.broadcasted_iota(jnp.int32, shape, axis) < SSM_HEAD_DIM, even, odd)

    bms, cms = split(bm), split(cm)
    cb = _bdot(jnp.stack(cms), jnp.stack(bms), BNT)
    cbd = jnp.stack([cb[i // SSM_HEADS_PER_GROUP] for i in heads]) * decay
    xp = jnp.stack(split(xs))
    xdt = xp * pairs(dt)
    yd = _bdot(cbd, jnp.stack([xdt[i // 2] for i in heads]), BNN)
    lane = lax.broadcasted_iota(jnp.int32, (n_pair, n, LANES), 2)
    y_diag = jnp.where(lane < SSM_HEAD_DIM, jnp.stack([yd[2 * p] for p in range(n_pair)]), jnp.stack([yd[2 * p + 1] for p in range(n_pair)]))
    st = state.reshape(n_pair, LANES, LANES)
    cm2 = jnp.stack([cms[p // 2] for p in range(n_pair)])
    bm2 = jnp.stack([bms[p // 2] for p in range(n_pair)])
    y_off = _bdot(cm2, st, BNT) * pairs(jnp.exp(acum))
    new = st * pairs(jnp.exp(a_last), axis=1) + _bdot(xdt * pairs(jnp.exp(a_last - acum)), bm2, BTN)
    y = y_diag + y_off + pairs(dsk) * xp
    y = jnp.concatenate([y[p] for p in range(n_pair)], axis=1) * _silu(z)
    wide = 2 * LANES
    y = jnp.concatenate([_rms(y[:, i * wide:(i + 1) * wide], nw[i]) for i in range(hb)], axis=1)
    return new.reshape(state.shape), new_tail, y


@functools.partial(jax.custom_vjp, nondiff_argnums=(1,))
def _unit_lower_inverse(a, seg):
    n = a.shape[-1]
    r = lax.broadcasted_iota(jnp.int32, (n, n), 0)
    c = lax.broadcasted_iota(jnp.int32, (n, n), 1)
    shift = min(INVERSE_BASE, seg).bit_length() - 1
    power = jnp.where((r >> shift) == (c >> shift), a, 0.0)
    inv = (r == c).astype(F32) - power
    span = 2
    while span < (1 << shift):
        power = _hdot(power, power, BNN)
        inv = inv + _hdot(inv, power, BNN)
        span *= 2
    while (1 << shift) < seg:
        below = jnp.logical_and((r >> (shift + 1)) == (c >> (shift + 1)), (r >> shift) != (c >> shift))
        inv = inv - _hdot(inv, _hdot(jnp.where(below, a, 0.0), inv, BNN), BNN)
        shift += 1
    return inv


def _unit_lower_inverse_fwd(a, seg):
    inv = _unit_lower_inverse(a, seg)
    return inv, inv


def _unit_lower_inverse_bwd(seg, inv, ct):
    return (-_hdot(_hdot(inv, ct, BTN), inv, BNT),)


_unit_lower_inverse.defvjp(_unit_lower_inverse_fwd, _unit_lower_inverse_bwd)


@jax.custom_vjp
def _known_inverse(a, inv):
    return inv


def _known_inverse_fwd(a, inv):
    return inv, inv


def _known_inverse_bwd(inv, ct):
    return _unit_lower_inverse_bwd(None, inv, ct)[0], jnp.zeros_like(inv)


_known_inverse.defvjp(_known_inverse_fwd, _known_inverse_bwd)


@functools.partial(jax.custom_vjp, nondiff_argnums=(1,))
def _rotate_rows(x, k):
    return x if k == 0 else pltpu.roll(x, k % x.shape[0], 0)


def _rotate_rows_fwd(x, k):
    return _rotate_rows(x, k), None


def _rotate_rows_bwd(k, _, ct):
    return (_rotate_rows(ct, -k),)


_rotate_rows.defvjp(_rotate_rows_fwd, _rotate_rows_bwd)


def _conv_silu_carried(tail, x, cw, cb=0.0):
    n = x.shape[0]
    ext = jnp.concatenate([tail, x], axis=0)
    pre = cb + sum(cw[j:j + 1, :] * _rotate_rows(ext, CONV_K - 1 - j)[TAIL_ROWS:] for j in range(CONV_K))
    return _silu(pre), x[n - TAIL_ROWS:]


def _l2norm(x):
    return x * lax.rsqrt(jnp.sum(x * x, axis=-1, keepdims=True) + EPS)


def gdn_step(hq0, state, tail, qkvz, small, cw, p_dtb, p_alog, nw, keep=False, kept=None):
    n, chunk = qkvz.shape[0], GDN_CHUNK
    hb = state.shape[0]
    nb = 2 * hb
    cur = state.reshape(nb, LANES, LANES)
    conv_cols = 4 * hb * LANES
    act, new_tail = _conv_silu_carried(tail, qkvz[:, :conv_cols], cw)
    q, k, v = act[:, :hb * LANES], act[:, hb * LANES:2 * hb * LANES], act[:, 2 * hb * LANES:]
    z = qkvz[:, conv_cols:]
    causal, strict = _tril(n, seg=chunk), _tril(n, True, seg=chunk)
    beta_all = _sigmoid(small)
    g_all = -jnp.exp(p_alog) * _softplus(small + p_dtb)
    gcum_all = _cumsum_rows(g_all, chunk)
    gcum_t = gcum_all.T
    split = lambda a: [a[:, i * LANES:(i + 1) * LANES] for i in range(a.shape[1] // LANES)]
    per_value_head = lambda a: jnp.stack([a[i // 2] for i in range(nb)])
    qh, kh = _l2norm(jnp.stack(split(q))) * (GDN_HEAD ** -0.5), _l2norm(jnp.stack(split(k)))
    q2, k2 = per_value_head(qh), per_value_head(kh)
    v2, z2 = jnp.stack(split(v)), jnp.stack(split(z))
    gcum = jnp.stack([_lane_col(gcum_all, LANE_A + 2 * hq0 + i) for i in range(nb)])
    gcum_row = jnp.stack([_head_rows(gcum_t, LANE_A + 2 * hq0 + i) for i in range(nb)])
    beta = jnp.stack([_lane_col(beta_all, LANE_B + 2 * hq0 + i) for i in range(nb)])
    dmat = jnp.exp(jnp.where(causal, gcum - gcum_row, -jnp.inf))
    a_low = jnp.where(strict, beta * per_value_head(_bdot(kh, kh, BNT)) * dmat, 0.0)
    inv = _unit_lower_inverse(a_low, chunk) if kept is None else _known_inverse(a_low, kept)
    egc = jnp.exp(gcum)
    u = _hdot(inv, v2 * beta, BNN)
    w = _hdot(inv, k2 * (beta * egc), BNN)
    q_dec = q2 * egc
    v_new, o_state = [], []
    for s in range(n // chunk):
        rows = slice(s * chunk, (s + 1) * chunk)
        v_new.append(u[:, rows] - _bdot(w[:, rows], cur, BNN))
        o_state.append(_bdot(q_dec[:, rows], cur, BNN))
        g_last = gcum[:, (s + 1) * chunk - 1:(s + 1) * chunk, :]
        k_dec = k2[:, rows] * jnp.exp(g_last - gcum[:, rows])
        cur = cur * jnp.exp(g_last) + _bdot(k_dec, v_new[-1], BTN)
    o = jnp.concatenate(o_state, axis=1) + _bdot(per_value_head(_bdot(qh, kh, BNT)) * dmat, jnp.concatenate(v_new, axis=1), BNN)
    out = _rms(o, nw) * _silu(z2)
    res = (cur.reshape(state.shape), new_tail, jnp.concatenate([out[i] for i in range(nb)], axis=1))
    return res + (inv,) if keep else res


STATE_SHAPE = (2, LANES, LANES)
GDN_CHUNKS_PER_STEP = 2
INVERSE_BASE = 16


def _scan_specs(rows, consts, chunk, chunk_of, hb):
    specs = []
    for _, n, off, per_group in rows:
        if per_group:
            assert off % (n * hb) == 0
            specs.append(pl.BlockSpec((chunk, n * hb), lambda c, g, cb=off // (n * hb): (chunk_of(c), cb + g)))
        else:
            assert off % n == 0
            specs.append(pl.BlockSpec((chunk, n), lambda c, g, cb=off // n: (chunk_of(c), cb)))
    for arr, per_group in consts:
        if per_group:
            specs.append(pl.BlockSpec((hb, 1, arr.shape[2]), lambda c, g: (g, 0, 0)))
        else:
            specs.append(pl.BlockSpec(arr.shape, lambda c, g, nd=arr.ndim: (0,) * nd))
    return specs


def scan_fwd(name, step, chunk, n_grp, rows, consts, out_cols, hb, keep=None, state_shape=None, tail_cols=None):
    t = rows[0][0].shape[0]
    nc = t // chunk
    n_rows, n_consts = len(rows), len(consts)
    state_shape = state_shape or (hb,) + STATE_SHAPE
    carried = [state_shape] + ([(TAIL_ROWS, tail_cols)] if tail_cols else [])
    n_car = len(carried)

    def body(*refs):
        row_refs, const_refs = refs[:n_rows], refs[n_rows:n_rows + n_consts]
        y_ref = refs[n_rows + n_consts]
        saved_refs = refs[n_rows + n_consts + 1:n_rows + n_consts + 1 + n_car]
        scratch = refs[-n_car:]
        c, g = pl.program_id(0), pl.program_id(1)

        @pl.when(c == 0)
        def _():
            for s, shape in zip(scratch, carried):
                s[g] = jnp.zeros(shape, F32)

        cur = [s[g] for s in scratch]
        for r, v in zip(saved_refs, cur):
            r[...] = v
        vals = [r[...] for r in row_refs] + [r[...] for r in const_refs]
        res = step(g * hb, *cur, *vals) if keep is None else step(g * hb, *cur, *vals, keep=True)
        for s, v in zip(scratch, res[:n_car]):
            s[g] = v
        y_ref[...] = res[n_car].astype(y_ref.dtype)
        if keep is not None:
            refs[n_rows + n_consts + 1 + n_car][...] = res[n_car + 1]

    lead = (nc, n_grp // hb)
    out_specs = [pl.BlockSpec((chunk, out_cols * hb), lambda c, g: (c, g))]
    out_shape = [jax.ShapeDtypeStruct((t, n_grp * out_cols), BF16)]
    for shape in carried + ([keep] if keep is not None else []):
        out_specs.append(pl.BlockSpec((None, None) + shape, lambda c, g, nd=len(shape): (c, g) + (0,) * nd))
        out_shape.append(jax.ShapeDtypeStruct(lead + shape, F32))
    return pl.pallas_call(
        body, name=name, grid=lead,
        in_specs=_scan_specs(rows, consts, chunk, lambda c: c, hb),
        out_specs=out_specs, out_shape=out_shape,
        scratch_shapes=[pltpu.VMEM((n_grp // hb,) + shape, F32) for shape in carried],
        compiler_params=_params(("arbitrary", "arbitrary")),
    )(*[r[0] for r in rows], *[c[0] for c in consts])


def scan_bwd(name, step, chunk, n_grp, rows, consts, saved, dy, row_dtypes, hb, into, kept=None):
    t = rows[0][0].shape[0]
    nc = t // chunk
    n_rows, n_consts, n_car = len(rows), len(consts), len(saved)
    carried = [s.shape[2:] for s in saved]
    out_cols = dy.shape[1] // n_grp
    n_alias = sum(not isinstance(v, jax.ShapeDtypeStruct) for v in into.values())
    n_kept = 0 if kept is None else 1
    n_in = n_rows + n_consts + n_car + 1 + n_kept + n_alias

    def body(*refs):
        row_refs, const_refs = refs[:n_rows], refs[n_rows:n_rows + n_consts]
        saved_refs = refs[n_rows + n_consts:n_rows + n_consts + n_car]
        dy_ref = refs[n_rows + n_consts + n_car]
        outs = refs[n_in:-n_car]
        scratch = refs[-n_car:]
        c, g = pl.program_id(0), pl.program_id(1)

        @pl.when(c == 0)
        def _():
            for s, shape in zip(scratch, carried):
                s[g] = jnp.zeros(shape, F32)

        @pl.when(jnp.logical_and(c == 0, g == 0))
        def _():
            for r in outs[n_rows:]:
                r[...] = jnp.zeros(r.shape, r.dtype)

        f = functools.partial(step, g * hb) if kept is None else functools.partial(step, g * hb, kept=refs[n_rows + n_consts + n_car + 1][...])
        _, vjp = jax.vjp(f, *[r[...] for r in saved_refs], *[r[...] for r in row_refs], *[r[...] for r in const_refs])
        grads = vjp(tuple(s[g] for s in scratch) + (dy_ref[...].astype(F32),))
        for s, d in zip(scratch, grads[:n_car]):
            s[g] = d
        for (_, _, _, per_group), r, d in zip(rows, outs[:n_rows], grads[n_car:n_car + n_rows]):
            if per_group:
                r[...] = d.astype(r.dtype)
            else:
                @pl.when(g == 0)
                def _(r=r):
                    r[...] = jnp.zeros(r.shape, r.dtype)

                r[...] += d.astype(r.dtype)
        for (_, per_group), r, d in zip(consts, outs[n_rows:], grads[n_car + n_rows:]):
            if per_group:
                r[pl.ds(g * hb, hb)] += d
            else:
                r[...] += d

    rev = lambda c: nc - 1 - c
    out_specs, out_shape = [], []
    into_arrays, aliases = [], {}
    first_into = n_in - n_alias
    kept_arrays = [] if kept is None else [kept]
    by_step = lambda a: pl.BlockSpec((None, None) + a.shape[2:], lambda c, g, nd=a.ndim - 2: (rev(c), g) + (0,) * nd)
    for k, ((_, n, off, per_group), dt) in enumerate(zip(rows, row_dtypes)):
        if k in into:
            assert per_group and off % (n * hb) == 0 and into[k].dtype == dt
            out_specs.append(pl.BlockSpec((chunk, n * hb), lambda c, g, cb=off // (n * hb): (rev(c), cb + g)))
            out_shape.append(jax.ShapeDtypeStruct(into[k].shape, dt))
            if not isinstance(into[k], jax.ShapeDtypeStruct):
                aliases[first_into + len(into_arrays)] = k
                into_arrays.append(into[k])
        elif per_group:
            out_specs.append(pl.BlockSpec((chunk, n * hb), lambda c, g: (rev(c), g)))
            out_shape.append(jax.ShapeDtypeStruct((t, n_grp * n), dt))
        else:
            out_specs.append(pl.BlockSpec((chunk, n), lambda c, g: (rev(c), 0)))
            out_shape.append(jax.ShapeDtypeStruct((t, n), dt))
    for arr, _ in consts:
        out_specs.append(pl.BlockSpec(arr.shape, lambda c, g, nd=arr.ndim: (0,) * nd))
        out_shape.append(jax.ShapeDtypeStruct(arr.shape, F32))
    return pl.pallas_call(
        body, name=name, grid=(nc, n_grp // hb),
        in_specs=_scan_specs(rows, consts, chunk, rev, hb) + [by_step(s) for s in saved]
        + [pl.BlockSpec((chunk, out_cols * hb), lambda c, g: (rev(c), g))] + [by_step(k) for k in kept_arrays]
        + [pl.BlockSpec(memory_space=pl.ANY)] * len(into_arrays),
        out_specs=out_specs, out_shape=out_shape, input_output_aliases=aliases,
        scratch_shapes=[pltpu.VMEM((n_grp // hb,) + shape, F32) for shape in carried],
        compiler_params=_params(("arbitrary", "arbitrary")),
    )(*[r[0] for r in rows], *[c[0] for c in consts], *saved, dy, *kept_arrays, *into_arrays)


def _place():
    return lax.axis_index("x"), lax.axis_index("y"), lax.axis_index("c")


def _other_chips(x, y):
    return [(1 - x, y), (x, 1 - y), (1 - x, 1 - y)]


ANY = pl.BlockSpec(memory_space=pl.ANY)


def all_gather8(name, v):
    m_per, n = v.shape

    def body(x_ref, out_ref, send_sems, recv_sems, local_sem):
        x, y, c = _place()
        me, sibling = (x, y, c), (x, y, 1 - c)
        chips = _other_chips(x, y)

        def rows(px, py, pc):
            return out_ref.at[pl.ds((4 * px + 2 * py + pc) * m_per, m_per), :]

        def copy(k, block, to, src=None):
            return pltpu.make_async_remote_copy(
                src_ref=rows(*block) if src is None else src, dst_ref=rows(*block),
                send_sem=send_sems.at[k], recv_sem=recv_sems.at[k], device_id=to, device_id_type=MESH)

        mine = pltpu.make_async_copy(x_ref, rows(*me), local_sem)
        mine.start()
        first = [copy(0, me, sibling, src=x_ref)]
        first += [copy(1 + q, me, (*chip, c), src=x_ref) for q, chip in enumerate(chips)]
        for cp in first:
            cp.start()
        passed = [copy(4 + q, (*chip, c), sibling) for q, chip in enumerate(chips)]
        for q, chip in enumerate(chips):
            copy(1 + q, (*chip, c), me).wait_recv()
            passed[q].start()
        copy(0, sibling, me).wait_recv()
        for q, chip in enumerate(chips):
            copy(4 + q, (*chip, 1 - c), me).wait_recv()
        for cp in first + passed:
            cp.wait_send()
        mine.wait()

    return pl.pallas_call(
        body, name=name, out_shape=jax.ShapeDtypeStruct((N_DEV * m_per, n), v.dtype),
        in_specs=[pl.BlockSpec(memory_space=pltpu.VMEM)], out_specs=pl.BlockSpec(memory_space=pltpu.VMEM),
        scratch_shapes=[pltpu.SemaphoreType.DMA((7,)), pltpu.SemaphoreType.DMA((7,)), pltpu.SemaphoreType.DMA],
    )(v)


def gather_flat(name, vec):
    n = vec.shape[0]
    n_pad = -(-n // (8 * LANES)) * (8 * LANES)
    v = jnp.pad(vec, (0, n_pad - n)).reshape(8, n_pad // 8)
    return all_gather8(name, v).reshape(N_DEV, n_pad)[:, :n]


class Exchange:
    def __init__(self, operands, out_shapes, sem_shape, start, finish, middle=None):
        self.operands, self.out_shapes, self.sem_shape = list(operands), out_shapes, sem_shape
        self.start, self.middle, self.finish = start, middle or (lambda *refs: None), finish


def _start_all_wait_all(make_copies):
    def start(*refs):
        for cp in make_copies(*refs):
            cp.start()

    def finish(*refs):
        for cp in make_copies(*refs):
            cp.wait()

    return start, finish


def run_exchange(name, ex):
    n_in, n_out = len(ex.operands), len(ex.out_shapes)

    def body(*refs):
        ins, outs = refs[:n_in], refs[n_in:n_in + n_out]
        ex.start(ins, outs, *refs[n_in + n_out:])
        ex.middle(ins, outs, *refs[n_in + n_out:])
        ex.finish(ins, outs, *refs[n_in + n_out:])

    return pl.pallas_call(
        body, name=name, out_shape=ex.out_shapes, in_specs=[ANY] * n_in, out_specs=[ANY] * n_out,
        scratch_shapes=[pltpu.SemaphoreType.DMA(ex.sem_shape), pltpu.SemaphoreType.DMA(ex.sem_shape)],
    )(*ex.operands)


def _half(shape, axis, pc):
    h = shape[axis] // 2
    return (pl.ds(pc * h, h), slice(None)) if axis == 0 else (slice(None), pl.ds(pc * h, h))


def _half_shape(shape, axis):
    return tuple(s // 2 if a == axis else s for a, s in enumerate(shape))


def all_gather_shards(shards, axes):
    n_t = len(shards)
    n_sem = 12

    def copies(ins, outs, send_sems, recv_sems):
        x, y, c = _place()
        me, sibling, x_nbr, y_nbr = (x, y, c), (x, y, 1 - c), (1 - x, y, c), (x, 1 - y, c)
        own, of_x, of_y, of_diag = 2 * x + y, 2 * (1 - x) + y, 2 * x + 1 - y, 2 * (1 - x) + 1 - y

        def copy(t, k, quarter, pc, piece, to, from_input=False):
            axis = axes[t]
            h = ins[t].shape[axis] // 4
            cut = pl.ds((2 * pc + piece) * h, h)
            part = (cut, slice(None)) if axis == 0 else (slice(None), cut)
            dst = outs[t].at[(quarter,) + part]
            return pltpu.make_async_remote_copy(
                src_ref=ins[t].at[part] if from_input else dst, dst_ref=dst,
                send_sem=send_sems.at[t, k], recv_sem=recv_sems.at[t, k], device_id=to, device_id_type=MESH)

        stages = []
        for t in range(n_t):
            direct = [copy(t, 0, own, c, 0, x_nbr, True), copy(t, 2, own, c, 1, y_nbr, True),
                      copy(t, 1, own, c, 1, x_nbr, True), copy(t, 3, own, c, 0, y_nbr, True)]
            landing = [
                (copy(t, 0, of_x, c, 0, me), [copy(t, 4, of_x, c, 0, y_nbr), copy(t, 6, of_x, c, 0, sibling)]),
                (copy(t, 2, of_y, c, 1, me), [copy(t, 5, of_y, c, 1, x_nbr), copy(t, 8, of_y, c, 1, sibling)]),
                (copy(t, 1, of_x, c, 1, me), [copy(t, 7, of_x, c, 1, sibling)]),
                (copy(t, 3, of_y, c, 0, me), [copy(t, 9, of_y, c, 0, sibling)]),
                (copy(t, 4, of_diag, c, 0, me), [copy(t, 10, of_diag, c, 0, sibling)]),
                (copy(t, 5, of_diag, c, 1, me), [copy(t, 11, of_diag, c, 1, sibling)])]
            from_sibling = [copy(t, 6, of_x, 1 - c, 0, me), copy(t, 8, of_y, 1 - c, 1, me), copy(t, 7, of_x, 1 - c, 1, me),
                            copy(t, 9, of_y, 1 - c, 0, me), copy(t, 10, of_diag, 1 - c, 0, me), copy(t, 11, of_diag, 1 - c, 1, me)]
            stages.append((direct, landing, from_sibling))
        return stages

    def start(*refs):
        for direct, _, _ in copies(*refs):
            for cp in direct:
                cp.start()

    def pass_on(landing):
        for arrived, onward in landing:
            arrived.wait_recv()
            for cp in onward:
                cp.start()

    def middle(*refs):
        for _, landing, _ in copies(*refs):
            pass_on(landing[:4])

    def finish(*refs):
        stages = copies(*refs)
        for _, landing, _ in stages:
            pass_on(landing[4:])
        for direct, landing, from_sibling in stages:
            for cp in from_sibling:
                cp.wait_recv()
            for cp in direct + [cp for _, onward in landing for cp in onward]:
                cp.wait_send()

    return Exchange(shards, [jax.ShapeDtypeStruct((4,) + s.shape, s.dtype) for s in shards], (n_t, n_sem), start, finish, middle)


def exchange_halves_d2d(grads, axes):
    n_t = len(grads)

    def copies(ins, outs, send_sems, recv_sems):
        x, y, c = _place()
        return [pltpu.make_async_remote_copy(
            src_ref=ins[t].at[(slice(None),) + _half(ins[t].shape[1:], axes[t], 1 - c)], dst_ref=outs[t],
            send_sem=send_sems.at[t], recv_sem=recv_sems.at[t], device_id=(x, y, 1 - c), device_id_type=MESH) for t in range(n_t)]

    shapes = [jax.ShapeDtypeStruct((4,) + _half_shape(g.shape[1:], a), g.dtype) for g, a in zip(grads, axes)]
    return Exchange(grads, shapes, (n_t,), *_start_all_wait_all(copies))


def exchange_quarters_ici(parts):
    n_t = len(parts)

    def copies(ins, outs, send_sems, recv_sems):
        x, y, c = _place()
        return [pltpu.make_async_remote_copy(
            src_ref=ins[t].at[2 * px + py], dst_ref=outs[t].at[q],
            send_sem=send_sems.at[t, q], recv_sem=recv_sems.at[t, q], device_id=(px, py, c), device_id_type=MESH)
            for t in range(n_t) for q, (px, py) in enumerate(_other_chips(x, y))]

    shapes = [jax.ShapeDtypeStruct((3,) + p.shape[1:], p.dtype) for p in parts]
    return Exchange(parts, shapes, (n_t, 3), *_start_all_wait_all(copies))


def swap_d2d(halves):
    n_t = len(halves)

    def copies(ins, outs, send_sems, recv_sems):
        x, y, c = _place()
        return [pltpu.make_async_remote_copy(
            src_ref=ins[t], dst_ref=outs[t], send_sem=send_sems.at[t], recv_sem=recv_sems.at[t],
            device_id=(x, y, 1 - c), device_id_type=MESH) for t in range(n_t)]

    return Exchange(halves, [jax.ShapeDtypeStruct(h.shape, h.dtype) for h in halves], (n_t,), *_start_all_wait_all(copies))


BLOCK_BYTES = 1 << 20


def _row_block(r, c):
    fits = [rb for rb in range(16, r + 1, 16) if r % rb == 0 and rb * c * 4 <= BLOCK_BYTES]
    return max(fits) if fits else r


def _place_scalars():
    x, y, c = _place()
    return jnp.stack([c, 2 * x + y]).astype(jnp.int32)


def reduce_on_chip(tag, grads, axes, from_sibling=None):
    if from_sibling is None:
        from_sibling = run_exchange(f"rs_d2d_{tag}", exchange_halves_d2d(grads, axes))
    parts, parts_bf16 = [], []
    for t, (g, s, axis) in enumerate(zip(grads, from_sibling, axes)):
        _, h, cols = s.shape
        rb = _row_block(h, cols)
        nb = h // rb
        blk = lambda k, i, s_ref: (k, i, 0)
        mine = (lambda k, i, s_ref, nb=nb: (k, s_ref[0] * nb + i, 0)) if axis == 0 else (lambda k, i, s_ref: (k, i, s_ref[0]))
        p32, p16 = blockmap(
            f"rs_add_{tag}{t}", lambda a, b: (a + b, a + b), (4, nb),
            [(g, (None, rb, cols), mine), (s, (None, rb, cols), blk)],
            [(s.shape, F32, (None, rb, cols), blk), (s.shape, BF16, (None, rb, cols), blk)], scalars=_place_scalars())
        parts.append(p32)
        parts_bf16.append(p16)
    return parts, parts_bf16


def reduce_across_chips(parts, from_chips):
    halves = []
    for t, (p, q) in enumerate(zip(parts, from_chips)):
        _, h, cols = p.shape
        rb = _row_block(h, cols)
        halves.append(blockmap(
            f"rs_sum{t}", lambda a, b: a + b[0].astype(F32) + b[1].astype(F32) + b[2].astype(F32), (h // rb,),
            [(p, (None, rb, cols), lambda i, s_ref: (s_ref[1], i, 0)), (q, (3, rb, cols), lambda i, s_ref: (0, i, 0))],
            [((h, cols), F32, (rb, cols), lambda i, s_ref: (i, 0))], scalars=_place_scalars())[0])
    return list(zip(halves, run_exchange("rs_swap", swap_d2d(halves))))


def _adamw(w, g, m, v):
    m = ADAM_B1 * m + (1.0 - ADAM_B1) * g
    v = ADAM_B2 * v + (1.0 - ADAM_B2) * jnp.square(g)
    m_hat = m / (1.0 - ADAM_B1 ** ADAM_STEP)
    v_hat = v / (1.0 - ADAM_B2 ** ADAM_STEP)
    delta = -ADAM_LR * (m_hat / (jnp.sqrt(v_hat) + ADAM_EPS) + ADAM_WD * w)
    return delta, m, v


def adamw(name, w, g, m, v):
    _, r, c = w.shape
    rb = _row_block(r, c)
    blk3 = lambda a: (a, (None, rb, c), lambda i: (0, i, 0))
    return blockmap(name, _adamw, (r // rb,), [blk3(w), (g, (rb, c), lambda i: (i, 0)), blk3(m), blk3(v)],
                    [(w.shape, F32, (None, rb, c), lambda i: (0, i, 0))] * 3)


def adamw_halves(name, w, mine, other, m, v, axis):
    _, r, c = w.shape
    h, c = mine.shape
    rb = _row_block(h, c)
    nb = h // rb

    def body(s_ref, w_ref, mine_ref, other_ref, m_ref, v_ref, g_out, d_out, m_out, v_out):
        g = jnp.where(pl.program_id(0) == s_ref[0], mine_ref[...], other_ref[...])
        d, nm, nv = _adamw(w_ref[...], g, m_ref[...], v_ref[...])
        g_out[...], d_out[...], m_out[...], v_out[...] = g, d, nm, nv

    spec3 = pl.BlockSpec((None, rb, c), (lambda k, i, s_ref: (0, k * nb + i, 0)) if axis == 0 else (lambda k, i, s_ref: (0, i, k)))
    spec2 = pl.BlockSpec((rb, c), lambda k, i, s_ref: (i, 0))
    grid_spec = pltpu.PrefetchScalarGridSpec(num_scalar_prefetch=1, grid=(2, nb), in_specs=[spec3, spec2, spec2, spec3, spec3],
                                             out_specs=[spec3] * 4)
    return pl.pallas_call(body, name=name, grid_spec=grid_spec, out_shape=[jax.ShapeDtypeStruct(w.shape, F32)] * 4,
                          compiler_params=_params(("parallel", "parallel")))(_place_scalars(), w, mine, other, m, v)


def _whole(name, fn, ins, outs):
    return blockmap(name, fn, (1,), [(a, a.shape, lambda i, nd=a.ndim: (0,) * nd) for a in ins],
                    [(s, d, s, lambda i, nd=len(s): (0,) * nd) for s, d in outs])


def _premix(x, w, sc, sh):
    return _rms(x, w) * (1.0 + sc) + sh


def _postmix(x, u, w_post, g1, w_pre2, sc2, sh2):
    x1 = x + g1 * _rms(u, w_post)
    return x1, _premix(x1, w_pre2, sc2, sh2)


def _merge(gs, gg, ys, yg):
    return _sigmoid(gs) * ys + _sigmoid(gg) * yg


def _final(x1, y2, w_post2, g2):
    return x1 + g2 * _rms(y2, w_post2)


def kernel(x, c, w_ada, b_ada, norm_mix_pre, norm_mix_post, w_in, ssm_conv_w, ssm_conv_b, ssm_dt_bias, ssm_A_log, ssm_D, ssm_norm_w, gdn_conv_w, gdn_dt_bias, gdn_A_log, gdn_norm_w, w_ssm_up, w_gdn_up, w_out, norm_mlp_pre, norm_mlp_post, w_mlp_up, w_mlp_down, loss_target, m_w_ada, m_b_ada, m_norm_mix_pre, m_norm_mix_post, m_w_in, m_ssm_conv_w, m_ssm_conv_b, m_ssm_dt_bias, m_ssm_A_log, m_ssm_D, m_ssm_norm_w, m_gdn_conv_w, m_gdn_dt_bias, m_gdn_A_log, m_gdn_norm_w, m_w_ssm_up, m_w_gdn_up, m_w_out, m_norm_mlp_pre, m_norm_mlp_post, m_w_mlp_up, m_w_mlp_down, v_w_ada, v_b_ada, v_norm_mix_pre, v_norm_mix_post, v_w_in, v_ssm_conv_w, v_ssm_conv_b, v_ssm_dt_bias, v_ssm_A_log, v_ssm_D, v_ssm_norm_w, v_gdn_conv_w, v_gdn_dt_bias, v_gdn_A_log, v_gdn_norm_w, v_w_ssm_up, v_w_gdn_up, v_w_out, v_norm_mlp_pre, v_norm_mlp_post, v_w_mlp_up, v_w_mlp_down):
    args = dict(locals())
    xi, yi, ci = _place()
    quarter = 2 * xi + yi
    batch = 4 * xi + 2 * yi + ci

    xt, target = x[0], loss_target[0]
    t, d = xt.shape
    hs, hv = ssm_dt_bias.shape[-1], gdn_dt_bias.shape[-1]
    assert hs <= LANE_B - LANE_DT and hv <= LANE_A - LANE_B and hv % 2 == 0 and hs % SSM_HEADS_PER_GROUP == 0
    assert t % SSM_CHUNK == 0 and t % (GDN_CHUNK * GDN_CHUNKS_PER_STEP) == 0 and d % LANES == 0
    d_inner = hs * SSM_HEAD_DIM
    n_grp = hs // SSM_HEADS_PER_GROUP
    gn = n_grp * SSM_D_STATE
    conv_ssm = d_inner + 2 * gn
    hq = hv // 2
    key, val = hq * GDN_HEAD, hv * GDN_HEAD
    conv_gdn = 2 * key + val
    o_dt = d_inner + conv_ssm
    o_qkv = o_dt + hs
    o_b = o_qkv + conv_gdn + val
    o_a = o_b + hv
    o_gs = o_a + hv
    n_proj = o_gs + 2 * d
    assert 4 * w_in.shape[-1] == n_proj and ssm_conv_w.shape[-1] * 4 == conv_ssm == conv_gdn
    a_z, a_q = 0, o_dt
    a_gs = a_q + conv_gdn + val
    a_gg = a_gs + d
    a_small = a_gg + d
    n_al = -(-(a_small + LANES) // MM_TILE_N) * MM_TILE_N

    def to_aligned(w):
        z = lambda n: jnp.zeros((n, w.shape[1]), w.dtype)
        return jnp.concatenate([
            w[:o_dt], w[o_qkv:o_b], w[o_gs:],
            w[o_dt:o_qkv], z(LANE_B - hs), w[o_b:o_a], z(LANE_A - LANE_B - hv), w[o_a:o_gs], z(LANES - LANE_A - hv),
            z(n_al - a_small - LANES)], axis=0)

    def from_aligned(w):
        s = a_small
        return jnp.concatenate([
            w[:o_dt], w[s + LANE_DT:s + LANE_DT + hs], w[a_q:a_gs], w[s + LANE_B:s + LANE_B + hv],
            w[s + LANE_A:s + LANE_A + hv], w[a_gs:a_small]], axis=0)

    def lanes(vec, at):
        return jnp.zeros((1, LANES), F32).at[:, at:at + vec.shape[-1]].set(vec.reshape(1, -1))

    n_cw = CONV_K * ssm_conv_w.shape[-1]
    small_in = gather_flat("ag_small", jnp.concatenate([c.reshape(-1), ssm_conv_w.reshape(-1), gdn_conv_w.reshape(-1)]))
    c_all = small_in[:, :d]
    by_chip = small_in[0::2]

    def whole_conv_w(lo):
        return jnp.transpose(by_chip[:, lo:lo + n_cw].reshape(4, CONV_K, -1), (1, 0, 2)).reshape(CONV_K, -1)

    cw_ssm, cw_gdn = whole_conv_w(d), whole_conv_w(d + n_cw)
    cb_ssm = ssm_conv_b

    n_ada = w_ada.shape[-1]
    b_q = lax.dynamic_slice_in_dim(b_ada, quarter * n_ada, n_ada, axis=1)
    mod_q = _whole("ada_fwd", lambda ca, w, b: _bdot(_silu(ca), w) + b, [c_all, w_ada[0], b_q], [((N_DEV, n_ada), F32)])[0]
    mod_all = gather_flat("ag_mod", mod_q.reshape(-1)).reshape(N_DEV, N_DEV, n_ada)[0::2]
    mod = lax.dynamic_index_in_dim(mod_all, batch, axis=1, keepdims=False).reshape(1, 4 * n_ada)
    sh1, sc1, g1, sh2, sc2, g2 = [mod[:, i * d:(i + 1) * d] for i in range(6)]

    transposed = lambda a: jnp.swapaxes(a, 1, 2)
    own = [w.astype(BF16) for w in (transposed(w_in)[0], w_ssm_up[0], w_gdn_up[0], w_out[0], w_mlp_up[0], w_mlp_down[0])]
    with_own = lambda gs, ws: [lax.dynamic_update_index_in_dim(g, w, quarter, 0) for g, w in zip(gs, ws)]
    rows_major = lambda g: g.reshape(-1, g.shape[2])
    wb_in = to_aligned(rows_major(with_own(run_exchange("ag_w_in", all_gather_shards(own[:1], [1])), own[:1])[0]))

    h1 = rowmap("premix", _premix, [xt], [norm_mix_pre, sc1, sh1], [(d, BF16)])[0]
    proj, gathered = matmul("in_proj", h1, wb_in, tb=True, comm=all_gather_shards(own[1:], [0] * 5))
    gathered = with_own(gathered, own[1:])
    wb_ssm_up, wb_gdn_up, wb_out = rows_major(gathered[0]), rows_major(gathered[1]), rows_major(gathered[2])
    wb_up, wb_down = gathered[3], rows_major(gathered[4])

    wide = 2 * LANES
    ssd_rows = [(proj, d_inner + conv_ssm, a_z, True), (proj, LANES, a_small, False)]
    ssd_consts = [(cw_ssm, False), (cb_ssm, False), (lanes(ssm_dt_bias, LANE_DT), False), (lanes(ssm_A_log, LANE_DT), False),
                  (lanes(ssm_D, LANE_DT), False), (ssm_norm_w.reshape(n_grp, 1, wide), False)]
    y_ssm_n, st_ssm, tails_ssm = scan_fwd("ssd_fwd", ssd_step, SSM_CHUNK, 1, ssd_rows, ssd_consts, d_inner, 1,
                                          state_shape=(n_grp,) + STATE_SHAPE, tail_cols=conv_ssm)
    gdn_rows = [(proj, conv_gdn + val, a_q, True), (proj, LANES, a_small, False)]
    gdn_consts = [(cw_gdn, False), (lanes(gdn_dt_bias, LANE_A), False), (lanes(gdn_A_log, LANE_A), False), (gdn_norm_w, False)]
    gdn_rows_per_step = GDN_CHUNK * GDN_CHUNKS_PER_STEP
    y_gdn_n, st_gdn, tails_gdn, inv_gdn = scan_fwd(
        "gdn_fwd", gdn_step, gdn_rows_per_step, 1, gdn_rows, gdn_consts, val, 1, keep=(hv, gdn_rows_per_step, gdn_rows_per_step),
        state_shape=(hq,) + STATE_SHAPE, tail_cols=conv_gdn)

    y_ssm = matmul("ssm_up", y_ssm_n, wb_ssm_up, out_dtypes=(BF16,))
    y_gdn = matmul("gdn_up", y_gdn_n, wb_gdn_up, out_dtypes=(BF16,))
    gates = [(proj, d, a_gs), (proj, d, a_gg)]
    merged = rowmap("merge", _merge, gates + [y_ssm, y_gdn], [], [(d, BF16)])[0]
    post_consts = [norm_mix_post, g1, norm_mlp_pre, sc2, sh2]
    u, x1, h2 = matmul("w_out_postmix", merged, wb_out, out_dtypes=(F32, F32, BF16), extras=[xt], consts=post_consts,
                       epi=lambda u_, x_, *cs: (u_,) + _postmix(x_, u_, *cs))
    relu2 = lambda acc: (acc, jnp.square(jnp.maximum(acc, 0.0)))
    a_up, act = matmul("mlp_up", h2, wb_up, out_dtypes=(BF16, BF16), epi=relu2, b_quarters=True)
    def final_bwd(y2_, x1_, tgt, w_, g_):
        x2, vjp = jax.vjp(_final, x1_, y2_, w_, g_)
        err = x2 - tgt
        loss = 0.5 * jnp.sum(jnp.mean(err * err, axis=-1, keepdims=True), axis=0, keepdims=True)
        dx1, dy2, dw, dg = vjp(err / d)
        return dx1, dy2, loss, dw, dg

    dx1, dy2, loss_part, d_norm_mlp_post, dg2 = matmul(
        "mlp_down_final", act, wb_down, out_dtypes=(F32, BF16), epi=final_bwd, extras=[x1, target], consts=[norm_mlp_post, g2],
        accs=[((1, 1), F32), ((1, d), F32), ((1, d), F32)])

    d_a = matmul("mlp_down_dx", dy2, wb_down, tb=True, out_dtypes=(BF16,), extras=[a_up],
                 epi=lambda acc, a: acc * 2.0 * jnp.maximum(a.astype(F32), 0.0))
    gw_down = matmul("mlp_down_dw", act, dy2, ta=True)
    dh2 = matmul("mlp_up_dx", d_a, wb_up, tb=True, b_quarters=True)
    gw_up = matmul("mlp_up_dw", h2, d_a, ta=True, out_quarters=True)

    def postmix_bwd(x_, u_, dx1_, dh2_, *cs):
        _, vjp = jax.vjp(_postmix, x_, u_, *cs)
        return vjp((dx1_, dh2_))

    dxa, du, d_norm_mix_post, dg1, d_norm_mlp_pre, dsc2, dsh2 = rowmap(
        "postmix_bwd", postmix_bwd, [xt, u, dx1, dh2], post_consts, [(d, F32), (d, BF16)], [((1, d), F32)] * 5)
    d_merged = matmul("w_out_dx", du, wb_out, tb=True, out_dtypes=(BF16,))
    gw_out = matmul("w_out_dw", merged, du, ta=True)

    def merge_bwd(gs, gg, ys, yg, dm):
        _, vjp = jax.vjp(_merge, gs, gg, ys, yg)
        dgs, dgg, dys, dyg = vjp(dm.astype(F32))
        return dys, dyg, jnp.concatenate([dgs, dgg], axis=1)

    dy_ssm, dy_gdn, dproj = rowmap("merge_bwd", merge_bwd, gates + [y_ssm, y_gdn, d_merged], [],
                                   [(d, BF16), (d, BF16), (2 * d, BF16, jax.ShapeDtypeStruct((t, n_al), BF16), a_gs)])
    dy_ssm_n = matmul("ssm_up_dx", dy_ssm, wb_ssm_up, tb=True, out_dtypes=(BF16,))
    quarters_rows = lambda g: g.reshape(4, g.shape[0] // 4, g.shape[1])
    mlp_grads = [gw_up, quarters_rows(gw_down)]
    gw_ssm_up, mlp_from_sibling = matmul("ssm_up_dw", y_ssm_n, dy_ssm, ta=True, comm=exchange_halves_d2d(mlp_grads, [0, 0]))
    dy_gdn_n = matmul("gdn_up_dx", dy_gdn, wb_gdn_up, tb=True, out_dtypes=(BF16,))
    gw_gdn_up = matmul("gdn_up_dw", y_gdn_n, dy_gdn, ta=True)

    dproj, dsmall_ssm, dcw_ssm, dcb_ssm, d_sdtb, d_salog, d_sdsk, d_snw = scan_bwd(
        "ssd_bwd", ssd_step, SSM_CHUNK, 1, ssd_rows, ssd_consts, [st_ssm, tails_ssm], dy_ssm_n, [BF16, F32], 1, {0: dproj})
    dproj, dsmall_gdn, dcw_gdn, d_gdtb, d_galog, d_gnw = scan_bwd(
        "gdn_bwd", gdn_step, gdn_rows_per_step, 1, gdn_rows, gdn_consts, [st_gdn, tails_gdn], dy_gdn_n, [BF16, F32], 1,
        {0: dproj}, kept=inv_gdn)
    tail = n_al - a_small
    dproj = rowmap("small_sum", lambda a, b: jnp.concatenate([a + b, jnp.zeros((a.shape[0], tail - LANES), F32)], axis=1),
                   [dsmall_ssm, dsmall_gdn], [], [(tail, BF16, dproj, a_small)])[0]
    mix32, mix16 = reduce_on_chip("mix", [quarters_rows(gw_ssm_up), quarters_rows(gw_gdn_up), quarters_rows(gw_out)], [0] * 3)
    mlp32, mlp16 = reduce_on_chip("mlp", mlp_grads, [0, 0], mlp_from_sibling)
    rest32, rest16 = mix32 + mlp32, mix16 + mlp16
    gw_in_al, rest_chips = matmul("in_proj_dw", dproj, h1, ta=True, comm=exchange_quarters_ici(rest16))
    in32, in16 = reduce_on_chip("in", [quarters_rows(from_aligned(gw_in_al))], [1])
    def premix_bwd(dh1_, x_, dxa_, w_, sc_, sh_):
        _, vjp = jax.vjp(_premix, x_, w_, sc_, sh_)
        dx, dw, dsc, dsh = vjp(dh1_)
        return dx + dxa_, dw, dsc, dsh

    (grad_x, d_norm_mix_pre, dsc1, dsh1), in_chips = matmul(
        "in_proj_dx", dproj, wb_in, comm=exchange_quarters_ici(in16), epi=premix_bwd, extras=[xt, dxa],
        consts=[norm_mix_pre, sc1, sh1], accs=[((1, d), F32)] * 3)

    dmod_all = gather_flat("ag_dmod", jnp.concatenate([dsh1, dsc1, dg1, dsh2, dsc2, dg2], axis=1).reshape(-1))
    dmod_q = lax.dynamic_slice_in_dim(dmod_all, quarter * n_ada, n_ada, axis=1)
    gw_ada, gb_ada = _whole(
        "ada_bwd", lambda ca, dq_, da_: (_bdot(_silu(ca), dq_, TN), jnp.sum(da_, axis=0, keepdims=True)),
        [c_all, dmod_q, dmod_all], [((d, n_ada), F32), ((1, 4 * n_ada), F32)])

    partial = [d_norm_mix_pre, d_norm_mix_post, dcw_ssm, dcb_ssm, d_sdtb[:, LANE_DT:LANE_DT + hs], d_salog[:, LANE_DT:LANE_DT + hs],
               d_sdsk[:, LANE_DT:LANE_DT + hs], d_snw, dcw_gdn, d_gdtb[:, LANE_A:LANE_A + hv], d_galog[:, LANE_A:LANE_A + hv], d_gnw,
               d_norm_mlp_pre, d_norm_mlp_post, loss_part]
    sizes = [p.size for p in partial]
    stacked = gather_flat("ag_grads", jnp.concatenate([p.reshape(-1) for p in partial]))
    summed = _whole("small_sum8", lambda s: jnp.sum(s, axis=0, keepdims=True), [stacked], [((1, stacked.shape[1]), F32)])[0][0]
    offs = [0]
    for s in sizes:
        offs.append(offs[-1] + s)
    red = [summed[offs[i]:offs[i + 1]] for i in range(len(sizes))]
    loss = red[-1][0]
    my_cols = lambda full: lax.dynamic_slice_in_dim(full.reshape(CONV_K, -1), quarter * (n_cw // CONV_K), n_cw // CONV_K, axis=1)
    small_grads = {
        "b_ada": gb_ada, "norm_mix_pre": red[0], "norm_mix_post": red[1], "ssm_conv_w": my_cols(red[2]), "ssm_conv_b": red[3],
        "ssm_dt_bias": red[4], "ssm_A_log": red[5], "ssm_D": red[6], "ssm_norm_w": red[7], "gdn_conv_w": my_cols(red[8]),
        "gdn_dt_bias": red[9], "gdn_A_log": red[10], "gdn_norm_w": red[11], "norm_mlp_pre": red[12], "norm_mlp_post": red[13]}

    big_names = ["w_in", "w_ssm_up", "w_gdn_up", "w_out", "w_mlp_up", "w_mlp_down"]
    big_grads = dict(zip(big_names, reduce_across_chips(in32 + rest32, in_chips + rest_chips)))

    names = ['w_ada', 'b_ada', 'norm_mix_pre', 'norm_mix_post', 'w_in', 'ssm_conv_w', 'ssm_conv_b', 'ssm_dt_bias', 'ssm_A_log', 'ssm_D',
             'ssm_norm_w', 'gdn_conv_w', 'gdn_dt_bias', 'gdn_A_log', 'gdn_norm_w', 'w_ssm_up', 'w_gdn_up', 'w_out', 'norm_mlp_pre',
             'norm_mlp_post', 'w_mlp_up', 'w_mlp_down']
    grad, delta, new_m, new_v = {}, {}, {}, {}
    for n, (mine, other) in big_grads.items():
        view, axis = (transposed, 1) if n == "w_in" else ((lambda a: a), 0)
        res = adamw_halves("adamw_" + n, view(args[n]), mine, other, view(args["m_" + n]), view(args["v_" + n]), axis)
        grad[n], delta[n], new_m[n], new_v[n] = [view(a) for a in res]
    grad["w_ada"] = gw_ada.reshape(w_ada.shape)
    delta["w_ada"], new_m["w_ada"], new_v["w_ada"] = adamw("adamw_w_ada", w_ada, gw_ada, m_w_ada, v_w_ada)
    small_names = [n for n in names if n not in grad]
    flat = lambda pre: jnp.concatenate([args[pre + n].reshape(-1) for n in small_names]).reshape(1, 1, -1)
    g_flat = jnp.concatenate([small_grads[n].reshape(-1) for n in small_names]).reshape(1, -1)
    dl, nm, nv = adamw("adamw_small", flat(""), g_flat, flat("m_"), flat("v_"))
    off = 0
    for n in small_names:
        shape = args[n].shape
        size = args[n].size
        grad[n], delta[n], new_m[n], new_v[n] = [a.reshape(-1)[off:off + size].reshape(shape) for a in (g_flat, dl, nm, nv)]
        off += size

    return (loss, grad_x.reshape(x.shape), *[grad[n] for n in names], *[delta[n] for n in names],
            *[new_m[n] for n in names], *[new_v[n] for n in names])
```
